```python
import math
import jax, jax.numpy as jnp
from jax import lax
import numpy as np

D_MODEL = 2048
BATCH = 16
SEQ = 2048
DEPTH = 4

N_MIXERS = 2
N_HGRN_LAYERS = (DEPTH + 1) // 2
N_MAMBA_LAYERS = DEPTH // 2
NORM_EPS = 1e-5

HGRN_EXPAND = 128
HGRN_HEADS = D_MODEL // HGRN_EXPAND
HGRN_DK = HGRN_EXPAND
HGRN_DV = D_MODEL // HGRN_HEADS
HGRN_FDIM = HGRN_HEADS * HGRN_DK
HGRN_IN_DIM = 2 * HGRN_FDIM + 2 * HGRN_HEADS * HGRN_DV
HGRN_CHUNK = 64

M_EXPAND = 2
M_D_INNER = M_EXPAND * D_MODEL
M_HEADDIM = 64
M_HEADS = M_D_INNER // M_HEADDIM
M_GROUPS = 8
M_HPG = M_HEADS // M_GROUPS
M_D_STATE = 128
M_CONV = 4
M_CONV_DIM = M_D_INNER + 2 * M_GROUPS * M_D_STATE
M_IN_DIM = M_D_INNER + M_CONV_DIM + M_HEADS
M_CHUNK = 128

D_FF = 5632
FFN_CONV = 3

kernel_name = 'hybrid_hgrn2_mamba2_convffn'


def rms_norm(x, w):
    xf = x.astype(jnp.float32)
    y = xf * lax.rsqrt(jnp.mean(xf * xf, axis=-1, keepdims=True) + NORM_EPS)
    return (y * w.astype(jnp.float32)).astype(x.dtype)


def causal_dwconv(x, w, b):
    K, C = w.shape
    y = lax.conv_general_dilated(x, w[:, None, :].astype(x.dtype), window_strides=(1,),
                                 padding=[(K - 1, 0)], dimension_numbers=('NWC', 'WIO', 'NWC'),
                                 feature_group_count=C)
    return y + b.astype(x.dtype)


def masked_exp(mask, logits):
    return jnp.where(mask, jnp.exp(jnp.where(mask, logits, 0.0)), 0.0)


def hgrn2_mixer(u, w_in, lb, gn_w, w_out):
    Bsz, L, _ = u.shape
    n_chunks = L // HGRN_CHUNK
    f32 = jnp.float32
    q, f, v, g = jnp.split(u @ w_in, [HGRN_FDIM, 2 * HGRN_FDIM, 2 * HGRN_FDIM + HGRN_HEADS * HGRN_DV], axis=-1)
    q = jax.nn.silu(q.astype(f32))
    f = f.astype(f32)
    lb = lb.astype(f32)
    log_f = jnp.log(lb + (1.0 - lb) * jax.nn.sigmoid(f))
    k = (1.0 - lb) * jax.nn.sigmoid(-f)
    v = v.astype(f32)

    def to_chunks(t, d):
        return t.reshape(Bsz, n_chunks, HGRN_CHUNK, HGRN_HEADS, d).transpose(1, 0, 3, 2, 4)

    causal = jnp.tril(jnp.ones((HGRN_CHUNK, HGRN_CHUNK), bool))[..., None]

    def chunk_step(S, inp):
        qc, kc, vc, gc = inp
        b = jnp.cumsum(gc, axis=2)
        diff = b[:, :, :, None, :] - b[:, :, None, :, :]
        decay = masked_exp(causal, diff)
        A = jnp.einsum('bhik,bhjk,bhijk->bhij', qc, kc, decay)
        o = jnp.einsum('bhij,bhjv->bhiv', A, vc) + jnp.einsum('bhik,bhkv->bhiv', qc * jnp.exp(b), S)
        b_last = b[:, :, -1:, :]
        S = jnp.exp(b_last[:, :, 0, :, None]) * S + jnp.einsum('bhjk,bhjv->bhkv', kc * jnp.exp(b_last - b), vc)
        return S, o

    S0 = jnp.zeros((Bsz, HGRN_HEADS, HGRN_DK, HGRN_DV), f32)
    _, o = lax.scan(chunk_step, S0, (to_chunks(q, HGRN_DK), to_chunks(k, HGRN_DK),
                                     to_chunks(v, HGRN_DV), to_chunks(log_f, HGRN_DK)))
    o = o.transpose(1, 0, 3, 2, 4).reshape(Bsz, L, HGRN_HEADS, HGRN_DV)
    g = g.astype(f32).reshape(Bsz, L, HGRN_HEADS, HGRN_DV)
    o = o * lax.rsqrt(jnp.mean(o * o, axis=-1, keepdims=True) + NORM_EPS) * gn_w.astype(f32) * jax.nn.silu(g)
    return o.reshape(Bsz, L, HGRN_HEADS * HGRN_DV).astype(u.dtype) @ w_out


def mamba2_mixer(u, w_in, conv_w, conv_b, dt_bias, A_log, D_skip, norm_w, w_out):
    Bsz, L, _ = u.shape
    nc = L // M_CHUNK
    f32 = jnp.float32
    z, xBC, dt = jnp.split(u @ w_in, [M_D_INNER, M_D_INNER + M_CONV_DIM], axis=-1)
    xBC = jax.nn.silu(causal_dwconv(xBC, conv_w, conv_b)).astype(f32)
    xs, Bm, Cm = jnp.split(xBC, [M_D_INNER, M_D_INNER + M_GROUPS * M_D_STATE], axis=-1)
    dt = jax.nn.softplus(dt.astype(f32) + dt_bias.astype(f32))
    A = -jnp.exp(A_log.astype(f32))
    xh = xs.reshape(Bsz, L, M_HEADS, M_HEADDIM)
    X = (xh * dt[..., None]).reshape(Bsz, nc, M_CHUNK, M_GROUPS, M_HPG, M_HEADDIM)
    Ad = (dt * A).reshape(Bsz, nc, M_CHUNK, M_GROUPS, M_HPG).transpose(0, 3, 4, 1, 2)
    Bc = Bm.reshape(Bsz, nc, M_CHUNK, M_GROUPS, M_D_STATE)
    Cc = Cm.reshape(Bsz, nc, M_CHUNK, M_GROUPS, M_D_STATE)
    a_cs = jnp.cumsum(Ad, axis=-1)
    causal = jnp.tril(jnp.ones((M_CHUNK, M_CHUNK), bool))
    Lmat = masked_exp(causal, a_cs[..., :, None] - a_cs[..., None, :])
    CB = jnp.einsum('bclgn,bcsgn->bcgls', Cc, Bc)
    y_diag = jnp.einsum('bcgls,bgjcls,bcsgjp->bclgjp', CB, Lmat, X)
    decay_states = jnp.exp(a_cs[..., -1:] - a_cs)
    states = jnp.einsum('bcsgn,bgjcs,bcsgjp->cbgjpn', Bc, decay_states, X)
    chunk_decay = jnp.exp(a_cs[..., -1]).transpose(3, 0, 1, 2)

    def state_step(h, inp):
        st, dec = inp
        return dec[..., None, None] * h + st, h

    h0 = jnp.zeros((Bsz, M_GROUPS, M_HPG, M_HEADDIM, M_D_STATE), f32)
    _, h_in = lax.scan(state_step, h0, (states, chunk_decay))
    y_off = jnp.einsum('bclgn,cbgjpn,bgjcl->bclgjp', Cc, h_in, jnp.exp(a_cs))
    y = (y_diag + y_off).reshape(Bsz, L, M_HEADS, M_HEADDIM) + xh * D_skip.astype(f32)[:, None]
    y = y.reshape(Bsz, L, M_D_INNER) * jax.nn.silu(z.astype(f32))
    y = y.reshape(Bsz, L, M_GROUPS, M_D_INNER // M_GROUPS)
    y = y * lax.rsqrt(jnp.mean(y * y, axis=-1, keepdims=True) + NORM_EPS)
    y = y.reshape(Bsz, L, M_D_INNER) * norm_w.astype(f32)
    return y.astype(u.dtype) @ w_out


def conv_ffn(u, w_up, conv_w, conv_b, w_down):
    h = causal_dwconv(u @ w_up, conv_w, conv_b)
    g, up = jnp.split(h, 2, axis=-1)
    return (jax.nn.silu(g) * up) @ w_down


def _fwd_setup_inputs(seed: int = 0) -> dict:
    key = jax.random.key(seed)
    ks = jax.random.split(key, 24)
    f32 = jnp.float32
    nh, nm = N_HGRN_LAYERS, N_MAMBA_LAYERS

    def dense(k, shape):
        return jax.random.normal(k, shape, f32) * shape[-2] ** -0.5

    def gain(k, shape):
        return 1.0 + 0.02 * jax.random.normal(k, shape, f32)

    def small(k, shape):
        return 0.02 * jax.random.normal(k, shape, f32)

    dt0 = jnp.exp(jax.random.uniform(ks[10], (nm, M_HEADS), f32, math.log(1e-3), math.log(1e-1)))
    return {
        'x': jax.random.normal(ks[0], (BATCH, SEQ, D_MODEL), f32),
        'mix_norm': gain(ks[1], (DEPTH, D_MODEL)),
        'ffn_norm': gain(ks[2], (DEPTH, D_MODEL)),
        'final_norm': gain(ks[3], (D_MODEL,)),
        'hgrn_w_in': dense(ks[4], (nh, D_MODEL, HGRN_IN_DIM)),
        'hgrn_lb_logits': 0.5 * jax.random.normal(ks[5], (nh, HGRN_FDIM), f32),
        'hgrn_gnorm': gain(ks[6], (nh, HGRN_DV)),
        'hgrn_w_out': dense(ks[7], (nh, HGRN_HEADS * HGRN_DV, D_MODEL)),
        'm_w_in': dense(ks[8], (nm, D_MODEL, M_IN_DIM)),
        'm_conv_w': dense(ks[9], (nm, M_CONV, M_CONV_DIM)),
        'm_conv_b': small(ks[11], (nm, M_CONV_DIM)),
        'm_dt_bias': dt0 + jnp.log(-jnp.expm1(-dt0)),
        'm_A_log': jnp.log(jax.random.uniform(ks[12], (nm, M_HEADS), f32, 1.0, 16.0)),
        'm_D': gain(ks[13], (nm, M_HEADS)),
        'm_norm': gain(ks[14], (nm, M_D_INNER)),
        'm_w_out': dense(ks[15], (nm, M_D_INNER, D_MODEL)),
        'f_w_up': dense(ks[16], (DEPTH, D_MODEL, 2 * D_FF)),
        'f_conv_w': dense(ks[17], (DEPTH, FFN_CONV, 2 * D_FF)),
        'f_conv_b': small(ks[18], (DEPTH, 2 * D_FF)),
        'f_w_down': dense(ks[19], (DEPTH, D_FF, D_MODEL)),
    }


def _fwd_reference(x, mix_norm, ffn_norm, final_norm, hgrn_w_in, hgrn_lb_logits, hgrn_gnorm, hgrn_w_out,
              m_w_in, m_conv_w, m_conv_b, m_dt_bias, m_A_log, m_D, m_norm, m_w_out,
              f_w_up, f_conv_w, f_conv_b, f_w_down):
    lb_p = jax.nn.softmax(hgrn_lb_logits.astype(jnp.float32), axis=0)
    lower_bounds = jnp.cumsum(lb_p, axis=0) - lb_p[0]
    h = x
    for i in range(DEPTH):
        u = rms_norm(h, mix_norm[i])
        j = i // N_MIXERS
        if i % N_MIXERS == 0:
            h = h + hgrn2_mixer(u, hgrn_w_in[j], lower_bounds[j], hgrn_gnorm[j], hgrn_w_out[j])
        else:
            h = h + mamba2_mixer(u, m_w_in[j], m_conv_w[j], m_conv_b[j], m_dt_bias[j], m_A_log[j],
                                 m_D[j], m_norm[j], m_w_out[j])
        h = h + conv_ffn(rms_norm(h, ffn_norm[i]), f_w_up[i], f_conv_w[i], f_conv_b[i], f_w_down[i])
    return rms_norm(h, final_norm)


import jax as _jax
import jax.numpy as _jnp

TWIN_FORMAT = 'train_step'
FWD_PARAMS = ['x', 'mix_norm', 'ffn_norm', 'final_norm', 'hgrn_w_in', 'hgrn_lb_logits', 'hgrn_gnorm', 'hgrn_w_out', 'm_w_in', 'm_conv_w', 'm_conv_b', 'm_dt_bias', 'm_A_log', 'm_D', 'm_norm', 'm_w_out', 'f_w_up', 'f_conv_w', 'f_conv_b', 'f_w_down']
TWIN_WEIGHTS = ['mix_norm', 'ffn_norm', 'final_norm', 'hgrn_w_in', 'hgrn_lb_logits', 'hgrn_gnorm', 'hgrn_w_out', 'm_w_in', 'm_conv_w', 'm_conv_b', 'm_dt_bias', 'm_A_log', 'm_D', 'm_norm', 'm_w_out', 'f_w_up', 'f_conv_w', 'f_conv_b', 'f_w_down']
TWIN_DIFF_INPUT = 'x'
TWIN_INPUTS = ['x', 'mix_norm', 'ffn_norm', 'final_norm', 'hgrn_w_in', 'hgrn_lb_logits', 'hgrn_gnorm', 'hgrn_w_out', 'm_w_in', 'm_conv_w', 'm_conv_b', 'm_dt_bias', 'm_A_log', 'm_D', 'm_norm', 'm_w_out', 'f_w_up', 'f_conv_w', 'f_conv_b', 'f_w_down', 'loss_target', 'm_mix_norm', 'm_ffn_norm', 'm_final_norm', 'm_hgrn_w_in', 'm_hgrn_lb_logits', 'm_hgrn_gnorm', 'm_hgrn_w_out', 'm_m_w_in', 'm_m_conv_w', 'm_m_conv_b', 'm_m_dt_bias', 'm_m_A_log', 'm_m_D', 'm_m_norm', 'm_m_w_out', 'm_f_w_up', 'm_f_conv_w', 'm_f_conv_b', 'm_f_w_down', 'v_mix_norm', 'v_ffn_norm', 'v_final_norm', 'v_hgrn_w_in', 'v_hgrn_lb_logits', 'v_hgrn_gnorm', 'v_hgrn_w_out', 'v_m_w_in', 'v_m_conv_w', 'v_m_conv_b', 'v_m_dt_bias', 'v_m_A_log', 'v_m_D', 'v_m_norm', 'v_m_w_out', 'v_f_w_up', 'v_f_conv_w', 'v_f_conv_b', 'v_f_w_down']
TWIN_OUTPUTS = ['loss', 'grad_x', 'grad_mix_norm', 'grad_ffn_norm', 'grad_final_norm', 'grad_hgrn_w_in', 'grad_hgrn_lb_logits', 'grad_hgrn_gnorm', 'grad_hgrn_w_out', 'grad_m_w_in', 'grad_m_conv_w', 'grad_m_conv_b', 'grad_m_dt_bias', 'grad_m_A_log', 'grad_m_D', 'grad_m_norm', 'grad_m_w_out', 'grad_f_w_up', 'grad_f_conv_w', 'grad_f_conv_b', 'grad_f_w_down', 'delta_mix_norm', 'delta_ffn_norm', 'delta_final_norm', 'delta_hgrn_w_in', 'delta_hgrn_lb_logits', 'delta_hgrn_gnorm', 'delta_hgrn_w_out', 'delta_m_w_in', 'delta_m_conv_w', 'delta_m_conv_b', 'delta_m_dt_bias', 'delta_m_A_log', 'delta_m_D', 'delta_m_norm', 'delta_m_w_out', 'delta_f_w_up', 'delta_f_conv_w', 'delta_f_conv_b', 'delta_f_w_down', 'new_m_mix_norm', 'new_m_ffn_norm', 'new_m_final_norm', 'new_m_hgrn_w_in', 'new_m_hgrn_lb_logits', 'new_m_hgrn_gnorm', 'new_m_hgrn_w_out', 'new_m_m_w_in', 'new_m_m_conv_w', 'new_m_m_conv_b', 'new_m_m_dt_bias', 'new_m_m_A_log', 'new_m_m_D', 'new_m_m_norm', 'new_m_m_w_out', 'new_m_f_w_up', 'new_m_f_conv_w', 'new_m_f_conv_b', 'new_m_f_w_down', 'new_v_mix_norm', 'new_v_ffn_norm', 'new_v_final_norm', 'new_v_hgrn_w_in', 'new_v_hgrn_lb_logits', 'new_v_hgrn_gnorm', 'new_v_hgrn_w_out', 'new_v_m_w_in', 'new_v_m_conv_w', 'new_v_m_conv_b', 'new_v_m_dt_bias', 'new_v_m_A_log', 'new_v_m_D', 'new_v_m_norm', 'new_v_m_w_out', 'new_v_f_w_up', 'new_v_f_conv_w', 'new_v_f_conv_b', 'new_v_f_w_down']
TWIN_LEAF_KINDS = {'loss': 'loss', 'grad_x': 'grad_x', 'grad_mix_norm': 'grad_w', 'grad_ffn_norm': 'grad_w', 'grad_final_norm': 'grad_w', 'grad_hgrn_w_in': 'grad_w', 'grad_hgrn_lb_logits': 'grad_w', 'grad_hgrn_gnorm': 'grad_w', 'grad_hgrn_w_out': 'grad_w', 'grad_m_w_in': 'grad_w', 'grad_m_conv_w': 'grad_w', 'grad_m_conv_b': 'grad_w', 'grad_m_dt_bias': 'grad_w', 'grad_m_A_log': 'grad_w', 'grad_m_D': 'grad_w', 'grad_m_norm': 'grad_w', 'grad_m_w_out': 'grad_w', 'grad_f_w_up': 'grad_w', 'grad_f_conv_w': 'grad_w', 'grad_f_conv_b': 'grad_w', 'grad_f_w_down': 'grad_w', 'delta_mix_norm': 'delta_w', 'delta_ffn_norm': 'delta_w', 'delta_final_norm': 'delta_w', 'delta_hgrn_w_in': 'delta_w', 'delta_hgrn_lb_logits': 'delta_w', 'delta_hgrn_gnorm': 'delta_w', 'delta_hgrn_w_out': 'delta_w', 'delta_m_w_in': 'delta_w', 'delta_m_conv_w': 'delta_w', 'delta_m_conv_b': 'delta_w', 'delta_m_dt_bias': 'delta_w', 'delta_m_A_log': 'delta_w', 'delta_m_D': 'delta_w', 'delta_m_norm': 'delta_w', 'delta_m_w_out': 'delta_w', 'delta_f_w_up': 'delta_w', 'delta_f_conv_w': 'delta_w', 'delta_f_conv_b': 'delta_w', 'delta_f_w_down': 'delta_w', 'new_m_mix_norm': 'new_m', 'new_m_ffn_norm': 'new_m', 'new_m_final_norm': 'new_m', 'new_m_hgrn_w_in': 'new_m', 'new_m_hgrn_lb_logits': 'new_m', 'new_m_hgrn_gnorm': 'new_m', 'new_m_hgrn_w_out': 'new_m', 'new_m_m_w_in': 'new_m', 'new_m_m_conv_w': 'new_m', 'new_m_m_conv_b': 'new_m', 'new_m_m_dt_bias': 'new_m', 'new_m_m_A_log': 'new_m', 'new_m_m_D': 'new_m', 'new_m_m_norm': 'new_m', 'new_m_m_w_out': 'new_m', 'new_m_f_w_up': 'new_m', 'new_m_f_conv_w': 'new_m', 'new_m_f_conv_b': 'new_m', 'new_m_f_w_down': 'new_m', 'new_v_mix_norm': 'new_v', 'new_v_ffn_norm': 'new_v', 'new_v_final_norm': 'new_v', 'new_v_hgrn_w_in': 'new_v', 'new_v_hgrn_lb_logits': 'new_v', 'new_v_hgrn_gnorm': 'new_v', 'new_v_hgrn_w_out': 'new_v', 'new_v_m_w_in': 'new_v', 'new_v_m_conv_w': 'new_v', 'new_v_m_conv_b': 'new_v', 'new_v_m_dt_bias': 'new_v', 'new_v_m_A_log': 'new_v', 'new_v_m_D': 'new_v', 'new_v_m_norm': 'new_v', 'new_v_m_w_out': 'new_v', 'new_v_f_w_up': 'new_v', 'new_v_f_conv_w': 'new_v', 'new_v_f_conv_b': 'new_v', 'new_v_f_w_down': 'new_v'}


def _forward(args):
    return _fwd_reference(*[args[k] for k in FWD_PARAMS])


def _output_shape():
    out = _jax.eval_shape(lambda: _forward(_fwd_setup_inputs(0)))
    return out.shape, out.dtype

N_MICROBATCH = 1
ADAM_LR = 0.001
ADAM_B1 = 0.9
ADAM_B2 = 0.999
ADAM_EPS = 1e-08
ADAM_WD = 0.01
ADAM_STEP = 10
PER_EXAMPLE_BATCH_AXIS = {'x': 0, 'loss_target': 0}
SHARED_INPUTS = []
_WEIGHT_DTYPES = {'mix_norm': _jnp.float32, 'ffn_norm': _jnp.float32, 'final_norm': _jnp.float32, 'hgrn_w_in': _jnp.float32, 'hgrn_lb_logits': _jnp.float32, 'hgrn_gnorm': _jnp.float32, 'hgrn_w_out': _jnp.float32, 'm_w_in': _jnp.float32, 'm_conv_w': _jnp.float32, 'm_conv_b': _jnp.float32, 'm_dt_bias': _jnp.float32, 'm_A_log': _jnp.float32, 'm_D': _jnp.float32, 'm_norm': _jnp.float32, 'm_w_out': _jnp.float32, 'f_w_up': _jnp.float32, 'f_conv_w': _jnp.float32, 'f_conv_b': _jnp.float32, 'f_w_down': _jnp.float32}
MOMENT_SCALE = {'mix_norm': 9.126716e-02, 'ffn_norm': 6.243953e-02, 'final_norm': 1.598948e+01, 'hgrn_w_in': 4.639479e-02, 'hgrn_lb_logits': 3.101802e-03, 'hgrn_gnorm': 2.603350e-01, 'hgrn_w_out': 6.389016e-02, 'm_w_in': 4.008756e-02, 'm_conv_w': 3.668125e-02, 'm_conv_b': 4.823689e-02, 'm_dt_bias': 8.640813e-02, 'm_A_log': 1.095985e-01, 'm_D': 3.345058e-01, 'm_norm': 4.220798e-02, 'm_w_out': 5.980287e-02, 'f_w_up': 2.685478e-02, 'f_conv_w': 2.683129e-02, 'f_conv_b': 2.716578e-02, 'f_w_down': 4.383866e-02}


def _to_microbatches(a, axis):
    t = _jnp.moveaxis(a, axis, 0)
    t = t.reshape((N_MICROBATCH, t.shape[0] // N_MICROBATCH) + t.shape[1:])
    return _jnp.moveaxis(t, 1, axis + 1)


def setup_inputs(seed: int = 0) -> dict:
    inp = _fwd_setup_inputs(seed)
    key = _jax.random.fold_in(_jax.random.key(seed), 7919)
    shape, _ = _output_shape()
    out = dict(inp)
    out["loss_target"] = _jax.random.normal(_jax.random.fold_in(key, 0), shape, _jnp.float32)
    for i, name in enumerate(TWIN_WEIGHTS):
        w = inp[name].astype(_jnp.float32)
        if MOMENT_SCALE is None:
            s = _jnp.sqrt(_jnp.mean(_jnp.square(w)) + 1e-30)
        else:
            s = MOMENT_SCALE[name]
        km, kv = _jax.random.split(_jax.random.fold_in(key, i + 1))
        out[name] = w
        out["m_" + name] = s * _jax.random.normal(km, w.shape, _jnp.float32)
        out["v_" + name] = (s * s) * _jax.random.uniform(kv, w.shape, _jnp.float32, 0.5, 1.5)
    if N_MICROBATCH > 1:
        for name, axis in PER_EXAMPLE_BATCH_AXIS.items():
            out[name] = _to_microbatches(out[name], axis)
    return {'x': out['x'], 'mix_norm': out['mix_norm'], 'ffn_norm': out['ffn_norm'], 'final_norm': out['final_norm'], 'hgrn_w_in': out['hgrn_w_in'], 'hgrn_lb_logits': out['hgrn_lb_logits'], 'hgrn_gnorm': out['hgrn_gnorm'], 'hgrn_w_out': out['hgrn_w_out'], 'm_w_in': out['m_w_in'], 'm_conv_w': out['m_conv_w'], 'm_conv_b': out['m_conv_b'], 'm_dt_bias': out['m_dt_bias'], 'm_A_log': out['m_A_log'], 'm_D': out['m_D'], 'm_norm': out['m_norm'], 'm_w_out': out['m_w_out'], 'f_w_up': out['f_w_up'], 'f_conv_w': out['f_conv_w'], 'f_conv_b': out['f_conv_b'], 'f_w_down': out['f_w_down'], 'loss_target': out['loss_target'], 'm_mix_norm': out['m_mix_norm'], 'm_ffn_norm': out['m_ffn_norm'], 'm_final_norm': out['m_final_norm'], 'm_hgrn_w_in': out['m_hgrn_w_in'], 'm_hgrn_lb_logits': out['m_hgrn_lb_logits'], 'm_hgrn_gnorm': out['m_hgrn_gnorm'], 'm_hgrn_w_out': out['m_hgrn_w_out'], 'm_m_w_in': out['m_m_w_in'], 'm_m_conv_w': out['m_m_conv_w'], 'm_m_conv_b': out['m_m_conv_b'], 'm_m_dt_bias': out['m_m_dt_bias'], 'm_m_A_log': out['m_m_A_log'], 'm_m_D': out['m_m_D'], 'm_m_norm': out['m_m_norm'], 'm_m_w_out': out['m_m_w_out'], 'm_f_w_up': out['m_f_w_up'], 'm_f_conv_w': out['m_f_conv_w'], 'm_f_conv_b': out['m_f_conv_b'], 'm_f_w_down': out['m_f_w_down'], 'v_mix_norm': out['v_mix_norm'], 'v_ffn_norm': out['v_ffn_norm'], 'v_final_norm': out['v_final_norm'], 'v_hgrn_w_in': out['v_hgrn_w_in'], 'v_hgrn_lb_logits': out['v_hgrn_lb_logits'], 'v_hgrn_gnorm': out['v_hgrn_gnorm'], 'v_hgrn_w_out': out['v_hgrn_w_out'], 'v_m_w_in': out['v_m_w_in'], 'v_m_conv_w': out['v_m_conv_w'], 'v_m_conv_b': out['v_m_conv_b'], 'v_m_dt_bias': out['v_m_dt_bias'], 'v_m_A_log': out['v_m_A_log'], 'v_m_D': out['v_m_D'], 'v_m_norm': out['v_m_norm'], 'v_m_w_out': out['v_m_w_out'], 'v_f_w_up': out['v_f_w_up'], 'v_f_conv_w': out['v_f_conv_w'], 'v_f_conv_b': out['v_f_conv_b'], 'v_f_w_down': out['v_f_w_down']}


def _loss(weights, diff, rest, loss_target):
    with _jax.named_scope("forward"):
        args = {**rest, TWIN_DIFF_INPUT: diff, **{k: w.astype(_WEIGHT_DTYPES[k]) for k, w in weights.items()}}
        y = _forward(args)
    with _jax.named_scope("loss_head"):
        err = _jnp.square(y.astype(_jnp.float32) - loss_target)
        return 0.5 * _jnp.sum(_jnp.mean(err, axis=-1)) if err.ndim else 0.5 * err


def _adamw(w, g, m, v):
    m = ADAM_B1 * m + (1.0 - ADAM_B1) * g
    v = ADAM_B2 * v + (1.0 - ADAM_B2) * _jnp.square(g)
    m_hat = m / (1.0 - ADAM_B1 ** ADAM_STEP)
    v_hat = v / (1.0 - ADAM_B2 ** ADAM_STEP)
    delta = -ADAM_LR * (m_hat / (_jnp.sqrt(v_hat) + ADAM_EPS) + ADAM_WD * w)
    return delta, m, v


def reference(x, mix_norm, ffn_norm, final_norm, hgrn_w_in, hgrn_lb_logits, hgrn_gnorm, hgrn_w_out, m_w_in, m_conv_w, m_conv_b, m_dt_bias, m_A_log, m_D, m_norm, m_w_out, f_w_up, f_conv_w, f_conv_b, f_w_down, loss_target, m_mix_norm, m_ffn_norm, m_final_norm, m_hgrn_w_in, m_hgrn_lb_logits, m_hgrn_gnorm, m_hgrn_w_out, m_m_w_in, m_m_conv_w, m_m_conv_b, m_m_dt_bias, m_m_A_log, m_m_D, m_m_norm, m_m_w_out, m_f_w_up, m_f_conv_w, m_f_conv_b, m_f_w_down, v_mix_norm, v_ffn_norm, v_final_norm, v_hgrn_w_in, v_hgrn_lb_logits, v_hgrn_gnorm, v_hgrn_w_out, v_m_w_in, v_m_conv_w, v_m_conv_b, v_m_dt_bias, v_m_A_log, v_m_D, v_m_norm, v_m_w_out, v_f_w_up, v_f_conv_w, v_f_conv_b, v_f_w_down):
    given = dict(x=x, mix_norm=mix_norm, ffn_norm=ffn_norm, final_norm=final_norm, hgrn_w_in=hgrn_w_in, hgrn_lb_logits=hgrn_lb_logits, hgrn_gnorm=hgrn_gnorm, hgrn_w_out=hgrn_w_out, m_w_in=m_w_in, m_conv_w=m_conv_w, m_conv_b=m_conv_b, m_dt_bias=m_dt_bias, m_A_log=m_A_log, m_D=m_D, m_norm=m_norm, m_w_out=m_w_out, f_w_up=f_w_up, f_conv_w=f_conv_w, f_conv_b=f_conv_b, f_w_down=f_w_down, loss_target=loss_target, m_mix_norm=m_mix_norm, m_ffn_norm=m_ffn_norm, m_final_norm=m_final_norm, m_hgrn_w_in=m_hgrn_w_in, m_hgrn_lb_logits=m_hgrn_lb_logits, m_hgrn_gnorm=m_hgrn_gnorm, m_hgrn_w_out=m_hgrn_w_out, m_m_w_in=m_m_w_in, m_m_conv_w=m_m_conv_w, m_m_conv_b=m_m_conv_b, m_m_dt_bias=m_m_dt_bias, m_m_A_log=m_m_A_log, m_m_D=m_m_D, m_m_norm=m_m_norm, m_m_w_out=m_m_w_out, m_f_w_up=m_f_w_up, m_f_conv_w=m_f_conv_w, m_f_conv_b=m_f_conv_b, m_f_w_down=m_f_w_down, v_mix_norm=v_mix_norm, v_ffn_norm=v_ffn_norm, v_final_norm=v_final_norm, v_hgrn_w_in=v_hgrn_w_in, v_hgrn_lb_logits=v_hgrn_lb_logits, v_hgrn_gnorm=v_hgrn_gnorm, v_hgrn_w_out=v_hgrn_w_out, v_m_w_in=v_m_w_in, v_m_conv_w=v_m_conv_w, v_m_conv_b=v_m_conv_b, v_m_dt_bias=v_m_dt_bias, v_m_A_log=v_m_A_log, v_m_D=v_m_D, v_m_norm=v_m_norm, v_m_w_out=v_m_w_out, v_f_w_up=v_f_w_up, v_f_conv_w=v_f_conv_w, v_f_conv_b=v_f_conv_b, v_f_w_down=v_f_w_down)
    weights = {n: given[n] for n in TWIN_WEIGHTS}
    shared = {n: given[n] for n in SHARED_INPUTS}
    per_example = {n: given[n] for n in ['x']}
    grad_fn = _jax.value_and_grad(_loss, argnums=(0, 1))

    def one_microbatch(ex, loss_target):
        ex = dict(ex)
        diff = ex.pop(TWIN_DIFF_INPUT)
        return grad_fn(weights, diff, {**shared, **ex}, loss_target)

    if N_MICROBATCH == 1:
        loss, (grad_w, grad_x) = one_microbatch(per_example, given["loss_target"])
    else:
        def body(carry, xs):
            loss_sum, grad_sum = carry
            l_k, (gw_k, gx_k) = one_microbatch(xs[0], xs[1])
            with _jax.named_scope("update"):
                return (loss_sum + l_k, _jax.tree.map(_jnp.add, grad_sum, gw_k)), gx_k

        init = (_jnp.zeros((), _jnp.float32), _jax.tree.map(_jnp.zeros_like, weights))
        (loss, grad_w), grad_x = _jax.lax.scan(body, init, (per_example, given["loss_target"]))
    with _jax.named_scope("update"):
        delta_w, new_m, new_v = {}, {}, {}
        for n in TWIN_WEIGHTS:
            delta_w[n], new_m[n], new_v[n] = _adamw(weights[n], grad_w[n], given["m_" + n], given["v_" + n])
    return (loss, grad_x, *[grad_w[n] for n in TWIN_WEIGHTS], *[delta_w[n] for n in TWIN_WEIGHTS],
            *[new_m[n] for n in TWIN_WEIGHTS], *[new_v[n] for n in TWIN_WEIGHTS])
```

```python
import functools
import math

import jax
import jax.numpy as jnp
from jax import lax
from jax.experimental import pallas as pl
from jax.experimental.pallas import tpu as pltpu

F32 = jnp.float32
BF16 = jnp.bfloat16
NORM_EPS = 1e-5
HGRN_DK = 128
HGRN_CHUNK = 64
M_HEADDIM = 64
M_GROUPS = 8
M_D_STATE = 128
M_CONV = 4
M_CHUNK = 128
FFN_CONV = 3
EXP_CLIP = 80.0
LANES = 128
VMEM_LIMIT_BYTES = 56 * 1024 * 1024
FLAT_COLS = 1024
ADAM_LR, ADAM_B1, ADAM_B2, ADAM_EPS, ADAM_WD, ADAM_STEP = 0.001, 0.9, 0.999, 1e-08, 0.01, 10
MESH = pl.DeviceIdType.MESH

NN = (((1,), (0,)), ((), ()))
NT = (((1,), (1,)), ((), ()))
TN = (((0,), (0,)), ((), ()))


def _cparams(*sems):
    return pltpu.CompilerParams(dimension_semantics=sems, vmem_limit_bytes=VMEM_LIMIT_BYTES)


def _dot(a, b, dn=NN):
    return lax.dot_general(a.astype(BF16), b.astype(BF16), dn, preferred_element_type=F32)


def _dot_exact(x, m, dn=NN, passes=3, x_first=True):
    acc = None
    r = x
    for _ in range(passes):
        p = r.astype(BF16)
        r = r - p.astype(F32)
        t = lax.dot_general(p, m, dn, preferred_element_type=F32) if x_first else lax.dot_general(m, p, dn, preferred_element_type=F32)
        acc = t if acc is None else acc + t
    return acc


def _iota(shape, dim):
    return lax.broadcasted_iota(jnp.int32, shape, dim)


def _cumsum_rows(x, reverse=False):
    n = x.shape[0]
    row = _iota(x.shape, 0)
    s = 1
    while s < n:
        if reverse:
            x = x + jnp.where(row < n - s, pltpu.roll(x, n - s, 0), 0.0)
        else:
            x = x + jnp.where(row >= s, pltpu.roll(x, s, 0), 0.0)
        s *= 2
    return x


def _silu(x):
    return x * jax.nn.sigmoid(x)


def _dsilu(x):
    s = jax.nn.sigmoid(x)
    return s * (1.0 + x * (1.0 - s))


def _pick_tile(dim, pref):
    if dim <= pref:
        return dim
    best = None
    t = LANES
    while t <= pref:
        if dim % t == 0:
            best = t
        t += LANES
    assert best is not None, (dim, pref)
    return best


def _matmul(a, b, *, ta=False, tb=False, res=None, out_dtype=F32, tm=1024, tn=1024, tk=512, name,
            a_parts=False, b_parts=False, b_layer=None, b_off=0, out=None, out_layer=None, out_off=0):
    a = a.astype(BF16)
    b = b.astype(BF16)
    if a_parts:
        assert not ta
        pa, M, kp = a.shape
        K = pa * kp
    else:
        M, K = (a.shape[1], a.shape[0]) if ta else a.shape
    bsh = b.shape[1:] if b_layer is not None else b.shape
    if b_parts:
        assert not tb
        pb, _, np_ = bsh
        N = pb * np_
    else:
        N = bsh[0] if tb else bsh[1]
    tm, tn, tk = _pick_tile(M, tm), _pick_tile(np_ if b_parts else N, tn), _pick_tile(kp if a_parts else K, tk)
    nk = K // tk
    dn = (((0 if ta else 1,), (1 if tb else 0,)), ((), ()))
    assert b_off % tk == 0 and out_off % tm == 0

    def body(*refs):
        if out is not None:
            refs = refs[:-3] + refs[-2:]
        if res is None:
            a_ref, b_ref, o_ref, acc = refs
        else:
            a_ref, b_ref, r_ref, o_ref, acc = refs
        k = pl.program_id(2)
        p = lax.dot_general(a_ref[...], b_ref[...], dn, preferred_element_type=F32)

        @pl.when(k == 0)
        def _():
            acc[...] = p

        @pl.when(k > 0)
        def _():
            acc[...] += p

        @pl.when(k == nk - 1)
        def _():
            r = acc[...]
            if res is not None:
                r = r + r_ref[...]
            o_ref[...] = r.astype(out_dtype)

    if a_parts:
        kpb = kp // tk
        a_spec = pl.BlockSpec((None, tm, tk), lambda i, j, k: (k // kpb, i, k % kpb))
    elif ta:
        a_spec = pl.BlockSpec((tk, tm), lambda i, j, k: (k, i))
    else:
        a_spec = pl.BlockSpec((tm, tk), lambda i, j, k: (i, k))
    lead = () if b_layer is None else (b_layer,)
    lead_blk = () if b_layer is None else (None,)
    kb0 = b_off // tk
    if b_parts:
        npb = np_ // tn
        b_spec = pl.BlockSpec(lead_blk + (None, tk, tn), lambda i, j, k: lead + (j // npb, k, j % npb))
    elif tb:
        b_spec = pl.BlockSpec(lead_blk + (tn, tk), lambda i, j, k: lead + (j, k))
    else:
        b_spec = pl.BlockSpec(lead_blk + (tk, tn), lambda i, j, k: lead + (kb0 + k, j))
    r_spec = pl.BlockSpec((tm, tn), lambda i, j, k: (i, j))
    in_specs = [a_spec, b_spec] + ([r_spec] if res is not None else [])
    args = (a, b) + ((res,) if res is not None else ())
    if out is None:
        o_spec, out_shape, aliases = r_spec, jax.ShapeDtypeStruct((M, N), out_dtype), {}
    else:
        assert out.dtype == out_dtype and out.shape[-1] == N
        olead = () if out_layer is None else (out_layer,)
        olead_blk = () if out_layer is None else (None,)
        ob0 = out_off // tm
        o_spec = pl.BlockSpec(olead_blk + (tm, tn), lambda i, j, k: olead + (ob0 + i, j))
        out_shape = jax.ShapeDtypeStruct(out.shape, out.dtype)
        aliases = {len(args): 0}
        in_specs = in_specs + [pl.BlockSpec(memory_space=pl.ANY)]
        args = args + (out,)
    return pl.pallas_call(
        body, name=name, grid=(M // tm, N // tn, nk), in_specs=in_specs, out_specs=o_spec, out_shape=out_shape,
        scratch_shapes=[pltpu.VMEM((tm, tn), F32)], input_output_aliases=aliases,
        compiler_params=_cparams("parallel", "parallel", "arbitrary"))(*args)


def _rmsnorm_fwd(h, w, name):
    T, D = h.shape
    tm = _pick_tile(T, 256)

    def body(h_ref, w_ref, u_ref):
        x = h_ref[...]
        r = lax.rsqrt(jnp.mean(x * x, axis=-1, keepdims=True) + NORM_EPS)
        u_ref[...] = (x * r * w_ref[...]).astype(BF16)

    return pl.pallas_call(
        body, name=name, grid=(T // tm,),
        in_specs=[pl.BlockSpec((tm, D), lambda i: (i, 0)), pl.BlockSpec((1, D), lambda i: (0, 0))],
        out_specs=pl.BlockSpec((tm, D), lambda i: (i, 0)), out_shape=jax.ShapeDtypeStruct((T, D), BF16),
        compiler_params=_cparams("parallel"))(h, w.reshape(1, D))


def _rmsnorm_bwd(h, w, du, dres, name):
    T, D = h.shape
    tm = _pick_tile(T, 256)

    def body(h_ref, w_ref, du_ref, dr_ref, dh_ref, dw_ref):
        x = h_ref[...]
        r = lax.rsqrt(jnp.mean(x * x, axis=-1, keepdims=True) + NORM_EPS)
        xh = x * r
        du_ = du_ref[...]
        dy = du_ * w_ref[...]
        dh_ref[...] = dr_ref[...] + r * (dy - xh * jnp.mean(dy * xh, axis=-1, keepdims=True))
        part = jnp.sum(du_ * xh, axis=0, keepdims=True)

        @pl.when(pl.program_id(0) == 0)
        def _():
            dw_ref[...] = part

        @pl.when(pl.program_id(0) > 0)
        def _():
            dw_ref[...] += part

    row = pl.BlockSpec((tm, D), lambda i: (i, 0))
    vec = pl.BlockSpec((1, D), lambda i: (0, 0))
    dh, dw = pl.pallas_call(
        body, name=name, grid=(T // tm,), in_specs=[row, vec, row, row], out_specs=[row, vec],
        out_shape=[jax.ShapeDtypeStruct((T, D), F32), jax.ShapeDtypeStruct((1, D), F32)],
        compiler_params=_cparams("arbitrary"))(h, w.reshape(1, D), du, dres)
    return dh, dw.reshape(D)


def _loss_head(h, w, target):
    T, D = h.shape
    tm = _pick_tile(T, 256)

    def body(h_ref, w_ref, t_ref, loss_ref, dh_ref, dw_ref):
        x = h_ref[...]
        wv = w_ref[...]
        r = lax.rsqrt(jnp.mean(x * x, axis=-1, keepdims=True) + NORM_EPS)
        xh = x * r
        e = xh * wv - t_ref[...]
        lpart = jnp.zeros((1, LANES), F32) + 0.5 * jnp.sum(jnp.mean(e * e, axis=-1, keepdims=True))
        dyo = e * (1.0 / D)
        dy = dyo * wv
        dh_ref[...] = r * (dy - xh * jnp.mean(dy * xh, axis=-1, keepdims=True))
        part = jnp.sum(dyo * xh, axis=0, keepdims=True)

        @pl.when(pl.program_id(0) == 0)
        def _():
            dw_ref[...] = part
            loss_ref[...] = lpart

        @pl.when(pl.program_id(0) > 0)
        def _():
            dw_ref[...] += part
            loss_ref[...] += lpart

    row = pl.BlockSpec((tm, D), lambda i: (i, 0))
    vec = pl.BlockSpec((1, D), lambda i: (0, 0))
    lvec = pl.BlockSpec((1, LANES), lambda i: (0, 0))
    loss, dh, dw = pl.pallas_call(
        body, name="loss_head", grid=(T // tm,), in_specs=[row, vec, row], out_specs=[lvec, row, vec],
        out_shape=[jax.ShapeDtypeStruct((1, LANES), F32), jax.ShapeDtypeStruct((T, D), F32),
                   jax.ShapeDtypeStruct((1, D), F32)],
        compiler_params=_cparams("arbitrary"))(h, w.reshape(1, D), target)
    return loss, dh, dw.reshape(D)


def _hgrn_gates(qr, fr, lb):
    sig = jax.nn.sigmoid(fr)
    nsig = jax.nn.sigmoid(-fr)
    fg = lb + (1.0 - lb) * sig
    logf = jnp.log(fg)
    k = (1.0 - lb) * nsig
    q = _silu(qr)
    return q, k, logf, sig, nsig, fg


def _hgrn_scaled(q, k, b, bmid):
    eq = jnp.exp(jnp.clip(b - bmid, -EXP_CLIP, EXP_CLIP))
    ek = jnp.exp(jnp.clip(bmid - b, -EXP_CLIP, EXP_CLIP))
    return q * eq, k * ek, eq, ek


def _hgrn_fwd(proj, lb, gnw, H):
    B, L, _ = proj.shape
    C, DK = HGRN_CHUNK, HGRN_DK
    F_ = H * DK
    NC = L // C

    def body(q_ref, f_ref, v_ref, g_ref, lb_ref, gn_ref, o_ref, on_ref, st_ref, ST, bsc):
        ST[...] = jnp.zeros_like(ST)
        lbv = lb_ref[...]
        gn = gn_ref[...]
        causal = _iota((C, C), 0) >= _iota((C, C), 1)

        def chunk(c, carry):
            r0 = pl.multiple_of(c * C, C)
            rows = pl.ds(r0, C)
            q, k, logf, _, _, _ = _hgrn_gates(q_ref[0, rows, :], f_ref[0, rows, :], lbv)
            v = v_ref[0, rows, :]
            b = _cumsum_rows(logf)
            bsc[...] = b
            bmid = bsc[C // 2 - 1:C // 2, :]
            blast = bsc[C - 1:C, :]
            qs, ks, _, _ = _hgrn_scaled(q, k, b, bmid)
            A = jnp.where(causal, _dot(qs, ks, NT), 0.0)
            st = ST[...]
            st_ref[0, 0, c] = st
            o = _dot(A, v) + _dot(q * jnp.exp(b), st, NT)
            kb = k * jnp.exp(blast - b)
            ST[...] = st * jnp.exp(blast) + _dot(v, kb, TN)
            rms = lax.rsqrt(jnp.mean(o * o, axis=-1, keepdims=True) + NORM_EPS)
            o_ref[0, rows, :] = o
            on_ref[0, rows, :] = (o * rms * gn * _silu(g_ref[0, rows, :])).astype(BF16)
            return carry

        lax.fori_loop(0, NC, chunk, 0)

    def col(off):
        return pl.BlockSpec((1, L, DK), lambda b, h: (b, 0, off + h))

    return pl.pallas_call(
        body, name="hgrn_fwd", grid=(B, H),
        in_specs=[col(0), col(H), col(2 * H), col(3 * H), pl.BlockSpec((1, DK), lambda b, h: (0, h)),
                  pl.BlockSpec((1, DK), lambda b, h: (0, 0))],
        out_specs=[col(0), col(0), pl.BlockSpec((1, 1, NC, DK, DK), lambda b, h: (b, h, 0, 0, 0))],
        out_shape=[jax.ShapeDtypeStruct((B, L, F_), F32), jax.ShapeDtypeStruct((B, L, F_), BF16),
                   jax.ShapeDtypeStruct((B, H, NC, DK, DK), F32)],
        scratch_shapes=[pltpu.VMEM((DK, DK), F32), pltpu.VMEM((C, DK), F32)],
        compiler_params=_cparams("parallel", "parallel"))(proj, proj, proj, proj, lb, gnw)


def _hgrn_bwd(proj, o, don, st, lb, gnw, H):
    B, L, _ = proj.shape
    C, DK = HGRN_CHUNK, HGRN_DK
    F_ = H * DK
    NC = L // C

    def body(q_ref, f_ref, v_ref, g_ref, o_ref, do_ref, st_ref, lb_ref, gn_ref,
             dp_ref, dlb_ref, dgn_ref, DST, bsc):
        DST[...] = jnp.zeros_like(DST)
        dlb_ref[...] = jnp.zeros_like(dlb_ref)
        dgn_ref[...] = jnp.zeros_like(dgn_ref)
        lbv = lb_ref[...]
        gn = gn_ref[...]
        causal = _iota((C, C), 0) >= _iota((C, C), 1)
        lastrow = _iota((C, DK), 0) == C - 1

        def chunk(i, carry):
            c = NC - 1 - i
            r0 = pl.multiple_of(c * C, C)
            rows = pl.ds(r0, C)
            qr = q_ref[0, rows, :]
            fr = f_ref[0, rows, :]
            q, k, logf, sig, nsig, fg = _hgrn_gates(qr, fr, lbv)
            v = v_ref[0, rows, :]
            b = _cumsum_rows(logf)
            bsc[...] = b
            bmid = bsc[C // 2 - 1:C // 2, :]
            blast = bsc[C - 1:C, :]
            qs, ks, eq, ek = _hgrn_scaled(q, k, b, bmid)
            A = jnp.where(causal, _dot(qs, ks, NT), 0.0)
            st_in = st_ref[0, 0, c]
            dst = DST[...]
            eb = jnp.exp(b)
            ebl = jnp.exp(blast)
            ekb = jnp.exp(blast - b)
            qb = q * eb
            kb = k * ekb
            ov = o_ref[0, rows, :]
            gr = g_ref[0, rows, :]
            rms = lax.rsqrt(jnp.mean(ov * ov, axis=-1, keepdims=True) + NORM_EPS)
            oh = ov * rms
            sg = _silu(gr)
            don_ = do_ref[0, rows, :]
            dgn_ref[0, 0] += jnp.sum(don_ * oh * sg, axis=0, keepdims=True)
            dp_ref[3, 0, rows, :] = (don_ * oh * gn * _dsilu(gr)).astype(BF16)
            doh = don_ * gn * sg
            do_ = rms * (doh - oh * jnp.mean(doh * oh, axis=-1, keepdims=True))
            dA = jnp.where(causal, _dot(do_, v, NT), 0.0)
            dp_ref[2, 0, rows, :] = (_dot(A, do_, TN) + _dot(kb, dst, NT)).astype(BF16)
            dqb = _dot(do_, st_in)
            dkb = _dot(v, dst)
            dq = _dot(dA, ks) * eq + dqb * eb
            dk_inter = dkb * ekb
            dk = _dot(dA, qs, TN) * ek + dk_inter
            db = q * dq - k * dk
            extra = jnp.sum(k * dk_inter, axis=0, keepdims=True) + ebl * jnp.sum(st_in * dst, axis=0, keepdims=True)
            db = db + jnp.where(lastrow, extra, 0.0)
            dlogf = _cumsum_rows(db, reverse=True)
            DST[...] = dst * ebl + _dot(do_, qb, TN)
            dp_ref[0, 0, rows, :] = (dq * _dsilu(qr)).astype(BF16)
            ss = sig * nsig
            dp_ref[1, 0, rows, :] = ((1.0 - lbv) * ss * (dlogf / fg - dk)).astype(BF16)
            dlb_ref[0] += jnp.sum(dlogf * nsig / fg - dk * nsig, axis=0, keepdims=True)
            return carry

        lax.fori_loop(0, NC, chunk, 0)

    def col(off):
        return pl.BlockSpec((1, L, DK), lambda b, h: (b, 0, off + h))

    outs = pl.pallas_call(
        body, name="hgrn_bwd", grid=(B, H),
        in_specs=[col(0), col(H), col(2 * H), col(3 * H), col(0), col(0),
                  pl.BlockSpec((1, 1, NC, DK, DK), lambda b, h: (b, h, 0, 0, 0)),
                  pl.BlockSpec((1, DK), lambda b, h: (0, h)), pl.BlockSpec((1, DK), lambda b, h: (0, 0))],
        out_specs=[pl.BlockSpec((4, 1, L, DK), lambda b, h: (0, b, 0, h)), pl.BlockSpec((1, 1, DK), lambda b, h: (b, 0, h)),
                   pl.BlockSpec((1, 1, 1, DK), lambda b, h: (b, h, 0, 0))],
        out_shape=[jax.ShapeDtypeStruct((4, B, L, F_), BF16), jax.ShapeDtypeStruct((B, 1, F_), F32),
                   jax.ShapeDtypeStruct((B, H, 1, DK), F32)],
        scratch_shapes=[pltpu.VMEM((DK, DK), F32), pltpu.VMEM((C, DK), F32)],
        compiler_params=_cparams("parallel", "parallel"))(proj, proj, proj, proj, o, don, st, lb, gnw)
    return outs


CONV_ROWS = 256
PAD_ROWS = 8


def _conv_taps(pad_ref, w_ref, r0, K, rb, forward=True):
    ext = pad_ref[pl.ds(r0, rb + PAD_ROWS), :]
    n = rb + PAD_ROWS
    acc = None
    for s in range(K):
        if forward:
            sh = ext if s == 0 else pltpu.roll(ext, s, 0)
            term = sh[PAD_ROWS:, :]
        else:
            sh = ext if s == 0 else pltpu.roll(ext, n - s, 0)
            term = sh[:rb, :]
        term = term * w_ref[K - 1 - s:K - s, :]
        acc = term if acc is None else acc + term
    return acc


def _conv_dw(ext, dc, K):
    row = _iota((8, dc.shape[1]), 0)
    out = jnp.zeros((8, dc.shape[1]), F32)
    for kk in range(K):
        s = K - 1 - kk
        sh = ext if s == 0 else pltpu.roll(ext, s, 0)
        out = out + jnp.where(row == kk, jnp.sum(dc * sh[PAD_ROWS:, :], axis=0, keepdims=True), 0.0)
    return out


def _mconv_fwd(zx, cw, cb, col0, width):
    B, L, _ = zx.shape
    K = cw.shape[0]
    ct = _pick_tile(width, 256)
    rb = min(CONV_ROWS, L)
    nrb = L // rb
    off = col0 // ct

    def body(x_ref, w_ref, b_ref, y_ref, xp):
        xp[0:PAD_ROWS, :] = jnp.zeros((PAD_ROWS, ct), F32)
        xp[PAD_ROWS:, :] = x_ref[0]
        bias = b_ref[...]

        def blk(i, carry):
            r0 = pl.multiple_of(i * rb, rb)
            y_ref[0, pl.ds(r0, rb), :] = _silu(_conv_taps(xp, w_ref, r0, K, rb) + bias)
            return carry

        lax.fori_loop(0, nrb, blk, 0)

    return pl.pallas_call(
        body, name="mconv_fwd", grid=(B, width // ct),
        in_specs=[pl.BlockSpec((1, L, ct), lambda b, j: (b, 0, off + j)), pl.BlockSpec((K, ct), lambda b, j: (0, j)),
                  pl.BlockSpec((1, ct), lambda b, j: (0, j))],
        out_specs=pl.BlockSpec((1, L, ct), lambda b, j: (b, 0, j)),
        out_shape=jax.ShapeDtypeStruct((B, L, width), F32),
        scratch_shapes=[pltpu.VMEM((L + PAD_ROWS, ct), F32)],
        compiler_params=_cparams("parallel", "parallel"))(zx, cw, cb.reshape(1, width))


def _mconv_bwd(zx, dya, cw, cb, col0, wcol0, name):
    B, L, _ = zx.shape
    K = cw.shape[0]
    npart, _, _, wq = dya.shape
    width = npart * wq
    ct = _pick_tile(wq, 256)
    rb = min(CONV_ROWS, L)
    nrb = L // rb
    off = (col0 + wcol0) // ct
    woff = wcol0 // ct
    pq = wq // ct

    def body(x_ref, dy_ref, w_ref, b_ref, dx_ref, dw_ref, db_ref, xp, dcp):
        xp[0:PAD_ROWS, :] = jnp.zeros((PAD_ROWS, ct), F32)
        xp[PAD_ROWS:, :] = x_ref[0]
        dcp[L:, :] = jnp.zeros((PAD_ROWS, ct), F32)
        bias = b_ref[...]

        def blk1(i, carry):
            dw, db = carry
            r0 = pl.multiple_of(i * rb, rb)
            cpre = _conv_taps(xp, w_ref, r0, K, rb) + bias
            dc = dy_ref[0, 0, pl.ds(r0, rb), :] * _dsilu(cpre)
            dcp[pl.ds(r0, rb), :] = dc
            ext = xp[pl.ds(r0, rb + PAD_ROWS), :]
            return dw + _conv_dw(ext, dc, K), db + jnp.sum(dc, axis=0, keepdims=True)

        dw, db = lax.fori_loop(0, nrb, blk1, (jnp.zeros((8, ct), F32), jnp.zeros((1, ct), F32)))
        dw_ref[0] = dw
        db_ref[0] = db

        def blk2(i, carry):
            r0 = pl.multiple_of(i * rb, rb)
            dx_ref[0, pl.ds(r0, rb), :] = _conv_taps(dcp, w_ref, r0, K, rb, forward=False).astype(BF16)
            return carry

        lax.fori_loop(0, nrb, blk2, 0)

    dx, dw, db = pl.pallas_call(
        body, name=name, grid=(B, width // ct),
        in_specs=[pl.BlockSpec((1, L, ct), lambda b, j: (b, 0, off + j)),
                  pl.BlockSpec((1, 1, L, ct), lambda b, j: (j // pq, b, 0, j % pq)),
                  pl.BlockSpec((K, ct), lambda b, j: (0, woff + j)), pl.BlockSpec((1, ct), lambda b, j: (0, woff + j))],
        out_specs=[pl.BlockSpec((1, L, ct), lambda b, j: (b, 0, j)), pl.BlockSpec((1, 8, ct), lambda b, j: (b, 0, j)),
                   pl.BlockSpec((1, 1, ct), lambda b, j: (b, 0, j))],
        out_shape=[jax.ShapeDtypeStruct((B, L, width), BF16), jax.ShapeDtypeStruct((B, 8, width), F32),
                   jax.ShapeDtypeStruct((B, 1, width), F32)],
        scratch_shapes=[pltpu.VMEM((L + PAD_ROWS, ct), F32), pltpu.VMEM((L + PAD_ROWS, ct), F32)],
        compiler_params=_cparams("parallel", "parallel"))(zx, dya, cw, cb.reshape(1, -1))
    return dx, dw[:, :K, :], db


def _ffn_mid_fwd(up, cw, cb, dff):
    B, L, _ = up.shape
    K = cw.shape[0]
    ct = _pick_tile(dff, 256)
    rb = min(CONV_ROWS, L)
    nrb = L // rb
    half = dff // ct

    def body(g_ref, u_ref, wg_ref, wu_ref, bg_ref, bu_ref, a_ref, gp, upad):
        gp[0:PAD_ROWS, :] = jnp.zeros((PAD_ROWS, ct), F32)
        upad[0:PAD_ROWS, :] = jnp.zeros((PAD_ROWS, ct), F32)
        gp[PAD_ROWS:, :] = g_ref[0]
        upad[PAD_ROWS:, :] = u_ref[0]
        bg, bu = bg_ref[...], bu_ref[...]

        def blk(i, carry):
            r0 = pl.multiple_of(i * rb, rb)
            cg = _conv_taps(gp, wg_ref, r0, K, rb) + bg
            cu = _conv_taps(upad, wu_ref, r0, K, rb) + bu
            a_ref[0, pl.ds(r0, rb), :] = (_silu(cg) * cu).astype(BF16)
            return carry

        lax.fori_loop(0, nrb, blk, 0)

    xg = pl.BlockSpec((1, L, ct), lambda b, j: (b, 0, j))
    xu = pl.BlockSpec((1, L, ct), lambda b, j: (b, 0, half + j))
    wgs = pl.BlockSpec((K, ct), lambda b, j: (0, j))
    wus = pl.BlockSpec((K, ct), lambda b, j: (0, half + j))
    bgs = pl.BlockSpec((1, ct), lambda b, j: (0, j))
    bus = pl.BlockSpec((1, ct), lambda b, j: (0, half + j))
    cb2 = cb.reshape(1, 2 * dff)
    return pl.pallas_call(
        body, name="ffn_mid_fwd", grid=(B, half), in_specs=[xg, xu, wgs, wus, bgs, bus], out_specs=xg,
        out_shape=jax.ShapeDtypeStruct((B, L, dff), BF16),
        scratch_shapes=[pltpu.VMEM((L + PAD_ROWS, ct), F32), pltpu.VMEM((L + PAD_ROWS, ct), F32)],
        compiler_params=_cparams("parallel", "parallel"))(up, up, cw, cw, cb2, cb2)


def _ffn_mid_bwd(up, dact, cw, cb, dff):
    B, L, _ = up.shape
    K = cw.shape[0]
    ct = _pick_tile(dff, 256)
    rb = min(CONV_ROWS, L)
    nrb = L // rb
    half = dff // ct

    def body(g_ref, u_ref, da_ref, wg_ref, wu_ref, bg_ref, bu_ref, dx_ref, dwg_ref, dwu_ref, dbg_ref, dbu_ref,
             gp, upad, dgp, dup):
        gp[0:PAD_ROWS, :] = jnp.zeros((PAD_ROWS, ct), F32)
        upad[0:PAD_ROWS, :] = jnp.zeros((PAD_ROWS, ct), F32)
        gp[PAD_ROWS:, :] = g_ref[0]
        upad[PAD_ROWS:, :] = u_ref[0]
        dgp[L:, :] = jnp.zeros((PAD_ROWS, ct), F32)
        dup[L:, :] = jnp.zeros((PAD_ROWS, ct), F32)
        bg, bu = bg_ref[...], bu_ref[...]

        def blk1(i, carry):
            dwg, dwu, dbg, dbu = carry
            r0 = pl.multiple_of(i * rb, rb)
            cg = _conv_taps(gp, wg_ref, r0, K, rb) + bg
            cu = _conv_taps(upad, wu_ref, r0, K, rb) + bu
            da = da_ref[0, pl.ds(r0, rb), :]
            dcg = da * cu * _dsilu(cg)
            dcu = da * _silu(cg)
            dgp[pl.ds(r0, rb), :] = dcg
            dup[pl.ds(r0, rb), :] = dcu
            eg = gp[pl.ds(r0, rb + PAD_ROWS), :]
            eu = upad[pl.ds(r0, rb + PAD_ROWS), :]
            return (dwg + _conv_dw(eg, dcg, K), dwu + _conv_dw(eu, dcu, K), dbg + jnp.sum(dcg, axis=0, keepdims=True),
                    dbu + jnp.sum(dcu, axis=0, keepdims=True))

        z8 = jnp.zeros((8, ct), F32)
        z1 = jnp.zeros((1, ct), F32)
        dwg, dwu, dbg, dbu = lax.fori_loop(0, nrb, blk1, (z8, z8, z1, z1))
        dwg_ref[0] = dwg
        dwu_ref[0] = dwu
        dbg_ref[0] = dbg
        dbu_ref[0] = dbu

        def blk2(i, carry):
            r0 = pl.multiple_of(i * rb, rb)
            dx_ref[0, 0, pl.ds(r0, rb), :] = _conv_taps(dgp, wg_ref, r0, K, rb, forward=False).astype(BF16)
            dx_ref[1, 0, pl.ds(r0, rb), :] = _conv_taps(dup, wu_ref, r0, K, rb, forward=False).astype(BF16)
            return carry

        lax.fori_loop(0, nrb, blk2, 0)

    xg = pl.BlockSpec((1, L, ct), lambda b, j: (b, 0, j))
    xu = pl.BlockSpec((1, L, ct), lambda b, j: (b, 0, half + j))
    wgs = pl.BlockSpec((K, ct), lambda b, j: (0, j))
    wus = pl.BlockSpec((K, ct), lambda b, j: (0, half + j))
    bgs = pl.BlockSpec((1, ct), lambda b, j: (0, j))
    bus = pl.BlockSpec((1, ct), lambda b, j: (0, half + j))
    w8 = pl.BlockSpec((1, 8, ct), lambda b, j: (b, 0, j))
    b1 = pl.BlockSpec((1, 1, ct), lambda b, j: (b, 0, j))
    cb2 = cb.reshape(1, 2 * dff)
    pad = pltpu.VMEM((L + PAD_ROWS, ct), F32)
    dx2, dwg, dwu, dbg, dbu = pl.pallas_call(
        body, name="ffn_mid_bwd", grid=(B, half), in_specs=[xg, xu, xg, wgs, wus, bgs, bus],
        out_specs=[pl.BlockSpec((2, 1, L, ct), lambda b, j: (0, b, 0, j)), w8, w8, b1, b1],
        out_shape=[jax.ShapeDtypeStruct((2, B, L, dff), BF16)] + [jax.ShapeDtypeStruct((B, 8, dff), F32)] * 2
        + [jax.ShapeDtypeStruct((B, 1, dff), F32)] * 2,
        scratch_shapes=[pad, pad, pad, pad],
        compiler_params=_cparams("parallel", "parallel"))(up, up, dact, cw, cw, cb2, cb2)
    dw = jnp.concatenate([dwg[:, :K], dwu[:, :K]], axis=-1)
    db = jnp.concatenate([dbg, dbu], axis=-1)
    return dx2, dw, db


def _ssd_consts(hpg, W):
    P = M_HEADDIM
    E = (_iota((LANES, W), 0) == _iota((LANES, W), 1) // P).astype(BF16)
    Ebig = (_iota((LANES, hpg * LANES), 0) == _iota((LANES, hpg * LANES), 1) // LANES).astype(BF16)
    causal = _iota((M_CHUNK, M_CHUNK), 0) >= _iota((M_CHUNK, M_CHUNK), 1)
    head_of_lane = _iota((1, W), 1) // P
    return E, Ebig, causal, head_of_lane


def _ssd_chunk_fwd(xs, Bm, Cm, dtr, bias, Aneg, E, Ebig, causal, head_of_lane, hpg, st, ar_sc, ae_sc):
    pre = dtr + bias
    dt = jnp.maximum(pre, 0.0) + jnp.log(1.0 + jnp.exp(-jnp.abs(pre)))
    Ad = dt * Aneg
    a_c = _cumsum_rows(Ad)
    ar_sc[...] = a_c.T
    aexp = _dot_exact(a_c, E)
    ae_sc[...] = aexp
    alast = ae_sc[M_CHUNK - 1:M_CHUNK, :]
    dtexp = _dot_exact(dt, E)
    X = xs * dtexp
    AC = _dot_exact(a_c, Ebig)
    CB = _dot(Cm, Bm, NT)
    Xb = X.astype(BF16)
    ydiag = jnp.zeros_like(xs)
    Ls = []
    for j in range(hpg):
        Lj = jnp.where(causal, jnp.exp(jnp.minimum(AC[:, j * LANES:(j + 1) * LANES] - ar_sc[j:j + 1, :], 0.0)), 0.0)
        Ls.append(Lj)
        Yj = _dot(CB * Lj, Xb)
        ydiag = ydiag + jnp.where(head_of_lane == j, Yj, 0.0)
    ea = jnp.exp(aexp)
    yoff = ea * _dot(Cm, st)
    dec = jnp.exp(alast - aexp)
    return dict(dt=dt, a_c=a_c, aexp=aexp, alast=alast, dtexp=dtexp, X=X, Xb=Xb, CB=CB, Ls=Ls, ydiag=ydiag, ea=ea,
                yoff=yoff, dec=dec)


def _ssd_fwd(xbca, zx, dtc, bias, Aneg, Dexp, nw, hpg):
    B, L, _ = xbca.shape
    G, N, C = M_GROUPS, M_D_STATE, M_CHUNK
    W = hpg * M_HEADDIM
    DI = G * W
    NC = L // C
    LB = min(L, 4 * C)
    ncb = LB // C

    def body(xs_ref, b_ref, c_ref, z_ref, dt_ref, bias_ref, a_ref, d_ref, nw_ref, y_ref, yn_ref, st_ref, ST, ar_sc, ae_sc):
        @pl.when(pl.program_id(2) == 0)
        def _():
            ST[...] = jnp.zeros_like(ST)

        E, Ebig, causal, head_of_lane = _ssd_consts(hpg, W)
        bias_ = bias_ref[0]
        Aneg_ = a_ref[0]
        Dv = d_ref[...]
        nwv = nw_ref[...]

        def chunk(ci, carry):
            r0 = pl.multiple_of(ci * C, C)
            rows = pl.ds(r0, C)
            xs = xs_ref[0, rows, :]
            Bm = b_ref[0, rows, :]
            Cm = c_ref[0, rows, :]
            st = ST[...]
            st_ref[0, 0, ci] = st
            f = _ssd_chunk_fwd(xs, Bm, Cm, dt_ref[0, 0, ci], bias_, Aneg_, E, Ebig, causal, head_of_lane, hpg, st, ar_sc, ae_sc)
            y = f["ydiag"] + f["yoff"] + xs * Dv
            ST[...] = st * jnp.exp(f["alast"]) + _dot(Bm, f["X"] * f["dec"], TN)
            yg = y * _silu(z_ref[0, rows, :])
            rstd = lax.rsqrt(jnp.mean(yg * yg, axis=-1, keepdims=True) + NORM_EPS)
            y_ref[0, rows, :] = y
            yn_ref[0, rows, :] = (yg * rstd * nwv).astype(BF16)
            return carry

        lax.fori_loop(0, ncb, chunk, 0)

    xw = pl.BlockSpec((1, LB, W), lambda b, g, s: (b, s, g))
    bsp = pl.BlockSpec((1, LB, N), lambda b, g, s: (b, s, DI // N + g))
    csp = pl.BlockSpec((1, LB, N), lambda b, g, s: (b, s, DI // N + G + g))
    dts = pl.BlockSpec((1, 1, ncb, C, LANES), lambda b, g, s: (b, g, s, 0, 0))
    hv = pl.BlockSpec((1, 1, LANES), lambda b, g, s: (g, 0, 0))
    wv = pl.BlockSpec((1, W), lambda b, g, s: (0, g))
    sts = pl.BlockSpec((1, 1, ncb, N, W), lambda b, g, s: (b, g, s, 0, 0))
    return pl.pallas_call(
        body, name="ssd_fwd", grid=(B, G, L // LB), in_specs=[xw, bsp, csp, xw, dts, hv, hv, wv, wv],
        out_specs=[xw, xw, sts],
        out_shape=[jax.ShapeDtypeStruct((B, L, DI), F32), jax.ShapeDtypeStruct((B, L, DI), BF16),
                   jax.ShapeDtypeStruct((B, G, NC, N, W), F32)],
        scratch_shapes=[pltpu.VMEM((N, W), F32), pltpu.VMEM((LANES, C), F32), pltpu.VMEM((C, W), F32)],
        compiler_params=_cparams("parallel", "parallel", "arbitrary"))(xbca, xbca, xbca, zx, dtc, bias, Aneg, Dexp, nw)


def _ssd_bwd(xbca, zx, dtc, ypre, dyn, st, bias, Aneg, Dexp, nw, hpg):
    B, L, _ = xbca.shape
    G, N, C = M_GROUPS, M_D_STATE, M_CHUNK
    W = hpg * M_HEADDIM
    DI = G * W
    NC = L // C
    LB = min(L, 4 * C)
    ncb = LB // C
    nsb = L // LB

    def body(xs_ref, b_ref, c_ref, z_ref, dt_ref, y_ref, dyn_ref, st_ref, bias_ref, a_ref, d_ref, nw_ref,
             dxs_ref, dbc_ref, dz_ref, ddt_ref, dnw_ref, dd_ref, da_ref, dbias_ref, DST, ar_sc, ae_sc):
        @pl.when(pl.program_id(2) == 0)
        def _():
            DST[...] = jnp.zeros_like(DST)
            dnw_ref[...] = jnp.zeros_like(dnw_ref)
            dd_ref[...] = jnp.zeros_like(dd_ref)
            da_ref[...] = jnp.zeros_like(da_ref)
            dbias_ref[...] = jnp.zeros_like(dbias_ref)

        E, Ebig, causal, head_of_lane = _ssd_consts(hpg, W)
        bias_ = bias_ref[0]
        Aneg_ = a_ref[0]
        Dv = d_ref[...]
        nwv = nw_ref[...]
        lane = _iota((1, LANES), 1)
        subl = _iota((LANES, 1), 0)
        lastrow = _iota((C, W), 0) == C - 1

        def chunk(i, carry):
            ci = ncb - 1 - i
            r0 = pl.multiple_of(ci * C, C)
            rows = pl.ds(r0, C)
            xs = xs_ref[0, rows, :]
            Bm = b_ref[0, rows, :]
            Cm = c_ref[0, rows, :]
            zr = z_ref[0, rows, :]
            dtr = dt_ref[0, 0, ci]
            st_in = st_ref[0, 0, ci]
            dst = DST[...]
            f = _ssd_chunk_fwd(xs, Bm, Cm, dtr, bias_, Aneg_, E, Ebig, causal, head_of_lane, hpg, st_in, ar_sc, ae_sc)
            X, Xb, dec, ea, CB = f["X"], f["Xb"], f["dec"], f["ea"], f["CB"]
            y = y_ref[0, rows, :]
            sz = _silu(zr)
            yg = y * sz
            rstd = lax.rsqrt(jnp.mean(yg * yg, axis=-1, keepdims=True) + NORM_EPS)
            yh = yg * rstd
            dyn_ = dyn_ref[0, rows, :]
            dnw_ref[0, 0] += jnp.sum(dyn_ * yh, axis=0, keepdims=True)
            dyh = dyn_ * nwv
            dyg = rstd * (dyh - yh * jnp.mean(dyh * yh, axis=-1, keepdims=True))
            dz_ref[0, rows, :] = (dyg * y * _dsilu(zr)).astype(BF16)
            dy = dyg * sz
            dd_ref[0, 0] += jnp.sum(dy * xs, axis=0, keepdims=True)
            dxs = dy * Dv
            dYo = dy * ea
            daexp = dy * f["yoff"]
            dCm = _dot(dYo, st_in, NT)
            dst_in = _dot(Cm, dYo, TN)
            dyb = dy.astype(BF16)
            dX = jnp.zeros_like(xs)
            dCB = jnp.zeros((C, C), F32)
            da_col = jnp.zeros((C, LANES), F32)
            da_row = jnp.zeros((LANES, C), F32)
            for j in range(hpg):
                Lj = f["Ls"][j]
                Gj = CB * Lj
                dYj = jnp.where(head_of_lane == j, dyb, jnp.zeros_like(dyb))
                dX = dX + _dot(Gj, dYj, TN)
                dGj = _dot(dYj, Xb, NT)
                dCB = dCB + dGj * Lj
                Wj = dGj * Gj
                da_col = da_col + jnp.sum(Wj, axis=1, keepdims=True) * (lane == j).astype(F32)
                da_row = da_row + (subl == j).astype(F32) * jnp.sum(Wj, axis=0, keepdims=True)
            dCm = dCm + _dot(dCB, Bm)
            dBm = _dot(dCB, Cm, TN)
            ela = jnp.exp(f["alast"])
            dalast = jnp.sum(dst * st_in, axis=0, keepdims=True) * ela
            DST[...] = dst * ela + dst_in
            dXd = _dot(Bm, dst)
            dBm = dBm + _dot(X * dec, dst, NT)
            dX = dX + dXd * dec
            ddec = dXd * X * dec
            dalast = dalast + jnp.sum(ddec, axis=0, keepdims=True)
            daexp = daexp - ddec + jnp.where(lastrow, dalast, 0.0)
            dxs = dxs + dX * f["dtexp"]
            ddtexp = dX * xs
            ddt = _dot_exact(ddtexp, E, NT, passes=2)
            da_c = _dot_exact(daexp, E, NT, passes=2) + da_col - da_row.T
            dAd = _cumsum_rows(da_c, reverse=True)
            ddt = ddt + dAd * Aneg_
            da_ref[0, 0] += jnp.sum(dAd * f["dt"], axis=0, keepdims=True) * Aneg_
            ddtr = ddt * jax.nn.sigmoid(dtr + bias_)
            dbias_ref[0, 0] += jnp.sum(ddtr, axis=0, keepdims=True)
            ddt_ref[0, 0, ci] = ddtr
            dxs_ref[0, rows, :] = dxs
            dbc_ref[0, 0, rows, :] = dBm
            dbc_ref[1, 0, rows, :] = dCm
            return carry

        lax.fori_loop(0, ncb, chunk, 0)

    def rev(s):
        return nsb - 1 - s

    xw = pl.BlockSpec((1, LB, W), lambda b, g, s: (b, rev(s), g))
    bsp = pl.BlockSpec((1, LB, N), lambda b, g, s: (b, rev(s), DI // N + g))
    csp = pl.BlockSpec((1, LB, N), lambda b, g, s: (b, rev(s), DI // N + G + g))
    gsp = pl.BlockSpec((1, LB, N), lambda b, g, s: (b, rev(s), g))
    dts = pl.BlockSpec((1, 1, ncb, C, LANES), lambda b, g, s: (b, g, rev(s), 0, 0))
    hv = pl.BlockSpec((1, 1, LANES), lambda b, g, s: (g, 0, 0))
    wv = pl.BlockSpec((1, W), lambda b, g, s: (0, g))
    sts = pl.BlockSpec((1, 1, ncb, N, W), lambda b, g, s: (b, g, rev(s), 0, 0))
    accw = pl.BlockSpec((1, 1, 1, W), lambda b, g, s: (b, g, 0, 0))
    acch = pl.BlockSpec((1, 1, 1, LANES), lambda b, g, s: (b, g, 0, 0))
    return pl.pallas_call(
        body, name="ssd_bwd", grid=(B, G, nsb), in_specs=[xw, bsp, csp, xw, dts, xw, xw, sts, hv, hv, wv, wv],
        out_specs=[xw, pl.BlockSpec((2, 1, LB, N), lambda b, g, s: (0, b, rev(s), g)), xw, dts, accw, accw, acch, acch],
        out_shape=[jax.ShapeDtypeStruct((B, L, DI), F32), jax.ShapeDtypeStruct((2, B, L, G * N), F32),
                   jax.ShapeDtypeStruct((B, L, DI), BF16),
                   jax.ShapeDtypeStruct((B, G, NC, C, LANES), F32), jax.ShapeDtypeStruct((B, G, 1, W), F32),
                   jax.ShapeDtypeStruct((B, G, 1, W), F32), jax.ShapeDtypeStruct((B, G, 1, LANES), F32),
                   jax.ShapeDtypeStruct((B, G, 1, LANES), F32)],
        scratch_shapes=[pltpu.VMEM((N, W), F32), pltpu.VMEM((LANES, C), F32), pltpu.VMEM((C, W), F32)],
        compiler_params=_cparams("parallel", "parallel", "arbitrary"))(
            xbca, xbca, xbca, zx, dtc, ypre, dyn, st, bias, Aneg, Dexp, nw)


def _adamw(w, g, m, v, name):
    shape = w.shape
    n = w.size
    cols = shape[-1]
    rows = n // cols
    tr = rows
    for cand in (512, 256, 128, 64, 32, 16, 8):
        if rows % cand == 0 and cand * cols * 4 <= 1024 * 1024:
            tr = cand
            break
    c1 = 1.0 / (1.0 - ADAM_B1 ** ADAM_STEP)
    c2 = 1.0 / (1.0 - ADAM_B2 ** ADAM_STEP)

    def body(w_ref, g_ref, m_ref, v_ref, d_ref, mo_ref, vo_ref):
        g_ = g_ref[...]
        mn = ADAM_B1 * m_ref[...] + (1.0 - ADAM_B1) * g_
        vn = ADAM_B2 * v_ref[...] + (1.0 - ADAM_B2) * (g_ * g_)
        d_ref[...] = -ADAM_LR * ((mn * c1) / (jnp.sqrt(vn * c2) + ADAM_EPS) + ADAM_WD * w_ref[...])
        mo_ref[...] = mn
        vo_ref[...] = vn

    spec = pl.BlockSpec((tr, cols), lambda i: (i, 0))
    r2 = lambda a: a.reshape(rows, cols)
    outs = pl.pallas_call(
        body, name=name, grid=(rows // tr,), in_specs=[spec] * 4, out_specs=[spec] * 3,
        out_shape=[jax.ShapeDtypeStruct((rows, cols), F32)] * 3,
        compiler_params=_cparams("parallel"))(r2(w), r2(g), r2(m), r2(v))
    return tuple(o.reshape(shape) for o in outs)


def _lower_bounds(lb_logits):
    p = jax.nn.softmax(lb_logits.astype(F32), axis=0)
    return jnp.cumsum(p, axis=0) - p[0]


def _pad_cols(a, n):
    return a if a.shape[-1] == n else jnp.pad(a, [(0, 0)] * (a.ndim - 1) + [(0, n - a.shape[-1])])


def _heads_to_lanes(a, G, hpg):
    return _pad_cols(a.reshape(G, 1, hpg), LANES)


def _local_step(x, target, P):
    B, L, D = x.shape
    T = B * L
    depth = P["mix_norm"].shape[0]
    H = D // HGRN_DK
    F_ = H * HGRN_DK
    DI = P["m_w_out"].shape[1]
    G, N = M_GROUPS, M_D_STATE
    MH = DI // M_HEADDIM
    hpg = MH // G
    assert hpg <= 8
    W = hpg * M_HEADDIM
    CD = DI + 2 * G * N
    MIN = DI + CD + MH
    MPAD = -(-MIN // LANES) * LANES
    dff = P["f_w_down"].shape[1]
    NC = L // M_CHUNK
    lbs = _lower_bounds(P["hgrn_lb_logits"])

    h = x.reshape(T, D)
    saved = []
    for i in range(depth):
        j = i // 2
        s = {"h_in": h}
        u = _rmsnorm_fwd(h, P["mix_norm"][i], "mix_norm_fwd")
        s["u"] = u
        if i % 2 == 0:
            proj = _matmul(u, P["hgrn_w_in"], b_layer=j, name="hgrn_in_fwd").reshape(B, L, 4 * F_)
            o, on, st = _hgrn_fwd(proj, lbs[j].reshape(1, F_), P["hgrn_gnorm"][j].reshape(1, HGRN_DK), H)
            h = _matmul(on.reshape(T, F_), P["hgrn_w_out"], b_layer=j, res=h, name="hgrn_out_fwd")
            s.update(proj=proj, o=o, on=on, st=st)
        else:
            zx = _matmul(u, P["m_w_in_t"], b_layer=j, tb=True, name="m_in_fwd").reshape(B, L, MPAD)
            xbca = _mconv_fwd(zx, P["m_conv_w"][j], P["m_conv_b"][j], DI, CD)
            dtr = zx[:, :, DI + CD:DI + CD + MH].reshape(B, NC, M_CHUNK, G, hpg).transpose(0, 3, 1, 2, 4)
            dtc = _pad_cols(dtr, LANES)
            bias = _heads_to_lanes(P["m_dt_bias"][j], G, hpg)
            Aneg = _heads_to_lanes(-jnp.exp(P["m_A_log"][j]), G, hpg)
            Dexp = jnp.repeat(P["m_D"][j], M_HEADDIM).reshape(1, DI)
            nw = P["m_norm"][j].reshape(1, DI)
            ypre, yn, st = _ssd_fwd(xbca, zx, dtc, bias, Aneg, Dexp, nw, hpg)
            h = _matmul(yn.reshape(T, DI), P["m_w_out"], b_layer=j, res=h, name="m_out_fwd")
            s.update(zx=zx, xbca=xbca, dtc=dtc, bias=bias, Aneg=Aneg, Dexp=Dexp, nw=nw, ypre=ypre, yn=yn, st=st)
        s["h_mid"] = h
        u2 = _rmsnorm_fwd(h, P["ffn_norm"][i], "ffn_norm_fwd")
        up = _matmul(u2, P["f_w_up"], b_layer=i, name="ffn_up_fwd").reshape(B, L, 2 * dff)
        act = _ffn_mid_fwd(up, P["f_conv_w"][i], P["f_conv_b"][i], dff)
        h = _matmul(act.reshape(T, dff), P["f_w_down"], b_layer=i, res=h, name="ffn_down_fwd")
        s.update(u2=u2, up=up, act=act)
        saved.append(s)

    loss, dh, d_final = _loss_head(h, P["final_norm"], target.reshape(T, D))

    g = {k: [None] * P[k].shape[0] for k in ("mix_norm", "ffn_norm", "hgrn_gnorm", "m_conv_w", "m_conv_b", "m_dt_bias",
                                              "m_A_log", "m_D", "m_norm", "f_conv_w", "f_conv_b")}
    gm = {k: lax.empty(P[k].shape, BF16) for k in ("hgrn_w_in", "hgrn_w_out", "m_w_in_t", "m_w_out", "f_w_up", "f_w_down")}

    def dw(kind, a, b, layer, name, **kw):
        gm[kind] = _matmul(a, b, ta=True, out_dtype=BF16, out=gm[kind], out_layer=layer, name=name, **kw)

    dlbs = [None] * lbs.shape[0]
    for i in reversed(range(depth)):
        j = i // 2
        s = saved[i]
        dact = _matmul(dh, P["f_w_down"], b_layer=i, tb=True, name="ffn_down_dx").reshape(B, L, dff)
        dw("f_w_down", s["act"].reshape(T, dff), dh, i, "ffn_down_dw")
        dup, dcw, dcb = _ffn_mid_bwd(s["up"], dact, P["f_conv_w"][i], P["f_conv_b"][i], dff)
        g["f_conv_w"][i] = jnp.sum(dcw, axis=0)
        g["f_conv_b"][i] = jnp.sum(dcb, axis=(0, 1))
        dup = dup.reshape(2, T, dff)
        dw("f_w_up", s["u2"], dup, i, "ffn_up_dw", b_parts=True)
        du2 = _matmul(dup, P["f_w_up"], a_parts=True, b_layer=i, tb=True, name="ffn_up_dx")
        dh, g["ffn_norm"][i] = _rmsnorm_bwd(s["h_mid"], P["ffn_norm"][i], du2, dh, "ffn_norm_bwd")
        if i % 2 == 0:
            don = _matmul(dh, P["hgrn_w_out"], b_layer=j, tb=True, name="hgrn_out_dx").reshape(B, L, F_)
            dw("hgrn_w_out", s["on"].reshape(T, F_), dh, j, "hgrn_out_dw")
            dproj, dlb, dgn = _hgrn_bwd(s["proj"], s["o"], don, s["st"], lbs[j].reshape(1, F_),
                                        P["hgrn_gnorm"][j].reshape(1, HGRN_DK), H)
            dlbs[j] = jnp.sum(dlb, axis=(0, 1))
            g["hgrn_gnorm"][j] = jnp.sum(dgn, axis=(0, 1, 2))
            dproj = dproj.reshape(4, T, F_)
            dw("hgrn_w_in", s["u"], dproj, j, "hgrn_in_dw", b_parts=True)
            du = _matmul(dproj, P["hgrn_w_in"], a_parts=True, b_layer=j, tb=True, name="hgrn_in_dx")
        else:
            dyn = _matmul(dh, P["m_w_out"], b_layer=j, tb=True, name="m_out_dx").reshape(B, L, DI)
            dw("m_w_out", s["yn"].reshape(T, DI), dh, j, "m_out_dw")
            dxs, dbc, dz, ddt, dnw, dD, dA, dbias = _ssd_bwd(s["xbca"], s["zx"], s["dtc"], s["ypre"], dyn, s["st"],
                                                             s["bias"], s["Aneg"], s["Dexp"], s["nw"], hpg)
            g["m_norm"][j] = jnp.sum(dnw, axis=(0, 2)).reshape(DI)
            g["m_D"][j] = jnp.sum(dD, axis=(0, 2)).reshape(MH, M_HEADDIM).sum(axis=-1)
            g["m_A_log"][j] = jnp.sum(dA, axis=(0, 2))[:, :hpg].reshape(MH)
            g["m_dt_bias"][j] = jnp.sum(dbias, axis=(0, 2))[:, :hpg].reshape(MH)
            cw, cb = P["m_conv_w"][j], P["m_conv_b"][j]
            dxx, dcw_x, dcb_x = _mconv_bwd(s["zx"], dxs[None], cw, cb, DI, 0, "mconv_bwd_x")
            dxb, dcw_b, dcb_b = _mconv_bwd(s["zx"], dbc, cw, cb, DI, DI, "mconv_bwd_bc")
            g["m_conv_w"][j] = jnp.concatenate([jnp.sum(dcw_x, axis=0), jnp.sum(dcw_b, axis=0)], axis=-1)
            g["m_conv_b"][j] = jnp.concatenate([jnp.sum(dcb_x, axis=(0, 1)), jnp.sum(dcb_b, axis=(0, 1))], axis=-1)
            ddt_t = _pad_cols(ddt[..., :hpg].transpose(0, 2, 3, 1, 4).reshape(T, MH), MPAD - DI - CD).astype(BF16)
            pieces = [(dz.reshape(T, DI), 0), (dxx.reshape(T, DI), DI), (dxb.reshape(T, 2 * G * N), 2 * DI), (ddt_t, DI + CD)]
            du = None
            for n_, (piece, off) in enumerate(pieces):
                gm["m_w_in_t"] = _matmul(piece, s["u"], ta=True, out_dtype=BF16, out=gm["m_w_in_t"], out_layer=j,
                                         out_off=off, name="m_in_dw%d" % n_)
                du = _matmul(piece, P["m_w_in_t"], b_layer=j, b_off=off, res=du, name="m_in_dx%d" % n_)
        dh, g["mix_norm"][i] = _rmsnorm_bwd(s["h_in"], P["mix_norm"][i], du, dh, "mix_norm_bwd")

    grads = {k: jnp.stack(vs) for k, vs in g.items()}
    grads["final_norm"] = d_final
    _, lb_vjp = jax.vjp(_lower_bounds, P["hgrn_lb_logits"])
    grads["hgrn_lb_logits"] = lb_vjp(jnp.stack(dlbs))[0]
    return loss, dh.reshape(B, L, D), grads, gm


ANY = pl.BlockSpec(memory_space=pl.ANY)
N_CHIPS = 4
N_DEV = 8


def _place():
    x, y, c = lax.axis_index("x"), lax.axis_index("y"), lax.axis_index("c")
    sibling = (x, y, 1 - c)
    chips = [(1 - x, y), (x, 1 - y), (1 - x, 1 - y)]
    return x, y, c, sibling, chips


def _remote(src, dst, send_sem, recv_sem, to):
    return pltpu.make_async_remote_copy(src_ref=src, dst_ref=dst, send_sem=send_sem, recv_sem=recv_sem, device_id=to,
                                        device_id_type=MESH)


KIND_AXIS = {"hgrn_w_in": "col", "f_w_up": "col", "m_w_in_t": "row", "hgrn_w_out": "row", "m_w_out": "row", "f_w_down": "row"}
KINDS = tuple(KIND_AXIS)
PEER_MASKS = (2, 1, 3)
ALL = slice(None)


def _chip_win(axis, cw, s):
    return (ALL, slice(s * cw, (s + 1) * cw)) if axis == "col" else (slice(s * cw, (s + 1) * cw), ALL)


def _half_win(axis, rows, cols, h):
    return (slice(h * rows // 2, (h + 1) * rows // 2), ALL) if axis == "col" else (ALL, slice(h * cols // 2, (h + 1) * cols // 2))


def _per_place(fn):
    x, y, c, sibling, chips = _place()
    chip = 2 * x + y
    for s in range(N_CHIPS):
        for cc in range(2):
            @pl.when(jnp.logical_and(chip == s, c == cc))
            def _():
                fn(s, cc, c, sibling, chips)


def _gather_weights(shards, pad_rows):
    full_shapes, cws = {}, {}
    for k in KINDS:
        lay, r, c_ = shards[k].shape
        cws[k] = c_ if KIND_AXIS[k] == "col" else r
        full_shapes[k] = (lay, r, N_CHIPS * c_) if KIND_AXIS[k] == "col" else (lay, N_CHIPS * r + pad_rows.get(k, 0), c_)
    padded = [k for k in KINDS if pad_rows.get(k, 0)]
    zeros = [jnp.zeros((shards[k].shape[0], pad_rows[k], shards[k].shape[2]), BF16) for k in padded]
    nq = len(KINDS)

    def body(*refs):
        sh = dict(zip(KINDS, refs[:nq]))
        zr = dict(zip(padded, refs[nq:nq + len(padded)]))
        full = dict(zip(KINDS, refs[nq + len(padded):2 * nq + len(padded)]))
        send_sems, recv_sems, local_sems = refs[2 * nq + len(padded):]

        def run(s, cc, c, sibling, chips):
            local, first, passed = [], [], []
            for q, k in enumerate(KINDS):
                ax, cw = KIND_AXIS[k], cws[k]
                lay = sh[k].shape[0]
                cp = pltpu.make_async_copy(sh[k], full[k].at[(ALL,) + _chip_win(ax, cw, s)], local_sems.at[q])
                cp.start()
                local.append(cp)
                if k in zr:
                    r0 = N_CHIPS * cw
                    cp = pltpu.make_async_copy(zr[k], full[k].at[:, r0:r0 + pad_rows[k], :], local_sems.at[nq + padded.index(k)])
                    cp.start()
                    local.append(cp)
                ls = slice(cc * lay // 2, (cc + 1) * lay // 2)
                for j, (px, py) in enumerate(chips):
                    cp = _remote(sh[k].at[ls], full[k].at[(ls,) + _chip_win(ax, cw, s)], send_sems.at[6 * q + j],
                                 recv_sems.at[6 * q + j], (px, py, c))
                    cp.start()
                    first.append(cp)
            for q, k in enumerate(KINDS):
                ax, cw = KIND_AXIS[k], cws[k]
                lay = sh[k].shape[0]
                ls = slice(cc * lay // 2, (cc + 1) * lay // 2)
                for j, (px, py) in enumerate(chips):
                    blk = full[k].at[(ls,) + _chip_win(ax, cw, s ^ PEER_MASKS[j])]
                    _remote(sh[k].at[ls], blk, send_sems.at[6 * q + j], recv_sems.at[6 * q + j], (px, py, c)).wait_recv()
                    cp = _remote(blk, blk, send_sems.at[6 * q + 3 + j], recv_sems.at[6 * q + 3 + j], sibling)
                    cp.start()
                    passed.append(cp)
            for q, k in enumerate(KINDS):
                ax, cw = KIND_AXIS[k], cws[k]
                lay = sh[k].shape[0]
                lo = slice((1 - cc) * lay // 2, (2 - cc) * lay // 2)
                for j in range(3):
                    blk = full[k].at[(lo,) + _chip_win(ax, cw, s ^ PEER_MASKS[j])]
                    _remote(blk, blk, send_sems.at[6 * q + 3 + j], recv_sems.at[6 * q + 3 + j], sibling).wait_recv()
            for cp in first + passed:
                cp.wait_send()
            for cp in local:
                cp.wait()

        _per_place(run)

    outs = pl.pallas_call(
        body, name="gather_weights", in_specs=[ANY] * (nq + len(padded)), out_specs=[ANY] * nq,
        out_shape=[jax.ShapeDtypeStruct(full_shapes[k], BF16) for k in KINDS],
        scratch_shapes=[pltpu.SemaphoreType.DMA((6 * nq,)), pltpu.SemaphoreType.DMA((6 * nq,)),
                        pltpu.SemaphoreType.DMA((nq + len(padded),))],
    )(*[shards[k] for k in KINDS], *zeros)
    return dict(zip(KINDS, outs))


def _swap_halves(gm):
    nq = len(KINDS)
    half_shapes = {}
    for k in KINDS:
        lay, r, c_ = gm[k].shape
        half_shapes[k] = (lay, r // 2, c_) if KIND_AXIS[k] == "col" else (lay, r, c_ // 2)

    def body(*refs):
        g = dict(zip(KINDS, refs[:nq]))
        ra = dict(zip(KINDS, refs[nq:2 * nq]))
        send_sems, recv_sems = refs[2 * nq:]

        def run(s, cc, c, sibling, chips):
            cps = []
            for q, k in enumerate(KINDS):
                _, r, c_ = g[k].shape
                src = g[k].at[(ALL,) + _half_win(KIND_AXIS[k], r, c_, 1 - cc)]
                cps.append(_remote(src, ra[k], send_sems.at[q], recv_sems.at[q], sibling))
            for cp in cps:
                cp.start()
            for cp in cps:
                cp.wait()

        _per_place(run)

    outs = pl.pallas_call(
        body, name="swap_halves", in_specs=[ANY] * nq, out_specs=[ANY] * nq,
        out_shape=[jax.ShapeDtypeStruct(half_shapes[k], BF16) for k in KINDS],
        scratch_shapes=[pltpu.SemaphoreType.DMA((nq,)), pltpu.SemaphoreType.DMA((nq,))],
    )(*[gm[k] for k in KINDS])
    return dict(zip(KINDS, outs))


def _scatter_partials(pa, cws):
    nq = len(KINDS)
    win_shapes = {}
    for k in KINDS:
        lay, r, c_ = pa[k].shape
        win_shapes[k] = (3, lay, r, cws[k]) if KIND_AXIS[k] == "col" else (3, lay, cws[k], c_)

    def body(*refs):
        p = dict(zip(KINDS, refs[:nq]))
        rb = dict(zip(KINDS, refs[nq:2 * nq]))
        send_sems, recv_sems = refs[2 * nq:]

        def run(s, cc, c, sibling, chips):
            cps = []
            for q, k in enumerate(KINDS):
                for j, (px, py) in enumerate(chips):
                    src = p[k].at[(ALL,) + _chip_win(KIND_AXIS[k], cws[k], s ^ PEER_MASKS[j])]
                    cps.append(_remote(src, rb[k].at[j], send_sems.at[3 * q + j], recv_sems.at[3 * q + j], (px, py, c)))
            for cp in cps:
                cp.start()
            for cp in cps:
                cp.wait()

        _per_place(run)

    outs = pl.pallas_call(
        body, name="scatter_partials", in_specs=[ANY] * nq, out_specs=[ANY] * nq,
        out_shape=[jax.ShapeDtypeStruct(win_shapes[k], BF16) for k in KINDS],
        scratch_shapes=[pltpu.SemaphoreType.DMA((3 * nq,)), pltpu.SemaphoreType.DMA((3 * nq,))],
    )(*[pa[k] for k in KINDS])
    return dict(zip(KINDS, outs))


def _share_halves(gh):
    nq = len(KINDS)
    shard_shapes = {}
    for k in KINDS:
        lay, r, c_ = gh[k].shape
        shard_shapes[k] = (lay, 2 * r, c_) if KIND_AXIS[k] == "col" else (lay, r, 2 * c_)

    def body(*refs):
        g = dict(zip(KINDS, refs[:nq]))
        out = dict(zip(KINDS, refs[nq:2 * nq]))
        send_sems, recv_sems, local_sems = refs[2 * nq:]

        def run(s, cc, c, sibling, chips):
            cps, local = [], []
            for q, k in enumerate(KINDS):
                _, r, c_ = shard_shapes[k]
                dst = out[k].at[(ALL,) + _half_win(KIND_AXIS[k], r, c_, cc)]
                local.append(pltpu.make_async_copy(g[k], dst, local_sems.at[q]))
                cps.append(_remote(g[k], dst, send_sems.at[q], recv_sems.at[q], sibling))
            for cp in local + cps:
                cp.start()
            for cp in cps + local:
                cp.wait()

        _per_place(run)

    outs = pl.pallas_call(
        body, name="share_halves", in_specs=[ANY] * nq, out_specs=[ANY] * nq,
        out_shape=[jax.ShapeDtypeStruct(shard_shapes[k], F32) for k in KINDS],
        scratch_shapes=[pltpu.SemaphoreType.DMA((nq,)), pltpu.SemaphoreType.DMA((nq,)), pltpu.SemaphoreType.DMA((nq,))],
    )(*[gh[k] for k in KINDS])
    return dict(zip(KINDS, outs))


def _all_gather_small(xs, name):
    m_per, n = xs.shape

    def body(x_ref, out_ref, send_sems, recv_sems, local_sem):
        x, y, c, sibling, chips = _place()
        me = (x, y, c)

        def rows(px, py, pc):
            return out_ref.at[pl.ds((4 * px + 2 * py + pc) * m_per, m_per), :]

        def copy(k, block, to, src=None):
            return _remote(rows(*block) if src is None else src, rows(*block), send_sems.at[k], recv_sems.at[k], to)

        mine = pltpu.make_async_copy(x_ref, rows(*me), local_sem)
        mine.start()
        first = [copy(0, me, sibling, src=x_ref)]
        first += [copy(1 + j, me, (*chip, c), src=x_ref) for j, chip in enumerate(chips)]
        for cp in first:
            cp.start()
        passed = [copy(4 + j, (*chip, c), sibling) for j, chip in enumerate(chips)]
        for j, chip in enumerate(chips):
            copy(1 + j, (*chip, c), me).wait_recv()
            passed[j].start()
        copy(0, sibling, me).wait_recv()
        for j, chip in enumerate(chips):
            copy(4 + j, (*chip, 1 - c), me).wait_recv()
        for cp in first + passed:
            cp.wait_send()
        mine.wait()

    vm = pl.BlockSpec(memory_space=pltpu.VMEM)
    return pl.pallas_call(
        body, name=name, in_specs=[vm], out_specs=vm, out_shape=jax.ShapeDtypeStruct((N_DEV * m_per, n), xs.dtype),
        scratch_shapes=[pltpu.SemaphoreType.DMA((7,)), pltpu.SemaphoreType.DMA((7,)), pltpu.SemaphoreType.DMA],
        compiler_params=pltpu.CompilerParams(vmem_limit_bytes=VMEM_LIMIT_BYTES),
    )(xs)


def _row_tile(rows, cap=256):
    for mult in (16, 8):
        best = None
        t = mult
        while t <= min(rows, cap):
            if rows % t == 0:
                best = t
            t += mult
        if best is not None:
            return best
    raise ValueError(rows)


def _add_sibling(kind, g, ra, core):
    lay, R, C = ra.shape
    axis = KIND_AXIS[kind]
    tr, tc = _row_tile(R), _pick_tile(C, 2048)
    nr, nc = R // tr, C // tc

    def body(c_ref, a_ref, b_ref, o_ref):
        o_ref[...] = (a_ref[...].astype(F32) + b_ref[...].astype(F32)).astype(o_ref.dtype)

    if axis == "col":
        own = pl.BlockSpec((None, tr, tc), lambda l, i, j, c_ref: (l, c_ref[0] * nr + i, j))
    else:
        own = pl.BlockSpec((None, tr, tc), lambda l, i, j, c_ref: (l, i, c_ref[0] * nc + j))
    same = pl.BlockSpec((None, tr, tc), lambda l, i, j, c_ref: (l, i, j))
    grid_spec = pltpu.PrefetchScalarGridSpec(num_scalar_prefetch=1, grid=(lay, nr, nc), in_specs=[own, same], out_specs=same)
    return pl.pallas_call(
        body, name="add_sibling_" + kind, grid_spec=grid_spec, out_shape=jax.ShapeDtypeStruct(ra.shape, BF16),
        compiler_params=_cparams("parallel", "parallel", "parallel"))(core.reshape(1).astype(jnp.int32), g, ra)


def _sum_chips(kind, pa, rb, chip):
    _, lay, R, C = rb.shape
    axis = KIND_AXIS[kind]
    tr, tc = _row_tile(R), _pick_tile(C, 2048)
    nr, nc = R // tr, C // tc

    def body(s_ref, a_ref, b0_ref, b1_ref, b2_ref, o_ref):
        o_ref[...] = ((a_ref[...].astype(F32) + b0_ref[...].astype(F32)) + b1_ref[...].astype(F32)) + b2_ref[...].astype(F32)

    def rb_spec(n):
        return pl.BlockSpec((None, None, tr, tc), lambda l, i, j, s_ref: (n, l, i, j))

    if axis == "col":
        own = pl.BlockSpec((None, tr, tc), lambda l, i, j, s_ref: (l, i, s_ref[0] * nc + j))
    else:
        own = pl.BlockSpec((None, tr, tc), lambda l, i, j, s_ref: (l, s_ref[0] * nr + i, j))
    grid_spec = pltpu.PrefetchScalarGridSpec(
        num_scalar_prefetch=1, grid=(lay, nr, nc), in_specs=[own, rb_spec(0), rb_spec(1), rb_spec(2)],
        out_specs=pl.BlockSpec((None, tr, tc), lambda l, i, j, s_ref: (l, i, j)))
    return pl.pallas_call(
        body, name="sum_chips_" + kind, grid_spec=grid_spec, out_shape=jax.ShapeDtypeStruct((lay, R, C), F32),
        compiler_params=_cparams("parallel", "parallel", "parallel"))(chip.reshape(1).astype(jnp.int32), pa, rb, rb, rb)


def _sum_devices(gathered):
    M = gathered.shape[0] // N_DEV
    C = gathered.shape[1]

    def body(g_ref, o_ref):
        acc = g_ref[0:M, :]
        for d in range(1, N_DEV):
            acc = acc + g_ref[d * M:(d + 1) * M, :]
        o_ref[...] = acc

    vm = pl.BlockSpec(memory_space=pltpu.VMEM)
    return pl.pallas_call(body, name="sum_devices", in_specs=[vm], out_specs=vm, out_shape=jax.ShapeDtypeStruct((M, C), F32),
                          compiler_params=pltpu.CompilerParams(vmem_limit_bytes=VMEM_LIMIT_BYTES))(gathered)


WEIGHTS = ["mix_norm", "ffn_norm", "final_norm", "hgrn_w_in", "hgrn_lb_logits", "hgrn_gnorm", "hgrn_w_out", "m_w_in",
           "m_conv_w", "m_conv_b", "m_dt_bias", "m_A_log", "m_D", "m_norm", "m_w_out", "f_w_up", "f_conv_w", "f_conv_b",
           "f_w_down"]
BIG_COLS = ("hgrn_w_in", "m_w_in", "f_w_up")
BIG_ROWS = ("hgrn_w_out", "m_w_out", "f_w_down")
BIG = BIG_COLS + BIG_ROWS
SMALL_SHARDED = ("m_conv_w", "m_conv_b", "m_norm", "f_conv_w")
SMALL_REPLICATED = ("mix_norm", "ffn_norm", "final_norm", "hgrn_lb_logits", "hgrn_gnorm", "m_dt_bias", "m_A_log", "m_D",
                    "f_conv_b")
SMALL = SMALL_REPLICATED + SMALL_SHARDED


def _pack_rows(arrs, row_mult=8):
    flat = jnp.concatenate([a.reshape(-1).astype(F32) for a in arrs])
    unit = FLAT_COLS * row_mult
    n = -(-flat.size // unit) * unit
    return jnp.pad(flat, (0, n - flat.size)).reshape(-1, FLAT_COLS)


def _unpack_rows(buf, shapes):
    flat = buf.reshape(-1)
    out, off = [], 0
    for shp in shapes:
        n = math.prod(shp)
        out.append(flat[off:off + n].reshape(shp))
        off += n
    return out


def _halves(a):
    return a.reshape(2, -1)


def kernel(x, mix_norm, ffn_norm, final_norm, hgrn_w_in, hgrn_lb_logits, hgrn_gnorm, hgrn_w_out, m_w_in, m_conv_w, m_conv_b, m_dt_bias, m_A_log, m_D, m_norm, m_w_out, f_w_up, f_conv_w, f_conv_b, f_w_down, loss_target, m_mix_norm, m_ffn_norm, m_final_norm, m_hgrn_w_in, m_hgrn_lb_logits, m_hgrn_gnorm, m_hgrn_w_out, m_m_w_in, m_m_conv_w, m_m_conv_b, m_m_dt_bias, m_m_A_log, m_m_D, m_m_norm, m_m_w_out, m_f_w_up, m_f_conv_w, m_f_conv_b, m_f_w_down, v_mix_norm, v_ffn_norm, v_final_norm, v_hgrn_w_in, v_hgrn_lb_logits, v_hgrn_gnorm, v_hgrn_w_out, v_m_w_in, v_m_conv_w, v_m_conv_b, v_m_dt_bias, v_m_A_log, v_m_D, v_m_norm, v_m_w_out, v_f_w_up, v_f_conv_w, v_f_conv_b, v_f_w_down):
    given = dict(locals())
    w = {n: given[n] for n in WEIGHTS}
    mom1 = {n: given["m_" + n] for n in WEIGHTS}
    mom2 = {n: given["v_" + n] for n in WEIGHTS}
    chip = 2 * lax.axis_index("x") + lax.axis_index("y")
    core = lax.axis_index("c")

    shards = {k: w[k].astype(BF16) for k in KINDS if k != "m_w_in_t"}
    shards["m_w_in_t"] = w["m_w_in"].transpose(0, 2, 1).astype(BF16)
    m_in = N_CHIPS * w["m_w_in"].shape[2]
    P = _gather_weights(shards, {"m_w_in_t": -(-m_in // LANES) * LANES - m_in})
    cws = {k: shards[k].shape[2] if KIND_AXIS[k] == "col" else shards[k].shape[1] for k in KINDS}
    own = _pack_rows([w[n] for n in SMALL_SHARDED])
    all_small = _all_gather_small(own, "gather_small_params").reshape(N_CHIPS, 2, -1)[:, 0]
    per_chip = [_unpack_rows(all_small[s], [w[n].shape for n in SMALL_SHARDED]) for s in range(N_CHIPS)]
    for i, n in enumerate(SMALL_SHARDED):
        P[n] = jnp.concatenate([per_chip[s][i] for s in range(N_CHIPS)], axis=-1)
    for n in SMALL_REPLICATED:
        P[n] = w[n]

    loss_part, grad_x, g_full, gm = _local_step(x, loss_target, P)

    ra = _swap_halves(gm)
    pa = {k: _add_sibling(k, gm[k], ra[k], core) for k in KINDS}
    rb = _scatter_partials(pa, cws)
    gh = {k: _sum_chips(k, pa[k], rb[k], chip) for k in KINDS}
    g_sh = _share_halves(gh)
    grads = {k: g_sh[k] for k in KINDS if k != "m_w_in_t"}
    grads["m_w_in"] = g_sh["m_w_in_t"].transpose(0, 2, 1)

    small_shapes = [g_full[n].shape for n in SMALL] + [(1,)]
    packed = _pack_rows([g_full[n] for n in SMALL] + [loss_part[0, 0:1]])
    summed = _sum_devices(_all_gather_small(packed, "gather_small_grads"))
    small = _unpack_rows(summed, small_shapes)
    loss = small[-1][0]
    for n, gs in zip(SMALL, small[:-1]):
        if n in SMALL_SHARDED:
            width = w[n].shape[-1]
            gs = lax.dynamic_slice_in_dim(gs, chip * width, width, axis=gs.ndim - 1)
        grads[n] = gs

    delta, new_m, new_v = {}, {}, {}
    for n in BIG:
        delta[n], new_m[n], new_v[n] = _adamw(w[n], grads[n], mom1[n], mom2[n], "adamw_" + n)
    shapes = [w[n].shape for n in SMALL]
    ds, ms, vs = _adamw(_pack_rows([w[n] for n in SMALL]), _pack_rows([grads[n] for n in SMALL]),
                        _pack_rows([mom1[n] for n in SMALL]), _pack_rows([mom2[n] for n in SMALL]), "adamw_small")
    for n, d_, m_, v_ in zip(SMALL, _unpack_rows(ds, shapes), _unpack_rows(ms, shapes), _unpack_rows(vs, shapes)):
        delta[n], new_m[n], new_v[n] = d_, m_, v_

    return (loss, grad_x, *[grads[n] for n in WEIGHTS], *[delta[n] for n in WEIGHTS], *[new_m[n] for n in WEIGHTS],
            *[new_v[n] for n in WEIGHTS])
```

```python
import functools
import math

import jax
import jax.numpy as jnp
from jax import lax
from jax.experimental import pallas as pl
from jax.experimental.pallas import tpu as pltpu

F32 = jnp.float32
BF16 = jnp.bfloat16
NORM_EPS = 1e-5
HGRN_DK = 128
HGRN_CHUNK = 64
M_HEADDIM = 64
M_GROUPS = 8
M_D_STATE = 128
M_CONV = 4
M_CHUNK = 128
FFN_CONV = 3
EXP_CLIP = 80.0
LANES = 128
VMEM_LIMIT_BYTES = 56 * 1024 * 1024
FLAT_COLS = 1024
ADAM_LR, ADAM_B1, ADAM_B2, ADAM_EPS, ADAM_WD, ADAM_STEP = 0.001, 0.9, 0.999, 1e-08, 0.01, 10
MESH = pl.DeviceIdType.MESH

NN = (((1,), (0,)), ((), ()))
NT = (((1,), (1,)), ((), ()))
TN = (((0,), (0,)), ((), ()))


def _cparams(*sems):
    return pltpu.CompilerParams(dimension_semantics=sems, vmem_limit_bytes=VMEM_LIMIT_BYTES)


def _dot(a, b, dn=NN):
    return lax.dot_general(a.astype(BF16), b.astype(BF16), dn, preferred_element_type=F32)


def _dot_exact(x, m, dn=NN, passes=3, x_first=True):
    acc = None
    r = x
    for _ in range(passes):
        p = r.astype(BF16)
        r = r - p.astype(F32)
        t = lax.dot_general(p, m, dn, preferred_element_type=F32) if x_first else lax.dot_general(m, p, dn, preferred_element_type=F32)
        acc = t if acc is None else acc + t
    return acc


def _iota(shape, dim):
    return lax.broadcasted_iota(jnp.int32, shape, dim)


def _cumsum_rows(x, reverse=False):
    n = x.shape[0]
    row = _iota(x.shape, 0)
    s = 1
    while s < n:
        if reverse:
            x = x + jnp.where(row < n - s, pltpu.roll(x, n - s, 0), 0.0)
        else:
            x = x + jnp.where(row >= s, pltpu.roll(x, s, 0), 0.0)
        s *= 2
    return x


def _silu(x):
    return x * jax.nn.sigmoid(x)


def _dsilu(x):
    s = jax.nn.sigmoid(x)
    return s * (1.0 + x * (1.0 - s))


def _pick_tile(dim, pref):
    if dim <= pref:
        return dim
    best = None
    t = LANES
    while t <= pref:
        if dim % t == 0:
            best = t
        t += LANES
    assert best is not None, (dim, pref)
    return best


def _matmul(a, b, *, ta=False, tb=False, res=None, out_dtype=F32, tm=1024, tn=1024, tk=2048, name,
            a_parts=False, b_parts=False, b_layer=None, b_off=0, out=None, out_layer=None, out_off=0):
    a = a.astype(BF16)
    b = b.astype(BF16)
    if a_parts:
        assert not ta
        pa, M, kp = a.shape
        K = pa * kp
    else:
        M, K = (a.shape[1], a.shape[0]) if ta else a.shape
    bsh = b.shape[1:] if b_layer is not None else b.shape
    if b_parts:
        assert not tb
        pb, _, np_ = bsh
        N = pb * np_
    else:
        N = bsh[0] if tb else bsh[1]
    tm, tn, tk = _pick_tile(M, tm), _pick_tile(np_ if b_parts else N, tn), _pick_tile(kp if a_parts else K, tk)
    nk = K // tk
    dn = (((0 if ta else 1,), (1 if tb else 0,)), ((), ()))
    assert b_off % tk == 0 and out_off % tm == 0

    def body(*refs):
        refs = list(refs)
        acc = refs.pop() if nk > 1 else None
        o_ref = refs.pop()
        if out is not None:
            refs.pop()
        a_ref, b_ref = refs[0], refs[1]
        r_ref = refs[2] if res is not None else None
        k = pl.program_id(2)

        def prod():
            return lax.dot_general(a_ref[...], b_ref[...], dn, preferred_element_type=F32)

        def finish(r):
            if res is not None:
                r = r + r_ref[...]
            o_ref[...] = r.astype(out_dtype)

        if nk == 1:
            finish(prod())
            return

        @pl.when(k == 0)
        def _():
            acc[...] = prod()

        @pl.when(jnp.logical_and(k > 0, k < nk - 1))
        def _():
            acc[...] += prod()

        @pl.when(k == nk - 1)
        def _():
            finish(acc[...] + prod())

    if a_parts:
        kpb = kp // tk
        a_spec = pl.BlockSpec((None, tm, tk), lambda i, j, k: (k // kpb, i, k % kpb))
    elif ta:
        a_spec = pl.BlockSpec((tk, tm), lambda i, j, k: (k, i))
    else:
        a_spec = pl.BlockSpec((tm, tk), lambda i, j, k: (i, k))
    lead = () if b_layer is None else (b_layer,)
    lead_blk = () if b_layer is None else (None,)
    kb0 = b_off // tk
    if b_parts:
        npb = np_ // tn
        b_spec = pl.BlockSpec(lead_blk + (None, tk, tn), lambda i, j, k: lead + (j // npb, k, j % npb))
    elif tb:
        b_spec = pl.BlockSpec(lead_blk + (tn, tk), lambda i, j, k: lead + (j, k))
    else:
        b_spec = pl.BlockSpec(lead_blk + (tk, tn), lambda i, j, k: lead + (kb0 + k, j))
    r_spec = pl.BlockSpec((tm, tn), lambda i, j, k: (i, j))
    in_specs = [a_spec, b_spec] + ([r_spec] if res is not None else [])
    args = (a, b) + ((res,) if res is not None else ())
    if out is None:
        o_spec, out_shape, aliases = r_spec, jax.ShapeDtypeStruct((M, N), out_dtype), {}
    else:
        assert out.dtype == out_dtype and out.shape[-1] == N
        olead = () if out_layer is None else (out_layer,)
        olead_blk = () if out_layer is None else (None,)
        ob0 = out_off // tm
        o_spec = pl.BlockSpec(olead_blk + (tm, tn), lambda i, j, k: olead + (ob0 + i, j))
        out_shape = jax.ShapeDtypeStruct(out.shape, out.dtype)
        aliases = {len(args): 0}
        in_specs = in_specs + [pl.BlockSpec(memory_space=pl.ANY)]
        args = args + (out,)
    return pl.pallas_call(
        body, name=name, grid=(M // tm, N // tn, nk), in_specs=in_specs, out_specs=o_spec, out_shape=out_shape,
        scratch_shapes=[pltpu.VMEM((tm, tn), F32)] if nk > 1 else [], input_output_aliases=aliases,
        compiler_params=_cparams("parallel", "parallel", "arbitrary"))(*args)


def _rmsnorm_fwd(h, w, name):
    T, D = h.shape
    tm = _pick_tile(T, 256)

    def body(h_ref, w_ref, u_ref):
        x = h_ref[...]
        r = lax.rsqrt(jnp.mean(x * x, axis=-1, keepdims=True) + NORM_EPS)
        u_ref[...] = (x * r * w_ref[...]).astype(BF16)

    return pl.pallas_call(
        body, name=name, grid=(T // tm,),
        in_specs=[pl.BlockSpec((tm, D), lambda i: (i, 0)), pl.BlockSpec((1, D), lambda i: (0, 0))],
        out_specs=pl.BlockSpec((tm, D), lambda i: (i, 0)), out_shape=jax.ShapeDtypeStruct((T, D), BF16),
        compiler_params=_cparams("parallel"))(h, w.reshape(1, D))


def _rmsnorm_bwd(h, w, du, dres, name):
    T, D = h.shape
    tm = _pick_tile(T, 256)

    def body(h_ref, w_ref, du_ref, dr_ref, dh_ref, dhb_ref, dw_ref):
        x = h_ref[...]
        r = lax.rsqrt(jnp.mean(x * x, axis=-1, keepdims=True) + NORM_EPS)
        xh = x * r
        du_ = du_ref[...]
        dy = du_ * w_ref[...]
        dh = dr_ref[...] + r * (dy - xh * jnp.mean(dy * xh, axis=-1, keepdims=True))
        dh_ref[...] = dh
        dhb_ref[...] = dh.astype(BF16)
        part = jnp.sum(du_ * xh, axis=0, keepdims=True)

        @pl.when(pl.program_id(0) == 0)
        def _():
            dw_ref[...] = part

        @pl.when(pl.program_id(0) > 0)
        def _():
            dw_ref[...] += part

    row = pl.BlockSpec((tm, D), lambda i: (i, 0))
    vec = pl.BlockSpec((1, D), lambda i: (0, 0))
    dh, dhb, dw = pl.pallas_call(
        body, name=name, grid=(T // tm,), in_specs=[row, vec, row, row], out_specs=[row, row, vec],
        out_shape=[jax.ShapeDtypeStruct((T, D), F32), jax.ShapeDtypeStruct((T, D), BF16), jax.ShapeDtypeStruct((1, D), F32)],
        compiler_params=_cparams("arbitrary"))(h, w.reshape(1, D), du, dres)
    return dh, dhb, dw.reshape(D)


def _loss_head(h, w, target):
    T, D = h.shape
    tm = _pick_tile(T, 256)

    def body(h_ref, w_ref, t_ref, loss_ref, dh_ref, dhb_ref, dw_ref):
        x = h_ref[...]
        wv = w_ref[...]
        r = lax.rsqrt(jnp.mean(x * x, axis=-1, keepdims=True) + NORM_EPS)
        xh = x * r
        e = xh * wv - t_ref[...]
        lpart = jnp.zeros((1, LANES), F32) + 0.5 * jnp.sum(jnp.mean(e * e, axis=-1, keepdims=True))
        dyo = e * (1.0 / D)
        dy = dyo * wv
        dh = r * (dy - xh * jnp.mean(dy * xh, axis=-1, keepdims=True))
        dh_ref[...] = dh
        dhb_ref[...] = dh.astype(BF16)
        part = jnp.sum(dyo * xh, axis=0, keepdims=True)

        @pl.when(pl.program_id(0) == 0)
        def _():
            dw_ref[...] = part
            loss_ref[...] = lpart

        @pl.when(pl.program_id(0) > 0)
        def _():
            dw_ref[...] += part
            loss_ref[...] += lpart

    row = pl.BlockSpec((tm, D), lambda i: (i, 0))
    vec = pl.BlockSpec((1, D), lambda i: (0, 0))
    lvec = pl.BlockSpec((1, LANES), lambda i: (0, 0))
    loss, dh, dhb, dw = pl.pallas_call(
        body, name="loss_head", grid=(T // tm,), in_specs=[row, vec, row], out_specs=[lvec, row, row, vec],
        out_shape=[jax.ShapeDtypeStruct((1, LANES), F32), jax.ShapeDtypeStruct((T, D), F32),
                   jax.ShapeDtypeStruct((T, D), BF16), jax.ShapeDtypeStruct((1, D), F32)],
        compiler_params=_cparams("arbitrary"))(h, w.reshape(1, D), target)
    return loss, dh, dhb, dw.reshape(D)


def _hgrn_gates(qr, fr, lb):
    sig = jax.nn.sigmoid(fr)
    nsig = jax.nn.sigmoid(-fr)
    fg = lb + (1.0 - lb) * sig
    logf = jnp.log(fg)
    k = (1.0 - lb) * nsig
    q = _silu(qr)
    return q, k, logf, sig, nsig, fg


def _hgrn_scaled(q, k, b, bmid):
    eq = jnp.exp(jnp.clip(b - bmid, -EXP_CLIP, EXP_CLIP))
    ek = jnp.exp(jnp.clip(bmid - b, -EXP_CLIP, EXP_CLIP))
    return q * eq, k * ek, eq, ek


def _hgrn_fwd(proj, lb, gnw, H):
    B, L, _ = proj.shape
    C, DK = HGRN_CHUNK, HGRN_DK
    F_ = H * DK
    NC = L // C

    def body(q_ref, f_ref, v_ref, g_ref, lb_ref, gn_ref, o_ref, on_ref, st_ref, ST, bsc):
        ST[...] = jnp.zeros_like(ST)
        lbv = lb_ref[...]
        gn = gn_ref[...]
        causal = _iota((C, C), 0) >= _iota((C, C), 1)

        def chunk(c, carry):
            r0 = pl.multiple_of(c * C, C)
            rows = pl.ds(r0, C)
            q, k, logf, _, _, _ = _hgrn_gates(q_ref[0, rows, :], f_ref[0, rows, :], lbv)
            v = v_ref[0, rows, :]
            b = _cumsum_rows(logf)
            bsc[...] = b
            bmid = bsc[C // 2 - 1:C // 2, :]
            blast = bsc[C - 1:C, :]
            qs, ks, _, _ = _hgrn_scaled(q, k, b, bmid)
            A = jnp.where(causal, _dot(qs, ks, NT), 0.0)
            st = ST[...]
            st_ref[0, 0, c] = st
            o = _dot(A, v) + _dot(q * jnp.exp(b), st, NT)
            kb = k * jnp.exp(blast - b)
            ST[...] = st * jnp.exp(blast) + _dot(v, kb, TN)
            rms = lax.rsqrt(jnp.mean(o * o, axis=-1, keepdims=True) + NORM_EPS)
            o_ref[0, rows, :] = o
            on_ref[0, rows, :] = (o * rms * gn * _silu(g_ref[0, rows, :])).astype(BF16)
            return carry

        lax.fori_loop(0, NC, chunk, 0)

    def col(off):
        return pl.BlockSpec((1, L, DK), lambda b, h: (b, 0, off + h))

    return pl.pallas_call(
        body, name="hgrn_fwd", grid=(B, H),
        in_specs=[col(0), col(H), col(2 * H), col(3 * H), pl.BlockSpec((1, DK), lambda b, h: (0, h)),
                  pl.BlockSpec((1, DK), lambda b, h: (0, 0))],
        out_specs=[col(0), col(0), pl.BlockSpec((1, 1, NC, DK, DK), lambda b, h: (b, h, 0, 0, 0))],
        out_shape=[jax.ShapeDtypeStruct((B, L, F_), F32), jax.ShapeDtypeStruct((B, L, F_), BF16),
                   jax.ShapeDtypeStruct((B, H, NC, DK, DK), F32)],
        scratch_shapes=[pltpu.VMEM((DK, DK), F32), pltpu.VMEM((C, DK), F32)],
        compiler_params=_cparams("parallel", "parallel"))(proj, proj, proj, proj, lb, gnw)


def _hgrn_bwd(proj, o, don, st, lb, gnw, H):
    B, L, _ = proj.shape
    C, DK = HGRN_CHUNK, HGRN_DK
    F_ = H * DK
    NC = L // C

    def body(q_ref, f_ref, v_ref, g_ref, o_ref, do_ref, st_ref, lb_ref, gn_ref,
             dp_ref, dlb_ref, dgn_ref, DST, bsc):
        DST[...] = jnp.zeros_like(DST)
        dlb_ref[...] = jnp.zeros_like(dlb_ref)
        dgn_ref[...] = jnp.zeros_like(dgn_ref)
        lbv = lb_ref[...]
        gn = gn_ref[...]
        causal = _iota((C, C), 0) >= _iota((C, C), 1)
        lastrow = _iota((C, DK), 0) == C - 1

        def chunk(i, carry):
            c = NC - 1 - i
            r0 = pl.multiple_of(c * C, C)
            rows = pl.ds(r0, C)
            qr = q_ref[0, rows, :]
            fr = f_ref[0, rows, :]
            q, k, logf, sig, nsig, fg = _hgrn_gates(qr, fr, lbv)
            v = v_ref[0, rows, :]
            b = _cumsum_rows(logf)
            bsc[...] = b
            bmid = bsc[C // 2 - 1:C // 2, :]
            blast = bsc[C - 1:C, :]
            qs, ks, eq, ek = _hgrn_scaled(q, k, b, bmid)
            A = jnp.where(causal, _dot(qs, ks, NT), 0.0)
            st_in = st_ref[0, 0, c]
            dst = DST[...]
            eb = jnp.exp(b)
            ebl = jnp.exp(blast)
            ekb = jnp.exp(blast - b)
            qb = q * eb
            kb = k * ekb
            ov = o_ref[0, rows, :]
            gr = g_ref[0, rows, :]
            rms = lax.rsqrt(jnp.mean(ov * ov, axis=-1, keepdims=True) + NORM_EPS)
            oh = ov * rms
            sg = _silu(gr)
            don_ = do_ref[0, rows, :]
            dgn_ref[0, 0] += jnp.sum(don_ * oh * sg, axis=0, keepdims=True)
            dp_ref[3, 0, rows, :] = (don_ * oh * gn * _dsilu(gr)).astype(BF16)
            doh = don_ * gn * sg
            do_ = rms * (doh - oh * jnp.mean(doh * oh, axis=-1, keepdims=True))
            dA = jnp.where(causal, _dot(do_, v, NT), 0.0)
            dp_ref[2, 0, rows, :] = (_dot(A, do_, TN) + _dot(kb, dst, NT)).astype(BF16)
            dqb = _dot(do_, st_in)
            dkb = _dot(v, dst)
            dq = _dot(dA, ks) * eq + dqb * eb
            dk_inter = dkb * ekb
            dk = _dot(dA, qs, TN) * ek + dk_inter
            db = q * dq - k * dk
            extra = jnp.sum(k * dk_inter, axis=0, keepdims=True) + ebl * jnp.sum(st_in * dst, axis=0, keepdims=True)
            db = db + jnp.where(lastrow, extra, 0.0)
            dlogf = _cumsum_rows(db, reverse=True)
            DST[...] = dst * ebl + _dot(do_, qb, TN)
            dp_ref[0, 0, rows, :] = (dq * _dsilu(qr)).astype(BF16)
            ss = sig * nsig
            dp_ref[1, 0, rows, :] = ((1.0 - lbv) * ss * (dlogf / fg - dk)).astype(BF16)
            dlb_ref[0] += jnp.sum(dlogf * nsig / fg - dk * nsig, axis=0, keepdims=True)
            return carry

        lax.fori_loop(0, NC, chunk, 0)

    def col(off):
        return pl.BlockSpec((1, L, DK), lambda b, h: (b, 0, off + h))

    outs = pl.pallas_call(
        body, name="hgrn_bwd", grid=(B, H),
        in_specs=[col(0), col(H), col(2 * H), col(3 * H), col(0), col(0),
                  pl.BlockSpec((1, 1, NC, DK, DK), lambda b, h: (b, h, 0, 0, 0)),
                  pl.BlockSpec((1, DK), lambda b, h: (0, h)), pl.BlockSpec((1, DK), lambda b, h: (0, 0))],
        out_specs=[pl.BlockSpec((4, 1, L, DK), lambda b, h: (0, b, 0, h)), pl.BlockSpec((1, 1, DK), lambda b, h: (b, 0, h)),
                   pl.BlockSpec((1, 1, 1, DK), lambda b, h: (b, h, 0, 0))],
        out_shape=[jax.ShapeDtypeStruct((4, B, L, F_), BF16), jax.ShapeDtypeStruct((B, 1, F_), F32),
                   jax.ShapeDtypeStruct((B, H, 1, DK), F32)],
        scratch_shapes=[pltpu.VMEM((DK, DK), F32), pltpu.VMEM((C, DK), F32)],
        compiler_params=_cparams("parallel", "parallel"))(proj, proj, proj, proj, o, don, st, lb, gnw)
    return outs


CONV_ROWS = 256
PAD_ROWS = 8


def _conv_taps(pad_ref, w_ref, r0, K, rb, forward=True):
    ext = pad_ref[pl.ds(r0, rb + PAD_ROWS), :]
    n = rb + PAD_ROWS
    acc = None
    for s in range(K):
        if forward:
            sh = ext if s == 0 else pltpu.roll(ext, s, 0)
            term = sh[PAD_ROWS:, :]
        else:
            sh = ext if s == 0 else pltpu.roll(ext, n - s, 0)
            term = sh[:rb, :]
        term = term * w_ref[K - 1 - s:K - s, :]
        acc = term if acc is None else acc + term
    return acc


def _conv_dw(ext, dc, K):
    row = _iota((8, dc.shape[1]), 0)
    out = jnp.zeros((8, dc.shape[1]), F32)
    for kk in range(K):
        s = K - 1 - kk
        sh = ext if s == 0 else pltpu.roll(ext, s, 0)
        out = out + jnp.where(row == kk, jnp.sum(dc * sh[PAD_ROWS:, :], axis=0, keepdims=True), 0.0)
    return out


def _mconv_fwd(zx, cw, cb, col0, width):
    B, L, _ = zx.shape
    K = cw.shape[0]
    ct = _pick_tile(width, 256)
    rb = min(CONV_ROWS, L)
    nrb = L // rb
    off = col0 // ct

    def body(x_ref, w_ref, b_ref, y_ref, xp):
        xp[0:PAD_ROWS, :] = jnp.zeros((PAD_ROWS, ct), F32)
        xp[PAD_ROWS:, :] = x_ref[0]
        bias = b_ref[...]

        def blk(i, carry):
            r0 = pl.multiple_of(i * rb, rb)
            y_ref[0, pl.ds(r0, rb), :] = _silu(_conv_taps(xp, w_ref, r0, K, rb) + bias)
            return carry

        lax.fori_loop(0, nrb, blk, 0)

    return pl.pallas_call(
        body, name="mconv_fwd", grid=(B, width // ct),
        in_specs=[pl.BlockSpec((1, L, ct), lambda b, j: (b, 0, off + j)), pl.BlockSpec((K, ct), lambda b, j: (0, j)),
                  pl.BlockSpec((1, ct), lambda b, j: (0, j))],
        out_specs=pl.BlockSpec((1, L, ct), lambda b, j: (b, 0, j)),
        out_shape=jax.ShapeDtypeStruct((B, L, width), F32),
        scratch_shapes=[pltpu.VMEM((L + PAD_ROWS, ct), F32)],
        compiler_params=_cparams("parallel", "parallel"))(zx, cw, cb.reshape(1, width))


def _mconv_bwd(zx, dya, cw, cb, col0, wcol0, name):
    B, L, _ = zx.shape
    K = cw.shape[0]
    npart, _, _, wq = dya.shape
    width = npart * wq
    ct = _pick_tile(wq, 256)
    rb = min(CONV_ROWS, L)
    nrb = L // rb
    off = (col0 + wcol0) // ct
    woff = wcol0 // ct
    pq = wq // ct

    def body(x_ref, dy_ref, w_ref, b_ref, dx_ref, dw_ref, db_ref, xp, dcp):
        xp[0:PAD_ROWS, :] = jnp.zeros((PAD_ROWS, ct), F32)
        xp[PAD_ROWS:, :] = x_ref[0]
        dcp[L:, :] = jnp.zeros((PAD_ROWS, ct), F32)
        bias = b_ref[...]

        def blk1(i, carry):
            dw, db = carry
            r0 = pl.multiple_of(i * rb, rb)
            cpre = _conv_taps(xp, w_ref, r0, K, rb) + bias
            dc = dy_ref[0, 0, pl.ds(r0, rb), :] * _dsilu(cpre)
            dcp[pl.ds(r0, rb), :] = dc
            ext = xp[pl.ds(r0, rb + PAD_ROWS), :]
            return dw + _conv_dw(ext, dc, K), db + jnp.sum(dc, axis=0, keepdims=True)

        dw, db = lax.fori_loop(0, nrb, blk1, (jnp.zeros((8, ct), F32), jnp.zeros((1, ct), F32)))
        dw_ref[0] = dw
        db_ref[0] = db

        def blk2(i, carry):
            r0 = pl.multiple_of(i * rb, rb)
            dx_ref[0, pl.ds(r0, rb), :] = _conv_taps(dcp, w_ref, r0, K, rb, forward=False).astype(BF16)
            return carry

        lax.fori_loop(0, nrb, blk2, 0)

    dx, dw, db = pl.pallas_call(
        body, name=name, grid=(B, width // ct),
        in_specs=[pl.BlockSpec((1, L, ct), lambda b, j: (b, 0, off + j)),
                  pl.BlockSpec((1, 1, L, ct), lambda b, j: (j // pq, b, 0, j % pq)),
                  pl.BlockSpec((K, ct), lambda b, j: (0, woff + j)), pl.BlockSpec((1, ct), lambda b, j: (0, woff + j))],
        out_specs=[pl.BlockSpec((1, L, ct), lambda b, j: (b, 0, j)), pl.BlockSpec((1, 8, ct), lambda b, j: (b, 0, j)),
                   pl.BlockSpec((1, 1, ct), lambda b, j: (b, 0, j))],
        out_shape=[jax.ShapeDtypeStruct((B, L, width), BF16), jax.ShapeDtypeStruct((B, 8, width), F32),
                   jax.ShapeDtypeStruct((B, 1, width), F32)],
        scratch_shapes=[pltpu.VMEM((L + PAD_ROWS, ct), F32), pltpu.VMEM((L + PAD_ROWS, ct), F32)],
        compiler_params=_cparams("parallel", "parallel"))(zx, dya, cw, cb.reshape(1, -1))
    return dx, dw[:, :K, :], db


def _ffn_mid_fwd(up, cw, cb, dff):
    B, L, _ = up.shape
    K = cw.shape[0]
    ct = _pick_tile(dff, 256)
    rb = min(CONV_ROWS, L)
    nrb = L // rb
    half = dff // ct

    def body(g_ref, u_ref, wg_ref, wu_ref, bg_ref, bu_ref, a_ref, gp, upad):
        gp[0:PAD_ROWS, :] = jnp.zeros((PAD_ROWS, ct), F32)
        upad[0:PAD_ROWS, :] = jnp.zeros((PAD_ROWS, ct), F32)
        gp[PAD_ROWS:, :] = g_ref[0]
        upad[PAD_ROWS:, :] = u_ref[0]
        bg, bu = bg_ref[...], bu_ref[...]

        def blk(i, carry):
            r0 = pl.multiple_of(i * rb, rb)
            cg = _conv_taps(gp, wg_ref, r0, K, rb) + bg
            cu = _conv_taps(upad, wu_ref, r0, K, rb) + bu
            a_ref[0, pl.ds(r0, rb), :] = (_silu(cg) * cu).astype(BF16)
            return carry

        lax.fori_loop(0, nrb, blk, 0)

    xg = pl.BlockSpec((1, L, ct), lambda b, j: (b, 0, j))
    xu = pl.BlockSpec((1, L, ct), lambda b, j: (b, 0, half + j))
    wgs = pl.BlockSpec((K, ct), lambda b, j: (0, j))
    wus = pl.BlockSpec((K, ct), lambda b, j: (0, half + j))
    bgs = pl.BlockSpec((1, ct), lambda b, j: (0, j))
    bus = pl.BlockSpec((1, ct), lambda b, j: (0, half + j))
    cb2 = cb.reshape(1, 2 * dff)
    return pl.pallas_call(
        body, name="ffn_mid_fwd", grid=(B, half), in_specs=[xg, xu, wgs, wus, bgs, bus], out_specs=xg,
        out_shape=jax.ShapeDtypeStruct((B, L, dff), BF16),
        scratch_shapes=[pltpu.VMEM((L + PAD_ROWS, ct), F32), pltpu.VMEM((L + PAD_ROWS, ct), F32)],
        compiler_params=_cparams("parallel", "parallel"))(up, up, cw, cw, cb2, cb2)


def _ffn_mid_bwd(up, dact, cw, cb, dff):
    B, L, _ = up.shape
    K = cw.shape[0]
    ct = _pick_tile(dff, 256)
    rb = min(CONV_ROWS, L)
    nrb = L // rb
    half = dff // ct

    def body(g_ref, u_ref, da_ref, wg_ref, wu_ref, bg_ref, bu_ref, dx_ref, dwg_ref, dwu_ref, dbg_ref, dbu_ref,
             gp, upad, dgp, dup):
        gp[0:PAD_ROWS, :] = jnp.zeros((PAD_ROWS, ct), F32)
        upad[0:PAD_ROWS, :] = jnp.zeros((PAD_ROWS, ct), F32)
        gp[PAD_ROWS:, :] = g_ref[0]
        upad[PAD_ROWS:, :] = u_ref[0]
        dgp[L:, :] = jnp.zeros((PAD_ROWS, ct), F32)
        dup[L:, :] = jnp.zeros((PAD_ROWS, ct), F32)
        bg, bu = bg_ref[...], bu_ref[...]

        def blk1(i, carry):
            dwg, dwu, dbg, dbu = carry
            r0 = pl.multiple_of(i * rb, rb)
            cg = _conv_taps(gp, wg_ref, r0, K, rb) + bg
            cu = _conv_taps(upad, wu_ref, r0, K, rb) + bu
            da = da_ref[0, pl.ds(r0, rb), :]
            dcg = da * cu * _dsilu(cg)
            dcu = da * _silu(cg)
            dgp[pl.ds(r0, rb), :] = dcg
            dup[pl.ds(r0, rb), :] = dcu
            eg = gp[pl.ds(r0, rb + PAD_ROWS), :]
            eu = upad[pl.ds(r0, rb + PAD_ROWS), :]
            return (dwg + _conv_dw(eg, dcg, K), dwu + _conv_dw(eu, dcu, K), dbg + jnp.sum(dcg, axis=0, keepdims=True),
                    dbu + jnp.sum(dcu, axis=0, keepdims=True))

        z8 = jnp.zeros((8, ct), F32)
        z1 = jnp.zeros((1, ct), F32)
        dwg, dwu, dbg, dbu = lax.fori_loop(0, nrb, blk1, (z8, z8, z1, z1))
        dwg_ref[0] = dwg
        dwu_ref[0] = dwu
        dbg_ref[0] = dbg
        dbu_ref[0] = dbu

        def blk2(i, carry):
            r0 = pl.multiple_of(i * rb, rb)
            dx_ref[0, 0, pl.ds(r0, rb), :] = _conv_taps(dgp, wg_ref, r0, K, rb, forward=False).astype(BF16)
            dx_ref[1, 0, pl.ds(r0, rb), :] = _conv_taps(dup, wu_ref, r0, K, rb, forward=False).astype(BF16)
            return carry

        lax.fori_loop(0, nrb, blk2, 0)

    xg = pl.BlockSpec((1, L, ct), lambda b, j: (b, 0, j))
    xu = pl.BlockSpec((1, L, ct), lambda b, j: (b, 0, half + j))
    wgs = pl.BlockSpec((K, ct), lambda b, j: (0, j))
    wus = pl.BlockSpec((K, ct), lambda b, j: (0, half + j))
    bgs = pl.BlockSpec((1, ct), lambda b, j: (0, j))
    bus = pl.BlockSpec((1, ct), lambda b, j: (0, half + j))
    w8 = pl.BlockSpec((1, 8, ct), lambda b, j: (b, 0, j))
    b1 = pl.BlockSpec((1, 1, ct), lambda b, j: (b, 0, j))
    cb2 = cb.reshape(1, 2 * dff)
    pad = pltpu.VMEM((L + PAD_ROWS, ct), F32)
    dx2, dwg, dwu, dbg, dbu = pl.pallas_call(
        body, name="ffn_mid_bwd", grid=(B, half), in_specs=[xg, xu, xg, wgs, wus, bgs, bus],
        out_specs=[pl.BlockSpec((2, 1, L, ct), lambda b, j: (0, b, 0, j)), w8, w8, b1, b1],
        out_shape=[jax.ShapeDtypeStruct((2, B, L, dff), BF16)] + [jax.ShapeDtypeStruct((B, 8, dff), F32)] * 2
        + [jax.ShapeDtypeStruct((B, 1, dff), F32)] * 2,
        scratch_shapes=[pad, pad, pad, pad],
        compiler_params=_cparams("parallel", "parallel"))(up, up, dact, cw, cw, cb2, cb2)
    dw = jnp.concatenate([dwg[:, :K], dwu[:, :K]], axis=-1)
    db = jnp.concatenate([dbg, dbu], axis=-1)
    return dx2, dw, db


def _ssd_consts(hpg, W):
    P = M_HEADDIM
    E = (_iota((LANES, W), 0) == _iota((LANES, W), 1) // P).astype(BF16)
    Ebig = (_iota((LANES, hpg * LANES), 0) == _iota((LANES, hpg * LANES), 1) // LANES).astype(BF16)
    causal = _iota((M_CHUNK, M_CHUNK), 0) >= _iota((M_CHUNK, M_CHUNK), 1)
    head_of_lane = _iota((1, W), 1) // P
    return E, Ebig, causal, head_of_lane


def _ssd_chunk_fwd(xs, Bm, Cm, dtr, bias, Aneg, E, Ebig, causal, head_of_lane, hpg, st, ar_sc, ae_sc):
    pre = dtr + bias
    dt = jnp.maximum(pre, 0.0) + jnp.log(1.0 + jnp.exp(-jnp.abs(pre)))
    Ad = dt * Aneg
    a_c = _cumsum_rows(Ad)
    ar_sc[...] = a_c.T
    aexp = _dot_exact(a_c, E)
    ae_sc[...] = aexp
    alast = ae_sc[M_CHUNK - 1:M_CHUNK, :]
    dtexp = _dot_exact(dt, E)
    X = xs * dtexp
    AC = _dot_exact(a_c, Ebig)
    CB = _dot(Cm, Bm, NT)
    Xb = X.astype(BF16)
    ydiag = jnp.zeros_like(xs)
    Ls = []
    for j in range(hpg):
        Lj = jnp.where(causal, jnp.exp(jnp.minimum(AC[:, j * LANES:(j + 1) * LANES] - ar_sc[j:j + 1, :], 0.0)), 0.0)
        Ls.append(Lj)
        Yj = _dot(CB * Lj, Xb)
        ydiag = ydiag + jnp.where(head_of_lane == j, Yj, 0.0)
    ea = jnp.exp(aexp)
    yoff = ea * _dot(Cm, st)
    dec = jnp.exp(alast - aexp)
    return dict(dt=dt, a_c=a_c, aexp=aexp, alast=alast, dtexp=dtexp, X=X, Xb=Xb, CB=CB, Ls=Ls, ydiag=ydiag, ea=ea,
                yoff=yoff, dec=dec)


def _ssd_fwd(xbca, zx, dtc, bias, Aneg, Dexp, nw, hpg):
    B, L, _ = xbca.shape
    G, N, C = M_GROUPS, M_D_STATE, M_CHUNK
    W = hpg * M_HEADDIM
    DI = G * W
    NC = L // C
    LB = min(L, 4 * C)
    ncb = LB // C

    def body(xs_ref, b_ref, c_ref, z_ref, dt_ref, bias_ref, a_ref, d_ref, nw_ref, y_ref, yn_ref, st_ref, ST, ar_sc, ae_sc):
        @pl.when(pl.program_id(2) == 0)
        def _():
            ST[...] = jnp.zeros_like(ST)

        E, Ebig, causal, head_of_lane = _ssd_consts(hpg, W)
        bias_ = bias_ref[0]
        Aneg_ = a_ref[0]
        Dv = d_ref[...]
        nwv = nw_ref[...]

        def chunk(ci, carry):
            r0 = pl.multiple_of(ci * C, C)
            rows = pl.ds(r0, C)
            xs = xs_ref[0, rows, :]
            Bm = b_ref[0, rows, :]
            Cm = c_ref[0, rows, :]
            st = ST[...]
            st_ref[0, 0, ci] = st
            f = _ssd_chunk_fwd(xs, Bm, Cm, dt_ref[0, 0, ci], bias_, Aneg_, E, Ebig, causal, head_of_lane, hpg, st, ar_sc, ae_sc)
            y = f["ydiag"] + f["yoff"] + xs * Dv
            ST[...] = st * jnp.exp(f["alast"]) + _dot(Bm, f["X"] * f["dec"], TN)
            yg = y * _silu(z_ref[0, rows, :])
            rstd = lax.rsqrt(jnp.mean(yg * yg, axis=-1, keepdims=True) + NORM_EPS)
            y_ref[0, rows, :] = y
            yn_ref[0, rows, :] = (yg * rstd * nwv).astype(BF16)
            return carry

        lax.fori_loop(0, ncb, chunk, 0)

    xw = pl.BlockSpec((1, LB, W), lambda b, g, s: (b, s, g))
    bsp = pl.BlockSpec((1, LB, N), lambda b, g, s: (b, s, DI // N + g))
    csp = pl.BlockSpec((1, LB, N), lambda b, g, s: (b, s, DI // N + G + g))
    dts = pl.BlockSpec((1, 1, ncb, C, LANES), lambda b, g, s: (b, g, s, 0, 0))
    hv = pl.BlockSpec((1, 1, LANES), lambda b, g, s: (g, 0, 0))
    wv = pl.BlockSpec((1, W), lambda b, g, s: (0, g))
    sts = pl.BlockSpec((1, 1, ncb, N, W), lambda b, g, s: (b, g, s, 0, 0))
    return pl.pallas_call(
        body, name="ssd_fwd", grid=(B, G, L // LB), in_specs=[xw, bsp, csp, xw, dts, hv, hv, wv, wv],
        out_specs=[xw, xw, sts],
        out_shape=[jax.ShapeDtypeStruct((B, L, DI), F32), jax.ShapeDtypeStruct((B, L, DI), BF16),
                   jax.ShapeDtypeStruct((B, G, NC, N, W), F32)],
        scratch_shapes=[pltpu.VMEM((N, W), F32), pltpu.VMEM((LANES, C), F32), pltpu.VMEM((C, W), F32)],
        compiler_params=_cparams("parallel", "parallel", "arbitrary"))(xbca, xbca, xbca, zx, dtc, bias, Aneg, Dexp, nw)


def _ssd_bwd(xbca, zx, dtc, ypre, dyn, st, bias, Aneg, Dexp, nw, hpg):
    B, L, _ = xbca.shape
    G, N, C = M_GROUPS, M_D_STATE, M_CHUNK
    W = hpg * M_HEADDIM
    DI = G * W
    NC = L // C
    LB = min(L, 4 * C)
    ncb = LB // C
    nsb = L // LB

    def body(xs_ref, b_ref, c_ref, z_ref, dt_ref, y_ref, dyn_ref, st_ref, bias_ref, a_ref, d_ref, nw_ref,
             dxs_ref, dbc_ref, dz_ref, ddt_ref, dnw_ref, dd_ref, da_ref, dbias_ref, DST, ar_sc, ae_sc):
        @pl.when(pl.program_id(2) == 0)
        def _():
            DST[...] = jnp.zeros_like(DST)
            dnw_ref[...] = jnp.zeros_like(dnw_ref)
            dd_ref[...] = jnp.zeros_like(dd_ref)
            da_ref[...] = jnp.zeros_like(da_ref)
            dbias_ref[...] = jnp.zeros_like(dbias_ref)

        E, Ebig, causal, head_of_lane = _ssd_consts(hpg, W)
        bias_ = bias_ref[0]
        Aneg_ = a_ref[0]
        Dv = d_ref[...]
        nwv = nw_ref[...]
        lane = _iota((1, LANES), 1)
        subl = _iota((LANES, 1), 0)
        lastrow = _iota((C, W), 0) == C - 1

        def chunk(i, carry):
            ci = ncb - 1 - i
            r0 = pl.multiple_of(ci * C, C)
            rows = pl.ds(r0, C)
            xs = xs_ref[0, rows, :]
            Bm = b_ref[0, rows, :]
            Cm = c_ref[0, rows, :]
            zr = z_ref[0, rows, :]
            dtr = dt_ref[0, 0, ci]
            st_in = st_ref[0, 0, ci]
            dst = DST[...]
            f = _ssd_chunk_fwd(xs, Bm, Cm, dtr, bias_, Aneg_, E, Ebig, causal, head_of_lane, hpg, st_in, ar_sc, ae_sc)
            X, Xb, dec, ea, CB = f["X"], f["Xb"], f["dec"], f["ea"], f["CB"]
            y = y_ref[0, rows, :]
            sz = _silu(zr)
            yg = y * sz
            rstd = lax.rsqrt(jnp.mean(yg * yg, axis=-1, keepdims=True) + NORM_EPS)
            yh = yg * rstd
            dyn_ = dyn_ref[0, rows, :]
            dnw_ref[0, 0] += jnp.sum(dyn_ * yh, axis=0, keepdims=True)
            dyh = dyn_ * nwv
            dyg = rstd * (dyh - yh * jnp.mean(dyh * yh, axis=-1, keepdims=True))
            dz_ref[0, rows, :] = (dyg * y * _dsilu(zr)).astype(BF16)
            dy = dyg * sz
            dd_ref[0, 0] += jnp.sum(dy * xs, axis=0, keepdims=True)
            dxs = dy * Dv
            dYo = dy * ea
            daexp = dy * f["yoff"]
            dCm = _dot(dYo, st_in, NT)
            dst_in = _dot(Cm, dYo, TN)
            dyb = dy.astype(BF16)
            dX = jnp.zeros_like(xs)
            dCB = jnp.zeros((C, C), F32)
            da_col = jnp.zeros((C, LANES), F32)
            da_row = jnp.zeros((LANES, C), F32)
            for j in range(hpg):
                Lj = f["Ls"][j]
                Gj = CB * Lj
                dYj = jnp.where(head_of_lane == j, dyb, jnp.zeros_like(dyb))
                dX = dX + _dot(Gj, dYj, TN)
                dGj = _dot(dYj, Xb, NT)
                dCB = dCB + dGj * Lj
                Wj = dGj * Gj
                da_col = da_col + jnp.sum(Wj, axis=1, keepdims=True) * (lane == j).astype(F32)
                da_row = da_row + (subl == j).astype(F32) * jnp.sum(Wj, axis=0, keepdims=True)
            dCm = dCm + _dot(dCB, Bm)
            dBm = _dot(dCB, Cm, TN)
            ela = jnp.exp(f["alast"])
            dalast = jnp.sum(dst * st_in, axis=0, keepdims=True) * ela
            DST[...] = dst * ela + dst_in
            dXd = _dot(Bm, dst)
            dBm = dBm + _dot(X * dec, dst, NT)
            dX = dX + dXd * dec
            ddec = dXd * X * dec
            dalast = dalast + jnp.sum(ddec, axis=0, keepdims=True)
            daexp = daexp - ddec + jnp.where(lastrow, dalast, 0.0)
            dxs = dxs + dX * f["dtexp"]
            ddtexp = dX * xs
            ddt = _dot_exact(ddtexp, E, NT, passes=2)
            da_c = _dot_exact(daexp, E, NT, passes=2) + da_col - da_row.T
            dAd = _cumsum_rows(da_c, reverse=True)
            ddt = ddt + dAd * Aneg_
            da_ref[0, 0] += jnp.sum(dAd * f["dt"], axis=0, keepdims=True) * Aneg_
            ddtr = ddt * jax.nn.sigmoid(dtr + bias_)
            dbias_ref[0, 0] += jnp.sum(ddtr, axis=0, keepdims=True)
            ddt_ref[0, 0, ci] = ddtr
            dxs_ref[0, rows, :] = dxs
            dbc_ref[0, 0, rows, :] = dBm
            dbc_ref[1, 0, rows, :] = dCm
            return carry

        lax.fori_loop(0, ncb, chunk, 0)

    def rev(s):
        return nsb - 1 - s

    xw = pl.BlockSpec((1, LB, W), lambda b, g, s: (b, rev(s), g))
    bsp = pl.BlockSpec((1, LB, N), lambda b, g, s: (b, rev(s), DI // N + g))
    csp = pl.BlockSpec((1, LB, N), lambda b, g, s: (b, rev(s), DI // N + G + g))
    gsp = pl.BlockSpec((1, LB, N), lambda b, g, s: (b, rev(s), g))
    dts = pl.BlockSpec((1, 1, ncb, C, LANES), lambda b, g, s: (b, g, rev(s), 0, 0))
    hv = pl.BlockSpec((1, 1, LANES), lambda b, g, s: (g, 0, 0))
    wv = pl.BlockSpec((1, W), lambda b, g, s: (0, g))
    sts = pl.BlockSpec((1, 1, ncb, N, W), lambda b, g, s: (b, g, rev(s), 0, 0))
    accw = pl.BlockSpec((1, 1, 1, W), lambda b, g, s: (b, g, 0, 0))
    acch = pl.BlockSpec((1, 1, 1, LANES), lambda b, g, s: (b, g, 0, 0))
    return pl.pallas_call(
        body, name="ssd_bwd", grid=(B, G, nsb), in_specs=[xw, bsp, csp, xw, dts, xw, xw, sts, hv, hv, wv, wv],
        out_specs=[xw, pl.BlockSpec((2, 1, LB, N), lambda b, g, s: (0, b, rev(s), g)), xw, dts, accw, accw, acch, acch],
        out_shape=[jax.ShapeDtypeStruct((B, L, DI), F32), jax.ShapeDtypeStruct((2, B, L, G * N), F32),
                   jax.ShapeDtypeStruct((B, L, DI), BF16),
                   jax.ShapeDtypeStruct((B, G, NC, C, LANES), F32), jax.ShapeDtypeStruct((B, G, 1, W), F32),
                   jax.ShapeDtypeStruct((B, G, 1, W), F32), jax.ShapeDtypeStruct((B, G, 1, LANES), F32),
                   jax.ShapeDtypeStruct((B, G, 1, LANES), F32)],
        scratch_shapes=[pltpu.VMEM((N, W), F32), pltpu.VMEM((LANES, C), F32), pltpu.VMEM((C, W), F32)],
        compiler_params=_cparams("parallel", "parallel", "arbitrary"))(
            xbca, xbca, xbca, zx, dtc, ypre, dyn, st, bias, Aneg, Dexp, nw)


def _adamw(w, g, m, v, name):
    shape = w.shape
    n = w.size
    cols = shape[-1]
    rows = n // cols
    tr = rows
    for cand in (512, 256, 128, 64, 32, 16, 8):
        if rows % cand == 0 and cand * cols * 4 <= 1024 * 1024:
            tr = cand
            break
    c1 = 1.0 / (1.0 - ADAM_B1 ** ADAM_STEP)
    c2 = 1.0 / (1.0 - ADAM_B2 ** ADAM_STEP)

    def body(w_ref, g_ref, m_ref, v_ref, d_ref, mo_ref, vo_ref):
        g_ = g_ref[...]
        mn = ADAM_B1 * m_ref[...] + (1.0 - ADAM_B1) * g_
        vn = ADAM_B2 * v_ref[...] + (1.0 - ADAM_B2) * (g_ * g_)
        d_ref[...] = -ADAM_LR * ((mn * c1) / (jnp.sqrt(vn * c2) + ADAM_EPS) + ADAM_WD * w_ref[...])
        mo_ref[...] = mn
        vo_ref[...] = vn

    spec = pl.BlockSpec((tr, cols), lambda i: (i, 0))
    r2 = lambda a: a.reshape(rows, cols)
    outs = pl.pallas_call(
        body, name=name, grid=(rows // tr,), in_specs=[spec] * 4, out_specs=[spec] * 3,
        out_shape=[jax.ShapeDtypeStruct((rows, cols), F32)] * 3,
        compiler_params=_cparams("parallel"))(r2(w), r2(g), r2(m), r2(v))
    return tuple(o.reshape(shape) for o in outs)


def _lower_bounds(lb_logits):
    p = jax.nn.softmax(lb_logits.astype(F32), axis=0)
    return jnp.cumsum(p, axis=0) - p[0]


def _pad_cols(a, n):
    return a if a.shape[-1] == n else jnp.pad(a, [(0, 0)] * (a.ndim - 1) + [(0, n - a.shape[-1])])


def _heads_to_lanes(a, G, hpg):
    return _pad_cols(a.reshape(G, 1, hpg), LANES)


def _local_step(x, target, P):
    B, L, D = x.shape
    T = B * L
    depth = P["mix_norm"].shape[0]
    H = D // HGRN_DK
    F_ = H * HGRN_DK
    DI = P["m_w_out"].shape[1]
    G, N = M_GROUPS, M_D_STATE
    MH = DI // M_HEADDIM
    hpg = MH // G
    assert hpg <= 8
    W = hpg * M_HEADDIM
    CD = DI + 2 * G * N
    MIN = DI + CD + MH
    MPAD = -(-MIN // LANES) * LANES
    dff = P["f_w_down"].shape[1]
    NC = L // M_CHUNK
    lbs = _lower_bounds(P["hgrn_lb_logits"])

    h = x.reshape(T, D)
    saved = []
    for i in range(depth):
        j = i // 2
        s = {"h_in": h}
        u = _rmsnorm_fwd(h, P["mix_norm"][i], "mix_norm_fwd")
        s["u"] = u
        if i % 2 == 0:
            proj = _matmul(u, P["hgrn_w_in"], b_layer=j, name="hgrn_in_fwd").reshape(B, L, 4 * F_)
            o, on, st = _hgrn_fwd(proj, lbs[j].reshape(1, F_), P["hgrn_gnorm"][j].reshape(1, HGRN_DK), H)
            h = _matmul(on.reshape(T, F_), P["hgrn_w_out"], b_layer=j, res=h, name="hgrn_out_fwd")
            s.update(proj=proj, o=o, on=on, st=st)
        else:
            zx = _matmul(u, P["m_w_in_t"], b_layer=j, tb=True, tn=1152, name="m_in_fwd").reshape(B, L, MPAD)
            xbca = _mconv_fwd(zx, P["m_conv_w"][j], P["m_conv_b"][j], DI, CD)
            dtr = zx[:, :, DI + CD:DI + CD + MH].reshape(B, NC, M_CHUNK, G, hpg).transpose(0, 3, 1, 2, 4)
            dtc = _pad_cols(dtr, LANES)
            bias = _heads_to_lanes(P["m_dt_bias"][j], G, hpg)
            Aneg = _heads_to_lanes(-jnp.exp(P["m_A_log"][j]), G, hpg)
            Dexp = jnp.repeat(P["m_D"][j], M_HEADDIM).reshape(1, DI)
            nw = P["m_norm"][j].reshape(1, DI)
            ypre, yn, st = _ssd_fwd(xbca, zx, dtc, bias, Aneg, Dexp, nw, hpg)
            h = _matmul(yn.reshape(T, DI), P["m_w_out"], b_layer=j, res=h, name="m_out_fwd")
            s.update(zx=zx, xbca=xbca, dtc=dtc, bias=bias, Aneg=Aneg, Dexp=Dexp, nw=nw, ypre=ypre, yn=yn, st=st)
        s["h_mid"] = h
        u2 = _rmsnorm_fwd(h, P["ffn_norm"][i], "ffn_norm_fwd")
        up = _matmul(u2, P["f_w_up"], b_layer=i, name="ffn_up_fwd").reshape(B, L, 2 * dff)
        act = _ffn_mid_fwd(up, P["f_conv_w"][i], P["f_conv_b"][i], dff)
        h = _matmul(act.reshape(T, dff), P["f_w_down"], b_layer=i, res=h, name="ffn_down_fwd")
        s.update(u2=u2, up=up, act=act)
        saved.append(s)

    loss, dh, dhb, d_final = _loss_head(h, P["final_norm"], target.reshape(T, D))

    g = {k: [None] * P[k].shape[0] for k in ("mix_norm", "ffn_norm", "hgrn_gnorm", "m_conv_w", "m_conv_b", "m_dt_bias",
                                              "m_A_log", "m_D", "m_norm", "f_conv_w", "f_conv_b")}
    gm = {k: lax.empty(P[k].shape, BF16) for k in ("hgrn_w_in", "hgrn_w_out", "m_w_in_t", "m_w_out", "f_w_up", "f_w_down")}

    def dw(kind, a, b, layer, name, **kw):
        gm[kind] = _matmul(a, b, ta=True, out_dtype=BF16, out=gm[kind], out_layer=layer, name=name, **kw)

    dlbs = [None] * lbs.shape[0]
    for i in reversed(range(depth)):
        j = i // 2
        s = saved[i]
        dact = _matmul(dhb, P["f_w_down"], b_layer=i, tb=True, name="ffn_down_dx").reshape(B, L, dff)
        dw("f_w_down", s["act"].reshape(T, dff), dhb, i, "ffn_down_dw")
        dup, dcw, dcb = _ffn_mid_bwd(s["up"], dact, P["f_conv_w"][i], P["f_conv_b"][i], dff)
        g["f_conv_w"][i] = jnp.sum(dcw, axis=0)
        g["f_conv_b"][i] = jnp.sum(dcb, axis=(0, 1))
        dup = dup.reshape(2, T, dff)
        dw("f_w_up", s["u2"], dup, i, "ffn_up_dw", b_parts=True)
        du2 = _matmul(dup, P["f_w_up"], a_parts=True, b_layer=i, tb=True, name="ffn_up_dx")
        dh, dhb, g["ffn_norm"][i] = _rmsnorm_bwd(s["h_mid"], P["ffn_norm"][i], du2, dh, "ffn_norm_bwd")
        if i % 2 == 0:
            don = _matmul(dhb, P["hgrn_w_out"], b_layer=j, tb=True, name="hgrn_out_dx").reshape(B, L, F_)
            dw("hgrn_w_out", s["on"].reshape(T, F_), dhb, j, "hgrn_out_dw")
            dproj, dlb, dgn = _hgrn_bwd(s["proj"], s["o"], don, s["st"], lbs[j].reshape(1, F_),
                                        P["hgrn_gnorm"][j].reshape(1, HGRN_DK), H)
            dlbs[j] = jnp.sum(dlb, axis=(0, 1))
            g["hgrn_gnorm"][j] = jnp.sum(dgn, axis=(0, 1, 2))
            dproj = dproj.reshape(4, T, F_)
            dw("hgrn_w_in", s["u"], dproj, j, "hgrn_in_dw", b_parts=True)
            du = _matmul(dproj, P["hgrn_w_in"], a_parts=True, b_layer=j, tb=True, name="hgrn_in_dx")
        else:
            dyn = _matmul(dhb, P["m_w_out"], b_layer=j, tb=True, name="m_out_dx").reshape(B, L, DI)
            dw("m_w_out", s["yn"].reshape(T, DI), dhb, j, "m_out_dw")
            dxs, dbc, dz, ddt, dnw, dD, dA, dbias = _ssd_bwd(s["xbca"], s["zx"], s["dtc"], s["ypre"], dyn, s["st"],
                                                             s["bias"], s["Aneg"], s["Dexp"], s["nw"], hpg)
            g["m_norm"][j] = jnp.sum(dnw, axis=(0, 2)).reshape(DI)
            g["m_D"][j] = jnp.sum(dD, axis=(0, 2)).reshape(MH, M_HEADDIM).sum(axis=-1)
            g["m_A_log"][j] = jnp.sum(dA, axis=(0, 2))[:, :hpg].reshape(MH)
            g["m_dt_bias"][j] = jnp.sum(dbias, axis=(0, 2))[:, :hpg].reshape(MH)
            cw, cb = P["m_conv_w"][j], P["m_conv_b"][j]
            dxx, dcw_x, dcb_x = _mconv_bwd(s["zx"], dxs[None], cw, cb, DI, 0, "mconv_bwd_x")
            dxb, dcw_b, dcb_b = _mconv_bwd(s["zx"], dbc, cw, cb, DI, DI, "mconv_bwd_bc")
            g["m_conv_w"][j] = jnp.concatenate([jnp.sum(dcw_x, axis=0), jnp.sum(dcw_b, axis=0)], axis=-1)
            g["m_conv_b"][j] = jnp.concatenate([jnp.sum(dcb_x, axis=(0, 1)), jnp.sum(dcb_b, axis=(0, 1))], axis=-1)
            ddt_t = _pad_cols(ddt[..., :hpg].transpose(0, 2, 3, 1, 4).reshape(T, MH), MPAD - DI - CD).astype(BF16)
            pieces = [(dz.reshape(T, DI), 0), (dxx.reshape(T, DI), DI), (dxb.reshape(T, 2 * G * N), 2 * DI), (ddt_t, DI + CD)]
            du = None
            for n_, (piece, off) in enumerate(pieces):
                gm["m_w_in_t"] = _matmul(piece, s["u"], ta=True, out_dtype=BF16, out=gm["m_w_in_t"], out_layer=j,
                                         out_off=off, name="m_in_dw%d" % n_)
                du = _matmul(piece, P["m_w_in_t"], b_layer=j, b_off=off, res=du, name="m_in_dx%d" % n_)
        dh, dhb, g["mix_norm"][i] = _rmsnorm_bwd(s["h_in"], P["mix_norm"][i], du, dh, "mix_norm_bwd")

    grads = {k: jnp.stack(vs) for k, vs in g.items()}
    grads["final_norm"] = d_final
    _, lb_vjp = jax.vjp(_lower_bounds, P["hgrn_lb_logits"])
    grads["hgrn_lb_logits"] = lb_vjp(jnp.stack(dlbs))[0]
    return loss, dh.reshape(B, L, D), grads, gm


ANY = pl.BlockSpec(memory_space=pl.ANY)
N_CHIPS = 4
N_DEV = 8


def _place():
    x, y, c = lax.axis_index("x"), lax.axis_index("y"), lax.axis_index("c")
    sibling = (x, y, 1 - c)
    chips = [(1 - x, y), (x, 1 - y), (1 - x, 1 - y)]
    return x, y, c, sibling, chips


def _remote(src, dst, send_sem, recv_sem, to):
    return pltpu.make_async_remote_copy(src_ref=src, dst_ref=dst, send_sem=send_sem, recv_sem=recv_sem, device_id=to,
                                        device_id_type=MESH)


KIND_AXIS = {"hgrn_w_in": "col", "f_w_up": "col", "m_w_in_t": "row", "hgrn_w_out": "row", "m_w_out": "row", "f_w_down": "row"}
KINDS = tuple(KIND_AXIS)
PEER_MASKS = (2, 1, 3)
ALL = slice(None)


def _chip_win(axis, cw, s):
    return (ALL, slice(s * cw, (s + 1) * cw)) if axis == "col" else (slice(s * cw, (s + 1) * cw), ALL)


def _half_win(axis, rows, cols, h):
    return (slice(h * rows // 2, (h + 1) * rows // 2), ALL) if axis == "col" else (ALL, slice(h * cols // 2, (h + 1) * cols // 2))


def _per_place(fn):
    x, y, c, sibling, chips = _place()
    chip = 2 * x + y
    for s in range(N_CHIPS):
        for cc in range(2):
            @pl.when(jnp.logical_and(chip == s, c == cc))
            def _():
                fn(s, cc, c, sibling, chips)


def _stage_shard(kind, shard, chip, pad_rows=0):
    lay, R, C = shard.shape
    axis = KIND_AXIS[kind]
    tr, tc = _row_tile(R), _pick_tile(C, 2048)
    nr, nc = R // tr, C // tc
    full = (lay, R, N_CHIPS * C) if axis == "col" else (lay, N_CHIPS * R + pad_rows, C)

    def body(s_ref, x_ref, o_ref):
        o_ref[...] = x_ref[...].astype(BF16)

    if axis == "col":
        dst = pl.BlockSpec((None, tr, tc), lambda l, i, j, s_ref: (l, i, s_ref[0] * nc + j))
    else:
        dst = pl.BlockSpec((None, tr, tc), lambda l, i, j, s_ref: (l, s_ref[0] * nr + i, j))
    grid_spec = pltpu.PrefetchScalarGridSpec(
        num_scalar_prefetch=1, grid=(lay, nr, nc),
        in_specs=[pl.BlockSpec((None, tr, tc), lambda l, i, j, s_ref: (l, i, j))], out_specs=dst)
    return pl.pallas_call(
        body, name="stage_" + kind, grid_spec=grid_spec, out_shape=jax.ShapeDtypeStruct(full, BF16),
        compiler_params=_cparams("parallel", "parallel", "parallel"))(chip.reshape(1).astype(jnp.int32), shard)


def _gather_weights(full, cws, pad_rows):
    padded = [k for k in KINDS if pad_rows.get(k, 0)]
    zeros = [jnp.zeros((full[k].shape[0], pad_rows[k], full[k].shape[2]), BF16) for k in padded]
    nq = len(KINDS)

    def body(*refs):
        zr = dict(zip(padded, refs[nq:nq + len(padded)]))
        fl = dict(zip(KINDS, refs[nq + len(padded):2 * nq + len(padded)]))
        send_sems, recv_sems, local_sems = refs[2 * nq + len(padded):]

        def run(s, cc, c, sibling, chips):
            local, first, passed = [], [], []
            for q, k in enumerate(KINDS):
                ax, cw = KIND_AXIS[k], cws[k]
                lay = fl[k].shape[0]
                if k in zr:
                    r0 = N_CHIPS * cw
                    cp = pltpu.make_async_copy(zr[k], fl[k].at[:, r0:r0 + pad_rows[k], :], local_sems.at[padded.index(k)])
                    cp.start()
                    local.append(cp)
                ls = slice(cc * lay // 2, (cc + 1) * lay // 2)
                mine = fl[k].at[(ls,) + _chip_win(ax, cw, s)]
                for j, (px, py) in enumerate(chips):
                    cp = _remote(mine, mine, send_sems.at[6 * q + j], recv_sems.at[6 * q + j], (px, py, c))
                    cp.start()
                    first.append(cp)
            for q, k in enumerate(KINDS):
                ax, cw = KIND_AXIS[k], cws[k]
                lay = fl[k].shape[0]
                ls = slice(cc * lay // 2, (cc + 1) * lay // 2)
                for j, (px, py) in enumerate(chips):
                    blk = fl[k].at[(ls,) + _chip_win(ax, cw, s ^ PEER_MASKS[j])]
                    _remote(blk, blk, send_sems.at[6 * q + j], recv_sems.at[6 * q + j], (px, py, c)).wait_recv()
                    cp = _remote(blk, blk, send_sems.at[6 * q + 3 + j], recv_sems.at[6 * q + 3 + j], sibling)
                    cp.start()
                    passed.append(cp)
            for q, k in enumerate(KINDS):
                ax, cw = KIND_AXIS[k], cws[k]
                lay = fl[k].shape[0]
                lo = slice((1 - cc) * lay // 2, (2 - cc) * lay // 2)
                for j in range(3):
                    blk = fl[k].at[(lo,) + _chip_win(ax, cw, s ^ PEER_MASKS[j])]
                    _remote(blk, blk, send_sems.at[6 * q + 3 + j], recv_sems.at[6 * q + 3 + j], sibling).wait_recv()
            for cp in first + passed:
                cp.wait_send()
            for cp in local:
                cp.wait()

        _per_place(run)

    outs = pl.pallas_call(
        body, name="gather_weights", in_specs=[ANY] * (nq + len(padded)), out_specs=[ANY] * nq,
        out_shape=[jax.ShapeDtypeStruct(full[k].shape, BF16) for k in KINDS],
        input_output_aliases={q: q for q in range(nq)},
        scratch_shapes=[pltpu.SemaphoreType.DMA((6 * nq,)), pltpu.SemaphoreType.DMA((6 * nq,)),
                        pltpu.SemaphoreType.DMA((max(len(padded), 1),))],
    )(*[full[k] for k in KINDS], *zeros)
    return dict(zip(KINDS, outs))


def _swap_halves(gm):
    nq = len(KINDS)
    half_shapes = {}
    for k in KINDS:
        lay, r, c_ = gm[k].shape
        half_shapes[k] = (lay, r // 2, c_) if KIND_AXIS[k] == "col" else (lay, r, c_ // 2)

    def body(*refs):
        g = dict(zip(KINDS, refs[:nq]))
        ra = dict(zip(KINDS, refs[nq:2 * nq]))
        send_sems, recv_sems = refs[2 * nq:]

        def run(s, cc, c, sibling, chips):
            cps = []
            for q, k in enumerate(KINDS):
                _, r, c_ = g[k].shape
                src = g[k].at[(ALL,) + _half_win(KIND_AXIS[k], r, c_, 1 - cc)]
                cps.append(_remote(src, ra[k], send_sems.at[q], recv_sems.at[q], sibling))
            for cp in cps:
                cp.start()
            for cp in cps:
                cp.wait()

        _per_place(run)

    outs = pl.pallas_call(
        body, name="swap_halves", in_specs=[ANY] * nq, out_specs=[ANY] * nq,
        out_shape=[jax.ShapeDtypeStruct(half_shapes[k], BF16) for k in KINDS],
        scratch_shapes=[pltpu.SemaphoreType.DMA((nq,)), pltpu.SemaphoreType.DMA((nq,))],
    )(*[gm[k] for k in KINDS])
    return dict(zip(KINDS, outs))


def _scatter_partials(pa, cws):
    nq = len(KINDS)
    win_shapes = {}
    for k in KINDS:
        lay, r, c_ = pa[k].shape
        win_shapes[k] = (3, lay, r, cws[k]) if KIND_AXIS[k] == "col" else (3, lay, cws[k], c_)

    def body(*refs):
        p = dict(zip(KINDS, refs[:nq]))
        rb = dict(zip(KINDS, refs[nq:2 * nq]))
        send_sems, recv_sems = refs[2 * nq:]

        def run(s, cc, c, sibling, chips):
            cps = []
            for q, k in enumerate(KINDS):
                for j, (px, py) in enumerate(chips):
                    src = p[k].at[(ALL,) + _chip_win(KIND_AXIS[k], cws[k], s ^ PEER_MASKS[j])]
                    cps.append(_remote(src, rb[k].at[j], send_sems.at[3 * q + j], recv_sems.at[3 * q + j], (px, py, c)))
            for cp in cps:
                cp.start()
            for cp in cps:
                cp.wait()

        _per_place(run)

    outs = pl.pallas_call(
        body, name="scatter_partials", in_specs=[ANY] * nq, out_specs=[ANY] * nq,
        out_shape=[jax.ShapeDtypeStruct(win_shapes[k], BF16) for k in KINDS],
        scratch_shapes=[pltpu.SemaphoreType.DMA((3 * nq,)), pltpu.SemaphoreType.DMA((3 * nq,))],
    )(*[pa[k] for k in KINDS])
    return dict(zip(KINDS, outs))


def _share_halves(g):
    nq = len(KINDS)

    def body(*refs):
        out = dict(zip(KINDS, refs[nq:2 * nq]))
        send_sems, recv_sems = refs[2 * nq:]

        def run(s, cc, c, sibling, chips):
            cps = []
            for q, k in enumerate(KINDS):
                _, r, c_ = out[k].shape
                mine = out[k].at[(ALL,) + _half_win(KIND_AXIS[k], r, c_, cc)]
                cps.append(_remote(mine, mine, send_sems.at[q], recv_sems.at[q], sibling))
            for cp in cps:
                cp.start()
            for cp in cps:
                cp.wait()

        _per_place(run)

    outs = pl.pallas_call(
        body, name="share_halves", in_specs=[ANY] * nq, out_specs=[ANY] * nq,
        out_shape=[jax.ShapeDtypeStruct(g[k].shape, F32) for k in KINDS],
        input_output_aliases={q: q for q in range(nq)},
        scratch_shapes=[pltpu.SemaphoreType.DMA((nq,)), pltpu.SemaphoreType.DMA((nq,))],
    )(*[g[k] for k in KINDS])
    return dict(zip(KINDS, outs))


def _all_gather_small(xs, name):
    m_per, n = xs.shape

    def body(x_ref, out_ref, send_sems, recv_sems, local_sem):
        x, y, c, sibling, chips = _place()
        me = (x, y, c)

        def rows(px, py, pc):
            return out_ref.at[pl.ds((4 * px + 2 * py + pc) * m_per, m_per), :]

        def copy(k, block, to, src=None):
            return _remote(rows(*block) if src is None else src, rows(*block), send_sems.at[k], recv_sems.at[k], to)

        mine = pltpu.make_async_copy(x_ref, rows(*me), local_sem)
        mine.start()
        first = [copy(0, me, sibling, src=x_ref)]
        first += [copy(1 + j, me, (*chip, c), src=x_ref) for j, chip in enumerate(chips)]
        for cp in first:
            cp.start()
        passed = [copy(4 + j, (*chip, c), sibling) for j, chip in enumerate(chips)]
        for j, chip in enumerate(chips):
            copy(1 + j, (*chip, c), me).wait_recv()
            passed[j].start()
        copy(0, sibling, me).wait_recv()
        for j, chip in enumerate(chips):
            copy(4 + j, (*chip, 1 - c), me).wait_recv()
        for cp in first + passed:
            cp.wait_send()
        mine.wait()

    vm = pl.BlockSpec(memory_space=pltpu.VMEM)
    return pl.pallas_call(
        body, name=name, in_specs=[vm], out_specs=vm, out_shape=jax.ShapeDtypeStruct((N_DEV * m_per, n), xs.dtype),
        scratch_shapes=[pltpu.SemaphoreType.DMA((7,)), pltpu.SemaphoreType.DMA((7,)), pltpu.SemaphoreType.DMA],
        compiler_params=pltpu.CompilerParams(vmem_limit_bytes=VMEM_LIMIT_BYTES),
    )(xs)


def _row_tile(rows, cap=256):
    for mult in (16, 8):
        best = None
        t = mult
        while t <= min(rows, cap):
            if rows % t == 0:
                best = t
            t += mult
        if best is not None:
            return best
    raise ValueError(rows)


def _add_sibling(kind, g, ra, core):
    lay, R, C = ra.shape
    axis = KIND_AXIS[kind]
    tr, tc = _row_tile(R), _pick_tile(C, 2048)
    nr, nc = R // tr, C // tc

    def body(c_ref, a_ref, b_ref, o_ref):
        o_ref[...] = (a_ref[...].astype(F32) + b_ref[...].astype(F32)).astype(o_ref.dtype)

    if axis == "col":
        own = pl.BlockSpec((None, tr, tc), lambda l, i, j, c_ref: (l, c_ref[0] * nr + i, j))
    else:
        own = pl.BlockSpec((None, tr, tc), lambda l, i, j, c_ref: (l, i, c_ref[0] * nc + j))
    same = pl.BlockSpec((None, tr, tc), lambda l, i, j, c_ref: (l, i, j))
    grid_spec = pltpu.PrefetchScalarGridSpec(num_scalar_prefetch=1, grid=(lay, nr, nc), in_specs=[own, same], out_specs=same)
    return pl.pallas_call(
        body, name="add_sibling_" + kind, grid_spec=grid_spec, out_shape=jax.ShapeDtypeStruct(ra.shape, BF16),
        compiler_params=_cparams("parallel", "parallel", "parallel"))(core.reshape(1).astype(jnp.int32), g, ra)


def _sum_chips(kind, pa, rb, chip, core):
    _, lay, R, C = rb.shape
    axis = KIND_AXIS[kind]
    tr, tc = _row_tile(R), _pick_tile(C, 2048)
    nr, nc = R // tr, C // tc

    def body(s_ref, c_ref, a_ref, b0_ref, b1_ref, b2_ref, o_ref):
        o_ref[...] = ((a_ref[...].astype(F32) + b0_ref[...].astype(F32)) + b1_ref[...].astype(F32)) + b2_ref[...].astype(F32)

    def rb_spec(n):
        return pl.BlockSpec((None, None, tr, tc), lambda l, i, j, s_ref, c_ref: (n, l, i, j))

    if axis == "col":
        own = pl.BlockSpec((None, tr, tc), lambda l, i, j, s_ref, c_ref: (l, i, s_ref[0] * nc + j))
        dst = pl.BlockSpec((None, tr, tc), lambda l, i, j, s_ref, c_ref: (l, c_ref[0] * nr + i, j))
        shard = (lay, 2 * R, C)
    else:
        own = pl.BlockSpec((None, tr, tc), lambda l, i, j, s_ref, c_ref: (l, s_ref[0] * nr + i, j))
        dst = pl.BlockSpec((None, tr, tc), lambda l, i, j, s_ref, c_ref: (l, i, c_ref[0] * nc + j))
        shard = (lay, R, 2 * C)
    grid_spec = pltpu.PrefetchScalarGridSpec(
        num_scalar_prefetch=2, grid=(lay, nr, nc), in_specs=[own, rb_spec(0), rb_spec(1), rb_spec(2)], out_specs=dst)
    return pl.pallas_call(
        body, name="sum_chips_" + kind, grid_spec=grid_spec, out_shape=jax.ShapeDtypeStruct(shard, F32),
        compiler_params=_cparams("parallel", "parallel", "parallel"))(
            chip.reshape(1).astype(jnp.int32), core.reshape(1).astype(jnp.int32), pa, rb, rb, rb)


def _sum_devices(gathered):
    M = gathered.shape[0] // N_DEV
    C = gathered.shape[1]

    def body(g_ref, o_ref):
        acc = g_ref[0:M, :]
        for d in range(1, N_DEV):
            acc = acc + g_ref[d * M:(d + 1) * M, :]
        o_ref[...] = acc

    vm = pl.BlockSpec(memory_space=pltpu.VMEM)
    return pl.pallas_call(body, name="sum_devices", in_specs=[vm], out_specs=vm, out_shape=jax.ShapeDtypeStruct((M, C), F32),
                          compiler_params=pltpu.CompilerParams(vmem_limit_bytes=VMEM_LIMIT_BYTES))(gathered)


WEIGHTS = ["mix_norm", "ffn_norm", "final_norm", "hgrn_w_in", "hgrn_lb_logits", "hgrn_gnorm", "hgrn_w_out", "m_w_in",
           "m_conv_w", "m_conv_b", "m_dt_bias", "m_A_log", "m_D", "m_norm", "m_w_out", "f_w_up", "f_conv_w", "f_conv_b",
           "f_w_down"]
BIG_COLS = ("hgrn_w_in", "m_w_in", "f_w_up")
BIG_ROWS = ("hgrn_w_out", "m_w_out", "f_w_down")
BIG = BIG_COLS + BIG_ROWS
SMALL_SHARDED = ("m_conv_w", "m_conv_b", "m_norm", "f_conv_w")
SMALL_REPLICATED = ("mix_norm", "ffn_norm", "final_norm", "hgrn_lb_logits", "hgrn_gnorm", "m_dt_bias", "m_A_log", "m_D",
                    "f_conv_b")
SMALL = SMALL_REPLICATED + SMALL_SHARDED


def _pack_rows(arrs, row_mult=8):
    flat = jnp.concatenate([a.reshape(-1).astype(F32) for a in arrs])
    unit = FLAT_COLS * row_mult
    n = -(-flat.size // unit) * unit
    return jnp.pad(flat, (0, n - flat.size)).reshape(-1, FLAT_COLS)


def _unpack_rows(buf, shapes):
    flat = buf.reshape(-1)
    out, off = [], 0
    for shp in shapes:
        n = math.prod(shp)
        out.append(flat[off:off + n].reshape(shp))
        off += n
    return out


def kernel(x, mix_norm, ffn_norm, final_norm, hgrn_w_in, hgrn_lb_logits, hgrn_gnorm, hgrn_w_out, m_w_in, m_conv_w, m_conv_b, m_dt_bias, m_A_log, m_D, m_norm, m_w_out, f_w_up, f_conv_w, f_conv_b, f_w_down, loss_target, m_mix_norm, m_ffn_norm, m_final_norm, m_hgrn_w_in, m_hgrn_lb_logits, m_hgrn_gnorm, m_hgrn_w_out, m_m_w_in, m_m_conv_w, m_m_conv_b, m_m_dt_bias, m_m_A_log, m_m_D, m_m_norm, m_m_w_out, m_f_w_up, m_f_conv_w, m_f_conv_b, m_f_w_down, v_mix_norm, v_ffn_norm, v_final_norm, v_hgrn_w_in, v_hgrn_lb_logits, v_hgrn_gnorm, v_hgrn_w_out, v_m_w_in, v_m_conv_w, v_m_conv_b, v_m_dt_bias, v_m_A_log, v_m_D, v_m_norm, v_m_w_out, v_f_w_up, v_f_conv_w, v_f_conv_b, v_f_w_down):
    given = dict(locals())
    w = {n: given[n] for n in WEIGHTS}
    mom1 = {n: given["m_" + n] for n in WEIGHTS}
    mom2 = {n: given["v_" + n] for n in WEIGHTS}
    chip = 2 * lax.axis_index("x") + lax.axis_index("y")
    core = lax.axis_index("c")

    shards = {k: w[k] for k in KINDS if k != "m_w_in_t"}
    shards["m_w_in_t"] = w["m_w_in"].transpose(0, 2, 1).astype(BF16)
    m_in = N_CHIPS * w["m_w_in"].shape[2]
    pad_rows = {"m_w_in_t": -(-m_in // LANES) * LANES - m_in}
    cws = {k: shards[k].shape[2] if KIND_AXIS[k] == "col" else shards[k].shape[1] for k in KINDS}
    P = _gather_weights({k: _stage_shard(k, shards[k], chip, pad_rows.get(k, 0)) for k in KINDS}, cws, pad_rows)
    own = _pack_rows([w[n] for n in SMALL_SHARDED])
    all_small = _all_gather_small(own, "gather_small_params").reshape(N_CHIPS, 2, -1)[:, 0]
    per_chip = [_unpack_rows(all_small[s], [w[n].shape for n in SMALL_SHARDED]) for s in range(N_CHIPS)]
    for i, n in enumerate(SMALL_SHARDED):
        P[n] = jnp.concatenate([per_chip[s][i] for s in range(N_CHIPS)], axis=-1)
    for n in SMALL_REPLICATED:
        P[n] = w[n]

    loss_part, grad_x, g_full, gm = _local_step(x, loss_target, P)

    ra = _swap_halves(gm)
    pa = {k: _add_sibling(k, gm[k], ra[k], core) for k in KINDS}
    rb = _scatter_partials(pa, cws)
    g_sh = _share_halves({k: _sum_chips(k, pa[k], rb[k], chip, core) for k in KINDS})
    grads = {k: g_sh[k] for k in KINDS if k != "m_w_in_t"}
    grads["m_w_in"] = g_sh["m_w_in_t"].transpose(0, 2, 1)

    small_shapes = [g_full[n].shape for n in SMALL] + [(1,)]
    packed = _pack_rows([g_full[n] for n in SMALL] + [loss_part[0, 0:1]])
    summed = _sum_devices(_all_gather_small(packed, "gather_small_grads"))
    small = _unpack_rows(summed, small_shapes)
    loss = small[-1][0]
    for n, gs in zip(SMALL, small[:-1]):
        if n in SMALL_SHARDED:
            width = w[n].shape[-1]
            gs = lax.dynamic_slice_in_dim(gs, chip * width, width, axis=gs.ndim - 1)
        grads[n] = gs

    delta, new_m, new_v = {}, {}, {}
    for n in BIG:
        delta[n], new_m[n], new_v[n] = _adamw(w[n], grads[n], mom1[n], mom2[n], "adamw_" + n)
    shapes = [w[n].shape for n in SMALL]
    ds, ms, vs = _adamw(_pack_rows([w[n] for n in SMALL]), _pack_rows([grads[n] for n in SMALL]),
                        _pack_rows([mom1[n] for n in SMALL]), _pack_rows([mom2[n] for n in SMALL]), "adamw_small")
    for n, d_, m_, v_ in zip(SMALL, _unpack_rows(ds, shapes), _unpack_rows(ms, shapes), _unpack_rows(vs, shapes)):
        delta[n], new_m[n], new_v[n] = d_, m_, v_

    return (loss, grad_x, *[grads[n] for n in WEIGHTS], *[delta[n] for n in WEIGHTS], *[new_m[n] for n in WEIGHTS],
            *[new_v[n] for n in WEIGHTS])
```

```python
import functools
import math

import jax
import jax.numpy as jnp
from jax import lax
from jax.experimental import pallas as pl
from jax.experimental.pallas import tpu as pltpu

F32 = jnp.float32
BF16 = jnp.bfloat16
NORM_EPS = 1e-5
HGRN_DK = 128
HGRN_CHUNK = 64
M_HEADDIM = 64
M_GROUPS = 8
M_D_STATE = 128
M_CONV = 4
M_CHUNK = 128
FFN_CONV = 3
EXP_CLIP = 80.0
LANES = 128
VMEM_LIMIT_BYTES = 56 * 1024 * 1024
FLAT_COLS = 1024
ADAM_LR, ADAM_B1, ADAM_B2, ADAM_EPS, ADAM_WD, ADAM_STEP = 0.001, 0.9, 0.999, 1e-08, 0.01, 10
MESH = pl.DeviceIdType.MESH

NN = (((1,), (0,)), ((), ()))
NT = (((1,), (1,)), ((), ()))
TN = (((0,), (0,)), ((), ()))


def _cparams(*sems):
    return pltpu.CompilerParams(dimension_semantics=sems, vmem_limit_bytes=VMEM_LIMIT_BYTES)


def _dot(a, b, dn=NN):
    return lax.dot_general(a.astype(BF16), b.astype(BF16), dn, preferred_element_type=F32)


def _dot_exact(x, m, dn=NN, passes=3, x_first=True):
    acc = None
    r = x
    for _ in range(passes):
        p = r.astype(BF16)
        r = r - p.astype(F32)
        t = lax.dot_general(p, m, dn, preferred_element_type=F32) if x_first else lax.dot_general(m, p, dn, preferred_element_type=F32)
        acc = t if acc is None else acc + t
    return acc


def _iota(shape, dim):
    return lax.broadcasted_iota(jnp.int32, shape, dim)


def _cumsum_rows(x, reverse=False):
    n = x.shape[0]
    row = _iota(x.shape, 0)
    s = 1
    while s < n:
        if reverse:
            x = x + jnp.where(row < n - s, pltpu.roll(x, n - s, 0), 0.0)
        else:
            x = x + jnp.where(row >= s, pltpu.roll(x, s, 0), 0.0)
        s *= 2
    return x


def _silu(x):
    return x * jax.nn.sigmoid(x)


def _dsilu(x):
    s = jax.nn.sigmoid(x)
    return s * (1.0 + x * (1.0 - s))


def _pick_tile(dim, pref):
    if dim <= pref:
        return dim
    best = None
    t = LANES
    while t <= pref:
        if dim % t == 0:
            best = t
        t += LANES
    assert best is not None, (dim, pref)
    return best


def _matmul(a, b, *, ta=False, tb=False, res=None, out_dtype=F32, tm=1024, tn=1024, tk=2048, name,
            a_parts=False, b_parts=False, b_layer=None, b_off=0, out=None, out_layer=None, out_off=0):
    a = a.astype(BF16)
    b = b.astype(BF16)
    if a_parts:
        assert not ta
        pa, M, kp = a.shape
        K = pa * kp
    else:
        M, K = (a.shape[1], a.shape[0]) if ta else a.shape
    bsh = b.shape[1:] if b_layer is not None else b.shape
    if b_parts:
        assert not tb
        pb, _, np_ = bsh
        N = pb * np_
    else:
        N = bsh[0] if tb else bsh[1]
    tm, tn, tk = _pick_tile(M, tm), _pick_tile(np_ if b_parts else N, tn), _pick_tile(kp if a_parts else K, tk)
    nk = K // tk
    dn = (((0 if ta else 1,), (1 if tb else 0,)), ((), ()))
    assert b_off % tk == 0 and out_off % tm == 0

    def body(*refs):
        refs = list(refs)
        acc = refs.pop() if nk > 1 else None
        o_ref = refs.pop()
        if out is not None:
            refs.pop()
        a_ref, b_ref = refs[0], refs[1]
        r_ref = refs[2] if res is not None else None
        k = pl.program_id(2)

        def prod():
            return lax.dot_general(a_ref[...], b_ref[...], dn, preferred_element_type=F32)

        def finish(r):
            if res is not None:
                r = r + r_ref[...]
            o_ref[...] = r.astype(out_dtype)

        if nk == 1:
            finish(prod())
            return

        @pl.when(k == 0)
        def _():
            acc[...] = prod()

        @pl.when(jnp.logical_and(k > 0, k < nk - 1))
        def _():
            acc[...] += prod()

        @pl.when(k == nk - 1)
        def _():
            finish(acc[...] + prod())

    if a_parts:
        kpb = kp // tk
        a_spec = pl.BlockSpec((None, tm, tk), lambda i, j, k: (k // kpb, i, k % kpb))
    elif ta:
        a_spec = pl.BlockSpec((tk, tm), lambda i, j, k: (k, i))
    else:
        a_spec = pl.BlockSpec((tm, tk), lambda i, j, k: (i, k))
    lead = () if b_layer is None else (b_layer,)
    lead_blk = () if b_layer is None else (None,)
    kb0 = b_off // tk
    if b_parts:
        npb = np_ // tn
        b_spec = pl.BlockSpec(lead_blk + (None, tk, tn), lambda i, j, k: lead + (j // npb, k, j % npb))
    elif tb:
        b_spec = pl.BlockSpec(lead_blk + (tn, tk), lambda i, j, k: lead + (j, k))
    else:
        b_spec = pl.BlockSpec(lead_blk + (tk, tn), lambda i, j, k: lead + (kb0 + k, j))
    r_spec = pl.BlockSpec((tm, tn), lambda i, j, k: (i, j))
    in_specs = [a_spec, b_spec] + ([r_spec] if res is not None else [])
    args = (a, b) + ((res,) if res is not None else ())
    if out is None:
        o_spec, out_shape, aliases = r_spec, jax.ShapeDtypeStruct((M, N), out_dtype), {}
    else:
        assert out.dtype == out_dtype and out.shape[-1] == N
        olead = () if out_layer is None else (out_layer,)
        olead_blk = () if out_layer is None else (None,)
        ob0 = out_off // tm
        o_spec = pl.BlockSpec(olead_blk + (tm, tn), lambda i, j, k: olead + (ob0 + i, j))
        out_shape = jax.ShapeDtypeStruct(out.shape, out.dtype)
        aliases = {len(args): 0}
        in_specs = in_specs + [pl.BlockSpec(memory_space=pl.ANY)]
        args = args + (out,)
    return pl.pallas_call(
        body, name=name, grid=(M // tm, N // tn, nk), in_specs=in_specs, out_specs=o_spec, out_shape=out_shape,
        scratch_shapes=[pltpu.VMEM((tm, tn), F32)] if nk > 1 else [], input_output_aliases=aliases,
        compiler_params=_cparams("parallel", "parallel", "arbitrary"))(*args)


def _rmsnorm_fwd(h, w, name):
    T, D = h.shape
    tm = _pick_tile(T, 256)

    def body(h_ref, w_ref, u_ref):
        x = h_ref[...]
        r = lax.rsqrt(jnp.mean(x * x, axis=-1, keepdims=True) + NORM_EPS)
        u_ref[...] = (x * r * w_ref[...]).astype(BF16)

    return pl.pallas_call(
        body, name=name, grid=(T // tm,),
        in_specs=[pl.BlockSpec((tm, D), lambda i: (i, 0)), pl.BlockSpec((1, D), lambda i: (0, 0))],
        out_specs=pl.BlockSpec((tm, D), lambda i: (i, 0)), out_shape=jax.ShapeDtypeStruct((T, D), BF16),
        compiler_params=_cparams("parallel"))(h, w.reshape(1, D))


def _rmsnorm_bwd(h, w, du, dres, name, dep=None):
    T, D = h.shape
    tm = _pick_tile(T, 256)

    def body(h_ref, w_ref, du_ref, dr_ref, *rest):
        dh_ref, dhb_ref, dw_ref = rest[-3:]
        x = h_ref[...]
        r = lax.rsqrt(jnp.mean(x * x, axis=-1, keepdims=True) + NORM_EPS)
        xh = x * r
        du_ = du_ref[...]
        dy = du_ * w_ref[...]
        dh = dr_ref[...] + r * (dy - xh * jnp.mean(dy * xh, axis=-1, keepdims=True))
        dh_ref[...] = dh
        dhb_ref[...] = dh.astype(BF16)
        part = jnp.sum(du_ * xh, axis=0, keepdims=True)

        @pl.when(pl.program_id(0) == 0)
        def _():
            dw_ref[...] = part

        @pl.when(pl.program_id(0) > 0)
        def _():
            dw_ref[...] += part

    row = pl.BlockSpec((tm, D), lambda i: (i, 0))
    vec = pl.BlockSpec((1, D), lambda i: (0, 0))
    extra_specs, extra = ([], ()) if dep is None else ([pl.BlockSpec(memory_space=pl.ANY)], (dep,))
    dh, dhb, dw = pl.pallas_call(
        body, name=name, grid=(T // tm,), in_specs=[row, vec, row, row] + extra_specs, out_specs=[row, row, vec],
        out_shape=[jax.ShapeDtypeStruct((T, D), F32), jax.ShapeDtypeStruct((T, D), BF16), jax.ShapeDtypeStruct((1, D), F32)],
        compiler_params=_cparams("arbitrary"))(h, w.reshape(1, D), du, dres, *extra)
    return dh, dhb, dw.reshape(D)


def _loss_head(h, w, target):
    T, D = h.shape
    tm = _pick_tile(T, 256)

    def body(h_ref, w_ref, t_ref, loss_ref, dh_ref, dhb_ref, dw_ref):
        x = h_ref[...]
        wv = w_ref[...]
        r = lax.rsqrt(jnp.mean(x * x, axis=-1, keepdims=True) + NORM_EPS)
        xh = x * r
        e = xh * wv - t_ref[...]
        lpart = jnp.zeros((1, LANES), F32) + 0.5 * jnp.sum(jnp.mean(e * e, axis=-1, keepdims=True))
        dyo = e * (1.0 / D)
        dy = dyo * wv
        dh = r * (dy - xh * jnp.mean(dy * xh, axis=-1, keepdims=True))
        dh_ref[...] = dh
        dhb_ref[...] = dh.astype(BF16)
        part = jnp.sum(dyo * xh, axis=0, keepdims=True)

        @pl.when(pl.program_id(0) == 0)
        def _():
            dw_ref[...] = part
            loss_ref[...] = lpart

        @pl.when(pl.program_id(0) > 0)
        def _():
            dw_ref[...] += part
            loss_ref[...] += lpart

    row = pl.BlockSpec((tm, D), lambda i: (i, 0))
    vec = pl.BlockSpec((1, D), lambda i: (0, 0))
    lvec = pl.BlockSpec((1, LANES), lambda i: (0, 0))
    loss, dh, dhb, dw = pl.pallas_call(
        body, name="loss_head", grid=(T // tm,), in_specs=[row, vec, row], out_specs=[lvec, row, row, vec],
        out_shape=[jax.ShapeDtypeStruct((1, LANES), F32), jax.ShapeDtypeStruct((T, D), F32),
                   jax.ShapeDtypeStruct((T, D), BF16), jax.ShapeDtypeStruct((1, D), F32)],
        compiler_params=_cparams("arbitrary"))(h, w.reshape(1, D), target)
    return loss, dh, dhb, dw.reshape(D)


def _hgrn_gates(qr, fr, lb):
    sig = jax.nn.sigmoid(fr)
    nsig = jax.nn.sigmoid(-fr)
    fg = lb + (1.0 - lb) * sig
    logf = jnp.log(fg)
    k = (1.0 - lb) * nsig
    q = _silu(qr)
    return q, k, logf, sig, nsig, fg


def _hgrn_scaled(q, k, b, bmid):
    eq = jnp.exp(jnp.clip(b - bmid, -EXP_CLIP, EXP_CLIP))
    ek = jnp.exp(jnp.clip(bmid - b, -EXP_CLIP, EXP_CLIP))
    return q * eq, k * ek, eq, ek


def _hgrn_fwd(proj, lb, gnw, H):
    B, L, _ = proj.shape
    C, DK = HGRN_CHUNK, HGRN_DK
    F_ = H * DK
    NC = L // C

    def body(q_ref, f_ref, v_ref, g_ref, lb_ref, gn_ref, o_ref, on_ref, st_ref, ST, bsc):
        ST[...] = jnp.zeros_like(ST)
        lbv = lb_ref[...]
        gn = gn_ref[...]
        causal = _iota((C, C), 0) >= _iota((C, C), 1)

        def chunk(c, carry):
            r0 = pl.multiple_of(c * C, C)
            rows = pl.ds(r0, C)
            q, k, logf, _, _, _ = _hgrn_gates(q_ref[0, rows, :], f_ref[0, rows, :], lbv)
            v = v_ref[0, rows, :]
            b = _cumsum_rows(logf)
            bsc[...] = b
            bmid = bsc[C // 2 - 1:C // 2, :]
            blast = bsc[C - 1:C, :]
            qs, ks, _, _ = _hgrn_scaled(q, k, b, bmid)
            A = jnp.where(causal, _dot(qs, ks, NT), 0.0)
            st = ST[...]
            st_ref[0, 0, c] = st
            o = _dot(A, v) + _dot(q * jnp.exp(b), st, NT)
            kb = k * jnp.exp(blast - b)
            ST[...] = st * jnp.exp(blast) + _dot(v, kb, TN)
            rms = lax.rsqrt(jnp.mean(o * o, axis=-1, keepdims=True) + NORM_EPS)
            o_ref[0, rows, :] = o
            on_ref[0, rows, :] = (o * rms * gn * _silu(g_ref[0, rows, :])).astype(BF16)
            return carry

        lax.fori_loop(0, NC, chunk, 0)

    def col(off):
        return pl.BlockSpec((1, L, DK), lambda b, h: (b, 0, off + h))

    return pl.pallas_call(
        body, name="hgrn_fwd", grid=(B, H),
        in_specs=[col(0), col(H), col(2 * H), col(3 * H), pl.BlockSpec((1, DK), lambda b, h: (0, h)),
                  pl.BlockSpec((1, DK), lambda b, h: (0, 0))],
        out_specs=[col(0), col(0), pl.BlockSpec((1, 1, NC, DK, DK), lambda b, h: (b, h, 0, 0, 0))],
        out_shape=[jax.ShapeDtypeStruct((B, L, F_), F32), jax.ShapeDtypeStruct((B, L, F_), BF16),
                   jax.ShapeDtypeStruct((B, H, NC, DK, DK), F32)],
        scratch_shapes=[pltpu.VMEM((DK, DK), F32), pltpu.VMEM((C, DK), F32)],
        compiler_params=_cparams("parallel", "parallel"))(proj, proj, proj, proj, lb, gnw)


def _hgrn_bwd(proj, o, don, st, lb, gnw, H):
    B, L, _ = proj.shape
    C, DK = HGRN_CHUNK, HGRN_DK
    F_ = H * DK
    NC = L // C

    def body(q_ref, f_ref, v_ref, g_ref, o_ref, do_ref, st_ref, lb_ref, gn_ref,
             dp_ref, dlb_ref, dgn_ref, DST, bsc):
        DST[...] = jnp.zeros_like(DST)
        dlb_ref[...] = jnp.zeros_like(dlb_ref)
        dgn_ref[...] = jnp.zeros_like(dgn_ref)
        lbv = lb_ref[...]
        gn = gn_ref[...]
        causal = _iota((C, C), 0) >= _iota((C, C), 1)
        lastrow = _iota((C, DK), 0) == C - 1

        def chunk(i, carry):
            c = NC - 1 - i
            r0 = pl.multiple_of(c * C, C)
            rows = pl.ds(r0, C)
            qr = q_ref[0, rows, :]
            fr = f_ref[0, rows, :]
            q, k, logf, sig, nsig, fg = _hgrn_gates(qr, fr, lbv)
            v = v_ref[0, rows, :]
            b = _cumsum_rows(logf)
            bsc[...] = b
            bmid = bsc[C // 2 - 1:C // 2, :]
            blast = bsc[C - 1:C, :]
            qs, ks, eq, ek = _hgrn_scaled(q, k, b, bmid)
            A = jnp.where(causal, _dot(qs, ks, NT), 0.0)
            st_in = st_ref[0, 0, c]
            dst = DST[...]
            eb = jnp.exp(b)
            ebl = jnp.exp(blast)
            ekb = jnp.exp(blast - b)
            qb = q * eb
            kb = k * ekb
            ov = o_ref[0, rows, :]
            gr = g_ref[0, rows, :]
            rms = lax.rsqrt(jnp.mean(ov * ov, axis=-1, keepdims=True) + NORM_EPS)
            oh = ov * rms
            sg = _silu(gr)
            don_ = do_ref[0, rows, :]
            dgn_ref[0, 0] += jnp.sum(don_ * oh * sg, axis=0, keepdims=True)
            dp_ref[3, 0, rows, :] = (don_ * oh * gn * _dsilu(gr)).astype(BF16)
            doh = don_ * gn * sg
            do_ = rms * (doh - oh * jnp.mean(doh * oh, axis=-1, keepdims=True))
            dA = jnp.where(causal, _dot(do_, v, NT), 0.0)
            dp_ref[2, 0, rows, :] = (_dot(A, do_, TN) + _dot(kb, dst, NT)).astype(BF16)
            dqb = _dot(do_, st_in)
            dkb = _dot(v, dst)
            dq = _dot(dA, ks) * eq + dqb * eb
            dk_inter = dkb * ekb
            dk = _dot(dA, qs, TN) * ek + dk_inter
            db = q * dq - k * dk
            extra = jnp.sum(k * dk_inter, axis=0, keepdims=True) + ebl * jnp.sum(st_in * dst, axis=0, keepdims=True)
            db = db + jnp.where(lastrow, extra, 0.0)
            dlogf = _cumsum_rows(db, reverse=True)
            DST[...] = dst * ebl + _dot(do_, qb, TN)
            dp_ref[0, 0, rows, :] = (dq * _dsilu(qr)).astype(BF16)
            ss = sig * nsig
            dp_ref[1, 0, rows, :] = ((1.0 - lbv) * ss * (dlogf / fg - dk)).astype(BF16)
            dlb_ref[0] += jnp.sum(dlogf * nsig / fg - dk * nsig, axis=0, keepdims=True)
            return carry

        lax.fori_loop(0, NC, chunk, 0)

    def col(off):
        return pl.BlockSpec((1, L, DK), lambda b, h: (b, 0, off + h))

    outs = pl.pallas_call(
        body, name="hgrn_bwd", grid=(B, H),
        in_specs=[col(0), col(H), col(2 * H), col(3 * H), col(0), col(0),
                  pl.BlockSpec((1, 1, NC, DK, DK), lambda b, h: (b, h, 0, 0, 0)),
                  pl.BlockSpec((1, DK), lambda b, h: (0, h)), pl.BlockSpec((1, DK), lambda b, h: (0, 0))],
        out_specs=[pl.BlockSpec((4, 1, L, DK), lambda b, h: (0, b, 0, h)), pl.BlockSpec((1, 1, DK), lambda b, h: (b, 0, h)),
                   pl.BlockSpec((1, 1, 1, DK), lambda b, h: (b, h, 0, 0))],
        out_shape=[jax.ShapeDtypeStruct((4, B, L, F_), BF16), jax.ShapeDtypeStruct((B, 1, F_), F32),
                   jax.ShapeDtypeStruct((B, H, 1, DK), F32)],
        scratch_shapes=[pltpu.VMEM((DK, DK), F32), pltpu.VMEM((C, DK), F32)],
        compiler_params=_cparams("parallel", "parallel"))(proj, proj, proj, proj, o, don, st, lb, gnw)
    return outs


CONV_ROWS = 256
PAD_ROWS = 8


def _conv_taps(pad_ref, w_ref, r0, K, rb, forward=True):
    ext = pad_ref[pl.ds(r0, rb + PAD_ROWS), :]
    n = rb + PAD_ROWS
    acc = None
    for s in range(K):
        if forward:
            sh = ext if s == 0 else pltpu.roll(ext, s, 0)
            term = sh[PAD_ROWS:, :]
        else:
            sh = ext if s == 0 else pltpu.roll(ext, n - s, 0)
            term = sh[:rb, :]
        term = term * w_ref[K - 1 - s:K - s, :]
        acc = term if acc is None else acc + term
    return acc


def _conv_dw(ext, dc, K):
    row = _iota((8, dc.shape[1]), 0)
    out = jnp.zeros((8, dc.shape[1]), F32)
    for kk in range(K):
        s = K - 1 - kk
        sh = ext if s == 0 else pltpu.roll(ext, s, 0)
        out = out + jnp.where(row == kk, jnp.sum(dc * sh[PAD_ROWS:, :], axis=0, keepdims=True), 0.0)
    return out


def _mconv_fwd(zx, cw, cb, col0, width):
    B, L, _ = zx.shape
    K = cw.shape[0]
    ct = _pick_tile(width, 256)
    rb = min(CONV_ROWS, L)
    nrb = L // rb
    off = col0 // ct

    def body(x_ref, w_ref, b_ref, y_ref, xp):
        xp[0:PAD_ROWS, :] = jnp.zeros((PAD_ROWS, ct), F32)
        xp[PAD_ROWS:, :] = x_ref[0]
        bias = b_ref[...]

        def blk(i, carry):
            r0 = pl.multiple_of(i * rb, rb)
            y_ref[0, pl.ds(r0, rb), :] = _silu(_conv_taps(xp, w_ref, r0, K, rb) + bias)
            return carry

        lax.fori_loop(0, nrb, blk, 0)

    return pl.pallas_call(
        body, name="mconv_fwd", grid=(B, width // ct),
        in_specs=[pl.BlockSpec((1, L, ct), lambda b, j: (b, 0, off + j)), pl.BlockSpec((K, ct), lambda b, j: (0, j)),
                  pl.BlockSpec((1, ct), lambda b, j: (0, j))],
        out_specs=pl.BlockSpec((1, L, ct), lambda b, j: (b, 0, j)),
        out_shape=jax.ShapeDtypeStruct((B, L, width), F32),
        scratch_shapes=[pltpu.VMEM((L + PAD_ROWS, ct), F32)],
        compiler_params=_cparams("parallel", "parallel"))(zx, cw, cb.reshape(1, width))


def _mconv_bwd(zx, dya, cw, cb, col0, wcol0, name):
    B, L, _ = zx.shape
    K = cw.shape[0]
    npart, _, _, wq = dya.shape
    width = npart * wq
    ct = _pick_tile(wq, 256)
    rb = min(CONV_ROWS, L)
    nrb = L // rb
    off = (col0 + wcol0) // ct
    woff = wcol0 // ct
    pq = wq // ct

    def body(x_ref, dy_ref, w_ref, b_ref, dx_ref, dw_ref, db_ref, xp, dcp):
        xp[0:PAD_ROWS, :] = jnp.zeros((PAD_ROWS, ct), F32)
        xp[PAD_ROWS:, :] = x_ref[0]
        dcp[L:, :] = jnp.zeros((PAD_ROWS, ct), F32)
        bias = b_ref[...]

        def blk1(i, carry):
            dw, db = carry
            r0 = pl.multiple_of(i * rb, rb)
            cpre = _conv_taps(xp, w_ref, r0, K, rb) + bias
            dc = dy_ref[0, 0, pl.ds(r0, rb), :] * _dsilu(cpre)
            dcp[pl.ds(r0, rb), :] = dc
            ext = xp[pl.ds(r0, rb + PAD_ROWS), :]
            return dw + _conv_dw(ext, dc, K), db + jnp.sum(dc, axis=0, keepdims=True)

        dw, db = lax.fori_loop(0, nrb, blk1, (jnp.zeros((8, ct), F32), jnp.zeros((1, ct), F32)))
        dw_ref[0] = dw
        db_ref[0] = db

        def blk2(i, carry):
            r0 = pl.multiple_of(i * rb, rb)
            dx_ref[0, pl.ds(r0, rb), :] = _conv_taps(dcp, w_ref, r0, K, rb, forward=False).astype(BF16)
            return carry

        lax.fori_loop(0, nrb, blk2, 0)

    dx, dw, db = pl.pallas_call(
        body, name=name, grid=(B, width // ct),
        in_specs=[pl.BlockSpec((1, L, ct), lambda b, j: (b, 0, off + j)),
                  pl.BlockSpec((1, 1, L, ct), lambda b, j: (j // pq, b, 0, j % pq)),
                  pl.BlockSpec((K, ct), lambda b, j: (0, woff + j)), pl.BlockSpec((1, ct), lambda b, j: (0, woff + j))],
        out_specs=[pl.BlockSpec((1, L, ct), lambda b, j: (b, 0, j)), pl.BlockSpec((1, 8, ct), lambda b, j: (b, 0, j)),
                   pl.BlockSpec((1, 1, ct), lambda b, j: (b, 0, j))],
        out_shape=[jax.ShapeDtypeStruct((B, L, width), BF16), jax.ShapeDtypeStruct((B, 8, width), F32),
                   jax.ShapeDtypeStruct((B, 1, width), F32)],
        scratch_shapes=[pltpu.VMEM((L + PAD_ROWS, ct), F32), pltpu.VMEM((L + PAD_ROWS, ct), F32)],
        compiler_params=_cparams("parallel", "parallel"))(zx, dya, cw, cb.reshape(1, -1))
    return dx, dw[:, :K, :], db


def _ffn_mid_fwd(up, cw, cb, dff):
    B, L, _ = up.shape
    K = cw.shape[0]
    ct = _pick_tile(dff, 256)
    rb = min(CONV_ROWS, L)
    nrb = L // rb
    half = dff // ct

    def body(g_ref, u_ref, wg_ref, wu_ref, bg_ref, bu_ref, a_ref, gp, upad):
        gp[0:PAD_ROWS, :] = jnp.zeros((PAD_ROWS, ct), F32)
        upad[0:PAD_ROWS, :] = jnp.zeros((PAD_ROWS, ct), F32)
        gp[PAD_ROWS:, :] = g_ref[0]
        upad[PAD_ROWS:, :] = u_ref[0]
        bg, bu = bg_ref[...], bu_ref[...]

        def blk(i, carry):
            r0 = pl.multiple_of(i * rb, rb)
            cg = _conv_taps(gp, wg_ref, r0, K, rb) + bg
            cu = _conv_taps(upad, wu_ref, r0, K, rb) + bu
            a_ref[0, pl.ds(r0, rb), :] = (_silu(cg) * cu).astype(BF16)
            return carry

        lax.fori_loop(0, nrb, blk, 0)

    xg = pl.BlockSpec((1, L, ct), lambda b, j: (b, 0, j))
    xu = pl.BlockSpec((1, L, ct), lambda b, j: (b, 0, half + j))
    wgs = pl.BlockSpec((K, ct), lambda b, j: (0, j))
    wus = pl.BlockSpec((K, ct), lambda b, j: (0, half + j))
    bgs = pl.BlockSpec((1, ct), lambda b, j: (0, j))
    bus = pl.BlockSpec((1, ct), lambda b, j: (0, half + j))
    cb2 = cb.reshape(1, 2 * dff)
    return pl.pallas_call(
        body, name="ffn_mid_fwd", grid=(B, half), in_specs=[xg, xu, wgs, wus, bgs, bus], out_specs=xg,
        out_shape=jax.ShapeDtypeStruct((B, L, dff), BF16),
        scratch_shapes=[pltpu.VMEM((L + PAD_ROWS, ct), F32), pltpu.VMEM((L + PAD_ROWS, ct), F32)],
        compiler_params=_cparams("parallel", "parallel"))(up, up, cw, cw, cb2, cb2)


def _ffn_mid_bwd(up, dact, cw, cb, dff):
    B, L, _ = up.shape
    K = cw.shape[0]
    ct = _pick_tile(dff, 256)
    rb = min(CONV_ROWS, L)
    nrb = L // rb
    half = dff // ct

    def body(g_ref, u_ref, da_ref, wg_ref, wu_ref, bg_ref, bu_ref, dx_ref, dwg_ref, dwu_ref, dbg_ref, dbu_ref,
             gp, upad, dgp, dup):
        gp[0:PAD_ROWS, :] = jnp.zeros((PAD_ROWS, ct), F32)
        upad[0:PAD_ROWS, :] = jnp.zeros((PAD_ROWS, ct), F32)
        gp[PAD_ROWS:, :] = g_ref[0]
        upad[PAD_ROWS:, :] = u_ref[0]
        dgp[L:, :] = jnp.zeros((PAD_ROWS, ct), F32)
        dup[L:, :] = jnp.zeros((PAD_ROWS, ct), F32)
        bg, bu = bg_ref[...], bu_ref[...]

        def blk1(i, carry):
            dwg, dwu, dbg, dbu = carry
            r0 = pl.multiple_of(i * rb, rb)
            cg = _conv_taps(gp, wg_ref, r0, K, rb) + bg
            cu = _conv_taps(upad, wu_ref, r0, K, rb) + bu
            da = da_ref[0, pl.ds(r0, rb), :]
            dcg = da * cu * _dsilu(cg)
            dcu = da * _silu(cg)
            dgp[pl.ds(r0, rb), :] = dcg
            dup[pl.ds(r0, rb), :] = dcu
            eg = gp[pl.ds(r0, rb + PAD_ROWS), :]
            eu = upad[pl.ds(r0, rb + PAD_ROWS), :]
            return (dwg + _conv_dw(eg, dcg, K), dwu + _conv_dw(eu, dcu, K), dbg + jnp.sum(dcg, axis=0, keepdims=True),
                    dbu + jnp.sum(dcu, axis=0, keepdims=True))

        z8 = jnp.zeros((8, ct), F32)
        z1 = jnp.zeros((1, ct), F32)
        dwg, dwu, dbg, dbu = lax.fori_loop(0, nrb, blk1, (z8, z8, z1, z1))
        dwg_ref[0] = dwg
        dwu_ref[0] = dwu
        dbg_ref[0] = dbg
        dbu_ref[0] = dbu

        def blk2(i, carry):
            r0 = pl.multiple_of(i * rb, rb)
            dx_ref[0, 0, pl.ds(r0, rb), :] = _conv_taps(dgp, wg_ref, r0, K, rb, forward=False).astype(BF16)
            dx_ref[1, 0, pl.ds(r0, rb), :] = _conv_taps(dup, wu_ref, r0, K, rb, forward=False).astype(BF16)
            return carry

        lax.fori_loop(0, nrb, blk2, 0)

    xg = pl.BlockSpec((1, L, ct), lambda b, j: (b, 0, j))
    xu = pl.BlockSpec((1, L, ct), lambda b, j: (b, 0, half + j))
    wgs = pl.BlockSpec((K, ct), lambda b, j: (0, j))
    wus = pl.BlockSpec((K, ct), lambda b, j: (0, half + j))
    bgs = pl.BlockSpec((1, ct), lambda b, j: (0, j))
    bus = pl.BlockSpec((1, ct), lambda b, j: (0, half + j))
    w8 = pl.BlockSpec((1, 8, ct), lambda b, j: (b, 0, j))
    b1 = pl.BlockSpec((1, 1, ct), lambda b, j: (b, 0, j))
    cb2 = cb.reshape(1, 2 * dff)
    pad = pltpu.VMEM((L + PAD_ROWS, ct), F32)
    dx2, dwg, dwu, dbg, dbu = pl.pallas_call(
        body, name="ffn_mid_bwd", grid=(B, half), in_specs=[xg, xu, xg, wgs, wus, bgs, bus],
        out_specs=[pl.BlockSpec((2, 1, L, ct), lambda b, j: (0, b, 0, j)), w8, w8, b1, b1],
        out_shape=[jax.ShapeDtypeStruct((2, B, L, dff), BF16)] + [jax.ShapeDtypeStruct((B, 8, dff), F32)] * 2
        + [jax.ShapeDtypeStruct((B, 1, dff), F32)] * 2,
        scratch_shapes=[pad, pad, pad, pad],
        compiler_params=_cparams("parallel", "parallel"))(up, up, dact, cw, cw, cb2, cb2)
    dw = jnp.concatenate([dwg[:, :K], dwu[:, :K]], axis=-1)
    db = jnp.concatenate([dbg, dbu], axis=-1)
    return dx2, dw, db


def _ssd_consts(hpg, W):
    P = M_HEADDIM
    E = (_iota((LANES, W), 0) == _iota((LANES, W), 1) // P).astype(BF16)
    Ebig = (_iota((LANES, hpg * LANES), 0) == _iota((LANES, hpg * LANES), 1) // LANES).astype(BF16)
    causal = _iota((M_CHUNK, M_CHUNK), 0) >= _iota((M_CHUNK, M_CHUNK), 1)
    head_of_lane = _iota((1, W), 1) // P
    return E, Ebig, causal, head_of_lane


def _ssd_chunk_fwd(xs, Bm, Cm, dtr, bias, Aneg, E, Ebig, causal, head_of_lane, hpg, st, ar_sc, ae_sc):
    pre = dtr + bias
    dt = jnp.maximum(pre, 0.0) + jnp.log(1.0 + jnp.exp(-jnp.abs(pre)))
    Ad = dt * Aneg
    a_c = _cumsum_rows(Ad)
    ar_sc[...] = a_c.T
    aexp = _dot_exact(a_c, E)
    ae_sc[...] = aexp
    alast = ae_sc[M_CHUNK - 1:M_CHUNK, :]
    dtexp = _dot_exact(dt, E)
    X = xs * dtexp
    AC = _dot_exact(a_c, Ebig)
    CB = _dot(Cm, Bm, NT)
    Xb = X.astype(BF16)
    ydiag = jnp.zeros_like(xs)
    Ls = []
    for j in range(hpg):
        Lj = jnp.where(causal, jnp.exp(jnp.minimum(AC[:, j * LANES:(j + 1) * LANES] - ar_sc[j:j + 1, :], 0.0)), 0.0)
        Ls.append(Lj)
        Yj = _dot(CB * Lj, Xb)
        ydiag = ydiag + jnp.where(head_of_lane == j, Yj, 0.0)
    ea = jnp.exp(aexp)
    yoff = ea * _dot(Cm, st)
    dec = jnp.exp(alast - aexp)
    return dict(dt=dt, a_c=a_c, aexp=aexp, alast=alast, dtexp=dtexp, X=X, Xb=Xb, CB=CB, Ls=Ls, ydiag=ydiag, ea=ea,
                yoff=yoff, dec=dec)


def _ssd_fwd(xbca, zx, dtc, bias, Aneg, Dexp, nw, hpg):
    B, L, _ = xbca.shape
    G, N, C = M_GROUPS, M_D_STATE, M_CHUNK
    W = hpg * M_HEADDIM
    DI = G * W
    NC = L // C
    LB = min(L, 4 * C)
    ncb = LB // C

    def body(xs_ref, b_ref, c_ref, z_ref, dt_ref, bias_ref, a_ref, d_ref, nw_ref, y_ref, yn_ref, st_ref, ST, ar_sc, ae_sc):
        @pl.when(pl.program_id(2) == 0)
        def _():
            ST[...] = jnp.zeros_like(ST)

        E, Ebig, causal, head_of_lane = _ssd_consts(hpg, W)
        bias_ = bias_ref[0]
        Aneg_ = a_ref[0]
        Dv = d_ref[...]
        nwv = nw_ref[...]

        def chunk(ci, carry):
            r0 = pl.multiple_of(ci * C, C)
            rows = pl.ds(r0, C)
            xs = xs_ref[0, rows, :]
            Bm = b_ref[0, rows, :]
            Cm = c_ref[0, rows, :]
            st = ST[...]
            st_ref[0, 0, ci] = st
            f = _ssd_chunk_fwd(xs, Bm, Cm, dt_ref[0, 0, ci], bias_, Aneg_, E, Ebig, causal, head_of_lane, hpg, st, ar_sc, ae_sc)
            y = f["ydiag"] + f["yoff"] + xs * Dv
            ST[...] = st * jnp.exp(f["alast"]) + _dot(Bm, f["X"] * f["dec"], TN)
            yg = y * _silu(z_ref[0, rows, :])
            rstd = lax.rsqrt(jnp.mean(yg * yg, axis=-1, keepdims=True) + NORM_EPS)
            y_ref[0, rows, :] = y
            yn_ref[0, rows, :] = (yg * rstd * nwv).astype(BF16)
            return carry

        lax.fori_loop(0, ncb, chunk, 0)

    xw = pl.BlockSpec((1, LB, W), lambda b, g, s: (b, s, g))
    bsp = pl.BlockSpec((1, LB, N), lambda b, g, s: (b, s, DI // N + g))
    csp = pl.BlockSpec((1, LB, N), lambda b, g, s: (b, s, DI // N + G + g))
    dts = pl.BlockSpec((1, 1, ncb, C, LANES), lambda b, g, s: (b, g, s, 0, 0))
    hv = pl.BlockSpec((1, 1, LANES), lambda b, g, s: (g, 0, 0))
    wv = pl.BlockSpec((1, W), lambda b, g, s: (0, g))
    sts = pl.BlockSpec((1, 1, ncb, N, W), lambda b, g, s: (b, g, s, 0, 0))
    return pl.pallas_call(
        body, name="ssd_fwd", grid=(B, G, L // LB), in_specs=[xw, bsp, csp, xw, dts, hv, hv, wv, wv],
        out_specs=[xw, xw, sts],
        out_shape=[jax.ShapeDtypeStruct((B, L, DI), F32), jax.ShapeDtypeStruct((B, L, DI), BF16),
                   jax.ShapeDtypeStruct((B, G, NC, N, W), F32)],
        scratch_shapes=[pltpu.VMEM((N, W), F32), pltpu.VMEM((LANES, C), F32), pltpu.VMEM((C, W), F32)],
        compiler_params=_cparams("parallel", "parallel", "arbitrary"))(xbca, xbca, xbca, zx, dtc, bias, Aneg, Dexp, nw)


def _ssd_bwd(xbca, zx, dtc, ypre, dyn, st, bias, Aneg, Dexp, nw, hpg):
    B, L, _ = xbca.shape
    G, N, C = M_GROUPS, M_D_STATE, M_CHUNK
    W = hpg * M_HEADDIM
    DI = G * W
    NC = L // C
    LB = min(L, 4 * C)
    ncb = LB // C
    nsb = L // LB

    def body(xs_ref, b_ref, c_ref, z_ref, dt_ref, y_ref, dyn_ref, st_ref, bias_ref, a_ref, d_ref, nw_ref,
             dxs_ref, dbc_ref, dz_ref, ddt_ref, dnw_ref, dd_ref, da_ref, dbias_ref, DST, ar_sc, ae_sc):
        @pl.when(pl.program_id(2) == 0)
        def _():
            DST[...] = jnp.zeros_like(DST)
            dnw_ref[...] = jnp.zeros_like(dnw_ref)
            dd_ref[...] = jnp.zeros_like(dd_ref)
            da_ref[...] = jnp.zeros_like(da_ref)
            dbias_ref[...] = jnp.zeros_like(dbias_ref)

        E, Ebig, causal, head_of_lane = _ssd_consts(hpg, W)
        bias_ = bias_ref[0]
        Aneg_ = a_ref[0]
        Dv = d_ref[...]
        nwv = nw_ref[...]
        lane = _iota((1, LANES), 1)
        subl = _iota((LANES, 1), 0)
        lastrow = _iota((C, W), 0) == C - 1

        def chunk(i, carry):
            ci = ncb - 1 - i
            r0 = pl.multiple_of(ci * C, C)
            rows = pl.ds(r0, C)
            xs = xs_ref[0, rows, :]
            Bm = b_ref[0, rows, :]
            Cm = c_ref[0, rows, :]
            zr = z_ref[0, rows, :]
            dtr = dt_ref[0, 0, ci]
            st_in = st_ref[0, 0, ci]
            dst = DST[...]
            f = _ssd_chunk_fwd(xs, Bm, Cm, dtr, bias_, Aneg_, E, Ebig, causal, head_of_lane, hpg, st_in, ar_sc, ae_sc)
            X, Xb, dec, ea, CB = f["X"], f["Xb"], f["dec"], f["ea"], f["CB"]
            y = y_ref[0, rows, :]
            sz = _silu(zr)
            yg = y * sz
            rstd = lax.rsqrt(jnp.mean(yg * yg, axis=-1, keepdims=True) + NORM_EPS)
            yh = yg * rstd
            dyn_ = dyn_ref[0, rows, :]
            dnw_ref[0, 0] += jnp.sum(dyn_ * yh, axis=0, keepdims=True)
            dyh = dyn_ * nwv
            dyg = rstd * (dyh - yh * jnp.mean(dyh * yh, axis=-1, keepdims=True))
            dz_ref[0, rows, :] = (dyg * y * _dsilu(zr)).astype(BF16)
            dy = dyg * sz
            dd_ref[0, 0] += jnp.sum(dy * xs, axis=0, keepdims=True)
            dxs = dy * Dv
            dYo = dy * ea
            daexp = dy * f["yoff"]
            dCm = _dot(dYo, st_in, NT)
            dst_in = _dot(Cm, dYo, TN)
            dyb = dy.astype(BF16)
            dX = jnp.zeros_like(xs)
            dCB = jnp.zeros((C, C), F32)
            da_col = jnp.zeros((C, LANES), F32)
            da_row = jnp.zeros((LANES, C), F32)
            for j in range(hpg):
                Lj = f["Ls"][j]
                Gj = CB * Lj
                dYj = jnp.where(head_of_lane == j, dyb, jnp.zeros_like(dyb))
                dX = dX + _dot(Gj, dYj, TN)
                dGj = _dot(dYj, Xb, NT)
                dCB = dCB + dGj * Lj
                Wj = dGj * Gj
                da_col = da_col + jnp.sum(Wj, axis=1, keepdims=True) * (lane == j).astype(F32)
                da_row = da_row + (subl == j).astype(F32) * jnp.sum(Wj, axis=0, keepdims=True)
            dCm = dCm + _dot(dCB, Bm)
            dBm = _dot(dCB, Cm, TN)
            ela = jnp.exp(f["alast"])
            dalast = jnp.sum(dst * st_in, axis=0, keepdims=True) * ela
            DST[...] = dst * ela + dst_in
            dXd = _dot(Bm, dst)
            dBm = dBm + _dot(X * dec, dst, NT)
            dX = dX + dXd * dec
            ddec = dXd * X * dec
            dalast = dalast + jnp.sum(ddec, axis=0, keepdims=True)
            daexp = daexp - ddec + jnp.where(lastrow, dalast, 0.0)
            dxs = dxs + dX * f["dtexp"]
            ddtexp = dX * xs
            ddt = _dot_exact(ddtexp, E, NT, passes=2)
            da_c = _dot_exact(daexp, E, NT, passes=2) + da_col - da_row.T
            dAd = _cumsum_rows(da_c, reverse=True)
            ddt = ddt + dAd * Aneg_
            da_ref[0, 0] += jnp.sum(dAd * f["dt"], axis=0, keepdims=True) * Aneg_
            ddtr = ddt * jax.nn.sigmoid(dtr + bias_)
            dbias_ref[0, 0] += jnp.sum(ddtr, axis=0, keepdims=True)
            ddt_ref[0, 0, ci] = ddtr
            dxs_ref[0, rows, :] = dxs
            dbc_ref[0, 0, rows, :] = dBm
            dbc_ref[1, 0, rows, :] = dCm
            return carry

        lax.fori_loop(0, ncb, chunk, 0)

    def rev(s):
        return nsb - 1 - s

    xw = pl.BlockSpec((1, LB, W), lambda b, g, s: (b, rev(s), g))
    bsp = pl.BlockSpec((1, LB, N), lambda b, g, s: (b, rev(s), DI // N + g))
    csp = pl.BlockSpec((1, LB, N), lambda b, g, s: (b, rev(s), DI // N + G + g))
    gsp = pl.BlockSpec((1, LB, N), lambda b, g, s: (b, rev(s), g))
    dts = pl.BlockSpec((1, 1, ncb, C, LANES), lambda b, g, s: (b, g, rev(s), 0, 0))
    hv = pl.BlockSpec((1, 1, LANES), lambda b, g, s: (g, 0, 0))
    wv = pl.BlockSpec((1, W), lambda b, g, s: (0, g))
    sts = pl.BlockSpec((1, 1, ncb, N, W), lambda b, g, s: (b, g, rev(s), 0, 0))
    accw = pl.BlockSpec((1, 1, 1, W), lambda b, g, s: (b, g, 0, 0))
    acch = pl.BlockSpec((1, 1, 1, LANES), lambda b, g, s: (b, g, 0, 0))
    return pl.pallas_call(
        body, name="ssd_bwd", grid=(B, G, nsb), in_specs=[xw, bsp, csp, xw, dts, xw, xw, sts, hv, hv, wv, wv],
        out_specs=[xw, pl.BlockSpec((2, 1, LB, N), lambda b, g, s: (0, b, rev(s), g)), xw, dts, accw, accw, acch, acch],
        out_shape=[jax.ShapeDtypeStruct((B, L, DI), F32), jax.ShapeDtypeStruct((2, B, L, G * N), F32),
                   jax.ShapeDtypeStruct((B, L, DI), BF16),
                   jax.ShapeDtypeStruct((B, G, NC, C, LANES), F32), jax.ShapeDtypeStruct((B, G, 1, W), F32),
                   jax.ShapeDtypeStruct((B, G, 1, W), F32), jax.ShapeDtypeStruct((B, G, 1, LANES), F32),
                   jax.ShapeDtypeStruct((B, G, 1, LANES), F32)],
        scratch_shapes=[pltpu.VMEM((N, W), F32), pltpu.VMEM((LANES, C), F32), pltpu.VMEM((C, W), F32)],
        compiler_params=_cparams("parallel", "parallel", "arbitrary"))(
            xbca, xbca, xbca, zx, dtc, ypre, dyn, st, bias, Aneg, Dexp, nw)


def _adamw(w, g, m, v, name):
    shape = w.shape
    n = w.size
    cols = shape[-1]
    rows = n // cols
    tr = rows
    for cand in (512, 256, 128, 64, 32, 16, 8):
        if rows % cand == 0 and cand * cols * 4 <= 1024 * 1024:
            tr = cand
            break
    c1 = 1.0 / (1.0 - ADAM_B1 ** ADAM_STEP)
    c2 = 1.0 / (1.0 - ADAM_B2 ** ADAM_STEP)

    def body(w_ref, g_ref, m_ref, v_ref, d_ref, mo_ref, vo_ref):
        g_ = g_ref[...]
        mn = ADAM_B1 * m_ref[...] + (1.0 - ADAM_B1) * g_
        vn = ADAM_B2 * v_ref[...] + (1.0 - ADAM_B2) * (g_ * g_)
        d_ref[...] = -ADAM_LR * ((mn * c1) / (jnp.sqrt(vn * c2) + ADAM_EPS) + ADAM_WD * w_ref[...])
        mo_ref[...] = mn
        vo_ref[...] = vn

    spec = pl.BlockSpec((tr, cols), lambda i: (i, 0))
    r2 = lambda a: a.reshape(rows, cols)
    outs = pl.pallas_call(
        body, name=name, grid=(rows // tr,), in_specs=[spec] * 4, out_specs=[spec] * 3,
        out_shape=[jax.ShapeDtypeStruct((rows, cols), F32)] * 3,
        compiler_params=_cparams("parallel"))(r2(w), r2(g), r2(m), r2(v))
    return tuple(o.reshape(shape) for o in outs)


def _lower_bounds(lb_logits):
    p = jax.nn.softmax(lb_logits.astype(F32), axis=0)
    return jnp.cumsum(p, axis=0) - p[0]


def _pad_cols(a, n):
    return a if a.shape[-1] == n else jnp.pad(a, [(0, 0)] * (a.ndim - 1) + [(0, n - a.shape[-1])])


def _heads_to_lanes(a, G, hpg):
    return _pad_cols(a.reshape(G, 1, hpg), LANES)


def _local_step(x, target, P, fetch, emit):
    B, L, D = x.shape
    T = B * L
    depth = P["mix_norm"].shape[0]
    H = D // HGRN_DK
    F_ = H * HGRN_DK
    DI = P["m_norm"].shape[1]
    G, N = M_GROUPS, M_D_STATE
    MH = DI // M_HEADDIM
    hpg = MH // G
    assert hpg <= 8
    W = hpg * M_HEADDIM
    CD = DI + 2 * G * N
    MIN = DI + CD + MH
    MPAD = -(-MIN // LANES) * LANES
    dff = P["f_conv_b"].shape[1] // 2
    NC = L // M_CHUNK
    lbs = _lower_bounds(P["hgrn_lb_logits"])

    h = x.reshape(T, D)
    saved = []
    for i in range(depth):
        j = i // 2
        Wl = fetch(i, h)
        s = {"h_in": h, "W": Wl}
        u = _rmsnorm_fwd(h, P["mix_norm"][i], "mix_norm_fwd")
        s["u"] = u
        if i % 2 == 0:
            proj = _matmul(u, Wl["mix_in"], name="hgrn_in_fwd").reshape(B, L, 4 * F_)
            o, on, st = _hgrn_fwd(proj, lbs[j].reshape(1, F_), P["hgrn_gnorm"][j].reshape(1, HGRN_DK), H)
            h = _matmul(on.reshape(T, F_), Wl["mix_out"], res=h, name="hgrn_out_fwd")
            s.update(proj=proj, o=o, on=on, st=st)
        else:
            zx = _matmul(u, Wl["mix_in"], tb=True, tn=1152, name="m_in_fwd").reshape(B, L, MPAD)
            xbca = _mconv_fwd(zx, P["m_conv_w"][j], P["m_conv_b"][j], DI, CD)
            dtr = zx[:, :, DI + CD:DI + CD + MH].reshape(B, NC, M_CHUNK, G, hpg).transpose(0, 3, 1, 2, 4)
            dtc = _pad_cols(dtr, LANES)
            bias = _heads_to_lanes(P["m_dt_bias"][j], G, hpg)
            Aneg = _heads_to_lanes(-jnp.exp(P["m_A_log"][j]), G, hpg)
            Dexp = jnp.repeat(P["m_D"][j], M_HEADDIM).reshape(1, DI)
            nw = P["m_norm"][j].reshape(1, DI)
            ypre, yn, st = _ssd_fwd(xbca, zx, dtc, bias, Aneg, Dexp, nw, hpg)
            h = _matmul(yn.reshape(T, DI), Wl["mix_out"], res=h, name="m_out_fwd")
            s.update(zx=zx, xbca=xbca, dtc=dtc, bias=bias, Aneg=Aneg, Dexp=Dexp, nw=nw, ypre=ypre, yn=yn, st=st)
        s["h_mid"] = h
        u2 = _rmsnorm_fwd(h, P["ffn_norm"][i], "ffn_norm_fwd")
        up = _matmul(u2, Wl["f_w_up"], name="ffn_up_fwd").reshape(B, L, 2 * dff)
        act = _ffn_mid_fwd(up, P["f_conv_w"][i], P["f_conv_b"][i], dff)
        h = _matmul(act.reshape(T, dff), Wl["f_w_down"], res=h, name="ffn_down_fwd")
        s.update(u2=u2, up=up, act=act)
        saved.append(s)

    loss, dh, dhb, d_final = _loss_head(h, P["final_norm"], target.reshape(T, D))

    g = {k: [None] * P[k].shape[0] for k in ("mix_norm", "ffn_norm", "hgrn_gnorm", "m_conv_w", "m_conv_b", "m_dt_bias",
                                              "m_A_log", "m_D", "m_norm", "f_conv_w", "f_conv_b")}
    dlbs = [None] * lbs.shape[0]
    for i in reversed(range(depth)):
        j = i // 2
        s = saved[i]
        Wl = s["W"]
        gm = {}

        def dw(key, a, b, name, **kw):
            gm[key] = _matmul(a, b, ta=True, out_dtype=BF16, name=name, **kw)

        dact = _matmul(dhb, Wl["f_w_down"], tb=True, name="ffn_down_dx").reshape(B, L, dff)
        dw("f_w_down", s["act"].reshape(T, dff), dhb, "ffn_down_dw")
        dup, dcw, dcb = _ffn_mid_bwd(s["up"], dact, P["f_conv_w"][i], P["f_conv_b"][i], dff)
        g["f_conv_w"][i] = jnp.sum(dcw, axis=0)
        g["f_conv_b"][i] = jnp.sum(dcb, axis=(0, 1))
        dup = dup.reshape(2, T, dff)
        dw("f_w_up", s["u2"], dup, "ffn_up_dw", b_parts=True)
        du2 = _matmul(dup, Wl["f_w_up"], a_parts=True, tb=True, name="ffn_up_dx")
        dh, dhb, g["ffn_norm"][i] = _rmsnorm_bwd(s["h_mid"], P["ffn_norm"][i], du2, dh, "ffn_norm_bwd")
        if i % 2 == 0:
            don = _matmul(dhb, Wl["mix_out"], tb=True, name="hgrn_out_dx").reshape(B, L, F_)
            dw("mix_out", s["on"].reshape(T, F_), dhb, "hgrn_out_dw")
            dproj, dlb, dgn = _hgrn_bwd(s["proj"], s["o"], don, s["st"], lbs[j].reshape(1, F_),
                                        P["hgrn_gnorm"][j].reshape(1, HGRN_DK), H)
            dlbs[j] = jnp.sum(dlb, axis=(0, 1))
            g["hgrn_gnorm"][j] = jnp.sum(dgn, axis=(0, 1, 2))
            dproj = dproj.reshape(4, T, F_)
            dw("mix_in", s["u"], dproj, "hgrn_in_dw", b_parts=True)
            du = _matmul(dproj, Wl["mix_in"], a_parts=True, tb=True, name="hgrn_in_dx")
        else:
            dyn = _matmul(dhb, Wl["mix_out"], tb=True, name="m_out_dx").reshape(B, L, DI)
            dw("mix_out", s["yn"].reshape(T, DI), dhb, "m_out_dw")
            dxs, dbc, dz, ddt, dnw, dD, dA, dbias = _ssd_bwd(s["xbca"], s["zx"], s["dtc"], s["ypre"], dyn, s["st"],
                                                             s["bias"], s["Aneg"], s["Dexp"], s["nw"], hpg)
            g["m_norm"][j] = jnp.sum(dnw, axis=(0, 2)).reshape(DI)
            g["m_D"][j] = jnp.sum(dD, axis=(0, 2)).reshape(MH, M_HEADDIM).sum(axis=-1)
            g["m_A_log"][j] = jnp.sum(dA, axis=(0, 2))[:, :hpg].reshape(MH)
            g["m_dt_bias"][j] = jnp.sum(dbias, axis=(0, 2))[:, :hpg].reshape(MH)
            cw, cb = P["m_conv_w"][j], P["m_conv_b"][j]
            dxx, dcw_x, dcb_x = _mconv_bwd(s["zx"], dxs[None], cw, cb, DI, 0, "mconv_bwd_x")
            dxb, dcw_b, dcb_b = _mconv_bwd(s["zx"], dbc, cw, cb, DI, DI, "mconv_bwd_bc")
            g["m_conv_w"][j] = jnp.concatenate([jnp.sum(dcw_x, axis=0), jnp.sum(dcw_b, axis=0)], axis=-1)
            g["m_conv_b"][j] = jnp.concatenate([jnp.sum(dcb_x, axis=(0, 1)), jnp.sum(dcb_b, axis=(0, 1))], axis=-1)
            ddt_t = _pad_cols(ddt[..., :hpg].transpose(0, 2, 3, 1, 4).reshape(T, MH), MPAD - DI - CD).astype(BF16)
            pieces = [(dz.reshape(T, DI), 0), (dxx.reshape(T, DI), DI), (dxb.reshape(T, 2 * G * N), 2 * DI), (ddt_t, DI + CD)]
            du = None
            gm["mix_in"] = lax.empty((MPAD, D), BF16)
            for n_, (piece, off) in enumerate(pieces):
                gm["mix_in"] = _matmul(piece, s["u"], ta=True, out_dtype=BF16, out=gm["mix_in"], out_off=off,
                                       name="m_in_dw%d" % n_)
                du = _matmul(piece, Wl["mix_in"], b_off=off, res=du, name="m_in_dx%d" % n_)
        dep = emit(i, gm)
        dh, dhb, g["mix_norm"][i] = _rmsnorm_bwd(s["h_in"], P["mix_norm"][i], du, dh, "mix_norm_bwd", dep=dep)

    grads = {k: jnp.stack(vs) for k, vs in g.items()}
    grads["final_norm"] = d_final
    _, lb_vjp = jax.vjp(_lower_bounds, P["hgrn_lb_logits"])
    grads["hgrn_lb_logits"] = lb_vjp(jnp.stack(dlbs))[0]
    return loss, dh.reshape(B, L, D), grads


ANY = pl.BlockSpec(memory_space=pl.ANY)
N_CHIPS = 4
N_DEV = 8


def _place():
    x, y, c = lax.axis_index("x"), lax.axis_index("y"), lax.axis_index("c")
    sibling = (x, y, 1 - c)
    chips = [(1 - x, y), (x, 1 - y), (1 - x, 1 - y)]
    return x, y, c, sibling, chips


def _remote(src, dst, send_sem, recv_sem, to):
    return pltpu.make_async_remote_copy(src_ref=src, dst_ref=dst, send_sem=send_sem, recv_sem=recv_sem, device_id=to,
                                        device_id_type=MESH)


KIND_AXIS = {"hgrn_w_in": "col", "f_w_up": "col", "m_w_in_t": "row", "hgrn_w_out": "row", "m_w_out": "row", "f_w_down": "row"}
KINDS = tuple(KIND_AXIS)
PEER_MASKS = (2, 1, 3)
ALL = slice(None)


def _chip_win(axis, cw, s):
    return (ALL, slice(s * cw, (s + 1) * cw)) if axis == "col" else (slice(s * cw, (s + 1) * cw), ALL)


def _half_win(axis, rows, cols, h):
    return (slice(h * rows // 2, (h + 1) * rows // 2), ALL) if axis == "col" else (ALL, slice(h * cols // 2, (h + 1) * cols // 2))


def _per_place(fn):
    x, y, c, sibling, chips = _place()
    chip = 2 * x + y
    for s in range(N_CHIPS):
        for cc in range(2):
            @pl.when(jnp.logical_and(chip == s, c == cc))
            def _():
                fn(s, cc, c, sibling, chips)


HBM = pl.BlockSpec(memory_space=pltpu.HBM)
SEM = pl.BlockSpec(memory_space=pltpu.SEMAPHORE)
EFFECT = pltpu.SideEffectType.DATAFLOW_SIDE_EFFECTING


def _cell(axis, rows, cols, cw, s, h):
    if axis == "col":
        return (slice(h * rows // 2, (h + 1) * rows // 2), slice(s * cw, (s + 1) * cw))
    return (slice(s * cw, (s + 1) * cw), slice(h * cols // 2, (h + 1) * cols // 2))


def _in_hbm(a):
    return pltpu.with_memory_space_constraint(a, pltpu.HBM)


def _stage_shard(kind, shard, layer, chip, pad_rows=0):
    _, R, C = shard.shape
    axis = KIND_AXIS[kind]
    tr, tc = _row_tile(R), _pick_tile(C, 2048)
    nr, nc = R // tr, C // tc
    full = (R, N_CHIPS * C) if axis == "col" else (N_CHIPS * R + pad_rows, C)

    def body(s_ref, x_ref, o_ref):
        o_ref[...] = x_ref[...].astype(BF16)

    if axis == "col":
        dst = pl.BlockSpec((tr, tc), lambda i, j, s_ref: (i, s_ref[0] * nc + j))
    else:
        dst = pl.BlockSpec((tr, tc), lambda i, j, s_ref: (s_ref[0] * nr + i, j))
    grid_spec = pltpu.PrefetchScalarGridSpec(
        num_scalar_prefetch=1, grid=(nr, nc),
        in_specs=[pl.BlockSpec((None, tr, tc), lambda i, j, s_ref: (layer, i, j))], out_specs=dst)
    out = pl.pallas_call(
        body, name="stage_" + kind, grid_spec=grid_spec, out_shape=jax.ShapeDtypeStruct(full, BF16),
        compiler_params=_cparams("parallel", "parallel"))(chip.reshape(1).astype(jnp.int32), shard)
    if pad_rows:
        rows0 = N_CHIPS * R
        pr = math.gcd(rows0, pad_rows)

        def zero_body(x_ref, o_ref):
            o_ref[...] = jnp.zeros_like(o_ref)

        out = pl.pallas_call(
            zero_body, name="zero_pad_" + kind, grid=(pad_rows // pr,), in_specs=[ANY],
            out_specs=pl.BlockSpec((pr, C), lambda i: (rows0 // pr + i, 0)), out_shape=jax.ShapeDtypeStruct(full, BF16),
            input_output_aliases={0: 0}, compiler_params=_cparams("parallel"))(out)
    return out


def _gather_start(items, mats, cws):
    n = len(items)

    def body(*refs):
        send_sems, recv_sems, token = refs[n], refs[n + 1], refs[-1]
        m = refs[n + 2:2 * n + 2]

        def run(s, cc, c, sibling, chips):
            for q, (k, _) in enumerate(items):
                r, c_ = m[q].shape
                mine = m[q].at[_cell(KIND_AXIS[k], r, c_, cws[k], s, cc)]
                for j, (px, py) in enumerate(chips):
                    _remote(mine, mine, send_sems.at[3 * q + j], recv_sems.at[3 * q + j], (px, py, c)).start()

        _per_place(run)
        token[...] = jnp.zeros_like(token)

    outs = pl.pallas_call(
        body, name="gather_start", in_specs=[HBM] * n,
        out_specs=[SEM, SEM] + [HBM] * n + [pl.BlockSpec(memory_space=pltpu.VMEM)],
        out_shape=[pltpu.SemaphoreType.DMA((3 * n,)), pltpu.SemaphoreType.DMA((3 * n,))]
        + [pltpu.HBM(a.shape, a.dtype) for a in mats] + [jax.ShapeDtypeStruct((8, LANES), F32)],
        input_output_aliases={q: 2 + q for q in range(n)},
        compiler_params=pltpu.CompilerParams(has_side_effects=EFFECT),
    )(*[_in_hbm(a) for a in mats])
    return outs[0], outs[1], list(outs[2:2 + n]), outs[-1]


def _gather_wait(items, idx, mats, send_sems, recv_sems, cws, after, name):
    n = len(idx)

    def body(*refs):
        m = refs[:n]
        s_sems, r_sems = refs[n], refs[n + 1]

        def run(s, cc, c, sibling, chips):
            for a, q in enumerate(idx):
                k = items[q][0]
                r, c_ = m[a].shape
                mine = m[a].at[_cell(KIND_AXIS[k], r, c_, cws[k], s, cc)]
                for j, (px, py) in enumerate(chips):
                    theirs = m[a].at[_cell(KIND_AXIS[k], r, c_, cws[k], s ^ PEER_MASKS[j], cc)]
                    cp = _remote(mine, theirs, s_sems.at[3 * q + j], r_sems.at[3 * q + j], (px, py, c))
                    cp.wait_send()
                    cp.wait_recv()

        _per_place(run)

    outs = pl.pallas_call(
        body, name=name, in_specs=[HBM] * n + [SEM, SEM, ANY], out_specs=[HBM] * n,
        out_shape=[pltpu.HBM(a.shape, a.dtype) for a in mats], input_output_aliases={a: a for a in range(n)},
        compiler_params=pltpu.CompilerParams(has_side_effects=EFFECT),
    )(*mats, send_sems, recv_sems, after)
    return list(outs)


def _forward_halves(kinds, mats, cws, name):
    n = len(mats)

    def body(*refs):
        m = refs[n:2 * n]
        send_sems, recv_sems = refs[2 * n:]

        def run(s, cc, c, sibling, chips):
            cps = []
            for a, k in enumerate(kinds):
                r, c_ = m[a].shape
                for j in range(3):
                    have = m[a].at[_cell(KIND_AXIS[k], r, c_, cws[k], s ^ PEER_MASKS[j], cc)]
                    cps.append(_remote(have, have, send_sems.at[3 * a + j], recv_sems.at[3 * a + j], sibling))
            for cp in cps:
                cp.start()
            for cp in cps:
                cp.wait()

        _per_place(run)

    outs = pl.pallas_call(
        body, name=name, in_specs=[ANY] * n, out_specs=[ANY] * n,
        out_shape=[jax.ShapeDtypeStruct(a.shape, a.dtype) for a in mats], input_output_aliases={a: a for a in range(n)},
        scratch_shapes=[pltpu.SemaphoreType.DMA((3 * n,)), pltpu.SemaphoreType.DMA((3 * n,))],
    )(*mats)
    return list(outs)


def _swap_halves(kinds, gms, name):
    n = len(gms)
    half_shapes = [(g.shape[0] // 2, g.shape[1]) if KIND_AXIS[k] == "col" else (g.shape[0], g.shape[1] // 2)
                   for k, g in zip(kinds, gms)]

    def body(*refs):
        g, ra = refs[:n], refs[n:2 * n]
        send_sems, recv_sems = refs[2 * n:]

        def run(s, cc, c, sibling, chips):
            cps = []
            for a, k in enumerate(kinds):
                r, c_ = g[a].shape
                cps.append(_remote(g[a].at[_half_win(KIND_AXIS[k], r, c_, 1 - cc)], ra[a], send_sems.at[a], recv_sems.at[a],
                                   sibling))
            for cp in cps:
                cp.start()
            for cp in cps:
                cp.wait()

        _per_place(run)

    outs = pl.pallas_call(
        body, name=name, in_specs=[ANY] * n, out_specs=[ANY] * n,
        out_shape=[jax.ShapeDtypeStruct(hs, BF16) for hs in half_shapes],
        scratch_shapes=[pltpu.SemaphoreType.DMA((n,)), pltpu.SemaphoreType.DMA((n,))],
    )(*gms)
    return list(outs)


def _win_shape(kind, pa, cw):
    return (pa.shape[0], cw) if KIND_AXIS[kind] == "col" else (cw, pa.shape[1])


def _scatter_start(kinds, pas, cws, name):
    n = len(pas)
    lands = [lax.empty((3,) + _win_shape(k, p, cws[k]), BF16) for k, p in zip(kinds, pas)]

    def body(*refs):
        send_sems, recv_sems, token = refs[2 * n], refs[2 * n + 1], refs[-1]
        p, rb = refs[2 * n + 2:3 * n + 2], refs[3 * n + 2:4 * n + 2]

        def run(s, cc, c, sibling, chips):
            for a, k in enumerate(kinds):
                for j, (px, py) in enumerate(chips):
                    src = p[a].at[_chip_win(KIND_AXIS[k], cws[k], s ^ PEER_MASKS[j])]
                    _remote(src, rb[a].at[j], send_sems.at[3 * a + j], recv_sems.at[3 * a + j], (px, py, c)).start()

        _per_place(run)
        token[...] = jnp.zeros_like(token)

    outs = pl.pallas_call(
        body, name=name, in_specs=[HBM] * (2 * n),
        out_specs=[SEM, SEM] + [HBM] * (2 * n) + [pl.BlockSpec(memory_space=pltpu.VMEM)],
        out_shape=[pltpu.SemaphoreType.DMA((3 * n,)), pltpu.SemaphoreType.DMA((3 * n,))]
        + [pltpu.HBM(a.shape, a.dtype) for a in pas + lands] + [jax.ShapeDtypeStruct((8, LANES), F32)],
        input_output_aliases={q: 2 + q for q in range(2 * n)},
        compiler_params=pltpu.CompilerParams(has_side_effects=EFFECT),
    )(*[_in_hbm(a) for a in pas + lands])
    return outs[0], outs[1], list(outs[2:2 + n]), list(outs[2 + n:2 + 2 * n]), outs[-1]


def _scatter_wait(kinds, pas, lands, send_sems, recv_sems, cws, after, name):
    n = len(pas)

    def body(*refs):
        p, rb = refs[:n], refs[n:2 * n]
        s_sems, r_sems = refs[2 * n], refs[2 * n + 1]

        def run(s, cc, c, sibling, chips):
            for a, k in enumerate(kinds):
                for j, (px, py) in enumerate(chips):
                    src = p[a].at[_chip_win(KIND_AXIS[k], cws[k], s ^ PEER_MASKS[j])]
                    cp = _remote(src, rb[a].at[j], s_sems.at[3 * a + j], r_sems.at[3 * a + j], (px, py, c))
                    cp.wait_send()
                    cp.wait_recv()

        _per_place(run)

    outs = pl.pallas_call(
        body, name=name, in_specs=[HBM] * (2 * n) + [SEM, SEM, ANY], out_specs=[HBM] * (2 * n),
        out_shape=[pltpu.HBM(a.shape, a.dtype) for a in pas + lands], input_output_aliases={a: a for a in range(2 * n)},
        compiler_params=pltpu.CompilerParams(has_side_effects=EFFECT),
    )(*pas, *lands, send_sems, recv_sems, after)
    return list(outs[:n]), list(outs[n:])


def _share_halves(g):
    nq = len(KINDS)

    def body(*refs):
        out = dict(zip(KINDS, refs[nq:2 * nq]))
        send_sems, recv_sems = refs[2 * nq:]

        def run(s, cc, c, sibling, chips):
            cps = []
            for q, k in enumerate(KINDS):
                _, r, c_ = out[k].shape
                mine = out[k].at[(ALL,) + _half_win(KIND_AXIS[k], r, c_, cc)]
                cps.append(_remote(mine, mine, send_sems.at[q], recv_sems.at[q], sibling))
            for cp in cps:
                cp.start()
            for cp in cps:
                cp.wait()

        _per_place(run)

    outs = pl.pallas_call(
        body, name="share_halves", in_specs=[ANY] * nq, out_specs=[ANY] * nq,
        out_shape=[jax.ShapeDtypeStruct(g[k].shape, F32) for k in KINDS],
        input_output_aliases={q: q for q in range(nq)},
        scratch_shapes=[pltpu.SemaphoreType.DMA((nq,)), pltpu.SemaphoreType.DMA((nq,))],
    )(*[g[k] for k in KINDS])
    return dict(zip(KINDS, outs))


def _all_gather_small(xs, name):
    m_per, n = xs.shape

    def body(x_ref, out_ref, send_sems, recv_sems, local_sem):
        x, y, c, sibling, chips = _place()
        me = (x, y, c)

        def rows(px, py, pc):
            return out_ref.at[pl.ds((4 * px + 2 * py + pc) * m_per, m_per), :]

        def copy(k, block, to, src=None):
            return _remote(rows(*block) if src is None else src, rows(*block), send_sems.at[k], recv_sems.at[k], to)

        mine = pltpu.make_async_copy(x_ref, rows(*me), local_sem)
        mine.start()
        first = [copy(0, me, sibling, src=x_ref)]
        first += [copy(1 + j, me, (*chip, c), src=x_ref) for j, chip in enumerate(chips)]
        for cp in first:
            cp.start()
        passed = [copy(4 + j, (*chip, c), sibling) for j, chip in enumerate(chips)]
        for j, chip in enumerate(chips):
            copy(1 + j, (*chip, c), me).wait_recv()
            passed[j].start()
        copy(0, sibling, me).wait_recv()
        for j, chip in enumerate(chips):
            copy(4 + j, (*chip, 1 - c), me).wait_recv()
        for cp in first + passed:
            cp.wait_send()
        mine.wait()

    vm = pl.BlockSpec(memory_space=pltpu.VMEM)
    return pl.pallas_call(
        body, name=name, in_specs=[vm], out_specs=vm, out_shape=jax.ShapeDtypeStruct((N_DEV * m_per, n), xs.dtype),
        scratch_shapes=[pltpu.SemaphoreType.DMA((7,)), pltpu.SemaphoreType.DMA((7,)), pltpu.SemaphoreType.DMA],
        compiler_params=pltpu.CompilerParams(vmem_limit_bytes=VMEM_LIMIT_BYTES),
    )(xs)


def _row_tile(rows, cap=256):
    for mult in (16, 8):
        best = None
        t = mult
        while t <= min(rows, cap):
            if rows % t == 0:
                best = t
            t += mult
        if best is not None:
            return best
    raise ValueError(rows)


def _add_sibling(kind, g, ra, core):
    R, C = ra.shape
    axis = KIND_AXIS[kind]
    tr, tc = _row_tile(R), _pick_tile(C, 2048)
    nr, nc = R // tr, C // tc

    def body(c_ref, a_ref, b_ref, o_ref):
        o_ref[...] = (a_ref[...].astype(F32) + b_ref[...].astype(F32)).astype(o_ref.dtype)

    if axis == "col":
        own = pl.BlockSpec((tr, tc), lambda i, j, c_ref: (c_ref[0] * nr + i, j))
    else:
        own = pl.BlockSpec((tr, tc), lambda i, j, c_ref: (i, c_ref[0] * nc + j))
    same = pl.BlockSpec((tr, tc), lambda i, j, c_ref: (i, j))
    grid_spec = pltpu.PrefetchScalarGridSpec(num_scalar_prefetch=1, grid=(nr, nc), in_specs=[own, same], out_specs=same)
    return pl.pallas_call(
        body, name="add_sibling_" + kind, grid_spec=grid_spec, out_shape=jax.ShapeDtypeStruct(ra.shape, BF16),
        compiler_params=_cparams("parallel", "parallel"))(core.reshape(1).astype(jnp.int32), g, ra)


def _sum_chips(kind, pa, rb, chip, core, out, layer):
    _, R, C = rb.shape
    axis = KIND_AXIS[kind]
    tr, tc = _row_tile(R), _pick_tile(C, 2048)
    nr, nc = R // tr, C // tc

    def body(s_ref, c_ref, a_ref, b0_ref, b1_ref, b2_ref, old_ref, o_ref):
        o_ref[...] = ((a_ref[...].astype(F32) + b0_ref[...].astype(F32)) + b1_ref[...].astype(F32)) + b2_ref[...].astype(F32)

    def rb_spec(n):
        return pl.BlockSpec((None, tr, tc), lambda i, j, s_ref, c_ref: (n, i, j))

    if axis == "col":
        own = pl.BlockSpec((tr, tc), lambda i, j, s_ref, c_ref: (i, s_ref[0] * nc + j))
        dst = pl.BlockSpec((None, tr, tc), lambda i, j, s_ref, c_ref: (layer, c_ref[0] * nr + i, j))
        assert out.shape[1:] == (2 * R, C)
    else:
        own = pl.BlockSpec((tr, tc), lambda i, j, s_ref, c_ref: (s_ref[0] * nr + i, j))
        dst = pl.BlockSpec((None, tr, tc), lambda i, j, s_ref, c_ref: (layer, i, c_ref[0] * nc + j))
        assert out.shape[1:] == (R, 2 * C)
    grid_spec = pltpu.PrefetchScalarGridSpec(
        num_scalar_prefetch=2, grid=(nr, nc), in_specs=[own, rb_spec(0), rb_spec(1), rb_spec(2), ANY], out_specs=dst)
    return pl.pallas_call(
        body, name="sum_chips_" + kind, grid_spec=grid_spec, out_shape=jax.ShapeDtypeStruct(out.shape, F32),
        input_output_aliases={6: 0}, compiler_params=_cparams("parallel", "parallel"))(
            chip.reshape(1).astype(jnp.int32), core.reshape(1).astype(jnp.int32), pa, rb, rb, rb, out)


def _sum_devices(gathered):
    M = gathered.shape[0] // N_DEV
    C = gathered.shape[1]

    def body(g_ref, o_ref):
        acc = g_ref[0:M, :]
        for d in range(1, N_DEV):
            acc = acc + g_ref[d * M:(d + 1) * M, :]
        o_ref[...] = acc

    vm = pl.BlockSpec(memory_space=pltpu.VMEM)
    return pl.pallas_call(body, name="sum_devices", in_specs=[vm], out_specs=vm, out_shape=jax.ShapeDtypeStruct((M, C), F32),
                          compiler_params=pltpu.CompilerParams(vmem_limit_bytes=VMEM_LIMIT_BYTES))(gathered)


WEIGHTS = ["mix_norm", "ffn_norm", "final_norm", "hgrn_w_in", "hgrn_lb_logits", "hgrn_gnorm", "hgrn_w_out", "m_w_in",
           "m_conv_w", "m_conv_b", "m_dt_bias", "m_A_log", "m_D", "m_norm", "m_w_out", "f_w_up", "f_conv_w", "f_conv_b",
           "f_w_down"]
BIG_COLS = ("hgrn_w_in", "m_w_in", "f_w_up")
BIG_ROWS = ("hgrn_w_out", "m_w_out", "f_w_down")
BIG = BIG_COLS + BIG_ROWS
SMALL_SHARDED = ("m_conv_w", "m_conv_b", "m_norm", "f_conv_w")
SMALL_REPLICATED = ("mix_norm", "ffn_norm", "final_norm", "hgrn_lb_logits", "hgrn_gnorm", "m_dt_bias", "m_A_log", "m_D",
                    "f_conv_b")
SMALL = SMALL_REPLICATED + SMALL_SHARDED


def _pack_rows(arrs, row_mult=8):
    flat = jnp.concatenate([a.reshape(-1).astype(F32) for a in arrs])
    unit = FLAT_COLS * row_mult
    n = -(-flat.size // unit) * unit
    return jnp.pad(flat, (0, n - flat.size)).reshape(-1, FLAT_COLS)


def _unpack_rows(buf, shapes):
    flat = buf.reshape(-1)
    out, off = [], 0
    for shp in shapes:
        n = math.prod(shp)
        out.append(flat[off:off + n].reshape(shp))
        off += n
    return out


def kernel(x, mix_norm, ffn_norm, final_norm, hgrn_w_in, hgrn_lb_logits, hgrn_gnorm, hgrn_w_out, m_w_in, m_conv_w, m_conv_b, m_dt_bias, m_A_log, m_D, m_norm, m_w_out, f_w_up, f_conv_w, f_conv_b, f_w_down, loss_target, m_mix_norm, m_ffn_norm, m_final_norm, m_hgrn_w_in, m_hgrn_lb_logits, m_hgrn_gnorm, m_hgrn_w_out, m_m_w_in, m_m_conv_w, m_m_conv_b, m_m_dt_bias, m_m_A_log, m_m_D, m_m_norm, m_m_w_out, m_f_w_up, m_f_conv_w, m_f_conv_b, m_f_w_down, v_mix_norm, v_ffn_norm, v_final_norm, v_hgrn_w_in, v_hgrn_lb_logits, v_hgrn_gnorm, v_hgrn_w_out, v_m_w_in, v_m_conv_w, v_m_conv_b, v_m_dt_bias, v_m_A_log, v_m_D, v_m_norm, v_m_w_out, v_f_w_up, v_f_conv_w, v_f_conv_b, v_f_w_down):
    given = dict(locals())
    w = {n: given[n] for n in WEIGHTS}
    mom1 = {n: given["m_" + n] for n in WEIGHTS}
    mom2 = {n: given["v_" + n] for n in WEIGHTS}
    chip = 2 * lax.axis_index("x") + lax.axis_index("y")
    core = lax.axis_index("c")

    shards = {k: w[k] for k in KINDS if k != "m_w_in_t"}
    shards["m_w_in_t"] = w["m_w_in"].transpose(0, 2, 1).astype(BF16)
    m_in = N_CHIPS * w["m_w_in"].shape[2]
    pad_rows = {"m_w_in_t": -(-m_in // LANES) * LANES - m_in}
    cws = {k: shards[k].shape[2] if KIND_AXIS[k] == "col" else shards[k].shape[1] for k in KINDS}
    depth = w["mix_norm"].shape[0]

    def layer_kinds(i):
        mixer = {"mix_in": ("hgrn_w_in", i // 2), "mix_out": ("hgrn_w_out", i // 2)} if i % 2 == 0 else \
                {"mix_in": ("m_w_in_t", i // 2), "mix_out": ("m_w_out", i // 2)}
        return {**mixer, "f_w_up": ("f_w_up", i), "f_w_down": ("f_w_down", i)}

    items = [it for i in range(depth) for it in layer_kinds(i).values()]
    staged = [_stage_shard(k, shards[k], l, chip, pad_rows.get(k, 0)) for k, l in items]
    send_sems, recv_sems, mats, token = _gather_start(items, staged, cws)
    own = _pack_rows([w[n] for n in SMALL_SHARDED]) + token[0, 0]
    all_small = _all_gather_small(own, "gather_small_params").reshape(N_CHIPS, 2, -1)[:, 0]
    per_chip = [_unpack_rows(all_small[s], [w[n].shape for n in SMALL_SHARDED]) for s in range(N_CHIPS)]
    P = {}
    for i, n in enumerate(SMALL_SHARDED):
        P[n] = jnp.concatenate([per_chip[s][i] for s in range(N_CHIPS)], axis=-1)
    for n in SMALL_REPLICATED:
        P[n] = w[n]

    def fetch(i, h):
        lk = layer_kinds(i)
        idx = [items.index(it) for it in lk.values()]
        got = _gather_wait(items, idx, [mats[q] for q in idx], send_sems, recv_sems, cws, h, "gather_wait_%d" % i)
        got = _forward_halves([k for k, _ in lk.values()], got, cws, "forward_halves_%d" % i)
        return dict(zip(lk.keys(), got))

    pending = {}

    def emit(i, gm):
        lk = layer_kinds(i)
        kinds = [k for k, _ in lk.values()]
        gms = [gm[key] for key in lk]
        ra = _swap_halves(kinds, gms, "swap_halves_%d" % i)
        pas = [_add_sibling(k, g_, r_, core) for k, g_, r_ in zip(kinds, gms, ra)]
        s_sems, r_sems, pas, lands, tok = _scatter_start(kinds, pas, cws, "scatter_start_%d" % i)
        pending[i] = (kinds, pas, lands, s_sems, r_sems)
        return tok

    loss_part, grad_x, g_full = _local_step(x, loss_target, P, fetch, emit)

    g_sh = {k: lax.empty(shards[k].shape, F32) for k in KINDS}
    for i in reversed(range(depth)):
        kinds, pas, lands, s_sems, r_sems = pending[i]
        pas, lands = _scatter_wait(kinds, pas, lands, s_sems, r_sems, cws, grad_x, "scatter_wait_%d" % i)
        for (k, l), p_, rb_ in zip(layer_kinds(i).values(), pas, lands):
            g_sh[k] = _sum_chips(k, p_, rb_, chip, core, g_sh[k], l)
    g_sh = _share_halves(g_sh)
    grads = {k: g_sh[k] for k in KINDS if k != "m_w_in_t"}
    grads["m_w_in"] = g_sh["m_w_in_t"].transpose(0, 2, 1)

    small_shapes = [g_full[n].shape for n in SMALL] + [(1,)]
    packed = _pack_rows([g_full[n] for n in SMALL] + [loss_part[0, 0:1]])
    summed = _sum_devices(_all_gather_small(packed, "gather_small_grads"))
    small = _unpack_rows(summed, small_shapes)
    loss = small[-1][0]
    for n, gs in zip(SMALL, small[:-1]):
        if n in SMALL_SHARDED:
            width = w[n].shape[-1]
            gs = lax.dynamic_slice_in_dim(gs, chip * width, width, axis=gs.ndim - 1)
        grads[n] = gs

    delta, new_m, new_v = {}, {}, {}
    for n in BIG:
        delta[n], new_m[n], new_v[n] = _adamw(w[n], grads[n], mom1[n], mom2[n], "adamw_" + n)
    shapes = [w[n].shape for n in SMALL]
    ds, ms, vs = _adamw(_pack_rows([w[n] for n in SMALL]), _pack_rows([grads[n] for n in SMALL]),
                        _pack_rows([mom1[n] for n in SMALL]), _pack_rows([mom2[n] for n in SMALL]), "adamw_small")
    for n, d_, m_, v_ in zip(SMALL, _unpack_rows(ds, shapes), _unpack_rows(ms, shapes), _unpack_rows(vs, shapes)):
        delta[n], new_m[n], new_v[n] = d_, m_, v_

    return (loss, grad_x, *[grads[n] for n in WEIGHTS], *[delta[n] for n in WEIGHTS], *[new_m[n] for n in WEIGHTS],
            *[new_v[n] for n in WEIGHTS])
```

```python
import functools
import math

import jax
import jax.numpy as jnp
from jax import lax
from jax.experimental import pallas as pl
from jax.experimental.pallas import tpu as pltpu

F32 = jnp.float32
BF16 = jnp.bfloat16
NORM_EPS = 1e-5
HGRN_DK = 128
HGRN_CHUNK = 64
M_HEADDIM = 64
M_GROUPS = 8
M_D_STATE = 128
M_CONV = 4
M_CHUNK = 128
FFN_CONV = 3
EXP_CLIP = 80.0
LANES = 128
VMEM_LIMIT_BYTES = 56 * 1024 * 1024
FLAT_COLS = 1024
ADAM_LR, ADAM_B1, ADAM_B2, ADAM_EPS, ADAM_WD, ADAM_STEP = 0.001, 0.9, 0.999, 1e-08, 0.01, 10
MESH = pl.DeviceIdType.MESH

NN = (((1,), (0,)), ((), ()))
NT = (((1,), (1,)), ((), ()))
TN = (((0,), (0,)), ((), ()))


def _cparams(*sems):
    return pltpu.CompilerParams(dimension_semantics=sems, vmem_limit_bytes=VMEM_LIMIT_BYTES)


def _dot(a, b, dn=NN):
    return lax.dot_general(a.astype(BF16), b.astype(BF16), dn, preferred_element_type=F32)


def _dot_exact(x, m, dn=NN, passes=3, x_first=True):
    acc = None
    r = x
    for _ in range(passes):
        p = r.astype(BF16)
        r = r - p.astype(F32)
        t = lax.dot_general(p, m, dn, preferred_element_type=F32) if x_first else lax.dot_general(m, p, dn, preferred_element_type=F32)
        acc = t if acc is None else acc + t
    return acc


def _iota(shape, dim):
    return lax.broadcasted_iota(jnp.int32, shape, dim)


def _cumsum_rows(x, reverse=False):
    n = x.shape[0]
    row = _iota(x.shape, 0)
    s = 1
    while s < n:
        if reverse:
            x = x + jnp.where(row < n - s, pltpu.roll(x, n - s, 0), 0.0)
        else:
            x = x + jnp.where(row >= s, pltpu.roll(x, s, 0), 0.0)
        s *= 2
    return x


def _silu(x):
    return x * jax.nn.sigmoid(x)


def _dsilu(x):
    s = jax.nn.sigmoid(x)
    return s * (1.0 + x * (1.0 - s))


def _pick_tile(dim, pref):
    if dim <= pref:
        return dim
    best = None
    t = LANES
    while t <= pref:
        if dim % t == 0:
            best = t
        t += LANES
    assert best is not None, (dim, pref)
    return best


def _matmul(a, b, *, ta=False, tb=False, res=None, out_dtype=F32, tm=1024, tn=1024, tk=2048, name,
            a_parts=False, b_parts=False, b_layer=None, b_off=0, out=None, out_layer=None, out_off=0):
    a = a.astype(BF16)
    b = b.astype(BF16)
    if a_parts:
        assert not ta
        pa, M, kp = a.shape
        K = pa * kp
    else:
        M, K = (a.shape[1], a.shape[0]) if ta else a.shape
    bsh = b.shape[1:] if b_layer is not None else b.shape
    if b_parts:
        assert not tb
        pb, _, np_ = bsh
        N = pb * np_
    else:
        N = bsh[0] if tb else bsh[1]
    tm, tn, tk = _pick_tile(M, tm), _pick_tile(np_ if b_parts else N, tn), _pick_tile(kp if a_parts else K, tk)
    nk = K // tk
    dn = (((0 if ta else 1,), (1 if tb else 0,)), ((), ()))
    assert b_off % tk == 0 and out_off % tm == 0

    def body(*refs):
        refs = list(refs)
        acc = refs.pop() if nk > 1 else None
        o_ref = refs.pop()
        if out is not None:
            refs.pop()
        a_ref, b_ref = refs[0], refs[1]
        r_ref = refs[2] if res is not None else None
        k = pl.program_id(2)

        def prod():
            return lax.dot_general(a_ref[...], b_ref[...], dn, preferred_element_type=F32)

        def finish(r):
            if res is not None:
                r = r + r_ref[...]
            o_ref[...] = r.astype(out_dtype)

        if nk == 1:
            finish(prod())
            return

        @pl.when(k == 0)
        def _():
            acc[...] = prod()

        @pl.when(jnp.logical_and(k > 0, k < nk - 1))
        def _():
            acc[...] += prod()

        @pl.when(k == nk - 1)
        def _():
            finish(acc[...] + prod())

    if a_parts:
        kpb = kp // tk
        a_spec = pl.BlockSpec((None, tm, tk), lambda i, j, k: (k // kpb, i, k % kpb))
    elif ta:
        a_spec = pl.BlockSpec((tk, tm), lambda i, j, k: (k, i))
    else:
        a_spec = pl.BlockSpec((tm, tk), lambda i, j, k: (i, k))
    lead = () if b_layer is None else (b_layer,)
    lead_blk = () if b_layer is None else (None,)
    kb0 = b_off // tk
    if b_parts:
        npb = np_ // tn
        b_spec = pl.BlockSpec(lead_blk + (None, tk, tn), lambda i, j, k: lead + (j // npb, k, j % npb))
    elif tb:
        b_spec = pl.BlockSpec(lead_blk + (tn, tk), lambda i, j, k: lead + (j, k))
    else:
        b_spec = pl.BlockSpec(lead_blk + (tk, tn), lambda i, j, k: lead + (kb0 + k, j))
    r_spec = pl.BlockSpec((tm, tn), lambda i, j, k: (i, j))
    in_specs = [a_spec, b_spec] + ([r_spec] if res is not None else [])
    args = (a, b) + ((res,) if res is not None else ())
    if out is None:
        o_spec, out_shape, aliases = r_spec, jax.ShapeDtypeStruct((M, N), out_dtype), {}
    else:
        assert out.dtype == out_dtype and out.shape[-1] == N
        olead = () if out_layer is None else (out_layer,)
        olead_blk = () if out_layer is None else (None,)
        ob0 = out_off // tm
        o_spec = pl.BlockSpec(olead_blk + (tm, tn), lambda i, j, k: olead + (ob0 + i, j))
        out_shape = jax.ShapeDtypeStruct(out.shape, out.dtype)
        aliases = {len(args): 0}
        in_specs = in_specs + [pl.BlockSpec(memory_space=pl.ANY)]
        args = args + (out,)
    return pl.pallas_call(
        body, name=name, grid=(M // tm, N // tn, nk), in_specs=in_specs, out_specs=o_spec, out_shape=out_shape,
        scratch_shapes=[pltpu.VMEM((tm, tn), F32)] if nk > 1 else [], input_output_aliases=aliases,
        compiler_params=_cparams("parallel", "parallel", "arbitrary"))(*args)


def _rmsnorm_fwd(h, w, name):
    T, D = h.shape
    tm = _pick_tile(T, 256)

    def body(h_ref, w_ref, u_ref):
        x = h_ref[...]
        r = lax.rsqrt(jnp.mean(x * x, axis=-1, keepdims=True) + NORM_EPS)
        u_ref[...] = (x * r * w_ref[...]).astype(BF16)

    return pl.pallas_call(
        body, name=name, grid=(T // tm,),
        in_specs=[pl.BlockSpec((tm, D), lambda i: (i, 0)), pl.BlockSpec((1, D), lambda i: (0, 0))],
        out_specs=pl.BlockSpec((tm, D), lambda i: (i, 0)), out_shape=jax.ShapeDtypeStruct((T, D), BF16),
        compiler_params=_cparams("parallel"))(h, w.reshape(1, D))


def _rmsnorm_bwd(h, w, du, dres, name, dep=None):
    T, D = h.shape
    tm = _pick_tile(T, 256)

    def body(h_ref, w_ref, du_ref, dr_ref, *rest):
        dh_ref, dhb_ref, dw_ref = rest[-3:]
        x = h_ref[...]
        r = lax.rsqrt(jnp.mean(x * x, axis=-1, keepdims=True) + NORM_EPS)
        xh = x * r
        du_ = du_ref[...]
        dy = du_ * w_ref[...]
        dh = dr_ref[...] + r * (dy - xh * jnp.mean(dy * xh, axis=-1, keepdims=True))
        dh_ref[...] = dh
        dhb_ref[...] = dh.astype(BF16)
        part = jnp.sum(du_ * xh, axis=0, keepdims=True)

        @pl.when(pl.program_id(0) == 0)
        def _():
            dw_ref[...] = part

        @pl.when(pl.program_id(0) > 0)
        def _():
            dw_ref[...] += part

    row = pl.BlockSpec((tm, D), lambda i: (i, 0))
    vec = pl.BlockSpec((1, D), lambda i: (0, 0))
    extra_specs, extra = ([], ()) if dep is None else ([pl.BlockSpec(memory_space=pl.ANY)], (dep,))
    dh, dhb, dw = pl.pallas_call(
        body, name=name, grid=(T // tm,), in_specs=[row, vec, row, row] + extra_specs, out_specs=[row, row, vec],
        out_shape=[jax.ShapeDtypeStruct((T, D), F32), jax.ShapeDtypeStruct((T, D), BF16), jax.ShapeDtypeStruct((1, D), F32)],
        compiler_params=_cparams("arbitrary"))(h, w.reshape(1, D), du, dres, *extra)
    return dh, dhb, dw.reshape(D)


def _loss_head(h, w, target):
    T, D = h.shape
    tm = _pick_tile(T, 256)

    def body(h_ref, w_ref, t_ref, loss_ref, dh_ref, dhb_ref, dw_ref):
        x = h_ref[...]
        wv = w_ref[...]
        r = lax.rsqrt(jnp.mean(x * x, axis=-1, keepdims=True) + NORM_EPS)
        xh = x * r
        e = xh * wv - t_ref[...]
        lpart = jnp.zeros((1, LANES), F32) + 0.5 * jnp.sum(jnp.mean(e * e, axis=-1, keepdims=True))
        dyo = e * (1.0 / D)
        dy = dyo * wv
        dh = r * (dy - xh * jnp.mean(dy * xh, axis=-1, keepdims=True))
        dh_ref[...] = dh
        dhb_ref[...] = dh.astype(BF16)
        part = jnp.sum(dyo * xh, axis=0, keepdims=True)

        @pl.when(pl.program_id(0) == 0)
        def _():
            dw_ref[...] = part
            loss_ref[...] = lpart

        @pl.when(pl.program_id(0) > 0)
        def _():
            dw_ref[...] += part
            loss_ref[...] += lpart

    row = pl.BlockSpec((tm, D), lambda i: (i, 0))
    vec = pl.BlockSpec((1, D), lambda i: (0, 0))
    lvec = pl.BlockSpec((1, LANES), lambda i: (0, 0))
    loss, dh, dhb, dw = pl.pallas_call(
        body, name="loss_head", grid=(T // tm,), in_specs=[row, vec, row], out_specs=[lvec, row, row, vec],
        out_shape=[jax.ShapeDtypeStruct((1, LANES), F32), jax.ShapeDtypeStruct((T, D), F32),
                   jax.ShapeDtypeStruct((T, D), BF16), jax.ShapeDtypeStruct((1, D), F32)],
        compiler_params=_cparams("arbitrary"))(h, w.reshape(1, D), target)
    return loss, dh, dhb, dw.reshape(D)


def _hgrn_gates(qr, fr, lb):
    sig = jax.nn.sigmoid(fr)
    nsig = jax.nn.sigmoid(-fr)
    fg = lb + (1.0 - lb) * sig
    logf = jnp.log(fg)
    k = (1.0 - lb) * nsig
    q = _silu(qr)
    return q, k, logf, sig, nsig, fg


def _hgrn_scaled(q, k, b, bmid):
    eq = jnp.exp(jnp.clip(b - bmid, -EXP_CLIP, EXP_CLIP))
    ek = jnp.exp(jnp.clip(bmid - b, -EXP_CLIP, EXP_CLIP))
    return q * eq, k * ek, eq, ek


def _hgrn_fwd(proj, lb, gnw, H):
    B, L, _ = proj.shape
    C, DK = HGRN_CHUNK, HGRN_DK
    F_ = H * DK
    NC = L // C

    def body(q_ref, f_ref, v_ref, g_ref, lb_ref, gn_ref, o_ref, on_ref, st_ref, ST, bsc):
        ST[...] = jnp.zeros_like(ST)
        lbv = lb_ref[...]
        gn = gn_ref[...]
        causal = _iota((C, C), 0) >= _iota((C, C), 1)

        def chunk(c, carry):
            r0 = pl.multiple_of(c * C, C)
            rows = pl.ds(r0, C)
            q, k, logf, _, _, _ = _hgrn_gates(q_ref[0, rows, :], f_ref[0, rows, :], lbv)
            v = v_ref[0, rows, :]
            b = _cumsum_rows(logf)
            bsc[...] = b
            bmid = bsc[C // 2 - 1:C // 2, :]
            blast = bsc[C - 1:C, :]
            qs, ks, _, _ = _hgrn_scaled(q, k, b, bmid)
            A = jnp.where(causal, _dot(qs, ks, NT), 0.0)
            st = ST[...]
            st_ref[0, 0, c] = st
            o = _dot(A, v) + _dot(q * jnp.exp(b), st, NT)
            kb = k * jnp.exp(blast - b)
            ST[...] = st * jnp.exp(blast) + _dot(v, kb, TN)
            rms = lax.rsqrt(jnp.mean(o * o, axis=-1, keepdims=True) + NORM_EPS)
            o_ref[0, rows, :] = o
            on_ref[0, rows, :] = (o * rms * gn * _silu(g_ref[0, rows, :])).astype(BF16)
            return carry

        lax.fori_loop(0, NC, chunk, 0)

    def col(off):
        return pl.BlockSpec((1, L, DK), lambda b, h: (b, 0, off + h))

    return pl.pallas_call(
        body, name="hgrn_fwd", grid=(B, H),
        in_specs=[col(0), col(H), col(2 * H), col(3 * H), pl.BlockSpec((1, DK), lambda b, h: (0, h)),
                  pl.BlockSpec((1, DK), lambda b, h: (0, 0))],
        out_specs=[col(0), col(0), pl.BlockSpec((1, 1, NC, DK, DK), lambda b, h: (b, h, 0, 0, 0))],
        out_shape=[jax.ShapeDtypeStruct((B, L, F_), F32), jax.ShapeDtypeStruct((B, L, F_), BF16),
                   jax.ShapeDtypeStruct((B, H, NC, DK, DK), F32)],
        scratch_shapes=[pltpu.VMEM((DK, DK), F32), pltpu.VMEM((C, DK), F32)],
        compiler_params=_cparams("parallel", "parallel"))(proj, proj, proj, proj, lb, gnw)


def _hgrn_bwd(proj, o, don, st, lb, gnw, H):
    B, L, _ = proj.shape
    C, DK = HGRN_CHUNK, HGRN_DK
    F_ = H * DK
    NC = L // C

    def body(q_ref, f_ref, v_ref, g_ref, o_ref, do_ref, st_ref, lb_ref, gn_ref,
             dp_ref, dlb_ref, dgn_ref, DST, bsc):
        DST[...] = jnp.zeros_like(DST)
        dlb_ref[...] = jnp.zeros_like(dlb_ref)
        dgn_ref[...] = jnp.zeros_like(dgn_ref)
        lbv = lb_ref[...]
        gn = gn_ref[...]
        causal = _iota((C, C), 0) >= _iota((C, C), 1)
        lastrow = _iota((C, DK), 0) == C - 1

        def chunk(i, carry):
            c = NC - 1 - i
            r0 = pl.multiple_of(c * C, C)
            rows = pl.ds(r0, C)
            qr = q_ref[0, rows, :]
            fr = f_ref[0, rows, :]
            q, k, logf, sig, nsig, fg = _hgrn_gates(qr, fr, lbv)
            v = v_ref[0, rows, :]
            b = _cumsum_rows(logf)
            bsc[...] = b
            bmid = bsc[C // 2 - 1:C // 2, :]
            blast = bsc[C - 1:C, :]
            qs, ks, eq, ek = _hgrn_scaled(q, k, b, bmid)
            A = jnp.where(causal, _dot(qs, ks, NT), 0.0)
            st_in = st_ref[0, 0, c]
            dst = DST[...]
            eb = jnp.exp(b)
            ebl = jnp.exp(blast)
            ekb = jnp.exp(blast - b)
            qb = q * eb
            kb = k * ekb
            ov = o_ref[0, rows, :]
            gr = g_ref[0, rows, :]
            rms = lax.rsqrt(jnp.mean(ov * ov, axis=-1, keepdims=True) + NORM_EPS)
            oh = ov * rms
            sg = _silu(gr)
            don_ = do_ref[0, rows, :]
            dgn_ref[0, 0] += jnp.sum(don_ * oh * sg, axis=0, keepdims=True)
            dp_ref[3, 0, rows, :] = (don_ * oh * gn * _dsilu(gr)).astype(BF16)
            doh = don_ * gn * sg
            do_ = rms * (doh - oh * jnp.mean(doh * oh, axis=-1, keepdims=True))
            dA = jnp.where(causal, _dot(do_, v, NT), 0.0)
            dp_ref[2, 0, rows, :] = (_dot(A, do_, TN) + _dot(kb, dst, NT)).astype(BF16)
            dqb = _dot(do_, st_in)
            dkb = _dot(v, dst)
            dq = _dot(dA, ks) * eq + dqb * eb
            dk_inter = dkb * ekb
            dk = _dot(dA, qs, TN) * ek + dk_inter
            db = q * dq - k * dk
            extra = jnp.sum(k * dk_inter, axis=0, keepdims=True) + ebl * jnp.sum(st_in * dst, axis=0, keepdims=True)
            db = db + jnp.where(lastrow, extra, 0.0)
            dlogf = _cumsum_rows(db, reverse=True)
            DST[...] = dst * ebl + _dot(do_, qb, TN)
            dp_ref[0, 0, rows, :] = (dq * _dsilu(qr)).astype(BF16)
            ss = sig * nsig
            dp_ref[1, 0, rows, :] = ((1.0 - lbv) * ss * (dlogf / fg - dk)).astype(BF16)
            dlb_ref[0] += jnp.sum(dlogf * nsig / fg - dk * nsig, axis=0, keepdims=True)
            return carry

        lax.fori_loop(0, NC, chunk, 0)

    def col(off):
        return pl.BlockSpec((1, L, DK), lambda b, h: (b, 0, off + h))

    outs = pl.pallas_call(
        body, name="hgrn_bwd", grid=(B, H),
        in_specs=[col(0), col(H), col(2 * H), col(3 * H), col(0), col(0),
                  pl.BlockSpec((1, 1, NC, DK, DK), lambda b, h: (b, h, 0, 0, 0)),
                  pl.BlockSpec((1, DK), lambda b, h: (0, h)), pl.BlockSpec((1, DK), lambda b, h: (0, 0))],
        out_specs=[pl.BlockSpec((4, 1, L, DK), lambda b, h: (0, b, 0, h)), pl.BlockSpec((1, 1, DK), lambda b, h: (b, 0, h)),
                   pl.BlockSpec((1, 1, 1, DK), lambda b, h: (b, h, 0, 0))],
        out_shape=[jax.ShapeDtypeStruct((4, B, L, F_), BF16), jax.ShapeDtypeStruct((B, 1, F_), F32),
                   jax.ShapeDtypeStruct((B, H, 1, DK), F32)],
        scratch_shapes=[pltpu.VMEM((DK, DK), F32), pltpu.VMEM((C, DK), F32)],
        compiler_params=_cparams("parallel", "parallel"))(proj, proj, proj, proj, o, don, st, lb, gnw)
    return outs


CONV_ROWS = 256
PAD_ROWS = 8


def _conv_taps(pad_ref, w_ref, r0, K, rb, forward=True):
    ext = pad_ref[pl.ds(r0, rb + PAD_ROWS), :]
    n = rb + PAD_ROWS
    acc = None
    for s in range(K):
        if forward:
            sh = ext if s == 0 else pltpu.roll(ext, s, 0)
            term = sh[PAD_ROWS:, :]
        else:
            sh = ext if s == 0 else pltpu.roll(ext, n - s, 0)
            term = sh[:rb, :]
        term = term * w_ref[K - 1 - s:K - s, :]
        acc = term if acc is None else acc + term
    return acc


def _conv_dw(ext, dc, K):
    row = _iota((8, dc.shape[1]), 0)
    out = jnp.zeros((8, dc.shape[1]), F32)
    for kk in range(K):
        s = K - 1 - kk
        sh = ext if s == 0 else pltpu.roll(ext, s, 0)
        out = out + jnp.where(row == kk, jnp.sum(dc * sh[PAD_ROWS:, :], axis=0, keepdims=True), 0.0)
    return out


def _mconv_fwd(zx, cw, cb, col0, width):
    B, L, _ = zx.shape
    K = cw.shape[0]
    ct = _pick_tile(width, 256)
    rb = min(CONV_ROWS, L)
    nrb = L // rb
    off = col0 // ct

    def body(x_ref, w_ref, b_ref, y_ref, xp):
        xp[0:PAD_ROWS, :] = jnp.zeros((PAD_ROWS, ct), F32)
        xp[PAD_ROWS:, :] = x_ref[0]
        bias = b_ref[...]

        def blk(i, carry):
            r0 = pl.multiple_of(i * rb, rb)
            y_ref[0, pl.ds(r0, rb), :] = _silu(_conv_taps(xp, w_ref, r0, K, rb) + bias)
            return carry

        lax.fori_loop(0, nrb, blk, 0)

    return pl.pallas_call(
        body, name="mconv_fwd", grid=(B, width // ct),
        in_specs=[pl.BlockSpec((1, L, ct), lambda b, j: (b, 0, off + j)), pl.BlockSpec((K, ct), lambda b, j: (0, j)),
                  pl.BlockSpec((1, ct), lambda b, j: (0, j))],
        out_specs=pl.BlockSpec((1, L, ct), lambda b, j: (b, 0, j)),
        out_shape=jax.ShapeDtypeStruct((B, L, width), F32),
        scratch_shapes=[pltpu.VMEM((L + PAD_ROWS, ct), F32)],
        compiler_params=_cparams("parallel", "parallel"))(zx, cw, cb.reshape(1, width))


def _mconv_bwd(zx, dya, cw, cb, col0, wcol0, name):
    B, L, _ = zx.shape
    K = cw.shape[0]
    npart, _, _, wq = dya.shape
    width = npart * wq
    ct = _pick_tile(wq, 256)
    rb = min(CONV_ROWS, L)
    nrb = L // rb
    off = (col0 + wcol0) // ct
    woff = wcol0 // ct
    pq = wq // ct

    def body(x_ref, dy_ref, w_ref, b_ref, dx_ref, dw_ref, db_ref, xp, dcp):
        xp[0:PAD_ROWS, :] = jnp.zeros((PAD_ROWS, ct), F32)
        xp[PAD_ROWS:, :] = x_ref[0]
        dcp[L:, :] = jnp.zeros((PAD_ROWS, ct), F32)
        bias = b_ref[...]

        def blk1(i, carry):
            dw, db = carry
            r0 = pl.multiple_of(i * rb, rb)
            cpre = _conv_taps(xp, w_ref, r0, K, rb) + bias
            dc = dy_ref[0, 0, pl.ds(r0, rb), :] * _dsilu(cpre)
            dcp[pl.ds(r0, rb), :] = dc
            ext = xp[pl.ds(r0, rb + PAD_ROWS), :]
            return dw + _conv_dw(ext, dc, K), db + jnp.sum(dc, axis=0, keepdims=True)

        dw, db = lax.fori_loop(0, nrb, blk1, (jnp.zeros((8, ct), F32), jnp.zeros((1, ct), F32)))
        dw_ref[0] = dw
        db_ref[0] = db

        def blk2(i, carry):
            r0 = pl.multiple_of(i * rb, rb)
            dx_ref[0, pl.ds(r0, rb), :] = _conv_taps(dcp, w_ref, r0, K, rb, forward=False).astype(BF16)
            return carry

        lax.fori_loop(0, nrb, blk2, 0)

    dx, dw, db = pl.pallas_call(
        body, name=name, grid=(B, width // ct),
        in_specs=[pl.BlockSpec((1, L, ct), lambda b, j: (b, 0, off + j)),
                  pl.BlockSpec((1, 1, L, ct), lambda b, j: (j // pq, b, 0, j % pq)),
                  pl.BlockSpec((K, ct), lambda b, j: (0, woff + j)), pl.BlockSpec((1, ct), lambda b, j: (0, woff + j))],
        out_specs=[pl.BlockSpec((1, L, ct), lambda b, j: (b, 0, j)), pl.BlockSpec((1, 8, ct), lambda b, j: (b, 0, j)),
                   pl.BlockSpec((1, 1, ct), lambda b, j: (b, 0, j))],
        out_shape=[jax.ShapeDtypeStruct((B, L, width), BF16), jax.ShapeDtypeStruct((B, 8, width), F32),
                   jax.ShapeDtypeStruct((B, 1, width), F32)],
        scratch_shapes=[pltpu.VMEM((L + PAD_ROWS, ct), F32), pltpu.VMEM((L + PAD_ROWS, ct), F32)],
        compiler_params=_cparams("parallel", "parallel"))(zx, dya, cw, cb.reshape(1, -1))
    return dx, dw[:, :K, :], db


def _ffn_mid_fwd(up, cw, cb, dff):
    B, L, _ = up.shape
    K = cw.shape[0]
    ct = _pick_tile(dff, 256)
    rb = min(CONV_ROWS, L)
    nrb = L // rb
    half = dff // ct

    def body(g_ref, u_ref, wg_ref, wu_ref, bg_ref, bu_ref, a_ref, gp, upad):
        gp[0:PAD_ROWS, :] = jnp.zeros((PAD_ROWS, ct), F32)
        upad[0:PAD_ROWS, :] = jnp.zeros((PAD_ROWS, ct), F32)
        gp[PAD_ROWS:, :] = g_ref[0]
        upad[PAD_ROWS:, :] = u_ref[0]
        bg, bu = bg_ref[...], bu_ref[...]

        def blk(i, carry):
            r0 = pl.multiple_of(i * rb, rb)
            cg = _conv_taps(gp, wg_ref, r0, K, rb) + bg
            cu = _conv_taps(upad, wu_ref, r0, K, rb) + bu
            a_ref[0, pl.ds(r0, rb), :] = (_silu(cg) * cu).astype(BF16)
            return carry

        lax.fori_loop(0, nrb, blk, 0)

    xg = pl.BlockSpec((1, L, ct), lambda b, j: (b, 0, j))
    xu = pl.BlockSpec((1, L, ct), lambda b, j: (b, 0, half + j))
    wgs = pl.BlockSpec((K, ct), lambda b, j: (0, j))
    wus = pl.BlockSpec((K, ct), lambda b, j: (0, half + j))
    bgs = pl.BlockSpec((1, ct), lambda b, j: (0, j))
    bus = pl.BlockSpec((1, ct), lambda b, j: (0, half + j))
    cb2 = cb.reshape(1, 2 * dff)
    return pl.pallas_call(
        body, name="ffn_mid_fwd", grid=(B, half), in_specs=[xg, xu, wgs, wus, bgs, bus], out_specs=xg,
        out_shape=jax.ShapeDtypeStruct((B, L, dff), BF16),
        scratch_shapes=[pltpu.VMEM((L + PAD_ROWS, ct), F32), pltpu.VMEM((L + PAD_ROWS, ct), F32)],
        compiler_params=_cparams("parallel", "parallel"))(up, up, cw, cw, cb2, cb2)


def _ffn_mid_bwd(up, dact, cw, cb, dff):
    B, L, _ = up.shape
    K = cw.shape[0]
    ct = _pick_tile(dff, 256)
    rb = min(CONV_ROWS, L)
    nrb = L // rb
    half = dff // ct

    def body(g_ref, u_ref, da_ref, wg_ref, wu_ref, bg_ref, bu_ref, dx_ref, dwg_ref, dwu_ref, dbg_ref, dbu_ref,
             gp, upad, dgp, dup):
        gp[0:PAD_ROWS, :] = jnp.zeros((PAD_ROWS, ct), F32)
        upad[0:PAD_ROWS, :] = jnp.zeros((PAD_ROWS, ct), F32)
        gp[PAD_ROWS:, :] = g_ref[0]
        upad[PAD_ROWS:, :] = u_ref[0]
        dgp[L:, :] = jnp.zeros((PAD_ROWS, ct), F32)
        dup[L:, :] = jnp.zeros((PAD_ROWS, ct), F32)
        bg, bu = bg_ref[...], bu_ref[...]

        def blk1(i, carry):
            dwg, dwu, dbg, dbu = carry
            r0 = pl.multiple_of(i * rb, rb)
            cg = _conv_taps(gp, wg_ref, r0, K, rb) + bg
            cu = _conv_taps(upad, wu_ref, r0, K, rb) + bu
            da = da_ref[0, pl.ds(r0, rb), :]
            dcg = da * cu * _dsilu(cg)
            dcu = da * _silu(cg)
            dgp[pl.ds(r0, rb), :] = dcg
            dup[pl.ds(r0, rb), :] = dcu
            eg = gp[pl.ds(r0, rb + PAD_ROWS), :]
            eu = upad[pl.ds(r0, rb + PAD_ROWS), :]
            return (dwg + _conv_dw(eg, dcg, K), dwu + _conv_dw(eu, dcu, K), dbg + jnp.sum(dcg, axis=0, keepdims=True),
                    dbu + jnp.sum(dcu, axis=0, keepdims=True))

        z8 = jnp.zeros((8, ct), F32)
        z1 = jnp.zeros((1, ct), F32)
        dwg, dwu, dbg, dbu = lax.fori_loop(0, nrb, blk1, (z8, z8, z1, z1))
        dwg_ref[0] = dwg
        dwu_ref[0] = dwu
        dbg_ref[0] = dbg
        dbu_ref[0] = dbu

        def blk2(i, carry):
            r0 = pl.multiple_of(i * rb, rb)
            dx_ref[0, 0, pl.ds(r0, rb), :] = _conv_taps(dgp, wg_ref, r0, K, rb, forward=False).astype(BF16)
            dx_ref[1, 0, pl.ds(r0, rb), :] = _conv_taps(dup, wu_ref, r0, K, rb, forward=False).astype(BF16)
            return carry

        lax.fori_loop(0, nrb, blk2, 0)

    xg = pl.BlockSpec((1, L, ct), lambda b, j: (b, 0, j))
    xu = pl.BlockSpec((1, L, ct), lambda b, j: (b, 0, half + j))
    wgs = pl.BlockSpec((K, ct), lambda b, j: (0, j))
    wus = pl.BlockSpec((K, ct), lambda b, j: (0, half + j))
    bgs = pl.BlockSpec((1, ct), lambda b, j: (0, j))
    bus = pl.BlockSpec((1, ct), lambda b, j: (0, half + j))
    w8 = pl.BlockSpec((1, 8, ct), lambda b, j: (b, 0, j))
    b1 = pl.BlockSpec((1, 1, ct), lambda b, j: (b, 0, j))
    cb2 = cb.reshape(1, 2 * dff)
    pad = pltpu.VMEM((L + PAD_ROWS, ct), F32)
    dx2, dwg, dwu, dbg, dbu = pl.pallas_call(
        body, name="ffn_mid_bwd", grid=(B, half), in_specs=[xg, xu, xg, wgs, wus, bgs, bus],
        out_specs=[pl.BlockSpec((2, 1, L, ct), lambda b, j: (0, b, 0, j)), w8, w8, b1, b1],
        out_shape=[jax.ShapeDtypeStruct((2, B, L, dff), BF16)] + [jax.ShapeDtypeStruct((B, 8, dff), F32)] * 2
        + [jax.ShapeDtypeStruct((B, 1, dff), F32)] * 2,
        scratch_shapes=[pad, pad, pad, pad],
        compiler_params=_cparams("parallel", "parallel"))(up, up, dact, cw, cw, cb2, cb2)
    dw = jnp.concatenate([dwg[:, :K], dwu[:, :K]], axis=-1)
    db = jnp.concatenate([dbg, dbu], axis=-1)
    return dx2, dw, db


def _ssd_consts(hpg, W):
    P = M_HEADDIM
    E = (_iota((LANES, W), 0) == _iota((LANES, W), 1) // P).astype(BF16)
    Ebig = (_iota((LANES, hpg * LANES), 0) == _iota((LANES, hpg * LANES), 1) // LANES).astype(BF16)
    causal = _iota((M_CHUNK, M_CHUNK), 0) >= _iota((M_CHUNK, M_CHUNK), 1)
    head_of_lane = _iota((1, W), 1) // P
    return E, Ebig, causal, head_of_lane


def _ssd_chunk_fwd(xs, Bm, Cm, dtr, bias, Aneg, E, Ebig, causal, head_of_lane, hpg, st, ar_sc, ae_sc):
    pre = dtr + bias
    dt = jnp.maximum(pre, 0.0) + jnp.log(1.0 + jnp.exp(-jnp.abs(pre)))
    Ad = dt * Aneg
    a_c = _cumsum_rows(Ad)
    ar_sc[...] = a_c.T
    aexp = _dot_exact(a_c, E)
    ae_sc[...] = aexp
    alast = ae_sc[M_CHUNK - 1:M_CHUNK, :]
    dtexp = _dot_exact(dt, E)
    X = xs * dtexp
    AC = _dot_exact(a_c, Ebig)
    CB = _dot(Cm, Bm, NT)
    Xb = X.astype(BF16)
    ydiag = jnp.zeros_like(xs)
    Ls = []
    for j in range(hpg):
        Lj = jnp.where(causal, jnp.exp(jnp.minimum(AC[:, j * LANES:(j + 1) * LANES] - ar_sc[j:j + 1, :], 0.0)), 0.0)
        Ls.append(Lj)
        Yj = _dot(CB * Lj, Xb)
        ydiag = ydiag + jnp.where(head_of_lane == j, Yj, 0.0)
    ea = jnp.exp(aexp)
    yoff = ea * _dot(Cm, st)
    dec = jnp.exp(alast - aexp)
    return dict(dt=dt, a_c=a_c, aexp=aexp, alast=alast, dtexp=dtexp, X=X, Xb=Xb, CB=CB, Ls=Ls, ydiag=ydiag, ea=ea,
                yoff=yoff, dec=dec)


def _ssd_fwd(xbca, zx, dtc, bias, Aneg, Dexp, nw, hpg):
    B, L, _ = xbca.shape
    G, N, C = M_GROUPS, M_D_STATE, M_CHUNK
    W = hpg * M_HEADDIM
    DI = G * W
    NC = L // C
    LB = min(L, 4 * C)
    ncb = LB // C

    def body(xs_ref, b_ref, c_ref, z_ref, dt_ref, bias_ref, a_ref, d_ref, nw_ref, y_ref, yn_ref, st_ref, ST, ar_sc, ae_sc):
        @pl.when(pl.program_id(2) == 0)
        def _():
            ST[...] = jnp.zeros_like(ST)

        E, Ebig, causal, head_of_lane = _ssd_consts(hpg, W)
        bias_ = bias_ref[0]
        Aneg_ = a_ref[0]
        Dv = d_ref[...]
        nwv = nw_ref[...]

        def chunk(ci, carry):
            r0 = pl.multiple_of(ci * C, C)
            rows = pl.ds(r0, C)
            xs = xs_ref[0, rows, :]
            Bm = b_ref[0, rows, :]
            Cm = c_ref[0, rows, :]
            st = ST[...]
            st_ref[0, 0, ci] = st
            f = _ssd_chunk_fwd(xs, Bm, Cm, dt_ref[0, 0, ci], bias_, Aneg_, E, Ebig, causal, head_of_lane, hpg, st, ar_sc, ae_sc)
            y = f["ydiag"] + f["yoff"] + xs * Dv
            ST[...] = st * jnp.exp(f["alast"]) + _dot(Bm, f["X"] * f["dec"], TN)
            yg = y * _silu(z_ref[0, rows, :])
            rstd = lax.rsqrt(jnp.mean(yg * yg, axis=-1, keepdims=True) + NORM_EPS)
            y_ref[0, rows, :] = y
            yn_ref[0, rows, :] = (yg * rstd * nwv).astype(BF16)
            return carry

        lax.fori_loop(0, ncb, chunk, 0)

    xw = pl.BlockSpec((1, LB, W), lambda b, g, s: (b, s, g))
    bsp = pl.BlockSpec((1, LB, N), lambda b, g, s: (b, s, DI // N + g))
    csp = pl.BlockSpec((1, LB, N), lambda b, g, s: (b, s, DI // N + G + g))
    dts = pl.BlockSpec((1, 1, ncb, C, LANES), lambda b, g, s: (b, g, s, 0, 0))
    hv = pl.BlockSpec((1, 1, LANES), lambda b, g, s: (g, 0, 0))
    wv = pl.BlockSpec((1, W), lambda b, g, s: (0, g))
    sts = pl.BlockSpec((1, 1, ncb, N, W), lambda b, g, s: (b, g, s, 0, 0))
    return pl.pallas_call(
        body, name="ssd_fwd", grid=(B, G, L // LB), in_specs=[xw, bsp, csp, xw, dts, hv, hv, wv, wv],
        out_specs=[xw, xw, sts],
        out_shape=[jax.ShapeDtypeStruct((B, L, DI), F32), jax.ShapeDtypeStruct((B, L, DI), BF16),
                   jax.ShapeDtypeStruct((B, G, NC, N, W), F32)],
        scratch_shapes=[pltpu.VMEM((N, W), F32), pltpu.VMEM((LANES, C), F32), pltpu.VMEM((C, W), F32)],
        compiler_params=_cparams("parallel", "parallel", "arbitrary"))(xbca, xbca, xbca, zx, dtc, bias, Aneg, Dexp, nw)


def _ssd_bwd(xbca, zx, dtc, ypre, dyn, st, bias, Aneg, Dexp, nw, hpg):
    B, L, _ = xbca.shape
    G, N, C = M_GROUPS, M_D_STATE, M_CHUNK
    W = hpg * M_HEADDIM
    DI = G * W
    NC = L // C
    LB = min(L, 4 * C)
    ncb = LB // C
    nsb = L // LB

    def body(xs_ref, b_ref, c_ref, z_ref, dt_ref, y_ref, dyn_ref, st_ref, bias_ref, a_ref, d_ref, nw_ref,
             dxs_ref, dbc_ref, dz_ref, ddt_ref, dnw_ref, dd_ref, da_ref, dbias_ref, DST, ar_sc, ae_sc):
        @pl.when(pl.program_id(2) == 0)
        def _():
            DST[...] = jnp.zeros_like(DST)
            dnw_ref[...] = jnp.zeros_like(dnw_ref)
            dd_ref[...] = jnp.zeros_like(dd_ref)
            da_ref[...] = jnp.zeros_like(da_ref)
            dbias_ref[...] = jnp.zeros_like(dbias_ref)

        E, Ebig, causal, head_of_lane = _ssd_consts(hpg, W)
        bias_ = bias_ref[0]
        Aneg_ = a_ref[0]
        Dv = d_ref[...]
        nwv = nw_ref[...]
        lane = _iota((1, LANES), 1)
        subl = _iota((LANES, 1), 0)
        lastrow = _iota((C, W), 0) == C - 1

        def chunk(i, carry):
            ci = ncb - 1 - i
            r0 = pl.multiple_of(ci * C, C)
            rows = pl.ds(r0, C)
            xs = xs_ref[0, rows, :]
            Bm = b_ref[0, rows, :]
            Cm = c_ref[0, rows, :]
            zr = z_ref[0, rows, :]
            dtr = dt_ref[0, 0, ci]
            st_in = st_ref[0, 0, ci]
            dst = DST[...]
            f = _ssd_chunk_fwd(xs, Bm, Cm, dtr, bias_, Aneg_, E, Ebig, causal, head_of_lane, hpg, st_in, ar_sc, ae_sc)
            X, Xb, dec, ea, CB = f["X"], f["Xb"], f["dec"], f["ea"], f["CB"]
            y = y_ref[0, rows, :]
            sz = _silu(zr)
            yg = y * sz
            rstd = lax.rsqrt(jnp.mean(yg * yg, axis=-1, keepdims=True) + NORM_EPS)
            yh = yg * rstd
            dyn_ = dyn_ref[0, rows, :]
            dnw_ref[0, 0] += jnp.sum(dyn_ * yh, axis=0, keepdims=True)
            dyh = dyn_ * nwv
            dyg = rstd * (dyh - yh * jnp.mean(dyh * yh, axis=-1, keepdims=True))
            dz_ref[0, rows, :] = (dyg * y * _dsilu(zr)).astype(BF16)
            dy = dyg * sz
            dd_ref[0, 0] += jnp.sum(dy * xs, axis=0, keepdims=True)
            dxs = dy * Dv
            dYo = dy * ea
            daexp = dy * f["yoff"]
            dCm = _dot(dYo, st_in, NT)
            dst_in = _dot(Cm, dYo, TN)
            dyb = dy.astype(BF16)
            dX = jnp.zeros_like(xs)
            dCB = jnp.zeros((C, C), F32)
            da_col = jnp.zeros((C, LANES), F32)
            da_row = jnp.zeros((LANES, C), F32)
            for j in range(hpg):
                Lj = f["Ls"][j]
                Gj = CB * Lj
                dYj = jnp.where(head_of_lane == j, dyb, jnp.zeros_like(dyb))
                dX = dX + _dot(Gj, dYj, TN)
                dGj = _dot(dYj, Xb, NT)
                dCB = dCB + dGj * Lj
                Wj = dGj * Gj
                da_col = da_col + jnp.sum(Wj, axis=1, keepdims=True) * (lane == j).astype(F32)
                da_row = da_row + (subl == j).astype(F32) * jnp.sum(Wj, axis=0, keepdims=True)
            dCm = dCm + _dot(dCB, Bm)
            dBm = _dot(dCB, Cm, TN)
            ela = jnp.exp(f["alast"])
            dalast = jnp.sum(dst * st_in, axis=0, keepdims=True) * ela
            DST[...] = dst * ela + dst_in
            dXd = _dot(Bm, dst)
            dBm = dBm + _dot(X * dec, dst, NT)
            dX = dX + dXd * dec
            ddec = dXd * X * dec
            dalast = dalast + jnp.sum(ddec, axis=0, keepdims=True)
            daexp = daexp - ddec + jnp.where(lastrow, dalast, 0.0)
            dxs = dxs + dX * f["dtexp"]
            ddtexp = dX * xs
            ddt = _dot_exact(ddtexp, E, NT, passes=2)
            da_c = _dot_exact(daexp, E, NT, passes=2) + da_col - da_row.T
            dAd = _cumsum_rows(da_c, reverse=True)
            ddt = ddt + dAd * Aneg_
            da_ref[0, 0] += jnp.sum(dAd * f["dt"], axis=0, keepdims=True) * Aneg_
            ddtr = ddt * jax.nn.sigmoid(dtr + bias_)
            dbias_ref[0, 0] += jnp.sum(ddtr, axis=0, keepdims=True)
            ddt_ref[0, 0, ci] = ddtr
            dxs_ref[0, rows, :] = dxs
            dbc_ref[0, 0, rows, :] = dBm
            dbc_ref[1, 0, rows, :] = dCm
            return carry

        lax.fori_loop(0, ncb, chunk, 0)

    def rev(s):
        return nsb - 1 - s

    xw = pl.BlockSpec((1, LB, W), lambda b, g, s: (b, rev(s), g))
    bsp = pl.BlockSpec((1, LB, N), lambda b, g, s: (b, rev(s), DI // N + g))
    csp = pl.BlockSpec((1, LB, N), lambda b, g, s: (b, rev(s), DI // N + G + g))
    gsp = pl.BlockSpec((1, LB, N), lambda b, g, s: (b, rev(s), g))
    dts = pl.BlockSpec((1, 1, ncb, C, LANES), lambda b, g, s: (b, g, rev(s), 0, 0))
    hv = pl.BlockSpec((1, 1, LANES), lambda b, g, s: (g, 0, 0))
    wv = pl.BlockSpec((1, W), lambda b, g, s: (0, g))
    sts = pl.BlockSpec((1, 1, ncb, N, W), lambda b, g, s: (b, g, rev(s), 0, 0))
    accw = pl.BlockSpec((1, 1, 1, W), lambda b, g, s: (b, g, 0, 0))
    acch = pl.BlockSpec((1, 1, 1, LANES), lambda b, g, s: (b, g, 0, 0))
    return pl.pallas_call(
        body, name="ssd_bwd", grid=(B, G, nsb), in_specs=[xw, bsp, csp, xw, dts, xw, xw, sts, hv, hv, wv, wv],
        out_specs=[xw, pl.BlockSpec((2, 1, LB, N), lambda b, g, s: (0, b, rev(s), g)), xw, dts, accw, accw, acch, acch],
        out_shape=[jax.ShapeDtypeStruct((B, L, DI), F32), jax.ShapeDtypeStruct((2, B, L, G * N), F32),
                   jax.ShapeDtypeStruct((B, L, DI), BF16),
                   jax.ShapeDtypeStruct((B, G, NC, C, LANES), F32), jax.ShapeDtypeStruct((B, G, 1, W), F32),
                   jax.ShapeDtypeStruct((B, G, 1, W), F32), jax.ShapeDtypeStruct((B, G, 1, LANES), F32),
                   jax.ShapeDtypeStruct((B, G, 1, LANES), F32)],
        scratch_shapes=[pltpu.VMEM((N, W), F32), pltpu.VMEM((LANES, C), F32), pltpu.VMEM((C, W), F32)],
        compiler_params=_cparams("parallel", "parallel", "arbitrary"))(
            xbca, xbca, xbca, zx, dtc, ypre, dyn, st, bias, Aneg, Dexp, nw)


def _adamw(w, g, m, v, name):
    shape = w.shape
    n = w.size
    cols = shape[-1]
    rows = n // cols
    tr = rows
    for cand in (512, 256, 128, 64, 32, 16, 8):
        if rows % cand == 0 and cand * cols * 4 <= 1024 * 1024:
            tr = cand
            break
    c1 = 1.0 / (1.0 - ADAM_B1 ** ADAM_STEP)
    c2 = 1.0 / (1.0 - ADAM_B2 ** ADAM_STEP)

    def body(w_ref, g_ref, m_ref, v_ref, d_ref, mo_ref, vo_ref):
        g_ = g_ref[...]
        mn = ADAM_B1 * m_ref[...] + (1.0 - ADAM_B1) * g_
        vn = ADAM_B2 * v_ref[...] + (1.0 - ADAM_B2) * (g_ * g_)
        d_ref[...] = -ADAM_LR * ((mn * c1) / (jnp.sqrt(vn * c2) + ADAM_EPS) + ADAM_WD * w_ref[...])
        mo_ref[...] = mn
        vo_ref[...] = vn

    spec = pl.BlockSpec((tr, cols), lambda i: (i, 0))
    r2 = lambda a: a.reshape(rows, cols)
    outs = pl.pallas_call(
        body, name=name, grid=(rows // tr,), in_specs=[spec] * 4, out_specs=[spec] * 3,
        out_shape=[jax.ShapeDtypeStruct((rows, cols), F32)] * 3,
        compiler_params=_cparams("parallel"))(r2(w), r2(g), r2(m), r2(v))
    return tuple(o.reshape(shape) for o in outs)


def _lower_bounds(lb_logits):
    p = jax.nn.softmax(lb_logits.astype(F32), axis=0)
    return jnp.cumsum(p, axis=0) - p[0]


def _pad_cols(a, n):
    return a if a.shape[-1] == n else jnp.pad(a, [(0, 0)] * (a.ndim - 1) + [(0, n - a.shape[-1])])


def _heads_to_lanes(a, G, hpg):
    return _pad_cols(a.reshape(G, 1, hpg), LANES)


def _local_step(x, target, P, fetch, emit):
    B, L, D = x.shape
    T = B * L
    depth = P["mix_norm"].shape[0]
    H = D // HGRN_DK
    F_ = H * HGRN_DK
    DI = P["m_norm"].shape[1]
    G, N = M_GROUPS, M_D_STATE
    MH = DI // M_HEADDIM
    hpg = MH // G
    assert hpg <= 8
    W = hpg * M_HEADDIM
    CD = DI + 2 * G * N
    MIN = DI + CD + MH
    MPAD = -(-MIN // LANES) * LANES
    dff = P["f_conv_b"].shape[1] // 2
    NC = L // M_CHUNK
    lbs = _lower_bounds(P["hgrn_lb_logits"])

    h = x.reshape(T, D)
    saved = []
    for i in range(depth):
        j = i // 2
        Wl = dict(fetch(i, ("mix_in", "mix_out"), h))
        s = {"h_in": h, "W": Wl}
        u = _rmsnorm_fwd(h, P["mix_norm"][i], "mix_norm_fwd")
        s["u"] = u
        if i % 2 == 0:
            proj = _matmul(u, Wl["mix_in"], name="hgrn_in_fwd").reshape(B, L, 4 * F_)
            o, on, st = _hgrn_fwd(proj, lbs[j].reshape(1, F_), P["hgrn_gnorm"][j].reshape(1, HGRN_DK), H)
            h = _matmul(on.reshape(T, F_), Wl["mix_out"], res=h, name="hgrn_out_fwd")
            s.update(proj=proj, o=o, on=on, st=st)
        else:
            zx = _matmul(u, Wl["mix_in"], tb=True, tn=1152, name="m_in_fwd").reshape(B, L, MPAD)
            xbca = _mconv_fwd(zx, P["m_conv_w"][j], P["m_conv_b"][j], DI, CD)
            dtr = zx[:, :, DI + CD:DI + CD + MH].reshape(B, NC, M_CHUNK, G, hpg).transpose(0, 3, 1, 2, 4)
            dtc = _pad_cols(dtr, LANES)
            bias = _heads_to_lanes(P["m_dt_bias"][j], G, hpg)
            Aneg = _heads_to_lanes(-jnp.exp(P["m_A_log"][j]), G, hpg)
            Dexp = jnp.repeat(P["m_D"][j], M_HEADDIM).reshape(1, DI)
            nw = P["m_norm"][j].reshape(1, DI)
            ypre, yn, st = _ssd_fwd(xbca, zx, dtc, bias, Aneg, Dexp, nw, hpg)
            h = _matmul(yn.reshape(T, DI), Wl["mix_out"], res=h, name="m_out_fwd")
            s.update(zx=zx, xbca=xbca, dtc=dtc, bias=bias, Aneg=Aneg, Dexp=Dexp, nw=nw, ypre=ypre, yn=yn, st=st)
        s["h_mid"] = h
        u2 = _rmsnorm_fwd(h, P["ffn_norm"][i], "ffn_norm_fwd")
        Wl.update(fetch(i, ("f_w_up", "f_w_down"), h))
        up = _matmul(u2, Wl["f_w_up"], name="ffn_up_fwd").reshape(B, L, 2 * dff)
        act = _ffn_mid_fwd(up, P["f_conv_w"][i], P["f_conv_b"][i], dff)
        h = _matmul(act.reshape(T, dff), Wl["f_w_down"], res=h, name="ffn_down_fwd")
        s.update(u2=u2, up=up, act=act)
        saved.append(s)

    loss, dh, dhb, d_final = _loss_head(h, P["final_norm"], target.reshape(T, D))

    g = {k: [None] * P[k].shape[0] for k in ("mix_norm", "ffn_norm", "hgrn_gnorm", "m_conv_w", "m_conv_b", "m_dt_bias",
                                              "m_A_log", "m_D", "m_norm", "f_conv_w", "f_conv_b")}
    dlbs = [None] * lbs.shape[0]
    for i in reversed(range(depth)):
        j = i // 2
        s = saved[i]
        Wl = s["W"]
        gm = {}

        def dw(key, a, b, name, **kw):
            gm[key] = _matmul(a, b, ta=True, out_dtype=BF16, name=name, **kw)

        dact = _matmul(dhb, Wl["f_w_down"], tb=True, name="ffn_down_dx").reshape(B, L, dff)
        dw("f_w_down", s["act"].reshape(T, dff), dhb, "ffn_down_dw")
        dup, dcw, dcb = _ffn_mid_bwd(s["up"], dact, P["f_conv_w"][i], P["f_conv_b"][i], dff)
        g["f_conv_w"][i] = jnp.sum(dcw, axis=0)
        g["f_conv_b"][i] = jnp.sum(dcb, axis=(0, 1))
        dup = dup.reshape(2, T, dff)
        dw("f_w_up", s["u2"], dup, "ffn_up_dw", b_parts=True)
        du2 = _matmul(dup, Wl["f_w_up"], a_parts=True, tb=True, name="ffn_up_dx")
        dh, dhb, g["ffn_norm"][i] = _rmsnorm_bwd(s["h_mid"], P["ffn_norm"][i], du2, dh, "ffn_norm_bwd")
        if i % 2 == 0:
            don = _matmul(dhb, Wl["mix_out"], tb=True, name="hgrn_out_dx").reshape(B, L, F_)
            dw("mix_out", s["on"].reshape(T, F_), dhb, "hgrn_out_dw")
            dproj, dlb, dgn = _hgrn_bwd(s["proj"], s["o"], don, s["st"], lbs[j].reshape(1, F_),
                                        P["hgrn_gnorm"][j].reshape(1, HGRN_DK), H)
            dlbs[j] = jnp.sum(dlb, axis=(0, 1))
            g["hgrn_gnorm"][j] = jnp.sum(dgn, axis=(0, 1, 2))
            dproj = dproj.reshape(4, T, F_)
            dw("mix_in", s["u"], dproj, "hgrn_in_dw", b_parts=True)
            du = _matmul(dproj, Wl["mix_in"], a_parts=True, tb=True, name="hgrn_in_dx")
        else:
            dyn = _matmul(dhb, Wl["mix_out"], tb=True, name="m_out_dx").reshape(B, L, DI)
            dw("mix_out", s["yn"].reshape(T, DI), dhb, "m_out_dw")
            dxs, dbc, dz, ddt, dnw, dD, dA, dbias = _ssd_bwd(s["xbca"], s["zx"], s["dtc"], s["ypre"], dyn, s["st"],
                                                             s["bias"], s["Aneg"], s["Dexp"], s["nw"], hpg)
            g["m_norm"][j] = jnp.sum(dnw, axis=(0, 2)).reshape(DI)
            g["m_D"][j] = jnp.sum(dD, axis=(0, 2)).reshape(MH, M_HEADDIM).sum(axis=-1)
            g["m_A_log"][j] = jnp.sum(dA, axis=(0, 2))[:, :hpg].reshape(MH)
            g["m_dt_bias"][j] = jnp.sum(dbias, axis=(0, 2))[:, :hpg].reshape(MH)
            cw, cb = P["m_conv_w"][j], P["m_conv_b"][j]
            dxx, dcw_x, dcb_x = _mconv_bwd(s["zx"], dxs[None], cw, cb, DI, 0, "mconv_bwd_x")
            dxb, dcw_b, dcb_b = _mconv_bwd(s["zx"], dbc, cw, cb, DI, DI, "mconv_bwd_bc")
            g["m_conv_w"][j] = jnp.concatenate([jnp.sum(dcw_x, axis=0), jnp.sum(dcw_b, axis=0)], axis=-1)
            g["m_conv_b"][j] = jnp.concatenate([jnp.sum(dcb_x, axis=(0, 1)), jnp.sum(dcb_b, axis=(0, 1))], axis=-1)
            ddt_t = _pad_cols(ddt[..., :hpg].transpose(0, 2, 3, 1, 4).reshape(T, MH), MPAD - DI - CD).astype(BF16)
            pieces = [(dz.reshape(T, DI), 0), (dxx.reshape(T, DI), DI), (dxb.reshape(T, 2 * G * N), 2 * DI), (ddt_t, DI + CD)]
            du = None
            gm["mix_in"] = lax.empty((MPAD, D), BF16)
            for n_, (piece, off) in enumerate(pieces):
                gm["mix_in"] = _matmul(piece, s["u"], ta=True, out_dtype=BF16, out=gm["mix_in"], out_off=off,
                                       name="m_in_dw%d" % n_)
                du = _matmul(piece, Wl["mix_in"], b_off=off, res=du, name="m_in_dx%d" % n_)
        dep = emit(i, gm)
        dh, dhb, g["mix_norm"][i] = _rmsnorm_bwd(s["h_in"], P["mix_norm"][i], du, dh, "mix_norm_bwd", dep=dep)

    grads = {k: jnp.stack(vs) for k, vs in g.items()}
    grads["final_norm"] = d_final
    _, lb_vjp = jax.vjp(_lower_bounds, P["hgrn_lb_logits"])
    grads["hgrn_lb_logits"] = lb_vjp(jnp.stack(dlbs))[0]
    return loss, dh.reshape(B, L, D), grads


ANY = pl.BlockSpec(memory_space=pl.ANY)
N_CHIPS = 4
N_DEV = 8


def _place():
    x, y, c = lax.axis_index("x"), lax.axis_index("y"), lax.axis_index("c")
    sibling = (x, y, 1 - c)
    chips = [(1 - x, y), (x, 1 - y), (1 - x, 1 - y)]
    return x, y, c, sibling, chips


def _remote(src, dst, send_sem, recv_sem, to):
    return pltpu.make_async_remote_copy(src_ref=src, dst_ref=dst, send_sem=send_sem, recv_sem=recv_sem, device_id=to,
                                        device_id_type=MESH)


KIND_AXIS = {"hgrn_w_in": "col", "f_w_up": "col", "m_w_in_t": "row", "hgrn_w_out": "row", "m_w_out": "row", "f_w_down": "row"}
KINDS = tuple(KIND_AXIS)
PEER_MASKS = (2, 1, 3)
ALL = slice(None)


def _chip_win(axis, cw, s):
    return (ALL, slice(s * cw, (s + 1) * cw)) if axis == "col" else (slice(s * cw, (s + 1) * cw), ALL)


def _half_win(axis, rows, cols, h):
    return (slice(h * rows // 2, (h + 1) * rows // 2), ALL) if axis == "col" else (ALL, slice(h * cols // 2, (h + 1) * cols // 2))


def _per_place(fn):
    x, y, c, sibling, chips = _place()
    chip = 2 * x + y
    for s in range(N_CHIPS):
        for cc in range(2):
            @pl.when(jnp.logical_and(chip == s, c == cc))
            def _():
                fn(s, cc, c, sibling, chips)


HBM = pl.BlockSpec(memory_space=pltpu.HBM)
SEM = pl.BlockSpec(memory_space=pltpu.SEMAPHORE)
EFFECT = pltpu.SideEffectType.DATAFLOW_SIDE_EFFECTING


def _cell(axis, rows, cols, cw, s, h):
    if axis == "col":
        return (slice(h * rows // 2, (h + 1) * rows // 2), slice(s * cw, (s + 1) * cw))
    return (slice(s * cw, (s + 1) * cw), slice(h * cols // 2, (h + 1) * cols // 2))


def _in_hbm(a):
    return pltpu.with_memory_space_constraint(a, pltpu.HBM)


def _stage_shard(kind, shard, layer, chip, pad_rows=0):
    _, R, C = shard.shape
    axis = KIND_AXIS[kind]
    tr, tc = _row_tile(R), _pick_tile(C, 2048)
    nr, nc = R // tr, C // tc
    full = (R, N_CHIPS * C) if axis == "col" else (N_CHIPS * R + pad_rows, C)

    def body(s_ref, x_ref, o_ref):
        o_ref[...] = x_ref[...].astype(BF16)

    if axis == "col":
        dst = pl.BlockSpec((tr, tc), lambda i, j, s_ref: (i, s_ref[0] * nc + j))
    else:
        dst = pl.BlockSpec((tr, tc), lambda i, j, s_ref: (s_ref[0] * nr + i, j))
    grid_spec = pltpu.PrefetchScalarGridSpec(
        num_scalar_prefetch=1, grid=(nr, nc),
        in_specs=[pl.BlockSpec((None, tr, tc), lambda i, j, s_ref: (layer, i, j))], out_specs=dst)
    out = pl.pallas_call(
        body, name="stage_" + kind, grid_spec=grid_spec, out_shape=jax.ShapeDtypeStruct(full, BF16),
        compiler_params=_cparams("parallel", "parallel"))(chip.reshape(1).astype(jnp.int32), shard)
    if pad_rows:
        rows0 = N_CHIPS * R
        pr = math.gcd(rows0, pad_rows)

        def zero_body(x_ref, o_ref):
            o_ref[...] = jnp.zeros_like(o_ref)

        out = pl.pallas_call(
            zero_body, name="zero_pad_" + kind, grid=(pad_rows // pr,), in_specs=[ANY],
            out_specs=pl.BlockSpec((pr, C), lambda i: (rows0 // pr + i, 0)), out_shape=jax.ShapeDtypeStruct(full, BF16),
            input_output_aliases={0: 0}, compiler_params=_cparams("parallel"))(out)
    return out


def _gather_start(items, mats, cws, after):
    n = len(items)

    def body(*refs):
        send_sems, recv_sems, token = refs[n + 1], refs[n + 2], refs[-1]
        m = refs[n + 3:2 * n + 3]

        def run(s, cc, c, sibling, chips):
            for q, (k, _) in enumerate(items):
                r, c_ = m[q].shape
                mine = m[q].at[_cell(KIND_AXIS[k], r, c_, cws[k], s, cc)]
                for j, (px, py) in enumerate(chips):
                    _remote(mine, mine, send_sems.at[3 * q + j], recv_sems.at[3 * q + j], (px, py, c)).start()

        _per_place(run)
        token[...] = jnp.zeros_like(token)

    outs = pl.pallas_call(
        body, name="gather_start", in_specs=[HBM] * n + [ANY],
        out_specs=[SEM, SEM] + [HBM] * n + [pl.BlockSpec(memory_space=pltpu.VMEM)],
        out_shape=[pltpu.SemaphoreType.DMA((3 * n,)), pltpu.SemaphoreType.DMA((3 * n,))]
        + [pltpu.HBM(a.shape, a.dtype) for a in mats] + [jax.ShapeDtypeStruct((8, LANES), F32)],
        input_output_aliases={q: 2 + q for q in range(n)},
        compiler_params=pltpu.CompilerParams(has_side_effects=EFFECT),
    )(*[_in_hbm(a) for a in mats], after)
    return outs[0], outs[1], list(outs[2:2 + n]), outs[-1]


def _gather_wait(items, idx, mats, send_sems, recv_sems, cws, after, name):
    n = len(idx)

    def body(*refs):
        m = refs[:n]
        s_sems, r_sems = refs[n], refs[n + 1]

        def run(s, cc, c, sibling, chips):
            for a, q in enumerate(idx):
                k = items[q][0]
                r, c_ = m[a].shape
                mine = m[a].at[_cell(KIND_AXIS[k], r, c_, cws[k], s, cc)]
                for j, (px, py) in enumerate(chips):
                    theirs = m[a].at[_cell(KIND_AXIS[k], r, c_, cws[k], s ^ PEER_MASKS[j], cc)]
                    cp = _remote(mine, theirs, s_sems.at[3 * q + j], r_sems.at[3 * q + j], (px, py, c))
                    cp.wait_send()
                    cp.wait_recv()

        _per_place(run)

    outs = pl.pallas_call(
        body, name=name, in_specs=[HBM] * n + [SEM, SEM, ANY], out_specs=[HBM] * n,
        out_shape=[pltpu.HBM(a.shape, a.dtype) for a in mats], input_output_aliases={a: a for a in range(n)},
        compiler_params=pltpu.CompilerParams(has_side_effects=EFFECT),
    )(*mats, send_sems, recv_sems, after)
    return list(outs)


def _forward_halves(kinds, mats, cws, name):
    n = len(mats)

    def body(*refs):
        m = refs[n:2 * n]
        send_sems, recv_sems = refs[2 * n:]

        def run(s, cc, c, sibling, chips):
            cps = []
            for a, k in enumerate(kinds):
                r, c_ = m[a].shape
                for j in range(3):
                    have = m[a].at[_cell(KIND_AXIS[k], r, c_, cws[k], s ^ PEER_MASKS[j], cc)]
                    cps.append(_remote(have, have, send_sems.at[3 * a + j], recv_sems.at[3 * a + j], sibling))
            for cp in cps:
                cp.start()
            for cp in cps:
                cp.wait()

        _per_place(run)

    outs = pl.pallas_call(
        body, name=name, in_specs=[ANY] * n, out_specs=[ANY] * n,
        out_shape=[jax.ShapeDtypeStruct(a.shape, a.dtype) for a in mats], input_output_aliases={a: a for a in range(n)},
        scratch_shapes=[pltpu.SemaphoreType.DMA((3 * n,)), pltpu.SemaphoreType.DMA((3 * n,))],
    )(*mats)
    return list(outs)


def _swap_halves(kinds, gms, name):
    n = len(gms)
    half_shapes = [(g.shape[0] // 2, g.shape[1]) if KIND_AXIS[k] == "col" else (g.shape[0], g.shape[1] // 2)
                   for k, g in zip(kinds, gms)]

    def body(*refs):
        g, ra = refs[:n], refs[n:2 * n]
        send_sems, recv_sems = refs[2 * n:]

        def run(s, cc, c, sibling, chips):
            cps = []
            for a, k in enumerate(kinds):
                r, c_ = g[a].shape
                cps.append(_remote(g[a].at[_half_win(KIND_AXIS[k], r, c_, 1 - cc)], ra[a], send_sems.at[a], recv_sems.at[a],
                                   sibling))
            for cp in cps:
                cp.start()
            for cp in cps:
                cp.wait()

        _per_place(run)

    outs = pl.pallas_call(
        body, name=name, in_specs=[ANY] * n, out_specs=[ANY] * n,
        out_shape=[jax.ShapeDtypeStruct(hs, BF16) for hs in half_shapes],
        scratch_shapes=[pltpu.SemaphoreType.DMA((n,)), pltpu.SemaphoreType.DMA((n,))],
    )(*gms)
    return list(outs)


def _win_shape(kind, pa, cw):
    return (pa.shape[0], cw) if KIND_AXIS[kind] == "col" else (cw, pa.shape[1])


def _scatter_start(kinds, pas, cws, name):
    n = len(pas)
    lands = [lax.empty((3,) + _win_shape(k, p, cws[k]), BF16) for k, p in zip(kinds, pas)]

    def body(*refs):
        send_sems, recv_sems, token = refs[2 * n], refs[2 * n + 1], refs[-1]
        p, rb = refs[2 * n + 2:3 * n + 2], refs[3 * n + 2:4 * n + 2]

        def run(s, cc, c, sibling, chips):
            for a, k in enumerate(kinds):
                for j, (px, py) in enumerate(chips):
                    src = p[a].at[_chip_win(KIND_AXIS[k], cws[k], s ^ PEER_MASKS[j])]
                    _remote(src, rb[a].at[j], send_sems.at[3 * a + j], recv_sems.at[3 * a + j], (px, py, c)).start()

        _per_place(run)
        token[...] = jnp.zeros_like(token)

    outs = pl.pallas_call(
        body, name=name, in_specs=[HBM] * (2 * n),
        out_specs=[SEM, SEM] + [HBM] * (2 * n) + [pl.BlockSpec(memory_space=pltpu.VMEM)],
        out_shape=[pltpu.SemaphoreType.DMA((3 * n,)), pltpu.SemaphoreType.DMA((3 * n,))]
        + [pltpu.HBM(a.shape, a.dtype) for a in pas + lands] + [jax.ShapeDtypeStruct((8, LANES), F32)],
        input_output_aliases={q: 2 + q for q in range(2 * n)},
        compiler_params=pltpu.CompilerParams(has_side_effects=EFFECT),
    )(*[_in_hbm(a) for a in pas + lands])
    return outs[0], outs[1], list(outs[2:2 + n]), list(outs[2 + n:2 + 2 * n]), outs[-1]


def _scatter_wait(kinds, pas, lands, send_sems, recv_sems, cws, after, name):
    n = len(pas)

    def body(*refs):
        p, rb = refs[:n], refs[n:2 * n]
        s_sems, r_sems = refs[2 * n], refs[2 * n + 1]

        def run(s, cc, c, sibling, chips):
            for a, k in enumerate(kinds):
                for j, (px, py) in enumerate(chips):
                    src = p[a].at[_chip_win(KIND_AXIS[k], cws[k], s ^ PEER_MASKS[j])]
                    cp = _remote(src, rb[a].at[j], s_sems.at[3 * a + j], r_sems.at[3 * a + j], (px, py, c))
                    cp.wait_send()
                    cp.wait_recv()

        _per_place(run)

    outs = pl.pallas_call(
        body, name=name, in_specs=[HBM] * (2 * n) + [SEM, SEM, ANY], out_specs=[HBM] * (2 * n),
        out_shape=[pltpu.HBM(a.shape, a.dtype) for a in pas + lands], input_output_aliases={a: a for a in range(2 * n)},
        compiler_params=pltpu.CompilerParams(has_side_effects=EFFECT),
    )(*pas, *lands, send_sems, recv_sems, after)
    return list(outs[:n]), list(outs[n:])


def _share_halves(g):
    nq = len(KINDS)

    def body(*refs):
        out = dict(zip(KINDS, refs[nq:2 * nq]))
        send_sems, recv_sems = refs[2 * nq:]

        def run(s, cc, c, sibling, chips):
            cps = []
            for q, k in enumerate(KINDS):
                _, r, c_ = out[k].shape
                mine = out[k].at[(ALL,) + _half_win(KIND_AXIS[k], r, c_, cc)]
                cps.append(_remote(mine, mine, send_sems.at[q], recv_sems.at[q], sibling))
            for cp in cps:
                cp.start()
            for cp in cps:
                cp.wait()

        _per_place(run)

    outs = pl.pallas_call(
        body, name="share_halves", in_specs=[ANY] * nq, out_specs=[ANY] * nq,
        out_shape=[jax.ShapeDtypeStruct(g[k].shape, F32) for k in KINDS],
        input_output_aliases={q: q for q in range(nq)},
        scratch_shapes=[pltpu.SemaphoreType.DMA((nq,)), pltpu.SemaphoreType.DMA((nq,))],
    )(*[g[k] for k in KINDS])
    return dict(zip(KINDS, outs))


def _all_gather_small(xs, name):
    m_per, n = xs.shape

    def body(x_ref, out_ref, send_sems, recv_sems, local_sem):
        x, y, c, sibling, chips = _place()
        me = (x, y, c)

        def rows(px, py, pc):
            return out_ref.at[pl.ds((4 * px + 2 * py + pc) * m_per, m_per), :]

        def copy(k, block, to, src=None):
            return _remote(rows(*block) if src is None else src, rows(*block), send_sems.at[k], recv_sems.at[k], to)

        mine = pltpu.make_async_copy(x_ref, rows(*me), local_sem)
        mine.start()
        first = [copy(0, me, sibling, src=x_ref)]
        first += [copy(1 + j, me, (*chip, c), src=x_ref) for j, chip in enumerate(chips)]
        for cp in first:
            cp.start()
        passed = [copy(4 + j, (*chip, c), sibling) for j, chip in enumerate(chips)]
        for j, chip in enumerate(chips):
            copy(1 + j, (*chip, c), me).wait_recv()
            passed[j].start()
        copy(0, sibling, me).wait_recv()
        for j, chip in enumerate(chips):
            copy(4 + j, (*chip, 1 - c), me).wait_recv()
        for cp in first + passed:
            cp.wait_send()
        mine.wait()

    vm = pl.BlockSpec(memory_space=pltpu.VMEM)
    return pl.pallas_call(
        body, name=name, in_specs=[vm], out_specs=vm, out_shape=jax.ShapeDtypeStruct((N_DEV * m_per, n), xs.dtype),
        scratch_shapes=[pltpu.SemaphoreType.DMA((7,)), pltpu.SemaphoreType.DMA((7,)), pltpu.SemaphoreType.DMA],
        compiler_params=pltpu.CompilerParams(vmem_limit_bytes=VMEM_LIMIT_BYTES),
    )(xs)


def _row_tile(rows, cap=256):
    for mult in (16, 8):
        best = None
        t = mult
        while t <= min(rows, cap):
            if rows % t == 0:
                best = t
            t += mult
        if best is not None:
            return best
    raise ValueError(rows)


def _add_sibling(kind, g, ra, core):
    R, C = ra.shape
    axis = KIND_AXIS[kind]
    tr, tc = _row_tile(R), _pick_tile(C, 2048)
    nr, nc = R // tr, C // tc

    def body(c_ref, a_ref, b_ref, o_ref):
        o_ref[...] = (a_ref[...].astype(F32) + b_ref[...].astype(F32)).astype(o_ref.dtype)

    if axis == "col":
        own = pl.BlockSpec((tr, tc), lambda i, j, c_ref: (c_ref[0] * nr + i, j))
    else:
        own = pl.BlockSpec((tr, tc), lambda i, j, c_ref: (i, c_ref[0] * nc + j))
    same = pl.BlockSpec((tr, tc), lambda i, j, c_ref: (i, j))
    grid_spec = pltpu.PrefetchScalarGridSpec(num_scalar_prefetch=1, grid=(nr, nc), in_specs=[own, same], out_specs=same)
    return pl.pallas_call(
        body, name="add_sibling_" + kind, grid_spec=grid_spec, out_shape=jax.ShapeDtypeStruct(ra.shape, BF16),
        compiler_params=_cparams("parallel", "parallel"))(core.reshape(1).astype(jnp.int32), g, ra)


def _sum_chips(kind, pa, rb, chip, core, out, layer):
    _, R, C = rb.shape
    axis = KIND_AXIS[kind]
    tr, tc = _row_tile(R), _pick_tile(C, 2048)
    nr, nc = R // tr, C // tc

    def body(s_ref, c_ref, a_ref, b0_ref, b1_ref, b2_ref, old_ref, o_ref):
        o_ref[...] = ((a_ref[...].astype(F32) + b0_ref[...].astype(F32)) + b1_ref[...].astype(F32)) + b2_ref[...].astype(F32)

    def rb_spec(n):
        return pl.BlockSpec((None, tr, tc), lambda i, j, s_ref, c_ref: (n, i, j))

    if axis == "col":
        own = pl.BlockSpec((tr, tc), lambda i, j, s_ref, c_ref: (i, s_ref[0] * nc + j))
        dst = pl.BlockSpec((None, tr, tc), lambda i, j, s_ref, c_ref: (layer, c_ref[0] * nr + i, j))
        assert out.shape[1:] == (2 * R, C)
    else:
        own = pl.BlockSpec((tr, tc), lambda i, j, s_ref, c_ref: (s_ref[0] * nr + i, j))
        dst = pl.BlockSpec((None, tr, tc), lambda i, j, s_ref, c_ref: (layer, i, c_ref[0] * nc + j))
        assert out.shape[1:] == (R, 2 * C)
    grid_spec = pltpu.PrefetchScalarGridSpec(
        num_scalar_prefetch=2, grid=(nr, nc), in_specs=[own, rb_spec(0), rb_spec(1), rb_spec(2), ANY], out_specs=dst)
    return pl.pallas_call(
        body, name="sum_chips_" + kind, grid_spec=grid_spec, out_shape=jax.ShapeDtypeStruct(out.shape, F32),
        input_output_aliases={6: 0}, compiler_params=_cparams("parallel", "parallel"))(
            chip.reshape(1).astype(jnp.int32), core.reshape(1).astype(jnp.int32), pa, rb, rb, rb, out)


def _sum_devices(gathered):
    M = gathered.shape[0] // N_DEV
    C = gathered.shape[1]

    def body(g_ref, o_ref):
        acc = g_ref[0:M, :]
        for d in range(1, N_DEV):
            acc = acc + g_ref[d * M:(d + 1) * M, :]
        o_ref[...] = acc

    vm = pl.BlockSpec(memory_space=pltpu.VMEM)
    return pl.pallas_call(body, name="sum_devices", in_specs=[vm], out_specs=vm, out_shape=jax.ShapeDtypeStruct((M, C), F32),
                          compiler_params=pltpu.CompilerParams(vmem_limit_bytes=VMEM_LIMIT_BYTES))(gathered)


WEIGHTS = ["mix_norm", "ffn_norm", "final_norm", "hgrn_w_in", "hgrn_lb_logits", "hgrn_gnorm", "hgrn_w_out", "m_w_in",
           "m_conv_w", "m_conv_b", "m_dt_bias", "m_A_log", "m_D", "m_norm", "m_w_out", "f_w_up", "f_conv_w", "f_conv_b",
           "f_w_down"]
BIG_COLS = ("hgrn_w_in", "m_w_in", "f_w_up")
BIG_ROWS = ("hgrn_w_out", "m_w_out", "f_w_down")
BIG = BIG_COLS + BIG_ROWS
SMALL_SHARDED = ("m_conv_w", "m_conv_b", "m_norm", "f_conv_w")
SMALL_REPLICATED = ("mix_norm", "ffn_norm", "final_norm", "hgrn_lb_logits", "hgrn_gnorm", "m_dt_bias", "m_A_log", "m_D",
                    "f_conv_b")
SMALL = SMALL_REPLICATED + SMALL_SHARDED


def _pack_rows(arrs, row_mult=8):
    flat = jnp.concatenate([a.reshape(-1).astype(F32) for a in arrs])
    unit = FLAT_COLS * row_mult
    n = -(-flat.size // unit) * unit
    return jnp.pad(flat, (0, n - flat.size)).reshape(-1, FLAT_COLS)


def _unpack_rows(buf, shapes):
    flat = buf.reshape(-1)
    out, off = [], 0
    for shp in shapes:
        n = math.prod(shp)
        out.append(flat[off:off + n].reshape(shp))
        off += n
    return out


def kernel(x, mix_norm, ffn_norm, final_norm, hgrn_w_in, hgrn_lb_logits, hgrn_gnorm, hgrn_w_out, m_w_in, m_conv_w, m_conv_b, m_dt_bias, m_A_log, m_D, m_norm, m_w_out, f_w_up, f_conv_w, f_conv_b, f_w_down, loss_target, m_mix_norm, m_ffn_norm, m_final_norm, m_hgrn_w_in, m_hgrn_lb_logits, m_hgrn_gnorm, m_hgrn_w_out, m_m_w_in, m_m_conv_w, m_m_conv_b, m_m_dt_bias, m_m_A_log, m_m_D, m_m_norm, m_m_w_out, m_f_w_up, m_f_conv_w, m_f_conv_b, m_f_w_down, v_mix_norm, v_ffn_norm, v_final_norm, v_hgrn_w_in, v_hgrn_lb_logits, v_hgrn_gnorm, v_hgrn_w_out, v_m_w_in, v_m_conv_w, v_m_conv_b, v_m_dt_bias, v_m_A_log, v_m_D, v_m_norm, v_m_w_out, v_f_w_up, v_f_conv_w, v_f_conv_b, v_f_w_down):
    given = dict(locals())
    w = {n: given[n] for n in WEIGHTS}
    mom1 = {n: given["m_" + n] for n in WEIGHTS}
    mom2 = {n: given["v_" + n] for n in WEIGHTS}
    chip = 2 * lax.axis_index("x") + lax.axis_index("y")
    core = lax.axis_index("c")

    shards = {k: w[k] for k in KINDS if k != "m_w_in_t"}
    shards["m_w_in_t"] = w["m_w_in"].transpose(0, 2, 1).astype(BF16)
    m_in = N_CHIPS * w["m_w_in"].shape[2]
    pad_rows = {"m_w_in_t": -(-m_in // LANES) * LANES - m_in}
    cws = {k: shards[k].shape[2] if KIND_AXIS[k] == "col" else shards[k].shape[1] for k in KINDS}
    depth = w["mix_norm"].shape[0]

    def layer_kinds(i):
        mixer = {"mix_in": ("hgrn_w_in", i // 2), "mix_out": ("hgrn_w_out", i // 2)} if i % 2 == 0 else \
                {"mix_in": ("m_w_in_t", i // 2), "mix_out": ("m_w_out", i // 2)}
        return {**mixer, "f_w_up": ("f_w_up", i), "f_w_down": ("f_w_down", i)}

    items = [it for i in range(depth) for it in layer_kinds(i).values()]
    staged = [_stage_shard(k, shards[k], l, chip, pad_rows.get(k, 0)) for k, l in items]
    own = _pack_rows([w[n] for n in SMALL_SHARDED])
    all_small = _all_gather_small(own, "gather_small_params")
    send_sems, recv_sems, mats, _ = _gather_start(items, staged, cws, all_small)
    all_small = all_small.reshape(N_CHIPS, 2, -1)[:, 0]
    per_chip = [_unpack_rows(all_small[s], [w[n].shape for n in SMALL_SHARDED]) for s in range(N_CHIPS)]
    P = {}
    for i, n in enumerate(SMALL_SHARDED):
        P[n] = jnp.concatenate([per_chip[s][i] for s in range(N_CHIPS)], axis=-1)
    for n in SMALL_REPLICATED:
        P[n] = w[n]

    def fetch(i, keys, h):
        lk = {key: layer_kinds(i)[key] for key in keys}
        idx = [items.index(it) for it in lk.values()]
        tag = "%d_%s" % (i, keys[0])
        got = _gather_wait(items, idx, [mats[q] for q in idx], send_sems, recv_sems, cws, h, "gather_wait_" + tag)
        got = _forward_halves([k for k, _ in lk.values()], got, cws, "forward_halves_" + tag)
        return dict(zip(lk.keys(), got))

    pending = {}

    def emit(i, gm):
        lk = layer_kinds(i)
        kinds = [k for k, _ in lk.values()]
        gms = [gm[key] for key in lk]
        ra = _swap_halves(kinds, gms, "swap_halves_%d" % i)
        pas = [_add_sibling(k, g_, r_, core) for k, g_, r_ in zip(kinds, gms, ra)]
        s_sems, r_sems, pas, lands, tok = _scatter_start(kinds, pas, cws, "scatter_start_%d" % i)
        pending[i] = (kinds, pas, lands, s_sems, r_sems)
        return tok

    loss_part, grad_x, g_full = _local_step(x, loss_target, P, fetch, emit)

    g_sh = {k: lax.empty(shards[k].shape, F32) for k in KINDS}
    for i in reversed(range(depth)):
        kinds, pas, lands, s_sems, r_sems = pending[i]
        pas, lands = _scatter_wait(kinds, pas, lands, s_sems, r_sems, cws, grad_x, "scatter_wait_%d" % i)
        for (k, l), p_, rb_ in zip(layer_kinds(i).values(), pas, lands):
            g_sh[k] = _sum_chips(k, p_, rb_, chip, core, g_sh[k], l)
    g_sh = _share_halves(g_sh)
    grads = {k: g_sh[k] for k in KINDS if k != "m_w_in_t"}
    grads["m_w_in"] = g_sh["m_w_in_t"].transpose(0, 2, 1)

    small_shapes = [g_full[n].shape for n in SMALL] + [(1,)]
    packed = _pack_rows([g_full[n] for n in SMALL] + [loss_part[0, 0:1]])
    summed = _sum_devices(_all_gather_small(packed, "gather_small_grads"))
    small = _unpack_rows(summed, small_shapes)
    loss = small[-1][0]
    for n, gs in zip(SMALL, small[:-1]):
        if n in SMALL_SHARDED:
            width = w[n].shape[-1]
            gs = lax.dynamic_slice_in_dim(gs, chip * width, width, axis=gs.ndim - 1)
        grads[n] = gs

    delta, new_m, new_v = {}, {}, {}
    for n in BIG:
        delta[n], new_m[n], new_v[n] = _adamw(w[n], grads[n], mom1[n], mom2[n], "adamw_" + n)
    shapes = [w[n].shape for n in SMALL]
    ds, ms, vs = _adamw(_pack_rows([w[n] for n in SMALL]), _pack_rows([grads[n] for n in SMALL]),
                        _pack_rows([mom1[n] for n in SMALL]), _pack_rows([mom2[n] for n in SMALL]), "adamw_small")
    for n, d_, m_, v_ in zip(SMALL, _unpack_rows(ds, shapes), _unpack_rows(ms, shapes), _unpack_rows(vs, shapes)):
        delta[n], new_m[n], new_v[n] = d_, m_, v_

    return (loss, grad_x, *[grads[n] for n in WEIGHTS], *[delta[n] for n in WEIGHTS], *[new_m[n] for n in WEIGHTS],
            *[new_v[n] for n in WEIGHTS])
```

```python
import functools
import math

import jax
import jax.numpy as jnp
from jax import lax
from jax.experimental import pallas as pl
from jax.experimental.pallas import tpu as pltpu

F32 = jnp.float32
BF16 = jnp.bfloat16
NORM_EPS = 1e-5
HGRN_DK = 128
HGRN_CHUNK = 64
HGRN_HEADS_PER_STEP = 2
HGRN_SEQ_BLOCK = 1024
M_HEADDIM = 64
M_GROUPS = 8
M_D_STATE = 128
M_CONV = 4
M_CHUNK = 128
FFN_CONV = 3
EXP_CLIP = 80.0
LANES = 128
VMEM_LIMIT_BYTES = 56 * 1024 * 1024
FLAT_COLS = 1024
ADAM_LR, ADAM_B1, ADAM_B2, ADAM_EPS, ADAM_WD, ADAM_STEP = 0.001, 0.9, 0.999, 1e-08, 0.01, 10
MESH = pl.DeviceIdType.MESH

NN = (((1,), (0,)), ((), ()))
NT = (((1,), (1,)), ((), ()))
TN = (((0,), (0,)), ((), ()))


def _cparams(*sems):
    return pltpu.CompilerParams(dimension_semantics=sems, vmem_limit_bytes=VMEM_LIMIT_BYTES)


def _dot(a, b, dn=NN):
    return lax.dot_general(a.astype(BF16), b.astype(BF16), dn, preferred_element_type=F32)


def _dot_exact(x, m, dn=NN, passes=3, x_first=True):
    acc = None
    r = x
    for _ in range(passes):
        p = r.astype(BF16)
        r = r - p.astype(F32)
        t = lax.dot_general(p, m, dn, preferred_element_type=F32) if x_first else lax.dot_general(m, p, dn, preferred_element_type=F32)
        acc = t if acc is None else acc + t
    return acc


def _iota(shape, dim):
    return lax.broadcasted_iota(jnp.int32, shape, dim)


def _cumsum_rows(x, reverse=False):
    n = x.shape[0]
    row = _iota(x.shape, 0)
    s = 1
    while s < n:
        if reverse:
            x = x + jnp.where(row < n - s, pltpu.roll(x, n - s, 0), 0.0)
        else:
            x = x + jnp.where(row >= s, pltpu.roll(x, s, 0), 0.0)
        s *= 2
    return x


def _silu(x):
    return x * jax.nn.sigmoid(x)


def _dsilu(x):
    s = jax.nn.sigmoid(x)
    return s * (1.0 + x * (1.0 - s))


def _pick_tile(dim, pref):
    if dim <= pref:
        return dim
    best = None
    t = LANES
    while t <= pref:
        if dim % t == 0:
            best = t
        t += LANES
    assert best is not None, (dim, pref)
    return best


def _matmul(a, b, *, ta=False, tb=False, res=None, out_dtype=F32, tm=1024, tn=1024, tk=2048, name,
            a_parts=False, b_parts=False, b_layer=None, b_off=0, out=None, out_layer=None, out_off=0):
    a = a.astype(BF16)
    b = b.astype(BF16)
    if a_parts:
        assert not ta
        pa, M, kp = a.shape
        K = pa * kp
    else:
        M, K = (a.shape[1], a.shape[0]) if ta else a.shape
    bsh = b.shape[1:] if b_layer is not None else b.shape
    if b_parts:
        assert not tb
        pb, _, np_ = bsh
        N = pb * np_
    else:
        N = bsh[0] if tb else bsh[1]
    tm, tn, tk = _pick_tile(M, tm), _pick_tile(np_ if b_parts else N, tn), _pick_tile(kp if a_parts else K, tk)
    nk = K // tk
    dn = (((0 if ta else 1,), (1 if tb else 0,)), ((), ()))
    assert b_off % tk == 0 and out_off % tm == 0

    def body(*refs):
        refs = list(refs)
        acc = refs.pop() if nk > 1 else None
        o_ref = refs.pop()
        if out is not None:
            refs.pop()
        a_ref, b_ref = refs[0], refs[1]
        r_ref = refs[2] if res is not None else None
        k = pl.program_id(2)

        def prod():
            return lax.dot_general(a_ref[...], b_ref[...], dn, preferred_element_type=F32)

        def finish(r):
            if res is not None:
                r = r + r_ref[...]
            o_ref[...] = r.astype(out_dtype)

        if nk == 1:
            finish(prod())
            return

        @pl.when(k == 0)
        def _():
            acc[...] = prod()

        @pl.when(jnp.logical_and(k > 0, k < nk - 1))
        def _():
            acc[...] += prod()

        @pl.when(k == nk - 1)
        def _():
            finish(acc[...] + prod())

    if a_parts:
        kpb = kp // tk
        a_spec = pl.BlockSpec((None, tm, tk), lambda i, j, k: (k // kpb, i, k % kpb))
    elif ta:
        a_spec = pl.BlockSpec((tk, tm), lambda i, j, k: (k, i))
    else:
        a_spec = pl.BlockSpec((tm, tk), lambda i, j, k: (i, k))
    lead = () if b_layer is None else (b_layer,)
    lead_blk = () if b_layer is None else (None,)
    kb0 = b_off // tk
    if b_parts:
        npb = np_ // tn
        b_spec = pl.BlockSpec(lead_blk + (None, tk, tn), lambda i, j, k: lead + (j // npb, k, j % npb))
    elif tb:
        b_spec = pl.BlockSpec(lead_blk + (tn, tk), lambda i, j, k: lead + (j, k))
    else:
        b_spec = pl.BlockSpec(lead_blk + (tk, tn), lambda i, j, k: lead + (kb0 + k, j))
    r_spec = pl.BlockSpec((tm, tn), lambda i, j, k: (i, j))
    in_specs = [a_spec, b_spec] + ([r_spec] if res is not None else [])
    args = (a, b) + ((res,) if res is not None else ())
    if out is None:
        o_spec, out_shape, aliases = r_spec, jax.ShapeDtypeStruct((M, N), out_dtype), {}
    else:
        assert out.dtype == out_dtype and out.shape[-1] == N
        olead = () if out_layer is None else (out_layer,)
        olead_blk = () if out_layer is None else (None,)
        ob0 = out_off // tm
        o_spec = pl.BlockSpec(olead_blk + (tm, tn), lambda i, j, k: olead + (ob0 + i, j))
        out_shape = jax.ShapeDtypeStruct(out.shape, out.dtype)
        aliases = {len(args): 0}
        in_specs = in_specs + [pl.BlockSpec(memory_space=pl.ANY)]
        args = args + (out,)
    return pl.pallas_call(
        body, name=name, grid=(M // tm, N // tn, nk), in_specs=in_specs, out_specs=o_spec, out_shape=out_shape,
        scratch_shapes=[pltpu.VMEM((tm, tn), F32)] if nk > 1 else [], input_output_aliases=aliases,
        compiler_params=_cparams("parallel", "parallel", "arbitrary"))(*args)


def _rmsnorm_fwd(h, w, name):
    T, D = h.shape
    tm = _pick_tile(T, 256)

    def body(h_ref, w_ref, u_ref):
        x = h_ref[...]
        r = lax.rsqrt(jnp.mean(x * x, axis=-1, keepdims=True) + NORM_EPS)
        u_ref[...] = (x * r * w_ref[...]).astype(BF16)

    return pl.pallas_call(
        body, name=name, grid=(T // tm,),
        in_specs=[pl.BlockSpec((tm, D), lambda i: (i, 0)), pl.BlockSpec((1, D), lambda i: (0, 0))],
        out_specs=pl.BlockSpec((tm, D), lambda i: (i, 0)), out_shape=jax.ShapeDtypeStruct((T, D), BF16),
        compiler_params=_cparams("parallel"))(h, w.reshape(1, D))


def _rmsnorm_bwd(h, w, du, dres, name, dep=None):
    T, D = h.shape
    tm = _pick_tile(T, 256)

    def body(h_ref, w_ref, du_ref, dr_ref, *rest):
        dh_ref, dhb_ref, dw_ref = rest[-3:]
        x = h_ref[...]
        r = lax.rsqrt(jnp.mean(x * x, axis=-1, keepdims=True) + NORM_EPS)
        xh = x * r
        du_ = du_ref[...]
        dy = du_ * w_ref[...]
        dh = dr_ref[...] + r * (dy - xh * jnp.mean(dy * xh, axis=-1, keepdims=True))
        dh_ref[...] = dh
        dhb_ref[...] = dh.astype(BF16)
        part = jnp.sum(du_ * xh, axis=0, keepdims=True)

        @pl.when(pl.program_id(0) == 0)
        def _():
            dw_ref[...] = part

        @pl.when(pl.program_id(0) > 0)
        def _():
            dw_ref[...] += part

    row = pl.BlockSpec((tm, D), lambda i: (i, 0))
    vec = pl.BlockSpec((1, D), lambda i: (0, 0))
    extra_specs, extra = ([], ()) if dep is None else ([pl.BlockSpec(memory_space=pl.ANY)], (dep,))
    dh, dhb, dw = pl.pallas_call(
        body, name=name, grid=(T // tm,), in_specs=[row, vec, row, row] + extra_specs, out_specs=[row, row, vec],
        out_shape=[jax.ShapeDtypeStruct((T, D), F32), jax.ShapeDtypeStruct((T, D), BF16), jax.ShapeDtypeStruct((1, D), F32)],
        compiler_params=_cparams("arbitrary"))(h, w.reshape(1, D), du, dres, *extra)
    return dh, dhb, dw.reshape(D)


def _loss_head(h, w, target):
    T, D = h.shape
    tm = _pick_tile(T, 256)

    def body(h_ref, w_ref, t_ref, loss_ref, dh_ref, dhb_ref, dw_ref):
        x = h_ref[...]
        wv = w_ref[...]
        r = lax.rsqrt(jnp.mean(x * x, axis=-1, keepdims=True) + NORM_EPS)
        xh = x * r
        e = xh * wv - t_ref[...]
        lpart = jnp.zeros((1, LANES), F32) + 0.5 * jnp.sum(jnp.mean(e * e, axis=-1, keepdims=True))
        dyo = e * (1.0 / D)
        dy = dyo * wv
        dh = r * (dy - xh * jnp.mean(dy * xh, axis=-1, keepdims=True))
        dh_ref[...] = dh
        dhb_ref[...] = dh.astype(BF16)
        part = jnp.sum(dyo * xh, axis=0, keepdims=True)

        @pl.when(pl.program_id(0) == 0)
        def _():
            dw_ref[...] = part
            loss_ref[...] = lpart

        @pl.when(pl.program_id(0) > 0)
        def _():
            dw_ref[...] += part
            loss_ref[...] += lpart

    row = pl.BlockSpec((tm, D), lambda i: (i, 0))
    vec = pl.BlockSpec((1, D), lambda i: (0, 0))
    lvec = pl.BlockSpec((1, LANES), lambda i: (0, 0))
    loss, dh, dhb, dw = pl.pallas_call(
        body, name="loss_head", grid=(T // tm,), in_specs=[row, vec, row], out_specs=[lvec, row, row, vec],
        out_shape=[jax.ShapeDtypeStruct((1, LANES), F32), jax.ShapeDtypeStruct((T, D), F32),
                   jax.ShapeDtypeStruct((T, D), BF16), jax.ShapeDtypeStruct((1, D), F32)],
        compiler_params=_cparams("arbitrary"))(h, w.reshape(1, D), target)
    return loss, dh, dhb, dw.reshape(D)


def _hgrn_gates(qr, fr, lb):
    sig = jax.nn.sigmoid(fr)
    nsig = jax.nn.sigmoid(-fr)
    fg = lb + (1.0 - lb) * sig
    logf = jnp.log(fg)
    k = (1.0 - lb) * nsig
    q = _silu(qr)
    return q, k, logf, sig, nsig, fg


def _hgrn_scaled(q, k, b, bmid):
    eq = jnp.exp(jnp.clip(b - bmid, -EXP_CLIP, EXP_CLIP))
    ek = jnp.exp(jnp.clip(bmid - b, -EXP_CLIP, EXP_CLIP))
    return q * eq, k * ek, eq, ek


def _hgrn_fwd(proj, lb, gnw, H):
    B, L, _ = proj.shape
    C, DK = HGRN_CHUNK, HGRN_DK
    F_ = H * DK
    NC = L // C

    nh = HGRN_HEADS_PER_STEP if H % HGRN_HEADS_PER_STEP == 0 else 1
    LB = min(L, HGRN_SEQ_BLOCK)
    ncb, nsb, WD = LB // C, L // LB, nh * DK

    def body(q_ref, f_ref, v_ref, g_ref, lb_ref, gn_ref, o_ref, on_ref, st_ref, ST, bsc):
        @pl.when(pl.program_id(2) == 0)
        def _():
            ST[...] = jnp.zeros_like(ST)

        gn = gn_ref[...]
        causal = _iota((C, C), 0) >= _iota((C, C), 1)

        def chunk(c, carry):
            r0 = pl.multiple_of(c * C, C)
            rows = pl.ds(r0, C)
            for hh in range(nh):
                ln = slice(hh * DK, (hh + 1) * DK)
                q, k, logf, _, _, _ = _hgrn_gates(q_ref[0, rows, ln], f_ref[0, rows, ln], lb_ref[:, ln])
                v = v_ref[0, rows, ln]
                b = _cumsum_rows(logf)
                bsc[hh] = b
                bmid = bsc[hh, C // 2 - 1:C // 2, :]
                blast = bsc[hh, C - 1:C, :]
                qs, ks, _, _ = _hgrn_scaled(q, k, b, bmid)
                A = jnp.where(causal, _dot(qs, ks, NT), 0.0)
                st = ST[hh]
                st_ref[0, hh, c] = st
                o = _dot(A, v) + _dot(q * jnp.exp(b), st, NT)
                kb = k * jnp.exp(blast - b)
                ST[hh] = st * jnp.exp(blast) + _dot(v, kb, TN)
                rms = lax.rsqrt(jnp.mean(o * o, axis=-1, keepdims=True) + NORM_EPS)
                o_ref[0, rows, ln] = o
                on_ref[0, rows, ln] = (o * rms * gn * _silu(g_ref[0, rows, ln])).astype(BF16)
            return carry

        lax.fori_loop(0, ncb, chunk, 0)

    def col(off):
        return pl.BlockSpec((1, LB, WD), lambda b, hp, s: (b, s, off // nh + hp))

    return pl.pallas_call(
        body, name="hgrn_fwd", grid=(B, H // nh, nsb),
        in_specs=[col(0), col(H), col(2 * H), col(3 * H), pl.BlockSpec((1, WD), lambda b, hp, s: (0, hp)),
                  pl.BlockSpec((1, DK), lambda b, hp, s: (0, 0))],
        out_specs=[col(0), col(0), pl.BlockSpec((1, nh, ncb, DK, DK), lambda b, hp, s: (b, hp, s, 0, 0))],
        out_shape=[jax.ShapeDtypeStruct((B, L, F_), F32), jax.ShapeDtypeStruct((B, L, F_), BF16),
                   jax.ShapeDtypeStruct((B, H, NC, DK, DK), F32)],
        scratch_shapes=[pltpu.VMEM((nh, DK, DK), F32), pltpu.VMEM((nh, C, DK), F32)],
        compiler_params=_cparams("parallel", "parallel", "arbitrary"))(proj, proj, proj, proj, lb, gnw)


def _hgrn_bwd(proj, o, don, st, lb, gnw, H):
    B, L, _ = proj.shape
    C, DK = HGRN_CHUNK, HGRN_DK
    F_ = H * DK
    NC = L // C

    nh = HGRN_HEADS_PER_STEP if H % HGRN_HEADS_PER_STEP == 0 else 1
    LB = min(L, HGRN_SEQ_BLOCK)
    ncb, nsb, WD = LB // C, L // LB, nh * DK

    def body(q_ref, f_ref, v_ref, g_ref, o_ref, do_ref, st_ref, lb_ref, gn_ref,
             dp_ref, dlb_ref, dgn_ref, DST, bsc):
        @pl.when(pl.program_id(2) == 0)
        def _():
            DST[...] = jnp.zeros_like(DST)
            dlb_ref[...] = jnp.zeros_like(dlb_ref)
            dgn_ref[...] = jnp.zeros_like(dgn_ref)

        gn = gn_ref[...]
        causal = _iota((C, C), 0) >= _iota((C, C), 1)
        lastrow = _iota((C, DK), 0) == C - 1

        def chunk(i, carry):
            c = ncb - 1 - i
            r0 = pl.multiple_of(c * C, C)
            rows = pl.ds(r0, C)
            for hh in range(nh):
                ln = slice(hh * DK, (hh + 1) * DK)
                lbv = lb_ref[:, ln]
                qr = q_ref[0, rows, ln]
                fr = f_ref[0, rows, ln]
                q, k, logf, sig, nsig, fg = _hgrn_gates(qr, fr, lbv)
                v = v_ref[0, rows, ln]
                b = _cumsum_rows(logf)
                bsc[hh] = b
                bmid = bsc[hh, C // 2 - 1:C // 2, :]
                blast = bsc[hh, C - 1:C, :]
                qs, ks, eq, ek = _hgrn_scaled(q, k, b, bmid)
                A = jnp.where(causal, _dot(qs, ks, NT), 0.0)
                st_in = st_ref[0, hh, c]
                dst = DST[hh]
                eb = jnp.exp(b)
                ebl = jnp.exp(blast)
                ekb = jnp.exp(blast - b)
                qb = q * eb
                kb = k * ekb
                ov = o_ref[0, rows, ln]
                gr = g_ref[0, rows, ln]
                rms = lax.rsqrt(jnp.mean(ov * ov, axis=-1, keepdims=True) + NORM_EPS)
                oh = ov * rms
                sg = _silu(gr)
                don_ = do_ref[0, rows, ln]
                dgn_ref[0, hh] += jnp.sum(don_ * oh * sg, axis=0, keepdims=True)
                dp_ref[3, 0, rows, ln] = (don_ * oh * gn * _dsilu(gr)).astype(BF16)
                doh = don_ * gn * sg
                do_ = rms * (doh - oh * jnp.mean(doh * oh, axis=-1, keepdims=True))
                dA = jnp.where(causal, _dot(do_, v, NT), 0.0)
                dp_ref[2, 0, rows, ln] = (_dot(A, do_, TN) + _dot(kb, dst, NT)).astype(BF16)
                dqb = _dot(do_, st_in)
                dkb = _dot(v, dst)
                dq = _dot(dA, ks) * eq + dqb * eb
                dk_inter = dkb * ekb
                dk = _dot(dA, qs, TN) * ek + dk_inter
                db = q * dq - k * dk
                extra = jnp.sum(k * dk_inter, axis=0, keepdims=True) + ebl * jnp.sum(st_in * dst, axis=0, keepdims=True)
                db = db + jnp.where(lastrow, extra, 0.0)
                dlogf = _cumsum_rows(db, reverse=True)
                DST[hh] = dst * ebl + _dot(do_, qb, TN)
                dp_ref[0, 0, rows, ln] = (dq * _dsilu(qr)).astype(BF16)
                ss = sig * nsig
                dp_ref[1, 0, rows, ln] = ((1.0 - lbv) * ss * (dlogf / fg - dk)).astype(BF16)
                dlb_ref[0, :, ln] += jnp.sum(dlogf * nsig / fg - dk * nsig, axis=0, keepdims=True)
            return carry

        lax.fori_loop(0, ncb, chunk, 0)

    def col(off):
        return pl.BlockSpec((1, LB, WD), lambda b, hp, s: (b, nsb - 1 - s, off // nh + hp))

    outs = pl.pallas_call(
        body, name="hgrn_bwd", grid=(B, H // nh, nsb),
        in_specs=[col(0), col(H), col(2 * H), col(3 * H), col(0), col(0),
                  pl.BlockSpec((1, nh, ncb, DK, DK), lambda b, hp, s: (b, hp, nsb - 1 - s, 0, 0)),
                  pl.BlockSpec((1, WD), lambda b, hp, s: (0, hp)), pl.BlockSpec((1, DK), lambda b, hp, s: (0, 0))],
        out_specs=[pl.BlockSpec((4, 1, LB, WD), lambda b, hp, s: (0, b, nsb - 1 - s, hp)),
                   pl.BlockSpec((1, 1, WD), lambda b, hp, s: (b, 0, hp)),
                   pl.BlockSpec((1, nh, 1, DK), lambda b, hp, s: (b, hp, 0, 0))],
        out_shape=[jax.ShapeDtypeStruct((4, B, L, F_), BF16), jax.ShapeDtypeStruct((B, 1, F_), F32),
                   jax.ShapeDtypeStruct((B, H, 1, DK), F32)],
        scratch_shapes=[pltpu.VMEM((nh, DK, DK), F32), pltpu.VMEM((nh, C, DK), F32)],
        compiler_params=_cparams("parallel", "parallel", "arbitrary"))(proj, proj, proj, proj, o, don, st, lb, gnw)
    return outs


CONV_ROWS = 256
PAD_ROWS = 8


def _conv_taps(pad_ref, w_ref, r0, K, rb, forward=True):
    ext = pad_ref[pl.ds(r0, rb + PAD_ROWS), :]
    n = rb + PAD_ROWS
    acc = None
    for s in range(K):
        if forward:
            sh = ext if s == 0 else pltpu.roll(ext, s, 0)
            term = sh[PAD_ROWS:, :]
        else:
            sh = ext if s == 0 else pltpu.roll(ext, n - s, 0)
            term = sh[:rb, :]
        term = term * w_ref[K - 1 - s:K - s, :]
        acc = term if acc is None else acc + term
    return acc


def _conv_dw(ext, dc, K):
    row = _iota((8, dc.shape[1]), 0)
    out = jnp.zeros((8, dc.shape[1]), F32)
    for kk in range(K):
        s = K - 1 - kk
        sh = ext if s == 0 else pltpu.roll(ext, s, 0)
        out = out + jnp.where(row == kk, jnp.sum(dc * sh[PAD_ROWS:, :], axis=0, keepdims=True), 0.0)
    return out


def _mconv_fwd(zx, cw, cb, col0, width):
    B, L, _ = zx.shape
    K = cw.shape[0]
    ct = _pick_tile(width, 256)
    rb = min(CONV_ROWS, L)
    nrb = L // rb
    off = col0 // ct

    def body(x_ref, w_ref, b_ref, y_ref, xp):
        xp[0:PAD_ROWS, :] = jnp.zeros((PAD_ROWS, ct), F32)
        xp[PAD_ROWS:, :] = x_ref[0]
        bias = b_ref[...]

        def blk(i, carry):
            r0 = pl.multiple_of(i * rb, rb)
            y_ref[0, pl.ds(r0, rb), :] = _silu(_conv_taps(xp, w_ref, r0, K, rb) + bias)
            return carry

        lax.fori_loop(0, nrb, blk, 0)

    return pl.pallas_call(
        body, name="mconv_fwd", grid=(B, width // ct),
        in_specs=[pl.BlockSpec((1, L, ct), lambda b, j: (b, 0, off + j)), pl.BlockSpec((K, ct), lambda b, j: (0, j)),
                  pl.BlockSpec((1, ct), lambda b, j: (0, j))],
        out_specs=pl.BlockSpec((1, L, ct), lambda b, j: (b, 0, j)),
        out_shape=jax.ShapeDtypeStruct((B, L, width), F32),
        scratch_shapes=[pltpu.VMEM((L + PAD_ROWS, ct), F32)],
        compiler_params=_cparams("parallel", "parallel"))(zx, cw, cb.reshape(1, width))


def _mconv_bwd(zx, dya, cw, cb, col0, wcol0, name):
    B, L, _ = zx.shape
    K = cw.shape[0]
    npart, _, _, wq = dya.shape
    width = npart * wq
    ct = _pick_tile(wq, 256)
    rb = min(CONV_ROWS, L)
    nrb = L // rb
    off = (col0 + wcol0) // ct
    woff = wcol0 // ct
    pq = wq // ct

    def body(x_ref, dy_ref, w_ref, b_ref, dx_ref, dw_ref, db_ref, xp, dcp):
        xp[0:PAD_ROWS, :] = jnp.zeros((PAD_ROWS, ct), F32)
        xp[PAD_ROWS:, :] = x_ref[0]
        dcp[L:, :] = jnp.zeros((PAD_ROWS, ct), F32)
        bias = b_ref[...]

        def blk1(i, carry):
            dw, db = carry
            r0 = pl.multiple_of(i * rb, rb)
            cpre = _conv_taps(xp, w_ref, r0, K, rb) + bias
            dc = dy_ref[0, 0, pl.ds(r0, rb), :] * _dsilu(cpre)
            dcp[pl.ds(r0, rb), :] = dc
            ext = xp[pl.ds(r0, rb + PAD_ROWS), :]
            return dw + _conv_dw(ext, dc, K), db + jnp.sum(dc, axis=0, keepdims=True)

        dw, db = lax.fori_loop(0, nrb, blk1, (jnp.zeros((8, ct), F32), jnp.zeros((1, ct), F32)))
        dw_ref[0] = dw
        db_ref[0] = db

        def blk2(i, carry):
            r0 = pl.multiple_of(i * rb, rb)
            dx_ref[0, pl.ds(r0, rb), :] = _conv_taps(dcp, w_ref, r0, K, rb, forward=False).astype(BF16)
            return carry

        lax.fori_loop(0, nrb, blk2, 0)

    dx, dw, db = pl.pallas_call(
        body, name=name, grid=(B, width // ct),
        in_specs=[pl.BlockSpec((1, L, ct), lambda b, j: (b, 0, off + j)),
                  pl.BlockSpec((1, 1, L, ct), lambda b, j: (j // pq, b, 0, j % pq)),
                  pl.BlockSpec((K, ct), lambda b, j: (0, woff + j)), pl.BlockSpec((1, ct), lambda b, j: (0, woff + j))],
        out_specs=[pl.BlockSpec((1, L, ct), lambda b, j: (b, 0, j)), pl.BlockSpec((1, 8, ct), lambda b, j: (b, 0, j)),
                   pl.BlockSpec((1, 1, ct), lambda b, j: (b, 0, j))],
        out_shape=[jax.ShapeDtypeStruct((B, L, width), BF16), jax.ShapeDtypeStruct((B, 8, width), F32),
                   jax.ShapeDtypeStruct((B, 1, width), F32)],
        scratch_shapes=[pltpu.VMEM((L + PAD_ROWS, ct), F32), pltpu.VMEM((L + PAD_ROWS, ct), F32)],
        compiler_params=_cparams("parallel", "parallel"))(zx, dya, cw, cb.reshape(1, -1))
    return dx, dw[:, :K, :], db


def _ffn_mid_fwd(up, cw, cb, dff):
    B, L, _ = up.shape
    K = cw.shape[0]
    ct = _pick_tile(dff, 256)
    rb = min(CONV_ROWS, L)
    nrb = L // rb
    half = dff // ct

    def body(g_ref, u_ref, wg_ref, wu_ref, bg_ref, bu_ref, a_ref, gp, upad):
        gp[0:PAD_ROWS, :] = jnp.zeros((PAD_ROWS, ct), F32)
        upad[0:PAD_ROWS, :] = jnp.zeros((PAD_ROWS, ct), F32)
        gp[PAD_ROWS:, :] = g_ref[0]
        upad[PAD_ROWS:, :] = u_ref[0]
        bg, bu = bg_ref[...], bu_ref[...]

        def blk(i, carry):
            r0 = pl.multiple_of(i * rb, rb)
            cg = _conv_taps(gp, wg_ref, r0, K, rb) + bg
            cu = _conv_taps(upad, wu_ref, r0, K, rb) + bu
            a_ref[0, pl.ds(r0, rb), :] = (_silu(cg) * cu).astype(BF16)
            return carry

        lax.fori_loop(0, nrb, blk, 0)

    xg = pl.BlockSpec((1, L, ct), lambda b, j: (b, 0, j))
    xu = pl.BlockSpec((1, L, ct), lambda b, j: (b, 0, half + j))
    wgs = pl.BlockSpec((K, ct), lambda b, j: (0, j))
    wus = pl.BlockSpec((K, ct), lambda b, j: (0, half + j))
    bgs = pl.BlockSpec((1, ct), lambda b, j: (0, j))
    bus = pl.BlockSpec((1, ct), lambda b, j: (0, half + j))
    cb2 = cb.reshape(1, 2 * dff)
    return pl.pallas_call(
        body, name="ffn_mid_fwd", grid=(B, half), in_specs=[xg, xu, wgs, wus, bgs, bus], out_specs=xg,
        out_shape=jax.ShapeDtypeStruct((B, L, dff), BF16),
        scratch_shapes=[pltpu.VMEM((L + PAD_ROWS, ct), F32), pltpu.VMEM((L + PAD_ROWS, ct), F32)],
        compiler_params=_cparams("parallel", "parallel"))(up, up, cw, cw, cb2, cb2)


def _ffn_mid_bwd(up, dact, cw, cb, dff):
    B, L, _ = up.shape
    K = cw.shape[0]
    ct = _pick_tile(dff, 256)
    rb = min(CONV_ROWS, L)
    nrb = L // rb
    half = dff // ct

    def body(g_ref, u_ref, da_ref, wg_ref, wu_ref, bg_ref, bu_ref, dx_ref, dwg_ref, dwu_ref, dbg_ref, dbu_ref,
             gp, upad, dgp, dup):
        gp[0:PAD_ROWS, :] = jnp.zeros((PAD_ROWS, ct), F32)
        upad[0:PAD_ROWS, :] = jnp.zeros((PAD_ROWS, ct), F32)
        gp[PAD_ROWS:, :] = g_ref[0]
        upad[PAD_ROWS:, :] = u_ref[0]
        dgp[L:, :] = jnp.zeros((PAD_ROWS, ct), F32)
        dup[L:, :] = jnp.zeros((PAD_ROWS, ct), F32)
        bg, bu = bg_ref[...], bu_ref[...]

        def blk1(i, carry):
            dwg, dwu, dbg, dbu = carry
            r0 = pl.multiple_of(i * rb, rb)
            cg = _conv_taps(gp, wg_ref, r0, K, rb) + bg
            cu = _conv_taps(upad, wu_ref, r0, K, rb) + bu
            da = da_ref[0, pl.ds(r0, rb), :]
            dcg = da * cu * _dsilu(cg)
            dcu = da * _silu(cg)
            dgp[pl.ds(r0, rb), :] = dcg
            dup[pl.ds(r0, rb), :] = dcu
            eg = gp[pl.ds(r0, rb + PAD_ROWS), :]
            eu = upad[pl.ds(r0, rb + PAD_ROWS), :]
            return (dwg + _conv_dw(eg, dcg, K), dwu + _conv_dw(eu, dcu, K), dbg + jnp.sum(dcg, axis=0, keepdims=True),
                    dbu + jnp.sum(dcu, axis=0, keepdims=True))

        z8 = jnp.zeros((8, ct), F32)
        z1 = jnp.zeros((1, ct), F32)
        dwg, dwu, dbg, dbu = lax.fori_loop(0, nrb, blk1, (z8, z8, z1, z1))
        dwg_ref[0] = dwg
        dwu_ref[0] = dwu
        dbg_ref[0] = dbg
        dbu_ref[0] = dbu

        def blk2(i, carry):
            r0 = pl.multiple_of(i * rb, rb)
            dx_ref[0, 0, pl.ds(r0, rb), :] = _conv_taps(dgp, wg_ref, r0, K, rb, forward=False).astype(BF16)
            dx_ref[1, 0, pl.ds(r0, rb), :] = _conv_taps(dup, wu_ref, r0, K, rb, forward=False).astype(BF16)
            return carry

        lax.fori_loop(0, nrb, blk2, 0)

    xg = pl.BlockSpec((1, L, ct), lambda b, j: (b, 0, j))
    xu = pl.BlockSpec((1, L, ct), lambda b, j: (b, 0, half + j))
    wgs = pl.BlockSpec((K, ct), lambda b, j: (0, j))
    wus = pl.BlockSpec((K, ct), lambda b, j: (0, half + j))
    bgs = pl.BlockSpec((1, ct), lambda b, j: (0, j))
    bus = pl.BlockSpec((1, ct), lambda b, j: (0, half + j))
    w8 = pl.BlockSpec((1, 8, ct), lambda b, j: (b, 0, j))
    b1 = pl.BlockSpec((1, 1, ct), lambda b, j: (b, 0, j))
    cb2 = cb.reshape(1, 2 * dff)
    pad = pltpu.VMEM((L + PAD_ROWS, ct), F32)
    dx2, dwg, dwu, dbg, dbu = pl.pallas_call(
        body, name="ffn_mid_bwd", grid=(B, half), in_specs=[xg, xu, xg, wgs, wus, bgs, bus],
        out_specs=[pl.BlockSpec((2, 1, L, ct), lambda b, j: (0, b, 0, j)), w8, w8, b1, b1],
        out_shape=[jax.ShapeDtypeStruct((2, B, L, dff), BF16)] + [jax.ShapeDtypeStruct((B, 8, dff), F32)] * 2
        + [jax.ShapeDtypeStruct((B, 1, dff), F32)] * 2,
        scratch_shapes=[pad, pad, pad, pad],
        compiler_params=_cparams("parallel", "parallel"))(up, up, dact, cw, cw, cb2, cb2)
    dw = jnp.concatenate([dwg[:, :K], dwu[:, :K]], axis=-1)
    db = jnp.concatenate([dbg, dbu], axis=-1)
    return dx2, dw, db


def _ssd_consts(hpg, W):
    P = M_HEADDIM
    E = (_iota((LANES, W), 0) == _iota((LANES, W), 1) // P).astype(BF16)
    Ebig = (_iota((LANES, hpg * LANES), 0) == _iota((LANES, hpg * LANES), 1) // LANES).astype(BF16)
    causal = _iota((M_CHUNK, M_CHUNK), 0) >= _iota((M_CHUNK, M_CHUNK), 1)
    head_of_lane = _iota((1, W), 1) // P
    return E, Ebig, causal, head_of_lane


def _ssd_chunk_fwd(xs, Bm, Cm, dtr, bias, Aneg, E, Ebig, causal, head_of_lane, hpg, st, ar_sc, ae_sc):
    pre = dtr + bias
    dt = jnp.maximum(pre, 0.0) + jnp.log(1.0 + jnp.exp(-jnp.abs(pre)))
    Ad = dt * Aneg
    a_c = _cumsum_rows(Ad)
    ar_sc[...] = a_c.T
    aexp = _dot_exact(a_c, E)
    ae_sc[...] = aexp
    alast = ae_sc[M_CHUNK - 1:M_CHUNK, :]
    dtexp = _dot_exact(dt, E)
    X = xs * dtexp
    AC = _dot_exact(a_c, Ebig)
    CB = _dot(Cm, Bm, NT)
    Xb = X.astype(BF16)
    Ls = [jnp.where(causal, jnp.exp(jnp.minimum(AC[:, j * LANES:(j + 1) * LANES] - ar_sc[j:j + 1, :], 0.0)), 0.0)
          for j in range(hpg)]
    first = _iota((1, LANES), 1) < M_HEADDIM
    pairs = []
    for p in range(hpg // 2):
        Xp = Xb[:, p * LANES:(p + 1) * LANES]
        pairs.append(jnp.where(first, _dot(CB * Ls[2 * p], Xp), _dot(CB * Ls[2 * p + 1], Xp)))
    ydiag = pairs[0] if len(pairs) == 1 else jnp.concatenate(pairs, axis=1)
    ea = jnp.exp(aexp)
    yoff = ea * _dot(Cm, st)
    dec = jnp.exp(alast - aexp)
    return dict(dt=dt, a_c=a_c, aexp=aexp, alast=alast, dtexp=dtexp, X=X, Xb=Xb, CB=CB, Ls=Ls, ydiag=ydiag, ea=ea,
                yoff=yoff, dec=dec)


def _ssd_fwd(xbca, zx, dtc, bias, Aneg, Dexp, nw, hpg):
    B, L, _ = xbca.shape
    G, N, C = M_GROUPS, M_D_STATE, M_CHUNK
    W = hpg * M_HEADDIM
    DI = G * W
    NC = L // C
    LB = min(L, 4 * C)
    ncb = LB // C

    def body(xs_ref, b_ref, c_ref, z_ref, dt_ref, bias_ref, a_ref, d_ref, nw_ref, y_ref, yn_ref, st_ref, ST, ar_sc, ae_sc):
        @pl.when(pl.program_id(2) == 0)
        def _():
            ST[...] = jnp.zeros_like(ST)

        E, Ebig, causal, head_of_lane = _ssd_consts(hpg, W)
        bias_ = bias_ref[0]
        Aneg_ = a_ref[0]
        Dv = d_ref[...]
        nwv = nw_ref[...]

        def chunk(ci, carry):
            r0 = pl.multiple_of(ci * C, C)
            rows = pl.ds(r0, C)
            xs = xs_ref[0, rows, :]
            Bm = b_ref[0, rows, :]
            Cm = c_ref[0, rows, :]
            st = ST[...]
            st_ref[0, 0, ci] = st
            f = _ssd_chunk_fwd(xs, Bm, Cm, dt_ref[0, 0, ci], bias_, Aneg_, E, Ebig, causal, head_of_lane, hpg, st, ar_sc, ae_sc)
            y = f["ydiag"] + f["yoff"] + xs * Dv
            ST[...] = st * jnp.exp(f["alast"]) + _dot(Bm, f["X"] * f["dec"], TN)
            yg = y * _silu(z_ref[0, rows, :])
            rstd = lax.rsqrt(jnp.mean(yg * yg, axis=-1, keepdims=True) + NORM_EPS)
            y_ref[0, rows, :] = y
            yn_ref[0, rows, :] = (yg * rstd * nwv).astype(BF16)
            return carry

        lax.fori_loop(0, ncb, chunk, 0)

    xw = pl.BlockSpec((1, LB, W), lambda b, g, s: (b, s, g))
    bsp = pl.BlockSpec((1, LB, N), lambda b, g, s: (b, s, DI // N + g))
    csp = pl.BlockSpec((1, LB, N), lambda b, g, s: (b, s, DI // N + G + g))
    dts = pl.BlockSpec((1, 1, ncb, C, LANES), lambda b, g, s: (b, g, s, 0, 0))
    hv = pl.BlockSpec((1, 1, LANES), lambda b, g, s: (g, 0, 0))
    wv = pl.BlockSpec((1, W), lambda b, g, s: (0, g))
    sts = pl.BlockSpec((1, 1, ncb, N, W), lambda b, g, s: (b, g, s, 0, 0))
    return pl.pallas_call(
        body, name="ssd_fwd", grid=(B, G, L // LB), in_specs=[xw, bsp, csp, xw, dts, hv, hv, wv, wv],
        out_specs=[xw, xw, sts],
        out_shape=[jax.ShapeDtypeStruct((B, L, DI), F32), jax.ShapeDtypeStruct((B, L, DI), BF16),
                   jax.ShapeDtypeStruct((B, G, NC, N, W), F32)],
        scratch_shapes=[pltpu.VMEM((N, W), F32), pltpu.VMEM((LANES, C), F32), pltpu.VMEM((C, W), F32)],
        compiler_params=_cparams("parallel", "parallel", "arbitrary"))(xbca, xbca, xbca, zx, dtc, bias, Aneg, Dexp, nw)


def _ssd_bwd(xbca, zx, dtc, ypre, dyn, st, bias, Aneg, Dexp, nw, hpg):
    B, L, _ = xbca.shape
    G, N, C = M_GROUPS, M_D_STATE, M_CHUNK
    W = hpg * M_HEADDIM
    DI = G * W
    NC = L // C
    LB = min(L, 4 * C)
    ncb = LB // C
    nsb = L // LB

    def body(xs_ref, b_ref, c_ref, z_ref, dt_ref, y_ref, dyn_ref, st_ref, bias_ref, a_ref, d_ref, nw_ref,
             dxs_ref, dbc_ref, dz_ref, ddt_ref, dnw_ref, dd_ref, da_ref, dbias_ref, DST, ar_sc, ae_sc):
        @pl.when(pl.program_id(2) == 0)
        def _():
            DST[...] = jnp.zeros_like(DST)
            dnw_ref[...] = jnp.zeros_like(dnw_ref)
            dd_ref[...] = jnp.zeros_like(dd_ref)
            da_ref[...] = jnp.zeros_like(da_ref)
            dbias_ref[...] = jnp.zeros_like(dbias_ref)

        E, Ebig, causal, head_of_lane = _ssd_consts(hpg, W)
        bias_ = bias_ref[0]
        Aneg_ = a_ref[0]
        Dv = d_ref[...]
        nwv = nw_ref[...]
        lane = _iota((1, LANES), 1)
        subl = _iota((LANES, 1), 0)
        lastrow = _iota((C, W), 0) == C - 1

        def chunk(i, carry):
            ci = ncb - 1 - i
            r0 = pl.multiple_of(ci * C, C)
            rows = pl.ds(r0, C)
            xs = xs_ref[0, rows, :]
            Bm = b_ref[0, rows, :]
            Cm = c_ref[0, rows, :]
            zr = z_ref[0, rows, :]
            dtr = dt_ref[0, 0, ci]
            st_in = st_ref[0, 0, ci]
            dst = DST[...]
            f = _ssd_chunk_fwd(xs, Bm, Cm, dtr, bias_, Aneg_, E, Ebig, causal, head_of_lane, hpg, st_in, ar_sc, ae_sc)
            X, Xb, dec, ea, CB = f["X"], f["Xb"], f["dec"], f["ea"], f["CB"]
            y = y_ref[0, rows, :]
            sz = _silu(zr)
            yg = y * sz
            rstd = lax.rsqrt(jnp.mean(yg * yg, axis=-1, keepdims=True) + NORM_EPS)
            yh = yg * rstd
            dyn_ = dyn_ref[0, rows, :]
            dnw_ref[0, 0] += jnp.sum(dyn_ * yh, axis=0, keepdims=True)
            dyh = dyn_ * nwv
            dyg = rstd * (dyh - yh * jnp.mean(dyh * yh, axis=-1, keepdims=True))
            dz_ref[0, rows, :] = (dyg * y * _dsilu(zr)).astype(BF16)
            dy = dyg * sz
            dd_ref[0, 0] += jnp.sum(dy * xs, axis=0, keepdims=True)
            dxs = dy * Dv
            dYo = dy * ea
            daexp = dy * f["yoff"]
            dCm = _dot(dYo, st_in, NT)
            dst_in = _dot(Cm, dYo, TN)
            dyb = dy.astype(BF16)
            dCB = jnp.zeros((C, C), F32)
            da_col = jnp.zeros((C, LANES), F32)
            da_row = jnp.zeros((LANES, C), F32)
            first = lane < M_HEADDIM
            dXs = []
            for p in range(hpg // 2):
                Xp = Xb[:, p * LANES:(p + 1) * LANES]
                dYp = dyb[:, p * LANES:(p + 1) * LANES]
                dXp = None
                for j in (2 * p, 2 * p + 1):
                    Lj = f["Ls"][j]
                    Gj = CB * Lj
                    dYj = jnp.where(first if j % 2 == 0 else jnp.logical_not(first), dYp, jnp.zeros_like(dYp))
                    t = _dot(Gj, dYj, TN)
                    dXp = t if dXp is None else dXp + t
                    dGj = _dot(dYj, Xp, NT)
                    dCB = dCB + dGj * Lj
                    Wj = dGj * Gj
                    da_col = da_col + jnp.sum(Wj, axis=1, keepdims=True) * (lane == j).astype(F32)
                    da_row = da_row + (subl == j).astype(F32) * jnp.sum(Wj, axis=0, keepdims=True)
                dXs.append(dXp)
            dX = dXs[0] if len(dXs) == 1 else jnp.concatenate(dXs, axis=1)
            dCm = dCm + _dot(dCB, Bm)
            dBm = _dot(dCB, Cm, TN)
            ela = jnp.exp(f["alast"])
            dalast = jnp.sum(dst * st_in, axis=0, keepdims=True) * ela
            DST[...] = dst * ela + dst_in
            dXd = _dot(Bm, dst)
            dBm = dBm + _dot(X * dec, dst, NT)
            dX = dX + dXd * dec
            ddec = dXd * X * dec
            dalast = dalast + jnp.sum(ddec, axis=0, keepdims=True)
            daexp = daexp - ddec + jnp.where(lastrow, dalast, 0.0)
            dxs = dxs + dX * f["dtexp"]
            ddtexp = dX * xs
            ddt = _dot_exact(ddtexp, E, NT, passes=2)
            da_c = _dot_exact(daexp, E, NT, passes=2) + da_col - da_row.T
            dAd = _cumsum_rows(da_c, reverse=True)
            ddt = ddt + dAd * Aneg_
            da_ref[0, 0] += jnp.sum(dAd * f["dt"], axis=0, keepdims=True) * Aneg_
            ddtr = ddt * jax.nn.sigmoid(dtr + bias_)
            dbias_ref[0, 0] += jnp.sum(ddtr, axis=0, keepdims=True)
            ddt_ref[0, 0, ci] = ddtr
            dxs_ref[0, rows, :] = dxs
            dbc_ref[0, 0, rows, :] = dBm
            dbc_ref[1, 0, rows, :] = dCm
            return carry

        lax.fori_loop(0, ncb, chunk, 0)

    def rev(s):
        return nsb - 1 - s

    xw = pl.BlockSpec((1, LB, W), lambda b, g, s: (b, rev(s), g))
    bsp = pl.BlockSpec((1, LB, N), lambda b, g, s: (b, rev(s), DI // N + g))
    csp = pl.BlockSpec((1, LB, N), lambda b, g, s: (b, rev(s), DI // N + G + g))
    gsp = pl.BlockSpec((1, LB, N), lambda b, g, s: (b, rev(s), g))
    dts = pl.BlockSpec((1, 1, ncb, C, LANES), lambda b, g, s: (b, g, rev(s), 0, 0))
    hv = pl.BlockSpec((1, 1, LANES), lambda b, g, s: (g, 0, 0))
    wv = pl.BlockSpec((1, W), lambda b, g, s: (0, g))
    sts = pl.BlockSpec((1, 1, ncb, N, W), lambda b, g, s: (b, g, rev(s), 0, 0))
    accw = pl.BlockSpec((1, 1, 1, W), lambda b, g, s: (b, g, 0, 0))
    acch = pl.BlockSpec((1, 1, 1, LANES), lambda b, g, s: (b, g, 0, 0))
    return pl.pallas_call(
        body, name="ssd_bwd", grid=(B, G, nsb), in_specs=[xw, bsp, csp, xw, dts, xw, xw, sts, hv, hv, wv, wv],
        out_specs=[xw, pl.BlockSpec((2, 1, LB, N), lambda b, g, s: (0, b, rev(s), g)), xw, dts, accw, accw, acch, acch],
        out_shape=[jax.ShapeDtypeStruct((B, L, DI), F32), jax.ShapeDtypeStruct((2, B, L, G * N), F32),
                   jax.ShapeDtypeStruct((B, L, DI), BF16),
                   jax.ShapeDtypeStruct((B, G, NC, C, LANES), F32), jax.ShapeDtypeStruct((B, G, 1, W), F32),
                   jax.ShapeDtypeStruct((B, G, 1, W), F32), jax.ShapeDtypeStruct((B, G, 1, LANES), F32),
                   jax.ShapeDtypeStruct((B, G, 1, LANES), F32)],
        scratch_shapes=[pltpu.VMEM((N, W), F32), pltpu.VMEM((LANES, C), F32), pltpu.VMEM((C, W), F32)],
        compiler_params=_cparams("parallel", "parallel", "arbitrary"))(
            xbca, xbca, xbca, zx, dtc, ypre, dyn, st, bias, Aneg, Dexp, nw)


def _adamw(w, g, m, v, name):
    shape = w.shape
    n = w.size
    cols = shape[-1]
    rows = n // cols
    tr = rows
    for cand in (512, 256, 128, 64, 32, 16, 8):
        if rows % cand == 0 and cand * cols * 4 <= 1024 * 1024:
            tr = cand
            break
    c1 = 1.0 / (1.0 - ADAM_B1 ** ADAM_STEP)
    c2 = 1.0 / (1.0 - ADAM_B2 ** ADAM_STEP)

    def body(w_ref, g_ref, m_ref, v_ref, d_ref, mo_ref, vo_ref):
        g_ = g_ref[...]
        mn = ADAM_B1 * m_ref[...] + (1.0 - ADAM_B1) * g_
        vn = ADAM_B2 * v_ref[...] + (1.0 - ADAM_B2) * (g_ * g_)
        d_ref[...] = -ADAM_LR * ((mn * c1) / (jnp.sqrt(vn * c2) + ADAM_EPS) + ADAM_WD * w_ref[...])
        mo_ref[...] = mn
        vo_ref[...] = vn

    spec = pl.BlockSpec((tr, cols), lambda i: (i, 0))
    r2 = lambda a: a.reshape(rows, cols)
    outs = pl.pallas_call(
        body, name=name, grid=(rows // tr,), in_specs=[spec] * 4, out_specs=[spec] * 3,
        out_shape=[jax.ShapeDtypeStruct((rows, cols), F32)] * 3,
        compiler_params=_cparams("parallel"))(r2(w), r2(g), r2(m), r2(v))
    return tuple(o.reshape(shape) for o in outs)


def _lower_bounds(lb_logits):
    p = jax.nn.softmax(lb_logits.astype(F32), axis=0)
    return jnp.cumsum(p, axis=0) - p[0]


def _pad_cols(a, n):
    return a if a.shape[-1] == n else jnp.pad(a, [(0, 0)] * (a.ndim - 1) + [(0, n - a.shape[-1])])


def _heads_to_lanes(a, G, hpg):
    return _pad_cols(a.reshape(G, 1, hpg), LANES)


def _local_step(x, target, P, fetch, emit):
    B, L, D = x.shape
    T = B * L
    depth = P["mix_norm"].shape[0]
    H = D // HGRN_DK
    F_ = H * HGRN_DK
    DI = P["m_norm"].shape[1]
    G, N = M_GROUPS, M_D_STATE
    MH = DI // M_HEADDIM
    hpg = MH // G
    assert hpg <= 8
    W = hpg * M_HEADDIM
    CD = DI + 2 * G * N
    MIN = DI + CD + MH
    MPAD = -(-MIN // LANES) * LANES
    dff = P["f_conv_b"].shape[1] // 2
    NC = L // M_CHUNK
    lbs = _lower_bounds(P["hgrn_lb_logits"])

    h = x.reshape(T, D)
    saved = []
    for i in range(depth):
        j = i // 2
        Wl = dict(fetch(i, ("mix_in", "mix_out"), h))
        s = {"h_in": h, "W": Wl}
        u = _rmsnorm_fwd(h, P["mix_norm"][i], "mix_norm_fwd")
        s["u"] = u
        if i % 2 == 0:
            proj = _matmul(u, Wl["mix_in"], name="hgrn_in_fwd").reshape(B, L, 4 * F_)
            o, on, st = _hgrn_fwd(proj, lbs[j].reshape(1, F_), P["hgrn_gnorm"][j].reshape(1, HGRN_DK), H)
            h = _matmul(on.reshape(T, F_), Wl["mix_out"], res=h, name="hgrn_out_fwd")
            s.update(proj=proj, o=o, on=on, st=st)
        else:
            zx = _matmul(u, Wl["mix_in"], tb=True, tn=1152, name="m_in_fwd").reshape(B, L, MPAD)
            xbca = _mconv_fwd(zx, P["m_conv_w"][j], P["m_conv_b"][j], DI, CD)
            dtr = zx[:, :, DI + CD:DI + CD + MH].reshape(B, NC, M_CHUNK, G, hpg).transpose(0, 3, 1, 2, 4)
            dtc = _pad_cols(dtr, LANES)
            bias = _heads_to_lanes(P["m_dt_bias"][j], G, hpg)
            Aneg = _heads_to_lanes(-jnp.exp(P["m_A_log"][j]), G, hpg)
            Dexp = jnp.repeat(P["m_D"][j], M_HEADDIM).reshape(1, DI)
            nw = P["m_norm"][j].reshape(1, DI)
            ypre, yn, st = _ssd_fwd(xbca, zx, dtc, bias, Aneg, Dexp, nw, hpg)
            h = _matmul(yn.reshape(T, DI), Wl["mix_out"], res=h, name="m_out_fwd")
            s.update(zx=zx, xbca=xbca, dtc=dtc, bias=bias, Aneg=Aneg, Dexp=Dexp, nw=nw, ypre=ypre, yn=yn, st=st)
        s["h_mid"] = h
        u2 = _rmsnorm_fwd(h, P["ffn_norm"][i], "ffn_norm_fwd")
        Wl.update(fetch(i, ("f_w_up", "f_w_down"), h))
        up = _matmul(u2, Wl["f_w_up"], name="ffn_up_fwd").reshape(B, L, 2 * dff)
        act = _ffn_mid_fwd(up, P["f_conv_w"][i], P["f_conv_b"][i], dff)
        h = _matmul(act.reshape(T, dff), Wl["f_w_down"], res=h, name="ffn_down_fwd")
        s.update(u2=u2, up=up, act=act)
        saved.append(s)

    loss, dh, dhb, d_final = _loss_head(h, P["final_norm"], target.reshape(T, D))

    g = {k: [None] * P[k].shape[0] for k in ("mix_norm", "ffn_norm", "hgrn_gnorm", "m_conv_w", "m_conv_b", "m_dt_bias",
                                              "m_A_log", "m_D", "m_norm", "f_conv_w", "f_conv_b")}
    dlbs = [None] * lbs.shape[0]
    for i in reversed(range(depth)):
        j = i // 2
        s = saved[i]
        Wl = s["W"]
        gm = {}

        def dw(key, a, b, name, **kw):
            gm[key] = _matmul(a, b, ta=True, out_dtype=BF16, name=name, **kw)

        dact = _matmul(dhb, Wl["f_w_down"], tb=True, name="ffn_down_dx").reshape(B, L, dff)
        dw("f_w_down", s["act"].reshape(T, dff), dhb, "ffn_down_dw")
        dup, dcw, dcb = _ffn_mid_bwd(s["up"], dact, P["f_conv_w"][i], P["f_conv_b"][i], dff)
        g["f_conv_w"][i] = jnp.sum(dcw, axis=0)
        g["f_conv_b"][i] = jnp.sum(dcb, axis=(0, 1))
        dup = dup.reshape(2, T, dff)
        dw("f_w_up", s["u2"], dup, "ffn_up_dw", b_parts=True)
        du2 = _matmul(dup, Wl["f_w_up"], a_parts=True, tb=True, name="ffn_up_dx")
        dh, dhb, g["ffn_norm"][i] = _rmsnorm_bwd(s["h_mid"], P["ffn_norm"][i], du2, dh, "ffn_norm_bwd")
        if i % 2 == 0:
            don = _matmul(dhb, Wl["mix_out"], tb=True, name="hgrn_out_dx").reshape(B, L, F_)
            dw("mix_out", s["on"].reshape(T, F_), dhb, "hgrn_out_dw")
            dproj, dlb, dgn = _hgrn_bwd(s["proj"], s["o"], don, s["st"], lbs[j].reshape(1, F_),
                                        P["hgrn_gnorm"][j].reshape(1, HGRN_DK), H)
            dlbs[j] = jnp.sum(dlb, axis=(0, 1))
            g["hgrn_gnorm"][j] = jnp.sum(dgn, axis=(0, 1, 2))
            dproj = dproj.reshape(4, T, F_)
            dw("mix_in", s["u"], dproj, "hgrn_in_dw", b_parts=True)
            du = _matmul(dproj, Wl["mix_in"], a_parts=True, tb=True, name="hgrn_in_dx")
        else:
            dyn = _matmul(dhb, Wl["mix_out"], tb=True, name="m_out_dx").reshape(B, L, DI)
            dw("mix_out", s["yn"].reshape(T, DI), dhb, "m_out_dw")
            dxs, dbc, dz, ddt, dnw, dD, dA, dbias = _ssd_bwd(s["xbca"], s["zx"], s["dtc"], s["ypre"], dyn, s["st"],
                                                             s["bias"], s["Aneg"], s["Dexp"], s["nw"], hpg)
            g["m_norm"][j] = jnp.sum(dnw, axis=(0, 2)).reshape(DI)
            g["m_D"][j] = jnp.sum(dD, axis=(0, 2)).reshape(MH, M_HEADDIM).sum(axis=-1)
            g["m_A_log"][j] = jnp.sum(dA, axis=(0, 2))[:, :hpg].reshape(MH)
            g["m_dt_bias"][j] = jnp.sum(dbias, axis=(0, 2))[:, :hpg].reshape(MH)
            cw, cb = P["m_conv_w"][j], P["m_conv_b"][j]
            dxx, dcw_x, dcb_x = _mconv_bwd(s["zx"], dxs[None], cw, cb, DI, 0, "mconv_bwd_x")
            dxb, dcw_b, dcb_b = _mconv_bwd(s["zx"], dbc, cw, cb, DI, DI, "mconv_bwd_bc")
            g["m_conv_w"][j] = jnp.concatenate([jnp.sum(dcw_x, axis=0), jnp.sum(dcw_b, axis=0)], axis=-1)
            g["m_conv_b"][j] = jnp.concatenate([jnp.sum(dcb_x, axis=(0, 1)), jnp.sum(dcb_b, axis=(0, 1))], axis=-1)
            ddt_t = _pad_cols(ddt[..., :hpg].transpose(0, 2, 3, 1, 4).reshape(T, MH), MPAD - DI - CD).astype(BF16)
            pieces = [(dz.reshape(T, DI), 0), (dxx.reshape(T, DI), DI), (dxb.reshape(T, 2 * G * N), 2 * DI), (ddt_t, DI + CD)]
            du = None
            gm["mix_in"] = lax.empty((MPAD, D), BF16)
            for n_, (piece, off) in enumerate(pieces):
                gm["mix_in"] = _matmul(piece, s["u"], ta=True, out_dtype=BF16, out=gm["mix_in"], out_off=off,
                                       name="m_in_dw%d" % n_)
                du = _matmul(piece, Wl["mix_in"], b_off=off, res=du, name="m_in_dx%d" % n_)
        dep = emit(i, gm)
        dh, dhb, g["mix_norm"][i] = _rmsnorm_bwd(s["h_in"], P["mix_norm"][i], du, dh, "mix_norm_bwd", dep=dep)

    grads = {k: jnp.stack(vs) for k, vs in g.items()}
    grads["final_norm"] = d_final
    _, lb_vjp = jax.vjp(_lower_bounds, P["hgrn_lb_logits"])
    grads["hgrn_lb_logits"] = lb_vjp(jnp.stack(dlbs))[0]
    return loss, dh.reshape(B, L, D), grads


ANY = pl.BlockSpec(memory_space=pl.ANY)
N_CHIPS = 4
N_DEV = 8


def _place():
    x, y, c = lax.axis_index("x"), lax.axis_index("y"), lax.axis_index("c")
    sibling = (x, y, 1 - c)
    chips = [(1 - x, y), (x, 1 - y), (1 - x, 1 - y)]
    return x, y, c, sibling, chips


def _remote(src, dst, send_sem, recv_sem, to):
    return pltpu.make_async_remote_copy(src_ref=src, dst_ref=dst, send_sem=send_sem, recv_sem=recv_sem, device_id=to,
                                        device_id_type=MESH)


KIND_AXIS = {"hgrn_w_in": "col", "f_w_up": "col", "m_w_in_t": "row", "hgrn_w_out": "row", "m_w_out": "row", "f_w_down": "row"}
KINDS = tuple(KIND_AXIS)
PEER_MASKS = (2, 1, 3)
ALL = slice(None)


def _chip_win(axis, cw, s):
    return (ALL, slice(s * cw, (s + 1) * cw)) if axis == "col" else (slice(s * cw, (s + 1) * cw), ALL)


def _half_win(axis, rows, cols, h):
    return (slice(h * rows // 2, (h + 1) * rows // 2), ALL) if axis == "col" else (ALL, slice(h * cols // 2, (h + 1) * cols // 2))


def _per_place(fn):
    x, y, c, sibling, chips = _place()
    chip = 2 * x + y
    for s in range(N_CHIPS):
        for cc in range(2):
            @pl.when(jnp.logical_and(chip == s, c == cc))
            def _():
                fn(s, cc, c, sibling, chips)


HBM = pl.BlockSpec(memory_space=pltpu.HBM)
SEM = pl.BlockSpec(memory_space=pltpu.SEMAPHORE)
EFFECT = pltpu.SideEffectType.DATAFLOW_SIDE_EFFECTING


def _cell(axis, rows, cols, cw, s, h):
    if axis == "col":
        return (slice(h * rows // 2, (h + 1) * rows // 2), slice(s * cw, (s + 1) * cw))
    return (slice(s * cw, (s + 1) * cw), slice(h * cols // 2, (h + 1) * cols // 2))


def _in_hbm(a):
    return pltpu.with_memory_space_constraint(a, pltpu.HBM)


def _stage_shard(kind, shard, layer, chip, pad_rows=0):
    _, R, C = shard.shape
    axis = KIND_AXIS[kind]
    tr, tc = _row_tile(R), _pick_tile(C, 2048)
    nr, nc = R // tr, C // tc
    full = (R, N_CHIPS * C) if axis == "col" else (N_CHIPS * R + pad_rows, C)

    def body(s_ref, x_ref, o_ref):
        o_ref[...] = x_ref[...].astype(BF16)

    if axis == "col":
        dst = pl.BlockSpec((tr, tc), lambda i, j, s_ref: (i, s_ref[0] * nc + j))
    else:
        dst = pl.BlockSpec((tr, tc), lambda i, j, s_ref: (s_ref[0] * nr + i, j))
    grid_spec = pltpu.PrefetchScalarGridSpec(
        num_scalar_prefetch=1, grid=(nr, nc),
        in_specs=[pl.BlockSpec((None, tr, tc), lambda i, j, s_ref: (layer, i, j))], out_specs=dst)
    out = pl.pallas_call(
        body, name="stage_" + kind, grid_spec=grid_spec, out_shape=jax.ShapeDtypeStruct(full, BF16),
        compiler_params=_cparams("parallel", "parallel"))(chip.reshape(1).astype(jnp.int32), shard)
    if pad_rows:
        rows0 = N_CHIPS * R
        pr = math.gcd(rows0, pad_rows)

        def zero_body(x_ref, o_ref):
            o_ref[...] = jnp.zeros_like(o_ref)

        out = pl.pallas_call(
            zero_body, name="zero_pad_" + kind, grid=(pad_rows // pr,), in_specs=[ANY],
            out_specs=pl.BlockSpec((pr, C), lambda i: (rows0 // pr + i, 0)), out_shape=jax.ShapeDtypeStruct(full, BF16),
            input_output_aliases={0: 0}, compiler_params=_cparams("parallel"))(out)
    return out


def _gather_start(items, mats, cws, after):
    n = len(items)

    def body(*refs):
        send_sems, recv_sems, token = refs[n + 1], refs[n + 2], refs[-1]
        m = refs[n + 3:2 * n + 3]

        def run(s, cc, c, sibling, chips):
            for q, (k, _) in enumerate(items):
                r, c_ = m[q].shape
                mine = m[q].at[_cell(KIND_AXIS[k], r, c_, cws[k], s, cc)]
                for j, (px, py) in enumerate(chips):
                    _remote(mine, mine, send_sems.at[3 * q + j], recv_sems.at[3 * q + j], (px, py, c)).start()

        _per_place(run)
        token[...] = jnp.zeros_like(token)

    outs = pl.pallas_call(
        body, name="gather_start", in_specs=[HBM] * n + [ANY],
        out_specs=[SEM, SEM] + [HBM] * n + [pl.BlockSpec(memory_space=pltpu.VMEM)],
        out_shape=[pltpu.SemaphoreType.DMA((3 * n,)), pltpu.SemaphoreType.DMA((3 * n,))]
        + [pltpu.HBM(a.shape, a.dtype) for a in mats] + [jax.ShapeDtypeStruct((8, LANES), F32)],
        input_output_aliases={q: 2 + q for q in range(n)},
        compiler_params=pltpu.CompilerParams(has_side_effects=EFFECT),
    )(*[_in_hbm(a) for a in mats], after)
    return outs[0], outs[1], list(outs[2:2 + n]), outs[-1]


def _gather_wait(items, idx, mats, send_sems, recv_sems, cws, after, name):
    n = len(idx)

    def body(*refs):
        m = refs[:n]
        s_sems, r_sems = refs[n], refs[n + 1]

        def run(s, cc, c, sibling, chips):
            for a, q in enumerate(idx):
                k = items[q][0]
                r, c_ = m[a].shape
                mine = m[a].at[_cell(KIND_AXIS[k], r, c_, cws[k], s, cc)]
                for j, (px, py) in enumerate(chips):
                    theirs = m[a].at[_cell(KIND_AXIS[k], r, c_, cws[k], s ^ PEER_MASKS[j], cc)]
                    cp = _remote(mine, theirs, s_sems.at[3 * q + j], r_sems.at[3 * q + j], (px, py, c))
                    cp.wait_send()
                    cp.wait_recv()

        _per_place(run)

    outs = pl.pallas_call(
        body, name=name, in_specs=[HBM] * n + [SEM, SEM, ANY], out_specs=[HBM] * n,
        out_shape=[pltpu.HBM(a.shape, a.dtype) for a in mats], input_output_aliases={a: a for a in range(n)},
        compiler_params=pltpu.CompilerParams(has_side_effects=EFFECT),
    )(*mats, send_sems, recv_sems, after)
    return list(outs)


def _forward_halves(kinds, mats, cws, name):
    n = len(mats)

    def body(*refs):
        m = refs[n:2 * n]
        send_sems, recv_sems = refs[2 * n:]

        def run(s, cc, c, sibling, chips):
            cps = []
            for a, k in enumerate(kinds):
                r, c_ = m[a].shape
                for j in range(3):
                    have = m[a].at[_cell(KIND_AXIS[k], r, c_, cws[k], s ^ PEER_MASKS[j], cc)]
                    cps.append(_remote(have, have, send_sems.at[3 * a + j], recv_sems.at[3 * a + j], sibling))
            for cp in cps:
                cp.start()
            for cp in cps:
                cp.wait()

        _per_place(run)

    outs = pl.pallas_call(
        body, name=name, in_specs=[ANY] * n, out_specs=[ANY] * n,
        out_shape=[jax.ShapeDtypeStruct(a.shape, a.dtype) for a in mats], input_output_aliases={a: a for a in range(n)},
        scratch_shapes=[pltpu.SemaphoreType.DMA((3 * n,)), pltpu.SemaphoreType.DMA((3 * n,))],
    )(*mats)
    return list(outs)


def _swap_halves(kinds, gms, name):
    n = len(gms)
    half_shapes = [(g.shape[0] // 2, g.shape[1]) if KIND_AXIS[k] == "col" else (g.shape[0], g.shape[1] // 2)
                   for k, g in zip(kinds, gms)]

    def body(*refs):
        g, ra = refs[:n], refs[n:2 * n]
        send_sems, recv_sems = refs[2 * n:]

        def run(s, cc, c, sibling, chips):
            cps = []
            for a, k in enumerate(kinds):
                r, c_ = g[a].shape
                cps.append(_remote(g[a].at[_half_win(KIND_AXIS[k], r, c_, 1 - cc)], ra[a], send_sems.at[a], recv_sems.at[a],
                                   sibling))
            for cp in cps:
                cp.start()
            for cp in cps:
                cp.wait()

        _per_place(run)

    outs = pl.pallas_call(
        body, name=name, in_specs=[ANY] * n, out_specs=[ANY] * n,
        out_shape=[jax.ShapeDtypeStruct(hs, BF16) for hs in half_shapes],
        scratch_shapes=[pltpu.SemaphoreType.DMA((n,)), pltpu.SemaphoreType.DMA((n,))],
    )(*gms)
    return list(outs)


def _win_shape(kind, pa, cw):
    return (pa.shape[0], cw) if KIND_AXIS[kind] == "col" else (cw, pa.shape[1])


def _scatter_start(kinds, pas, cws, name):
    n = len(pas)
    lands = [lax.empty((3,) + _win_shape(k, p, cws[k]), BF16) for k, p in zip(kinds, pas)]

    def body(*refs):
        send_sems, recv_sems, token = refs[2 * n], refs[2 * n + 1], refs[-1]
        p, rb = refs[2 * n + 2:3 * n + 2], refs[3 * n + 2:4 * n + 2]

        def run(s, cc, c, sibling, chips):
            for a, k in enumerate(kinds):
                for j, (px, py) in enumerate(chips):
                    src = p[a].at[_chip_win(KIND_AXIS[k], cws[k], s ^ PEER_MASKS[j])]
                    _remote(src, rb[a].at[j], send_sems.at[3 * a + j], recv_sems.at[3 * a + j], (px, py, c)).start()

        _per_place(run)
        token[...] = jnp.zeros_like(token)

    outs = pl.pallas_call(
        body, name=name, in_specs=[HBM] * (2 * n),
        out_specs=[SEM, SEM] + [HBM] * (2 * n) + [pl.BlockSpec(memory_space=pltpu.VMEM)],
        out_shape=[pltpu.SemaphoreType.DMA((3 * n,)), pltpu.SemaphoreType.DMA((3 * n,))]
        + [pltpu.HBM(a.shape, a.dtype) for a in pas + lands] + [jax.ShapeDtypeStruct((8, LANES), F32)],
        input_output_aliases={q: 2 + q for q in range(2 * n)},
        compiler_params=pltpu.CompilerParams(has_side_effects=EFFECT),
    )(*[_in_hbm(a) for a in pas + lands])
    return outs[0], outs[1], list(outs[2:2 + n]), list(outs[2 + n:2 + 2 * n]), outs[-1]


def _scatter_wait(kinds, pas, lands, send_sems, recv_sems, cws, after, name):
    n = len(pas)

    def body(*refs):
        p, rb = refs[:n], refs[n:2 * n]
        s_sems, r_sems = refs[2 * n], refs[2 * n + 1]

        def run(s, cc, c, sibling, chips):
            for a, k in enumerate(kinds):
                for j, (px, py) in enumerate(chips):
                    src = p[a].at[_chip_win(KIND_AXIS[k], cws[k], s ^ PEER_MASKS[j])]
                    cp = _remote(src, rb[a].at[j], s_sems.at[3 * a + j], r_sems.at[3 * a + j], (px, py, c))
                    cp.wait_send()
                    cp.wait_recv()

        _per_place(run)

    outs = pl.pallas_call(
        body, name=name, in_specs=[HBM] * (2 * n) + [SEM, SEM, ANY], out_specs=[HBM] * (2 * n),
        out_shape=[pltpu.HBM(a.shape, a.dtype) for a in pas + lands], input_output_aliases={a: a for a in range(2 * n)},
        compiler_params=pltpu.CompilerParams(has_side_effects=EFFECT),
    )(*pas, *lands, send_sems, recv_sems, after)
    return list(outs[:n]), list(outs[n:])


def _share_halves(g):
    nq = len(KINDS)

    def body(*refs):
        out = dict(zip(KINDS, refs[nq:2 * nq]))
        send_sems, recv_sems = refs[2 * nq:]

        def run(s, cc, c, sibling, chips):
            cps = []
            for q, k in enumerate(KINDS):
                _, r, c_ = out[k].shape
                mine = out[k].at[(ALL,) + _half_win(KIND_AXIS[k], r, c_, cc)]
                cps.append(_remote(mine, mine, send_sems.at[q], recv_sems.at[q], sibling))
            for cp in cps:
                cp.start()
            for cp in cps:
                cp.wait()

        _per_place(run)

    outs = pl.pallas_call(
        body, name="share_halves", in_specs=[ANY] * nq, out_specs=[ANY] * nq,
        out_shape=[jax.ShapeDtypeStruct(g[k].shape, F32) for k in KINDS],
        input_output_aliases={q: q for q in range(nq)},
        scratch_shapes=[pltpu.SemaphoreType.DMA((nq,)), pltpu.SemaphoreType.DMA((nq,))],
    )(*[g[k] for k in KINDS])
    return dict(zip(KINDS, outs))


def _all_gather_small(xs, name):
    m_per, n = xs.shape

    def body(x_ref, out_ref, send_sems, recv_sems, local_sem):
        x, y, c, sibling, chips = _place()
        me = (x, y, c)

        def rows(px, py, pc):
            return out_ref.at[pl.ds((4 * px + 2 * py + pc) * m_per, m_per), :]

        def copy(k, block, to, src=None):
            return _remote(rows(*block) if src is None else src, rows(*block), send_sems.at[k], recv_sems.at[k], to)

        mine = pltpu.make_async_copy(x_ref, rows(*me), local_sem)
        mine.start()
        first = [copy(0, me, sibling, src=x_ref)]
        first += [copy(1 + j, me, (*chip, c), src=x_ref) for j, chip in enumerate(chips)]
        for cp in first:
            cp.start()
        passed = [copy(4 + j, (*chip, c), sibling) for j, chip in enumerate(chips)]
        for j, chip in enumerate(chips):
            copy(1 + j, (*chip, c), me).wait_recv()
            passed[j].start()
        copy(0, sibling, me).wait_recv()
        for j, chip in enumerate(chips):
            copy(4 + j, (*chip, 1 - c), me).wait_recv()
        for cp in first + passed:
            cp.wait_send()
        mine.wait()

    vm = pl.BlockSpec(memory_space=pltpu.VMEM)
    return pl.pallas_call(
        body, name=name, in_specs=[vm], out_specs=vm, out_shape=jax.ShapeDtypeStruct((N_DEV * m_per, n), xs.dtype),
        scratch_shapes=[pltpu.SemaphoreType.DMA((7,)), pltpu.SemaphoreType.DMA((7,)), pltpu.SemaphoreType.DMA],
        compiler_params=pltpu.CompilerParams(vmem_limit_bytes=VMEM_LIMIT_BYTES),
    )(xs)


def _row_tile(rows, cap=256):
    for mult in (16, 8):
        best = None
        t = mult
        while t <= min(rows, cap):
            if rows % t == 0:
                best = t
            t += mult
        if best is not None:
            return best
    raise ValueError(rows)


def _add_sibling(kind, g, ra, core):
    R, C = ra.shape
    axis = KIND_AXIS[kind]
    tr, tc = _row_tile(R), _pick_tile(C, 2048)
    nr, nc = R // tr, C // tc

    def body(c_ref, a_ref, b_ref, o_ref):
        o_ref[...] = (a_ref[...].astype(F32) + b_ref[...].astype(F32)).astype(o_ref.dtype)

    if axis == "col":
        own = pl.BlockSpec((tr, tc), lambda i, j, c_ref: (c_ref[0] * nr + i, j))
    else:
        own = pl.BlockSpec((tr, tc), lambda i, j, c_ref: (i, c_ref[0] * nc + j))
    same = pl.BlockSpec((tr, tc), lambda i, j, c_ref: (i, j))
    grid_spec = pltpu.PrefetchScalarGridSpec(num_scalar_prefetch=1, grid=(nr, nc), in_specs=[own, same], out_specs=same)
    return pl.pallas_call(
        body, name="add_sibling_" + kind, grid_spec=grid_spec, out_shape=jax.ShapeDtypeStruct(ra.shape, BF16),
        compiler_params=_cparams("parallel", "parallel"))(core.reshape(1).astype(jnp.int32), g, ra)


def _sum_chips(kind, pa, rb, chip, core, out, layer):
    _, R, C = rb.shape
    axis = KIND_AXIS[kind]
    tr, tc = _row_tile(R), _pick_tile(C, 2048)
    nr, nc = R // tr, C // tc

    def body(s_ref, c_ref, a_ref, b0_ref, b1_ref, b2_ref, old_ref, o_ref):
        o_ref[...] = ((a_ref[...].astype(F32) + b0_ref[...].astype(F32)) + b1_ref[...].astype(F32)) + b2_ref[...].astype(F32)

    def rb_spec(n):
        return pl.BlockSpec((None, tr, tc), lambda i, j, s_ref, c_ref: (n, i, j))

    if axis == "col":
        own = pl.BlockSpec((tr, tc), lambda i, j, s_ref, c_ref: (i, s_ref[0] * nc + j))
        dst = pl.BlockSpec((None, tr, tc), lambda i, j, s_ref, c_ref: (layer, c_ref[0] * nr + i, j))
        assert out.shape[1:] == (2 * R, C)
    else:
        own = pl.BlockSpec((tr, tc), lambda i, j, s_ref, c_ref: (s_ref[0] * nr + i, j))
        dst = pl.BlockSpec((None, tr, tc), lambda i, j, s_ref, c_ref: (layer, i, c_ref[0] * nc + j))
        assert out.shape[1:] == (R, 2 * C)
    grid_spec = pltpu.PrefetchScalarGridSpec(
        num_scalar_prefetch=2, grid=(nr, nc), in_specs=[own, rb_spec(0), rb_spec(1), rb_spec(2), ANY], out_specs=dst)
    return pl.pallas_call(
        body, name="sum_chips_" + kind, grid_spec=grid_spec, out_shape=jax.ShapeDtypeStruct(out.shape, F32),
        input_output_aliases={6: 0}, compiler_params=_cparams("parallel", "parallel"))(
            chip.reshape(1).astype(jnp.int32), core.reshape(1).astype(jnp.int32), pa, rb, rb, rb, out)


def _sum_devices(gathered):
    M = gathered.shape[0] // N_DEV
    C = gathered.shape[1]

    def body(g_ref, o_ref):
        acc = g_ref[0:M, :]
        for d in range(1, N_DEV):
            acc = acc + g_ref[d * M:(d + 1) * M, :]
        o_ref[...] = acc

    vm = pl.BlockSpec(memory_space=pltpu.VMEM)
    return pl.pallas_call(body, name="sum_devices", in_specs=[vm], out_specs=vm, out_shape=jax.ShapeDtypeStruct((M, C), F32),
                          compiler_params=pltpu.CompilerParams(vmem_limit_bytes=VMEM_LIMIT_BYTES))(gathered)


WEIGHTS = ["mix_norm", "ffn_norm", "final_norm", "hgrn_w_in", "hgrn_lb_logits", "hgrn_gnorm", "hgrn_w_out", "m_w_in",
           "m_conv_w", "m_conv_b", "m_dt_bias", "m_A_log", "m_D", "m_norm", "m_w_out", "f_w_up", "f_conv_w", "f_conv_b",
           "f_w_down"]
BIG_COLS = ("hgrn_w_in", "m_w_in", "f_w_up")
BIG_ROWS = ("hgrn_w_out", "m_w_out", "f_w_down")
BIG = BIG_COLS + BIG_ROWS
SMALL_SHARDED = ("m_conv_w", "m_conv_b", "m_norm", "f_conv_w")
SMALL_REPLICATED = ("mix_norm", "ffn_norm", "final_norm", "hgrn_lb_logits", "hgrn_gnorm", "m_dt_bias", "m_A_log", "m_D",
                    "f_conv_b")
SMALL = SMALL_REPLICATED + SMALL_SHARDED


def _pack_rows(arrs, row_mult=8):
    flat = jnp.concatenate([a.reshape(-1).astype(F32) for a in arrs])
    unit = FLAT_COLS * row_mult
    n = -(-flat.size // unit) * unit
    return jnp.pad(flat, (0, n - flat.size)).reshape(-1, FLAT_COLS)


def _unpack_rows(buf, shapes):
    flat = buf.reshape(-1)
    out, off = [], 0
    for shp in shapes:
        n = math.prod(shp)
        out.append(flat[off:off + n].reshape(shp))
        off += n
    return out


def kernel(x, mix_norm, ffn_norm, final_norm, hgrn_w_in, hgrn_lb_logits, hgrn_gnorm, hgrn_w_out, m_w_in, m_conv_w, m_conv_b, m_dt_bias, m_A_log, m_D, m_norm, m_w_out, f_w_up, f_conv_w, f_conv_b, f_w_down, loss_target, m_mix_norm, m_ffn_norm, m_final_norm, m_hgrn_w_in, m_hgrn_lb_logits, m_hgrn_gnorm, m_hgrn_w_out, m_m_w_in, m_m_conv_w, m_m_conv_b, m_m_dt_bias, m_m_A_log, m_m_D, m_m_norm, m_m_w_out, m_f_w_up, m_f_conv_w, m_f_conv_b, m_f_w_down, v_mix_norm, v_ffn_norm, v_final_norm, v_hgrn_w_in, v_hgrn_lb_logits, v_hgrn_gnorm, v_hgrn_w_out, v_m_w_in, v_m_conv_w, v_m_conv_b, v_m_dt_bias, v_m_A_log, v_m_D, v_m_norm, v_m_w_out, v_f_w_up, v_f_conv_w, v_f_conv_b, v_f_w_down):
    given = dict(locals())
    w = {n: given[n] for n in WEIGHTS}
    mom1 = {n: given["m_" + n] for n in WEIGHTS}
    mom2 = {n: given["v_" + n] for n in WEIGHTS}
    chip = 2 * lax.axis_index("x") + lax.axis_index("y")
    core = lax.axis_index("c")

    shards = {k: w[k] for k in KINDS if k != "m_w_in_t"}
    shards["m_w_in_t"] = w["m_w_in"].transpose(0, 2, 1).astype(BF16)
    m_in = N_CHIPS * w["m_w_in"].shape[2]
    pad_rows = {"m_w_in_t": -(-m_in // LANES) * LANES - m_in}
    cws = {k: shards[k].shape[2] if KIND_AXIS[k] == "col" else shards[k].shape[1] for k in KINDS}
    depth = w["mix_norm"].shape[0]

    def layer_kinds(i):
        mixer = {"mix_in": ("hgrn_w_in", i // 2), "mix_out": ("hgrn_w_out", i // 2)} if i % 2 == 0 else \
                {"mix_in": ("m_w_in_t", i // 2), "mix_out": ("m_w_out", i // 2)}
        return {**mixer, "f_w_up": ("f_w_up", i), "f_w_down": ("f_w_down", i)}

    items = [it for i in range(depth) for it in layer_kinds(i).values()]
    staged = [_stage_shard(k, shards[k], l, chip, pad_rows.get(k, 0)) for k, l in items]
    own = _pack_rows([w[n] for n in SMALL_SHARDED])
    all_small = _all_gather_small(own, "gather_small_params")
    send_sems, recv_sems, mats, _ = _gather_start(items, staged, cws, all_small)
    all_small = all_small.reshape(N_CHIPS, 2, -1)[:, 0]
    per_chip = [_unpack_rows(all_small[s], [w[n].shape for n in SMALL_SHARDED]) for s in range(N_CHIPS)]
    P = {}
    for i, n in enumerate(SMALL_SHARDED):
        P[n] = jnp.concatenate([per_chip[s][i] for s in range(N_CHIPS)], axis=-1)
    for n in SMALL_REPLICATED:
        P[n] = w[n]

    def fetch(i, keys, h):
        lk = {key: layer_kinds(i)[key] for key in keys}
        idx = [items.index(it) for it in lk.values()]
        tag = "%d_%s" % (i, keys[0])
        got = _gather_wait(items, idx, [mats[q] for q in idx], send_sems, recv_sems, cws, h, "gather_wait_" + tag)
        got = _forward_halves([k for k, _ in lk.values()], got, cws, "forward_halves_" + tag)
        return dict(zip(lk.keys(), got))

    pending = {}

    def emit(i, gm):
        lk = layer_kinds(i)
        kinds = [k for k, _ in lk.values()]
        gms = [gm[key] for key in lk]
        ra = _swap_halves(kinds, gms, "swap_halves_%d" % i)
        pas = [_add_sibling(k, g_, r_, core) for k, g_, r_ in zip(kinds, gms, ra)]
        s_sems, r_sems, pas, lands, tok = _scatter_start(kinds, pas, cws, "scatter_start_%d" % i)
        pending[i] = (kinds, pas, lands, s_sems, r_sems)
        return tok

    loss_part, grad_x, g_full = _local_step(x, loss_target, P, fetch, emit)

    g_sh = {k: lax.empty(shards[k].shape, F32) for k in KINDS}
    for i in reversed(range(depth)):
        kinds, pas, lands, s_sems, r_sems = pending[i]
        pas, lands = _scatter_wait(kinds, pas, lands, s_sems, r_sems, cws, grad_x, "scatter_wait_%d" % i)
        for (k, l), p_, rb_ in zip(layer_kinds(i).values(), pas, lands):
            g_sh[k] = _sum_chips(k, p_, rb_, chip, core, g_sh[k], l)
    g_sh = _share_halves(g_sh)
    grads = {k: g_sh[k] for k in KINDS if k != "m_w_in_t"}
    grads["m_w_in"] = g_sh["m_w_in_t"].transpose(0, 2, 1)

    small_shapes = [g_full[n].shape for n in SMALL] + [(1,)]
    packed = _pack_rows([g_full[n] for n in SMALL] + [loss_part[0, 0:1]])
    summed = _sum_devices(_all_gather_small(packed, "gather_small_grads"))
    small = _unpack_rows(summed, small_shapes)
    loss = small[-1][0]
    for n, gs in zip(SMALL, small[:-1]):
        if n in SMALL_SHARDED:
            width = w[n].shape[-1]
            gs = lax.dynamic_slice_in_dim(gs, chip * width, width, axis=gs.ndim - 1)
        grads[n] = gs

    delta, new_m, new_v = {}, {}, {}
    for n in BIG:
        delta[n], new_m[n], new_v[n] = _adamw(w[n], grads[n], mom1[n], mom2[n], "adamw_" + n)
    shapes = [w[n].shape for n in SMALL]
    ds, ms, vs = _adamw(_pack_rows([w[n] for n in SMALL]), _pack_rows([grads[n] for n in SMALL]),
                        _pack_rows([mom1[n] for n in SMALL]), _pack_rows([mom2[n] for n in SMALL]), "adamw_small")
    for n, d_, m_, v_ in zip(SMALL, _unpack_rows(ds, shapes), _unpack_rows(ms, shapes), _unpack_rows(vs, shapes)):
        delta[n], new_m[n], new_v[n] = d_, m_, v_

    return (loss, grad_x, *[grads[n] for n in WEIGHTS], *[delta[n] for n in WEIGHTS], *[new_m[n] for n in WEIGHTS],
            *[new_v[n] for n in WEIGHTS])
```

```python
import functools
import math

import jax
import jax.numpy as jnp
from jax import lax
from jax.experimental import pallas as pl
from jax.experimental.pallas import tpu as pltpu

F32 = jnp.float32
BF16 = jnp.bfloat16
NORM_EPS = 1e-5
HGRN_DK = 128
HGRN_CHUNK = 64
HGRN_HEADS_PER_STEP = 4
HGRN_SEQ_BLOCK = 512
M_HEADDIM = 64
M_GROUPS = 8
M_D_STATE = 128
M_CONV = 4
M_CHUNK = 128
FFN_CONV = 3
EXP_CLIP = 80.0
LANES = 128
VMEM_LIMIT_BYTES = 56 * 1024 * 1024
FLAT_COLS = 1024
ADAM_LR, ADAM_B1, ADAM_B2, ADAM_EPS, ADAM_WD, ADAM_STEP = 0.001, 0.9, 0.999, 1e-08, 0.01, 10
MESH = pl.DeviceIdType.MESH

NN = (((1,), (0,)), ((), ()))
NT = (((1,), (1,)), ((), ()))
TN = (((0,), (0,)), ((), ()))


def _cparams(*sems):
    return pltpu.CompilerParams(dimension_semantics=sems, vmem_limit_bytes=VMEM_LIMIT_BYTES)


def _dot(a, b, dn=NN):
    return lax.dot_general(a.astype(BF16), b.astype(BF16), dn, preferred_element_type=F32)


def _dot_exact(x, m, dn=NN, passes=3, x_first=True):
    acc = None
    r = x
    for _ in range(passes):
        p = r.astype(BF16)
        r = r - p.astype(F32)
        t = lax.dot_general(p, m, dn, preferred_element_type=F32) if x_first else lax.dot_general(m, p, dn, preferred_element_type=F32)
        acc = t if acc is None else acc + t
    return acc


def _iota(shape, dim):
    return lax.broadcasted_iota(jnp.int32, shape, dim)


def _cumsum_rows(x, reverse=False):
    n = x.shape[0]
    row = _iota(x.shape, 0)
    s = 1
    while s < n:
        if reverse:
            x = x + jnp.where(row < n - s, pltpu.roll(x, n - s, 0), 0.0)
        else:
            x = x + jnp.where(row >= s, pltpu.roll(x, s, 0), 0.0)
        s *= 2
    return x


def _silu(x):
    return x * jax.nn.sigmoid(x)


def _dsilu(x):
    s = jax.nn.sigmoid(x)
    return s * (1.0 + x * (1.0 - s))


def _pick_tile(dim, pref):
    if dim <= pref:
        return dim
    best = None
    t = LANES
    while t <= pref:
        if dim % t == 0:
            best = t
        t += LANES
    assert best is not None, (dim, pref)
    return best


def _matmul(a, b, *, ta=False, tb=False, res=None, out_dtype=F32, tm=1024, tn=1024, tk=2048, name,
            a_parts=False, b_parts=False, b_layer=None, b_off=0, out=None, out_layer=None, out_off=0):
    a = a.astype(BF16)
    b = b.astype(BF16)
    if a_parts:
        assert not ta
        pa, M, kp = a.shape
        K = pa * kp
    else:
        M, K = (a.shape[1], a.shape[0]) if ta else a.shape
    bsh = b.shape[1:] if b_layer is not None else b.shape
    if b_parts:
        assert not tb
        pb, _, np_ = bsh
        N = pb * np_
    else:
        N = bsh[0] if tb else bsh[1]
    tm, tn, tk = _pick_tile(M, tm), _pick_tile(np_ if b_parts else N, tn), _pick_tile(kp if a_parts else K, tk)
    nk = K // tk
    dn = (((0 if ta else 1,), (1 if tb else 0,)), ((), ()))
    assert b_off % tk == 0 and out_off % tm == 0

    def body(*refs):
        refs = list(refs)
        acc = refs.pop() if nk > 1 else None
        o_ref = refs.pop()
        if out is not None:
            refs.pop()
        a_ref, b_ref = refs[0], refs[1]
        r_ref = refs[2] if res is not None else None
        k = pl.program_id(2)

        def prod():
            return lax.dot_general(a_ref[...], b_ref[...], dn, preferred_element_type=F32)

        def finish(r):
            if res is not None:
                r = r + r_ref[...]
            o_ref[...] = r.astype(out_dtype)

        if nk == 1:
            finish(prod())
            return

        @pl.when(k == 0)
        def _():
            acc[...] = prod()

        @pl.when(jnp.logical_and(k > 0, k < nk - 1))
        def _():
            acc[...] += prod()

        @pl.when(k == nk - 1)
        def _():
            finish(acc[...] + prod())

    if a_parts:
        kpb = kp // tk
        a_spec = pl.BlockSpec((None, tm, tk), lambda i, j, k: (k // kpb, i, k % kpb))
    elif ta:
        a_spec = pl.BlockSpec((tk, tm), lambda i, j, k: (k, i))
    else:
        a_spec = pl.BlockSpec((tm, tk), lambda i, j, k: (i, k))
    lead = () if b_layer is None else (b_layer,)
    lead_blk = () if b_layer is None else (None,)
    kb0 = b_off // tk
    if b_parts:
        npb = np_ // tn
        b_spec = pl.BlockSpec(lead_blk + (None, tk, tn), lambda i, j, k: lead + (j // npb, k, j % npb))
    elif tb:
        b_spec = pl.BlockSpec(lead_blk + (tn, tk), lambda i, j, k: lead + (j, k))
    else:
        b_spec = pl.BlockSpec(lead_blk + (tk, tn), lambda i, j, k: lead + (kb0 + k, j))
    r_spec = pl.BlockSpec((tm, tn), lambda i, j, k: (i, j))
    in_specs = [a_spec, b_spec] + ([r_spec] if res is not None else [])
    args = (a, b) + ((res,) if res is not None else ())
    if out is None:
        o_spec, out_shape, aliases = r_spec, jax.ShapeDtypeStruct((M, N), out_dtype), {}
    else:
        assert out.dtype == out_dtype and out.shape[-1] == N
        olead = () if out_layer is None else (out_layer,)
        olead_blk = () if out_layer is None else (None,)
        ob0 = out_off // tm
        o_spec = pl.BlockSpec(olead_blk + (tm, tn), lambda i, j, k: olead + (ob0 + i, j))
        out_shape = jax.ShapeDtypeStruct(out.shape, out.dtype)
        aliases = {len(args): 0}
        in_specs = in_specs + [pl.BlockSpec(memory_space=pl.ANY)]
        args = args + (out,)
    return pl.pallas_call(
        body, name=name, grid=(M // tm, N // tn, nk), in_specs=in_specs, out_specs=o_spec, out_shape=out_shape,
        scratch_shapes=[pltpu.VMEM((tm, tn), F32)] if nk > 1 else [], input_output_aliases=aliases,
        compiler_params=_cparams("parallel", "parallel", "arbitrary"))(*args)


def _rmsnorm_fwd(h, w, name):
    T, D = h.shape
    tm = _pick_tile(T, 256)

    def body(h_ref, w_ref, u_ref):
        x = h_ref[...]
        r = lax.rsqrt(jnp.mean(x * x, axis=-1, keepdims=True) + NORM_EPS)
        u_ref[...] = (x * r * w_ref[...]).astype(BF16)

    return pl.pallas_call(
        body, name=name, grid=(T // tm,),
        in_specs=[pl.BlockSpec((tm, D), lambda i: (i, 0)), pl.BlockSpec((1, D), lambda i: (0, 0))],
        out_specs=pl.BlockSpec((tm, D), lambda i: (i, 0)), out_shape=jax.ShapeDtypeStruct((T, D), BF16),
        compiler_params=_cparams("parallel"))(h, w.reshape(1, D))


def _rmsnorm_bwd(h, w, du, dres, name, dep=None):
    T, D = h.shape
    tm = _pick_tile(T, 256)

    def body(h_ref, w_ref, du_ref, dr_ref, *rest):
        dh_ref, dhb_ref, dw_ref = rest[-3:]
        x = h_ref[...]
        r = lax.rsqrt(jnp.mean(x * x, axis=-1, keepdims=True) + NORM_EPS)
        xh = x * r
        du_ = du_ref[...]
        dy = du_ * w_ref[...]
        dh = dr_ref[...] + r * (dy - xh * jnp.mean(dy * xh, axis=-1, keepdims=True))
        dh_ref[...] = dh
        dhb_ref[...] = dh.astype(BF16)
        part = jnp.sum(du_ * xh, axis=0, keepdims=True)

        @pl.when(pl.program_id(0) == 0)
        def _():
            dw_ref[...] = part

        @pl.when(pl.program_id(0) > 0)
        def _():
            dw_ref[...] += part

    row = pl.BlockSpec((tm, D), lambda i: (i, 0))
    vec = pl.BlockSpec((1, D), lambda i: (0, 0))
    extra_specs, extra = ([], ()) if dep is None else ([pl.BlockSpec(memory_space=pl.ANY)], (dep,))
    dh, dhb, dw = pl.pallas_call(
        body, name=name, grid=(T // tm,), in_specs=[row, vec, row, row] + extra_specs, out_specs=[row, row, vec],
        out_shape=[jax.ShapeDtypeStruct((T, D), F32), jax.ShapeDtypeStruct((T, D), BF16), jax.ShapeDtypeStruct((1, D), F32)],
        compiler_params=_cparams("arbitrary"))(h, w.reshape(1, D), du, dres, *extra)
    return dh, dhb, dw.reshape(D)


def _loss_head(h, w, target):
    T, D = h.shape
    tm = _pick_tile(T, 256)

    def body(h_ref, w_ref, t_ref, loss_ref, dh_ref, dhb_ref, dw_ref):
        x = h_ref[...]
        wv = w_ref[...]
        r = lax.rsqrt(jnp.mean(x * x, axis=-1, keepdims=True) + NORM_EPS)
        xh = x * r
        e = xh * wv - t_ref[...]
        lpart = jnp.zeros((1, LANES), F32) + 0.5 * jnp.sum(jnp.mean(e * e, axis=-1, keepdims=True))
        dyo = e * (1.0 / D)
        dy = dyo * wv
        dh = r * (dy - xh * jnp.mean(dy * xh, axis=-1, keepdims=True))
        dh_ref[...] = dh
        dhb_ref[...] = dh.astype(BF16)
        part = jnp.sum(dyo * xh, axis=0, keepdims=True)

        @pl.when(pl.program_id(0) == 0)
        def _():
            dw_ref[...] = part
            loss_ref[...] = lpart

        @pl.when(pl.program_id(0) > 0)
        def _():
            dw_ref[...] += part
            loss_ref[...] += lpart

    row = pl.BlockSpec((tm, D), lambda i: (i, 0))
    vec = pl.BlockSpec((1, D), lambda i: (0, 0))
    lvec = pl.BlockSpec((1, LANES), lambda i: (0, 0))
    loss, dh, dhb, dw = pl.pallas_call(
        body, name="loss_head", grid=(T // tm,), in_specs=[row, vec, row], out_specs=[lvec, row, row, vec],
        out_shape=[jax.ShapeDtypeStruct((1, LANES), F32), jax.ShapeDtypeStruct((T, D), F32),
                   jax.ShapeDtypeStruct((T, D), BF16), jax.ShapeDtypeStruct((1, D), F32)],
        compiler_params=_cparams("arbitrary"))(h, w.reshape(1, D), target)
    return loss, dh, dhb, dw.reshape(D)


def _hgrn_gates(qr, fr, lb):
    sig = jax.nn.sigmoid(fr)
    nsig = jax.nn.sigmoid(-fr)
    fg = lb + (1.0 - lb) * sig
    logf = jnp.log(fg)
    k = (1.0 - lb) * nsig
    q = _silu(qr)
    return q, k, logf, sig, nsig, fg


def _hgrn_scaled(q, k, b, bmid):
    eq = jnp.exp(jnp.clip(b - bmid, -EXP_CLIP, EXP_CLIP))
    ek = jnp.exp(jnp.clip(bmid - b, -EXP_CLIP, EXP_CLIP))
    return q * eq, k * ek, eq, ek


def _hgrn_fwd(proj, lb, gnw, H):
    B, L, _ = proj.shape
    C, DK = HGRN_CHUNK, HGRN_DK
    F_ = H * DK
    NC = L // C

    nh = HGRN_HEADS_PER_STEP if H % HGRN_HEADS_PER_STEP == 0 else 1
    LB = min(L, HGRN_SEQ_BLOCK)
    ncb, nsb, WD = LB // C, L // LB, nh * DK

    def body(q_ref, f_ref, v_ref, g_ref, lb_ref, gn_ref, o_ref, on_ref, st_ref, ST, bsc):
        @pl.when(pl.program_id(2) == 0)
        def _():
            ST[...] = jnp.zeros_like(ST)

        gn = gn_ref[...]
        causal = _iota((C, C), 0) >= _iota((C, C), 1)

        def chunk(c, carry):
            r0 = pl.multiple_of(c * C, C)
            rows = pl.ds(r0, C)
            for hh in range(nh):
                ln = slice(hh * DK, (hh + 1) * DK)
                q, k, logf, _, _, _ = _hgrn_gates(q_ref[0, rows, ln], f_ref[0, rows, ln], lb_ref[:, ln])
                v = v_ref[0, rows, ln]
                b = _cumsum_rows(logf)
                bsc[hh] = b
                bmid = bsc[hh, C // 2 - 1:C // 2, :]
                blast = bsc[hh, C - 1:C, :]
                qs, ks, _, _ = _hgrn_scaled(q, k, b, bmid)
                A = jnp.where(causal, _dot(qs, ks, NT), 0.0)
                st = ST[hh]
                st_ref[0, hh, c] = st
                o = _dot(A, v) + _dot(q * jnp.exp(b), st, NT)
                kb = k * jnp.exp(blast - b)
                ST[hh] = st * jnp.exp(blast) + _dot(v, kb, TN)
                rms = lax.rsqrt(jnp.mean(o * o, axis=-1, keepdims=True) + NORM_EPS)
                o_ref[0, rows, ln] = o
                on_ref[0, rows, ln] = (o * rms * gn * _silu(g_ref[0, rows, ln])).astype(BF16)
            return carry

        lax.fori_loop(0, ncb, chunk, 0)

    def col(off):
        return pl.BlockSpec((1, LB, WD), lambda b, hp, s: (b, s, off // nh + hp))

    return pl.pallas_call(
        body, name="hgrn_fwd", grid=(B, H // nh, nsb),
        in_specs=[col(0), col(H), col(2 * H), col(3 * H), pl.BlockSpec((1, WD), lambda b, hp, s: (0, hp)),
                  pl.BlockSpec((1, DK), lambda b, hp, s: (0, 0))],
        out_specs=[col(0), col(0), pl.BlockSpec((1, nh, ncb, DK, DK), lambda b, hp, s: (b, hp, s, 0, 0))],
        out_shape=[jax.ShapeDtypeStruct((B, L, F_), F32), jax.ShapeDtypeStruct((B, L, F_), BF16),
                   jax.ShapeDtypeStruct((B, H, NC, DK, DK), F32)],
        scratch_shapes=[pltpu.VMEM((nh, DK, DK), F32), pltpu.VMEM((nh, C, DK), F32)],
        compiler_params=_cparams("parallel", "parallel", "arbitrary"))(proj, proj, proj, proj, lb, gnw)


def _hgrn_bwd(proj, o, don, st, lb, gnw, H):
    B, L, _ = proj.shape
    C, DK = HGRN_CHUNK, HGRN_DK
    F_ = H * DK
    NC = L // C

    nh = HGRN_HEADS_PER_STEP if H % HGRN_HEADS_PER_STEP == 0 else 1
    LB = min(L, HGRN_SEQ_BLOCK)
    ncb, nsb, WD = LB // C, L // LB, nh * DK

    def body(q_ref, f_ref, v_ref, g_ref, o_ref, do_ref, st_ref, lb_ref, gn_ref,
             dp_ref, dlb_ref, dgn_ref, DST, bsc):
        @pl.when(pl.program_id(2) == 0)
        def _():
            DST[...] = jnp.zeros_like(DST)
            dlb_ref[...] = jnp.zeros_like(dlb_ref)
            dgn_ref[...] = jnp.zeros_like(dgn_ref)

        gn = gn_ref[...]
        causal = _iota((C, C), 0) >= _iota((C, C), 1)
        lastrow = _iota((C, DK), 0) == C - 1

        def chunk(i, carry):
            c = ncb - 1 - i
            r0 = pl.multiple_of(c * C, C)
            rows = pl.ds(r0, C)
            for hh in range(nh):
                ln = slice(hh * DK, (hh + 1) * DK)
                lbv = lb_ref[:, ln]
                qr = q_ref[0, rows, ln]
                fr = f_ref[0, rows, ln]
                q, k, logf, sig, nsig, fg = _hgrn_gates(qr, fr, lbv)
                v = v_ref[0, rows, ln]
                b = _cumsum_rows(logf)
                bsc[hh] = b
                bmid = bsc[hh, C // 2 - 1:C // 2, :]
                blast = bsc[hh, C - 1:C, :]
                qs, ks, eq, ek = _hgrn_scaled(q, k, b, bmid)
                A = jnp.where(causal, _dot(qs, ks, NT), 0.0)
                st_in = st_ref[0, hh, c]
                dst = DST[hh]
                eb = jnp.exp(b)
                ebl = jnp.exp(blast)
                ekb = jnp.exp(blast - b)
                qb = q * eb
                kb = k * ekb
                ov = o_ref[0, rows, ln]
                gr = g_ref[0, rows, ln]
                rms = lax.rsqrt(jnp.mean(ov * ov, axis=-1, keepdims=True) + NORM_EPS)
                oh = ov * rms
                sg = _silu(gr)
                don_ = do_ref[0, rows, ln]
                dgn_ref[0, hh] += jnp.sum(don_ * oh * sg, axis=0, keepdims=True)
                dp_ref[3, 0, rows, ln] = (don_ * oh * gn * _dsilu(gr)).astype(BF16)
                doh = don_ * gn * sg
                do_ = rms * (doh - oh * jnp.mean(doh * oh, axis=-1, keepdims=True))
                dA = jnp.where(causal, _dot(do_, v, NT), 0.0)
                dp_ref[2, 0, rows, ln] = (_dot(A, do_, TN) + _dot(kb, dst, NT)).astype(BF16)
                dqb = _dot(do_, st_in)
                dkb = _dot(v, dst)
                dq = _dot(dA, ks) * eq + dqb * eb
                dk_inter = dkb * ekb
                dk = _dot(dA, qs, TN) * ek + dk_inter
                db = q * dq - k * dk
                extra = jnp.sum(k * dk_inter, axis=0, keepdims=True) + ebl * jnp.sum(st_in * dst, axis=0, keepdims=True)
                db = db + jnp.where(lastrow, extra, 0.0)
                dlogf = _cumsum_rows(db, reverse=True)
                DST[hh] = dst * ebl + _dot(do_, qb, TN)
                dp_ref[0, 0, rows, ln] = (dq * _dsilu(qr)).astype(BF16)
                ss = sig * nsig
                dp_ref[1, 0, rows, ln] = ((1.0 - lbv) * ss * (dlogf / fg - dk)).astype(BF16)
                dlb_ref[0, :, ln] += jnp.sum(dlogf * nsig / fg - dk * nsig, axis=0, keepdims=True)
            return carry

        lax.fori_loop(0, ncb, chunk, 0)

    def col(off):
        return pl.BlockSpec((1, LB, WD), lambda b, hp, s: (b, nsb - 1 - s, off // nh + hp))

    outs = pl.pallas_call(
        body, name="hgrn_bwd", grid=(B, H // nh, nsb),
        in_specs=[col(0), col(H), col(2 * H), col(3 * H), col(0), col(0),
                  pl.BlockSpec((1, nh, ncb, DK, DK), lambda b, hp, s: (b, hp, nsb - 1 - s, 0, 0)),
                  pl.BlockSpec((1, WD), lambda b, hp, s: (0, hp)), pl.BlockSpec((1, DK), lambda b, hp, s: (0, 0))],
        out_specs=[pl.BlockSpec((4, 1, LB, WD), lambda b, hp, s: (0, b, nsb - 1 - s, hp)),
                   pl.BlockSpec((1, 1, WD), lambda b, hp, s: (b, 0, hp)),
                   pl.BlockSpec((1, nh, 1, DK), lambda b, hp, s: (b, hp, 0, 0))],
        out_shape=[jax.ShapeDtypeStruct((4, B, L, F_), BF16), jax.ShapeDtypeStruct((B, 1, F_), F32),
                   jax.ShapeDtypeStruct((B, H, 1, DK), F32)],
        scratch_shapes=[pltpu.VMEM((nh, DK, DK), F32), pltpu.VMEM((nh, C, DK), F32)],
        compiler_params=_cparams("parallel", "parallel", "arbitrary"))(proj, proj, proj, proj, o, don, st, lb, gnw)
    return outs


CONV_ROWS = 256
PAD_ROWS = 8


def _conv_taps(pad_ref, w_ref, r0, K, rb, forward=True):
    ext = pad_ref[pl.ds(r0, rb + PAD_ROWS), :]
    n = rb + PAD_ROWS
    acc = None
    for s in range(K):
        if forward:
            sh = ext if s == 0 else pltpu.roll(ext, s, 0)
            term = sh[PAD_ROWS:, :]
        else:
            sh = ext if s == 0 else pltpu.roll(ext, n - s, 0)
            term = sh[:rb, :]
        term = term * w_ref[K - 1 - s:K - s, :]
        acc = term if acc is None else acc + term
    return acc


def _conv_dw(ext, dc, K):
    row = _iota((8, dc.shape[1]), 0)
    out = jnp.zeros((8, dc.shape[1]), F32)
    for kk in range(K):
        s = K - 1 - kk
        sh = ext if s == 0 else pltpu.roll(ext, s, 0)
        out = out + jnp.where(row == kk, jnp.sum(dc * sh[PAD_ROWS:, :], axis=0, keepdims=True), 0.0)
    return out


def _mconv_fwd(zx, cw, cb, col0, width):
    B, L, _ = zx.shape
    K = cw.shape[0]
    ct = _pick_tile(width, 256)
    rb = min(CONV_ROWS, L)
    nrb = L // rb
    off = col0 // ct

    def body(x_ref, w_ref, b_ref, y_ref, xp):
        xp[0:PAD_ROWS, :] = jnp.zeros((PAD_ROWS, ct), F32)
        xp[PAD_ROWS:, :] = x_ref[0]
        bias = b_ref[...]

        def blk(i, carry):
            r0 = pl.multiple_of(i * rb, rb)
            y_ref[0, pl.ds(r0, rb), :] = _silu(_conv_taps(xp, w_ref, r0, K, rb) + bias)
            return carry

        lax.fori_loop(0, nrb, blk, 0)

    return pl.pallas_call(
        body, name="mconv_fwd", grid=(B, width // ct),
        in_specs=[pl.BlockSpec((1, L, ct), lambda b, j: (b, 0, off + j)), pl.BlockSpec((K, ct), lambda b, j: (0, j)),
                  pl.BlockSpec((1, ct), lambda b, j: (0, j))],
        out_specs=pl.BlockSpec((1, L, ct), lambda b, j: (b, 0, j)),
        out_shape=jax.ShapeDtypeStruct((B, L, width), F32),
        scratch_shapes=[pltpu.VMEM((L + PAD_ROWS, ct), F32)],
        compiler_params=_cparams("parallel", "parallel"))(zx, cw, cb.reshape(1, width))


def _mconv_bwd(zx, dya, cw, cb, col0, wcol0, name):
    B, L, _ = zx.shape
    K = cw.shape[0]
    npart, _, _, wq = dya.shape
    width = npart * wq
    ct = _pick_tile(wq, 256)
    rb = min(CONV_ROWS, L)
    nrb = L // rb
    off = (col0 + wcol0) // ct
    woff = wcol0 // ct
    pq = wq // ct

    def body(x_ref, dy_ref, w_ref, b_ref, dx_ref, dw_ref, db_ref, xp, dcp):
        xp[0:PAD_ROWS, :] = jnp.zeros((PAD_ROWS, ct), F32)
        xp[PAD_ROWS:, :] = x_ref[0]
        dcp[L:, :] = jnp.zeros((PAD_ROWS, ct), F32)
        bias = b_ref[...]

        def blk1(i, carry):
            dw, db = carry
            r0 = pl.multiple_of(i * rb, rb)
            cpre = _conv_taps(xp, w_ref, r0, K, rb) + bias
            dc = dy_ref[0, 0, pl.ds(r0, rb), :] * _dsilu(cpre)
            dcp[pl.ds(r0, rb), :] = dc
            ext = xp[pl.ds(r0, rb + PAD_ROWS), :]
            return dw + _conv_dw(ext, dc, K), db + jnp.sum(dc, axis=0, keepdims=True)

        dw, db = lax.fori_loop(0, nrb, blk1, (jnp.zeros((8, ct), F32), jnp.zeros((1, ct), F32)))
        dw_ref[0] = dw
        db_ref[0] = db

        def blk2(i, carry):
            r0 = pl.multiple_of(i * rb, rb)
            dx_ref[0, pl.ds(r0, rb), :] = _conv_taps(dcp, w_ref, r0, K, rb, forward=False).astype(BF16)
            return carry

        lax.fori_loop(0, nrb, blk2, 0)

    dx, dw, db = pl.pallas_call(
        body, name=name, grid=(B, width // ct),
        in_specs=[pl.BlockSpec((1, L, ct), lambda b, j: (b, 0, off + j)),
                  pl.BlockSpec((1, 1, L, ct), lambda b, j: (j // pq, b, 0, j % pq)),
                  pl.BlockSpec((K, ct), lambda b, j: (0, woff + j)), pl.BlockSpec((1, ct), lambda b, j: (0, woff + j))],
        out_specs=[pl.BlockSpec((1, L, ct), lambda b, j: (b, 0, j)), pl.BlockSpec((1, 8, ct), lambda b, j: (b, 0, j)),
                   pl.BlockSpec((1, 1, ct), lambda b, j: (b, 0, j))],
        out_shape=[jax.ShapeDtypeStruct((B, L, width), BF16), jax.ShapeDtypeStruct((B, 8, width), F32),
                   jax.ShapeDtypeStruct((B, 1, width), F32)],
        scratch_shapes=[pltpu.VMEM((L + PAD_ROWS, ct), F32), pltpu.VMEM((L + PAD_ROWS, ct), F32)],
        compiler_params=_cparams("parallel", "parallel"))(zx, dya, cw, cb.reshape(1, -1))
    return dx, dw[:, :K, :], db


def _ffn_mid_fwd(up, cw, cb, dff):
    B, L, _ = up.shape
    K = cw.shape[0]
    ct = _pick_tile(dff, 256)
    rb = min(CONV_ROWS, L)
    nrb = L // rb
    half = dff // ct

    def body(g_ref, u_ref, wg_ref, wu_ref, bg_ref, bu_ref, a_ref, gp, upad):
        gp[0:PAD_ROWS, :] = jnp.zeros((PAD_ROWS, ct), F32)
        upad[0:PAD_ROWS, :] = jnp.zeros((PAD_ROWS, ct), F32)
        gp[PAD_ROWS:, :] = g_ref[0]
        upad[PAD_ROWS:, :] = u_ref[0]
        bg, bu = bg_ref[...], bu_ref[...]

        def blk(i, carry):
            r0 = pl.multiple_of(i * rb, rb)
            cg = _conv_taps(gp, wg_ref, r0, K, rb) + bg
            cu = _conv_taps(upad, wu_ref, r0, K, rb) + bu
            a_ref[0, pl.ds(r0, rb), :] = (_silu(cg) * cu).astype(BF16)
            return carry

        lax.fori_loop(0, nrb, blk, 0)

    xg = pl.BlockSpec((1, L, ct), lambda b, j: (b, 0, j))
    xu = pl.BlockSpec((1, L, ct), lambda b, j: (b, 0, half + j))
    wgs = pl.BlockSpec((K, ct), lambda b, j: (0, j))
    wus = pl.BlockSpec((K, ct), lambda b, j: (0, half + j))
    bgs = pl.BlockSpec((1, ct), lambda b, j: (0, j))
    bus = pl.BlockSpec((1, ct), lambda b, j: (0, half + j))
    cb2 = cb.reshape(1, 2 * dff)
    return pl.pallas_call(
        body, name="ffn_mid_fwd", grid=(B, half), in_specs=[xg, xu, wgs, wus, bgs, bus], out_specs=xg,
        out_shape=jax.ShapeDtypeStruct((B, L, dff), BF16),
        scratch_shapes=[pltpu.VMEM((L + PAD_ROWS, ct), F32), pltpu.VMEM((L + PAD_ROWS, ct), F32)],
        compiler_params=_cparams("parallel", "parallel"))(up, up, cw, cw, cb2, cb2)


def _ffn_mid_bwd(up, dact, cw, cb, dff):
    B, L, _ = up.shape
    K = cw.shape[0]
    ct = _pick_tile(dff, 256)
    rb = min(CONV_ROWS, L)
    nrb = L // rb
    half = dff // ct

    def body(g_ref, u_ref, da_ref, wg_ref, wu_ref, bg_ref, bu_ref, dx_ref, dwg_ref, dwu_ref, dbg_ref, dbu_ref,
             gp, upad, dgp, dup):
        gp[0:PAD_ROWS, :] = jnp.zeros((PAD_ROWS, ct), F32)
        upad[0:PAD_ROWS, :] = jnp.zeros((PAD_ROWS, ct), F32)
        gp[PAD_ROWS:, :] = g_ref[0]
        upad[PAD_ROWS:, :] = u_ref[0]
        dgp[L:, :] = jnp.zeros((PAD_ROWS, ct), F32)
        dup[L:, :] = jnp.zeros((PAD_ROWS, ct), F32)
        bg, bu = bg_ref[...], bu_ref[...]

        def blk1(i, carry):
            dwg, dwu, dbg, dbu = carry
            r0 = pl.multiple_of(i * rb, rb)
            cg = _conv_taps(gp, wg_ref, r0, K, rb) + bg
            cu = _conv_taps(upad, wu_ref, r0, K, rb) + bu
            da = da_ref[0, pl.ds(r0, rb), :]
            dcg = da * cu * _dsilu(cg)
            dcu = da * _silu(cg)
            dgp[pl.ds(r0, rb), :] = dcg
            dup[pl.ds(r0, rb), :] = dcu
            eg = gp[pl.ds(r0, rb + PAD_ROWS), :]
            eu = upad[pl.ds(r0, rb + PAD_ROWS), :]
            return (dwg + _conv_dw(eg, dcg, K), dwu + _conv_dw(eu, dcu, K), dbg + jnp.sum(dcg, axis=0, keepdims=True),
                    dbu + jnp.sum(dcu, axis=0, keepdims=True))

        z8 = jnp.zeros((8, ct), F32)
        z1 = jnp.zeros((1, ct), F32)
        dwg, dwu, dbg, dbu = lax.fori_loop(0, nrb, blk1, (z8, z8, z1, z1))
        dwg_ref[0] = dwg
        dwu_ref[0] = dwu
        dbg_ref[0] = dbg
        dbu_ref[0] = dbu

        def blk2(i, carry):
            r0 = pl.multiple_of(i * rb, rb)
            dx_ref[0, 0, pl.ds(r0, rb), :] = _conv_taps(dgp, wg_ref, r0, K, rb, forward=False).astype(BF16)
            dx_ref[1, 0, pl.ds(r0, rb), :] = _conv_taps(dup, wu_ref, r0, K, rb, forward=False).astype(BF16)
            return carry

        lax.fori_loop(0, nrb, blk2, 0)

    xg = pl.BlockSpec((1, L, ct), lambda b, j: (b, 0, j))
    xu = pl.BlockSpec((1, L, ct), lambda b, j: (b, 0, half + j))
    wgs = pl.BlockSpec((K, ct), lambda b, j: (0, j))
    wus = pl.BlockSpec((K, ct), lambda b, j: (0, half + j))
    bgs = pl.BlockSpec((1, ct), lambda b, j: (0, j))
    bus = pl.BlockSpec((1, ct), lambda b, j: (0, half + j))
    w8 = pl.BlockSpec((1, 8, ct), lambda b, j: (b, 0, j))
    b1 = pl.BlockSpec((1, 1, ct), lambda b, j: (b, 0, j))
    cb2 = cb.reshape(1, 2 * dff)
    pad = pltpu.VMEM((L + PAD_ROWS, ct), F32)
    dx2, dwg, dwu, dbg, dbu = pl.pallas_call(
        body, name="ffn_mid_bwd", grid=(B, half), in_specs=[xg, xu, xg, wgs, wus, bgs, bus],
        out_specs=[pl.BlockSpec((2, 1, L, ct), lambda b, j: (0, b, 0, j)), w8, w8, b1, b1],
        out_shape=[jax.ShapeDtypeStruct((2, B, L, dff), BF16)] + [jax.ShapeDtypeStruct((B, 8, dff), F32)] * 2
        + [jax.ShapeDtypeStruct((B, 1, dff), F32)] * 2,
        scratch_shapes=[pad, pad, pad, pad],
        compiler_params=_cparams("parallel", "parallel"))(up, up, dact, cw, cw, cb2, cb2)
    dw = jnp.concatenate([dwg[:, :K], dwu[:, :K]], axis=-1)
    db = jnp.concatenate([dbg, dbu], axis=-1)
    return dx2, dw, db


def _ssd_consts(hpg, W):
    P = M_HEADDIM
    E = (_iota((LANES, W), 0) == _iota((LANES, W), 1) // P).astype(BF16)
    Ebig = (_iota((LANES, hpg * LANES), 0) == _iota((LANES, hpg * LANES), 1) // LANES).astype(BF16)
    causal = _iota((M_CHUNK, M_CHUNK), 0) >= _iota((M_CHUNK, M_CHUNK), 1)
    head_of_lane = _iota((1, W), 1) // P
    return E, Ebig, causal, head_of_lane


def _ssd_chunk_fwd(xs, Bm, Cm, dtr, bias, Aneg, E, Ebig, causal, head_of_lane, hpg, st, ar_sc, ae_sc):
    pre = dtr + bias
    dt = jnp.maximum(pre, 0.0) + jnp.log(1.0 + jnp.exp(-jnp.abs(pre)))
    Ad = dt * Aneg
    a_c = _cumsum_rows(Ad)
    ar_sc[...] = a_c.T
    aexp = _dot_exact(a_c, E)
    ae_sc[...] = aexp
    alast = ae_sc[M_CHUNK - 1:M_CHUNK, :]
    dtexp = _dot_exact(dt, E)
    X = xs * dtexp
    AC = _dot_exact(a_c, Ebig)
    CB = _dot(Cm, Bm, NT)
    Xb = X.astype(BF16)
    Ls = [jnp.where(causal, jnp.exp(jnp.minimum(AC[:, j * LANES:(j + 1) * LANES] - ar_sc[j:j + 1, :], 0.0)), 0.0)
          for j in range(hpg)]
    first = _iota((1, LANES), 1) < M_HEADDIM
    pairs = []
    for p in range(hpg // 2):
        Xp = Xb[:, p * LANES:(p + 1) * LANES]
        pairs.append(jnp.where(first, _dot(CB * Ls[2 * p], Xp), _dot(CB * Ls[2 * p + 1], Xp)))
    ydiag = pairs[0] if len(pairs) == 1 else jnp.concatenate(pairs, axis=1)
    ea = jnp.exp(aexp)
    yoff = ea * _dot(Cm, st)
    dec = jnp.exp(alast - aexp)
    return dict(dt=dt, a_c=a_c, aexp=aexp, alast=alast, dtexp=dtexp, X=X, Xb=Xb, CB=CB, Ls=Ls, ydiag=ydiag, ea=ea,
                yoff=yoff, dec=dec)


def _ssd_fwd(xbca, zx, dtc, bias, Aneg, Dexp, nw, hpg):
    B, L, _ = xbca.shape
    G, N, C = M_GROUPS, M_D_STATE, M_CHUNK
    W = hpg * M_HEADDIM
    DI = G * W
    NC = L // C
    LB = min(L, 4 * C)
    ncb = LB // C

    def body(xs_ref, b_ref, c_ref, z_ref, dt_ref, bias_ref, a_ref, d_ref, nw_ref, y_ref, yn_ref, st_ref, ST, ar_sc, ae_sc):
        @pl.when(pl.program_id(2) == 0)
        def _():
            ST[...] = jnp.zeros_like(ST)

        E, Ebig, causal, head_of_lane = _ssd_consts(hpg, W)
        bias_ = bias_ref[0]
        Aneg_ = a_ref[0]
        Dv = d_ref[...]
        nwv = nw_ref[...]

        def chunk(ci, carry):
            r0 = pl.multiple_of(ci * C, C)
            rows = pl.ds(r0, C)
            xs = xs_ref[0, rows, :]
            Bm = b_ref[0, rows, :]
            Cm = c_ref[0, rows, :]
            st = ST[...]
            st_ref[0, 0, ci] = st
            f = _ssd_chunk_fwd(xs, Bm, Cm, dt_ref[0, 0, ci], bias_, Aneg_, E, Ebig, causal, head_of_lane, hpg, st, ar_sc, ae_sc)
            y = f["ydiag"] + f["yoff"] + xs * Dv
            ST[...] = st * jnp.exp(f["alast"]) + _dot(Bm, f["X"] * f["dec"], TN)
            yg = y * _silu(z_ref[0, rows, :])
            rstd = lax.rsqrt(jnp.mean(yg * yg, axis=-1, keepdims=True) + NORM_EPS)
            y_ref[0, rows, :] = y
            yn_ref[0, rows, :] = (yg * rstd * nwv).astype(BF16)
            return carry

        lax.fori_loop(0, ncb, chunk, 0)

    xw = pl.BlockSpec((1, LB, W), lambda b, g, s: (b, s, g))
    bsp = pl.BlockSpec((1, LB, N), lambda b, g, s: (b, s, DI // N + g))
    csp = pl.BlockSpec((1, LB, N), lambda b, g, s: (b, s, DI // N + G + g))
    dts = pl.BlockSpec((1, 1, ncb, C, LANES), lambda b, g, s: (b, g, s, 0, 0))
    hv = pl.BlockSpec((1, 1, LANES), lambda b, g, s: (g, 0, 0))
    wv = pl.BlockSpec((1, W), lambda b, g, s: (0, g))
    sts = pl.BlockSpec((1, 1, ncb, N, W), lambda b, g, s: (b, g, s, 0, 0))
    return pl.pallas_call(
        body, name="ssd_fwd", grid=(B, G, L // LB), in_specs=[xw, bsp, csp, xw, dts, hv, hv, wv, wv],
        out_specs=[xw, xw, sts],
        out_shape=[jax.ShapeDtypeStruct((B, L, DI), F32), jax.ShapeDtypeStruct((B, L, DI), BF16),
                   jax.ShapeDtypeStruct((B, G, NC, N, W), F32)],
        scratch_shapes=[pltpu.VMEM((N, W), F32), pltpu.VMEM((LANES, C), F32), pltpu.VMEM((C, W), F32)],
        compiler_params=_cparams("parallel", "parallel", "arbitrary"))(xbca, xbca, xbca, zx, dtc, bias, Aneg, Dexp, nw)


def _ssd_bwd(xbca, zx, dtc, ypre, dyn, st, bias, Aneg, Dexp, nw, hpg):
    B, L, _ = xbca.shape
    G, N, C = M_GROUPS, M_D_STATE, M_CHUNK
    W = hpg * M_HEADDIM
    DI = G * W
    NC = L // C
    LB = min(L, 4 * C)
    ncb = LB // C
    nsb = L // LB

    def body(xs_ref, b_ref, c_ref, z_ref, dt_ref, y_ref, dyn_ref, st_ref, bias_ref, a_ref, d_ref, nw_ref,
             dxs_ref, dbc_ref, dz_ref, ddt_ref, dnw_ref, dd_ref, da_ref, dbias_ref, DST, ar_sc, ae_sc):
        @pl.when(pl.program_id(2) == 0)
        def _():
            DST[...] = jnp.zeros_like(DST)
            dnw_ref[...] = jnp.zeros_like(dnw_ref)
            dd_ref[...] = jnp.zeros_like(dd_ref)
            da_ref[...] = jnp.zeros_like(da_ref)
            dbias_ref[...] = jnp.zeros_like(dbias_ref)

        E, Ebig, causal, head_of_lane = _ssd_consts(hpg, W)
        bias_ = bias_ref[0]
        Aneg_ = a_ref[0]
        Dv = d_ref[...]
        nwv = nw_ref[...]
        lane = _iota((1, LANES), 1)
        subl = _iota((LANES, 1), 0)
        lastrow = _iota((C, W), 0) == C - 1

        def chunk(i, carry):
            ci = ncb - 1 - i
            r0 = pl.multiple_of(ci * C, C)
            rows = pl.ds(r0, C)
            xs = xs_ref[0, rows, :]
            Bm = b_ref[0, rows, :]
            Cm = c_ref[0, rows, :]
            zr = z_ref[0, rows, :]
            dtr = dt_ref[0, 0, ci]
            st_in = st_ref[0, 0, ci]
            dst = DST[...]
            f = _ssd_chunk_fwd(xs, Bm, Cm, dtr, bias_, Aneg_, E, Ebig, causal, head_of_lane, hpg, st_in, ar_sc, ae_sc)
            X, Xb, dec, ea, CB = f["X"], f["Xb"], f["dec"], f["ea"], f["CB"]
            y = y_ref[0, rows, :]
            sz = _silu(zr)
            yg = y * sz
            rstd = lax.rsqrt(jnp.mean(yg * yg, axis=-1, keepdims=True) + NORM_EPS)
            yh = yg * rstd
            dyn_ = dyn_ref[0, rows, :]
            dnw_ref[0, 0] += jnp.sum(dyn_ * yh, axis=0, keepdims=True)
            dyh = dyn_ * nwv
            dyg = rstd * (dyh - yh * jnp.mean(dyh * yh, axis=-1, keepdims=True))
            dz_ref[0, rows, :] = (dyg * y * _dsilu(zr)).astype(BF16)
            dy = dyg * sz
            dd_ref[0, 0] += jnp.sum(dy * xs, axis=0, keepdims=True)
            dxs = dy * Dv
            dYo = dy * ea
            daexp = dy * f["yoff"]
            dCm = _dot(dYo, st_in, NT)
            dst_in = _dot(Cm, dYo, TN)
            dyb = dy.astype(BF16)
            dCB = jnp.zeros((C, C), F32)
            da_col = jnp.zeros((C, LANES), F32)
            da_row = jnp.zeros((LANES, C), F32)
            first = lane < M_HEADDIM
            dXs = []
            for p in range(hpg // 2):
                Xp = Xb[:, p * LANES:(p + 1) * LANES]
                dYp = dyb[:, p * LANES:(p + 1) * LANES]
                dXp = None
                for j in (2 * p, 2 * p + 1):
                    Lj = f["Ls"][j]
                    Gj = CB * Lj
                    dYj = jnp.where(first if j % 2 == 0 else jnp.logical_not(first), dYp, jnp.zeros_like(dYp))
                    t = _dot(Gj, dYj, TN)
                    dXp = t if dXp is None else dXp + t
                    dGj = _dot(dYj, Xp, NT)
                    dCB = dCB + dGj * Lj
                    Wj = dGj * Gj
                    da_col = da_col + jnp.sum(Wj, axis=1, keepdims=True) * (lane == j).astype(F32)
                    da_row = da_row + (subl == j).astype(F32) * jnp.sum(Wj, axis=0, keepdims=True)
                dXs.append(dXp)
            dX = dXs[0] if len(dXs) == 1 else jnp.concatenate(dXs, axis=1)
            dCm = dCm + _dot(dCB, Bm)
            dBm = _dot(dCB, Cm, TN)
            ela = jnp.exp(f["alast"])
            dalast = jnp.sum(dst * st_in, axis=0, keepdims=True) * ela
            DST[...] = dst * ela + dst_in
            dXd = _dot(Bm, dst)
            dBm = dBm + _dot(X * dec, dst, NT)
            dX = dX + dXd * dec
            ddec = dXd * X * dec
            dalast = dalast + jnp.sum(ddec, axis=0, keepdims=True)
            daexp = daexp - ddec + jnp.where(lastrow, dalast, 0.0)
            dxs = dxs + dX * f["dtexp"]
            ddtexp = dX * xs
            ddt = _dot_exact(ddtexp, E, NT, passes=2)
            da_c = _dot_exact(daexp, E, NT, passes=2) + da_col - da_row.T
            dAd = _cumsum_rows(da_c, reverse=True)
            ddt = ddt + dAd * Aneg_
            da_ref[0, 0] += jnp.sum(dAd * f["dt"], axis=0, keepdims=True) * Aneg_
            ddtr = ddt * jax.nn.sigmoid(dtr + bias_)
            dbias_ref[0, 0] += jnp.sum(ddtr, axis=0, keepdims=True)
            ddt_ref[0, 0, ci] = ddtr
            dxs_ref[0, rows, :] = dxs
            dbc_ref[0, 0, rows, :] = dBm
            dbc_ref[1, 0, rows, :] = dCm
            return carry

        lax.fori_loop(0, ncb, chunk, 0)

    def rev(s):
        return nsb - 1 - s

    xw = pl.BlockSpec((1, LB, W), lambda b, g, s: (b, rev(s), g))
    bsp = pl.BlockSpec((1, LB, N), lambda b, g, s: (b, rev(s), DI // N + g))
    csp = pl.BlockSpec((1, LB, N), lambda b, g, s: (b, rev(s), DI // N + G + g))
    gsp = pl.BlockSpec((1, LB, N), lambda b, g, s: (b, rev(s), g))
    dts = pl.BlockSpec((1, 1, ncb, C, LANES), lambda b, g, s: (b, g, rev(s), 0, 0))
    hv = pl.BlockSpec((1, 1, LANES), lambda b, g, s: (g, 0, 0))
    wv = pl.BlockSpec((1, W), lambda b, g, s: (0, g))
    sts = pl.BlockSpec((1, 1, ncb, N, W), lambda b, g, s: (b, g, rev(s), 0, 0))
    accw = pl.BlockSpec((1, 1, 1, W), lambda b, g, s: (b, g, 0, 0))
    acch = pl.BlockSpec((1, 1, 1, LANES), lambda b, g, s: (b, g, 0, 0))
    return pl.pallas_call(
        body, name="ssd_bwd", grid=(B, G, nsb), in_specs=[xw, bsp, csp, xw, dts, xw, xw, sts, hv, hv, wv, wv],
        out_specs=[xw, pl.BlockSpec((2, 1, LB, N), lambda b, g, s: (0, b, rev(s), g)), xw, dts, accw, accw, acch, acch],
        out_shape=[jax.ShapeDtypeStruct((B, L, DI), F32), jax.ShapeDtypeStruct((2, B, L, G * N), F32),
                   jax.ShapeDtypeStruct((B, L, DI), BF16),
                   jax.ShapeDtypeStruct((B, G, NC, C, LANES), F32), jax.ShapeDtypeStruct((B, G, 1, W), F32),
                   jax.ShapeDtypeStruct((B, G, 1, W), F32), jax.ShapeDtypeStruct((B, G, 1, LANES), F32),
                   jax.ShapeDtypeStruct((B, G, 1, LANES), F32)],
        scratch_shapes=[pltpu.VMEM((N, W), F32), pltpu.VMEM((LANES, C), F32), pltpu.VMEM((C, W), F32)],
        compiler_params=_cparams("parallel", "parallel", "arbitrary"))(
            xbca, xbca, xbca, zx, dtc, ypre, dyn, st, bias, Aneg, Dexp, nw)


def _adamw(w, g, m, v, name):
    shape = w.shape
    n = w.size
    cols = shape[-1]
    rows = n // cols
    tr = rows
    for cand in (512, 256, 128, 64, 32, 16, 8):
        if rows % cand == 0 and cand * cols * 4 <= 1024 * 1024:
            tr = cand
            break
    c1 = 1.0 / (1.0 - ADAM_B1 ** ADAM_STEP)
    c2 = 1.0 / (1.0 - ADAM_B2 ** ADAM_STEP)

    def body(w_ref, g_ref, m_ref, v_ref, d_ref, mo_ref, vo_ref):
        g_ = g_ref[...]
        mn = ADAM_B1 * m_ref[...] + (1.0 - ADAM_B1) * g_
        vn = ADAM_B2 * v_ref[...] + (1.0 - ADAM_B2) * (g_ * g_)
        d_ref[...] = -ADAM_LR * ((mn * c1) / (jnp.sqrt(vn * c2) + ADAM_EPS) + ADAM_WD * w_ref[...])
        mo_ref[...] = mn
        vo_ref[...] = vn

    spec = pl.BlockSpec((tr, cols), lambda i: (i, 0))
    r2 = lambda a: a.reshape(rows, cols)
    outs = pl.pallas_call(
        body, name=name, grid=(rows // tr,), in_specs=[spec] * 4, out_specs=[spec] * 3,
        out_shape=[jax.ShapeDtypeStruct((rows, cols), F32)] * 3,
        compiler_params=_cparams("parallel"))(r2(w), r2(g), r2(m), r2(v))
    return tuple(o.reshape(shape) for o in outs)


def _lower_bounds(lb_logits):
    p = jax.nn.softmax(lb_logits.astype(F32), axis=0)
    return jnp.cumsum(p, axis=0) - p[0]


def _pad_cols(a, n):
    return a if a.shape[-1] == n else jnp.pad(a, [(0, 0)] * (a.ndim - 1) + [(0, n - a.shape[-1])])


def _heads_to_lanes(a, G, hpg):
    return _pad_cols(a.reshape(G, 1, hpg), LANES)


def _local_step(x, target, P, fetch, emit):
    B, L, D = x.shape
    T = B * L
    depth = P["mix_norm"].shape[0]
    H = D // HGRN_DK
    F_ = H * HGRN_DK
    DI = P["m_norm"].shape[1]
    G, N = M_GROUPS, M_D_STATE
    MH = DI // M_HEADDIM
    hpg = MH // G
    assert hpg <= 8
    W = hpg * M_HEADDIM
    CD = DI + 2 * G * N
    MIN = DI + CD + MH
    MPAD = -(-MIN // LANES) * LANES
    dff = P["f_conv_b"].shape[1] // 2
    NC = L // M_CHUNK
    lbs = _lower_bounds(P["hgrn_lb_logits"])

    h = x.reshape(T, D)
    saved = []
    for i in range(depth):
        j = i // 2
        Wl = dict(fetch(i, ("mix_in", "mix_out"), h))
        s = {"h_in": h, "W": Wl}
        u = _rmsnorm_fwd(h, P["mix_norm"][i], "mix_norm_fwd")
        s["u"] = u
        if i % 2 == 0:
            proj = _matmul(u, Wl["mix_in"], name="hgrn_in_fwd").reshape(B, L, 4 * F_)
            o, on, st = _hgrn_fwd(proj, lbs[j].reshape(1, F_), P["hgrn_gnorm"][j].reshape(1, HGRN_DK), H)
            h = _matmul(on.reshape(T, F_), Wl["mix_out"], res=h, name="hgrn_out_fwd")
            s.update(proj=proj, o=o, on=on, st=st)
        else:
            zx = _matmul(u, Wl["mix_in"], tb=True, tn=1152, name="m_in_fwd").reshape(B, L, MPAD)
            xbca = _mconv_fwd(zx, P["m_conv_w"][j], P["m_conv_b"][j], DI, CD)
            dtr = zx[:, :, DI + CD:DI + CD + MH].reshape(B, NC, M_CHUNK, G, hpg).transpose(0, 3, 1, 2, 4)
            dtc = _pad_cols(dtr, LANES)
            bias = _heads_to_lanes(P["m_dt_bias"][j], G, hpg)
            Aneg = _heads_to_lanes(-jnp.exp(P["m_A_log"][j]), G, hpg)
            Dexp = jnp.repeat(P["m_D"][j], M_HEADDIM).reshape(1, DI)
            nw = P["m_norm"][j].reshape(1, DI)
            ypre, yn, st = _ssd_fwd(xbca, zx, dtc, bias, Aneg, Dexp, nw, hpg)
            h = _matmul(yn.reshape(T, DI), Wl["mix_out"], res=h, name="m_out_fwd")
            s.update(zx=zx, xbca=xbca, dtc=dtc, bias=bias, Aneg=Aneg, Dexp=Dexp, nw=nw, ypre=ypre, yn=yn, st=st)
        s["h_mid"] = h
        u2 = _rmsnorm_fwd(h, P["ffn_norm"][i], "ffn_norm_fwd")
        Wl.update(fetch(i, ("f_w_up", "f_w_down"), h))
        up = _matmul(u2, Wl["f_w_up"], name="ffn_up_fwd").reshape(B, L, 2 * dff)
        act = _ffn_mid_fwd(up, P["f_conv_w"][i], P["f_conv_b"][i], dff)
        h = _matmul(act.reshape(T, dff), Wl["f_w_down"], res=h, name="ffn_down_fwd")
        s.update(u2=u2, up=up, act=act)
        saved.append(s)

    loss, dh, dhb, d_final = _loss_head(h, P["final_norm"], target.reshape(T, D))

    g = {k: [None] * P[k].shape[0] for k in ("mix_norm", "ffn_norm", "hgrn_gnorm", "m_conv_w", "m_conv_b", "m_dt_bias",
                                              "m_A_log", "m_D", "m_norm", "f_conv_w", "f_conv_b")}
    dlbs = [None] * lbs.shape[0]
    for i in reversed(range(depth)):
        j = i // 2
        s = saved[i]
        Wl = s["W"]
        gm = {}

        def dw(key, a, b, name, **kw):
            gm[key] = _matmul(a, b, ta=True, out_dtype=BF16, name=name, **kw)

        dact = _matmul(dhb, Wl["f_w_down"], tb=True, name="ffn_down_dx").reshape(B, L, dff)
        dw("f_w_down", s["act"].reshape(T, dff), dhb, "ffn_down_dw")
        dup, dcw, dcb = _ffn_mid_bwd(s["up"], dact, P["f_conv_w"][i], P["f_conv_b"][i], dff)
        g["f_conv_w"][i] = jnp.sum(dcw, axis=0)
        g["f_conv_b"][i] = jnp.sum(dcb, axis=(0, 1))
        dup = dup.reshape(2, T, dff)
        dw("f_w_up", s["u2"], dup, "ffn_up_dw", b_parts=True)
        dep = emit(i, {key: gm[key] for key in ("f_w_up", "f_w_down")})
        du2 = _matmul(dup, Wl["f_w_up"], a_parts=True, tb=True, name="ffn_up_dx")
        dh, dhb, g["ffn_norm"][i] = _rmsnorm_bwd(s["h_mid"], P["ffn_norm"][i], du2, dh, "ffn_norm_bwd", dep=dep)
        if i % 2 == 0:
            don = _matmul(dhb, Wl["mix_out"], tb=True, name="hgrn_out_dx").reshape(B, L, F_)
            dw("mix_out", s["on"].reshape(T, F_), dhb, "hgrn_out_dw")
            dproj, dlb, dgn = _hgrn_bwd(s["proj"], s["o"], don, s["st"], lbs[j].reshape(1, F_),
                                        P["hgrn_gnorm"][j].reshape(1, HGRN_DK), H)
            dlbs[j] = jnp.sum(dlb, axis=(0, 1))
            g["hgrn_gnorm"][j] = jnp.sum(dgn, axis=(0, 1, 2))
            dproj = dproj.reshape(4, T, F_)
            dw("mix_in", s["u"], dproj, "hgrn_in_dw", b_parts=True)
            dep = emit(i, {key: gm[key] for key in ("mix_in", "mix_out")})
            du = _matmul(dproj, Wl["mix_in"], a_parts=True, tb=True, name="hgrn_in_dx")
        else:
            dyn = _matmul(dhb, Wl["mix_out"], tb=True, name="m_out_dx").reshape(B, L, DI)
            dw("mix_out", s["yn"].reshape(T, DI), dhb, "m_out_dw")
            dxs, dbc, dz, ddt, dnw, dD, dA, dbias = _ssd_bwd(s["xbca"], s["zx"], s["dtc"], s["ypre"], dyn, s["st"],
                                                             s["bias"], s["Aneg"], s["Dexp"], s["nw"], hpg)
            g["m_norm"][j] = jnp.sum(dnw, axis=(0, 2)).reshape(DI)
            g["m_D"][j] = jnp.sum(dD, axis=(0, 2)).reshape(MH, M_HEADDIM).sum(axis=-1)
            g["m_A_log"][j] = jnp.sum(dA, axis=(0, 2))[:, :hpg].reshape(MH)
            g["m_dt_bias"][j] = jnp.sum(dbias, axis=(0, 2))[:, :hpg].reshape(MH)
            cw, cb = P["m_conv_w"][j], P["m_conv_b"][j]
            dxx, dcw_x, dcb_x = _mconv_bwd(s["zx"], dxs[None], cw, cb, DI, 0, "mconv_bwd_x")
            dxb, dcw_b, dcb_b = _mconv_bwd(s["zx"], dbc, cw, cb, DI, DI, "mconv_bwd_bc")
            g["m_conv_w"][j] = jnp.concatenate([jnp.sum(dcw_x, axis=0), jnp.sum(dcw_b, axis=0)], axis=-1)
            g["m_conv_b"][j] = jnp.concatenate([jnp.sum(dcb_x, axis=(0, 1)), jnp.sum(dcb_b, axis=(0, 1))], axis=-1)
            ddt_t = _pad_cols(ddt[..., :hpg].transpose(0, 2, 3, 1, 4).reshape(T, MH), MPAD - DI - CD).astype(BF16)
            pieces = [(dz.reshape(T, DI), 0), (dxx.reshape(T, DI), DI), (dxb.reshape(T, 2 * G * N), 2 * DI), (ddt_t, DI + CD)]
            gm["mix_in"] = lax.empty((MPAD, D), BF16)
            for n_, (piece, off) in enumerate(pieces):
                gm["mix_in"] = _matmul(piece, s["u"], ta=True, out_dtype=BF16, out=gm["mix_in"], out_off=off,
                                       name="m_in_dw%d" % n_)
            dep = emit(i, {key: gm[key] for key in ("mix_in", "mix_out")})
            du = None
            for n_, (piece, off) in enumerate(pieces):
                du = _matmul(piece, Wl["mix_in"], b_off=off, res=du, name="m_in_dx%d" % n_)
        dh, dhb, g["mix_norm"][i] = _rmsnorm_bwd(s["h_in"], P["mix_norm"][i], du, dh, "mix_norm_bwd", dep=dep)

    grads = {k: jnp.stack(vs) for k, vs in g.items()}
    grads["final_norm"] = d_final
    _, lb_vjp = jax.vjp(_lower_bounds, P["hgrn_lb_logits"])
    grads["hgrn_lb_logits"] = lb_vjp(jnp.stack(dlbs))[0]
    return loss, dh.reshape(B, L, D), grads


ANY = pl.BlockSpec(memory_space=pl.ANY)
N_CHIPS = 4
N_DEV = 8


def _place():
    x, y, c = lax.axis_index("x"), lax.axis_index("y"), lax.axis_index("c")
    sibling = (x, y, 1 - c)
    chips = [(1 - x, y), (x, 1 - y), (1 - x, 1 - y)]
    return x, y, c, sibling, chips


def _remote(src, dst, send_sem, recv_sem, to):
    return pltpu.make_async_remote_copy(src_ref=src, dst_ref=dst, send_sem=send_sem, recv_sem=recv_sem, device_id=to,
                                        device_id_type=MESH)


KIND_AXIS = {"hgrn_w_in": "col", "f_w_up": "col", "m_w_in_t": "row", "hgrn_w_out": "row", "m_w_out": "row", "f_w_down": "row"}
KINDS = tuple(KIND_AXIS)
PEER_MASKS = (2, 1, 3)
ALL = slice(None)


def _chip_win(axis, cw, s):
    return (ALL, slice(s * cw, (s + 1) * cw)) if axis == "col" else (slice(s * cw, (s + 1) * cw), ALL)


def _half_win(axis, rows, cols, h):
    return (slice(h * rows // 2, (h + 1) * rows // 2), ALL) if axis == "col" else (ALL, slice(h * cols // 2, (h + 1) * cols // 2))


def _per_place(fn):
    x, y, c, sibling, chips = _place()
    chip = 2 * x + y
    for s in range(N_CHIPS):
        for cc in range(2):
            @pl.when(jnp.logical_and(chip == s, c == cc))
            def _():
                fn(s, cc, c, sibling, chips)


HBM = pl.BlockSpec(memory_space=pltpu.HBM)
SEM = pl.BlockSpec(memory_space=pltpu.SEMAPHORE)
EFFECT = pltpu.SideEffectType.DATAFLOW_SIDE_EFFECTING


def _cell(axis, rows, cols, cw, s, h):
    if axis == "col":
        return (slice(h * rows // 2, (h + 1) * rows // 2), slice(s * cw, (s + 1) * cw))
    return (slice(s * cw, (s + 1) * cw), slice(h * cols // 2, (h + 1) * cols // 2))


def _in_hbm(a):
    return pltpu.with_memory_space_constraint(a, pltpu.HBM)


def _stage_shard(kind, shard, layer, chip, pad_rows=0, dep=None):
    _, R, C = shard.shape
    axis = KIND_AXIS[kind]
    tr, tc = _row_tile(R), _pick_tile(C, 2048)
    nr, nc = R // tr, C // tc
    full = (R, N_CHIPS * C) if axis == "col" else (N_CHIPS * R + pad_rows, C)

    def body(s_ref, x_ref, *rest):
        o_ref = rest[-1]
        o_ref[...] = x_ref[...].astype(BF16)

    if axis == "col":
        dst = pl.BlockSpec((tr, tc), lambda i, j, s_ref: (i, s_ref[0] * nc + j))
    else:
        dst = pl.BlockSpec((tr, tc), lambda i, j, s_ref: (s_ref[0] * nr + i, j))
    extra_specs, extra = ([], ()) if dep is None else ([ANY], (dep,))
    grid_spec = pltpu.PrefetchScalarGridSpec(
        num_scalar_prefetch=1, grid=(nr, nc),
        in_specs=[pl.BlockSpec((None, tr, tc), lambda i, j, s_ref: (layer, i, j))] + extra_specs, out_specs=dst)
    out = pl.pallas_call(
        body, name="stage_" + kind, grid_spec=grid_spec, out_shape=jax.ShapeDtypeStruct(full, BF16),
        compiler_params=_cparams("parallel", "parallel"))(chip.reshape(1).astype(jnp.int32), shard, *extra)
    if pad_rows:
        rows0 = N_CHIPS * R
        pr = math.gcd(rows0, pad_rows)

        def zero_body(x_ref, o_ref):
            o_ref[...] = jnp.zeros_like(o_ref)

        out = pl.pallas_call(
            zero_body, name="zero_pad_" + kind, grid=(pad_rows // pr,), in_specs=[ANY],
            out_specs=pl.BlockSpec((pr, C), lambda i: (rows0 // pr + i, 0)), out_shape=jax.ShapeDtypeStruct(full, BF16),
            input_output_aliases={0: 0}, compiler_params=_cparams("parallel"))(out)
    return out


def _gather_start(items, mats, cws, after, name):
    n = len(items)

    def body(*refs):
        send_sems, recv_sems, token = refs[n + 1], refs[n + 2], refs[-1]
        m = refs[n + 3:2 * n + 3]

        def run(s, cc, c, sibling, chips):
            for q, (k, _) in enumerate(items):
                r, c_ = m[q].shape
                mine = m[q].at[_cell(KIND_AXIS[k], r, c_, cws[k], s, cc)]
                for j, (px, py) in enumerate(chips):
                    _remote(mine, mine, send_sems.at[3 * q + j], recv_sems.at[3 * q + j], (px, py, c)).start()

        _per_place(run)
        token[...] = jnp.zeros_like(token)

    outs = pl.pallas_call(
        body, name=name, in_specs=[HBM] * n + [ANY],
        out_specs=[SEM, SEM] + [HBM] * n + [pl.BlockSpec(memory_space=pltpu.VMEM)],
        out_shape=[pltpu.SemaphoreType.DMA((3 * n,)), pltpu.SemaphoreType.DMA((3 * n,))]
        + [pltpu.HBM(a.shape, a.dtype) for a in mats] + [jax.ShapeDtypeStruct((8, LANES), F32)],
        input_output_aliases={q: 2 + q for q in range(n)},
        compiler_params=pltpu.CompilerParams(has_side_effects=EFFECT),
    )(*[_in_hbm(a) for a in mats], after)
    return outs[0], outs[1], list(outs[2:2 + n]), outs[-1]


def _gather_wait(items, idx, mats, send_sems, recv_sems, cws, after, name):
    n = len(idx)

    def body(*refs):
        m = refs[:n]
        s_sems, r_sems = refs[n], refs[n + 1]

        def run(s, cc, c, sibling, chips):
            for a, q in enumerate(idx):
                k = items[q][0]
                r, c_ = m[a].shape
                mine = m[a].at[_cell(KIND_AXIS[k], r, c_, cws[k], s, cc)]
                for j, (px, py) in enumerate(chips):
                    theirs = m[a].at[_cell(KIND_AXIS[k], r, c_, cws[k], s ^ PEER_MASKS[j], cc)]
                    cp = _remote(mine, theirs, s_sems.at[3 * q + j], r_sems.at[3 * q + j], (px, py, c))
                    cp.wait_send()
                    cp.wait_recv()

        _per_place(run)

    outs = pl.pallas_call(
        body, name=name, in_specs=[HBM] * n + [SEM, SEM, ANY], out_specs=[HBM] * n,
        out_shape=[pltpu.HBM(a.shape, a.dtype) for a in mats], input_output_aliases={a: a for a in range(n)},
        compiler_params=pltpu.CompilerParams(has_side_effects=EFFECT),
    )(*mats, send_sems, recv_sems, after)
    return list(outs)


def _forward_halves(kinds, mats, cws, name):
    n = len(mats)

    def body(*refs):
        m = refs[n:2 * n]
        send_sems, recv_sems = refs[2 * n:]

        def run(s, cc, c, sibling, chips):
            cps = []
            for a, k in enumerate(kinds):
                r, c_ = m[a].shape
                for j in range(3):
                    have = m[a].at[_cell(KIND_AXIS[k], r, c_, cws[k], s ^ PEER_MASKS[j], cc)]
                    cps.append(_remote(have, have, send_sems.at[3 * a + j], recv_sems.at[3 * a + j], sibling))
            for cp in cps:
                cp.start()
            for cp in cps:
                cp.wait()

        _per_place(run)

    outs = pl.pallas_call(
        body, name=name, in_specs=[ANY] * n, out_specs=[ANY] * n,
        out_shape=[jax.ShapeDtypeStruct(a.shape, a.dtype) for a in mats], input_output_aliases={a: a for a in range(n)},
        scratch_shapes=[pltpu.SemaphoreType.DMA((3 * n,)), pltpu.SemaphoreType.DMA((3 * n,))],
    )(*mats)
    return list(outs)


def _swap_halves(kinds, gms, name):
    n = len(gms)
    half_shapes = [(g.shape[0] // 2, g.shape[1]) if KIND_AXIS[k] == "col" else (g.shape[0], g.shape[1] // 2)
                   for k, g in zip(kinds, gms)]

    def body(*refs):
        g, ra = refs[:n], refs[n:2 * n]
        send_sems, recv_sems = refs[2 * n:]

        def run(s, cc, c, sibling, chips):
            cps = []
            for a, k in enumerate(kinds):
                r, c_ = g[a].shape
                cps.append(_remote(g[a].at[_half_win(KIND_AXIS[k], r, c_, 1 - cc)], ra[a], send_sems.at[a], recv_sems.at[a],
                                   sibling))
            for cp in cps:
                cp.start()
            for cp in cps:
                cp.wait()

        _per_place(run)

    outs = pl.pallas_call(
        body, name=name, in_specs=[ANY] * n, out_specs=[ANY] * n,
        out_shape=[jax.ShapeDtypeStruct(hs, BF16) for hs in half_shapes],
        scratch_shapes=[pltpu.SemaphoreType.DMA((n,)), pltpu.SemaphoreType.DMA((n,))],
    )(*gms)
    return list(outs)


def _win_shape(kind, pa, cw):
    return (pa.shape[0], cw) if KIND_AXIS[kind] == "col" else (cw, pa.shape[1])


def _scatter_start(kinds, pas, cws, name):
    n = len(pas)
    lands = [lax.empty((3,) + _win_shape(k, p, cws[k]), BF16) for k, p in zip(kinds, pas)]

    def body(*refs):
        send_sems, recv_sems, token = refs[2 * n], refs[2 * n + 1], refs[-1]
        p, rb = refs[2 * n + 2:3 * n + 2], refs[3 * n + 2:4 * n + 2]

        def run(s, cc, c, sibling, chips):
            for a, k in enumerate(kinds):
                for j, (px, py) in enumerate(chips):
                    src = p[a].at[_chip_win(KIND_AXIS[k], cws[k], s ^ PEER_MASKS[j])]
                    _remote(src, rb[a].at[j], send_sems.at[3 * a + j], recv_sems.at[3 * a + j], (px, py, c)).start()

        _per_place(run)
        token[...] = jnp.zeros_like(token)

    outs = pl.pallas_call(
        body, name=name, in_specs=[HBM] * (2 * n),
        out_specs=[SEM, SEM] + [HBM] * (2 * n) + [pl.BlockSpec(memory_space=pltpu.VMEM)],
        out_shape=[pltpu.SemaphoreType.DMA((3 * n,)), pltpu.SemaphoreType.DMA((3 * n,))]
        + [pltpu.HBM(a.shape, a.dtype) for a in pas + lands] + [jax.ShapeDtypeStruct((8, LANES), F32)],
        input_output_aliases={q: 2 + q for q in range(2 * n)},
        compiler_params=pltpu.CompilerParams(has_side_effects=EFFECT),
    )(*[_in_hbm(a) for a in pas + lands])
    return outs[0], outs[1], list(outs[2:2 + n]), list(outs[2 + n:2 + 2 * n]), outs[-1]


def _scatter_wait(kinds, pas, lands, send_sems, recv_sems, cws, after, name):
    n = len(pas)

    def body(*refs):
        p, rb = refs[:n], refs[n:2 * n]
        s_sems, r_sems = refs[2 * n], refs[2 * n + 1]

        def run(s, cc, c, sibling, chips):
            for a, k in enumerate(kinds):
                for j, (px, py) in enumerate(chips):
                    src = p[a].at[_chip_win(KIND_AXIS[k], cws[k], s ^ PEER_MASKS[j])]
                    cp = _remote(src, rb[a].at[j], s_sems.at[3 * a + j], r_sems.at[3 * a + j], (px, py, c))
                    cp.wait_send()
                    cp.wait_recv()

        _per_place(run)

    outs = pl.pallas_call(
        body, name=name, in_specs=[HBM] * (2 * n) + [SEM, SEM, ANY], out_specs=[HBM] * (2 * n),
        out_shape=[pltpu.HBM(a.shape, a.dtype) for a in pas + lands], input_output_aliases={a: a for a in range(2 * n)},
        compiler_params=pltpu.CompilerParams(has_side_effects=EFFECT),
    )(*pas, *lands, send_sems, recv_sems, after)
    return list(outs[:n]), list(outs[n:])


def _share_halves(g):
    nq = len(KINDS)

    def body(*refs):
        out = dict(zip(KINDS, refs[nq:2 * nq]))
        send_sems, recv_sems = refs[2 * nq:]

        def run(s, cc, c, sibling, chips):
            cps = []
            for q, k in enumerate(KINDS):
                _, r, c_ = out[k].shape
                mine = out[k].at[(ALL,) + _half_win(KIND_AXIS[k], r, c_, cc)]
                cps.append(_remote(mine, mine, send_sems.at[q], recv_sems.at[q], sibling))
            for cp in cps:
                cp.start()
            for cp in cps:
                cp.wait()

        _per_place(run)

    outs = pl.pallas_call(
        body, name="share_halves", in_specs=[ANY] * nq, out_specs=[ANY] * nq,
        out_shape=[jax.ShapeDtypeStruct(g[k].shape, F32) for k in KINDS],
        input_output_aliases={q: q for q in range(nq)},
        scratch_shapes=[pltpu.SemaphoreType.DMA((nq,)), pltpu.SemaphoreType.DMA((nq,))],
    )(*[g[k] for k in KINDS])
    return dict(zip(KINDS, outs))


def _all_gather_small(xs, name):
    m_per, n = xs.shape

    def body(x_ref, out_ref, send_sems, recv_sems, local_sem):
        x, y, c, sibling, chips = _place()
        me = (x, y, c)

        def rows(px, py, pc):
            return out_ref.at[pl.ds((4 * px + 2 * py + pc) * m_per, m_per), :]

        def copy(k, block, to, src=None):
            return _remote(rows(*block) if src is None else src, rows(*block), send_sems.at[k], recv_sems.at[k], to)

        mine = pltpu.make_async_copy(x_ref, rows(*me), local_sem)
        mine.start()
        first = [copy(0, me, sibling, src=x_ref)]
        first += [copy(1 + j, me, (*chip, c), src=x_ref) for j, chip in enumerate(chips)]
        for cp in first:
            cp.start()
        passed = [copy(4 + j, (*chip, c), sibling) for j, chip in enumerate(chips)]
        for j, chip in enumerate(chips):
            copy(1 + j, (*chip, c), me).wait_recv()
            passed[j].start()
        copy(0, sibling, me).wait_recv()
        for j, chip in enumerate(chips):
            copy(4 + j, (*chip, 1 - c), me).wait_recv()
        for cp in first + passed:
            cp.wait_send()
        mine.wait()

    vm = pl.BlockSpec(memory_space=pltpu.VMEM)
    return pl.pallas_call(
        body, name=name, in_specs=[vm], out_specs=vm, out_shape=jax.ShapeDtypeStruct((N_DEV * m_per, n), xs.dtype),
        scratch_shapes=[pltpu.SemaphoreType.DMA((7,)), pltpu.SemaphoreType.DMA((7,)), pltpu.SemaphoreType.DMA],
        compiler_params=pltpu.CompilerParams(vmem_limit_bytes=VMEM_LIMIT_BYTES),
    )(xs)


def _row_tile(rows, cap=256):
    for mult in (16, 8):
        best = None
        t = mult
        while t <= min(rows, cap):
            if rows % t == 0:
                best = t
            t += mult
        if best is not None:
            return best
    raise ValueError(rows)


def _add_sibling(kind, g, ra, core):
    R, C = ra.shape
    axis = KIND_AXIS[kind]
    tr, tc = _row_tile(R), _pick_tile(C, 2048)
    nr, nc = R // tr, C // tc

    def body(c_ref, a_ref, b_ref, o_ref):
        o_ref[...] = (a_ref[...].astype(F32) + b_ref[...].astype(F32)).astype(o_ref.dtype)

    if axis == "col":
        own = pl.BlockSpec((tr, tc), lambda i, j, c_ref: (c_ref[0] * nr + i, j))
    else:
        own = pl.BlockSpec((tr, tc), lambda i, j, c_ref: (i, c_ref[0] * nc + j))
    same = pl.BlockSpec((tr, tc), lambda i, j, c_ref: (i, j))
    grid_spec = pltpu.PrefetchScalarGridSpec(num_scalar_prefetch=1, grid=(nr, nc), in_specs=[own, same], out_specs=same)
    return pl.pallas_call(
        body, name="add_sibling_" + kind, grid_spec=grid_spec, out_shape=jax.ShapeDtypeStruct(ra.shape, BF16),
        compiler_params=_cparams("parallel", "parallel"))(core.reshape(1).astype(jnp.int32), g, ra)


def _sum_chips(kind, pa, rb, chip, core, out, layer):
    _, R, C = rb.shape
    axis = KIND_AXIS[kind]
    tr, tc = _row_tile(R), _pick_tile(C, 2048)
    nr, nc = R // tr, C // tc

    def body(s_ref, c_ref, a_ref, b0_ref, b1_ref, b2_ref, old_ref, o_ref):
        o_ref[...] = ((a_ref[...].astype(F32) + b0_ref[...].astype(F32)) + b1_ref[...].astype(F32)) + b2_ref[...].astype(F32)

    def rb_spec(n):
        return pl.BlockSpec((None, tr, tc), lambda i, j, s_ref, c_ref: (n, i, j))

    if axis == "col":
        own = pl.BlockSpec((tr, tc), lambda i, j, s_ref, c_ref: (i, s_ref[0] * nc + j))
        dst = pl.BlockSpec((None, tr, tc), lambda i, j, s_ref, c_ref: (layer, c_ref[0] * nr + i, j))
        assert out.shape[1:] == (2 * R, C)
    else:
        own = pl.BlockSpec((tr, tc), lambda i, j, s_ref, c_ref: (s_ref[0] * nr + i, j))
        dst = pl.BlockSpec((None, tr, tc), lambda i, j, s_ref, c_ref: (layer, i, c_ref[0] * nc + j))
        assert out.shape[1:] == (R, 2 * C)
    grid_spec = pltpu.PrefetchScalarGridSpec(
        num_scalar_prefetch=2, grid=(nr, nc), in_specs=[own, rb_spec(0), rb_spec(1), rb_spec(2), ANY], out_specs=dst)
    return pl.pallas_call(
        body, name="sum_chips_" + kind, grid_spec=grid_spec, out_shape=jax.ShapeDtypeStruct(out.shape, F32),
        input_output_aliases={6: 0}, compiler_params=_cparams("parallel", "parallel"))(
            chip.reshape(1).astype(jnp.int32), core.reshape(1).astype(jnp.int32), pa, rb, rb, rb, out)


def _sum_devices(gathered):
    M = gathered.shape[0] // N_DEV
    C = gathered.shape[1]

    def body(g_ref, o_ref):
        acc = g_ref[0:M, :]
        for d in range(1, N_DEV):
            acc = acc + g_ref[d * M:(d + 1) * M, :]
        o_ref[...] = acc

    vm = pl.BlockSpec(memory_space=pltpu.VMEM)
    return pl.pallas_call(body, name="sum_devices", in_specs=[vm], out_specs=vm, out_shape=jax.ShapeDtypeStruct((M, C), F32),
                          compiler_params=pltpu.CompilerParams(vmem_limit_bytes=VMEM_LIMIT_BYTES))(gathered)


WEIGHTS = ["mix_norm", "ffn_norm", "final_norm", "hgrn_w_in", "hgrn_lb_logits", "hgrn_gnorm", "hgrn_w_out", "m_w_in",
           "m_conv_w", "m_conv_b", "m_dt_bias", "m_A_log", "m_D", "m_norm", "m_w_out", "f_w_up", "f_conv_w", "f_conv_b",
           "f_w_down"]
BIG_COLS = ("hgrn_w_in", "m_w_in", "f_w_up")
BIG_ROWS = ("hgrn_w_out", "m_w_out", "f_w_down")
BIG = BIG_COLS + BIG_ROWS
SMALL_SHARDED = ("m_conv_w", "m_conv_b", "m_norm", "f_conv_w")
SMALL_REPLICATED = ("mix_norm", "ffn_norm", "final_norm", "hgrn_lb_logits", "hgrn_gnorm", "m_dt_bias", "m_A_log", "m_D",
                    "f_conv_b")
SMALL = SMALL_REPLICATED + SMALL_SHARDED


def _pack_rows(arrs, row_mult=8):
    flat = jnp.concatenate([a.reshape(-1).astype(F32) for a in arrs])
    unit = FLAT_COLS * row_mult
    n = -(-flat.size // unit) * unit
    return jnp.pad(flat, (0, n - flat.size)).reshape(-1, FLAT_COLS)


def _unpack_rows(buf, shapes):
    flat = buf.reshape(-1)
    out, off = [], 0
    for shp in shapes:
        n = math.prod(shp)
        out.append(flat[off:off + n].reshape(shp))
        off += n
    return out


def kernel(x, mix_norm, ffn_norm, final_norm, hgrn_w_in, hgrn_lb_logits, hgrn_gnorm, hgrn_w_out, m_w_in, m_conv_w, m_conv_b, m_dt_bias, m_A_log, m_D, m_norm, m_w_out, f_w_up, f_conv_w, f_conv_b, f_w_down, loss_target, m_mix_norm, m_ffn_norm, m_final_norm, m_hgrn_w_in, m_hgrn_lb_logits, m_hgrn_gnorm, m_hgrn_w_out, m_m_w_in, m_m_conv_w, m_m_conv_b, m_m_dt_bias, m_m_A_log, m_m_D, m_m_norm, m_m_w_out, m_f_w_up, m_f_conv_w, m_f_conv_b, m_f_w_down, v_mix_norm, v_ffn_norm, v_final_norm, v_hgrn_w_in, v_hgrn_lb_logits, v_hgrn_gnorm, v_hgrn_w_out, v_m_w_in, v_m_conv_w, v_m_conv_b, v_m_dt_bias, v_m_A_log, v_m_D, v_m_norm, v_m_w_out, v_f_w_up, v_f_conv_w, v_f_conv_b, v_f_w_down):
    given = dict(locals())
    w = {n: given[n] for n in WEIGHTS}
    mom1 = {n: given["m_" + n] for n in WEIGHTS}
    mom2 = {n: given["v_" + n] for n in WEIGHTS}
    chip = 2 * lax.axis_index("x") + lax.axis_index("y")
    core = lax.axis_index("c")

    shards = {k: w[k] for k in KINDS if k != "m_w_in_t"}
    shards["m_w_in_t"] = w["m_w_in"].transpose(0, 2, 1).astype(BF16)
    m_in = N_CHIPS * w["m_w_in"].shape[2]
    pad_rows = {"m_w_in_t": -(-m_in // LANES) * LANES - m_in}
    cws = {k: shards[k].shape[2] if KIND_AXIS[k] == "col" else shards[k].shape[1] for k in KINDS}
    depth = w["mix_norm"].shape[0]

    def layer_kinds(i):
        mixer = {"mix_in": ("hgrn_w_in", i // 2), "mix_out": ("hgrn_w_out", i // 2)} if i % 2 == 0 else \
                {"mix_in": ("m_w_in_t", i // 2), "mix_out": ("m_w_out", i // 2)}
        return {**mixer, "f_w_up": ("f_w_up", i), "f_w_down": ("f_w_down", i)}

    own = _pack_rows([w[n] for n in SMALL_SHARDED])
    all_small = _all_gather_small(own, "gather_small_params")
    groups, started = [list(layer_kinds(0).values()), [it for i in range(1, depth) for it in layer_kinds(i).values()]], []
    after = all_small
    for n_, items in enumerate(groups):
        staged = [_stage_shard(k, shards[k], l, chip, pad_rows.get(k, 0), dep=None if n_ == 0 else after) for k, l in items]
        send_sems, recv_sems, mats, after = _gather_start(items, staged, cws, after, "gather_start_%d" % n_)
        started.append((items, send_sems, recv_sems, mats))
    all_small = all_small.reshape(N_CHIPS, 2, -1)[:, 0]
    per_chip = [_unpack_rows(all_small[s], [w[n].shape for n in SMALL_SHARDED]) for s in range(N_CHIPS)]
    P = {}
    for i, n in enumerate(SMALL_SHARDED):
        P[n] = jnp.concatenate([per_chip[s][i] for s in range(N_CHIPS)], axis=-1)
    for n in SMALL_REPLICATED:
        P[n] = w[n]

    def fetch(i, keys, h):
        lk = {key: layer_kinds(i)[key] for key in keys}
        items, send_sems, recv_sems, mats = started[0 if i == 0 else 1]
        idx = [items.index(it) for it in lk.values()]
        tag = "%d_%s" % (i, keys[0])
        got = _gather_wait(items, idx, [mats[q] for q in idx], send_sems, recv_sems, cws, h, "gather_wait_" + tag)
        got = _forward_halves([k for k, _ in lk.values()], got, cws, "forward_halves_" + tag)
        return dict(zip(lk.keys(), got))

    pending = []

    def emit(i, gm):
        lk = {key: layer_kinds(i)[key] for key in gm}
        kinds = [k for k, _ in lk.values()]
        gms = list(gm.values())
        tag = "%d_%s" % (i, next(iter(gm)))
        ra = _swap_halves(kinds, gms, "swap_halves_" + tag)
        pas = [_add_sibling(k, g_, r_, core) for k, g_, r_ in zip(kinds, gms, ra)]
        s_sems, r_sems, pas, lands, tok = _scatter_start(kinds, pas, cws, "scatter_start_" + tag)
        pending.append((tag, list(lk.values()), pas, lands, s_sems, r_sems))
        return tok

    loss_part, grad_x, g_full = _local_step(x, loss_target, P, fetch, emit)

    g_sh = {k: lax.empty(shards[k].shape, F32) for k in KINDS}
    for tag, its, pas, lands, s_sems, r_sems in pending:
        kinds = [k for k, _ in its]
        pas, lands = _scatter_wait(kinds, pas, lands, s_sems, r_sems, cws, grad_x, "scatter_wait_" + tag)
        for (k, l), p_, rb_ in zip(its, pas, lands):
            g_sh[k] = _sum_chips(k, p_, rb_, chip, core, g_sh[k], l)
    g_sh = _share_halves(g_sh)
    grads = {k: g_sh[k] for k in KINDS if k != "m_w_in_t"}
    grads["m_w_in"] = g_sh["m_w_in_t"].transpose(0, 2, 1)

    small_shapes = [g_full[n].shape for n in SMALL] + [(1,)]
    packed = _pack_rows([g_full[n] for n in SMALL] + [loss_part[0, 0:1]])
    summed = _sum_devices(_all_gather_small(packed, "gather_small_grads"))
    small = _unpack_rows(summed, small_shapes)
    loss = small[-1][0]
    for n, gs in zip(SMALL, small[:-1]):
        if n in SMALL_SHARDED:
            width = w[n].shape[-1]
            gs = lax.dynamic_slice_in_dim(gs, chip * width, width, axis=gs.ndim - 1)
        grads[n] = gs

    delta, new_m, new_v = {}, {}, {}
    for n in BIG:
        delta[n], new_m[n], new_v[n] = _adamw(w[n], grads[n], mom1[n], mom2[n], "adamw_" + n)
    shapes = [w[n].shape for n in SMALL]
    ds, ms, vs = _adamw(_pack_rows([w[n] for n in SMALL]), _pack_rows([grads[n] for n in SMALL]),
                        _pack_rows([mom1[n] for n in SMALL]), _pack_rows([mom2[n] for n in SMALL]), "adamw_small")
    for n, d_, m_, v_ in zip(SMALL, _unpack_rows(ds, shapes), _unpack_rows(ms, shapes), _unpack_rows(vs, shapes)):
        delta[n], new_m[n], new_v[n] = d_, m_, v_

    return (loss, grad_x, *[grads[n] for n in WEIGHTS], *[delta[n] for n in WEIGHTS], *[new_m[n] for n in WEIGHTS],
            *[new_v[n] for n in WEIGHTS])
```

```python
import functools
import math

import jax
import jax.numpy as jnp
from jax import lax
from jax.experimental import pallas as pl
from jax.experimental.pallas import tpu as pltpu

F32 = jnp.float32
BF16 = jnp.bfloat16
NORM_EPS = 1e-5
HGRN_DK = 128
HGRN_CHUNK = 64
HGRN_HEADS_PER_STEP = 4
HGRN_SEQ_BLOCK = 512
M_HEADDIM = 64
M_GROUPS = 8
M_D_STATE = 128
M_CONV = 4
M_CHUNK = 128
FFN_CONV = 3
EXP_CLIP = 80.0
LANES = 128
VMEM_LIMIT_BYTES = 56 * 1024 * 1024
FLAT_COLS = 1024
ADAM_LR, ADAM_B1, ADAM_B2, ADAM_EPS, ADAM_WD, ADAM_STEP = 0.001, 0.9, 0.999, 1e-08, 0.01, 10
MESH = pl.DeviceIdType.MESH

NN = (((1,), (0,)), ((), ()))
NT = (((1,), (1,)), ((), ()))
TN = (((0,), (0,)), ((), ()))


def _cparams(*sems):
    return pltpu.CompilerParams(dimension_semantics=sems, vmem_limit_bytes=VMEM_LIMIT_BYTES)


def _dot(a, b, dn=NN):
    return lax.dot_general(a.astype(BF16), b.astype(BF16), dn, preferred_element_type=F32)


def _dot_exact(x, m, dn=NN, passes=3, x_first=True):
    acc = None
    r = x
    for _ in range(passes):
        p = r.astype(BF16)
        r = r - p.astype(F32)
        t = lax.dot_general(p, m, dn, preferred_element_type=F32) if x_first else lax.dot_general(m, p, dn, preferred_element_type=F32)
        acc = t if acc is None else acc + t
    return acc


def _iota(shape, dim):
    return lax.broadcasted_iota(jnp.int32, shape, dim)


def _cumsum_rows(x, reverse=False):
    n = x.shape[0]
    row = _iota(x.shape, 0)
    s = 1
    while s < n:
        if reverse:
            x = x + jnp.where(row < n - s, pltpu.roll(x, n - s, 0), 0.0)
        else:
            x = x + jnp.where(row >= s, pltpu.roll(x, s, 0), 0.0)
        s *= 2
    return x


def _silu(x):
    return x * jax.nn.sigmoid(x)


def _dsilu(x):
    s = jax.nn.sigmoid(x)
    return s * (1.0 + x * (1.0 - s))


def _pick_tile(dim, pref):
    if dim <= pref:
        return dim
    best = None
    t = LANES
    while t <= pref:
        if dim % t == 0:
            best = t
        t += LANES
    assert best is not None, (dim, pref)
    return best


def _matmul(a, b, *, ta=False, tb=False, res=None, out_dtype=F32, tm=1024, tn=1024, tk=2048, name,
            a_parts=False, b_parts=False, b_layer=None, b_off=0, out=None, out_layer=None, out_off=0, dep=None):
    a = a.astype(BF16)
    b = b.astype(BF16)
    if a_parts:
        assert not ta
        pa, M, kp = a.shape
        K = pa * kp
    else:
        M, K = (a.shape[1], a.shape[0]) if ta else a.shape
    bsh = b.shape[1:] if b_layer is not None else b.shape
    if b_parts:
        assert not tb
        pb, _, np_ = bsh
        N = pb * np_
    else:
        N = bsh[0] if tb else bsh[1]
    tm, tn, tk = _pick_tile(M, tm), _pick_tile(np_ if b_parts else N, tn), _pick_tile(kp if a_parts else K, tk)
    nk = K // tk
    dn = (((0 if ta else 1,), (1 if tb else 0,)), ((), ()))
    assert b_off % tk == 0 and out_off % tm == 0

    def body(*refs):
        refs = list(refs)
        acc = refs.pop() if nk > 1 else None
        o_ref = refs.pop()
        if dep is not None:
            refs.pop()
        if out is not None:
            refs.pop()
        a_ref, b_ref = refs[0], refs[1]
        r_ref = refs[2] if res is not None else None
        k = pl.program_id(2)

        def prod():
            return lax.dot_general(a_ref[...], b_ref[...], dn, preferred_element_type=F32)

        def finish(r):
            if res is not None:
                r = r + r_ref[...]
            o_ref[...] = r.astype(out_dtype)

        if nk == 1:
            finish(prod())
            return

        @pl.when(k == 0)
        def _():
            acc[...] = prod()

        @pl.when(jnp.logical_and(k > 0, k < nk - 1))
        def _():
            acc[...] += prod()

        @pl.when(k == nk - 1)
        def _():
            finish(acc[...] + prod())

    if a_parts:
        kpb = kp // tk
        a_spec = pl.BlockSpec((None, tm, tk), lambda i, j, k: (k // kpb, i, k % kpb))
    elif ta:
        a_spec = pl.BlockSpec((tk, tm), lambda i, j, k: (k, i))
    else:
        a_spec = pl.BlockSpec((tm, tk), lambda i, j, k: (i, k))
    lead = () if b_layer is None else (b_layer,)
    lead_blk = () if b_layer is None else (None,)
    kb0 = b_off // tk
    if b_parts:
        npb = np_ // tn
        b_spec = pl.BlockSpec(lead_blk + (None, tk, tn), lambda i, j, k: lead + (j // npb, k, j % npb))
    elif tb:
        b_spec = pl.BlockSpec(lead_blk + (tn, tk), lambda i, j, k: lead + (j, k))
    else:
        b_spec = pl.BlockSpec(lead_blk + (tk, tn), lambda i, j, k: lead + (kb0 + k, j))
    r_spec = pl.BlockSpec((tm, tn), lambda i, j, k: (i, j))
    in_specs = [a_spec, b_spec] + ([r_spec] if res is not None else [])
    args = (a, b) + ((res,) if res is not None else ())
    if out is None:
        o_spec, out_shape, aliases = r_spec, jax.ShapeDtypeStruct((M, N), out_dtype), {}
    else:
        assert out.dtype == out_dtype and out.shape[-1] == N
        olead = () if out_layer is None else (out_layer,)
        olead_blk = () if out_layer is None else (None,)
        ob0 = out_off // tm
        o_spec = pl.BlockSpec(olead_blk + (tm, tn), lambda i, j, k: olead + (ob0 + i, j))
        out_shape = jax.ShapeDtypeStruct(out.shape, out.dtype)
        aliases = {len(args): 0}
        in_specs = in_specs + [pl.BlockSpec(memory_space=pl.ANY)]
        args = args + (out,)
    if dep is not None:
        in_specs = in_specs + [pl.BlockSpec(memory_space=pl.ANY)]
        args = args + (dep,)
    return pl.pallas_call(
        body, name=name, grid=(M // tm, N // tn, nk), in_specs=in_specs, out_specs=o_spec, out_shape=out_shape,
        scratch_shapes=[pltpu.VMEM((tm, tn), F32)] if nk > 1 else [], input_output_aliases=aliases,
        compiler_params=_cparams("parallel", "parallel", "arbitrary"))(*args)


def _rmsnorm_fwd(h, w, name):
    T, D = h.shape
    tm = _pick_tile(T, 256)

    def body(h_ref, w_ref, u_ref):
        x = h_ref[...]
        r = lax.rsqrt(jnp.mean(x * x, axis=-1, keepdims=True) + NORM_EPS)
        u_ref[...] = (x * r * w_ref[...]).astype(BF16)

    return pl.pallas_call(
        body, name=name, grid=(T // tm,),
        in_specs=[pl.BlockSpec((tm, D), lambda i: (i, 0)), pl.BlockSpec((1, D), lambda i: (0, 0))],
        out_specs=pl.BlockSpec((tm, D), lambda i: (i, 0)), out_shape=jax.ShapeDtypeStruct((T, D), BF16),
        compiler_params=_cparams("parallel"))(h, w.reshape(1, D))


def _rmsnorm_bwd(h, w, du, dres, name, dep=None):
    T, D = h.shape
    tm = _pick_tile(T, 256)

    def body(h_ref, w_ref, du_ref, dr_ref, *rest):
        dh_ref, dhb_ref, dw_ref = rest[-3:]
        x = h_ref[...]
        r = lax.rsqrt(jnp.mean(x * x, axis=-1, keepdims=True) + NORM_EPS)
        xh = x * r
        du_ = du_ref[...]
        dy = du_ * w_ref[...]
        dh = dr_ref[...] + r * (dy - xh * jnp.mean(dy * xh, axis=-1, keepdims=True))
        dh_ref[...] = dh
        dhb_ref[...] = dh.astype(BF16)
        part = jnp.sum(du_ * xh, axis=0, keepdims=True)

        @pl.when(pl.program_id(0) == 0)
        def _():
            dw_ref[...] = part

        @pl.when(pl.program_id(0) > 0)
        def _():
            dw_ref[...] += part

    row = pl.BlockSpec((tm, D), lambda i: (i, 0))
    vec = pl.BlockSpec((1, D), lambda i: (0, 0))
    extra_specs, extra = ([], ()) if dep is None else ([pl.BlockSpec(memory_space=pl.ANY)], (dep,))
    dh, dhb, dw = pl.pallas_call(
        body, name=name, grid=(T // tm,), in_specs=[row, vec, row, row] + extra_specs, out_specs=[row, row, vec],
        out_shape=[jax.ShapeDtypeStruct((T, D), F32), jax.ShapeDtypeStruct((T, D), BF16), jax.ShapeDtypeStruct((1, D), F32)],
        compiler_params=_cparams("arbitrary"))(h, w.reshape(1, D), du, dres, *extra)
    return dh, dhb, dw.reshape(D)


def _loss_head(h, w, target):
    T, D = h.shape
    tm = _pick_tile(T, 256)

    def body(h_ref, w_ref, t_ref, loss_ref, dh_ref, dhb_ref, dw_ref):
        x = h_ref[...]
        wv = w_ref[...]
        r = lax.rsqrt(jnp.mean(x * x, axis=-1, keepdims=True) + NORM_EPS)
        xh = x * r
        e = xh * wv - t_ref[...]
        lpart = jnp.zeros((1, LANES), F32) + 0.5 * jnp.sum(jnp.mean(e * e, axis=-1, keepdims=True))
        dyo = e * (1.0 / D)
        dy = dyo * wv
        dh = r * (dy - xh * jnp.mean(dy * xh, axis=-1, keepdims=True))
        dh_ref[...] = dh
        dhb_ref[...] = dh.astype(BF16)
        part = jnp.sum(dyo * xh, axis=0, keepdims=True)

        @pl.when(pl.program_id(0) == 0)
        def _():
            dw_ref[...] = part
            loss_ref[...] = lpart

        @pl.when(pl.program_id(0) > 0)
        def _():
            dw_ref[...] += part
            loss_ref[...] += lpart

    row = pl.BlockSpec((tm, D), lambda i: (i, 0))
    vec = pl.BlockSpec((1, D), lambda i: (0, 0))
    lvec = pl.BlockSpec((1, LANES), lambda i: (0, 0))
    loss, dh, dhb, dw = pl.pallas_call(
        body, name="loss_head", grid=(T // tm,), in_specs=[row, vec, row], out_specs=[lvec, row, row, vec],
        out_shape=[jax.ShapeDtypeStruct((1, LANES), F32), jax.ShapeDtypeStruct((T, D), F32),
                   jax.ShapeDtypeStruct((T, D), BF16), jax.ShapeDtypeStruct((1, D), F32)],
        compiler_params=_cparams("arbitrary"))(h, w.reshape(1, D), target)
    return loss, dh, dhb, dw.reshape(D)


def _hgrn_gates(qr, fr, lb):
    sig = jax.nn.sigmoid(fr)
    nsig = jax.nn.sigmoid(-fr)
    fg = lb + (1.0 - lb) * sig
    logf = jnp.log(fg)
    k = (1.0 - lb) * nsig
    q = _silu(qr)
    return q, k, logf, sig, nsig, fg


def _hgrn_scaled(q, k, b, bmid):
    eq = jnp.exp(jnp.clip(b - bmid, -EXP_CLIP, EXP_CLIP))
    ek = jnp.exp(jnp.clip(bmid - b, -EXP_CLIP, EXP_CLIP))
    return q * eq, k * ek, eq, ek


def _hgrn_fwd(proj, lb, gnw, H):
    B, L, _ = proj.shape
    C, DK = HGRN_CHUNK, HGRN_DK
    F_ = H * DK
    NC = L // C

    nh = HGRN_HEADS_PER_STEP if H % HGRN_HEADS_PER_STEP == 0 else 1
    LB = min(L, HGRN_SEQ_BLOCK)
    ncb, nsb, WD = LB // C, L // LB, nh * DK

    def body(q_ref, f_ref, v_ref, g_ref, lb_ref, gn_ref, o_ref, on_ref, st_ref, ST, bsc):
        @pl.when(pl.program_id(2) == 0)
        def _():
            ST[...] = jnp.zeros_like(ST)

        gn = gn_ref[...]
        causal = _iota((C, C), 0) >= _iota((C, C), 1)

        def chunk(c, carry):
            r0 = pl.multiple_of(c * C, C)
            rows = pl.ds(r0, C)
            for hh in range(nh):
                ln = slice(hh * DK, (hh + 1) * DK)
                q, k, logf, _, _, _ = _hgrn_gates(q_ref[0, rows, ln], f_ref[0, rows, ln], lb_ref[:, ln])
                v = v_ref[0, rows, ln]
                b = _cumsum_rows(logf)
                bsc[hh] = b
                bmid = bsc[hh, C // 2 - 1:C // 2, :]
                blast = bsc[hh, C - 1:C, :]
                qs, ks, _, _ = _hgrn_scaled(q, k, b, bmid)
                A = jnp.where(causal, _dot(qs, ks, NT), 0.0)
                st = ST[hh]
                st_ref[0, hh, c] = st
                o = _dot(A, v) + _dot(q * jnp.exp(b), st, NT)
                kb = k * jnp.exp(blast - b)
                ST[hh] = st * jnp.exp(blast) + _dot(v, kb, TN)
                rms = lax.rsqrt(jnp.mean(o * o, axis=-1, keepdims=True) + NORM_EPS)
                o_ref[0, rows, ln] = o
                on_ref[0, rows, ln] = (o * rms * gn * _silu(g_ref[0, rows, ln])).astype(BF16)
            return carry

        lax.fori_loop(0, ncb, chunk, 0)

    def col(off):
        return pl.BlockSpec((1, LB, WD), lambda b, hp, s: (b, s, off // nh + hp))

    return pl.pallas_call(
        body, name="hgrn_fwd", grid=(B, H // nh, nsb),
        in_specs=[col(0), col(H), col(2 * H), col(3 * H), pl.BlockSpec((1, WD), lambda b, hp, s: (0, hp)),
                  pl.BlockSpec((1, DK), lambda b, hp, s: (0, 0))],
        out_specs=[col(0), col(0), pl.BlockSpec((1, nh, ncb, DK, DK), lambda b, hp, s: (b, hp, s, 0, 0))],
        out_shape=[jax.ShapeDtypeStruct((B, L, F_), F32), jax.ShapeDtypeStruct((B, L, F_), BF16),
                   jax.ShapeDtypeStruct((B, H, NC, DK, DK), F32)],
        scratch_shapes=[pltpu.VMEM((nh, DK, DK), F32), pltpu.VMEM((nh, C, DK), F32)],
        compiler_params=_cparams("parallel", "parallel", "arbitrary"))(proj, proj, proj, proj, lb, gnw)


def _hgrn_bwd(proj, o, don, st, lb, gnw, H):
    B, L, _ = proj.shape
    C, DK = HGRN_CHUNK, HGRN_DK
    F_ = H * DK
    NC = L // C

    nh = HGRN_HEADS_PER_STEP if H % HGRN_HEADS_PER_STEP == 0 else 1
    LB = min(L, HGRN_SEQ_BLOCK)
    ncb, nsb, WD = LB // C, L // LB, nh * DK

    def body(q_ref, f_ref, v_ref, g_ref, o_ref, do_ref, st_ref, lb_ref, gn_ref,
             dp_ref, dlb_ref, dgn_ref, DST, bsc):
        @pl.when(pl.program_id(2) == 0)
        def _():
            DST[...] = jnp.zeros_like(DST)
            dlb_ref[...] = jnp.zeros_like(dlb_ref)
            dgn_ref[...] = jnp.zeros_like(dgn_ref)

        gn = gn_ref[...]
        causal = _iota((C, C), 0) >= _iota((C, C), 1)
        lastrow = _iota((C, DK), 0) == C - 1

        def chunk(i, carry):
            c = ncb - 1 - i
            r0 = pl.multiple_of(c * C, C)
            rows = pl.ds(r0, C)
            for hh in range(nh):
                ln = slice(hh * DK, (hh + 1) * DK)
                lbv = lb_ref[:, ln]
                qr = q_ref[0, rows, ln]
                fr = f_ref[0, rows, ln]
                q, k, logf, sig, nsig, fg = _hgrn_gates(qr, fr, lbv)
                v = v_ref[0, rows, ln]
                b = _cumsum_rows(logf)
                bsc[hh] = b
                bmid = bsc[hh, C // 2 - 1:C // 2, :]
                blast = bsc[hh, C - 1:C, :]
                qs, ks, eq, ek = _hgrn_scaled(q, k, b, bmid)
                A = jnp.where(causal, _dot(qs, ks, NT), 0.0)
                st_in = st_ref[0, hh, c]
                dst = DST[hh]
                eb = jnp.exp(b)
                ebl = jnp.exp(blast)
                ekb = jnp.exp(blast - b)
                qb = q * eb
                kb = k * ekb
                ov = o_ref[0, rows, ln]
                gr = g_ref[0, rows, ln]
                rms = lax.rsqrt(jnp.mean(ov * ov, axis=-1, keepdims=True) + NORM_EPS)
                oh = ov * rms
                sg = _silu(gr)
                don_ = do_ref[0, rows, ln]
                dgn_ref[0, hh] += jnp.sum(don_ * oh * sg, axis=0, keepdims=True)
                dp_ref[3, 0, rows, ln] = (don_ * oh * gn * _dsilu(gr)).astype(BF16)
                doh = don_ * gn * sg
                do_ = rms * (doh - oh * jnp.mean(doh * oh, axis=-1, keepdims=True))
                dA = jnp.where(causal, _dot(do_, v, NT), 0.0)
                dp_ref[2, 0, rows, ln] = (_dot(A, do_, TN) + _dot(kb, dst, NT)).astype(BF16)
                dqb = _dot(do_, st_in)
                dkb = _dot(v, dst)
                dq = _dot(dA, ks) * eq + dqb * eb
                dk_inter = dkb * ekb
                dk = _dot(dA, qs, TN) * ek + dk_inter
                db = q * dq - k * dk
                extra = jnp.sum(k * dk_inter, axis=0, keepdims=True) + ebl * jnp.sum(st_in * dst, axis=0, keepdims=True)
                db = db + jnp.where(lastrow, extra, 0.0)
                dlogf = _cumsum_rows(db, reverse=True)
                DST[hh] = dst * ebl + _dot(do_, qb, TN)
                dp_ref[0, 0, rows, ln] = (dq * _dsilu(qr)).astype(BF16)
                ss = sig * nsig
                dp_ref[1, 0, rows, ln] = ((1.0 - lbv) * ss * (dlogf / fg - dk)).astype(BF16)
                dlb_ref[0, :, ln] += jnp.sum(dlogf * nsig / fg - dk * nsig, axis=0, keepdims=True)
            return carry

        lax.fori_loop(0, ncb, chunk, 0)

    def col(off):
        return pl.BlockSpec((1, LB, WD), lambda b, hp, s: (b, nsb - 1 - s, off // nh + hp))

    outs = pl.pallas_call(
        body, name="hgrn_bwd", grid=(B, H // nh, nsb),
        in_specs=[col(0), col(H), col(2 * H), col(3 * H), col(0), col(0),
                  pl.BlockSpec((1, nh, ncb, DK, DK), lambda b, hp, s: (b, hp, nsb - 1 - s, 0, 0)),
                  pl.BlockSpec((1, WD), lambda b, hp, s: (0, hp)), pl.BlockSpec((1, DK), lambda b, hp, s: (0, 0))],
        out_specs=[pl.BlockSpec((4, 1, LB, WD), lambda b, hp, s: (0, b, nsb - 1 - s, hp)),
                   pl.BlockSpec((1, 1, WD), lambda b, hp, s: (b, 0, hp)),
                   pl.BlockSpec((1, nh, 1, DK), lambda b, hp, s: (b, hp, 0, 0))],
        out_shape=[jax.ShapeDtypeStruct((4, B, L, F_), BF16), jax.ShapeDtypeStruct((B, 1, F_), F32),
                   jax.ShapeDtypeStruct((B, H, 1, DK), F32)],
        scratch_shapes=[pltpu.VMEM((nh, DK, DK), F32), pltpu.VMEM((nh, C, DK), F32)],
        compiler_params=_cparams("parallel", "parallel", "arbitrary"))(proj, proj, proj, proj, o, don, st, lb, gnw)
    return outs


CONV_ROWS = 256
PAD_ROWS = 8


def _conv_taps(pad_ref, w_ref, r0, K, rb, forward=True):
    ext = pad_ref[pl.ds(r0, rb + PAD_ROWS), :]
    n = rb + PAD_ROWS
    acc = None
    for s in range(K):
        if forward:
            sh = ext if s == 0 else pltpu.roll(ext, s, 0)
            term = sh[PAD_ROWS:, :]
        else:
            sh = ext if s == 0 else pltpu.roll(ext, n - s, 0)
            term = sh[:rb, :]
        term = term * w_ref[K - 1 - s:K - s, :]
        acc = term if acc is None else acc + term
    return acc


def _conv_dw(ext, dc, K):
    row = _iota((8, dc.shape[1]), 0)
    out = jnp.zeros((8, dc.shape[1]), F32)
    for kk in range(K):
        s = K - 1 - kk
        sh = ext if s == 0 else pltpu.roll(ext, s, 0)
        out = out + jnp.where(row == kk, jnp.sum(dc * sh[PAD_ROWS:, :], axis=0, keepdims=True), 0.0)
    return out


def _mconv_fwd(zx, cw, cb, col0, width):
    B, L, _ = zx.shape
    K = cw.shape[0]
    ct = _pick_tile(width, 256)
    rb = min(CONV_ROWS, L)
    nrb = L // rb
    off = col0 // ct

    def body(x_ref, w_ref, b_ref, y_ref, xp):
        xp[0:PAD_ROWS, :] = jnp.zeros((PAD_ROWS, ct), F32)
        xp[PAD_ROWS:, :] = x_ref[0]
        bias = b_ref[...]

        def blk(i, carry):
            r0 = pl.multiple_of(i * rb, rb)
            y_ref[0, pl.ds(r0, rb), :] = _silu(_conv_taps(xp, w_ref, r0, K, rb) + bias)
            return carry

        lax.fori_loop(0, nrb, blk, 0)

    return pl.pallas_call(
        body, name="mconv_fwd", grid=(B, width // ct),
        in_specs=[pl.BlockSpec((1, L, ct), lambda b, j: (b, 0, off + j)), pl.BlockSpec((K, ct), lambda b, j: (0, j)),
                  pl.BlockSpec((1, ct), lambda b, j: (0, j))],
        out_specs=pl.BlockSpec((1, L, ct), lambda b, j: (b, 0, j)),
        out_shape=jax.ShapeDtypeStruct((B, L, width), F32),
        scratch_shapes=[pltpu.VMEM((L + PAD_ROWS, ct), F32)],
        compiler_params=_cparams("parallel", "parallel"))(zx, cw, cb.reshape(1, width))


def _mconv_bwd(zx, dya, cw, cb, col0, wcol0, name):
    B, L, _ = zx.shape
    K = cw.shape[0]
    npart, _, _, wq = dya.shape
    width = npart * wq
    ct = _pick_tile(wq, 256)
    rb = min(CONV_ROWS, L)
    nrb = L // rb
    off = (col0 + wcol0) // ct
    woff = wcol0 // ct
    pq = wq // ct

    def body(x_ref, dy_ref, w_ref, b_ref, dx_ref, dw_ref, db_ref, xp, dcp):
        xp[0:PAD_ROWS, :] = jnp.zeros((PAD_ROWS, ct), F32)
        xp[PAD_ROWS:, :] = x_ref[0]
        dcp[L:, :] = jnp.zeros((PAD_ROWS, ct), F32)
        bias = b_ref[...]

        def blk1(i, carry):
            dw, db = carry
            r0 = pl.multiple_of(i * rb, rb)
            cpre = _conv_taps(xp, w_ref, r0, K, rb) + bias
            dc = dy_ref[0, 0, pl.ds(r0, rb), :] * _dsilu(cpre)
            dcp[pl.ds(r0, rb), :] = dc
            ext = xp[pl.ds(r0, rb + PAD_ROWS), :]
            return dw + _conv_dw(ext, dc, K), db + jnp.sum(dc, axis=0, keepdims=True)

        dw, db = lax.fori_loop(0, nrb, blk1, (jnp.zeros((8, ct), F32), jnp.zeros((1, ct), F32)))
        dw_ref[0] = dw
        db_ref[0] = db

        def blk2(i, carry):
            r0 = pl.multiple_of(i * rb, rb)
            dx_ref[0, pl.ds(r0, rb), :] = _conv_taps(dcp, w_ref, r0, K, rb, forward=False).astype(BF16)
            return carry

        lax.fori_loop(0, nrb, blk2, 0)

    dx, dw, db = pl.pallas_call(
        body, name=name, grid=(B, width // ct),
        in_specs=[pl.BlockSpec((1, L, ct), lambda b, j: (b, 0, off + j)),
                  pl.BlockSpec((1, 1, L, ct), lambda b, j: (j // pq, b, 0, j % pq)),
                  pl.BlockSpec((K, ct), lambda b, j: (0, woff + j)), pl.BlockSpec((1, ct), lambda b, j: (0, woff + j))],
        out_specs=[pl.BlockSpec((1, L, ct), lambda b, j: (b, 0, j)), pl.BlockSpec((1, 8, ct), lambda b, j: (b, 0, j)),
                   pl.BlockSpec((1, 1, ct), lambda b, j: (b, 0, j))],
        out_shape=[jax.ShapeDtypeStruct((B, L, width), BF16), jax.ShapeDtypeStruct((B, 8, width), F32),
                   jax.ShapeDtypeStruct((B, 1, width), F32)],
        scratch_shapes=[pltpu.VMEM((L + PAD_ROWS, ct), F32), pltpu.VMEM((L + PAD_ROWS, ct), F32)],
        compiler_params=_cparams("parallel", "parallel"))(zx, dya, cw, cb.reshape(1, -1))
    return dx, dw[:, :K, :], db


def _ffn_mid_fwd(up, cw, cb, dff):
    B, L, _ = up.shape
    K = cw.shape[0]
    ct = _pick_tile(dff, 256)
    rb = min(CONV_ROWS, L)
    nrb = L // rb
    half = dff // ct

    def body(g_ref, u_ref, wg_ref, wu_ref, bg_ref, bu_ref, a_ref, gp, upad):
        gp[0:PAD_ROWS, :] = jnp.zeros((PAD_ROWS, ct), F32)
        upad[0:PAD_ROWS, :] = jnp.zeros((PAD_ROWS, ct), F32)
        gp[PAD_ROWS:, :] = g_ref[0]
        upad[PAD_ROWS:, :] = u_ref[0]
        bg, bu = bg_ref[...], bu_ref[...]

        def blk(i, carry):
            r0 = pl.multiple_of(i * rb, rb)
            cg = _conv_taps(gp, wg_ref, r0, K, rb) + bg
            cu = _conv_taps(upad, wu_ref, r0, K, rb) + bu
            a_ref[0, pl.ds(r0, rb), :] = (_silu(cg) * cu).astype(BF16)
            return carry

        lax.fori_loop(0, nrb, blk, 0)

    xg = pl.BlockSpec((1, L, ct), lambda b, j: (b, 0, j))
    xu = pl.BlockSpec((1, L, ct), lambda b, j: (b, 0, half + j))
    wgs = pl.BlockSpec((K, ct), lambda b, j: (0, j))
    wus = pl.BlockSpec((K, ct), lambda b, j: (0, half + j))
    bgs = pl.BlockSpec((1, ct), lambda b, j: (0, j))
    bus = pl.BlockSpec((1, ct), lambda b, j: (0, half + j))
    cb2 = cb.reshape(1, 2 * dff)
    return pl.pallas_call(
        body, name="ffn_mid_fwd", grid=(B, half), in_specs=[xg, xu, wgs, wus, bgs, bus], out_specs=xg,
        out_shape=jax.ShapeDtypeStruct((B, L, dff), BF16),
        scratch_shapes=[pltpu.VMEM((L + PAD_ROWS, ct), F32), pltpu.VMEM((L + PAD_ROWS, ct), F32)],
        compiler_params=_cparams("parallel", "parallel"))(up, up, cw, cw, cb2, cb2)


def _ffn_mid_bwd(up, dact, cw, cb, dff):
    B, L, _ = up.shape
    K = cw.shape[0]
    ct = _pick_tile(dff, 256)
    rb = min(CONV_ROWS, L)
    nrb = L // rb
    half = dff // ct

    def body(g_ref, u_ref, da_ref, wg_ref, wu_ref, bg_ref, bu_ref, dx_ref, dwg_ref, dwu_ref, dbg_ref, dbu_ref,
             gp, upad, dgp, dup):
        gp[0:PAD_ROWS, :] = jnp.zeros((PAD_ROWS, ct), F32)
        upad[0:PAD_ROWS, :] = jnp.zeros((PAD_ROWS, ct), F32)
        gp[PAD_ROWS:, :] = g_ref[0]
        upad[PAD_ROWS:, :] = u_ref[0]
        dgp[L:, :] = jnp.zeros((PAD_ROWS, ct), F32)
        dup[L:, :] = jnp.zeros((PAD_ROWS, ct), F32)
        bg, bu = bg_ref[...], bu_ref[...]

        def blk1(i, carry):
            dwg, dwu, dbg, dbu = carry
            r0 = pl.multiple_of(i * rb, rb)
            cg = _conv_taps(gp, wg_ref, r0, K, rb) + bg
            cu = _conv_taps(upad, wu_ref, r0, K, rb) + bu
            da = da_ref[0, pl.ds(r0, rb), :]
            dcg = da * cu * _dsilu(cg)
            dcu = da * _silu(cg)
            dgp[pl.ds(r0, rb), :] = dcg
            dup[pl.ds(r0, rb), :] = dcu
            eg = gp[pl.ds(r0, rb + PAD_ROWS), :]
            eu = upad[pl.ds(r0, rb + PAD_ROWS), :]
            return (dwg + _conv_dw(eg, dcg, K), dwu + _conv_dw(eu, dcu, K), dbg + jnp.sum(dcg, axis=0, keepdims=True),
                    dbu + jnp.sum(dcu, axis=0, keepdims=True))

        z8 = jnp.zeros((8, ct), F32)
        z1 = jnp.zeros((1, ct), F32)
        dwg, dwu, dbg, dbu = lax.fori_loop(0, nrb, blk1, (z8, z8, z1, z1))
        dwg_ref[0] = dwg
        dwu_ref[0] = dwu
        dbg_ref[0] = dbg
        dbu_ref[0] = dbu

        def blk2(i, carry):
            r0 = pl.multiple_of(i * rb, rb)
            dx_ref[0, 0, pl.ds(r0, rb), :] = _conv_taps(dgp, wg_ref, r0, K, rb, forward=False).astype(BF16)
            dx_ref[1, 0, pl.ds(r0, rb), :] = _conv_taps(dup, wu_ref, r0, K, rb, forward=False).astype(BF16)
            return carry

        lax.fori_loop(0, nrb, blk2, 0)

    xg = pl.BlockSpec((1, L, ct), lambda b, j: (b, 0, j))
    xu = pl.BlockSpec((1, L, ct), lambda b, j: (b, 0, half + j))
    wgs = pl.BlockSpec((K, ct), lambda b, j: (0, j))
    wus = pl.BlockSpec((K, ct), lambda b, j: (0, half + j))
    bgs = pl.BlockSpec((1, ct), lambda b, j: (0, j))
    bus = pl.BlockSpec((1, ct), lambda b, j: (0, half + j))
    w8 = pl.BlockSpec((1, 8, ct), lambda b, j: (b, 0, j))
    b1 = pl.BlockSpec((1, 1, ct), lambda b, j: (b, 0, j))
    cb2 = cb.reshape(1, 2 * dff)
    pad = pltpu.VMEM((L + PAD_ROWS, ct), F32)
    dx2, dwg, dwu, dbg, dbu = pl.pallas_call(
        body, name="ffn_mid_bwd", grid=(B, half), in_specs=[xg, xu, xg, wgs, wus, bgs, bus],
        out_specs=[pl.BlockSpec((2, 1, L, ct), lambda b, j: (0, b, 0, j)), w8, w8, b1, b1],
        out_shape=[jax.ShapeDtypeStruct((2, B, L, dff), BF16)] + [jax.ShapeDtypeStruct((B, 8, dff), F32)] * 2
        + [jax.ShapeDtypeStruct((B, 1, dff), F32)] * 2,
        scratch_shapes=[pad, pad, pad, pad],
        compiler_params=_cparams("parallel", "parallel"))(up, up, dact, cw, cw, cb2, cb2)
    dw = jnp.concatenate([dwg[:, :K], dwu[:, :K]], axis=-1)
    db = jnp.concatenate([dbg, dbu], axis=-1)
    return dx2, dw, db


def _ssd_consts(hpg, W):
    P = M_HEADDIM
    E = (_iota((LANES, W), 0) == _iota((LANES, W), 1) // P).astype(BF16)
    Ebig = (_iota((LANES, hpg * LANES), 0) == _iota((LANES, hpg * LANES), 1) // LANES).astype(BF16)
    causal = _iota((M_CHUNK, M_CHUNK), 0) >= _iota((M_CHUNK, M_CHUNK), 1)
    head_of_lane = _iota((1, W), 1) // P
    return E, Ebig, causal, head_of_lane


def _ssd_chunk_fwd(xs, Bm, Cm, dtr, bias, Aneg, E, Ebig, causal, head_of_lane, hpg, st, ar_sc, ae_sc):
    pre = dtr + bias
    dt = jnp.maximum(pre, 0.0) + jnp.log(1.0 + jnp.exp(-jnp.abs(pre)))
    Ad = dt * Aneg
    a_c = _cumsum_rows(Ad)
    ar_sc[...] = a_c.T
    aexp = _dot_exact(a_c, E)
    ae_sc[...] = aexp
    alast = ae_sc[M_CHUNK - 1:M_CHUNK, :]
    dtexp = _dot_exact(dt, E)
    X = xs * dtexp
    AC = _dot_exact(a_c, Ebig)
    CB = _dot(Cm, Bm, NT)
    Xb = X.astype(BF16)
    Ls = [jnp.where(causal, jnp.exp(jnp.minimum(AC[:, j * LANES:(j + 1) * LANES] - ar_sc[j:j + 1, :], 0.0)), 0.0)
          for j in range(hpg)]
    first = _iota((1, LANES), 1) < M_HEADDIM
    pairs = []
    for p in range(hpg // 2):
        Xp = Xb[:, p * LANES:(p + 1) * LANES]
        pairs.append(jnp.where(first, _dot(CB * Ls[2 * p], Xp), _dot(CB * Ls[2 * p + 1], Xp)))
    ydiag = pairs[0] if len(pairs) == 1 else jnp.concatenate(pairs, axis=1)
    ea = jnp.exp(aexp)
    yoff = ea * _dot(Cm, st)
    dec = jnp.exp(alast - aexp)
    return dict(dt=dt, a_c=a_c, aexp=aexp, alast=alast, dtexp=dtexp, X=X, Xb=Xb, CB=CB, Ls=Ls, ydiag=ydiag, ea=ea,
                yoff=yoff, dec=dec)


def _ssd_fwd(xbca, zx, dtc, bias, Aneg, Dexp, nw, hpg):
    B, L, _ = xbca.shape
    G, N, C = M_GROUPS, M_D_STATE, M_CHUNK
    W = hpg * M_HEADDIM
    DI = G * W
    NC = L // C
    LB = min(L, 4 * C)
    ncb = LB // C

    def body(xs_ref, b_ref, c_ref, z_ref, dt_ref, bias_ref, a_ref, d_ref, nw_ref, y_ref, yn_ref, st_ref, ST, ar_sc, ae_sc):
        @pl.when(pl.program_id(2) == 0)
        def _():
            ST[...] = jnp.zeros_like(ST)

        E, Ebig, causal, head_of_lane = _ssd_consts(hpg, W)
        bias_ = bias_ref[0]
        Aneg_ = a_ref[0]
        Dv = d_ref[...]
        nwv = nw_ref[...]

        def chunk(ci, carry):
            r0 = pl.multiple_of(ci * C, C)
            rows = pl.ds(r0, C)
            xs = xs_ref[0, rows, :]
            Bm = b_ref[0, rows, :]
            Cm = c_ref[0, rows, :]
            st = ST[...]
            st_ref[0, 0, ci] = st
            f = _ssd_chunk_fwd(xs, Bm, Cm, dt_ref[0, 0, ci], bias_, Aneg_, E, Ebig, causal, head_of_lane, hpg, st, ar_sc, ae_sc)
            y = f["ydiag"] + f["yoff"] + xs * Dv
            ST[...] = st * jnp.exp(f["alast"]) + _dot(Bm, f["X"] * f["dec"], TN)
            yg = y * _silu(z_ref[0, rows, :])
            rstd = lax.rsqrt(jnp.mean(yg * yg, axis=-1, keepdims=True) + NORM_EPS)
            y_ref[0, rows, :] = y
            yn_ref[0, rows, :] = (yg * rstd * nwv).astype(BF16)
            return carry

        lax.fori_loop(0, ncb, chunk, 0)

    xw = pl.BlockSpec((1, LB, W), lambda b, g, s: (b, s, g))
    bsp = pl.BlockSpec((1, LB, N), lambda b, g, s: (b, s, DI // N + g))
    csp = pl.BlockSpec((1, LB, N), lambda b, g, s: (b, s, DI // N + G + g))
    dts = pl.BlockSpec((1, 1, ncb, C, LANES), lambda b, g, s: (b, g, s, 0, 0))
    hv = pl.BlockSpec((1, 1, LANES), lambda b, g, s: (g, 0, 0))
    wv = pl.BlockSpec((1, W), lambda b, g, s: (0, g))
    sts = pl.BlockSpec((1, 1, ncb, N, W), lambda b, g, s: (b, g, s, 0, 0))
    return pl.pallas_call(
        body, name="ssd_fwd", grid=(B, G, L // LB), in_specs=[xw, bsp, csp, xw, dts, hv, hv, wv, wv],
        out_specs=[xw, xw, sts],
        out_shape=[jax.ShapeDtypeStruct((B, L, DI), F32), jax.ShapeDtypeStruct((B, L, DI), BF16),
                   jax.ShapeDtypeStruct((B, G, NC, N, W), F32)],
        scratch_shapes=[pltpu.VMEM((N, W), F32), pltpu.VMEM((LANES, C), F32), pltpu.VMEM((C, W), F32)],
        compiler_params=_cparams("parallel", "parallel", "arbitrary"))(xbca, xbca, xbca, zx, dtc, bias, Aneg, Dexp, nw)


def _ssd_bwd(xbca, zx, dtc, ypre, dyn, st, bias, Aneg, Dexp, nw, hpg):
    B, L, _ = xbca.shape
    G, N, C = M_GROUPS, M_D_STATE, M_CHUNK
    W = hpg * M_HEADDIM
    DI = G * W
    NC = L // C
    LB = min(L, 4 * C)
    ncb = LB // C
    nsb = L // LB

    def body(xs_ref, b_ref, c_ref, z_ref, dt_ref, y_ref, dyn_ref, st_ref, bias_ref, a_ref, d_ref, nw_ref,
             dxs_ref, dbc_ref, dz_ref, ddt_ref, dnw_ref, dd_ref, da_ref, dbias_ref, DST, ar_sc, ae_sc):
        @pl.when(pl.program_id(2) == 0)
        def _():
            DST[...] = jnp.zeros_like(DST)
            dnw_ref[...] = jnp.zeros_like(dnw_ref)
            dd_ref[...] = jnp.zeros_like(dd_ref)
            da_ref[...] = jnp.zeros_like(da_ref)
            dbias_ref[...] = jnp.zeros_like(dbias_ref)

        E, Ebig, causal, head_of_lane = _ssd_consts(hpg, W)
        bias_ = bias_ref[0]
        Aneg_ = a_ref[0]
        Dv = d_ref[...]
        nwv = nw_ref[...]
        lane = _iota((1, LANES), 1)
        subl = _iota((LANES, 1), 0)
        lastrow = _iota((C, W), 0) == C - 1

        def chunk(i, carry):
            ci = ncb - 1 - i
            r0 = pl.multiple_of(ci * C, C)
            rows = pl.ds(r0, C)
            xs = xs_ref[0, rows, :]
            Bm = b_ref[0, rows, :]
            Cm = c_ref[0, rows, :]
            zr = z_ref[0, rows, :]
            dtr = dt_ref[0, 0, ci]
            st_in = st_ref[0, 0, ci]
            dst = DST[...]
            f = _ssd_chunk_fwd(xs, Bm, Cm, dtr, bias_, Aneg_, E, Ebig, causal, head_of_lane, hpg, st_in, ar_sc, ae_sc)
            X, Xb, dec, ea, CB = f["X"], f["Xb"], f["dec"], f["ea"], f["CB"]
            y = y_ref[0, rows, :]
            sz = _silu(zr)
            yg = y * sz
            rstd = lax.rsqrt(jnp.mean(yg * yg, axis=-1, keepdims=True) + NORM_EPS)
            yh = yg * rstd
            dyn_ = dyn_ref[0, rows, :]
            dnw_ref[0, 0] += jnp.sum(dyn_ * yh, axis=0, keepdims=True)
            dyh = dyn_ * nwv
            dyg = rstd * (dyh - yh * jnp.mean(dyh * yh, axis=-1, keepdims=True))
            dz_ref[0, rows, :] = (dyg * y * _dsilu(zr)).astype(BF16)
            dy = dyg * sz
            dd_ref[0, 0] += jnp.sum(dy * xs, axis=0, keepdims=True)
            dxs = dy * Dv
            dYo = dy * ea
            daexp = dy * f["yoff"]
            dCm = _dot(dYo, st_in, NT)
            dst_in = _dot(Cm, dYo, TN)
            dyb = dy.astype(BF16)
            dCB = jnp.zeros((C, C), F32)
            da_col = jnp.zeros((C, LANES), F32)
            da_row = jnp.zeros((LANES, C), F32)
            first = lane < M_HEADDIM
            dXs = []
            for p in range(hpg // 2):
                Xp = Xb[:, p * LANES:(p + 1) * LANES]
                dYp = dyb[:, p * LANES:(p + 1) * LANES]
                dXp = None
                for j in (2 * p, 2 * p + 1):
                    Lj = f["Ls"][j]
                    Gj = CB * Lj
                    dYj = jnp.where(first if j % 2 == 0 else jnp.logical_not(first), dYp, jnp.zeros_like(dYp))
                    t = _dot(Gj, dYj, TN)
                    dXp = t if dXp is None else dXp + t
                    dGj = _dot(dYj, Xp, NT)
                    dCB = dCB + dGj * Lj
                    Wj = dGj * Gj
                    da_col = da_col + jnp.sum(Wj, axis=1, keepdims=True) * (lane == j).astype(F32)
                    da_row = da_row + (subl == j).astype(F32) * jnp.sum(Wj, axis=0, keepdims=True)
                dXs.append(dXp)
            dX = dXs[0] if len(dXs) == 1 else jnp.concatenate(dXs, axis=1)
            dCm = dCm + _dot(dCB, Bm)
            dBm = _dot(dCB, Cm, TN)
            ela = jnp.exp(f["alast"])
            dalast = jnp.sum(dst * st_in, axis=0, keepdims=True) * ela
            DST[...] = dst * ela + dst_in
            dXd = _dot(Bm, dst)
            dBm = dBm + _dot(X * dec, dst, NT)
            dX = dX + dXd * dec
            ddec = dXd * X * dec
            dalast = dalast + jnp.sum(ddec, axis=0, keepdims=True)
            daexp = daexp - ddec + jnp.where(lastrow, dalast, 0.0)
            dxs = dxs + dX * f["dtexp"]
            ddtexp = dX * xs
            ddt = _dot_exact(ddtexp, E, NT, passes=2)
            da_c = _dot_exact(daexp, E, NT, passes=2) + da_col - da_row.T
            dAd = _cumsum_rows(da_c, reverse=True)
            ddt = ddt + dAd * Aneg_
            da_ref[0, 0] += jnp.sum(dAd * f["dt"], axis=0, keepdims=True) * Aneg_
            ddtr = ddt * jax.nn.sigmoid(dtr + bias_)
            dbias_ref[0, 0] += jnp.sum(ddtr, axis=0, keepdims=True)
            ddt_ref[0, 0, ci] = ddtr
            dxs_ref[0, rows, :] = dxs
            dbc_ref[0, 0, rows, :] = dBm
            dbc_ref[1, 0, rows, :] = dCm
            return carry

        lax.fori_loop(0, ncb, chunk, 0)

    def rev(s):
        return nsb - 1 - s

    xw = pl.BlockSpec((1, LB, W), lambda b, g, s: (b, rev(s), g))
    bsp = pl.BlockSpec((1, LB, N), lambda b, g, s: (b, rev(s), DI // N + g))
    csp = pl.BlockSpec((1, LB, N), lambda b, g, s: (b, rev(s), DI // N + G + g))
    gsp = pl.BlockSpec((1, LB, N), lambda b, g, s: (b, rev(s), g))
    dts = pl.BlockSpec((1, 1, ncb, C, LANES), lambda b, g, s: (b, g, rev(s), 0, 0))
    hv = pl.BlockSpec((1, 1, LANES), lambda b, g, s: (g, 0, 0))
    wv = pl.BlockSpec((1, W), lambda b, g, s: (0, g))
    sts = pl.BlockSpec((1, 1, ncb, N, W), lambda b, g, s: (b, g, rev(s), 0, 0))
    accw = pl.BlockSpec((1, 1, 1, W), lambda b, g, s: (b, g, 0, 0))
    acch = pl.BlockSpec((1, 1, 1, LANES), lambda b, g, s: (b, g, 0, 0))
    return pl.pallas_call(
        body, name="ssd_bwd", grid=(B, G, nsb), in_specs=[xw, bsp, csp, xw, dts, xw, xw, sts, hv, hv, wv, wv],
        out_specs=[xw, pl.BlockSpec((2, 1, LB, N), lambda b, g, s: (0, b, rev(s), g)), xw, dts, accw, accw, acch, acch],
        out_shape=[jax.ShapeDtypeStruct((B, L, DI), F32), jax.ShapeDtypeStruct((2, B, L, G * N), F32),
                   jax.ShapeDtypeStruct((B, L, DI), BF16),
                   jax.ShapeDtypeStruct((B, G, NC, C, LANES), F32), jax.ShapeDtypeStruct((B, G, 1, W), F32),
                   jax.ShapeDtypeStruct((B, G, 1, W), F32), jax.ShapeDtypeStruct((B, G, 1, LANES), F32),
                   jax.ShapeDtypeStruct((B, G, 1, LANES), F32)],
        scratch_shapes=[pltpu.VMEM((N, W), F32), pltpu.VMEM((LANES, C), F32), pltpu.VMEM((C, W), F32)],
        compiler_params=_cparams("parallel", "parallel", "arbitrary"))(
            xbca, xbca, xbca, zx, dtc, ypre, dyn, st, bias, Aneg, Dexp, nw)


def _adamw(w, g, m, v, name):
    shape = w.shape
    n = w.size
    cols = shape[-1]
    rows = n // cols
    tr = rows
    for cand in (512, 256, 128, 64, 32, 16, 8):
        if rows % cand == 0 and cand * cols * 4 <= 1024 * 1024:
            tr = cand
            break
    c1 = 1.0 / (1.0 - ADAM_B1 ** ADAM_STEP)
    c2 = 1.0 / (1.0 - ADAM_B2 ** ADAM_STEP)

    def body(w_ref, g_ref, m_ref, v_ref, d_ref, mo_ref, vo_ref):
        g_ = g_ref[...]
        mn = ADAM_B1 * m_ref[...] + (1.0 - ADAM_B1) * g_
        vn = ADAM_B2 * v_ref[...] + (1.0 - ADAM_B2) * (g_ * g_)
        d_ref[...] = -ADAM_LR * ((mn * c1) / (jnp.sqrt(vn * c2) + ADAM_EPS) + ADAM_WD * w_ref[...])
        mo_ref[...] = mn
        vo_ref[...] = vn

    spec = pl.BlockSpec((tr, cols), lambda i: (i, 0))
    r2 = lambda a: a.reshape(rows, cols)
    outs = pl.pallas_call(
        body, name=name, grid=(rows // tr,), in_specs=[spec] * 4, out_specs=[spec] * 3,
        out_shape=[jax.ShapeDtypeStruct((rows, cols), F32)] * 3,
        compiler_params=_cparams("parallel"))(r2(w), r2(g), r2(m), r2(v))
    return tuple(o.reshape(shape) for o in outs)


def _lower_bounds(lb_logits):
    p = jax.nn.softmax(lb_logits.astype(F32), axis=0)
    return jnp.cumsum(p, axis=0) - p[0]


def _pad_cols(a, n):
    return a if a.shape[-1] == n else jnp.pad(a, [(0, 0)] * (a.ndim - 1) + [(0, n - a.shape[-1])])


def _heads_to_lanes(a, G, hpg):
    return _pad_cols(a.reshape(G, 1, hpg), LANES)


def _local_step(x, target, P, fetch, emit):
    B, L, D = x.shape
    T = B * L
    depth = P["mix_norm"].shape[0]
    H = D // HGRN_DK
    F_ = H * HGRN_DK
    DI = P["m_norm"].shape[1]
    G, N = M_GROUPS, M_D_STATE
    MH = DI // M_HEADDIM
    hpg = MH // G
    assert hpg <= 8
    W = hpg * M_HEADDIM
    CD = DI + 2 * G * N
    MIN = DI + CD + MH
    MPAD = -(-MIN // LANES) * LANES
    dff = P["f_conv_b"].shape[1] // 2
    NC = L // M_CHUNK
    lbs = _lower_bounds(P["hgrn_lb_logits"])

    h = x.reshape(T, D)
    saved = []
    for i in range(depth):
        j = i // 2
        Wl = dict(fetch(i, ("mix_in",), h))
        s = {"h_in": h, "W": Wl}
        u = _rmsnorm_fwd(h, P["mix_norm"][i], "mix_norm_fwd")
        s["u"] = u
        if i % 2 == 0:
            proj = _matmul(u, Wl["mix_in"], name="hgrn_in_fwd").reshape(B, L, 4 * F_)
            o, on, st = _hgrn_fwd(proj, lbs[j].reshape(1, F_), P["hgrn_gnorm"][j].reshape(1, HGRN_DK), H)
            Wl.update(fetch(i, ("mix_out",), on))
            h = _matmul(on.reshape(T, F_), Wl["mix_out"], res=h, name="hgrn_out_fwd")
            s.update(proj=proj, o=o, on=on, st=st)
        else:
            zx = _matmul(u, Wl["mix_in"], tb=True, tn=1152, name="m_in_fwd").reshape(B, L, MPAD)
            xbca = _mconv_fwd(zx, P["m_conv_w"][j], P["m_conv_b"][j], DI, CD)
            dtr = zx[:, :, DI + CD:DI + CD + MH].reshape(B, NC, M_CHUNK, G, hpg).transpose(0, 3, 1, 2, 4)
            dtc = _pad_cols(dtr, LANES)
            bias = _heads_to_lanes(P["m_dt_bias"][j], G, hpg)
            Aneg = _heads_to_lanes(-jnp.exp(P["m_A_log"][j]), G, hpg)
            Dexp = jnp.repeat(P["m_D"][j], M_HEADDIM).reshape(1, DI)
            nw = P["m_norm"][j].reshape(1, DI)
            ypre, yn, st = _ssd_fwd(xbca, zx, dtc, bias, Aneg, Dexp, nw, hpg)
            Wl.update(fetch(i, ("mix_out",), yn))
            h = _matmul(yn.reshape(T, DI), Wl["mix_out"], res=h, name="m_out_fwd")
            s.update(zx=zx, xbca=xbca, dtc=dtc, bias=bias, Aneg=Aneg, Dexp=Dexp, nw=nw, ypre=ypre, yn=yn, st=st)
        s["h_mid"] = h
        u2 = _rmsnorm_fwd(h, P["ffn_norm"][i], "ffn_norm_fwd")
        Wl.update(fetch(i, ("f_w_up", "f_w_down"), h))
        up = _matmul(u2, Wl["f_w_up"], name="ffn_up_fwd").reshape(B, L, 2 * dff)
        act = _ffn_mid_fwd(up, P["f_conv_w"][i], P["f_conv_b"][i], dff)
        h = _matmul(act.reshape(T, dff), Wl["f_w_down"], res=h, name="ffn_down_fwd")
        s.update(u2=u2, up=up, act=act)
        saved.append(s)

    loss, dh, dhb, d_final = _loss_head(h, P["final_norm"], target.reshape(T, D))

    g = {k: [None] * P[k].shape[0] for k in ("mix_norm", "ffn_norm", "hgrn_gnorm", "m_conv_w", "m_conv_b", "m_dt_bias",
                                              "m_A_log", "m_D", "m_norm", "f_conv_w", "f_conv_b")}
    dlbs = [None] * lbs.shape[0]
    for i in reversed(range(depth)):
        j = i // 2
        s = saved[i]
        Wl = s["W"]
        gm = {}

        def dw(key, a, b, name, **kw):
            gm[key] = _matmul(a, b, ta=True, out_dtype=BF16, tk=T, name=name, **kw)

        dact = _matmul(dhb, Wl["f_w_down"], tb=True, name="ffn_down_dx").reshape(B, L, dff)
        dw("f_w_down", s["act"].reshape(T, dff), dhb, "ffn_down_dw")
        dup, dcw, dcb = _ffn_mid_bwd(s["up"], dact, P["f_conv_w"][i], P["f_conv_b"][i], dff)
        g["f_conv_w"][i] = jnp.sum(dcw, axis=0)
        g["f_conv_b"][i] = jnp.sum(dcb, axis=(0, 1))
        dup = dup.reshape(2, T, dff)
        dw("f_w_up", s["u2"], dup, "ffn_up_dw", b_parts=True)
        tok, finish = emit(i, {key: gm[key] for key in ("f_w_up", "f_w_down")})
        du2 = _matmul(dup, Wl["f_w_up"], a_parts=True, tb=True, name="ffn_up_dx", dep=tok)
        dh, dhb, g["ffn_norm"][i] = _rmsnorm_bwd(s["h_mid"], P["ffn_norm"][i], du2, dh, "ffn_norm_bwd", dep=finish(du2))
        if i % 2 == 0:
            don = _matmul(dhb, Wl["mix_out"], tb=True, name="hgrn_out_dx").reshape(B, L, F_)
            dw("mix_out", s["on"].reshape(T, F_), dhb, "hgrn_out_dw")
            dproj, dlb, dgn = _hgrn_bwd(s["proj"], s["o"], don, s["st"], lbs[j].reshape(1, F_),
                                        P["hgrn_gnorm"][j].reshape(1, HGRN_DK), H)
            dlbs[j] = jnp.sum(dlb, axis=(0, 1))
            g["hgrn_gnorm"][j] = jnp.sum(dgn, axis=(0, 1, 2))
            dproj = dproj.reshape(4, T, F_)
            dw("mix_in", s["u"], dproj, "hgrn_in_dw", b_parts=True)
            tok, finish = emit(i, {key: gm[key] for key in ("mix_in", "mix_out")})
            du = _matmul(dproj, Wl["mix_in"], a_parts=True, tb=True, name="hgrn_in_dx", dep=tok)
        else:
            dyn = _matmul(dhb, Wl["mix_out"], tb=True, name="m_out_dx").reshape(B, L, DI)
            dw("mix_out", s["yn"].reshape(T, DI), dhb, "m_out_dw")
            dxs, dbc, dz, ddt, dnw, dD, dA, dbias = _ssd_bwd(s["xbca"], s["zx"], s["dtc"], s["ypre"], dyn, s["st"],
                                                             s["bias"], s["Aneg"], s["Dexp"], s["nw"], hpg)
            g["m_norm"][j] = jnp.sum(dnw, axis=(0, 2)).reshape(DI)
            g["m_D"][j] = jnp.sum(dD, axis=(0, 2)).reshape(MH, M_HEADDIM).sum(axis=-1)
            g["m_A_log"][j] = jnp.sum(dA, axis=(0, 2))[:, :hpg].reshape(MH)
            g["m_dt_bias"][j] = jnp.sum(dbias, axis=(0, 2))[:, :hpg].reshape(MH)
            cw, cb = P["m_conv_w"][j], P["m_conv_b"][j]
            dxx, dcw_x, dcb_x = _mconv_bwd(s["zx"], dxs[None], cw, cb, DI, 0, "mconv_bwd_x")
            dxb, dcw_b, dcb_b = _mconv_bwd(s["zx"], dbc, cw, cb, DI, DI, "mconv_bwd_bc")
            g["m_conv_w"][j] = jnp.concatenate([jnp.sum(dcw_x, axis=0), jnp.sum(dcw_b, axis=0)], axis=-1)
            g["m_conv_b"][j] = jnp.concatenate([jnp.sum(dcb_x, axis=(0, 1)), jnp.sum(dcb_b, axis=(0, 1))], axis=-1)
            ddt_t = _pad_cols(ddt[..., :hpg].transpose(0, 2, 3, 1, 4).reshape(T, MH), MPAD - DI - CD).astype(BF16)
            pieces = [(dz.reshape(T, DI), 0), (dxx.reshape(T, DI), DI), (dxb.reshape(T, 2 * G * N), 2 * DI), (ddt_t, DI + CD)]
            gm["mix_in"] = lax.empty((MPAD, D), BF16)
            for n_, (piece, off) in enumerate(pieces):
                gm["mix_in"] = _matmul(piece, s["u"], ta=True, out_dtype=BF16, tk=T, out=gm["mix_in"], out_off=off,
                                       name="m_in_dw%d" % n_)
            tok, finish = emit(i, {key: gm[key] for key in ("mix_in", "mix_out")})
            du = None
            for n_, (piece, off) in enumerate(pieces):
                du = _matmul(piece, Wl["mix_in"], b_off=off, res=du, name="m_in_dx%d" % n_, dep=tok if n_ == 0 else None)
        dh, dhb, g["mix_norm"][i] = _rmsnorm_bwd(s["h_in"], P["mix_norm"][i], du, dh, "mix_norm_bwd", dep=finish(du))

    grads = {k: jnp.stack(vs) for k, vs in g.items()}
    grads["final_norm"] = d_final
    _, lb_vjp = jax.vjp(_lower_bounds, P["hgrn_lb_logits"])
    grads["hgrn_lb_logits"] = lb_vjp(jnp.stack(dlbs))[0]
    return loss, dh.reshape(B, L, D), grads


ANY = pl.BlockSpec(memory_space=pl.ANY)
N_CHIPS = 4
N_DEV = 8


def _place():
    x, y, c = lax.axis_index("x"), lax.axis_index("y"), lax.axis_index("c")
    sibling = (x, y, 1 - c)
    chips = [(1 - x, y), (x, 1 - y), (1 - x, 1 - y)]
    return x, y, c, sibling, chips


def _remote(src, dst, send_sem, recv_sem, to):
    return pltpu.make_async_remote_copy(src_ref=src, dst_ref=dst, send_sem=send_sem, recv_sem=recv_sem, device_id=to,
                                        device_id_type=MESH)


KIND_AXIS = {"hgrn_w_in": "col", "f_w_up": "col", "m_w_in_t": "row", "hgrn_w_out": "row", "m_w_out": "row", "f_w_down": "row"}
KINDS = tuple(KIND_AXIS)
PEER_MASKS = (2, 1, 3)
ALL = slice(None)


def _chip_win(axis, cw, s):
    return (ALL, slice(s * cw, (s + 1) * cw)) if axis == "col" else (slice(s * cw, (s + 1) * cw), ALL)


def _half_win(axis, rows, cols, h):
    return (slice(h * rows // 2, (h + 1) * rows // 2), ALL) if axis == "col" else (ALL, slice(h * cols // 2, (h + 1) * cols // 2))


def _per_place(fn):
    x, y, c, sibling, chips = _place()
    chip = 2 * x + y
    for s in range(N_CHIPS):
        for cc in range(2):
            @pl.when(jnp.logical_and(chip == s, c == cc))
            def _():
                fn(s, cc, c, sibling, chips)


HBM = pl.BlockSpec(memory_space=pltpu.HBM)
SEM = pl.BlockSpec(memory_space=pltpu.SEMAPHORE)
EFFECT = pltpu.SideEffectType.DATAFLOW_SIDE_EFFECTING


def _cell(axis, rows, cols, cw, s, h):
    if axis == "col":
        return (slice(h * rows // 2, (h + 1) * rows // 2), slice(s * cw, (s + 1) * cw))
    return (slice(s * cw, (s + 1) * cw), slice(h * cols // 2, (h + 1) * cols // 2))


def _in_hbm(a):
    return pltpu.with_memory_space_constraint(a, pltpu.HBM)


def _stage_shard(kind, shard, layer, chip, pad_rows=0, dep=None):
    _, R, C = shard.shape
    axis = KIND_AXIS[kind]
    tr, tc = _row_tile(R), _pick_tile(C, 2048)
    nr, nc = R // tr, C // tc
    full = (R, N_CHIPS * C) if axis == "col" else (N_CHIPS * R + pad_rows, C)

    def body(s_ref, x_ref, *rest):
        o_ref = rest[-1]
        o_ref[...] = x_ref[...].astype(BF16)

    if axis == "col":
        dst = pl.BlockSpec((tr, tc), lambda i, j, s_ref: (i, s_ref[0] * nc + j))
    else:
        dst = pl.BlockSpec((tr, tc), lambda i, j, s_ref: (s_ref[0] * nr + i, j))
    extra_specs, extra = ([], ()) if dep is None else ([ANY], (dep,))
    grid_spec = pltpu.PrefetchScalarGridSpec(
        num_scalar_prefetch=1, grid=(nr, nc),
        in_specs=[pl.BlockSpec((None, tr, tc), lambda i, j, s_ref: (layer, i, j))] + extra_specs, out_specs=dst)
    out = pl.pallas_call(
        body, name="stage_" + kind, grid_spec=grid_spec, out_shape=jax.ShapeDtypeStruct(full, BF16),
        compiler_params=_cparams("parallel", "parallel"))(chip.reshape(1).astype(jnp.int32), shard, *extra)
    if pad_rows:
        rows0 = N_CHIPS * R
        pr = math.gcd(rows0, pad_rows)

        def zero_body(x_ref, o_ref):
            o_ref[...] = jnp.zeros_like(o_ref)

        out = pl.pallas_call(
            zero_body, name="zero_pad_" + kind, grid=(pad_rows // pr,), in_specs=[ANY],
            out_specs=pl.BlockSpec((pr, C), lambda i: (rows0 // pr + i, 0)), out_shape=jax.ShapeDtypeStruct(full, BF16),
            input_output_aliases={0: 0}, compiler_params=_cparams("parallel"))(out)
    return out


def _gather_start(items, mats, cws, after, name):
    n = len(items)

    def body(*refs):
        send_sems, recv_sems, token = refs[n + 1], refs[n + 2], refs[-1]
        m = refs[n + 3:2 * n + 3]

        def run(s, cc, c, sibling, chips):
            for q, (k, _) in enumerate(items):
                r, c_ = m[q].shape
                mine = m[q].at[_cell(KIND_AXIS[k], r, c_, cws[k], s, cc)]
                for j, (px, py) in enumerate(chips):
                    _remote(mine, mine, send_sems.at[3 * q + j], recv_sems.at[3 * q + j], (px, py, c)).start()

        _per_place(run)
        token[...] = jnp.zeros_like(token)

    outs = pl.pallas_call(
        body, name=name, in_specs=[HBM] * n + [ANY],
        out_specs=[SEM, SEM] + [HBM] * n + [pl.BlockSpec(memory_space=pltpu.VMEM)],
        out_shape=[pltpu.SemaphoreType.DMA((3 * n,)), pltpu.SemaphoreType.DMA((3 * n,))]
        + [pltpu.HBM(a.shape, a.dtype) for a in mats] + [jax.ShapeDtypeStruct((8, LANES), F32)],
        input_output_aliases={q: 2 + q for q in range(n)},
        compiler_params=pltpu.CompilerParams(has_side_effects=EFFECT),
    )(*[_in_hbm(a) for a in mats], after)
    return outs[0], outs[1], list(outs[2:2 + n]), outs[-1]


def _gather_wait(items, idx, mats, send_sems, recv_sems, cws, after, name):
    n = len(idx)

    def body(*refs):
        m = refs[:n]
        s_sems, r_sems = refs[n], refs[n + 1]

        def run(s, cc, c, sibling, chips):
            for a, q in enumerate(idx):
                k = items[q][0]
                r, c_ = m[a].shape
                mine = m[a].at[_cell(KIND_AXIS[k], r, c_, cws[k], s, cc)]
                for j, (px, py) in enumerate(chips):
                    theirs = m[a].at[_cell(KIND_AXIS[k], r, c_, cws[k], s ^ PEER_MASKS[j], cc)]
                    cp = _remote(mine, theirs, s_sems.at[3 * q + j], r_sems.at[3 * q + j], (px, py, c))
                    cp.wait_send()
                    cp.wait_recv()

        _per_place(run)

    outs = pl.pallas_call(
        body, name=name, in_specs=[HBM] * n + [SEM, SEM, ANY], out_specs=[HBM] * n,
        out_shape=[pltpu.HBM(a.shape, a.dtype) for a in mats], input_output_aliases={a: a for a in range(n)},
        compiler_params=pltpu.CompilerParams(has_side_effects=EFFECT),
    )(*mats, send_sems, recv_sems, after)
    return list(outs)


def _forward_halves(kinds, mats, cws, name):
    n = len(mats)

    def body(*refs):
        m = refs[n:2 * n]
        send_sems, recv_sems = refs[2 * n:]

        def run(s, cc, c, sibling, chips):
            cps = []
            for a, k in enumerate(kinds):
                r, c_ = m[a].shape
                for j in range(3):
                    have = m[a].at[_cell(KIND_AXIS[k], r, c_, cws[k], s ^ PEER_MASKS[j], cc)]
                    cps.append(_remote(have, have, send_sems.at[3 * a + j], recv_sems.at[3 * a + j], sibling))
            for cp in cps:
                cp.start()
            for cp in cps:
                cp.wait()

        _per_place(run)

    outs = pl.pallas_call(
        body, name=name, in_specs=[ANY] * n, out_specs=[ANY] * n,
        out_shape=[jax.ShapeDtypeStruct(a.shape, a.dtype) for a in mats], input_output_aliases={a: a for a in range(n)},
        scratch_shapes=[pltpu.SemaphoreType.DMA((3 * n,)), pltpu.SemaphoreType.DMA((3 * n,))],
    )(*mats)
    return list(outs)


def _swap_start(kinds, gms, name):
    n = len(gms)
    lands = [lax.empty((g.shape[0] // 2, g.shape[1]) if KIND_AXIS[k] == "col" else (g.shape[0], g.shape[1] // 2), BF16)
             for k, g in zip(kinds, gms)]

    def body(*refs):
        send_sems, recv_sems, token = refs[2 * n], refs[2 * n + 1], refs[-1]
        g, ra = refs[2 * n + 2:3 * n + 2], refs[3 * n + 2:4 * n + 2]

        def run(s, cc, c, sibling, chips):
            for a, k in enumerate(kinds):
                r, c_ = g[a].shape
                _remote(g[a].at[_half_win(KIND_AXIS[k], r, c_, 1 - cc)], ra[a], send_sems.at[a], recv_sems.at[a],
                        sibling).start()

        _per_place(run)
        token[...] = jnp.zeros_like(token)

    outs = pl.pallas_call(
        body, name=name, in_specs=[HBM] * (2 * n),
        out_specs=[SEM, SEM] + [HBM] * (2 * n) + [pl.BlockSpec(memory_space=pltpu.VMEM)],
        out_shape=[pltpu.SemaphoreType.DMA((n,)), pltpu.SemaphoreType.DMA((n,))]
        + [pltpu.HBM(a.shape, a.dtype) for a in gms + lands] + [jax.ShapeDtypeStruct((8, LANES), F32)],
        input_output_aliases={q: 2 + q for q in range(2 * n)},
        compiler_params=pltpu.CompilerParams(has_side_effects=EFFECT),
    )(*[_in_hbm(a) for a in gms + lands])
    return outs[0], outs[1], list(outs[2:2 + n]), list(outs[2 + n:2 + 2 * n]), outs[-1]


def _swap_wait(kinds, gms, lands, send_sems, recv_sems, after, name):
    n = len(gms)

    def body(*refs):
        g, ra = refs[:n], refs[n:2 * n]
        s_sems, r_sems = refs[2 * n], refs[2 * n + 1]

        def run(s, cc, c, sibling, chips):
            for a, k in enumerate(kinds):
                r, c_ = g[a].shape
                cp = _remote(g[a].at[_half_win(KIND_AXIS[k], r, c_, 1 - cc)], ra[a], s_sems.at[a], r_sems.at[a], sibling)
                cp.wait_send()
                cp.wait_recv()

        _per_place(run)

    outs = pl.pallas_call(
        body, name=name, in_specs=[HBM] * (2 * n) + [SEM, SEM, ANY], out_specs=[HBM] * (2 * n),
        out_shape=[pltpu.HBM(a.shape, a.dtype) for a in gms + lands], input_output_aliases={a: a for a in range(2 * n)},
        compiler_params=pltpu.CompilerParams(has_side_effects=EFFECT),
    )(*gms, *lands, send_sems, recv_sems, after)
    return list(outs[:n]), list(outs[n:])


def _win_shape(kind, pa, cw):
    return (pa.shape[0], cw) if KIND_AXIS[kind] == "col" else (cw, pa.shape[1])


def _scatter_start(kinds, pas, cws, name):
    n = len(pas)
    lands = [lax.empty((3,) + _win_shape(k, p, cws[k]), BF16) for k, p in zip(kinds, pas)]

    def body(*refs):
        send_sems, recv_sems, token = refs[2 * n], refs[2 * n + 1], refs[-1]
        p, rb = refs[2 * n + 2:3 * n + 2], refs[3 * n + 2:4 * n + 2]

        def run(s, cc, c, sibling, chips):
            for a, k in enumerate(kinds):
                for j, (px, py) in enumerate(chips):
                    src = p[a].at[_chip_win(KIND_AXIS[k], cws[k], s ^ PEER_MASKS[j])]
                    _remote(src, rb[a].at[j], send_sems.at[3 * a + j], recv_sems.at[3 * a + j], (px, py, c)).start()

        _per_place(run)
        token[...] = jnp.zeros_like(token)

    outs = pl.pallas_call(
        body, name=name, in_specs=[HBM] * (2 * n),
        out_specs=[SEM, SEM] + [HBM] * (2 * n) + [pl.BlockSpec(memory_space=pltpu.VMEM)],
        out_shape=[pltpu.SemaphoreType.DMA((3 * n,)), pltpu.SemaphoreType.DMA((3 * n,))]
        + [pltpu.HBM(a.shape, a.dtype) for a in pas + lands] + [jax.ShapeDtypeStruct((8, LANES), F32)],
        input_output_aliases={q: 2 + q for q in range(2 * n)},
        compiler_params=pltpu.CompilerParams(has_side_effects=EFFECT),
    )(*[_in_hbm(a) for a in pas + lands])
    return outs[0], outs[1], list(outs[2:2 + n]), list(outs[2 + n:2 + 2 * n]), outs[-1]


def _scatter_wait(kinds, pas, lands, send_sems, recv_sems, cws, after, name):
    n = len(pas)

    def body(*refs):
        p, rb = refs[:n], refs[n:2 * n]
        s_sems, r_sems = refs[2 * n], refs[2 * n + 1]

        def run(s, cc, c, sibling, chips):
            for a, k in enumerate(kinds):
                for j, (px, py) in enumerate(chips):
                    src = p[a].at[_chip_win(KIND_AXIS[k], cws[k], s ^ PEER_MASKS[j])]
                    cp = _remote(src, rb[a].at[j], s_sems.at[3 * a + j], r_sems.at[3 * a + j], (px, py, c))
                    cp.wait_send()
                    cp.wait_recv()

        _per_place(run)

    outs = pl.pallas_call(
        body, name=name, in_specs=[HBM] * (2 * n) + [SEM, SEM, ANY], out_specs=[HBM] * (2 * n),
        out_shape=[pltpu.HBM(a.shape, a.dtype) for a in pas + lands], input_output_aliases={a: a for a in range(2 * n)},
        compiler_params=pltpu.CompilerParams(has_side_effects=EFFECT),
    )(*pas, *lands, send_sems, recv_sems, after)
    return list(outs[:n]), list(outs[n:])


def _share_halves(g):
    nq = len(KINDS)

    def body(*refs):
        out = dict(zip(KINDS, refs[nq:2 * nq]))
        send_sems, recv_sems = refs[2 * nq:]

        def run(s, cc, c, sibling, chips):
            cps = []
            for q, k in enumerate(KINDS):
                _, r, c_ = out[k].shape
                mine = out[k].at[(ALL,) + _half_win(KIND_AXIS[k], r, c_, cc)]
                cps.append(_remote(mine, mine, send_sems.at[q], recv_sems.at[q], sibling))
            for cp in cps:
                cp.start()
            for cp in cps:
                cp.wait()

        _per_place(run)

    outs = pl.pallas_call(
        body, name="share_halves", in_specs=[ANY] * nq, out_specs=[ANY] * nq,
        out_shape=[jax.ShapeDtypeStruct(g[k].shape, F32) for k in KINDS],
        input_output_aliases={q: q for q in range(nq)},
        scratch_shapes=[pltpu.SemaphoreType.DMA((nq,)), pltpu.SemaphoreType.DMA((nq,))],
    )(*[g[k] for k in KINDS])
    return dict(zip(KINDS, outs))


def _all_gather_small(xs, name):
    m_per, n = xs.shape

    def body(x_ref, out_ref, send_sems, recv_sems, local_sem):
        x, y, c, sibling, chips = _place()
        me = (x, y, c)

        def rows(px, py, pc):
            return out_ref.at[pl.ds((4 * px + 2 * py + pc) * m_per, m_per), :]

        def copy(k, block, to, src=None):
            return _remote(rows(*block) if src is None else src, rows(*block), send_sems.at[k], recv_sems.at[k], to)

        mine = pltpu.make_async_copy(x_ref, rows(*me), local_sem)
        mine.start()
        first = [copy(0, me, sibling, src=x_ref)]
        first += [copy(1 + j, me, (*chip, c), src=x_ref) for j, chip in enumerate(chips)]
        for cp in first:
            cp.start()
        passed = [copy(4 + j, (*chip, c), sibling) for j, chip in enumerate(chips)]
        for j, chip in enumerate(chips):
            copy(1 + j, (*chip, c), me).wait_recv()
            passed[j].start()
        copy(0, sibling, me).wait_recv()
        for j, chip in enumerate(chips):
            copy(4 + j, (*chip, 1 - c), me).wait_recv()
        for cp in first + passed:
            cp.wait_send()
        mine.wait()

    vm = pl.BlockSpec(memory_space=pltpu.VMEM)
    return pl.pallas_call(
        body, name=name, in_specs=[vm], out_specs=vm, out_shape=jax.ShapeDtypeStruct((N_DEV * m_per, n), xs.dtype),
        scratch_shapes=[pltpu.SemaphoreType.DMA((7,)), pltpu.SemaphoreType.DMA((7,)), pltpu.SemaphoreType.DMA],
        compiler_params=pltpu.CompilerParams(vmem_limit_bytes=VMEM_LIMIT_BYTES),
    )(xs)


def _row_tile(rows, cap=512):
    for mult in (16, 8):
        best = None
        t = mult
        while t <= min(rows, cap):
            if rows % t == 0:
                best = t
            t += mult
        if best is not None:
            return best
    raise ValueError(rows)


def _add_sibling(kind, g, ra, core):
    R, C = ra.shape
    axis = KIND_AXIS[kind]
    tr, tc = _row_tile(R), _pick_tile(C, 2048)
    nr, nc = R // tr, C // tc

    def body(c_ref, a_ref, b_ref, o_ref):
        o_ref[...] = (a_ref[...].astype(F32) + b_ref[...].astype(F32)).astype(o_ref.dtype)

    if axis == "col":
        own = pl.BlockSpec((tr, tc), lambda i, j, c_ref: (c_ref[0] * nr + i, j))
    else:
        own = pl.BlockSpec((tr, tc), lambda i, j, c_ref: (i, c_ref[0] * nc + j))
    same = pl.BlockSpec((tr, tc), lambda i, j, c_ref: (i, j))
    grid_spec = pltpu.PrefetchScalarGridSpec(num_scalar_prefetch=1, grid=(nr, nc), in_specs=[own, same], out_specs=same)
    return pl.pallas_call(
        body, name="add_sibling_" + kind, grid_spec=grid_spec, out_shape=jax.ShapeDtypeStruct(ra.shape, BF16),
        compiler_params=_cparams("parallel", "parallel"))(core.reshape(1).astype(jnp.int32), g, ra)


def _sum_chips(kind, pa, rb, chip, core, out, layer):
    _, R, C = rb.shape
    axis = KIND_AXIS[kind]
    tr, tc = _row_tile(R), _pick_tile(C, 2048)
    nr, nc = R // tr, C // tc

    def body(s_ref, c_ref, a_ref, b0_ref, b1_ref, b2_ref, old_ref, o_ref):
        o_ref[...] = ((a_ref[...].astype(F32) + b0_ref[...].astype(F32)) + b1_ref[...].astype(F32)) + b2_ref[...].astype(F32)

    def rb_spec(n):
        return pl.BlockSpec((None, tr, tc), lambda i, j, s_ref, c_ref: (n, i, j))

    if axis == "col":
        own = pl.BlockSpec((tr, tc), lambda i, j, s_ref, c_ref: (i, s_ref[0] * nc + j))
        dst = pl.BlockSpec((None, tr, tc), lambda i, j, s_ref, c_ref: (layer, c_ref[0] * nr + i, j))
        assert out.shape[1:] == (2 * R, C)
    else:
        own = pl.BlockSpec((tr, tc), lambda i, j, s_ref, c_ref: (s_ref[0] * nr + i, j))
        dst = pl.BlockSpec((None, tr, tc), lambda i, j, s_ref, c_ref: (layer, i, c_ref[0] * nc + j))
        assert out.shape[1:] == (R, 2 * C)
    grid_spec = pltpu.PrefetchScalarGridSpec(
        num_scalar_prefetch=2, grid=(nr, nc), in_specs=[own, rb_spec(0), rb_spec(1), rb_spec(2), ANY], out_specs=dst)
    return pl.pallas_call(
        body, name="sum_chips_" + kind, grid_spec=grid_spec, out_shape=jax.ShapeDtypeStruct(out.shape, F32),
        input_output_aliases={6: 0}, compiler_params=_cparams("parallel", "parallel"))(
            chip.reshape(1).astype(jnp.int32), core.reshape(1).astype(jnp.int32), pa, rb, rb, rb, out)


def _sum_devices(gathered):
    M = gathered.shape[0] // N_DEV
    C = gathered.shape[1]

    def body(g_ref, o_ref):
        acc = g_ref[0:M, :]
        for d in range(1, N_DEV):
            acc = acc + g_ref[d * M:(d + 1) * M, :]
        o_ref[...] = acc

    vm = pl.BlockSpec(memory_space=pltpu.VMEM)
    return pl.pallas_call(body, name="sum_devices", in_specs=[vm], out_specs=vm, out_shape=jax.ShapeDtypeStruct((M, C), F32),
                          compiler_params=pltpu.CompilerParams(vmem_limit_bytes=VMEM_LIMIT_BYTES))(gathered)


WEIGHTS = ["mix_norm", "ffn_norm", "final_norm", "hgrn_w_in", "hgrn_lb_logits", "hgrn_gnorm", "hgrn_w_out", "m_w_in",
           "m_conv_w", "m_conv_b", "m_dt_bias", "m_A_log", "m_D", "m_norm", "m_w_out", "f_w_up", "f_conv_w", "f_conv_b",
           "f_w_down"]
BIG_COLS = ("hgrn_w_in", "m_w_in", "f_w_up")
BIG_ROWS = ("hgrn_w_out", "m_w_out", "f_w_down")
BIG = BIG_COLS + BIG_ROWS
SMALL_SHARDED = ("m_conv_w", "m_conv_b", "m_norm", "f_conv_w")
SMALL_REPLICATED = ("mix_norm", "ffn_norm", "final_norm", "hgrn_lb_logits", "hgrn_gnorm", "m_dt_bias", "m_A_log", "m_D",
                    "f_conv_b")
SMALL = SMALL_REPLICATED + SMALL_SHARDED


def _pack_rows(arrs, row_mult=8):
    flat = jnp.concatenate([a.reshape(-1).astype(F32) for a in arrs])
    unit = FLAT_COLS * row_mult
    n = -(-flat.size // unit) * unit
    return jnp.pad(flat, (0, n - flat.size)).reshape(-1, FLAT_COLS)


def _unpack_rows(buf, shapes):
    flat = buf.reshape(-1)
    out, off = [], 0
    for shp in shapes:
        n = math.prod(shp)
        out.append(flat[off:off + n].reshape(shp))
        off += n
    return out


def kernel(x, mix_norm, ffn_norm, final_norm, hgrn_w_in, hgrn_lb_logits, hgrn_gnorm, hgrn_w_out, m_w_in, m_conv_w, m_conv_b, m_dt_bias, m_A_log, m_D, m_norm, m_w_out, f_w_up, f_conv_w, f_conv_b, f_w_down, loss_target, m_mix_norm, m_ffn_norm, m_final_norm, m_hgrn_w_in, m_hgrn_lb_logits, m_hgrn_gnorm, m_hgrn_w_out, m_m_w_in, m_m_conv_w, m_m_conv_b, m_m_dt_bias, m_m_A_log, m_m_D, m_m_norm, m_m_w_out, m_f_w_up, m_f_conv_w, m_f_conv_b, m_f_w_down, v_mix_norm, v_ffn_norm, v_final_norm, v_hgrn_w_in, v_hgrn_lb_logits, v_hgrn_gnorm, v_hgrn_w_out, v_m_w_in, v_m_conv_w, v_m_conv_b, v_m_dt_bias, v_m_A_log, v_m_D, v_m_norm, v_m_w_out, v_f_w_up, v_f_conv_w, v_f_conv_b, v_f_w_down):
    given = dict(locals())
    w = {n: given[n] for n in WEIGHTS}
    mom1 = {n: given["m_" + n] for n in WEIGHTS}
    mom2 = {n: given["v_" + n] for n in WEIGHTS}
    chip = 2 * lax.axis_index("x") + lax.axis_index("y")
    core = lax.axis_index("c")

    shards = {k: w[k] for k in KINDS if k != "m_w_in_t"}
    shards["m_w_in_t"] = w["m_w_in"].transpose(0, 2, 1).astype(BF16)
    m_in = N_CHIPS * w["m_w_in"].shape[2]
    pad_rows = {"m_w_in_t": -(-m_in // LANES) * LANES - m_in}
    cws = {k: shards[k].shape[2] if KIND_AXIS[k] == "col" else shards[k].shape[1] for k in KINDS}
    depth = w["mix_norm"].shape[0]

    def layer_kinds(i):
        mixer = {"mix_in": ("hgrn_w_in", i // 2), "mix_out": ("hgrn_w_out", i // 2)} if i % 2 == 0 else \
                {"mix_in": ("m_w_in_t", i // 2), "mix_out": ("m_w_out", i // 2)}
        return {**mixer, "f_w_up": ("f_w_up", i), "f_w_down": ("f_w_down", i)}

    own = _pack_rows([w[n] for n in SMALL_SHARDED])
    all_small = _all_gather_small(own, "gather_small_params")
    groups, started = [list(layer_kinds(0).values()), [it for i in range(1, depth) for it in layer_kinds(i).values()]], []
    after = all_small
    for n_, items in enumerate(groups):
        staged = [_stage_shard(k, shards[k], l, chip, pad_rows.get(k, 0), dep=None if n_ == 0 else after) for k, l in items]
        send_sems, recv_sems, mats, after = _gather_start(items, staged, cws, after, "gather_start_%d" % n_)
        started.append((items, send_sems, recv_sems, mats))
    all_small = all_small.reshape(N_CHIPS, 2, -1)[:, 0]
    per_chip = [_unpack_rows(all_small[s], [w[n].shape for n in SMALL_SHARDED]) for s in range(N_CHIPS)]
    P = {}
    for i, n in enumerate(SMALL_SHARDED):
        P[n] = jnp.concatenate([per_chip[s][i] for s in range(N_CHIPS)], axis=-1)
    for n in SMALL_REPLICATED:
        P[n] = w[n]

    def fetch(i, keys, h):
        lk = {key: layer_kinds(i)[key] for key in keys}
        items, send_sems, recv_sems, mats = started[0 if i == 0 else 1]
        idx = [items.index(it) for it in lk.values()]
        tag = "%d_%s" % (i, keys[0])
        got = _gather_wait(items, idx, [mats[q] for q in idx], send_sems, recv_sems, cws, h, "gather_wait_" + tag)
        got = _forward_halves([k for k, _ in lk.values()], got, cws, "forward_halves_" + tag)
        return dict(zip(lk.keys(), got))

    pending = []

    def emit(i, gm):
        lk = {key: layer_kinds(i)[key] for key in gm}
        kinds = [k for k, _ in lk.values()]
        gms = list(gm.values())
        tag = "%d_%s" % (i, next(iter(gm)))
        s1, r1, gms, half_lands, tok = _swap_start(kinds, gms, "swap_start_" + tag)

        def finish(after):
            gms2, ra = _swap_wait(kinds, gms, half_lands, s1, r1, after, "swap_wait_" + tag)
            pas = [_add_sibling(k, g_, r_, core) for k, g_, r_ in zip(kinds, gms2, ra)]
            s_sems, r_sems, pas, lands, tok2 = _scatter_start(kinds, pas, cws, "scatter_start_" + tag)
            pending.append((tag, list(lk.values()), pas, lands, s_sems, r_sems))
            return tok2

        return tok, finish

    loss_part, grad_x, g_full = _local_step(x, loss_target, P, fetch, emit)

    g_sh = {k: lax.empty(shards[k].shape, F32) for k in KINDS}
    for tag, its, pas, lands, s_sems, r_sems in pending:
        kinds = [k for k, _ in its]
        pas, lands = _scatter_wait(kinds, pas, lands, s_sems, r_sems, cws, grad_x, "scatter_wait_" + tag)
        for (k, l), p_, rb_ in zip(its, pas, lands):
            g_sh[k] = _sum_chips(k, p_, rb_, chip, core, g_sh[k], l)
    g_sh = _share_halves(g_sh)
    grads = {k: g_sh[k] for k in KINDS if k != "m_w_in_t"}
    grads["m_w_in"] = g_sh["m_w_in_t"].transpose(0, 2, 1)

    small_shapes = [g_full[n].shape for n in SMALL] + [(1,)]
    packed = _pack_rows([g_full[n] for n in SMALL] + [loss_part[0, 0:1]])
    summed = _sum_devices(_all_gather_small(packed, "gather_small_grads"))
    small = _unpack_rows(summed, small_shapes)
    loss = small[-1][0]
    for n, gs in zip(SMALL, small[:-1]):
        if n in SMALL_SHARDED:
            width = w[n].shape[-1]
            gs = lax.dynamic_slice_in_dim(gs, chip * width, width, axis=gs.ndim - 1)
        grads[n] = gs

    delta, new_m, new_v = {}, {}, {}
    for n in BIG:
        delta[n], new_m[n], new_v[n] = _adamw(w[n], grads[n], mom1[n], mom2[n], "adamw_" + n)
    shapes = [w[n].shape for n in SMALL]
    ds, ms, vs = _adamw(_pack_rows([w[n] for n in SMALL]), _pack_rows([grads[n] for n in SMALL]),
                        _pack_rows([mom1[n] for n in SMALL]), _pack_rows([mom2[n] for n in SMALL]), "adamw_small")
    for n, d_, m_, v_ in zip(SMALL, _unpack_rows(ds, shapes), _unpack_rows(ms, shapes), _unpack_rows(vs, shapes)):
        delta[n], new_m[n], new_v[n] = d_, m_, v_

    return (loss, grad_x, *[grads[n] for n in WEIGHTS], *[delta[n] for n in WEIGHTS], *[new_m[n] for n in WEIGHTS],
            *[new_v[n] for n in WEIGHTS])
```

```python
import functools
import math

import jax
import jax.numpy as jnp
from jax import lax
from jax.experimental import pallas as pl
from jax.experimental.pallas import tpu as pltpu

F32 = jnp.float32
BF16 = jnp.bfloat16
NORM_EPS = 1e-5
HGRN_DK = 128
HGRN_CHUNK = 64
HGRN_HEADS_PER_STEP = 4
HGRN_SEQ_BLOCK = 512
M_HEADDIM = 64
M_GROUPS = 8
M_D_STATE = 128
M_CONV = 4
M_CHUNK = 128
FFN_CONV = 3
EXP_CLIP = 80.0
LANES = 128
VMEM_LIMIT_BYTES = 56 * 1024 * 1024
FLAT_COLS = 1024
ADAM_LR, ADAM_B1, ADAM_B2, ADAM_EPS, ADAM_WD, ADAM_STEP = 0.001, 0.9, 0.999, 1e-08, 0.01, 10
MESH = pl.DeviceIdType.MESH

NN = (((1,), (0,)), ((), ()))
NT = (((1,), (1,)), ((), ()))
TN = (((0,), (0,)), ((), ()))


def _cparams(*sems):
    return pltpu.CompilerParams(dimension_semantics=sems, vmem_limit_bytes=VMEM_LIMIT_BYTES)


def _dot(a, b, dn=NN):
    return lax.dot_general(a.astype(BF16), b.astype(BF16), dn, preferred_element_type=F32)


def _dot_exact(x, m, dn=NN, passes=3, x_first=True):
    acc = None
    r = x
    for _ in range(passes):
        p = r.astype(BF16)
        r = r - p.astype(F32)
        t = lax.dot_general(p, m, dn, preferred_element_type=F32) if x_first else lax.dot_general(m, p, dn, preferred_element_type=F32)
        acc = t if acc is None else acc + t
    return acc


def _iota(shape, dim):
    return lax.broadcasted_iota(jnp.int32, shape, dim)


def _cumsum_rows(x, reverse=False):
    n = x.shape[0]
    row = _iota(x.shape, 0)
    s = 1
    while s < n:
        if reverse:
            x = x + jnp.where(row < n - s, pltpu.roll(x, n - s, 0), 0.0)
        else:
            x = x + jnp.where(row >= s, pltpu.roll(x, s, 0), 0.0)
        s *= 2
    return x


def _silu(x):
    return x * jax.nn.sigmoid(x)


def _dsilu(x):
    s = jax.nn.sigmoid(x)
    return s * (1.0 + x * (1.0 - s))


def _pick_tile(dim, pref):
    if dim <= pref:
        return dim
    best = None
    t = LANES
    while t <= pref:
        if dim % t == 0:
            best = t
        t += LANES
    assert best is not None, (dim, pref)
    return best


def _matmul(a, b, *, ta=False, tb=False, res=None, out_dtype=F32, tm=1024, tn=1024, tk=4096, name,
            a_parts=False, b_parts=False, b_layer=None, b_off=0, out=None, out_layer=None, out_off=0, dep=None):
    a = a.astype(BF16)
    b = b.astype(BF16)
    if a_parts:
        assert not ta
        pa, M, kp = a.shape
        K = pa * kp
    else:
        M, K = (a.shape[1], a.shape[0]) if ta else a.shape
    bsh = b.shape[1:] if b_layer is not None else b.shape
    if b_parts:
        assert not tb
        pb, _, np_ = bsh
        N = pb * np_
    else:
        N = bsh[0] if tb else bsh[1]
    tm, tn, tk = _pick_tile(M, tm), _pick_tile(np_ if b_parts else N, tn), _pick_tile(kp if a_parts else K, tk)
    nk = K // tk
    dn = (((0 if ta else 1,), (1 if tb else 0,)), ((), ()))
    assert b_off % tk == 0 and out_off % tm == 0

    def body(*refs):
        refs = list(refs)
        acc = refs.pop() if nk > 1 else None
        o_ref = refs.pop()
        if dep is not None:
            refs.pop()
        if out is not None:
            refs.pop()
        a_ref, b_ref = refs[0], refs[1]
        r_ref = refs[2] if res is not None else None
        k = pl.program_id(2)

        def prod():
            return lax.dot_general(a_ref[...], b_ref[...], dn, preferred_element_type=F32)

        def finish(r):
            if res is not None:
                r = r + r_ref[...]
            o_ref[...] = r.astype(out_dtype)

        if nk == 1:
            finish(prod())
            return

        @pl.when(k == 0)
        def _():
            acc[...] = prod()

        @pl.when(jnp.logical_and(k > 0, k < nk - 1))
        def _():
            acc[...] += prod()

        @pl.when(k == nk - 1)
        def _():
            finish(acc[...] + prod())

    if a_parts:
        kpb = kp // tk
        a_spec = pl.BlockSpec((None, tm, tk), lambda i, j, k: (k // kpb, i, k % kpb))
    elif ta:
        a_spec = pl.BlockSpec((tk, tm), lambda i, j, k: (k, i))
    else:
        a_spec = pl.BlockSpec((tm, tk), lambda i, j, k: (i, k))
    lead = () if b_layer is None else (b_layer,)
    lead_blk = () if b_layer is None else (None,)
    kb0 = b_off // tk
    if b_parts:
        npb = np_ // tn
        b_spec = pl.BlockSpec(lead_blk + (None, tk, tn), lambda i, j, k: lead + (j // npb, k, j % npb))
    elif tb:
        b_spec = pl.BlockSpec(lead_blk + (tn, tk), lambda i, j, k: lead + (j, k))
    else:
        b_spec = pl.BlockSpec(lead_blk + (tk, tn), lambda i, j, k: lead + (kb0 + k, j))
    r_spec = pl.BlockSpec((tm, tn), lambda i, j, k: (i, j))
    in_specs = [a_spec, b_spec] + ([r_spec] if res is not None else [])
    args = (a, b) + ((res,) if res is not None else ())
    if out is None:
        o_spec, out_shape, aliases = r_spec, jax.ShapeDtypeStruct((M, N), out_dtype), {}
    else:
        assert out.dtype == out_dtype and out.shape[-1] == N
        olead = () if out_layer is None else (out_layer,)
        olead_blk = () if out_layer is None else (None,)
        ob0 = out_off // tm
        o_spec = pl.BlockSpec(olead_blk + (tm, tn), lambda i, j, k: olead + (ob0 + i, j))
        out_shape = jax.ShapeDtypeStruct(out.shape, out.dtype)
        aliases = {len(args): 0}
        in_specs = in_specs + [pl.BlockSpec(memory_space=pl.ANY)]
        args = args + (out,)
    if dep is not None:
        in_specs = in_specs + [pl.BlockSpec(memory_space=pl.ANY)]
        args = args + (dep,)
    return pl.pallas_call(
        body, name=name, grid=(M // tm, N // tn, nk), in_specs=in_specs, out_specs=o_spec, out_shape=out_shape,
        scratch_shapes=[pltpu.VMEM((tm, tn), F32)] if nk > 1 else [], input_output_aliases=aliases,
        compiler_params=_cparams("parallel", "parallel", "arbitrary"))(*args)


def _rmsnorm_fwd(h, w, name):
    T, D = h.shape
    tm = _pick_tile(T, 256)

    def body(h_ref, w_ref, u_ref):
        x = h_ref[...]
        r = lax.rsqrt(jnp.mean(x * x, axis=-1, keepdims=True) + NORM_EPS)
        u_ref[...] = (x * r * w_ref[...]).astype(BF16)

    return pl.pallas_call(
        body, name=name, grid=(T // tm,),
        in_specs=[pl.BlockSpec((tm, D), lambda i: (i, 0)), pl.BlockSpec((1, D), lambda i: (0, 0))],
        out_specs=pl.BlockSpec((tm, D), lambda i: (i, 0)), out_shape=jax.ShapeDtypeStruct((T, D), BF16),
        compiler_params=_cparams("parallel"))(h, w.reshape(1, D))


def _rmsnorm_bwd(h, w, du, dres, name, dep=None):
    T, D = h.shape
    tm = _pick_tile(T, 256)

    def body(h_ref, w_ref, du_ref, dr_ref, *rest):
        dh_ref, dhb_ref, dw_ref = rest[-3:]
        x = h_ref[...]
        r = lax.rsqrt(jnp.mean(x * x, axis=-1, keepdims=True) + NORM_EPS)
        xh = x * r
        du_ = du_ref[...]
        dy = du_ * w_ref[...]
        dh = dr_ref[...] + r * (dy - xh * jnp.mean(dy * xh, axis=-1, keepdims=True))
        dh_ref[...] = dh
        dhb_ref[...] = dh.astype(BF16)
        part = jnp.sum(du_ * xh, axis=0, keepdims=True)

        @pl.when(pl.program_id(0) == 0)
        def _():
            dw_ref[...] = part

        @pl.when(pl.program_id(0) > 0)
        def _():
            dw_ref[...] += part

    row = pl.BlockSpec((tm, D), lambda i: (i, 0))
    vec = pl.BlockSpec((1, D), lambda i: (0, 0))
    extra_specs, extra = ([], ()) if dep is None else ([pl.BlockSpec(memory_space=pl.ANY)], (dep,))
    dh, dhb, dw = pl.pallas_call(
        body, name=name, grid=(T // tm,), in_specs=[row, vec, row, row] + extra_specs, out_specs=[row, row, vec],
        out_shape=[jax.ShapeDtypeStruct((T, D), F32), jax.ShapeDtypeStruct((T, D), BF16), jax.ShapeDtypeStruct((1, D), F32)],
        compiler_params=_cparams("arbitrary"))(h, w.reshape(1, D), du, dres, *extra)
    return dh, dhb, dw.reshape(D)


def _loss_head(h, w, target):
    T, D = h.shape
    tm = _pick_tile(T, 256)

    def body(h_ref, w_ref, t_ref, loss_ref, dh_ref, dhb_ref, dw_ref):
        x = h_ref[...]
        wv = w_ref[...]
        r = lax.rsqrt(jnp.mean(x * x, axis=-1, keepdims=True) + NORM_EPS)
        xh = x * r
        e = xh * wv - t_ref[...]
        lpart = jnp.zeros((1, LANES), F32) + 0.5 * jnp.sum(jnp.mean(e * e, axis=-1, keepdims=True))
        dyo = e * (1.0 / D)
        dy = dyo * wv
        dh = r * (dy - xh * jnp.mean(dy * xh, axis=-1, keepdims=True))
        dh_ref[...] = dh
        dhb_ref[...] = dh.astype(BF16)
        part = jnp.sum(dyo * xh, axis=0, keepdims=True)

        @pl.when(pl.program_id(0) == 0)
        def _():
            dw_ref[...] = part
            loss_ref[...] = lpart

        @pl.when(pl.program_id(0) > 0)
        def _():
            dw_ref[...] += part
            loss_ref[...] += lpart

    row = pl.BlockSpec((tm, D), lambda i: (i, 0))
    vec = pl.BlockSpec((1, D), lambda i: (0, 0))
    lvec = pl.BlockSpec((1, LANES), lambda i: (0, 0))
    loss, dh, dhb, dw = pl.pallas_call(
        body, name="loss_head", grid=(T // tm,), in_specs=[row, vec, row], out_specs=[lvec, row, row, vec],
        out_shape=[jax.ShapeDtypeStruct((1, LANES), F32), jax.ShapeDtypeStruct((T, D), F32),
                   jax.ShapeDtypeStruct((T, D), BF16), jax.ShapeDtypeStruct((1, D), F32)],
        compiler_params=_cparams("arbitrary"))(h, w.reshape(1, D), target)
    return loss, dh, dhb, dw.reshape(D)


def _hgrn_gates(qr, fr, lb):
    sig = jax.nn.sigmoid(fr)
    nsig = jax.nn.sigmoid(-fr)
    fg = lb + (1.0 - lb) * sig
    logf = jnp.log(fg)
    k = (1.0 - lb) * nsig
    q = _silu(qr)
    return q, k, logf, sig, nsig, fg


def _hgrn_scaled(q, k, b, bmid):
    eq = jnp.exp(jnp.clip(b - bmid, -EXP_CLIP, EXP_CLIP))
    ek = jnp.exp(jnp.clip(bmid - b, -EXP_CLIP, EXP_CLIP))
    return q * eq, k * ek, eq, ek


def _hgrn_fwd(proj, lb, gnw, H):
    B, L, _ = proj.shape
    C, DK = HGRN_CHUNK, HGRN_DK
    F_ = H * DK
    NC = L // C

    nh = HGRN_HEADS_PER_STEP if H % HGRN_HEADS_PER_STEP == 0 else 1
    LB = min(L, HGRN_SEQ_BLOCK)
    ncb, nsb, WD = LB // C, L // LB, nh * DK

    def body(q_ref, f_ref, v_ref, g_ref, lb_ref, gn_ref, o_ref, on_ref, st_ref, ST, bsc):
        @pl.when(pl.program_id(2) == 0)
        def _():
            ST[...] = jnp.zeros_like(ST)

        gn = gn_ref[...]
        causal = _iota((C, C), 0) >= _iota((C, C), 1)

        def chunk(c, carry):
            r0 = pl.multiple_of(c * C, C)
            rows = pl.ds(r0, C)
            for hh in range(nh):
                ln = slice(hh * DK, (hh + 1) * DK)
                q, k, logf, _, _, _ = _hgrn_gates(q_ref[0, rows, ln], f_ref[0, rows, ln], lb_ref[:, ln])
                v = v_ref[0, rows, ln]
                b = _cumsum_rows(logf)
                bsc[hh] = b
                bmid = bsc[hh, C // 2 - 1:C // 2, :]
                blast = bsc[hh, C - 1:C, :]
                qs, ks, _, _ = _hgrn_scaled(q, k, b, bmid)
                A = jnp.where(causal, _dot(qs, ks, NT), 0.0)
                st = ST[hh]
                st_ref[0, hh, c] = st
                o = _dot(A, v) + _dot(q * jnp.exp(b), st, NT)
                kb = k * jnp.exp(blast - b)
                ST[hh] = st * jnp.exp(blast) + _dot(v, kb, TN)
                rms = lax.rsqrt(jnp.mean(o * o, axis=-1, keepdims=True) + NORM_EPS)
                o_ref[0, rows, ln] = o
                on_ref[0, rows, ln] = (o * rms * gn * _silu(g_ref[0, rows, ln])).astype(BF16)
            return carry

        lax.fori_loop(0, ncb, chunk, 0)

    def col(off):
        return pl.BlockSpec((1, LB, WD), lambda b, hp, s: (b, s, off // nh + hp))

    return pl.pallas_call(
        body, name="hgrn_fwd", grid=(B, H // nh, nsb),
        in_specs=[col(0), col(H), col(2 * H), col(3 * H), pl.BlockSpec((1, WD), lambda b, hp, s: (0, hp)),
                  pl.BlockSpec((1, DK), lambda b, hp, s: (0, 0))],
        out_specs=[col(0), col(0), pl.BlockSpec((1, nh, ncb, DK, DK), lambda b, hp, s: (b, hp, s, 0, 0))],
        out_shape=[jax.ShapeDtypeStruct((B, L, F_), F32), jax.ShapeDtypeStruct((B, L, F_), BF16),
                   jax.ShapeDtypeStruct((B, H, NC, DK, DK), F32)],
        scratch_shapes=[pltpu.VMEM((nh, DK, DK), F32), pltpu.VMEM((nh, C, DK), F32)],
        compiler_params=_cparams("parallel", "parallel", "arbitrary"))(proj, proj, proj, proj, lb, gnw)


def _hgrn_bwd(proj, o, don, st, lb, gnw, H):
    B, L, _ = proj.shape
    C, DK = HGRN_CHUNK, HGRN_DK
    F_ = H * DK
    NC = L // C

    nh = HGRN_HEADS_PER_STEP if H % HGRN_HEADS_PER_STEP == 0 else 1
    LB = min(L, HGRN_SEQ_BLOCK)
    ncb, nsb, WD = LB // C, L // LB, nh * DK

    def body(q_ref, f_ref, v_ref, g_ref, o_ref, do_ref, st_ref, lb_ref, gn_ref,
             dp_ref, dlb_ref, dgn_ref, DST, bsc):
        @pl.when(pl.program_id(2) == 0)
        def _():
            DST[...] = jnp.zeros_like(DST)
            dlb_ref[...] = jnp.zeros_like(dlb_ref)
            dgn_ref[...] = jnp.zeros_like(dgn_ref)

        gn = gn_ref[...]
        causal = _iota((C, C), 0) >= _iota((C, C), 1)
        lastrow = _iota((C, DK), 0) == C - 1

        def chunk(i, carry):
            c = ncb - 1 - i
            r0 = pl.multiple_of(c * C, C)
            rows = pl.ds(r0, C)
            for hh in range(nh):
                ln = slice(hh * DK, (hh + 1) * DK)
                lbv = lb_ref[:, ln]
                qr = q_ref[0, rows, ln]
                fr = f_ref[0, rows, ln]
                q, k, logf, sig, nsig, fg = _hgrn_gates(qr, fr, lbv)
                v = v_ref[0, rows, ln]
                b = _cumsum_rows(logf)
                bsc[hh] = b
                bmid = bsc[hh, C // 2 - 1:C // 2, :]
                blast = bsc[hh, C - 1:C, :]
                qs, ks, eq, ek = _hgrn_scaled(q, k, b, bmid)
                A = jnp.where(causal, _dot(qs, ks, NT), 0.0)
                st_in = st_ref[0, hh, c]
                dst = DST[hh]
                eb = jnp.exp(b)
                ebl = jnp.exp(blast)
                ekb = jnp.exp(blast - b)
                qb = q * eb
                kb = k * ekb
                ov = o_ref[0, rows, ln]
                gr = g_ref[0, rows, ln]
                rms = lax.rsqrt(jnp.mean(ov * ov, axis=-1, keepdims=True) + NORM_EPS)
                oh = ov * rms
                sg = _silu(gr)
                don_ = do_ref[0, rows, ln]
                dgn_ref[0, hh] += jnp.sum(don_ * oh * sg, axis=0, keepdims=True)
                dp_ref[3, 0, rows, ln] = (don_ * oh * gn * _dsilu(gr)).astype(BF16)
                doh = don_ * gn * sg
                do_ = rms * (doh - oh * jnp.mean(doh * oh, axis=-1, keepdims=True))
                dA = jnp.where(causal, _dot(do_, v, NT), 0.0)
                dp_ref[2, 0, rows, ln] = (_dot(A, do_, TN) + _dot(kb, dst, NT)).astype(BF16)
                dqb = _dot(do_, st_in)
                dkb = _dot(v, dst)
                dq = _dot(dA, ks) * eq + dqb * eb
                dk_inter = dkb * ekb
                dk = _dot(dA, qs, TN) * ek + dk_inter
                db = q * dq - k * dk
                extra = jnp.sum(k * dk_inter, axis=0, keepdims=True) + ebl * jnp.sum(st_in * dst, axis=0, keepdims=True)
                db = db + jnp.where(lastrow, extra, 0.0)
                dlogf = _cumsum_rows(db, reverse=True)
                DST[hh] = dst * ebl + _dot(do_, qb, TN)
                dp_ref[0, 0, rows, ln] = (dq * _dsilu(qr)).astype(BF16)
                ss = sig * nsig
                dp_ref[1, 0, rows, ln] = ((1.0 - lbv) * ss * (dlogf / fg - dk)).astype(BF16)
                dlb_ref[0, :, ln] += jnp.sum(dlogf * nsig / fg - dk * nsig, axis=0, keepdims=True)
            return carry

        lax.fori_loop(0, ncb, chunk, 0)

    def col(off):
        return pl.BlockSpec((1, LB, WD), lambda b, hp, s: (b, nsb - 1 - s, off // nh + hp))

    outs = pl.pallas_call(
        body, name="hgrn_bwd", grid=(B, H // nh, nsb),
        in_specs=[col(0), col(H), col(2 * H), col(3 * H), col(0), col(0),
                  pl.BlockSpec((1, nh, ncb, DK, DK), lambda b, hp, s: (b, hp, nsb - 1 - s, 0, 0)),
                  pl.BlockSpec((1, WD), lambda b, hp, s: (0, hp)), pl.BlockSpec((1, DK), lambda b, hp, s: (0, 0))],
        out_specs=[pl.BlockSpec((4, 1, LB, WD), lambda b, hp, s: (0, b, nsb - 1 - s, hp)),
                   pl.BlockSpec((1, 1, WD), lambda b, hp, s: (b, 0, hp)),
                   pl.BlockSpec((1, nh, 1, DK), lambda b, hp, s: (b, hp, 0, 0))],
        out_shape=[jax.ShapeDtypeStruct((4, B, L, F_), BF16), jax.ShapeDtypeStruct((B, 1, F_), F32),
                   jax.ShapeDtypeStruct((B, H, 1, DK), F32)],
        scratch_shapes=[pltpu.VMEM((nh, DK, DK), F32), pltpu.VMEM((nh, C, DK), F32)],
        compiler_params=_cparams("parallel", "parallel", "arbitrary"))(proj, proj, proj, proj, o, don, st, lb, gnw)
    return outs


CONV_ROWS = 256
PAD_ROWS = 8


def _conv_taps(pad_ref, w_ref, r0, K, rb, forward=True, keep=False):
    ext = pad_ref[pl.ds(r0, rb + PAD_ROWS), :]
    n = rb + PAD_ROWS
    acc = None
    shifts = []
    for s in range(K):
        if forward:
            sh = ext if s == 0 else pltpu.roll(ext, s, 0)
            term = sh[PAD_ROWS:, :]
        else:
            sh = ext if s == 0 else pltpu.roll(ext, n - s, 0)
            term = sh[:rb, :]
        shifts.append(term)
        term = term * w_ref[K - 1 - s:K - s, :]
        acc = term if acc is None else acc + term
    return (acc, shifts) if keep else acc


def _conv_dw(shifts, dc, K):
    row = _iota((8, dc.shape[1]), 0)
    out = jnp.zeros((8, dc.shape[1]), F32)
    for kk in range(K):
        out = out + jnp.where(row == kk, jnp.sum(dc * shifts[K - 1 - kk], axis=0, keepdims=True), 0.0)
    return out


def _mconv_fwd(zx, cw, cb, col0, width):
    B, L, _ = zx.shape
    K = cw.shape[0]
    ct = _pick_tile(width, 256)
    rb = min(CONV_ROWS, L)
    nrb = L // rb
    off = col0 // ct

    def body(x_ref, w_ref, b_ref, y_ref, xp):
        xp[0:PAD_ROWS, :] = jnp.zeros((PAD_ROWS, ct), F32)
        xp[PAD_ROWS:, :] = x_ref[0]
        bias = b_ref[...]

        def blk(i, carry):
            r0 = pl.multiple_of(i * rb, rb)
            y_ref[0, pl.ds(r0, rb), :] = _silu(_conv_taps(xp, w_ref, r0, K, rb) + bias)
            return carry

        lax.fori_loop(0, nrb, blk, 0)

    return pl.pallas_call(
        body, name="mconv_fwd", grid=(B, width // ct),
        in_specs=[pl.BlockSpec((1, L, ct), lambda b, j: (b, 0, off + j)), pl.BlockSpec((K, ct), lambda b, j: (0, j)),
                  pl.BlockSpec((1, ct), lambda b, j: (0, j))],
        out_specs=pl.BlockSpec((1, L, ct), lambda b, j: (b, 0, j)),
        out_shape=jax.ShapeDtypeStruct((B, L, width), F32),
        scratch_shapes=[pltpu.VMEM((L + PAD_ROWS, ct), F32)],
        compiler_params=_cparams("parallel", "parallel"))(zx, cw, cb.reshape(1, width))


def _mconv_bwd(zx, dya, cw, cb, col0, wcol0, name):
    B, L, _ = zx.shape
    K = cw.shape[0]
    npart, _, _, wq = dya.shape
    width = npart * wq
    ct = _pick_tile(wq, 256)
    rb = min(CONV_ROWS, L)
    nrb = L // rb
    off = (col0 + wcol0) // ct
    woff = wcol0 // ct
    pq = wq // ct

    def body(x_ref, dy_ref, w_ref, b_ref, dx_ref, dw_ref, db_ref, xp, dcp):
        xp[0:PAD_ROWS, :] = jnp.zeros((PAD_ROWS, ct), F32)
        xp[PAD_ROWS:, :] = x_ref[0]
        dcp[L:, :] = jnp.zeros((PAD_ROWS, ct), F32)
        bias = b_ref[...]

        def blk1(i, carry):
            dw, db = carry
            r0 = pl.multiple_of(i * rb, rb)
            cpre, shifts = _conv_taps(xp, w_ref, r0, K, rb, keep=True)
            dc = dy_ref[0, 0, pl.ds(r0, rb), :] * _dsilu(cpre + bias)
            dcp[pl.ds(r0, rb), :] = dc
            return dw + _conv_dw(shifts, dc, K), db + jnp.sum(dc, axis=0, keepdims=True)

        dw, db = lax.fori_loop(0, nrb, blk1, (jnp.zeros((8, ct), F32), jnp.zeros((1, ct), F32)))
        dw_ref[0] = dw
        db_ref[0] = db

        def blk2(i, carry):
            r0 = pl.multiple_of(i * rb, rb)
            dx_ref[0, pl.ds(r0, rb), :] = _conv_taps(dcp, w_ref, r0, K, rb, forward=False).astype(BF16)
            return carry

        lax.fori_loop(0, nrb, blk2, 0)

    dx, dw, db = pl.pallas_call(
        body, name=name, grid=(B, width // ct),
        in_specs=[pl.BlockSpec((1, L, ct), lambda b, j: (b, 0, off + j)),
                  pl.BlockSpec((1, 1, L, ct), lambda b, j: (j // pq, b, 0, j % pq)),
                  pl.BlockSpec((K, ct), lambda b, j: (0, woff + j)), pl.BlockSpec((1, ct), lambda b, j: (0, woff + j))],
        out_specs=[pl.BlockSpec((1, L, ct), lambda b, j: (b, 0, j)), pl.BlockSpec((1, 8, ct), lambda b, j: (b, 0, j)),
                   pl.BlockSpec((1, 1, ct), lambda b, j: (b, 0, j))],
        out_shape=[jax.ShapeDtypeStruct((B, L, width), BF16), jax.ShapeDtypeStruct((B, 8, width), F32),
                   jax.ShapeDtypeStruct((B, 1, width), F32)],
        scratch_shapes=[pltpu.VMEM((L + PAD_ROWS, ct), F32), pltpu.VMEM((L + PAD_ROWS, ct), F32)],
        compiler_params=_cparams("parallel", "parallel"))(zx, dya, cw, cb.reshape(1, -1))
    return dx, dw[:, :K, :], db


def _ffn_mid_fwd(up, cw, cb, dff):
    B, L, _ = up.shape
    K = cw.shape[0]
    ct = _pick_tile(dff, 256)
    rb = min(CONV_ROWS, L)
    nrb = L // rb
    half = dff // ct

    def body(g_ref, u_ref, wg_ref, wu_ref, bg_ref, bu_ref, a_ref, gp, upad):
        gp[0:PAD_ROWS, :] = jnp.zeros((PAD_ROWS, ct), F32)
        upad[0:PAD_ROWS, :] = jnp.zeros((PAD_ROWS, ct), F32)
        gp[PAD_ROWS:, :] = g_ref[0]
        upad[PAD_ROWS:, :] = u_ref[0]
        bg, bu = bg_ref[...], bu_ref[...]

        def blk(i, carry):
            r0 = pl.multiple_of(i * rb, rb)
            cg = _conv_taps(gp, wg_ref, r0, K, rb) + bg
            cu = _conv_taps(upad, wu_ref, r0, K, rb) + bu
            a_ref[0, pl.ds(r0, rb), :] = (_silu(cg) * cu).astype(BF16)
            return carry

        lax.fori_loop(0, nrb, blk, 0)

    xg = pl.BlockSpec((1, L, ct), lambda b, j: (b, 0, j))
    xu = pl.BlockSpec((1, L, ct), lambda b, j: (b, 0, half + j))
    wgs = pl.BlockSpec((K, ct), lambda b, j: (0, j))
    wus = pl.BlockSpec((K, ct), lambda b, j: (0, half + j))
    bgs = pl.BlockSpec((1, ct), lambda b, j: (0, j))
    bus = pl.BlockSpec((1, ct), lambda b, j: (0, half + j))
    cb2 = cb.reshape(1, 2 * dff)
    return pl.pallas_call(
        body, name="ffn_mid_fwd", grid=(B, half), in_specs=[xg, xu, wgs, wus, bgs, bus], out_specs=xg,
        out_shape=jax.ShapeDtypeStruct((B, L, dff), BF16),
        scratch_shapes=[pltpu.VMEM((L + PAD_ROWS, ct), F32), pltpu.VMEM((L + PAD_ROWS, ct), F32)],
        compiler_params=_cparams("parallel", "parallel"))(up, up, cw, cw, cb2, cb2)


def _ffn_mid_bwd(up, dact, cw, cb, dff):
    B, L, _ = up.shape
    K = cw.shape[0]
    ct = _pick_tile(dff, 256)
    rb = min(CONV_ROWS, L)
    nrb = L // rb
    half = dff // ct

    def body(g_ref, u_ref, da_ref, wg_ref, wu_ref, bg_ref, bu_ref, dx_ref, dwg_ref, dwu_ref, dbg_ref, dbu_ref,
             gp, upad, dgp, dup):
        gp[0:PAD_ROWS, :] = jnp.zeros((PAD_ROWS, ct), F32)
        upad[0:PAD_ROWS, :] = jnp.zeros((PAD_ROWS, ct), F32)
        gp[PAD_ROWS:, :] = g_ref[0]
        upad[PAD_ROWS:, :] = u_ref[0]
        dgp[L:, :] = jnp.zeros((PAD_ROWS, ct), F32)
        dup[L:, :] = jnp.zeros((PAD_ROWS, ct), F32)
        bg, bu = bg_ref[...], bu_ref[...]

        def blk1(i, carry):
            dwg, dwu, dbg, dbu = carry
            r0 = pl.multiple_of(i * rb, rb)
            cg, sg_ = _conv_taps(gp, wg_ref, r0, K, rb, keep=True)
            cu, su_ = _conv_taps(upad, wu_ref, r0, K, rb, keep=True)
            cg = cg + bg
            cu = cu + bu
            da = da_ref[0, pl.ds(r0, rb), :]
            sig = jax.nn.sigmoid(cg)
            dcg = da * cu * (sig * (1.0 + cg * (1.0 - sig)))
            dcu = da * (cg * sig)
            dgp[pl.ds(r0, rb), :] = dcg
            dup[pl.ds(r0, rb), :] = dcu
            return (dwg + _conv_dw(sg_, dcg, K), dwu + _conv_dw(su_, dcu, K), dbg + jnp.sum(dcg, axis=0, keepdims=True),
                    dbu + jnp.sum(dcu, axis=0, keepdims=True))

        z8 = jnp.zeros((8, ct), F32)
        z1 = jnp.zeros((1, ct), F32)
        dwg, dwu, dbg, dbu = lax.fori_loop(0, nrb, blk1, (z8, z8, z1, z1))
        dwg_ref[0] = dwg
        dwu_ref[0] = dwu
        dbg_ref[0] = dbg
        dbu_ref[0] = dbu

        def blk2(i, carry):
            r0 = pl.multiple_of(i * rb, rb)
            dx_ref[0, 0, pl.ds(r0, rb), :] = _conv_taps(dgp, wg_ref, r0, K, rb, forward=False).astype(BF16)
            dx_ref[1, 0, pl.ds(r0, rb), :] = _conv_taps(dup, wu_ref, r0, K, rb, forward=False).astype(BF16)
            return carry

        lax.fori_loop(0, nrb, blk2, 0)

    xg = pl.BlockSpec((1, L, ct), lambda b, j: (b, 0, j))
    xu = pl.BlockSpec((1, L, ct), lambda b, j: (b, 0, half + j))
    wgs = pl.BlockSpec((K, ct), lambda b, j: (0, j))
    wus = pl.BlockSpec((K, ct), lambda b, j: (0, half + j))
    bgs = pl.BlockSpec((1, ct), lambda b, j: (0, j))
    bus = pl.BlockSpec((1, ct), lambda b, j: (0, half + j))
    w8 = pl.BlockSpec((1, 8, ct), lambda b, j: (b, 0, j))
    b1 = pl.BlockSpec((1, 1, ct), lambda b, j: (b, 0, j))
    cb2 = cb.reshape(1, 2 * dff)
    pad = pltpu.VMEM((L + PAD_ROWS, ct), F32)
    dx2, dwg, dwu, dbg, dbu = pl.pallas_call(
        body, name="ffn_mid_bwd", grid=(B, half), in_specs=[xg, xu, xg, wgs, wus, bgs, bus],
        out_specs=[pl.BlockSpec((2, 1, L, ct), lambda b, j: (0, b, 0, j)), w8, w8, b1, b1],
        out_shape=[jax.ShapeDtypeStruct((2, B, L, dff), BF16)] + [jax.ShapeDtypeStruct((B, 8, dff), F32)] * 2
        + [jax.ShapeDtypeStruct((B, 1, dff), F32)] * 2,
        scratch_shapes=[pad, pad, pad, pad],
        compiler_params=_cparams("parallel", "parallel"))(up, up, dact, cw, cw, cb2, cb2)
    dw = jnp.concatenate([dwg[:, :K], dwu[:, :K]], axis=-1)
    db = jnp.concatenate([dbg, dbu], axis=-1)
    return dx2, dw, db


def _ssd_consts(hpg, W):
    P = M_HEADDIM
    E = (_iota((LANES, W), 0) == _iota((LANES, W), 1) // P).astype(BF16)
    Ebig = (_iota((LANES, hpg * LANES), 0) == _iota((LANES, hpg * LANES), 1) // LANES).astype(BF16)
    causal = _iota((M_CHUNK, M_CHUNK), 0) >= _iota((M_CHUNK, M_CHUNK), 1)
    head_of_lane = _iota((1, W), 1) // P
    return E, Ebig, causal, head_of_lane


def _ssd_chunk_fwd(xs, Bm, Cm, dtr, bias, Aneg, E, Ebig, causal, head_of_lane, hpg, st, ar_sc, ae_sc):
    pre = dtr + bias
    dt = jnp.maximum(pre, 0.0) + jnp.log(1.0 + jnp.exp(-jnp.abs(pre)))
    Ad = dt * Aneg
    a_c = _cumsum_rows(Ad)
    ar_sc[...] = a_c.T
    aexp = _dot_exact(a_c, E)
    ae_sc[...] = aexp
    alast = ae_sc[M_CHUNK - 1:M_CHUNK, :]
    dtexp = _dot_exact(dt, E)
    X = xs * dtexp
    AC = _dot_exact(a_c, Ebig)
    CB = _dot(Cm, Bm, NT)
    Xb = X.astype(BF16)
    Ls = [jnp.where(causal, jnp.exp(jnp.minimum(AC[:, j * LANES:(j + 1) * LANES] - ar_sc[j:j + 1, :], 0.0)), 0.0)
          for j in range(hpg)]
    first = _iota((1, LANES), 1) < M_HEADDIM
    pairs = []
    for p in range(hpg // 2):
        Xp = Xb[:, p * LANES:(p + 1) * LANES]
        pairs.append(jnp.where(first, _dot(CB * Ls[2 * p], Xp), _dot(CB * Ls[2 * p + 1], Xp)))
    ydiag = pairs[0] if len(pairs) == 1 else jnp.concatenate(pairs, axis=1)
    ea = jnp.exp(aexp)
    yoff = ea * _dot(Cm, st)
    dec = jnp.exp(alast - aexp)
    return dict(dt=dt, a_c=a_c, aexp=aexp, alast=alast, dtexp=dtexp, X=X, Xb=Xb, CB=CB, Ls=Ls, ydiag=ydiag, ea=ea,
                yoff=yoff, dec=dec)


def _ssd_fwd(xbca, zx, dtc, bias, Aneg, Dexp, nw, hpg):
    B, L, _ = xbca.shape
    G, N, C = M_GROUPS, M_D_STATE, M_CHUNK
    W = hpg * M_HEADDIM
    DI = G * W
    NC = L // C
    LB = min(L, 4 * C)
    ncb = LB // C

    def body(xs_ref, b_ref, c_ref, z_ref, dt_ref, bias_ref, a_ref, d_ref, nw_ref, y_ref, yn_ref, st_ref, ST, ar_sc, ae_sc):
        @pl.when(pl.program_id(2) == 0)
        def _():
            ST[...] = jnp.zeros_like(ST)

        E, Ebig, causal, head_of_lane = _ssd_consts(hpg, W)
        bias_ = bias_ref[0]
        Aneg_ = a_ref[0]
        Dv = d_ref[...]
        nwv = nw_ref[...]

        def chunk(ci, carry):
            r0 = pl.multiple_of(ci * C, C)
            rows = pl.ds(r0, C)
            xs = xs_ref[0, rows, :]
            Bm = b_ref[0, rows, :]
            Cm = c_ref[0, rows, :]
            st = ST[...]
            st_ref[0, 0, ci] = st
            f = _ssd_chunk_fwd(xs, Bm, Cm, dt_ref[0, 0, ci], bias_, Aneg_, E, Ebig, causal, head_of_lane, hpg, st, ar_sc, ae_sc)
            y = f["ydiag"] + f["yoff"] + xs * Dv
            ST[...] = st * jnp.exp(f["alast"]) + _dot(Bm, f["X"] * f["dec"], TN)
            yg = y * _silu(z_ref[0, rows, :])
            rstd = lax.rsqrt(jnp.mean(yg * yg, axis=-1, keepdims=True) + NORM_EPS)
            y_ref[0, rows, :] = y
            yn_ref[0, rows, :] = (yg * rstd * nwv).astype(BF16)
            return carry

        lax.fori_loop(0, ncb, chunk, 0)

    xw = pl.BlockSpec((1, LB, W), lambda b, g, s: (b, s, g))
    bsp = pl.BlockSpec((1, LB, N), lambda b, g, s: (b, s, DI // N + g))
    csp = pl.BlockSpec((1, LB, N), lambda b, g, s: (b, s, DI // N + G + g))
    dts = pl.BlockSpec((1, 1, ncb, C, LANES), lambda b, g, s: (b, g, s, 0, 0))
    hv = pl.BlockSpec((1, 1, LANES), lambda b, g, s: (g, 0, 0))
    wv = pl.BlockSpec((1, W), lambda b, g, s: (0, g))
    sts = pl.BlockSpec((1, 1, ncb, N, W), lambda b, g, s: (b, g, s, 0, 0))
    return pl.pallas_call(
        body, name="ssd_fwd", grid=(B, G, L // LB), in_specs=[xw, bsp, csp, xw, dts, hv, hv, wv, wv],
        out_specs=[xw, xw, sts],
        out_shape=[jax.ShapeDtypeStruct((B, L, DI), F32), jax.ShapeDtypeStruct((B, L, DI), BF16),
                   jax.ShapeDtypeStruct((B, G, NC, N, W), F32)],
        scratch_shapes=[pltpu.VMEM((N, W), F32), pltpu.VMEM((LANES, C), F32), pltpu.VMEM((C, W), F32)],
        compiler_params=_cparams("parallel", "parallel", "arbitrary"))(xbca, xbca, xbca, zx, dtc, bias, Aneg, Dexp, nw)


def _ssd_bwd(xbca, zx, dtc, ypre, dyn, st, bias, Aneg, Dexp, nw, hpg):
    B, L, _ = xbca.shape
    G, N, C = M_GROUPS, M_D_STATE, M_CHUNK
    W = hpg * M_HEADDIM
    DI = G * W
    NC = L // C
    LB = min(L, 4 * C)
    ncb = LB // C
    nsb = L // LB

    def body(xs_ref, b_ref, c_ref, z_ref, dt_ref, y_ref, dyn_ref, st_ref, bias_ref, a_ref, d_ref, nw_ref,
             dxs_ref, dbc_ref, dz_ref, ddt_ref, dnw_ref, dd_ref, da_ref, dbias_ref, DST, ar_sc, ae_sc):
        @pl.when(pl.program_id(2) == 0)
        def _():
            DST[...] = jnp.zeros_like(DST)
            dnw_ref[...] = jnp.zeros_like(dnw_ref)
            dd_ref[...] = jnp.zeros_like(dd_ref)
            da_ref[...] = jnp.zeros_like(da_ref)
            dbias_ref[...] = jnp.zeros_like(dbias_ref)

        E, Ebig, causal, head_of_lane = _ssd_consts(hpg, W)
        bias_ = bias_ref[0]
        Aneg_ = a_ref[0]
        Dv = d_ref[...]
        nwv = nw_ref[...]
        lane = _iota((1, LANES), 1)
        subl = _iota((LANES, 1), 0)
        lastrow = _iota((C, W), 0) == C - 1

        def chunk(i, carry):
            ci = ncb - 1 - i
            r0 = pl.multiple_of(ci * C, C)
            rows = pl.ds(r0, C)
            xs = xs_ref[0, rows, :]
            Bm = b_ref[0, rows, :]
            Cm = c_ref[0, rows, :]
            zr = z_ref[0, rows, :]
            dtr = dt_ref[0, 0, ci]
            st_in = st_ref[0, 0, ci]
            dst = DST[...]
            f = _ssd_chunk_fwd(xs, Bm, Cm, dtr, bias_, Aneg_, E, Ebig, causal, head_of_lane, hpg, st_in, ar_sc, ae_sc)
            X, Xb, dec, ea, CB = f["X"], f["Xb"], f["dec"], f["ea"], f["CB"]
            y = y_ref[0, rows, :]
            sz = _silu(zr)
            yg = y * sz
            rstd = lax.rsqrt(jnp.mean(yg * yg, axis=-1, keepdims=True) + NORM_EPS)
            yh = yg * rstd
            dyn_ = dyn_ref[0, rows, :]
            dnw_ref[0, 0] += jnp.sum(dyn_ * yh, axis=0, keepdims=True)
            dyh = dyn_ * nwv
            dyg = rstd * (dyh - yh * jnp.mean(dyh * yh, axis=-1, keepdims=True))
            dz_ref[0, rows, :] = (dyg * y * _dsilu(zr)).astype(BF16)
            dy = dyg * sz
            dd_ref[0, 0] += jnp.sum(dy * xs, axis=0, keepdims=True)
            dxs = dy * Dv
            dYo = dy * ea
            daexp = dy * f["yoff"]
            dCm = _dot(dYo, st_in, NT)
            dst_in = _dot(Cm, dYo, TN)
            dyb = dy.astype(BF16)
            dCB = jnp.zeros((C, C), F32)
            da_col = jnp.zeros((C, LANES), F32)
            da_row = jnp.zeros((LANES, C), F32)
            first = lane < M_HEADDIM
            dXs = []
            for p in range(hpg // 2):
                Xp = Xb[:, p * LANES:(p + 1) * LANES]
                dYp = dyb[:, p * LANES:(p + 1) * LANES]
                dXp = None
                for j in (2 * p, 2 * p + 1):
                    Lj = f["Ls"][j]
                    Gj = CB * Lj
                    dYj = jnp.where(first if j % 2 == 0 else jnp.logical_not(first), dYp, jnp.zeros_like(dYp))
                    t = _dot(Gj, dYj, TN)
                    dXp = t if dXp is None else dXp + t
                    dGj = _dot(dYj, Xp, NT)
                    dCB = dCB + dGj * Lj
                    Wj = dGj * Gj
                    da_col = da_col + jnp.sum(Wj, axis=1, keepdims=True) * (lane == j).astype(F32)
                    da_row = da_row + (subl == j).astype(F32) * jnp.sum(Wj, axis=0, keepdims=True)
                dXs.append(dXp)
            dX = dXs[0] if len(dXs) == 1 else jnp.concatenate(dXs, axis=1)
            dCm = dCm + _dot(dCB, Bm)
            dBm = _dot(dCB, Cm, TN)
            ela = jnp.exp(f["alast"])
            dalast = jnp.sum(dst * st_in, axis=0, keepdims=True) * ela
            DST[...] = dst * ela + dst_in
            dXd = _dot(Bm, dst)
            dBm = dBm + _dot(X * dec, dst, NT)
            dX = dX + dXd * dec
            ddec = dXd * X * dec
            dalast = dalast + jnp.sum(ddec, axis=0, keepdims=True)
            daexp = daexp - ddec + jnp.where(lastrow, dalast, 0.0)
            dxs = dxs + dX * f["dtexp"]
            ddtexp = dX * xs
            ddt = _dot_exact(ddtexp, E, NT, passes=2)
            da_c = _dot_exact(daexp, E, NT, passes=2) + da_col - da_row.T
            dAd = _cumsum_rows(da_c, reverse=True)
            ddt = ddt + dAd * Aneg_
            da_ref[0, 0] += jnp.sum(dAd * f["dt"], axis=0, keepdims=True) * Aneg_
            ddtr = ddt * jax.nn.sigmoid(dtr + bias_)
            dbias_ref[0, 0] += jnp.sum(ddtr, axis=0, keepdims=True)
            ddt_ref[0, 0, ci] = ddtr
            dxs_ref[0, rows, :] = dxs
            dbc_ref[0, 0, rows, :] = dBm
            dbc_ref[1, 0, rows, :] = dCm
            return carry

        lax.fori_loop(0, ncb, chunk, 0)

    def rev(s):
        return nsb - 1 - s

    xw = pl.BlockSpec((1, LB, W), lambda b, g, s: (b, rev(s), g))
    bsp = pl.BlockSpec((1, LB, N), lambda b, g, s: (b, rev(s), DI // N + g))
    csp = pl.BlockSpec((1, LB, N), lambda b, g, s: (b, rev(s), DI // N + G + g))
    gsp = pl.BlockSpec((1, LB, N), lambda b, g, s: (b, rev(s), g))
    dts = pl.BlockSpec((1, 1, ncb, C, LANES), lambda b, g, s: (b, g, rev(s), 0, 0))
    hv = pl.BlockSpec((1, 1, LANES), lambda b, g, s: (g, 0, 0))
    wv = pl.BlockSpec((1, W), lambda b, g, s: (0, g))
    sts = pl.BlockSpec((1, 1, ncb, N, W), lambda b, g, s: (b, g, rev(s), 0, 0))
    accw = pl.BlockSpec((1, 1, 1, W), lambda b, g, s: (b, g, 0, 0))
    acch = pl.BlockSpec((1, 1, 1, LANES), lambda b, g, s: (b, g, 0, 0))
    return pl.pallas_call(
        body, name="ssd_bwd", grid=(B, G, nsb), in_specs=[xw, bsp, csp, xw, dts, xw, xw, sts, hv, hv, wv, wv],
        out_specs=[xw, pl.BlockSpec((2, 1, LB, N), lambda b, g, s: (0, b, rev(s), g)), xw, dts, accw, accw, acch, acch],
        out_shape=[jax.ShapeDtypeStruct((B, L, DI), F32), jax.ShapeDtypeStruct((2, B, L, G * N), F32),
                   jax.ShapeDtypeStruct((B, L, DI), BF16),
                   jax.ShapeDtypeStruct((B, G, NC, C, LANES), F32), jax.ShapeDtypeStruct((B, G, 1, W), F32),
                   jax.ShapeDtypeStruct((B, G, 1, W), F32), jax.ShapeDtypeStruct((B, G, 1, LANES), F32),
                   jax.ShapeDtypeStruct((B, G, 1, LANES), F32)],
        scratch_shapes=[pltpu.VMEM((N, W), F32), pltpu.VMEM((LANES, C), F32), pltpu.VMEM((C, W), F32)],
        compiler_params=_cparams("parallel", "parallel", "arbitrary"))(
            xbca, xbca, xbca, zx, dtc, ypre, dyn, st, bias, Aneg, Dexp, nw)


def _adamw(w, g, m, v, name, echo=False):
    shape = w.shape
    n = w.size
    cols = shape[-1]
    rows = n // cols
    tr = rows
    for cand in (512, 256, 128, 64, 32, 16, 8):
        if rows % cand == 0 and cand * cols * 4 <= 1024 * 1024:
            tr = cand
            break
    c1 = 1.0 / (1.0 - ADAM_B1 ** ADAM_STEP)
    c2 = 1.0 / (1.0 - ADAM_B2 ** ADAM_STEP)

    def body(w_ref, g_ref, m_ref, v_ref, d_ref, mo_ref, vo_ref, *go_ref):
        g_ = g_ref[...]
        mn = ADAM_B1 * m_ref[...] + (1.0 - ADAM_B1) * g_
        vn = ADAM_B2 * v_ref[...] + (1.0 - ADAM_B2) * (g_ * g_)
        d_ref[...] = -ADAM_LR * ((mn * c1) / (jnp.sqrt(vn * c2) + ADAM_EPS) + ADAM_WD * w_ref[...])
        mo_ref[...] = mn
        vo_ref[...] = vn
        if echo:
            go_ref[0][...] = g_

    spec = pl.BlockSpec((tr, cols), lambda i: (i, 0))
    r2 = lambda a: a.reshape(rows, cols)
    nout = 4 if echo else 3
    outs = pl.pallas_call(
        body, name=name, grid=(rows // tr,), in_specs=[spec] * 4, out_specs=[spec] * nout,
        out_shape=[jax.ShapeDtypeStruct((rows, cols), F32)] * nout,
        compiler_params=_cparams("parallel"))(r2(w), r2(g), r2(m), r2(v))
    return tuple(o.reshape(shape) for o in outs)


def _lower_bounds(lb_logits):
    p = jax.nn.softmax(lb_logits.astype(F32), axis=0)
    return jnp.cumsum(p, axis=0) - p[0]


def _pad_cols(a, n):
    return a if a.shape[-1] == n else jnp.pad(a, [(0, 0)] * (a.ndim - 1) + [(0, n - a.shape[-1])])


def _heads_to_lanes(a, G, hpg):
    return _pad_cols(a.reshape(G, 1, hpg), LANES)


def _local_step(x, target, P, fetch, emit):
    B, L, D = x.shape
    T = B * L
    depth = P["mix_norm"].shape[0]
    H = D // HGRN_DK
    F_ = H * HGRN_DK
    DI = P["m_norm"].shape[1]
    G, N = M_GROUPS, M_D_STATE
    MH = DI // M_HEADDIM
    hpg = MH // G
    assert hpg <= 8
    W = hpg * M_HEADDIM
    CD = DI + 2 * G * N
    MIN = DI + CD + MH
    MPAD = -(-MIN // LANES) * LANES
    dff = P["f_conv_b"].shape[1] // 2
    NC = L // M_CHUNK
    lbs = _lower_bounds(P["hgrn_lb_logits"])

    h = x.reshape(T, D)
    saved = []
    for i in range(depth):
        j = i // 2
        Wl = dict(fetch(i, ("mix_in",), h))
        s = {"h_in": h, "W": Wl}
        u = _rmsnorm_fwd(h, P["mix_norm"][i], "mix_norm_fwd")
        s["u"] = u
        if i % 2 == 0:
            proj = _matmul(u, Wl["mix_in"], name="hgrn_in_fwd").reshape(B, L, 4 * F_)
            o, on, st = _hgrn_fwd(proj, lbs[j].reshape(1, F_), P["hgrn_gnorm"][j].reshape(1, HGRN_DK), H)
            Wl.update(fetch(i, ("mix_out",), on))
            h = _matmul(on.reshape(T, F_), Wl["mix_out"], res=h, name="hgrn_out_fwd")
            s.update(proj=proj, o=o, on=on, st=st)
        else:
            zx = _matmul(u, Wl["mix_in"], tb=True, tn=1152, name="m_in_fwd").reshape(B, L, MPAD)
            xbca = _mconv_fwd(zx, P["m_conv_w"][j], P["m_conv_b"][j], DI, CD)
            dtr = zx[:, :, DI + CD:DI + CD + MH].reshape(B, NC, M_CHUNK, G, hpg).transpose(0, 3, 1, 2, 4)
            dtc = _pad_cols(dtr, LANES)
            bias = _heads_to_lanes(P["m_dt_bias"][j], G, hpg)
            Aneg = _heads_to_lanes(-jnp.exp(P["m_A_log"][j]), G, hpg)
            Dexp = jnp.repeat(P["m_D"][j], M_HEADDIM).reshape(1, DI)
            nw = P["m_norm"][j].reshape(1, DI)
            ypre, yn, st = _ssd_fwd(xbca, zx, dtc, bias, Aneg, Dexp, nw, hpg)
            Wl.update(fetch(i, ("mix_out",), yn))
            h = _matmul(yn.reshape(T, DI), Wl["mix_out"], res=h, name="m_out_fwd")
            s.update(zx=zx, xbca=xbca, dtc=dtc, bias=bias, Aneg=Aneg, Dexp=Dexp, nw=nw, ypre=ypre, yn=yn, st=st)
        s["h_mid"] = h
        u2 = _rmsnorm_fwd(h, P["ffn_norm"][i], "ffn_norm_fwd")
        Wl.update(fetch(i, ("f_w_up", "f_w_down"), h))
        up = _matmul(u2, Wl["f_w_up"], name="ffn_up_fwd").reshape(B, L, 2 * dff)
        act = _ffn_mid_fwd(up, P["f_conv_w"][i], P["f_conv_b"][i], dff)
        h = _matmul(act.reshape(T, dff), Wl["f_w_down"], res=h, name="ffn_down_fwd")
        s.update(u2=u2, up=up, act=act)
        saved.append(s)

    loss, dh, dhb, d_final = _loss_head(h, P["final_norm"], target.reshape(T, D))

    g = {k: [None] * P[k].shape[0] for k in ("mix_norm", "ffn_norm", "hgrn_gnorm", "m_conv_w", "m_conv_b", "m_dt_bias",
                                              "m_A_log", "m_D", "m_norm", "f_conv_w", "f_conv_b")}
    dlbs = [None] * lbs.shape[0]
    for i in reversed(range(depth)):
        j = i // 2
        s = saved[i]
        Wl = s["W"]
        gm = {}

        def dw(key, a, b, name, **kw):
            gm[key] = _matmul(a, b, ta=True, out_dtype=BF16, tk=T, name=name, **kw)

        dact = _matmul(dhb, Wl["f_w_down"], tb=True, name="ffn_down_dx").reshape(B, L, dff)
        dw("f_w_down", s["act"].reshape(T, dff), dhb, "ffn_down_dw")
        dup, dcw, dcb = _ffn_mid_bwd(s["up"], dact, P["f_conv_w"][i], P["f_conv_b"][i], dff)
        g["f_conv_w"][i] = jnp.sum(dcw, axis=0)
        g["f_conv_b"][i] = jnp.sum(dcb, axis=(0, 1))
        dup = dup.reshape(2, T, dff)
        dw("f_w_up", s["u2"], dup, "ffn_up_dw", b_parts=True)
        tok, finish = emit(i, {key: gm[key] for key in ("f_w_up", "f_w_down")})
        du2 = _matmul(dup, Wl["f_w_up"], a_parts=True, tb=True, name="ffn_up_dx", dep=tok)
        dh, dhb, g["ffn_norm"][i] = _rmsnorm_bwd(s["h_mid"], P["ffn_norm"][i], du2, dh, "ffn_norm_bwd", dep=finish(du2))
        if i % 2 == 0:
            don = _matmul(dhb, Wl["mix_out"], tb=True, name="hgrn_out_dx").reshape(B, L, F_)
            dw("mix_out", s["on"].reshape(T, F_), dhb, "hgrn_out_dw")
            dproj, dlb, dgn = _hgrn_bwd(s["proj"], s["o"], don, s["st"], lbs[j].reshape(1, F_),
                                        P["hgrn_gnorm"][j].reshape(1, HGRN_DK), H)
            dlbs[j] = jnp.sum(dlb, axis=(0, 1))
            g["hgrn_gnorm"][j] = jnp.sum(dgn, axis=(0, 1, 2))
            dproj = dproj.reshape(4, T, F_)
            dw("mix_in", s["u"], dproj, "hgrn_in_dw", b_parts=True)
            tok, finish = emit(i, {key: gm[key] for key in ("mix_in", "mix_out")})
            du = _matmul(dproj, Wl["mix_in"], a_parts=True, tb=True, name="hgrn_in_dx", dep=tok)
        else:
            dyn = _matmul(dhb, Wl["mix_out"], tb=True, name="m_out_dx").reshape(B, L, DI)
            dw("mix_out", s["yn"].reshape(T, DI), dhb, "m_out_dw")
            dxs, dbc, dz, ddt, dnw, dD, dA, dbias = _ssd_bwd(s["xbca"], s["zx"], s["dtc"], s["ypre"], dyn, s["st"],
                                                             s["bias"], s["Aneg"], s["Dexp"], s["nw"], hpg)
            g["m_norm"][j] = jnp.sum(dnw, axis=(0, 2)).reshape(DI)
            g["m_D"][j] = jnp.sum(dD, axis=(0, 2)).reshape(MH, M_HEADDIM).sum(axis=-1)
            g["m_A_log"][j] = jnp.sum(dA, axis=(0, 2))[:, :hpg].reshape(MH)
            g["m_dt_bias"][j] = jnp.sum(dbias, axis=(0, 2))[:, :hpg].reshape(MH)
            cw, cb = P["m_conv_w"][j], P["m_conv_b"][j]
            dxx, dcw_x, dcb_x = _mconv_bwd(s["zx"], dxs[None], cw, cb, DI, 0, "mconv_bwd_x")
            dxb, dcw_b, dcb_b = _mconv_bwd(s["zx"], dbc, cw, cb, DI, DI, "mconv_bwd_bc")
            g["m_conv_w"][j] = jnp.concatenate([jnp.sum(dcw_x, axis=0), jnp.sum(dcw_b, axis=0)], axis=-1)
            g["m_conv_b"][j] = jnp.concatenate([jnp.sum(dcb_x, axis=(0, 1)), jnp.sum(dcb_b, axis=(0, 1))], axis=-1)
            ddt_t = _pad_cols(ddt[..., :hpg].transpose(0, 2, 3, 1, 4).reshape(T, MH), MPAD - DI - CD).astype(BF16)
            pieces = [(dz.reshape(T, DI), 0), (dxx.reshape(T, DI), DI), (dxb.reshape(T, 2 * G * N), 2 * DI), (ddt_t, DI + CD)]
            gm["mix_in"] = lax.empty((MPAD, D), BF16)
            for n_, (piece, off) in enumerate(pieces):
                gm["mix_in"] = _matmul(piece, s["u"], ta=True, out_dtype=BF16, tk=T, out=gm["mix_in"], out_off=off,
                                       name="m_in_dw%d" % n_)
            tok, finish = emit(i, {key: gm[key] for key in ("mix_in", "mix_out")})
            du = None
            for n_, (piece, off) in enumerate(pieces):
                du = _matmul(piece, Wl["mix_in"], b_off=off, res=du, name="m_in_dx%d" % n_, dep=tok if n_ == 0 else None)
        dh, dhb, g["mix_norm"][i] = _rmsnorm_bwd(s["h_in"], P["mix_norm"][i], du, dh, "mix_norm_bwd", dep=finish(du))

    grads = {k: jnp.stack(vs) for k, vs in g.items()}
    grads["final_norm"] = d_final
    _, lb_vjp = jax.vjp(_lower_bounds, P["hgrn_lb_logits"])
    grads["hgrn_lb_logits"] = lb_vjp(jnp.stack(dlbs))[0]
    return loss, dh.reshape(B, L, D), grads


ANY = pl.BlockSpec(memory_space=pl.ANY)
N_CHIPS = 4
N_DEV = 8


def _place():
    x, y, c = lax.axis_index("x"), lax.axis_index("y"), lax.axis_index("c")
    sibling = (x, y, 1 - c)
    chips = [(1 - x, y), (x, 1 - y), (1 - x, 1 - y)]
    return x, y, c, sibling, chips


def _remote(src, dst, send_sem, recv_sem, to):
    return pltpu.make_async_remote_copy(src_ref=src, dst_ref=dst, send_sem=send_sem, recv_sem=recv_sem, device_id=to,
                                        device_id_type=MESH)


KIND_AXIS = {"hgrn_w_in": "col", "f_w_up": "col", "m_w_in_t": "row", "hgrn_w_out": "row", "m_w_out": "row", "f_w_down": "row"}
KINDS = tuple(KIND_AXIS)
PEER_MASKS = (2, 1, 3)
ALL = slice(None)


def _chip_win(axis, cw, s):
    return (ALL, slice(s * cw, (s + 1) * cw)) if axis == "col" else (slice(s * cw, (s + 1) * cw), ALL)


def _half_win(axis, rows, cols, h):
    return (slice(h * rows // 2, (h + 1) * rows // 2), ALL) if axis == "col" else (ALL, slice(h * cols // 2, (h + 1) * cols // 2))


def _per_place(fn):
    x, y, c, sibling, chips = _place()
    chip = 2 * x + y
    for s in range(N_CHIPS):
        for cc in range(2):
            @pl.when(jnp.logical_and(chip == s, c == cc))
            def _():
                fn(s, cc, c, sibling, chips)


HBM = pl.BlockSpec(memory_space=pltpu.HBM)
SEM = pl.BlockSpec(memory_space=pltpu.SEMAPHORE)
EFFECT = pltpu.SideEffectType.DATAFLOW_SIDE_EFFECTING


def _cell(axis, rows, cols, cw, s, h):
    if axis == "col":
        return (slice(h * rows // 2, (h + 1) * rows // 2), slice(s * cw, (s + 1) * cw))
    return (slice(s * cw, (s + 1) * cw), slice(h * cols // 2, (h + 1) * cols // 2))


def _in_hbm(a):
    return pltpu.with_memory_space_constraint(a, pltpu.HBM)


def _stage_shard(kind, shard, layer, chip, pad_rows=0, dep=None):
    _, R, C = shard.shape
    axis = KIND_AXIS[kind]
    tr, tc = _row_tile(R), _pick_tile(C, 2048)
    nr, nc = R // tr, C // tc
    full = (R, N_CHIPS * C) if axis == "col" else (N_CHIPS * R + pad_rows, C)

    def body(s_ref, x_ref, *rest):
        o_ref = rest[-1]
        o_ref[...] = x_ref[...].astype(BF16)

    if axis == "col":
        dst = pl.BlockSpec((tr, tc), lambda i, j, s_ref: (i, s_ref[0] * nc + j))
    else:
        dst = pl.BlockSpec((tr, tc), lambda i, j, s_ref: (s_ref[0] * nr + i, j))
    extra_specs, extra = ([], ()) if dep is None else ([ANY], (dep,))
    grid_spec = pltpu.PrefetchScalarGridSpec(
        num_scalar_prefetch=1, grid=(nr, nc),
        in_specs=[pl.BlockSpec((None, tr, tc), lambda i, j, s_ref: (layer, i, j))] + extra_specs, out_specs=dst)
    out = pl.pallas_call(
        body, name="stage_" + kind, grid_spec=grid_spec, out_shape=jax.ShapeDtypeStruct(full, BF16),
        compiler_params=_cparams("parallel", "parallel"))(chip.reshape(1).astype(jnp.int32), shard, *extra)
    if pad_rows:
        rows0 = N_CHIPS * R
        pr = math.gcd(rows0, pad_rows)

        def zero_body(x_ref, o_ref):
            o_ref[...] = jnp.zeros_like(o_ref)

        out = pl.pallas_call(
            zero_body, name="zero_pad_" + kind, grid=(pad_rows // pr,), in_specs=[ANY],
            out_specs=pl.BlockSpec((pr, C), lambda i: (rows0 // pr + i, 0)), out_shape=jax.ShapeDtypeStruct(full, BF16),
            input_output_aliases={0: 0}, compiler_params=_cparams("parallel"))(out)
    return out


def _gather_start(items, mats, cws, after, name):
    n = len(items)

    def body(*refs):
        send_sems, recv_sems, token = refs[n + 1], refs[n + 2], refs[-1]
        m = refs[n + 3:2 * n + 3]

        def run(s, cc, c, sibling, chips):
            for q, (k, _) in enumerate(items):
                r, c_ = m[q].shape
                mine = m[q].at[_cell(KIND_AXIS[k], r, c_, cws[k], s, cc)]
                for j, (px, py) in enumerate(chips):
                    _remote(mine, mine, send_sems.at[3 * q + j], recv_sems.at[3 * q + j], (px, py, c)).start()

        _per_place(run)
        token[...] = jnp.zeros_like(token)

    outs = pl.pallas_call(
        body, name=name, in_specs=[HBM] * n + [ANY],
        out_specs=[SEM, SEM] + [HBM] * n + [pl.BlockSpec(memory_space=pltpu.VMEM)],
        out_shape=[pltpu.SemaphoreType.DMA((3 * n,)), pltpu.SemaphoreType.DMA((3 * n,))]
        + [pltpu.HBM(a.shape, a.dtype) for a in mats] + [jax.ShapeDtypeStruct((8, LANES), F32)],
        input_output_aliases={q: 2 + q for q in range(n)},
        compiler_params=pltpu.CompilerParams(has_side_effects=EFFECT),
    )(*[_in_hbm(a) for a in mats], after)
    return outs[0], outs[1], list(outs[2:2 + n]), outs[-1]


def _gather_wait(items, idx, mats, send_sems, recv_sems, cws, after, name):
    n = len(idx)

    def body(*refs):
        m = refs[:n]
        s_sems, r_sems = refs[n], refs[n + 1]

        def run(s, cc, c, sibling, chips):
            for a, q in enumerate(idx):
                k = items[q][0]
                r, c_ = m[a].shape
                mine = m[a].at[_cell(KIND_AXIS[k], r, c_, cws[k], s, cc)]
                for j, (px, py) in enumerate(chips):
                    theirs = m[a].at[_cell(KIND_AXIS[k], r, c_, cws[k], s ^ PEER_MASKS[j], cc)]
                    cp = _remote(mine, theirs, s_sems.at[3 * q + j], r_sems.at[3 * q + j], (px, py, c))
                    cp.wait_send()
                    cp.wait_recv()

        _per_place(run)

    outs = pl.pallas_call(
        body, name=name, in_specs=[HBM] * n + [SEM, SEM, ANY], out_specs=[HBM] * n,
        out_shape=[pltpu.HBM(a.shape, a.dtype) for a in mats], input_output_aliases={a: a for a in range(n)},
        compiler_params=pltpu.CompilerParams(has_side_effects=EFFECT),
    )(*mats, send_sems, recv_sems, after)
    return list(outs)


def _forward_halves(kinds, mats, cws, name):
    n = len(mats)

    def body(*refs):
        m = refs[n:2 * n]
        send_sems, recv_sems = refs[2 * n:]

        def run(s, cc, c, sibling, chips):
            cps = []
            for a, k in enumerate(kinds):
                r, c_ = m[a].shape
                for j in range(3):
                    have = m[a].at[_cell(KIND_AXIS[k], r, c_, cws[k], s ^ PEER_MASKS[j], cc)]
                    cps.append(_remote(have, have, send_sems.at[3 * a + j], recv_sems.at[3 * a + j], sibling))
            for cp in cps:
                cp.start()
            for cp in cps:
                cp.wait()

        _per_place(run)

    outs = pl.pallas_call(
        body, name=name, in_specs=[ANY] * n, out_specs=[ANY] * n,
        out_shape=[jax.ShapeDtypeStruct(a.shape, a.dtype) for a in mats], input_output_aliases={a: a for a in range(n)},
        scratch_shapes=[pltpu.SemaphoreType.DMA((3 * n,)), pltpu.SemaphoreType.DMA((3 * n,))],
    )(*mats)
    return list(outs)


def _swap_start(kinds, gms, name):
    n = len(gms)
    lands = [lax.empty((g.shape[0] // 2, g.shape[1]) if KIND_AXIS[k] == "col" else (g.shape[0], g.shape[1] // 2), BF16)
             for k, g in zip(kinds, gms)]

    def body(*refs):
        send_sems, recv_sems, token = refs[2 * n], refs[2 * n + 1], refs[-1]
        g, ra = refs[2 * n + 2:3 * n + 2], refs[3 * n + 2:4 * n + 2]

        def run(s, cc, c, sibling, chips):
            for a, k in enumerate(kinds):
                r, c_ = g[a].shape
                _remote(g[a].at[_half_win(KIND_AXIS[k], r, c_, 1 - cc)], ra[a], send_sems.at[a], recv_sems.at[a],
                        sibling).start()

        _per_place(run)
        token[...] = jnp.zeros_like(token)

    outs = pl.pallas_call(
        body, name=name, in_specs=[HBM] * (2 * n),
        out_specs=[SEM, SEM] + [HBM] * (2 * n) + [pl.BlockSpec(memory_space=pltpu.VMEM)],
        out_shape=[pltpu.SemaphoreType.DMA((n,)), pltpu.SemaphoreType.DMA((n,))]
        + [pltpu.HBM(a.shape, a.dtype) for a in gms + lands] + [jax.ShapeDtypeStruct((8, LANES), F32)],
        input_output_aliases={q: 2 + q for q in range(2 * n)},
        compiler_params=pltpu.CompilerParams(has_side_effects=EFFECT),
    )(*[_in_hbm(a) for a in gms + lands])
    return outs[0], outs[1], list(outs[2:2 + n]), list(outs[2 + n:2 + 2 * n]), outs[-1]


def _swap_wait(kinds, gms, lands, send_sems, recv_sems, after, name):
    n = len(gms)

    def body(*refs):
        g, ra = refs[:n], refs[n:2 * n]
        s_sems, r_sems = refs[2 * n], refs[2 * n + 1]

        def run(s, cc, c, sibling, chips):
            for a, k in enumerate(kinds):
                r, c_ = g[a].shape
                cp = _remote(g[a].at[_half_win(KIND_AXIS[k], r, c_, 1 - cc)], ra[a], s_sems.at[a], r_sems.at[a], sibling)
                cp.wait_send()
                cp.wait_recv()

        _per_place(run)

    outs = pl.pallas_call(
        body, name=name, in_specs=[HBM] * (2 * n) + [SEM, SEM, ANY], out_specs=[HBM] * (2 * n),
        out_shape=[pltpu.HBM(a.shape, a.dtype) for a in gms + lands], input_output_aliases={a: a for a in range(2 * n)},
        compiler_params=pltpu.CompilerParams(has_side_effects=EFFECT),
    )(*gms, *lands, send_sems, recv_sems, after)
    return list(outs[:n]), list(outs[n:])


def _win_shape(kind, pa, cw):
    return (pa.shape[0], cw) if KIND_AXIS[kind] == "col" else (cw, pa.shape[1])


def _scatter_start(kinds, pas, cws, name):
    n = len(pas)
    lands = [lax.empty((3,) + _win_shape(k, p, cws[k]), BF16) for k, p in zip(kinds, pas)]

    def body(*refs):
        send_sems, recv_sems, token = refs[2 * n], refs[2 * n + 1], refs[-1]
        p, rb = refs[2 * n + 2:3 * n + 2], refs[3 * n + 2:4 * n + 2]

        def run(s, cc, c, sibling, chips):
            for a, k in enumerate(kinds):
                for j, (px, py) in enumerate(chips):
                    src = p[a].at[_chip_win(KIND_AXIS[k], cws[k], s ^ PEER_MASKS[j])]
                    _remote(src, rb[a].at[j], send_sems.at[3 * a + j], recv_sems.at[3 * a + j], (px, py, c)).start()

        _per_place(run)
        token[...] = jnp.zeros_like(token)

    outs = pl.pallas_call(
        body, name=name, in_specs=[HBM] * (2 * n),
        out_specs=[SEM, SEM] + [HBM] * (2 * n) + [pl.BlockSpec(memory_space=pltpu.VMEM)],
        out_shape=[pltpu.SemaphoreType.DMA((3 * n,)), pltpu.SemaphoreType.DMA((3 * n,))]
        + [pltpu.HBM(a.shape, a.dtype) for a in pas + lands] + [jax.ShapeDtypeStruct((8, LANES), F32)],
        input_output_aliases={q: 2 + q for q in range(2 * n)},
        compiler_params=pltpu.CompilerParams(has_side_effects=EFFECT),
    )(*[_in_hbm(a) for a in pas + lands])
    return outs[0], outs[1], list(outs[2:2 + n]), list(outs[2 + n:2 + 2 * n]), outs[-1]


def _scatter_wait(kinds, pas, lands, send_sems, recv_sems, cws, after, name):
    n = len(pas)

    def body(*refs):
        p, rb = refs[:n], refs[n:2 * n]
        s_sems, r_sems = refs[2 * n], refs[2 * n + 1]

        def run(s, cc, c, sibling, chips):
            for a, k in enumerate(kinds):
                for j, (px, py) in enumerate(chips):
                    src = p[a].at[_chip_win(KIND_AXIS[k], cws[k], s ^ PEER_MASKS[j])]
                    cp = _remote(src, rb[a].at[j], s_sems.at[3 * a + j], r_sems.at[3 * a + j], (px, py, c))
                    cp.wait_send()
                    cp.wait_recv()

        _per_place(run)

    outs = pl.pallas_call(
        body, name=name, in_specs=[HBM] * (2 * n) + [SEM, SEM, ANY], out_specs=[HBM] * (2 * n),
        out_shape=[pltpu.HBM(a.shape, a.dtype) for a in pas + lands], input_output_aliases={a: a for a in range(2 * n)},
        compiler_params=pltpu.CompilerParams(has_side_effects=EFFECT),
    )(*pas, *lands, send_sems, recv_sems, after)
    return list(outs[:n]), list(outs[n:])


def _share_halves(g):
    nq = len(KINDS)

    def body(*refs):
        out = dict(zip(KINDS, refs[nq:2 * nq]))
        send_sems, recv_sems = refs[2 * nq:]

        def run(s, cc, c, sibling, chips):
            cps = []
            for q, k in enumerate(KINDS):
                _, r, c_ = out[k].shape
                mine = out[k].at[(ALL,) + _half_win(KIND_AXIS[k], r, c_, cc)]
                cps.append(_remote(mine, mine, send_sems.at[q], recv_sems.at[q], sibling))
            for cp in cps:
                cp.start()
            for cp in cps:
                cp.wait()

        _per_place(run)

    outs = pl.pallas_call(
        body, name="share_halves", in_specs=[ANY] * nq, out_specs=[ANY] * nq,
        out_shape=[jax.ShapeDtypeStruct(g[k].shape, F32) for k in KINDS],
        input_output_aliases={q: q for q in range(nq)},
        scratch_shapes=[pltpu.SemaphoreType.DMA((nq,)), pltpu.SemaphoreType.DMA((nq,))],
    )(*[g[k] for k in KINDS])
    return dict(zip(KINDS, outs))


def _all_gather_small(xs, name):
    m_per, n = xs.shape

    def body(x_ref, out_ref, send_sems, recv_sems, local_sem):
        x, y, c, sibling, chips = _place()
        me = (x, y, c)

        def rows(px, py, pc):
            return out_ref.at[pl.ds((4 * px + 2 * py + pc) * m_per, m_per), :]

        def copy(k, block, to, src=None):
            return _remote(rows(*block) if src is None else src, rows(*block), send_sems.at[k], recv_sems.at[k], to)

        mine = pltpu.make_async_copy(x_ref, rows(*me), local_sem)
        mine.start()
        first = [copy(0, me, sibling, src=x_ref)]
        first += [copy(1 + j, me, (*chip, c), src=x_ref) for j, chip in enumerate(chips)]
        for cp in first:
            cp.start()
        passed = [copy(4 + j, (*chip, c), sibling) for j, chip in enumerate(chips)]
        for j, chip in enumerate(chips):
            copy(1 + j, (*chip, c), me).wait_recv()
            passed[j].start()
        copy(0, sibling, me).wait_recv()
        for j, chip in enumerate(chips):
            copy(4 + j, (*chip, 1 - c), me).wait_recv()
        for cp in first + passed:
            cp.wait_send()
        mine.wait()

    vm = pl.BlockSpec(memory_space=pltpu.VMEM)
    return pl.pallas_call(
        body, name=name, in_specs=[vm], out_specs=vm, out_shape=jax.ShapeDtypeStruct((N_DEV * m_per, n), xs.dtype),
        scratch_shapes=[pltpu.SemaphoreType.DMA((7,)), pltpu.SemaphoreType.DMA((7,)), pltpu.SemaphoreType.DMA],
        compiler_params=pltpu.CompilerParams(vmem_limit_bytes=VMEM_LIMIT_BYTES),
    )(xs)


def _row_tile(rows, cap=512):
    for mult in (16, 8):
        best = None
        t = mult
        while t <= min(rows, cap):
            if rows % t == 0:
                best = t
            t += mult
        if best is not None:
            return best
    raise ValueError(rows)


def _add_sibling(kind, g, ra, core):
    R, C = ra.shape
    axis = KIND_AXIS[kind]
    tr, tc = _row_tile(R), _pick_tile(C, 2048)
    nr, nc = R // tr, C // tc

    def body(c_ref, a_ref, b_ref, o_ref):
        o_ref[...] = (a_ref[...].astype(F32) + b_ref[...].astype(F32)).astype(o_ref.dtype)

    if axis == "col":
        own = pl.BlockSpec((tr, tc), lambda i, j, c_ref: (c_ref[0] * nr + i, j))
    else:
        own = pl.BlockSpec((tr, tc), lambda i, j, c_ref: (i, c_ref[0] * nc + j))
    same = pl.BlockSpec((tr, tc), lambda i, j, c_ref: (i, j))
    grid_spec = pltpu.PrefetchScalarGridSpec(num_scalar_prefetch=1, grid=(nr, nc), in_specs=[own, same], out_specs=same)
    return pl.pallas_call(
        body, name="add_sibling_" + kind, grid_spec=grid_spec, out_shape=jax.ShapeDtypeStruct(ra.shape, BF16),
        compiler_params=_cparams("parallel", "parallel"))(core.reshape(1).astype(jnp.int32), g, ra)


def _sum_chips(kind, pa, rb, chip, core, out, layer):
    _, R, C = rb.shape
    axis = KIND_AXIS[kind]
    tr, tc = _row_tile(R), _pick_tile(C, 2048)
    nr, nc = R // tr, C // tc

    def body(s_ref, c_ref, a_ref, b0_ref, b1_ref, b2_ref, old_ref, o_ref):
        o_ref[...] = ((a_ref[...].astype(F32) + b0_ref[...].astype(F32)) + b1_ref[...].astype(F32)) + b2_ref[...].astype(F32)

    def rb_spec(n):
        return pl.BlockSpec((None, tr, tc), lambda i, j, s_ref, c_ref: (n, i, j))

    if axis == "col":
        own = pl.BlockSpec((tr, tc), lambda i, j, s_ref, c_ref: (i, s_ref[0] * nc + j))
        dst = pl.BlockSpec((None, tr, tc), lambda i, j, s_ref, c_ref: (layer, c_ref[0] * nr + i, j))
        assert out.shape[1:] == (2 * R, C)
    else:
        own = pl.BlockSpec((tr, tc), lambda i, j, s_ref, c_ref: (s_ref[0] * nr + i, j))
        dst = pl.BlockSpec((None, tr, tc), lambda i, j, s_ref, c_ref: (layer, i, c_ref[0] * nc + j))
        assert out.shape[1:] == (R, 2 * C)
    grid_spec = pltpu.PrefetchScalarGridSpec(
        num_scalar_prefetch=2, grid=(nr, nc), in_specs=[own, rb_spec(0), rb_spec(1), rb_spec(2), ANY], out_specs=dst)
    return pl.pallas_call(
        body, name="sum_chips_" + kind, grid_spec=grid_spec, out_shape=jax.ShapeDtypeStruct(out.shape, F32),
        input_output_aliases={6: 0}, compiler_params=_cparams("parallel", "parallel"))(
            chip.reshape(1).astype(jnp.int32), core.reshape(1).astype(jnp.int32), pa, rb, rb, rb, out)


def _sum_devices(gathered):
    M = gathered.shape[0] // N_DEV
    C = gathered.shape[1]

    def body(g_ref, o_ref):
        acc = g_ref[0:M, :]
        for d in range(1, N_DEV):
            acc = acc + g_ref[d * M:(d + 1) * M, :]
        o_ref[...] = acc

    vm = pl.BlockSpec(memory_space=pltpu.VMEM)
    return pl.pallas_call(body, name="sum_devices", in_specs=[vm], out_specs=vm, out_shape=jax.ShapeDtypeStruct((M, C), F32),
                          compiler_params=pltpu.CompilerParams(vmem_limit_bytes=VMEM_LIMIT_BYTES))(gathered)


WEIGHTS = ["mix_norm", "ffn_norm", "final_norm", "hgrn_w_in", "hgrn_lb_logits", "hgrn_gnorm", "hgrn_w_out", "m_w_in",
           "m_conv_w", "m_conv_b", "m_dt_bias", "m_A_log", "m_D", "m_norm", "m_w_out", "f_w_up", "f_conv_w", "f_conv_b",
           "f_w_down"]
BIG_COLS = ("hgrn_w_in", "m_w_in", "f_w_up")
BIG_ROWS = ("hgrn_w_out", "m_w_out", "f_w_down")
BIG = BIG_COLS + BIG_ROWS
SMALL_SHARDED = ("m_conv_w", "m_conv_b", "m_norm", "f_conv_w")
SMALL_REPLICATED = ("mix_norm", "ffn_norm", "final_norm", "hgrn_lb_logits", "hgrn_gnorm", "m_dt_bias", "m_A_log", "m_D",
                    "f_conv_b")
SMALL = SMALL_REPLICATED + SMALL_SHARDED


def _pack_rows(arrs, row_mult=8):
    flat = jnp.concatenate([a.reshape(-1).astype(F32) for a in arrs])
    unit = FLAT_COLS * row_mult
    n = -(-flat.size // unit) * unit
    return jnp.pad(flat, (0, n - flat.size)).reshape(-1, FLAT_COLS)


def _unpack_rows(buf, shapes):
    flat = buf.reshape(-1)
    out, off = [], 0
    for shp in shapes:
        n = math.prod(shp)
        out.append(flat[off:off + n].reshape(shp))
        off += n
    return out


def kernel(x, mix_norm, ffn_norm, final_norm, hgrn_w_in, hgrn_lb_logits, hgrn_gnorm, hgrn_w_out, m_w_in, m_conv_w, m_conv_b, m_dt_bias, m_A_log, m_D, m_norm, m_w_out, f_w_up, f_conv_w, f_conv_b, f_w_down, loss_target, m_mix_norm, m_ffn_norm, m_final_norm, m_hgrn_w_in, m_hgrn_lb_logits, m_hgrn_gnorm, m_hgrn_w_out, m_m_w_in, m_m_conv_w, m_m_conv_b, m_m_dt_bias, m_m_A_log, m_m_D, m_m_norm, m_m_w_out, m_f_w_up, m_f_conv_w, m_f_conv_b, m_f_w_down, v_mix_norm, v_ffn_norm, v_final_norm, v_hgrn_w_in, v_hgrn_lb_logits, v_hgrn_gnorm, v_hgrn_w_out, v_m_w_in, v_m_conv_w, v_m_conv_b, v_m_dt_bias, v_m_A_log, v_m_D, v_m_norm, v_m_w_out, v_f_w_up, v_f_conv_w, v_f_conv_b, v_f_w_down):
    given = dict(locals())
    w = {n: given[n] for n in WEIGHTS}
    mom1 = {n: given["m_" + n] for n in WEIGHTS}
    mom2 = {n: given["v_" + n] for n in WEIGHTS}
    chip = 2 * lax.axis_index("x") + lax.axis_index("y")
    core = lax.axis_index("c")

    shards = {k: w[k] for k in KINDS if k != "m_w_in_t"}
    shards["m_w_in_t"] = w["m_w_in"].transpose(0, 2, 1).astype(BF16)
    m_in = N_CHIPS * w["m_w_in"].shape[2]
    pad_rows = {"m_w_in_t": -(-m_in // LANES) * LANES - m_in}
    cws = {k: shards[k].shape[2] if KIND_AXIS[k] == "col" else shards[k].shape[1] for k in KINDS}
    depth = w["mix_norm"].shape[0]

    def layer_kinds(i):
        mixer = {"mix_in": ("hgrn_w_in", i // 2), "mix_out": ("hgrn_w_out", i // 2)} if i % 2 == 0 else \
                {"mix_in": ("m_w_in_t", i // 2), "mix_out": ("m_w_out", i // 2)}
        return {**mixer, "f_w_up": ("f_w_up", i), "f_w_down": ("f_w_down", i)}

    own = _pack_rows([w[n] for n in SMALL_SHARDED])
    all_small = _all_gather_small(own, "gather_small_params")
    groups, started = [list(layer_kinds(0).values()), [it for i in range(1, depth) for it in layer_kinds(i).values()]], []
    after = all_small
    for n_, items in enumerate(groups):
        staged = [_stage_shard(k, shards[k], l, chip, pad_rows.get(k, 0), dep=None if n_ == 0 else after) for k, l in items]
        send_sems, recv_sems, mats, after = _gather_start(items, staged, cws, after, "gather_start_%d" % n_)
        started.append((items, send_sems, recv_sems, mats))
    all_small = all_small.reshape(N_CHIPS, 2, -1)[:, 0]
    per_chip = [_unpack_rows(all_small[s], [w[n].shape for n in SMALL_SHARDED]) for s in range(N_CHIPS)]
    P = {}
    for i, n in enumerate(SMALL_SHARDED):
        P[n] = jnp.concatenate([per_chip[s][i] for s in range(N_CHIPS)], axis=-1)
    for n in SMALL_REPLICATED:
        P[n] = w[n]

    def fetch(i, keys, h):
        lk = {key: layer_kinds(i)[key] for key in keys}
        items, send_sems, recv_sems, mats = started[0 if i == 0 else 1]
        idx = [items.index(it) for it in lk.values()]
        tag = "%d_%s" % (i, keys[0])
        got = _gather_wait(items, idx, [mats[q] for q in idx], send_sems, recv_sems, cws, h, "gather_wait_" + tag)
        got = _forward_halves([k for k, _ in lk.values()], got, cws, "forward_halves_" + tag)
        return dict(zip(lk.keys(), got))

    pending = []

    def emit(i, gm):
        lk = {key: layer_kinds(i)[key] for key in gm}
        kinds = [k for k, _ in lk.values()]
        gms = list(gm.values())
        tag = "%d_%s" % (i, next(iter(gm)))
        s1, r1, gms, half_lands, tok = _swap_start(kinds, gms, "swap_start_" + tag)

        def finish(after):
            gms2, ra = _swap_wait(kinds, gms, half_lands, s1, r1, after, "swap_wait_" + tag)
            pas = [_add_sibling(k, g_, r_, core) for k, g_, r_ in zip(kinds, gms2, ra)]
            s_sems, r_sems, pas, lands, tok2 = _scatter_start(kinds, pas, cws, "scatter_start_" + tag)
            pending.append((tag, list(lk.values()), pas, lands, s_sems, r_sems))
            return tok2

        return tok, finish

    loss_part, grad_x, g_full = _local_step(x, loss_target, P, fetch, emit)

    g_sh = {k: lax.empty(shards[k].shape, F32) for k in KINDS}
    for tag, its, pas, lands, s_sems, r_sems in pending:
        kinds = [k for k, _ in its]
        pas, lands = _scatter_wait(kinds, pas, lands, s_sems, r_sems, cws, grad_x, "scatter_wait_" + tag)
        for (k, l), p_, rb_ in zip(its, pas, lands):
            g_sh[k] = _sum_chips(k, p_, rb_, chip, core, g_sh[k], l)
    g_sh = _share_halves(g_sh)
    grads = {k: g_sh[k] for k in KINDS if k != "m_w_in_t"}
    grads["m_w_in"] = g_sh["m_w_in_t"].transpose(0, 2, 1)

    small_shapes = [g_full[n].shape for n in SMALL] + [(1,)]
    packed = _pack_rows([g_full[n] for n in SMALL] + [loss_part[0, 0:1]])
    summed = _sum_devices(_all_gather_small(packed, "gather_small_grads"))
    small = _unpack_rows(summed, small_shapes)
    loss = small[-1][0]
    for n, gs in zip(SMALL, small[:-1]):
        if n in SMALL_SHARDED:
            width = w[n].shape[-1]
            gs = lax.dynamic_slice_in_dim(gs, chip * width, width, axis=gs.ndim - 1)
        grads[n] = gs

    delta, new_m, new_v = {}, {}, {}
    for n in BIG:
        if n == "m_w_in":
            delta[n], new_m[n], new_v[n] = _adamw(w[n], grads[n], mom1[n], mom2[n], "adamw_" + n)
        else:
            delta[n], new_m[n], new_v[n], grads[n] = _adamw(w[n], grads[n], mom1[n], mom2[n], "adamw_" + n, echo=True)
    shapes = [w[n].shape for n in SMALL]
    ds, ms, vs = _adamw(_pack_rows([w[n] for n in SMALL]), _pack_rows([grads[n] for n in SMALL]),
                        _pack_rows([mom1[n] for n in SMALL]), _pack_rows([mom2[n] for n in SMALL]), "adamw_small")
    for n, d_, m_, v_ in zip(SMALL, _unpack_rows(ds, shapes), _unpack_rows(ms, shapes), _unpack_rows(vs, shapes)):
        delta[n], new_m[n], new_v[n] = d_, m_, v_

    return (loss, grad_x, *[grads[n] for n in WEIGHTS], *[delta[n] for n in WEIGHTS], *[new_m[n] for n in WEIGHTS],
            *[new_v[n] for n in WEIGHTS])
```

```python
import functools
import math

import jax
import jax.numpy as jnp
from jax import lax
from jax.experimental import pallas as pl
from jax.experimental.pallas import tpu as pltpu

F32 = jnp.float32
BF16 = jnp.bfloat16
NORM_EPS = 1e-5
HGRN_DK = 128
HGRN_CHUNK = 64
HGRN_HEADS_PER_STEP = 4
HGRN_SEQ_BLOCK = 512
M_HEADDIM = 64
M_GROUPS = 8
M_D_STATE = 128
M_CONV = 4
M_CHUNK = 128
FFN_CONV = 3
EXP_CLIP = 80.0
LANES = 128
VMEM_LIMIT_BYTES = 56 * 1024 * 1024
FLAT_COLS = 1024
ADAM_LR, ADAM_B1, ADAM_B2, ADAM_EPS, ADAM_WD, ADAM_STEP = 0.001, 0.9, 0.999, 1e-08, 0.01, 10
MESH = pl.DeviceIdType.MESH

NN = (((1,), (0,)), ((), ()))
NT = (((1,), (1,)), ((), ()))
TN = (((0,), (0,)), ((), ()))


def _cparams(*sems):
    return pltpu.CompilerParams(dimension_semantics=sems, vmem_limit_bytes=VMEM_LIMIT_BYTES)


def _dot(a, b, dn=NN):
    return lax.dot_general(a.astype(BF16), b.astype(BF16), dn, preferred_element_type=F32)


def _dot_exact(x, m, dn=NN, passes=3, x_first=True):
    acc = None
    r = x
    for _ in range(passes):
        p = r.astype(BF16)
        r = r - p.astype(F32)
        t = lax.dot_general(p, m, dn, preferred_element_type=F32) if x_first else lax.dot_general(m, p, dn, preferred_element_type=F32)
        acc = t if acc is None else acc + t
    return acc


def _iota(shape, dim):
    return lax.broadcasted_iota(jnp.int32, shape, dim)


def _cumsum_rows(x, reverse=False):
    n = x.shape[0]
    row = _iota(x.shape, 0)
    s = 1
    while s < n:
        if reverse:
            x = x + jnp.where(row < n - s, pltpu.roll(x, n - s, 0), 0.0)
        else:
            x = x + jnp.where(row >= s, pltpu.roll(x, s, 0), 0.0)
        s *= 2
    return x


def _silu(x):
    return x * jax.nn.sigmoid(x)


def _dsilu(x):
    s = jax.nn.sigmoid(x)
    return s * (1.0 + x * (1.0 - s))


def _pick_tile(dim, pref):
    if dim <= pref:
        return dim
    best = None
    t = LANES
    while t <= pref:
        if dim % t == 0:
            best = t
        t += LANES
    assert best is not None, (dim, pref)
    return best


def _matmul(a, b, *, ta=False, tb=False, res=None, out_dtype=F32, tm=1024, tn=1024, tk=4096, name,
            a_parts=False, b_parts=False, b_layer=None, b_off=0, out=None, out_layer=None, out_off=0, dep=None):
    a = a.astype(BF16)
    b = b.astype(BF16)
    if a_parts:
        assert not ta
        pa, M, kp = a.shape
        K = pa * kp
    else:
        M, K = (a.shape[1], a.shape[0]) if ta else a.shape
    bsh = b.shape[1:] if b_layer is not None else b.shape
    if b_parts:
        assert not tb
        pb, _, np_ = bsh
        N = pb * np_
    else:
        N = bsh[0] if tb else bsh[1]
    tm, tn, tk = _pick_tile(M, tm), _pick_tile(np_ if b_parts else N, tn), _pick_tile(kp if a_parts else K, tk)
    nk = K // tk
    dn = (((0 if ta else 1,), (1 if tb else 0,)), ((), ()))
    assert b_off % tk == 0 and out_off % tm == 0

    def body(*refs):
        refs = list(refs)
        acc = refs.pop() if nk > 1 else None
        o_ref = refs.pop()
        if dep is not None:
            refs.pop()
        if out is not None:
            refs.pop()
        a_ref, b_ref = refs[0], refs[1]
        r_ref = refs[2] if res is not None else None
        k = pl.program_id(2)

        def prod():
            return lax.dot_general(a_ref[...], b_ref[...], dn, preferred_element_type=F32)

        def finish(r):
            if res is not None:
                r = r + r_ref[...]
            o_ref[...] = r.astype(out_dtype)

        if nk == 1:
            finish(prod())
            return

        @pl.when(k == 0)
        def _():
            acc[...] = prod()

        @pl.when(jnp.logical_and(k > 0, k < nk - 1))
        def _():
            acc[...] += prod()

        @pl.when(k == nk - 1)
        def _():
            finish(acc[...] + prod())

    if a_parts:
        kpb = kp // tk
        a_spec = pl.BlockSpec((None, tm, tk), lambda i, j, k: (k // kpb, i, k % kpb))
    elif ta:
        a_spec = pl.BlockSpec((tk, tm), lambda i, j, k: (k, i))
    else:
        a_spec = pl.BlockSpec((tm, tk), lambda i, j, k: (i, k))
    lead = () if b_layer is None else (b_layer,)
    lead_blk = () if b_layer is None else (None,)
    kb0 = b_off // tk
    if b_parts:
        npb = np_ // tn
        b_spec = pl.BlockSpec(lead_blk + (None, tk, tn), lambda i, j, k: lead + (j // npb, k, j % npb))
    elif tb:
        b_spec = pl.BlockSpec(lead_blk + (tn, tk), lambda i, j, k: lead + (j, k))
    else:
        b_spec = pl.BlockSpec(lead_blk + (tk, tn), lambda i, j, k: lead + (kb0 + k, j))
    r_spec = pl.BlockSpec((tm, tn), lambda i, j, k: (i, j))
    in_specs = [a_spec, b_spec] + ([r_spec] if res is not None else [])
    args = (a, b) + ((res,) if res is not None else ())
    if out is None:
        o_spec, out_shape, aliases = r_spec, jax.ShapeDtypeStruct((M, N), out_dtype), {}
    else:
        assert out.dtype == out_dtype and out.shape[-1] == N
        olead = () if out_layer is None else (out_layer,)
        olead_blk = () if out_layer is None else (None,)
        ob0 = out_off // tm
        o_spec = pl.BlockSpec(olead_blk + (tm, tn), lambda i, j, k: olead + (ob0 + i, j))
        out_shape = jax.ShapeDtypeStruct(out.shape, out.dtype)
        aliases = {len(args): 0}
        in_specs = in_specs + [pl.BlockSpec(memory_space=pl.ANY)]
        args = args + (out,)
    if dep is not None:
        in_specs = in_specs + [pl.BlockSpec(memory_space=pl.ANY)]
        args = args + (dep,)
    return pl.pallas_call(
        body, name=name, grid=(M // tm, N // tn, nk), in_specs=in_specs, out_specs=o_spec, out_shape=out_shape,
        scratch_shapes=[pltpu.VMEM((tm, tn), F32)] if nk > 1 else [], input_output_aliases=aliases,
        compiler_params=_cparams("parallel", "parallel", "arbitrary"))(*args)


def _rmsnorm_fwd(h, w, name):
    T, D = h.shape
    tm = _pick_tile(T, 256)

    def body(h_ref, w_ref, u_ref):
        x = h_ref[...]
        r = lax.rsqrt(jnp.mean(x * x, axis=-1, keepdims=True) + NORM_EPS)
        u_ref[...] = (x * r * w_ref[...]).astype(BF16)

    return pl.pallas_call(
        body, name=name, grid=(T // tm,),
        in_specs=[pl.BlockSpec((tm, D), lambda i: (i, 0)), pl.BlockSpec((1, D), lambda i: (0, 0))],
        out_specs=pl.BlockSpec((tm, D), lambda i: (i, 0)), out_shape=jax.ShapeDtypeStruct((T, D), BF16),
        compiler_params=_cparams("parallel"))(h, w.reshape(1, D))


def _rmsnorm_bwd(h, w, du, dres, name, dep=None):
    T, D = h.shape
    tm = _pick_tile(T, 256)

    def body(h_ref, w_ref, du_ref, dr_ref, *rest):
        dh_ref, dhb_ref, dw_ref = rest[-3:]
        x = h_ref[...]
        r = lax.rsqrt(jnp.mean(x * x, axis=-1, keepdims=True) + NORM_EPS)
        xh = x * r
        du_ = du_ref[...]
        dy = du_ * w_ref[...]
        dh = dr_ref[...] + r * (dy - xh * jnp.mean(dy * xh, axis=-1, keepdims=True))
        dh_ref[...] = dh
        dhb_ref[...] = dh.astype(BF16)
        part = jnp.sum(du_ * xh, axis=0, keepdims=True)

        @pl.when(pl.program_id(0) == 0)
        def _():
            dw_ref[...] = part

        @pl.when(pl.program_id(0) > 0)
        def _():
            dw_ref[...] += part

    row = pl.BlockSpec((tm, D), lambda i: (i, 0))
    vec = pl.BlockSpec((1, D), lambda i: (0, 0))
    extra_specs, extra = ([], ()) if dep is None else ([pl.BlockSpec(memory_space=pl.ANY)], (dep,))
    dh, dhb, dw = pl.pallas_call(
        body, name=name, grid=(T // tm,), in_specs=[row, vec, row, row] + extra_specs, out_specs=[row, row, vec],
        out_shape=[jax.ShapeDtypeStruct((T, D), F32), jax.ShapeDtypeStruct((T, D), BF16), jax.ShapeDtypeStruct((1, D), F32)],
        compiler_params=_cparams("arbitrary"))(h, w.reshape(1, D), du, dres, *extra)
    return dh, dhb, dw.reshape(D)


def _loss_head(h, w, target):
    T, D = h.shape
    tm = _pick_tile(T, 256)

    def body(h_ref, w_ref, t_ref, loss_ref, dh_ref, dhb_ref, dw_ref):
        x = h_ref[...]
        wv = w_ref[...]
        r = lax.rsqrt(jnp.mean(x * x, axis=-1, keepdims=True) + NORM_EPS)
        xh = x * r
        e = xh * wv - t_ref[...]
        lpart = jnp.zeros((1, LANES), F32) + 0.5 * jnp.sum(jnp.mean(e * e, axis=-1, keepdims=True))
        dyo = e * (1.0 / D)
        dy = dyo * wv
        dh = r * (dy - xh * jnp.mean(dy * xh, axis=-1, keepdims=True))
        dh_ref[...] = dh
        dhb_ref[...] = dh.astype(BF16)
        part = jnp.sum(dyo * xh, axis=0, keepdims=True)

        @pl.when(pl.program_id(0) == 0)
        def _():
            dw_ref[...] = part
            loss_ref[...] = lpart

        @pl.when(pl.program_id(0) > 0)
        def _():
            dw_ref[...] += part
            loss_ref[...] += lpart

    row = pl.BlockSpec((tm, D), lambda i: (i, 0))
    vec = pl.BlockSpec((1, D), lambda i: (0, 0))
    lvec = pl.BlockSpec((1, LANES), lambda i: (0, 0))
    loss, dh, dhb, dw = pl.pallas_call(
        body, name="loss_head", grid=(T // tm,), in_specs=[row, vec, row], out_specs=[lvec, row, row, vec],
        out_shape=[jax.ShapeDtypeStruct((1, LANES), F32), jax.ShapeDtypeStruct((T, D), F32),
                   jax.ShapeDtypeStruct((T, D), BF16), jax.ShapeDtypeStruct((1, D), F32)],
        compiler_params=_cparams("arbitrary"))(h, w.reshape(1, D), target)
    return loss, dh, dhb, dw.reshape(D)


def _hgrn_gates(qr, fr, lb):
    sig = jax.nn.sigmoid(fr)
    nsig = jax.nn.sigmoid(-fr)
    fg = lb + (1.0 - lb) * sig
    logf = jnp.log(fg)
    k = (1.0 - lb) * nsig
    q = _silu(qr)
    return q, k, logf, sig, nsig, fg


def _hgrn_scaled(q, k, b, bmid):
    eq = jnp.exp(jnp.clip(b - bmid, -EXP_CLIP, EXP_CLIP))
    ek = jnp.exp(jnp.clip(bmid - b, -EXP_CLIP, EXP_CLIP))
    return q * eq, k * ek, eq, ek


def _hgrn_fwd(proj, lb, gnw, H):
    B, L, _ = proj.shape
    C, DK = HGRN_CHUNK, HGRN_DK
    F_ = H * DK
    NC = L // C

    nh = HGRN_HEADS_PER_STEP if H % HGRN_HEADS_PER_STEP == 0 else 1
    LB = min(L, HGRN_SEQ_BLOCK)
    ncb, nsb, WD = LB // C, L // LB, nh * DK

    def body(q_ref, f_ref, v_ref, g_ref, lb_ref, gn_ref, o_ref, on_ref, st_ref, ST, bsc):
        @pl.when(pl.program_id(2) == 0)
        def _():
            ST[...] = jnp.zeros_like(ST)

        gn = gn_ref[...]
        causal = _iota((C, C), 0) >= _iota((C, C), 1)

        def chunk(c, carry):
            r0 = pl.multiple_of(c * C, C)
            rows = pl.ds(r0, C)
            for hh in range(nh):
                ln = slice(hh * DK, (hh + 1) * DK)
                q, k, logf, _, _, _ = _hgrn_gates(q_ref[0, rows, ln], f_ref[0, rows, ln], lb_ref[:, ln])
                v = v_ref[0, rows, ln]
                b = _cumsum_rows(logf)
                bsc[hh] = b
                bmid = bsc[hh, C // 2 - 1:C // 2, :]
                blast = bsc[hh, C - 1:C, :]
                qs, ks, _, _ = _hgrn_scaled(q, k, b, bmid)
                A = jnp.where(causal, _dot(qs, ks, NT), 0.0)
                st = ST[hh]
                st_ref[0, hh, c] = st
                o = _dot(A, v) + _dot(q * jnp.exp(b), st, NT)
                kb = k * jnp.exp(blast - b)
                ST[hh] = st * jnp.exp(blast) + _dot(v, kb, TN)
                rms = lax.rsqrt(jnp.mean(o * o, axis=-1, keepdims=True) + NORM_EPS)
                o_ref[0, rows, ln] = o
                on_ref[0, rows, ln] = (o * rms * gn * _silu(g_ref[0, rows, ln])).astype(BF16)
            return carry

        lax.fori_loop(0, ncb, chunk, 0)

    def col(off):
        return pl.BlockSpec((1, LB, WD), lambda b, hp, s: (b, s, off // nh + hp))

    return pl.pallas_call(
        body, name="hgrn_fwd", grid=(B, H // nh, nsb),
        in_specs=[col(0), col(H), col(2 * H), col(3 * H), pl.BlockSpec((1, WD), lambda b, hp, s: (0, hp)),
                  pl.BlockSpec((1, DK), lambda b, hp, s: (0, 0))],
        out_specs=[col(0), col(0), pl.BlockSpec((1, nh, ncb, DK, DK), lambda b, hp, s: (b, hp, s, 0, 0))],
        out_shape=[jax.ShapeDtypeStruct((B, L, F_), F32), jax.ShapeDtypeStruct((B, L, F_), BF16),
                   jax.ShapeDtypeStruct((B, H, NC, DK, DK), F32)],
        scratch_shapes=[pltpu.VMEM((nh, DK, DK), F32), pltpu.VMEM((nh, C, DK), F32)],
        compiler_params=_cparams("parallel", "parallel", "arbitrary"))(proj, proj, proj, proj, lb, gnw)


def _hgrn_bwd(proj, o, don, st, lb, gnw, H):
    B, L, _ = proj.shape
    C, DK = HGRN_CHUNK, HGRN_DK
    F_ = H * DK
    NC = L // C

    nh = HGRN_HEADS_PER_STEP if H % HGRN_HEADS_PER_STEP == 0 else 1
    LB = min(L, HGRN_SEQ_BLOCK)
    ncb, nsb, WD = LB // C, L // LB, nh * DK

    def body(q_ref, f_ref, v_ref, g_ref, o_ref, do_ref, st_ref, lb_ref, gn_ref,
             dp_ref, dlb_ref, dgn_ref, DST, bsc):
        @pl.when(pl.program_id(2) == 0)
        def _():
            DST[...] = jnp.zeros_like(DST)
            dlb_ref[...] = jnp.zeros_like(dlb_ref)
            dgn_ref[...] = jnp.zeros_like(dgn_ref)

        gn = gn_ref[...]
        causal = _iota((C, C), 0) >= _iota((C, C), 1)
        lastrow = _iota((C, DK), 0) == C - 1

        def chunk(i, carry):
            c = ncb - 1 - i
            r0 = pl.multiple_of(c * C, C)
            rows = pl.ds(r0, C)
            for hh in range(nh):
                ln = slice(hh * DK, (hh + 1) * DK)
                lbv = lb_ref[:, ln]
                qr = q_ref[0, rows, ln]
                fr = f_ref[0, rows, ln]
                q, k, logf, sig, nsig, fg = _hgrn_gates(qr, fr, lbv)
                v = v_ref[0, rows, ln]
                b = _cumsum_rows(logf)
                bsc[hh] = b
                bmid = bsc[hh, C // 2 - 1:C // 2, :]
                blast = bsc[hh, C - 1:C, :]
                qs, ks, eq, ek = _hgrn_scaled(q, k, b, bmid)
                A = jnp.where(causal, _dot(qs, ks, NT), 0.0)
                st_in = st_ref[0, hh, c]
                dst = DST[hh]
                eb = jnp.exp(b)
                ebl = jnp.exp(blast)
                ekb = jnp.exp(blast - b)
                qb = q * eb
                kb = k * ekb
                ov = o_ref[0, rows, ln]
                gr = g_ref[0, rows, ln]
                rms = lax.rsqrt(jnp.mean(ov * ov, axis=-1, keepdims=True) + NORM_EPS)
                oh = ov * rms
                sg = _silu(gr)
                don_ = do_ref[0, rows, ln]
                dgn_ref[0, hh] += jnp.sum(don_ * oh * sg, axis=0, keepdims=True)
                dp_ref[3, 0, rows, ln] = (don_ * oh * gn * _dsilu(gr)).astype(BF16)
                doh = don_ * gn * sg
                do_ = rms * (doh - oh * jnp.mean(doh * oh, axis=-1, keepdims=True))
                dA = jnp.where(causal, _dot(do_, v, NT), 0.0)
                dp_ref[2, 0, rows, ln] = (_dot(A, do_, TN) + _dot(kb, dst, NT)).astype(BF16)
                dqb = _dot(do_, st_in)
                dkb = _dot(v, dst)
                dq = _dot(dA, ks) * eq + dqb * eb
                dk_inter = dkb * ekb
                dk = _dot(dA, qs, TN) * ek + dk_inter
                db = q * dq - k * dk
                extra = jnp.sum(k * dk_inter, axis=0, keepdims=True) + ebl * jnp.sum(st_in * dst, axis=0, keepdims=True)
                db = db + jnp.where(lastrow, extra, 0.0)
                dlogf = _cumsum_rows(db, reverse=True)
                DST[hh] = dst * ebl + _dot(do_, qb, TN)
                dp_ref[0, 0, rows, ln] = (dq * _dsilu(qr)).astype(BF16)
                ss = sig * nsig
                dp_ref[1, 0, rows, ln] = ((1.0 - lbv) * ss * (dlogf / fg - dk)).astype(BF16)
                dlb_ref[0, :, ln] += jnp.sum(dlogf * nsig / fg - dk * nsig, axis=0, keepdims=True)
            return carry

        lax.fori_loop(0, ncb, chunk, 0)

    def col(off):
        return pl.BlockSpec((1, LB, WD), lambda b, hp, s: (b, nsb - 1 - s, off // nh + hp))

    outs = pl.pallas_call(
        body, name="hgrn_bwd", grid=(B, H // nh, nsb),
        in_specs=[col(0), col(H), col(2 * H), col(3 * H), col(0), col(0),
                  pl.BlockSpec((1, nh, ncb, DK, DK), lambda b, hp, s: (b, hp, nsb - 1 - s, 0, 0)),
                  pl.BlockSpec((1, WD), lambda b, hp, s: (0, hp)), pl.BlockSpec((1, DK), lambda b, hp, s: (0, 0))],
        out_specs=[pl.BlockSpec((4, 1, LB, WD), lambda b, hp, s: (0, b, nsb - 1 - s, hp)),
                   pl.BlockSpec((1, 1, WD), lambda b, hp, s: (b, 0, hp)),
                   pl.BlockSpec((1, nh, 1, DK), lambda b, hp, s: (b, hp, 0, 0))],
        out_shape=[jax.ShapeDtypeStruct((4, B, L, F_), BF16), jax.ShapeDtypeStruct((B, 1, F_), F32),
                   jax.ShapeDtypeStruct((B, H, 1, DK), F32)],
        scratch_shapes=[pltpu.VMEM((nh, DK, DK), F32), pltpu.VMEM((nh, C, DK), F32)],
        compiler_params=_cparams("parallel", "parallel", "arbitrary"))(proj, proj, proj, proj, o, don, st, lb, gnw)
    return outs


CONV_ROWS = 256
PAD_ROWS = 8


def _conv_taps(pad_ref, w_ref, r0, K, rb, forward=True, keep=False):
    ext = pad_ref[pl.ds(r0, rb + PAD_ROWS), :]
    n = rb + PAD_ROWS
    acc = None
    shifts = []
    for s in range(K):
        if forward:
            sh = ext if s == 0 else pltpu.roll(ext, s, 0)
            term = sh[PAD_ROWS:, :]
        else:
            sh = ext if s == 0 else pltpu.roll(ext, n - s, 0)
            term = sh[:rb, :]
        shifts.append(term)
        term = term * w_ref[K - 1 - s:K - s, :]
        acc = term if acc is None else acc + term
    return (acc, shifts) if keep else acc


def _conv_dw(shifts, dc, K):
    row = _iota((8, dc.shape[1]), 0)
    out = jnp.zeros((8, dc.shape[1]), F32)
    for kk in range(K):
        out = out + jnp.where(row == kk, jnp.sum(dc * shifts[K - 1 - kk], axis=0, keepdims=True), 0.0)
    return out


def _mconv_fwd(zx, cw, cb, col0, width):
    B, L, _ = zx.shape
    K = cw.shape[0]
    ct = _pick_tile(width, 256)
    rb = min(CONV_ROWS, L)
    nrb = L // rb
    off = col0 // ct

    def body(x_ref, w_ref, b_ref, y_ref, xp):
        xp[0:PAD_ROWS, :] = jnp.zeros((PAD_ROWS, ct), F32)
        xp[PAD_ROWS:, :] = x_ref[0]
        bias = b_ref[...]

        def blk(i, carry):
            r0 = pl.multiple_of(i * rb, rb)
            y_ref[0, pl.ds(r0, rb), :] = _silu(_conv_taps(xp, w_ref, r0, K, rb) + bias)
            return carry

        lax.fori_loop(0, nrb, blk, 0)

    return pl.pallas_call(
        body, name="mconv_fwd", grid=(B, width // ct),
        in_specs=[pl.BlockSpec((1, L, ct), lambda b, j: (b, 0, off + j)), pl.BlockSpec((K, ct), lambda b, j: (0, j)),
                  pl.BlockSpec((1, ct), lambda b, j: (0, j))],
        out_specs=pl.BlockSpec((1, L, ct), lambda b, j: (b, 0, j)),
        out_shape=jax.ShapeDtypeStruct((B, L, width), F32),
        scratch_shapes=[pltpu.VMEM((L + PAD_ROWS, ct), F32)],
        compiler_params=_cparams("parallel", "parallel"))(zx, cw, cb.reshape(1, width))


def _mconv_bwd(zx, dya, cw, cb, col0, wcol0, name):
    B, L, _ = zx.shape
    K = cw.shape[0]
    npart, _, _, wq = dya.shape
    width = npart * wq
    ct = _pick_tile(wq, 256)
    rb = min(CONV_ROWS, L)
    nrb = L // rb
    off = (col0 + wcol0) // ct
    woff = wcol0 // ct
    pq = wq // ct

    def body(x_ref, dy_ref, w_ref, b_ref, dx_ref, dw_ref, db_ref, xp, dcp):
        xp[0:PAD_ROWS, :] = jnp.zeros((PAD_ROWS, ct), F32)
        xp[PAD_ROWS:, :] = x_ref[0]
        dcp[L:, :] = jnp.zeros((PAD_ROWS, ct), F32)
        bias = b_ref[...]

        def blk1(i, carry):
            dw, db = carry
            r0 = pl.multiple_of(i * rb, rb)
            cpre, shifts = _conv_taps(xp, w_ref, r0, K, rb, keep=True)
            dc = dy_ref[0, 0, pl.ds(r0, rb), :] * _dsilu(cpre + bias)
            dcp[pl.ds(r0, rb), :] = dc
            return dw + _conv_dw(shifts, dc, K), db + jnp.sum(dc, axis=0, keepdims=True)

        dw, db = lax.fori_loop(0, nrb, blk1, (jnp.zeros((8, ct), F32), jnp.zeros((1, ct), F32)))
        dw_ref[0] = dw
        db_ref[0] = db

        def blk2(i, carry):
            r0 = pl.multiple_of(i * rb, rb)
            dx_ref[0, pl.ds(r0, rb), :] = _conv_taps(dcp, w_ref, r0, K, rb, forward=False).astype(BF16)
            return carry

        lax.fori_loop(0, nrb, blk2, 0)

    dx, dw, db = pl.pallas_call(
        body, name=name, grid=(B, width // ct),
        in_specs=[pl.BlockSpec((1, L, ct), lambda b, j: (b, 0, off + j)),
                  pl.BlockSpec((1, 1, L, ct), lambda b, j: (j // pq, b, 0, j % pq)),
                  pl.BlockSpec((K, ct), lambda b, j: (0, woff + j)), pl.BlockSpec((1, ct), lambda b, j: (0, woff + j))],
        out_specs=[pl.BlockSpec((1, L, ct), lambda b, j: (b, 0, j)), pl.BlockSpec((1, 8, ct), lambda b, j: (b, 0, j)),
                   pl.BlockSpec((1, 1, ct), lambda b, j: (b, 0, j))],
        out_shape=[jax.ShapeDtypeStruct((B, L, width), BF16), jax.ShapeDtypeStruct((B, 8, width), F32),
                   jax.ShapeDtypeStruct((B, 1, width), F32)],
        scratch_shapes=[pltpu.VMEM((L + PAD_ROWS, ct), F32), pltpu.VMEM((L + PAD_ROWS, ct), F32)],
        compiler_params=_cparams("parallel", "parallel"))(zx, dya, cw, cb.reshape(1, -1))
    return dx, dw[:, :K, :], db


def _ffn_mid_fwd(up, cw, cb, dff):
    B, L, _ = up.shape
    K = cw.shape[0]
    ct = _pick_tile(dff, 256)
    rb = min(CONV_ROWS, L)
    nrb = L // rb
    half = dff // ct

    def body(g_ref, u_ref, wg_ref, wu_ref, bg_ref, bu_ref, a_ref, gp, upad):
        gp[0:PAD_ROWS, :] = jnp.zeros((PAD_ROWS, ct), F32)
        upad[0:PAD_ROWS, :] = jnp.zeros((PAD_ROWS, ct), F32)
        gp[PAD_ROWS:, :] = g_ref[0]
        upad[PAD_ROWS:, :] = u_ref[0]
        bg, bu = bg_ref[...], bu_ref[...]

        def blk(i, carry):
            r0 = pl.multiple_of(i * rb, rb)
            cg = _conv_taps(gp, wg_ref, r0, K, rb) + bg
            cu = _conv_taps(upad, wu_ref, r0, K, rb) + bu
            a_ref[0, pl.ds(r0, rb), :] = (_silu(cg) * cu).astype(BF16)
            return carry

        lax.fori_loop(0, nrb, blk, 0)

    xg = pl.BlockSpec((1, L, ct), lambda b, j: (b, 0, j))
    xu = pl.BlockSpec((1, L, ct), lambda b, j: (b, 0, half + j))
    wgs = pl.BlockSpec((K, ct), lambda b, j: (0, j))
    wus = pl.BlockSpec((K, ct), lambda b, j: (0, half + j))
    bgs = pl.BlockSpec((1, ct), lambda b, j: (0, j))
    bus = pl.BlockSpec((1, ct), lambda b, j: (0, half + j))
    cb2 = cb.reshape(1, 2 * dff)
    return pl.pallas_call(
        body, name="ffn_mid_fwd", grid=(B, half), in_specs=[xg, xu, wgs, wus, bgs, bus], out_specs=xg,
        out_shape=jax.ShapeDtypeStruct((B, L, dff), BF16),
        scratch_shapes=[pltpu.VMEM((L + PAD_ROWS, ct), F32), pltpu.VMEM((L + PAD_ROWS, ct), F32)],
        compiler_params=_cparams("parallel", "parallel"))(up, up, cw, cw, cb2, cb2)


def _ffn_mid_bwd(up, dact, cw, cb, dff):
    B, L, _ = up.shape
    K = cw.shape[0]
    ct = _pick_tile(dff, 256)
    rb = min(CONV_ROWS, L)
    nrb = L // rb
    half = dff // ct

    def body(g_ref, u_ref, da_ref, wg_ref, wu_ref, bg_ref, bu_ref, dx_ref, dwg_ref, dwu_ref, dbg_ref, dbu_ref,
             gp, upad, dgp, dup):
        gp[0:PAD_ROWS, :] = jnp.zeros((PAD_ROWS, ct), F32)
        upad[0:PAD_ROWS, :] = jnp.zeros((PAD_ROWS, ct), F32)
        gp[PAD_ROWS:, :] = g_ref[0]
        upad[PAD_ROWS:, :] = u_ref[0]
        dgp[L:, :] = jnp.zeros((PAD_ROWS, ct), F32)
        dup[L:, :] = jnp.zeros((PAD_ROWS, ct), F32)
        bg, bu = bg_ref[...], bu_ref[...]

        def blk1(i, carry):
            dwg, dwu, dbg, dbu = carry
            r0 = pl.multiple_of(i * rb, rb)
            cg, sg_ = _conv_taps(gp, wg_ref, r0, K, rb, keep=True)
            cu, su_ = _conv_taps(upad, wu_ref, r0, K, rb, keep=True)
            cg = cg + bg
            cu = cu + bu
            da = da_ref[0, pl.ds(r0, rb), :]
            sig = jax.nn.sigmoid(cg)
            dcg = da * cu * (sig * (1.0 + cg * (1.0 - sig)))
            dcu = da * (cg * sig)
            dgp[pl.ds(r0, rb), :] = dcg
            dup[pl.ds(r0, rb), :] = dcu
            return (dwg + _conv_dw(sg_, dcg, K), dwu + _conv_dw(su_, dcu, K), dbg + jnp.sum(dcg, axis=0, keepdims=True),
                    dbu + jnp.sum(dcu, axis=0, keepdims=True))

        z8 = jnp.zeros((8, ct), F32)
        z1 = jnp.zeros((1, ct), F32)
        dwg, dwu, dbg, dbu = lax.fori_loop(0, nrb, blk1, (z8, z8, z1, z1))
        dwg_ref[0] = dwg
        dwu_ref[0] = dwu
        dbg_ref[0] = dbg
        dbu_ref[0] = dbu

        def blk2(i, carry):
            r0 = pl.multiple_of(i * rb, rb)
            dx_ref[0, 0, pl.ds(r0, rb), :] = _conv_taps(dgp, wg_ref, r0, K, rb, forward=False).astype(BF16)
            dx_ref[1, 0, pl.ds(r0, rb), :] = _conv_taps(dup, wu_ref, r0, K, rb, forward=False).astype(BF16)
            return carry

        lax.fori_loop(0, nrb, blk2, 0)

    xg = pl.BlockSpec((1, L, ct), lambda b, j: (b, 0, j))
    xu = pl.BlockSpec((1, L, ct), lambda b, j: (b, 0, half + j))
    wgs = pl.BlockSpec((K, ct), lambda b, j: (0, j))
    wus = pl.BlockSpec((K, ct), lambda b, j: (0, half + j))
    bgs = pl.BlockSpec((1, ct), lambda b, j: (0, j))
    bus = pl.BlockSpec((1, ct), lambda b, j: (0, half + j))
    w8 = pl.BlockSpec((1, 8, ct), lambda b, j: (b, 0, j))
    b1 = pl.BlockSpec((1, 1, ct), lambda b, j: (b, 0, j))
    cb2 = cb.reshape(1, 2 * dff)
    pad = pltpu.VMEM((L + PAD_ROWS, ct), F32)
    dx2, dwg, dwu, dbg, dbu = pl.pallas_call(
        body, name="ffn_mid_bwd", grid=(B, half), in_specs=[xg, xu, xg, wgs, wus, bgs, bus],
        out_specs=[pl.BlockSpec((2, 1, L, ct), lambda b, j: (0, b, 0, j)), w8, w8, b1, b1],
        out_shape=[jax.ShapeDtypeStruct((2, B, L, dff), BF16)] + [jax.ShapeDtypeStruct((B, 8, dff), F32)] * 2
        + [jax.ShapeDtypeStruct((B, 1, dff), F32)] * 2,
        scratch_shapes=[pad, pad, pad, pad],
        compiler_params=_cparams("parallel", "parallel"))(up, up, dact, cw, cw, cb2, cb2)
    dw = jnp.concatenate([dwg[:, :K], dwu[:, :K]], axis=-1)
    db = jnp.concatenate([dbg, dbu], axis=-1)
    return dx2, dw, db


def _ssd_consts(hpg, W):
    P = M_HEADDIM
    E = (_iota((LANES, W), 0) == _iota((LANES, W), 1) // P).astype(BF16)
    Ebig = (_iota((LANES, hpg * LANES), 0) == _iota((LANES, hpg * LANES), 1) // LANES).astype(BF16)
    causal = _iota((M_CHUNK, M_CHUNK), 0) >= _iota((M_CHUNK, M_CHUNK), 1)
    head_of_lane = _iota((1, W), 1) // P
    return E, Ebig, causal, head_of_lane


def _ssd_chunk_fwd(xs, Bm, Cm, dtr, bias, Aneg, E, Ebig, causal, head_of_lane, hpg, st, ar_sc, ae_sc):
    pre = dtr + bias
    dt = jnp.maximum(pre, 0.0) + jnp.log(1.0 + jnp.exp(-jnp.abs(pre)))
    Ad = dt * Aneg
    a_c = _cumsum_rows(Ad)
    ar_sc[...] = a_c.T
    aexp = _dot_exact(a_c, E)
    ae_sc[...] = aexp
    alast = ae_sc[M_CHUNK - 1:M_CHUNK, :]
    dtexp = _dot_exact(dt, E)
    X = xs * dtexp
    AC = _dot_exact(a_c, Ebig)
    CB = _dot(Cm, Bm, NT)
    Xb = X.astype(BF16)
    Ls = [jnp.where(causal, jnp.exp(jnp.minimum(AC[:, j * LANES:(j + 1) * LANES] - ar_sc[j:j + 1, :], 0.0)), 0.0)
          for j in range(hpg)]
    first = _iota((1, LANES), 1) < M_HEADDIM
    pairs = []
    for p in range(hpg // 2):
        Xp = Xb[:, p * LANES:(p + 1) * LANES]
        pairs.append(jnp.where(first, _dot(CB * Ls[2 * p], Xp), _dot(CB * Ls[2 * p + 1], Xp)))
    ydiag = pairs[0] if len(pairs) == 1 else jnp.concatenate(pairs, axis=1)
    ea = jnp.exp(aexp)
    yoff = ea * _dot(Cm, st)
    dec = jnp.exp(alast - aexp)
    return dict(dt=dt, a_c=a_c, aexp=aexp, alast=alast, dtexp=dtexp, X=X, Xb=Xb, CB=CB, Ls=Ls, ydiag=ydiag, ea=ea,
                yoff=yoff, dec=dec)


def _ssd_fwd(xbca, zx, dtc, bias, Aneg, Dexp, nw, hpg):
    B, L, _ = xbca.shape
    G, N, C = M_GROUPS, M_D_STATE, M_CHUNK
    W = hpg * M_HEADDIM
    DI = G * W
    NC = L // C
    LB = min(L, 4 * C)
    ncb = LB // C

    def body(xs_ref, b_ref, c_ref, z_ref, dt_ref, bias_ref, a_ref, d_ref, nw_ref, y_ref, yn_ref, st_ref, ST, ar_sc, ae_sc):
        @pl.when(pl.program_id(2) == 0)
        def _():
            ST[...] = jnp.zeros_like(ST)

        E, Ebig, causal, head_of_lane = _ssd_consts(hpg, W)
        bias_ = bias_ref[0]
        Aneg_ = a_ref[0]
        Dv = d_ref[...]
        nwv = nw_ref[...]

        def chunk(ci, carry):
            r0 = pl.multiple_of(ci * C, C)
            rows = pl.ds(r0, C)
            xs = xs_ref[0, rows, :]
            Bm = b_ref[0, rows, :]
            Cm = c_ref[0, rows, :]
            st = ST[...]
            st_ref[0, 0, ci] = st
            f = _ssd_chunk_fwd(xs, Bm, Cm, dt_ref[0, 0, ci], bias_, Aneg_, E, Ebig, causal, head_of_lane, hpg, st, ar_sc, ae_sc)
            y = f["ydiag"] + f["yoff"] + xs * Dv
            ST[...] = st * jnp.exp(f["alast"]) + _dot(Bm, f["X"] * f["dec"], TN)
            yg = y * _silu(z_ref[0, rows, :])
            rstd = lax.rsqrt(jnp.mean(yg * yg, axis=-1, keepdims=True) + NORM_EPS)
            y_ref[0, rows, :] = y
            yn_ref[0, rows, :] = (yg * rstd * nwv).astype(BF16)
            return carry

        lax.fori_loop(0, ncb, chunk, 0)

    xw = pl.BlockSpec((1, LB, W), lambda b, g, s: (b, s, g))
    bsp = pl.BlockSpec((1, LB, N), lambda b, g, s: (b, s, DI // N + g))
    csp = pl.BlockSpec((1, LB, N), lambda b, g, s: (b, s, DI // N + G + g))
    dts = pl.BlockSpec((1, 1, ncb, C, LANES), lambda b, g, s: (b, g, s, 0, 0))
    hv = pl.BlockSpec((1, 1, LANES), lambda b, g, s: (g, 0, 0))
    wv = pl.BlockSpec((1, W), lambda b, g, s: (0, g))
    sts = pl.BlockSpec((1, 1, ncb, N, W), lambda b, g, s: (b, g, s, 0, 0))
    return pl.pallas_call(
        body, name="ssd_fwd", grid=(B, G, L // LB), in_specs=[xw, bsp, csp, xw, dts, hv, hv, wv, wv],
        out_specs=[xw, xw, sts],
        out_shape=[jax.ShapeDtypeStruct((B, L, DI), F32), jax.ShapeDtypeStruct((B, L, DI), BF16),
                   jax.ShapeDtypeStruct((B, G, NC, N, W), F32)],
        scratch_shapes=[pltpu.VMEM((N, W), F32), pltpu.VMEM((LANES, C), F32), pltpu.VMEM((C, W), F32)],
        compiler_params=_cparams("parallel", "parallel", "arbitrary"))(xbca, xbca, xbca, zx, dtc, bias, Aneg, Dexp, nw)


def _ssd_bwd(xbca, zx, dtc, ypre, dyn, st, bias, Aneg, Dexp, nw, hpg):
    B, L, _ = xbca.shape
    G, N, C = M_GROUPS, M_D_STATE, M_CHUNK
    W = hpg * M_HEADDIM
    DI = G * W
    NC = L // C
    LB = min(L, 4 * C)
    ncb = LB // C
    nsb = L // LB

    def body(xs_ref, b_ref, c_ref, z_ref, dt_ref, y_ref, dyn_ref, st_ref, bias_ref, a_ref, d_ref, nw_ref,
             dxs_ref, dbc_ref, dz_ref, ddt_ref, dnw_ref, dd_ref, da_ref, dbias_ref, DST, ar_sc, ae_sc):
        @pl.when(pl.program_id(2) == 0)
        def _():
            DST[...] = jnp.zeros_like(DST)
            dnw_ref[...] = jnp.zeros_like(dnw_ref)
            dd_ref[...] = jnp.zeros_like(dd_ref)
            da_ref[...] = jnp.zeros_like(da_ref)
            dbias_ref[...] = jnp.zeros_like(dbias_ref)

        E, Ebig, causal, head_of_lane = _ssd_consts(hpg, W)
        bias_ = bias_ref[0]
        Aneg_ = a_ref[0]
        Dv = d_ref[...]
        nwv = nw_ref[...]
        lane = _iota((1, LANES), 1)
        subl = _iota((LANES, 1), 0)
        lastrow = _iota((C, W), 0) == C - 1

        def chunk(i, carry):
            ci = ncb - 1 - i
            r0 = pl.multiple_of(ci * C, C)
            rows = pl.ds(r0, C)
            xs = xs_ref[0, rows, :]
            Bm = b_ref[0, rows, :]
            Cm = c_ref[0, rows, :]
            zr = z_ref[0, rows, :]
            dtr = dt_ref[0, 0, ci]
            st_in = st_ref[0, 0, ci]
            dst = DST[...]
            f = _ssd_chunk_fwd(xs, Bm, Cm, dtr, bias_, Aneg_, E, Ebig, causal, head_of_lane, hpg, st_in, ar_sc, ae_sc)
            X, Xb, dec, ea, CB = f["X"], f["Xb"], f["dec"], f["ea"], f["CB"]
            y = y_ref[0, rows, :]
            sz = _silu(zr)
            yg = y * sz
            rstd = lax.rsqrt(jnp.mean(yg * yg, axis=-1, keepdims=True) + NORM_EPS)
            yh = yg * rstd
            dyn_ = dyn_ref[0, rows, :]
            dnw_ref[0, 0] += jnp.sum(dyn_ * yh, axis=0, keepdims=True)
            dyh = dyn_ * nwv
            dyg = rstd * (dyh - yh * jnp.mean(dyh * yh, axis=-1, keepdims=True))
            dz_ref[0, rows, :] = (dyg * y * _dsilu(zr)).astype(BF16)
            dy = dyg * sz
            dd_ref[0, 0] += jnp.sum(dy * xs, axis=0, keepdims=True)
            dxs = dy * Dv
            dYo = dy * ea
            daexp = dy * f["yoff"]
            dCm = _dot(dYo, st_in, NT)
            dst_in = _dot(Cm, dYo, TN)
            dyb = dy.astype(BF16)
            dCB = jnp.zeros((C, C), F32)
            da_col = jnp.zeros((C, LANES), F32)
            da_row = jnp.zeros((LANES, C), F32)
            first = lane < M_HEADDIM
            dXs = []
            for p in range(hpg // 2):
                Xp = Xb[:, p * LANES:(p + 1) * LANES]
                dYp = dyb[:, p * LANES:(p + 1) * LANES]
                dXp = None
                for j in (2 * p, 2 * p + 1):
                    Lj = f["Ls"][j]
                    Gj = CB * Lj
                    dYj = jnp.where(first if j % 2 == 0 else jnp.logical_not(first), dYp, jnp.zeros_like(dYp))
                    t = _dot(Gj, dYj, TN)
                    dXp = t if dXp is None else dXp + t
                    dGj = _dot(dYj, Xp, NT)
                    dCB = dCB + dGj * Lj
                    Wj = dGj * Gj
                    da_col = da_col + jnp.sum(Wj, axis=1, keepdims=True) * (lane == j).astype(F32)
                    da_row = da_row + (subl == j).astype(F32) * jnp.sum(Wj, axis=0, keepdims=True)
                dXs.append(dXp)
            dX = dXs[0] if len(dXs) == 1 else jnp.concatenate(dXs, axis=1)
            dCm = dCm + _dot(dCB, Bm)
            dBm = _dot(dCB, Cm, TN)
            ela = jnp.exp(f["alast"])
            dalast = jnp.sum(dst * st_in, axis=0, keepdims=True) * ela
            DST[...] = dst * ela + dst_in
            dXd = _dot(Bm, dst)
            dBm = dBm + _dot(X * dec, dst, NT)
            dX = dX + dXd * dec
            ddec = dXd * X * dec
            dalast = dalast + jnp.sum(ddec, axis=0, keepdims=True)
            daexp = daexp - ddec + jnp.where(lastrow, dalast, 0.0)
            dxs = dxs + dX * f["dtexp"]
            ddtexp = dX * xs
            ddt = _dot_exact(ddtexp, E, NT, passes=2)
            da_c = _dot_exact(daexp, E, NT, passes=2) + da_col - da_row.T
            dAd = _cumsum_rows(da_c, reverse=True)
            ddt = ddt + dAd * Aneg_
            da_ref[0, 0] += jnp.sum(dAd * f["dt"], axis=0, keepdims=True) * Aneg_
            ddtr = ddt * jax.nn.sigmoid(dtr + bias_)
            dbias_ref[0, 0] += jnp.sum(ddtr, axis=0, keepdims=True)
            ddt_ref[0, 0, ci] = ddtr
            dxs_ref[0, rows, :] = dxs
            dbc_ref[0, 0, rows, :] = dBm
            dbc_ref[1, 0, rows, :] = dCm
            return carry

        lax.fori_loop(0, ncb, chunk, 0)

    def rev(s):
        return nsb - 1 - s

    xw = pl.BlockSpec((1, LB, W), lambda b, g, s: (b, rev(s), g))
    bsp = pl.BlockSpec((1, LB, N), lambda b, g, s: (b, rev(s), DI // N + g))
    csp = pl.BlockSpec((1, LB, N), lambda b, g, s: (b, rev(s), DI // N + G + g))
    gsp = pl.BlockSpec((1, LB, N), lambda b, g, s: (b, rev(s), g))
    dts = pl.BlockSpec((1, 1, ncb, C, LANES), lambda b, g, s: (b, g, rev(s), 0, 0))
    hv = pl.BlockSpec((1, 1, LANES), lambda b, g, s: (g, 0, 0))
    wv = pl.BlockSpec((1, W), lambda b, g, s: (0, g))
    sts = pl.BlockSpec((1, 1, ncb, N, W), lambda b, g, s: (b, g, rev(s), 0, 0))
    accw = pl.BlockSpec((1, 1, 1, W), lambda b, g, s: (b, g, 0, 0))
    acch = pl.BlockSpec((1, 1, 1, LANES), lambda b, g, s: (b, g, 0, 0))
    return pl.pallas_call(
        body, name="ssd_bwd", grid=(B, G, nsb), in_specs=[xw, bsp, csp, xw, dts, xw, xw, sts, hv, hv, wv, wv],
        out_specs=[xw, pl.BlockSpec((2, 1, LB, N), lambda b, g, s: (0, b, rev(s), g)), xw, dts, accw, accw, acch, acch],
        out_shape=[jax.ShapeDtypeStruct((B, L, DI), F32), jax.ShapeDtypeStruct((2, B, L, G * N), F32),
                   jax.ShapeDtypeStruct((B, L, DI), BF16),
                   jax.ShapeDtypeStruct((B, G, NC, C, LANES), F32), jax.ShapeDtypeStruct((B, G, 1, W), F32),
                   jax.ShapeDtypeStruct((B, G, 1, W), F32), jax.ShapeDtypeStruct((B, G, 1, LANES), F32),
                   jax.ShapeDtypeStruct((B, G, 1, LANES), F32)],
        scratch_shapes=[pltpu.VMEM((N, W), F32), pltpu.VMEM((LANES, C), F32), pltpu.VMEM((C, W), F32)],
        compiler_params=_cparams("parallel", "parallel", "arbitrary"))(
            xbca, xbca, xbca, zx, dtc, ypre, dyn, st, bias, Aneg, Dexp, nw)


def _adamw(w, g, m, v, name, echo=False):
    shape = w.shape
    n = w.size
    cols = shape[-1]
    rows = n // cols
    tr = rows
    for cand in (512, 256, 128, 64, 32, 16, 8):
        if rows % cand == 0 and cand * cols * 4 <= 1024 * 1024:
            tr = cand
            break
    c1 = 1.0 / (1.0 - ADAM_B1 ** ADAM_STEP)
    c2 = 1.0 / (1.0 - ADAM_B2 ** ADAM_STEP)

    def body(w_ref, g_ref, m_ref, v_ref, d_ref, mo_ref, vo_ref, *go_ref):
        g_ = g_ref[...]
        mn = ADAM_B1 * m_ref[...] + (1.0 - ADAM_B1) * g_
        vn = ADAM_B2 * v_ref[...] + (1.0 - ADAM_B2) * (g_ * g_)
        d_ref[...] = -ADAM_LR * ((mn * c1) / (jnp.sqrt(vn * c2) + ADAM_EPS) + ADAM_WD * w_ref[...])
        mo_ref[...] = mn
        vo_ref[...] = vn
        if echo:
            go_ref[0][...] = g_

    spec = pl.BlockSpec((tr, cols), lambda i: (i, 0))
    r2 = lambda a: a.reshape(rows, cols)
    nout = 4 if echo else 3
    outs = pl.pallas_call(
        body, name=name, grid=(rows // tr,), in_specs=[spec] * 4, out_specs=[spec] * nout,
        out_shape=[jax.ShapeDtypeStruct((rows, cols), F32)] * nout,
        compiler_params=_cparams("parallel"))(r2(w), r2(g), r2(m), r2(v))
    return tuple(o.reshape(shape) for o in outs)


def _lower_bounds(lb_logits):
    p = jax.nn.softmax(lb_logits.astype(F32), axis=0)
    return jnp.cumsum(p, axis=0) - p[0]


def _pad_cols(a, n):
    return a if a.shape[-1] == n else jnp.pad(a, [(0, 0)] * (a.ndim - 1) + [(0, n - a.shape[-1])])


def _heads_to_lanes(a, G, hpg):
    return _pad_cols(a.reshape(G, 1, hpg), LANES)


def _local_step(x, target, P, fetch, emit):
    B, L, D = x.shape
    T = B * L
    depth = P["mix_norm"].shape[0]
    H = D // HGRN_DK
    F_ = H * HGRN_DK
    DI = P["m_norm"].shape[1]
    G, N = M_GROUPS, M_D_STATE
    MH = DI // M_HEADDIM
    hpg = MH // G
    assert hpg <= 8
    W = hpg * M_HEADDIM
    CD = DI + 2 * G * N
    MIN = DI + CD + MH
    MPAD = -(-MIN // LANES) * LANES
    dff = P["f_conv_b"].shape[1] // 2
    NC = L // M_CHUNK
    lbs = _lower_bounds(P["hgrn_lb_logits"])

    h = x.reshape(T, D)
    saved = []
    for i in range(depth):
        j = i // 2
        Wl = dict(fetch(i, ("mix_in",), h))
        s = {"h_in": h, "W": Wl}
        u = _rmsnorm_fwd(h, P["mix_norm"][i], "mix_norm_fwd")
        s["u"] = u
        if i % 2 == 0:
            proj = _matmul(u, Wl["mix_in"], name="hgrn_in_fwd").reshape(B, L, 4 * F_)
            o, on, st = _hgrn_fwd(proj, lbs[j].reshape(1, F_), P["hgrn_gnorm"][j].reshape(1, HGRN_DK), H)
            Wl.update(fetch(i, ("mix_out",), on))
            h = _matmul(on.reshape(T, F_), Wl["mix_out"], res=h, name="hgrn_out_fwd")
            s.update(proj=proj, o=o, on=on, st=st)
        else:
            zx = _matmul(u, Wl["mix_in"], tb=True, tn=1152, name="m_in_fwd").reshape(B, L, MPAD)
            xbca = _mconv_fwd(zx, P["m_conv_w"][j], P["m_conv_b"][j], DI, CD)
            dtr = zx[:, :, DI + CD:DI + CD + MH].reshape(B, NC, M_CHUNK, G, hpg).transpose(0, 3, 1, 2, 4)
            dtc = _pad_cols(dtr, LANES)
            bias = _heads_to_lanes(P["m_dt_bias"][j], G, hpg)
            Aneg = _heads_to_lanes(-jnp.exp(P["m_A_log"][j]), G, hpg)
            Dexp = jnp.repeat(P["m_D"][j], M_HEADDIM).reshape(1, DI)
            nw = P["m_norm"][j].reshape(1, DI)
            ypre, yn, st = _ssd_fwd(xbca, zx, dtc, bias, Aneg, Dexp, nw, hpg)
            Wl.update(fetch(i, ("mix_out",), yn))
            h = _matmul(yn.reshape(T, DI), Wl["mix_out"], res=h, name="m_out_fwd")
            s.update(zx=zx, xbca=xbca, dtc=dtc, bias=bias, Aneg=Aneg, Dexp=Dexp, nw=nw, ypre=ypre, yn=yn, st=st)
        s["h_mid"] = h
        u2 = _rmsnorm_fwd(h, P["ffn_norm"][i], "ffn_norm_fwd")
        Wl.update(fetch(i, ("f_w_up", "f_w_down"), h))
        up = _matmul(u2, Wl["f_w_up"], name="ffn_up_fwd").reshape(B, L, 2 * dff)
        act = _ffn_mid_fwd(up, P["f_conv_w"][i], P["f_conv_b"][i], dff)
        h = _matmul(act.reshape(T, dff), Wl["f_w_down"], res=h, name="ffn_down_fwd")
        s.update(u2=u2, up=up, act=act)
        saved.append(s)

    loss, dh, dhb, d_final = _loss_head(h, P["final_norm"], target.reshape(T, D))

    g = {k: [None] * P[k].shape[0] for k in ("mix_norm", "ffn_norm", "hgrn_gnorm", "m_conv_w", "m_conv_b", "m_dt_bias",
                                              "m_A_log", "m_D", "m_norm", "f_conv_w", "f_conv_b")}
    dlbs = [None] * lbs.shape[0]
    for i in reversed(range(depth)):
        j = i // 2
        s = saved[i]
        Wl = s["W"]
        gm = {}

        def dw(key, a, b, name, **kw):
            gm[key] = _matmul(a, b, ta=True, out_dtype=BF16, tk=T, name=name, **kw)

        dact = _matmul(dhb, Wl["f_w_down"], tb=True, name="ffn_down_dx").reshape(B, L, dff)
        dw("f_w_down", s["act"].reshape(T, dff), dhb, "ffn_down_dw")
        dup, dcw, dcb = _ffn_mid_bwd(s["up"], dact, P["f_conv_w"][i], P["f_conv_b"][i], dff)
        g["f_conv_w"][i] = jnp.sum(dcw, axis=0)
        g["f_conv_b"][i] = jnp.sum(dcb, axis=(0, 1))
        dup = dup.reshape(2, T, dff)
        dw("f_w_up", s["u2"], dup, "ffn_up_dw", b_parts=True)
        tok, finish = emit(i, {key: gm[key] for key in ("f_w_up", "f_w_down")})
        du2 = _matmul(dup, Wl["f_w_up"], a_parts=True, tb=True, name="ffn_up_dx", dep=tok)
        dh, dhb, g["ffn_norm"][i] = _rmsnorm_bwd(s["h_mid"], P["ffn_norm"][i], du2, dh, "ffn_norm_bwd", dep=finish(du2))
        if i % 2 == 0:
            don = _matmul(dhb, Wl["mix_out"], tb=True, name="hgrn_out_dx").reshape(B, L, F_)
            dw("mix_out", s["on"].reshape(T, F_), dhb, "hgrn_out_dw")
            dproj, dlb, dgn = _hgrn_bwd(s["proj"], s["o"], don, s["st"], lbs[j].reshape(1, F_),
                                        P["hgrn_gnorm"][j].reshape(1, HGRN_DK), H)
            dlbs[j] = jnp.sum(dlb, axis=(0, 1))
            g["hgrn_gnorm"][j] = jnp.sum(dgn, axis=(0, 1, 2))
            dproj = dproj.reshape(4, T, F_)
            dw("mix_in", s["u"], dproj, "hgrn_in_dw", b_parts=True)
            tok, finish = emit(i, {key: gm[key] for key in ("mix_in", "mix_out")})
            du = _matmul(dproj, Wl["mix_in"], a_parts=True, tb=True, name="hgrn_in_dx", dep=tok)
        else:
            dyn = _matmul(dhb, Wl["mix_out"], tb=True, name="m_out_dx").reshape(B, L, DI)
            dw("mix_out", s["yn"].reshape(T, DI), dhb, "m_out_dw")
            dxs, dbc, dz, ddt, dnw, dD, dA, dbias = _ssd_bwd(s["xbca"], s["zx"], s["dtc"], s["ypre"], dyn, s["st"],
                                                             s["bias"], s["Aneg"], s["Dexp"], s["nw"], hpg)
            g["m_norm"][j] = jnp.sum(dnw, axis=(0, 2)).reshape(DI)
            g["m_D"][j] = jnp.sum(dD, axis=(0, 2)).reshape(MH, M_HEADDIM).sum(axis=-1)
            g["m_A_log"][j] = jnp.sum(dA, axis=(0, 2))[:, :hpg].reshape(MH)
            g["m_dt_bias"][j] = jnp.sum(dbias, axis=(0, 2))[:, :hpg].reshape(MH)
            cw, cb = P["m_conv_w"][j], P["m_conv_b"][j]
            dxx, dcw_x, dcb_x = _mconv_bwd(s["zx"], dxs[None], cw, cb, DI, 0, "mconv_bwd_x")
            dxb, dcw_b, dcb_b = _mconv_bwd(s["zx"], dbc, cw, cb, DI, DI, "mconv_bwd_bc")
            g["m_conv_w"][j] = jnp.concatenate([jnp.sum(dcw_x, axis=0), jnp.sum(dcw_b, axis=0)], axis=-1)
            g["m_conv_b"][j] = jnp.concatenate([jnp.sum(dcb_x, axis=(0, 1)), jnp.sum(dcb_b, axis=(0, 1))], axis=-1)
            ddt_t = _pad_cols(ddt[..., :hpg].transpose(0, 2, 3, 1, 4).reshape(T, MH), MPAD - DI - CD).astype(BF16)
            pieces = [(dz.reshape(T, DI), 0), (dxx.reshape(T, DI), DI), (dxb.reshape(T, 2 * G * N), 2 * DI), (ddt_t, DI + CD)]
            gm["mix_in"] = lax.empty((MPAD, D), BF16)
            for n_, (piece, off) in enumerate(pieces):
                gm["mix_in"] = _matmul(piece, s["u"], ta=True, out_dtype=BF16, tk=T, out=gm["mix_in"], out_off=off,
                                       name="m_in_dw%d" % n_)
            tok, finish = emit(i, {key: gm[key] for key in ("mix_in", "mix_out")})
            du = None
            for n_, (piece, off) in enumerate(pieces):
                du = _matmul(piece, Wl["mix_in"], b_off=off, res=du, name="m_in_dx%d" % n_, dep=tok if n_ == 0 else None)
        dh, dhb, g["mix_norm"][i] = _rmsnorm_bwd(s["h_in"], P["mix_norm"][i], du, dh, "mix_norm_bwd", dep=finish(du))

    grads = {k: jnp.stack(vs) for k, vs in g.items()}
    grads["final_norm"] = d_final
    _, lb_vjp = jax.vjp(_lower_bounds, P["hgrn_lb_logits"])
    grads["hgrn_lb_logits"] = lb_vjp(jnp.stack(dlbs))[0]
    return loss, dh.reshape(B, L, D), grads


ANY = pl.BlockSpec(memory_space=pl.ANY)
N_CHIPS = 4
N_DEV = 8


def _place():
    x, y, c = lax.axis_index("x"), lax.axis_index("y"), lax.axis_index("c")
    sibling = (x, y, 1 - c)
    chips = [(1 - x, y), (x, 1 - y), (1 - x, 1 - y)]
    return x, y, c, sibling, chips


def _remote(src, dst, send_sem, recv_sem, to):
    return pltpu.make_async_remote_copy(src_ref=src, dst_ref=dst, send_sem=send_sem, recv_sem=recv_sem, device_id=to,
                                        device_id_type=MESH)


KIND_AXIS = {"hgrn_w_in": "col", "f_w_up": "col", "m_w_in_t": "row", "hgrn_w_out": "row", "m_w_out": "row", "f_w_down": "row"}
KINDS = tuple(KIND_AXIS)
PEER_MASKS = (2, 1, 3)
ALL = slice(None)


def _chip_win(axis, cw, s):
    return (ALL, slice(s * cw, (s + 1) * cw)) if axis == "col" else (slice(s * cw, (s + 1) * cw), ALL)


def _half_win(axis, rows, cols, h):
    return (slice(h * rows // 2, (h + 1) * rows // 2), ALL) if axis == "col" else (ALL, slice(h * cols // 2, (h + 1) * cols // 2))


def _per_place(fn):
    x, y, c, sibling, chips = _place()
    chip = 2 * x + y
    for s in range(N_CHIPS):
        for cc in range(2):
            @pl.when(jnp.logical_and(chip == s, c == cc))
            def _():
                fn(s, cc, c, sibling, chips)


HBM = pl.BlockSpec(memory_space=pltpu.HBM)
SEM = pl.BlockSpec(memory_space=pltpu.SEMAPHORE)
EFFECT = pltpu.SideEffectType.DATAFLOW_SIDE_EFFECTING


def _cell(axis, rows, cols, cw, s, h):
    if axis == "col":
        return (slice(h * rows // 2, (h + 1) * rows // 2), slice(s * cw, (s + 1) * cw))
    return (slice(s * cw, (s + 1) * cw), slice(h * cols // 2, (h + 1) * cols // 2))


def _in_hbm(a):
    return pltpu.with_memory_space_constraint(a, pltpu.HBM)


def _stage_shard(kind, shard, layer, chip, pad_rows=0, dep=None):
    _, R, C = shard.shape
    axis = KIND_AXIS[kind]
    tr, tc = _row_tile(R), _pick_tile(C, 2048)
    nr, nc = R // tr, C // tc
    full = (R, N_CHIPS * C) if axis == "col" else (N_CHIPS * R + pad_rows, C)

    def body(s_ref, x_ref, *rest):
        o_ref = rest[-1]
        o_ref[...] = x_ref[...].astype(BF16)

    if axis == "col":
        dst = pl.BlockSpec((tr, tc), lambda i, j, s_ref: (i, s_ref[0] * nc + j))
    else:
        dst = pl.BlockSpec((tr, tc), lambda i, j, s_ref: (s_ref[0] * nr + i, j))
    extra_specs, extra = ([], ()) if dep is None else ([ANY], (dep,))
    grid_spec = pltpu.PrefetchScalarGridSpec(
        num_scalar_prefetch=1, grid=(nr, nc),
        in_specs=[pl.BlockSpec((None, tr, tc), lambda i, j, s_ref: (layer, i, j))] + extra_specs, out_specs=dst)
    out = pl.pallas_call(
        body, name="stage_" + kind, grid_spec=grid_spec, out_shape=jax.ShapeDtypeStruct(full, BF16),
        compiler_params=_cparams("parallel", "parallel"))(chip.reshape(1).astype(jnp.int32), shard, *extra)
    if pad_rows:
        rows0 = N_CHIPS * R
        pr = math.gcd(rows0, pad_rows)

        def zero_body(x_ref, o_ref):
            o_ref[...] = jnp.zeros_like(o_ref)

        out = pl.pallas_call(
            zero_body, name="zero_pad_" + kind, grid=(pad_rows // pr,), in_specs=[ANY],
            out_specs=pl.BlockSpec((pr, C), lambda i: (rows0 // pr + i, 0)), out_shape=jax.ShapeDtypeStruct(full, BF16),
            input_output_aliases={0: 0}, compiler_params=_cparams("parallel"))(out)
    return out


def _gather_start(items, mats, cws, after, name):
    n = len(items)

    def body(*refs):
        send_sems, recv_sems, token = refs[n + 1], refs[n + 2], refs[-1]
        m = refs[n + 3:2 * n + 3]

        def run(s, cc, c, sibling, chips):
            for q, (k, _) in enumerate(items):
                r, c_ = m[q].shape
                mine = m[q].at[_cell(KIND_AXIS[k], r, c_, cws[k], s, cc)]
                for j, (px, py) in enumerate(chips):
                    _remote(mine, mine, send_sems.at[3 * q + j], recv_sems.at[3 * q + j], (px, py, c)).start()

        _per_place(run)
        token[...] = jnp.zeros_like(token)

    outs = pl.pallas_call(
        body, name=name, in_specs=[HBM] * n + [ANY],
        out_specs=[SEM, SEM] + [HBM] * n + [pl.BlockSpec(memory_space=pltpu.VMEM)],
        out_shape=[pltpu.SemaphoreType.DMA((3 * n,)), pltpu.SemaphoreType.DMA((3 * n,))]
        + [pltpu.HBM(a.shape, a.dtype) for a in mats] + [jax.ShapeDtypeStruct((8, LANES), F32)],
        input_output_aliases={q: 2 + q for q in range(n)},
        compiler_params=pltpu.CompilerParams(has_side_effects=EFFECT),
    )(*[_in_hbm(a) for a in mats], after)
    return outs[0], outs[1], list(outs[2:2 + n]), outs[-1]


def _gather_wait(items, idx, mats, send_sems, recv_sems, cws, after, name):
    n = len(idx)

    def body(*refs):
        m = refs[:n]
        s_sems, r_sems = refs[n], refs[n + 1]

        def run(s, cc, c, sibling, chips):
            for a, q in enumerate(idx):
                k = items[q][0]
                r, c_ = m[a].shape
                mine = m[a].at[_cell(KIND_AXIS[k], r, c_, cws[k], s, cc)]
                for j, (px, py) in enumerate(chips):
                    theirs = m[a].at[_cell(KIND_AXIS[k], r, c_, cws[k], s ^ PEER_MASKS[j], cc)]
                    cp = _remote(mine, theirs, s_sems.at[3 * q + j], r_sems.at[3 * q + j], (px, py, c))
                    cp.wait_send()
                    cp.wait_recv()

        _per_place(run)

    outs = pl.pallas_call(
        body, name=name, in_specs=[HBM] * n + [SEM, SEM, ANY], out_specs=[HBM] * n,
        out_shape=[pltpu.HBM(a.shape, a.dtype) for a in mats], input_output_aliases={a: a for a in range(n)},
        compiler_params=pltpu.CompilerParams(has_side_effects=EFFECT),
    )(*mats, send_sems, recv_sems, after)
    return list(outs)


def _forward_halves(kinds, mats, cws, name):
    n = len(mats)

    def body(*refs):
        m = refs[n:2 * n]
        send_sems, recv_sems = refs[2 * n:]

        def run(s, cc, c, sibling, chips):
            cps = []
            for a, k in enumerate(kinds):
                r, c_ = m[a].shape
                for j in range(3):
                    have = m[a].at[_cell(KIND_AXIS[k], r, c_, cws[k], s ^ PEER_MASKS[j], cc)]
                    cps.append(_remote(have, have, send_sems.at[3 * a + j], recv_sems.at[3 * a + j], sibling))
            for cp in cps:
                cp.start()
            for cp in cps:
                cp.wait()

        _per_place(run)

    outs = pl.pallas_call(
        body, name=name, in_specs=[ANY] * n, out_specs=[ANY] * n,
        out_shape=[jax.ShapeDtypeStruct(a.shape, a.dtype) for a in mats], input_output_aliases={a: a for a in range(n)},
        scratch_shapes=[pltpu.SemaphoreType.DMA((3 * n,)), pltpu.SemaphoreType.DMA((3 * n,))],
    )(*mats)
    return list(outs)


def _swap_start(kinds, gms, name):
    n = len(gms)
    lands = [lax.empty((g.shape[0] // 2, g.shape[1]) if KIND_AXIS[k] == "col" else (g.shape[0], g.shape[1] // 2), BF16)
             for k, g in zip(kinds, gms)]

    def body(*refs):
        send_sems, recv_sems, token = refs[2 * n], refs[2 * n + 1], refs[-1]
        g, ra = refs[2 * n + 2:3 * n + 2], refs[3 * n + 2:4 * n + 2]

        def run(s, cc, c, sibling, chips):
            for a, k in enumerate(kinds):
                r, c_ = g[a].shape
                _remote(g[a].at[_half_win(KIND_AXIS[k], r, c_, 1 - cc)], ra[a], send_sems.at[a], recv_sems.at[a],
                        sibling).start()

        _per_place(run)
        token[...] = jnp.zeros_like(token)

    outs = pl.pallas_call(
        body, name=name, in_specs=[HBM] * (2 * n),
        out_specs=[SEM, SEM] + [HBM] * (2 * n) + [pl.BlockSpec(memory_space=pltpu.VMEM)],
        out_shape=[pltpu.SemaphoreType.DMA((n,)), pltpu.SemaphoreType.DMA((n,))]
        + [pltpu.HBM(a.shape, a.dtype) for a in gms + lands] + [jax.ShapeDtypeStruct((8, LANES), F32)],
        input_output_aliases={q: 2 + q for q in range(2 * n)},
        compiler_params=pltpu.CompilerParams(has_side_effects=EFFECT),
    )(*[_in_hbm(a) for a in gms + lands])
    return outs[0], outs[1], list(outs[2:2 + n]), list(outs[2 + n:2 + 2 * n]), outs[-1]


def _swap_wait(kinds, gms, lands, send_sems, recv_sems, after, name):
    n = len(gms)

    def body(*refs):
        g, ra = refs[:n], refs[n:2 * n]
        s_sems, r_sems = refs[2 * n], refs[2 * n + 1]

        def run(s, cc, c, sibling, chips):
            for a, k in enumerate(kinds):
                r, c_ = g[a].shape
                cp = _remote(g[a].at[_half_win(KIND_AXIS[k], r, c_, 1 - cc)], ra[a], s_sems.at[a], r_sems.at[a], sibling)
                cp.wait_send()
                cp.wait_recv()

        _per_place(run)

    outs = pl.pallas_call(
        body, name=name, in_specs=[HBM] * (2 * n) + [SEM, SEM, ANY], out_specs=[HBM] * (2 * n),
        out_shape=[pltpu.HBM(a.shape, a.dtype) for a in gms + lands], input_output_aliases={a: a for a in range(2 * n)},
        compiler_params=pltpu.CompilerParams(has_side_effects=EFFECT),
    )(*gms, *lands, send_sems, recv_sems, after)
    return list(outs[:n]), list(outs[n:])


def _win_shape(kind, pa, cw):
    return (pa.shape[0], cw) if KIND_AXIS[kind] == "col" else (cw, pa.shape[1])


def _scatter_start(kinds, pas, cws, name):
    n = len(pas)
    lands = [lax.empty((3,) + _win_shape(k, p, cws[k]), BF16) for k, p in zip(kinds, pas)]

    def body(*refs):
        send_sems, recv_sems, token = refs[2 * n], refs[2 * n + 1], refs[-1]
        p, rb = refs[2 * n + 2:3 * n + 2], refs[3 * n + 2:4 * n + 2]

        def run(s, cc, c, sibling, chips):
            for a, k in enumerate(kinds):
                for j, (px, py) in enumerate(chips):
                    src = p[a].at[_chip_win(KIND_AXIS[k], cws[k], s ^ PEER_MASKS[j])]
                    _remote(src, rb[a].at[j], send_sems.at[3 * a + j], recv_sems.at[3 * a + j], (px, py, c)).start()

        _per_place(run)
        token[...] = jnp.zeros_like(token)

    outs = pl.pallas_call(
        body, name=name, in_specs=[HBM] * (2 * n),
        out_specs=[SEM, SEM] + [HBM] * (2 * n) + [pl.BlockSpec(memory_space=pltpu.VMEM)],
        out_shape=[pltpu.SemaphoreType.DMA((3 * n,)), pltpu.SemaphoreType.DMA((3 * n,))]
        + [pltpu.HBM(a.shape, a.dtype) for a in pas + lands] + [jax.ShapeDtypeStruct((8, LANES), F32)],
        input_output_aliases={q: 2 + q for q in range(2 * n)},
        compiler_params=pltpu.CompilerParams(has_side_effects=EFFECT),
    )(*[_in_hbm(a) for a in pas + lands])
    return outs[0], outs[1], list(outs[2:2 + n]), list(outs[2 + n:2 + 2 * n]), outs[-1]


def _scatter_wait(kinds, pas, lands, send_sems, recv_sems, cws, after, name):
    n = len(pas)

    def body(*refs):
        p, rb = refs[:n], refs[n:2 * n]
        s_sems, r_sems = refs[2 * n], refs[2 * n + 1]

        def run(s, cc, c, sibling, chips):
            for a, k in enumerate(kinds):
                for j, (px, py) in enumerate(chips):
                    src = p[a].at[_chip_win(KIND_AXIS[k], cws[k], s ^ PEER_MASKS[j])]
                    cp = _remote(src, rb[a].at[j], s_sems.at[3 * a + j], r_sems.at[3 * a + j], (px, py, c))
                    cp.wait_send()
                    cp.wait_recv()

        _per_place(run)

    outs = pl.pallas_call(
        body, name=name, in_specs=[HBM] * (2 * n) + [SEM, SEM, ANY], out_specs=[HBM] * (2 * n),
        out_shape=[pltpu.HBM(a.shape, a.dtype) for a in pas + lands], input_output_aliases={a: a for a in range(2 * n)},
        compiler_params=pltpu.CompilerParams(has_side_effects=EFFECT),
    )(*pas, *lands, send_sems, recv_sems, after)
    return list(outs[:n]), list(outs[n:])


def _share_halves(g):
    nq = len(KINDS)

    def body(*refs):
        out = dict(zip(KINDS, refs[nq:2 * nq]))
        send_sems, recv_sems = refs[2 * nq:]

        def run(s, cc, c, sibling, chips):
            cps = []
            for q, k in enumerate(KINDS):
                _, r, c_ = out[k].shape
                mine = out[k].at[(ALL,) + _half_win(KIND_AXIS[k], r, c_, cc)]
                cps.append(_remote(mine, mine, send_sems.at[q], recv_sems.at[q], sibling))
            for cp in cps:
                cp.start()
            for cp in cps:
                cp.wait()

        _per_place(run)

    outs = pl.pallas_call(
        body, name="share_halves", in_specs=[ANY] * nq, out_specs=[ANY] * nq,
        out_shape=[jax.ShapeDtypeStruct(g[k].shape, F32) for k in KINDS],
        input_output_aliases={q: q for q in range(nq)},
        scratch_shapes=[pltpu.SemaphoreType.DMA((nq,)), pltpu.SemaphoreType.DMA((nq,))],
    )(*[g[k] for k in KINDS])
    return dict(zip(KINDS, outs))


def _all_gather_small(xs, name):
    m_per, n = xs.shape

    def body(x_ref, out_ref, send_sems, recv_sems, local_sem):
        x, y, c, sibling, chips = _place()
        me = (x, y, c)

        def rows(px, py, pc):
            return out_ref.at[pl.ds((4 * px + 2 * py + pc) * m_per, m_per), :]

        def copy(k, block, to, src=None):
            return _remote(rows(*block) if src is None else src, rows(*block), send_sems.at[k], recv_sems.at[k], to)

        mine = pltpu.make_async_copy(x_ref, rows(*me), local_sem)
        mine.start()
        first = [copy(0, me, sibling, src=x_ref)]
        first += [copy(1 + j, me, (*chip, c), src=x_ref) for j, chip in enumerate(chips)]
        for cp in first:
            cp.start()
        passed = [copy(4 + j, (*chip, c), sibling) for j, chip in enumerate(chips)]
        for j, chip in enumerate(chips):
            copy(1 + j, (*chip, c), me).wait_recv()
            passed[j].start()
        copy(0, sibling, me).wait_recv()
        for j, chip in enumerate(chips):
            copy(4 + j, (*chip, 1 - c), me).wait_recv()
        for cp in first + passed:
            cp.wait_send()
        mine.wait()

    vm = pl.BlockSpec(memory_space=pltpu.VMEM)
    return pl.pallas_call(
        body, name=name, in_specs=[vm], out_specs=vm, out_shape=jax.ShapeDtypeStruct((N_DEV * m_per, n), xs.dtype),
        scratch_shapes=[pltpu.SemaphoreType.DMA((7,)), pltpu.SemaphoreType.DMA((7,)), pltpu.SemaphoreType.DMA],
        compiler_params=pltpu.CompilerParams(vmem_limit_bytes=VMEM_LIMIT_BYTES),
    )(xs)


def _row_tile(rows, cap=512):
    for mult in (16, 8):
        best = None
        t = mult
        while t <= min(rows, cap):
            if rows % t == 0:
                best = t
            t += mult
        if best is not None:
            return best
    raise ValueError(rows)


def _add_sibling(kind, g, ra, core):
    R, C = ra.shape
    axis = KIND_AXIS[kind]
    tr, tc = _row_tile(R), _pick_tile(C, 2048)
    nr, nc = R // tr, C // tc

    def body(c_ref, a_ref, b_ref, o_ref):
        o_ref[...] = (a_ref[...].astype(F32) + b_ref[...].astype(F32)).astype(o_ref.dtype)

    if axis == "col":
        own = pl.BlockSpec((tr, tc), lambda i, j, c_ref: (c_ref[0] * nr + i, j))
    else:
        own = pl.BlockSpec((tr, tc), lambda i, j, c_ref: (i, c_ref[0] * nc + j))
    same = pl.BlockSpec((tr, tc), lambda i, j, c_ref: (i, j))
    grid_spec = pltpu.PrefetchScalarGridSpec(num_scalar_prefetch=1, grid=(nr, nc), in_specs=[own, same], out_specs=same)
    return pl.pallas_call(
        body, name="add_sibling_" + kind, grid_spec=grid_spec, out_shape=jax.ShapeDtypeStruct(ra.shape, BF16),
        compiler_params=_cparams("parallel", "parallel"))(core.reshape(1).astype(jnp.int32), g, ra)


def _sum_chips(kind, pa, rb, chip, core, out, layer):
    _, R, C = rb.shape
    axis = KIND_AXIS[kind]
    tr, tc = _row_tile(R), _pick_tile(C, 2048)
    nr, nc = R // tr, C // tc

    def body(s_ref, c_ref, a_ref, b0_ref, b1_ref, b2_ref, old_ref, o_ref):
        o_ref[...] = ((a_ref[...].astype(F32) + b0_ref[...].astype(F32)) + b1_ref[...].astype(F32)) + b2_ref[...].astype(F32)

    def rb_spec(n):
        return pl.BlockSpec((None, tr, tc), lambda i, j, s_ref, c_ref: (n, i, j))

    if axis == "col":
        own = pl.BlockSpec((tr, tc), lambda i, j, s_ref, c_ref: (i, s_ref[0] * nc + j))
        dst = pl.BlockSpec((None, tr, tc), lambda i, j, s_ref, c_ref: (layer, c_ref[0] * nr + i, j))
        assert out.shape[1:] == (2 * R, C)
    else:
        own = pl.BlockSpec((tr, tc), lambda i, j, s_ref, c_ref: (s_ref[0] * nr + i, j))
        dst = pl.BlockSpec((None, tr, tc), lambda i, j, s_ref, c_ref: (layer, i, c_ref[0] * nc + j))
        assert out.shape[1:] == (R, 2 * C)
    grid_spec = pltpu.PrefetchScalarGridSpec(
        num_scalar_prefetch=2, grid=(nr, nc), in_specs=[own, rb_spec(0), rb_spec(1), rb_spec(2), ANY], out_specs=dst)
    return pl.pallas_call(
        body, name="sum_chips_" + kind, grid_spec=grid_spec, out_shape=jax.ShapeDtypeStruct(out.shape, F32),
        input_output_aliases={6: 0}, compiler_params=_cparams("parallel", "parallel"))(
            chip.reshape(1).astype(jnp.int32), core.reshape(1).astype(jnp.int32), pa, rb, rb, rb, out)


def _sum_devices(gathered):
    M = gathered.shape[0] // N_DEV
    C = gathered.shape[1]

    def body(g_ref, o_ref):
        acc = g_ref[0:M, :]
        for d in range(1, N_DEV):
            acc = acc + g_ref[d * M:(d + 1) * M, :]
        o_ref[...] = acc

    vm = pl.BlockSpec(memory_space=pltpu.VMEM)
    return pl.pallas_call(body, name="sum_devices", in_specs=[vm], out_specs=vm, out_shape=jax.ShapeDtypeStruct((M, C), F32),
                          compiler_params=pltpu.CompilerParams(vmem_limit_bytes=VMEM_LIMIT_BYTES))(gathered)


WEIGHTS = ["mix_norm", "ffn_norm", "final_norm", "hgrn_w_in", "hgrn_lb_logits", "hgrn_gnorm", "hgrn_w_out", "m_w_in",
           "m_conv_w", "m_conv_b", "m_dt_bias", "m_A_log", "m_D", "m_norm", "m_w_out", "f_w_up", "f_conv_w", "f_conv_b",
           "f_w_down"]
BIG_COLS = ("hgrn_w_in", "m_w_in", "f_w_up")
BIG_ROWS = ("hgrn_w_out", "m_w_out", "f_w_down")
BIG = BIG_COLS + BIG_ROWS
SMALL_SHARDED = ("m_conv_w", "m_conv_b", "m_norm", "f_conv_w")
SMALL_REPLICATED = ("mix_norm", "ffn_norm", "final_norm", "hgrn_lb_logits", "hgrn_gnorm", "m_dt_bias", "m_A_log", "m_D",
                    "f_conv_b")
SMALL = SMALL_REPLICATED + SMALL_SHARDED


def _pack_rows(arrs, row_mult=8):
    flat = jnp.concatenate([a.reshape(-1).astype(F32) for a in arrs])
    unit = FLAT_COLS * row_mult
    n = -(-flat.size // unit) * unit
    return jnp.pad(flat, (0, n - flat.size)).reshape(-1, FLAT_COLS)


def _unpack_rows(buf, shapes):
    flat = buf.reshape(-1)
    out, off = [], 0
    for shp in shapes:
        n = math.prod(shp)
        out.append(flat[off:off + n].reshape(shp))
        off += n
    return out


def kernel(x, mix_norm, ffn_norm, final_norm, hgrn_w_in, hgrn_lb_logits, hgrn_gnorm, hgrn_w_out, m_w_in, m_conv_w, m_conv_b, m_dt_bias, m_A_log, m_D, m_norm, m_w_out, f_w_up, f_conv_w, f_conv_b, f_w_down, loss_target, m_mix_norm, m_ffn_norm, m_final_norm, m_hgrn_w_in, m_hgrn_lb_logits, m_hgrn_gnorm, m_hgrn_w_out, m_m_w_in, m_m_conv_w, m_m_conv_b, m_m_dt_bias, m_m_A_log, m_m_D, m_m_norm, m_m_w_out, m_f_w_up, m_f_conv_w, m_f_conv_b, m_f_w_down, v_mix_norm, v_ffn_norm, v_final_norm, v_hgrn_w_in, v_hgrn_lb_logits, v_hgrn_gnorm, v_hgrn_w_out, v_m_w_in, v_m_conv_w, v_m_conv_b, v_m_dt_bias, v_m_A_log, v_m_D, v_m_norm, v_m_w_out, v_f_w_up, v_f_conv_w, v_f_conv_b, v_f_w_down):
    given = dict(locals())
    w = {n: given[n] for n in WEIGHTS}
    mom1 = {n: given["m_" + n] for n in WEIGHTS}
    mom2 = {n: given["v_" + n] for n in WEIGHTS}
    chip = 2 * lax.axis_index("x") + lax.axis_index("y")
    core = lax.axis_index("c")

    shards = {k: w[k] for k in KINDS if k != "m_w_in_t"}
    shards["m_w_in_t"] = w["m_w_in"].transpose(0, 2, 1).astype(BF16)
    m_in = N_CHIPS * w["m_w_in"].shape[2]
    pad_rows = {"m_w_in_t": -(-m_in // LANES) * LANES - m_in}
    cws = {k: shards[k].shape[2] if KIND_AXIS[k] == "col" else shards[k].shape[1] for k in KINDS}
    depth = w["mix_norm"].shape[0]

    def layer_kinds(i):
        mixer = {"mix_in": ("hgrn_w_in", i // 2), "mix_out": ("hgrn_w_out", i // 2)} if i % 2 == 0 else \
                {"mix_in": ("m_w_in_t", i // 2), "mix_out": ("m_w_out", i // 2)}
        return {**mixer, "f_w_up": ("f_w_up", i), "f_w_down": ("f_w_down", i)}

    own = _pack_rows([w[n] for n in SMALL_SHARDED])
    all_small = _all_gather_small(own, "gather_small_params")
    groups, started = [list(layer_kinds(0).values()), [it for i in range(1, depth) for it in layer_kinds(i).values()]], []
    after = all_small
    for n_, items in enumerate(groups):
        staged = [_stage_shard(k, shards[k], l, chip, pad_rows.get(k, 0), dep=None if n_ == 0 else after) for k, l in items]
        send_sems, recv_sems, mats, after = _gather_start(items, staged, cws, after, "gather_start_%d" % n_)
        started.append((items, send_sems, recv_sems, mats))
    all_small = all_small.reshape(N_CHIPS, 2, -1)[:, 0]
    per_chip = [_unpack_rows(all_small[s], [w[n].shape for n in SMALL_SHARDED]) for s in range(N_CHIPS)]
    P = {}
    for i, n in enumerate(SMALL_SHARDED):
        P[n] = jnp.concatenate([per_chip[s][i] for s in range(N_CHIPS)], axis=-1)
    for n in SMALL_REPLICATED:
        P[n] = w[n]

    def fetch(i, keys, h):
        lk = {key: layer_kinds(i)[key] for key in keys}
        items, send_sems, recv_sems, mats = started[0 if i == 0 else 1]
        idx = [items.index(it) for it in lk.values()]
        tag = "%d_%s" % (i, keys[0])
        if i == 0 and keys[0] == "mix_in":
            h = after
        got = _gather_wait(items, idx, [mats[q] for q in idx], send_sems, recv_sems, cws, h, "gather_wait_" + tag)
        got = _forward_halves([k for k, _ in lk.values()], got, cws, "forward_halves_" + tag)
        return dict(zip(lk.keys(), got))

    pending = []

    def emit(i, gm):
        lk = {key: layer_kinds(i)[key] for key in gm}
        kinds = [k for k, _ in lk.values()]
        gms = list(gm.values())
        tag = "%d_%s" % (i, next(iter(gm)))
        s1, r1, gms, half_lands, tok = _swap_start(kinds, gms, "swap_start_" + tag)

        def finish(after):
            gms2, ra = _swap_wait(kinds, gms, half_lands, s1, r1, after, "swap_wait_" + tag)
            pas = [_add_sibling(k, g_, r_, core) for k, g_, r_ in zip(kinds, gms2, ra)]
            s_sems, r_sems, pas, lands, tok2 = _scatter_start(kinds, pas, cws, "scatter_start_" + tag)
            pending.append((tag, list(lk.values()), pas, lands, s_sems, r_sems))
            return tok2

        return tok, finish

    loss_part, grad_x, g_full = _local_step(x, loss_target, P, fetch, emit)

    g_sh = {k: lax.empty(shards[k].shape, F32) for k in KINDS}
    for tag, its, pas, lands, s_sems, r_sems in pending:
        kinds = [k for k, _ in its]
        pas, lands = _scatter_wait(kinds, pas, lands, s_sems, r_sems, cws, grad_x, "scatter_wait_" + tag)
        for (k, l), p_, rb_ in zip(its, pas, lands):
            g_sh[k] = _sum_chips(k, p_, rb_, chip, core, g_sh[k], l)
    g_sh = _share_halves(g_sh)
    grads = {k: g_sh[k] for k in KINDS if k != "m_w_in_t"}
    grads["m_w_in"] = g_sh["m_w_in_t"].transpose(0, 2, 1)

    small_shapes = [g_full[n].shape for n in SMALL] + [(1,)]
    packed = _pack_rows([g_full[n] for n in SMALL] + [loss_part[0, 0:1]])
    summed = _sum_devices(_all_gather_small(packed, "gather_small_grads"))
    small = _unpack_rows(summed, small_shapes)
    loss = small[-1][0]
    for n, gs in zip(SMALL, small[:-1]):
        if n in SMALL_SHARDED:
            width = w[n].shape[-1]
            gs = lax.dynamic_slice_in_dim(gs, chip * width, width, axis=gs.ndim - 1)
        grads[n] = gs

    delta, new_m, new_v = {}, {}, {}
    for n in BIG:
        if n == "m_w_in":
            delta[n], new_m[n], new_v[n] = _adamw(w[n], grads[n], mom1[n], mom2[n], "adamw_" + n)
        else:
            delta[n], new_m[n], new_v[n], grads[n] = _adamw(w[n], grads[n], mom1[n], mom2[n], "adamw_" + n, echo=True)
    shapes = [w[n].shape for n in SMALL]
    ds, ms, vs = _adamw(_pack_rows([w[n] for n in SMALL]), _pack_rows([grads[n] for n in SMALL]),
                        _pack_rows([mom1[n] for n in SMALL]), _pack_rows([mom2[n] for n in SMALL]), "adamw_small")
    for n, d_, m_, v_ in zip(SMALL, _unpack_rows(ds, shapes), _unpack_rows(ms, shapes), _unpack_rows(vs, shapes)):
        delta[n], new_m[n], new_v[n] = d_, m_, v_

    return (loss, grad_x, *[grads[n] for n in WEIGHTS], *[delta[n] for n in WEIGHTS], *[new_m[n] for n in WEIGHTS],
            *[new_v[n] for n in WEIGHTS])
```

```python
import functools
import math

import jax
import jax.numpy as jnp
from jax import lax
from jax.experimental import pallas as pl
from jax.experimental.pallas import tpu as pltpu

F32 = jnp.float32
BF16 = jnp.bfloat16
NORM_EPS = 1e-5
HGRN_DK = 128
HGRN_CHUNK = 64
HGRN_HEADS_PER_STEP = 8
HGRN_SEQ_BLOCK = 256
M_HEADDIM = 64
M_GROUPS = 8
M_D_STATE = 128
M_CONV = 4
M_CHUNK = 128
FFN_CONV = 3
EXP_CLIP = 80.0
LANES = 128
VMEM_LIMIT_BYTES = 56 * 1024 * 1024
FLAT_COLS = 1024
ADAM_LR, ADAM_B1, ADAM_B2, ADAM_EPS, ADAM_WD, ADAM_STEP = 0.001, 0.9, 0.999, 1e-08, 0.01, 10
MESH = pl.DeviceIdType.MESH

NN = (((1,), (0,)), ((), ()))
NT = (((1,), (1,)), ((), ()))
TN = (((0,), (0,)), ((), ()))


def _cparams(*sems):
    return pltpu.CompilerParams(dimension_semantics=sems, vmem_limit_bytes=VMEM_LIMIT_BYTES)


def _dot(a, b, dn=NN):
    return lax.dot_general(a.astype(BF16), b.astype(BF16), dn, preferred_element_type=F32)


def _dot_exact(x, m, dn=NN, passes=3, x_first=True):
    acc = None
    r = x
    for _ in range(passes):
        p = r.astype(BF16)
        r = r - p.astype(F32)
        t = lax.dot_general(p, m, dn, preferred_element_type=F32) if x_first else lax.dot_general(m, p, dn, preferred_element_type=F32)
        acc = t if acc is None else acc + t
    return acc


def _iota(shape, dim):
    return lax.broadcasted_iota(jnp.int32, shape, dim)


def _cumsum_rows(x, reverse=False):
    n = x.shape[0]
    row = _iota(x.shape, 0)
    s = 1
    while s < n:
        if reverse:
            x = x + jnp.where(row < n - s, pltpu.roll(x, n - s, 0), 0.0)
        else:
            x = x + jnp.where(row >= s, pltpu.roll(x, s, 0), 0.0)
        s *= 2
    return x


def _silu(x):
    return x * jax.nn.sigmoid(x)


def _dsilu(x):
    s = jax.nn.sigmoid(x)
    return s * (1.0 + x * (1.0 - s))


def _pick_tile(dim, pref):
    if dim <= pref:
        return dim
    best = None
    t = LANES
    while t <= pref:
        if dim % t == 0:
            best = t
        t += LANES
    assert best is not None, (dim, pref)
    return best


def _matmul(a, b, *, ta=False, tb=False, res=None, out_dtype=F32, tm=1024, tn=1024, tk=4096, name,
            a_parts=False, b_parts=False, b_layer=None, b_off=0, out=None, out_layer=None, out_off=0, dep=None):
    a = a.astype(BF16)
    b = b.astype(BF16)
    if a_parts:
        assert not ta
        pa, M, kp = a.shape
        K = pa * kp
    else:
        M, K = (a.shape[1], a.shape[0]) if ta else a.shape
    bsh = b.shape[1:] if b_layer is not None else b.shape
    if b_parts:
        assert not tb
        pb, _, np_ = bsh
        N = pb * np_
    else:
        N = bsh[0] if tb else bsh[1]
    tm, tn, tk = _pick_tile(M, tm), _pick_tile(np_ if b_parts else N, tn), _pick_tile(kp if a_parts else K, tk)
    nk = K // tk
    dn = (((0 if ta else 1,), (1 if tb else 0,)), ((), ()))
    assert b_off % tk == 0 and out_off % tm == 0

    def body(*refs):
        refs = list(refs)
        acc = refs.pop() if nk > 1 else None
        o_ref = refs.pop()
        if dep is not None:
            refs.pop()
        if out is not None:
            refs.pop()
        a_ref, b_ref = refs[0], refs[1]
        r_ref = refs[2] if res is not None else None
        k = pl.program_id(2)

        def prod():
            return lax.dot_general(a_ref[...], b_ref[...], dn, preferred_element_type=F32)

        def finish(r):
            if res is not None:
                r = r + r_ref[...]
            o_ref[...] = r.astype(out_dtype)

        if nk == 1:
            finish(prod())
            return

        @pl.when(k == 0)
        def _():
            acc[...] = prod()

        @pl.when(jnp.logical_and(k > 0, k < nk - 1))
        def _():
            acc[...] += prod()

        @pl.when(k == nk - 1)
        def _():
            finish(acc[...] + prod())

    if a_parts:
        kpb = kp // tk
        a_spec = pl.BlockSpec((None, tm, tk), lambda i, j, k: (k // kpb, i, k % kpb))
    elif ta:
        a_spec = pl.BlockSpec((tk, tm), lambda i, j, k: (k, i))
    else:
        a_spec = pl.BlockSpec((tm, tk), lambda i, j, k: (i, k))
    lead = () if b_layer is None else (b_layer,)
    lead_blk = () if b_layer is None else (None,)
    kb0 = b_off // tk
    if b_parts:
        npb = np_ // tn
        b_spec = pl.BlockSpec(lead_blk + (None, tk, tn), lambda i, j, k: lead + (j // npb, k, j % npb))
    elif tb:
        b_spec = pl.BlockSpec(lead_blk + (tn, tk), lambda i, j, k: lead + (j, k))
    else:
        b_spec = pl.BlockSpec(lead_blk + (tk, tn), lambda i, j, k: lead + (kb0 + k, j))
    r_spec = pl.BlockSpec((tm, tn), lambda i, j, k: (i, j))
    in_specs = [a_spec, b_spec] + ([r_spec] if res is not None else [])
    args = (a, b) + ((res,) if res is not None else ())
    if out is None:
        o_spec, out_shape, aliases = r_spec, jax.ShapeDtypeStruct((M, N), out_dtype), {}
    else:
        assert out.dtype == out_dtype and out.shape[-1] == N
        olead = () if out_layer is None else (out_layer,)
        olead_blk = () if out_layer is None else (None,)
        ob0 = out_off // tm
        o_spec = pl.BlockSpec(olead_blk + (tm, tn), lambda i, j, k: olead + (ob0 + i, j))
        out_shape = jax.ShapeDtypeStruct(out.shape, out.dtype)
        aliases = {len(args): 0}
        in_specs = in_specs + [pl.BlockSpec(memory_space=pl.ANY)]
        args = args + (out,)
    if dep is not None:
        in_specs = in_specs + [pl.BlockSpec(memory_space=pl.ANY)]
        args = args + (dep,)
    return pl.pallas_call(
        body, name=name, grid=(M // tm, N // tn, nk), in_specs=in_specs, out_specs=o_spec, out_shape=out_shape,
        scratch_shapes=[pltpu.VMEM((tm, tn), F32)] if nk > 1 else [], input_output_aliases=aliases,
        compiler_params=_cparams("parallel", "parallel", "arbitrary"))(*args)


def _rmsnorm_fwd(h, w, name):
    T, D = h.shape
    tm = _pick_tile(T, 256)

    def body(h_ref, w_ref, u_ref):
        x = h_ref[...]
        r = lax.rsqrt(jnp.mean(x * x, axis=-1, keepdims=True) + NORM_EPS)
        u_ref[...] = (x * r * w_ref[...]).astype(BF16)

    return pl.pallas_call(
        body, name=name, grid=(T // tm,),
        in_specs=[pl.BlockSpec((tm, D), lambda i: (i, 0)), pl.BlockSpec((1, D), lambda i: (0, 0))],
        out_specs=pl.BlockSpec((tm, D), lambda i: (i, 0)), out_shape=jax.ShapeDtypeStruct((T, D), BF16),
        compiler_params=_cparams("parallel"))(h, w.reshape(1, D))


def _rmsnorm_bwd(h, w, du, dres, name, dep=None):
    T, D = h.shape
    tm = _pick_tile(T, 256)

    def body(h_ref, w_ref, du_ref, dr_ref, *rest):
        dh_ref, dhb_ref, dw_ref = rest[-3:]
        x = h_ref[...]
        r = lax.rsqrt(jnp.mean(x * x, axis=-1, keepdims=True) + NORM_EPS)
        xh = x * r
        du_ = du_ref[...]
        dy = du_ * w_ref[...]
        dh = dr_ref[...] + r * (dy - xh * jnp.mean(dy * xh, axis=-1, keepdims=True))
        dh_ref[...] = dh
        dhb_ref[...] = dh.astype(BF16)
        part = jnp.sum(du_ * xh, axis=0, keepdims=True)

        @pl.when(pl.program_id(0) == 0)
        def _():
            dw_ref[...] = part

        @pl.when(pl.program_id(0) > 0)
        def _():
            dw_ref[...] += part

    row = pl.BlockSpec((tm, D), lambda i: (i, 0))
    vec = pl.BlockSpec((1, D), lambda i: (0, 0))
    extra_specs, extra = ([], ()) if dep is None else ([pl.BlockSpec(memory_space=pl.ANY)], (dep,))
    dh, dhb, dw = pl.pallas_call(
        body, name=name, grid=(T // tm,), in_specs=[row, vec, row, row] + extra_specs, out_specs=[row, row, vec],
        out_shape=[jax.ShapeDtypeStruct((T, D), F32), jax.ShapeDtypeStruct((T, D), BF16), jax.ShapeDtypeStruct((1, D), F32)],
        compiler_params=_cparams("arbitrary"))(h, w.reshape(1, D), du, dres, *extra)
    return dh, dhb, dw.reshape(D)


def _loss_head(h, w, target):
    T, D = h.shape
    tm = _pick_tile(T, 256)

    def body(h_ref, w_ref, t_ref, loss_ref, dh_ref, dhb_ref, dw_ref):
        x = h_ref[...]
        wv = w_ref[...]
        r = lax.rsqrt(jnp.mean(x * x, axis=-1, keepdims=True) + NORM_EPS)
        xh = x * r
        e = xh * wv - t_ref[...]
        lpart = jnp.zeros((1, LANES), F32) + 0.5 * jnp.sum(jnp.mean(e * e, axis=-1, keepdims=True))
        dyo = e * (1.0 / D)
        dy = dyo * wv
        dh = r * (dy - xh * jnp.mean(dy * xh, axis=-1, keepdims=True))
        dh_ref[...] = dh
        dhb_ref[...] = dh.astype(BF16)
        part = jnp.sum(dyo * xh, axis=0, keepdims=True)

        @pl.when(pl.program_id(0) == 0)
        def _():
            dw_ref[...] = part
            loss_ref[...] = lpart

        @pl.when(pl.program_id(0) > 0)
        def _():
            dw_ref[...] += part
            loss_ref[...] += lpart

    row = pl.BlockSpec((tm, D), lambda i: (i, 0))
    vec = pl.BlockSpec((1, D), lambda i: (0, 0))
    lvec = pl.BlockSpec((1, LANES), lambda i: (0, 0))
    loss, dh, dhb, dw = pl.pallas_call(
        body, name="loss_head", grid=(T // tm,), in_specs=[row, vec, row], out_specs=[lvec, row, row, vec],
        out_shape=[jax.ShapeDtypeStruct((1, LANES), F32), jax.ShapeDtypeStruct((T, D), F32),
                   jax.ShapeDtypeStruct((T, D), BF16), jax.ShapeDtypeStruct((1, D), F32)],
        compiler_params=_cparams("arbitrary"))(h, w.reshape(1, D), target)
    return loss, dh, dhb, dw.reshape(D)


def _hgrn_gates(qr, fr, lb):
    sig = jax.nn.sigmoid(fr)
    nsig = jax.nn.sigmoid(-fr)
    fg = lb + (1.0 - lb) * sig
    logf = jnp.log(fg)
    k = (1.0 - lb) * nsig
    q = _silu(qr)
    return q, k, logf, sig, nsig, fg


def _hgrn_scaled(q, k, b, bmid):
    eq = jnp.exp(jnp.clip(b - bmid, -EXP_CLIP, EXP_CLIP))
    ek = jnp.exp(jnp.clip(bmid - b, -EXP_CLIP, EXP_CLIP))
    return q * eq, k * ek, eq, ek


def _hgrn_fwd(proj, lb, gnw, H):
    B, L, _ = proj.shape
    C, DK = HGRN_CHUNK, HGRN_DK
    F_ = H * DK
    NC = L // C

    nh = HGRN_HEADS_PER_STEP if H % HGRN_HEADS_PER_STEP == 0 else 1
    LB = min(L, HGRN_SEQ_BLOCK)
    ncb, nsb, WD = LB // C, L // LB, nh * DK

    def body(q_ref, f_ref, v_ref, g_ref, lb_ref, gn_ref, o_ref, on_ref, st_ref, ST, bsc):
        @pl.when(pl.program_id(2) == 0)
        def _():
            ST[...] = jnp.zeros_like(ST)

        gn = gn_ref[...]
        causal = _iota((C, C), 0) >= _iota((C, C), 1)

        def chunk(c, carry):
            r0 = pl.multiple_of(c * C, C)
            rows = pl.ds(r0, C)
            for hh in range(nh):
                ln = slice(hh * DK, (hh + 1) * DK)
                q, k, logf, _, _, _ = _hgrn_gates(q_ref[0, rows, ln], f_ref[0, rows, ln], lb_ref[:, ln])
                v = v_ref[0, rows, ln]
                b = _cumsum_rows(logf)
                bsc[hh] = b
                bmid = bsc[hh, C // 2 - 1:C // 2, :]
                blast = bsc[hh, C - 1:C, :]
                qs, ks, _, _ = _hgrn_scaled(q, k, b, bmid)
                A = jnp.where(causal, _dot(qs, ks, NT), 0.0)
                st = ST[hh]
                st_ref[0, hh, c] = st
                o = _dot(A, v) + _dot(q * jnp.exp(b), st, NT)
                kb = k * jnp.exp(blast - b)
                ST[hh] = st * jnp.exp(blast) + _dot(v, kb, TN)
                rms = lax.rsqrt(jnp.mean(o * o, axis=-1, keepdims=True) + NORM_EPS)
                o_ref[0, rows, ln] = o
                on_ref[0, rows, ln] = (o * rms * gn * _silu(g_ref[0, rows, ln])).astype(BF16)
            return carry

        lax.fori_loop(0, ncb, chunk, 0)

    def col(off):
        return pl.BlockSpec((1, LB, WD), lambda b, hp, s: (b, s, off // nh + hp))

    return pl.pallas_call(
        body, name="hgrn_fwd", grid=(B, H // nh, nsb),
        in_specs=[col(0), col(H), col(2 * H), col(3 * H), pl.BlockSpec((1, WD), lambda b, hp, s: (0, hp)),
                  pl.BlockSpec((1, DK), lambda b, hp, s: (0, 0))],
        out_specs=[col(0), col(0), pl.BlockSpec((1, nh, ncb, DK, DK), lambda b, hp, s: (b, hp, s, 0, 0))],
        out_shape=[jax.ShapeDtypeStruct((B, L, F_), F32), jax.ShapeDtypeStruct((B, L, F_), BF16),
                   jax.ShapeDtypeStruct((B, H, NC, DK, DK), F32)],
        scratch_shapes=[pltpu.VMEM((nh, DK, DK), F32), pltpu.VMEM((nh, C, DK), F32)],
        compiler_params=_cparams("parallel", "parallel", "arbitrary"))(proj, proj, proj, proj, lb, gnw)


def _hgrn_bwd(proj, o, don, st, lb, gnw, H):
    B, L, _ = proj.shape
    C, DK = HGRN_CHUNK, HGRN_DK
    F_ = H * DK
    NC = L // C

    nh = HGRN_HEADS_PER_STEP if H % HGRN_HEADS_PER_STEP == 0 else 1
    LB = min(L, HGRN_SEQ_BLOCK)
    ncb, nsb, WD = LB // C, L // LB, nh * DK

    def body(q_ref, f_ref, v_ref, g_ref, o_ref, do_ref, st_ref, lb_ref, gn_ref,
             dp_ref, dlb_ref, dgn_ref, DST, bsc):
        @pl.when(pl.program_id(2) == 0)
        def _():
            DST[...] = jnp.zeros_like(DST)
            dlb_ref[...] = jnp.zeros_like(dlb_ref)
            dgn_ref[...] = jnp.zeros_like(dgn_ref)

        gn = gn_ref[...]
        causal = _iota((C, C), 0) >= _iota((C, C), 1)
        lastrow = _iota((C, DK), 0) == C - 1

        def chunk(i, carry):
            c = ncb - 1 - i
            r0 = pl.multiple_of(c * C, C)
            rows = pl.ds(r0, C)
            for hh in range(nh):
                ln = slice(hh * DK, (hh + 1) * DK)
                lbv = lb_ref[:, ln]
                qr = q_ref[0, rows, ln]
                fr = f_ref[0, rows, ln]
                q, k, logf, sig, nsig, fg = _hgrn_gates(qr, fr, lbv)
                v = v_ref[0, rows, ln]
                b = _cumsum_rows(logf)
                bsc[hh] = b
                bmid = bsc[hh, C // 2 - 1:C // 2, :]
                blast = bsc[hh, C - 1:C, :]
                qs, ks, eq, ek = _hgrn_scaled(q, k, b, bmid)
                A = jnp.where(causal, _dot(qs, ks, NT), 0.0)
                st_in = st_ref[0, hh, c]
                dst = DST[hh]
                eb = jnp.exp(b)
                ebl = jnp.exp(blast)
                ekb = jnp.exp(blast - b)
                qb = q * eb
                kb = k * ekb
                ov = o_ref[0, rows, ln]
                gr = g_ref[0, rows, ln]
                rms = lax.rsqrt(jnp.mean(ov * ov, axis=-1, keepdims=True) + NORM_EPS)
                oh = ov * rms
                sg = _silu(gr)
                don_ = do_ref[0, rows, ln]
                dgn_ref[0, hh] += jnp.sum(don_ * oh * sg, axis=0, keepdims=True)
                dp_ref[3, 0, rows, ln] = (don_ * oh * gn * _dsilu(gr)).astype(BF16)
                doh = don_ * gn * sg
                do_ = rms * (doh - oh * jnp.mean(doh * oh, axis=-1, keepdims=True))
                dA = jnp.where(causal, _dot(do_, v, NT), 0.0)
                dp_ref[2, 0, rows, ln] = (_dot(A, do_, TN) + _dot(kb, dst, NT)).astype(BF16)
                dqb = _dot(do_, st_in)
                dkb = _dot(v, dst)
                dq = _dot(dA, ks) * eq + dqb * eb
                dk_inter = dkb * ekb
                dk = _dot(dA, qs, TN) * ek + dk_inter
                db = q * dq - k * dk
                extra = jnp.sum(k * dk_inter, axis=0, keepdims=True) + ebl * jnp.sum(st_in * dst, axis=0, keepdims=True)
                db = db + jnp.where(lastrow, extra, 0.0)
                dlogf = _cumsum_rows(db, reverse=True)
                DST[hh] = dst * ebl + _dot(do_, qb, TN)
                dp_ref[0, 0, rows, ln] = (dq * _dsilu(qr)).astype(BF16)
                ss = sig * nsig
                dp_ref[1, 0, rows, ln] = ((1.0 - lbv) * ss * (dlogf / fg - dk)).astype(BF16)
                dlb_ref[0, :, ln] += jnp.sum(dlogf * nsig / fg - dk * nsig, axis=0, keepdims=True)
            return carry

        lax.fori_loop(0, ncb, chunk, 0)

    def col(off):
        return pl.BlockSpec((1, LB, WD), lambda b, hp, s: (b, nsb - 1 - s, off // nh + hp))

    outs = pl.pallas_call(
        body, name="hgrn_bwd", grid=(B, H // nh, nsb),
        in_specs=[col(0), col(H), col(2 * H), col(3 * H), col(0), col(0),
                  pl.BlockSpec((1, nh, ncb, DK, DK), lambda b, hp, s: (b, hp, nsb - 1 - s, 0, 0)),
                  pl.BlockSpec((1, WD), lambda b, hp, s: (0, hp)), pl.BlockSpec((1, DK), lambda b, hp, s: (0, 0))],
        out_specs=[pl.BlockSpec((4, 1, LB, WD), lambda b, hp, s: (0, b, nsb - 1 - s, hp)),
                   pl.BlockSpec((1, 1, WD), lambda b, hp, s: (b, 0, hp)),
                   pl.BlockSpec((1, nh, 1, DK), lambda b, hp, s: (b, hp, 0, 0))],
        out_shape=[jax.ShapeDtypeStruct((4, B, L, F_), BF16), jax.ShapeDtypeStruct((B, 1, F_), F32),
                   jax.ShapeDtypeStruct((B, H, 1, DK), F32)],
        scratch_shapes=[pltpu.VMEM((nh, DK, DK), F32), pltpu.VMEM((nh, C, DK), F32)],
        compiler_params=_cparams("parallel", "parallel", "arbitrary"))(proj, proj, proj, proj, o, don, st, lb, gnw)
    return outs


CONV_ROWS = 256
PAD_ROWS = 8


def _conv_taps(pad_ref, w_ref, r0, K, rb, forward=True, keep=False):
    ext = pad_ref[pl.ds(r0, rb + PAD_ROWS), :]
    n = rb + PAD_ROWS
    acc = None
    shifts = []
    for s in range(K):
        if forward:
            sh = ext if s == 0 else pltpu.roll(ext, s, 0)
            term = sh[PAD_ROWS:, :]
        else:
            sh = ext if s == 0 else pltpu.roll(ext, n - s, 0)
            term = sh[:rb, :]
        shifts.append(term)
        term = term * w_ref[K - 1 - s:K - s, :]
        acc = term if acc is None else acc + term
    return (acc, shifts) if keep else acc


def _conv_dw(shifts, dc, K):
    row = _iota((8, dc.shape[1]), 0)
    out = jnp.zeros((8, dc.shape[1]), F32)
    for kk in range(K):
        out = out + jnp.where(row == kk, jnp.sum(dc * shifts[K - 1 - kk], axis=0, keepdims=True), 0.0)
    return out


def _mconv_fwd(zx, cw, cb, col0, width):
    B, L, _ = zx.shape
    K = cw.shape[0]
    ct = _pick_tile(width, 256)
    rb = min(CONV_ROWS, L)
    nrb = L // rb
    off = col0 // ct

    def body(x_ref, w_ref, b_ref, y_ref, xp):
        xp[0:PAD_ROWS, :] = jnp.zeros((PAD_ROWS, ct), F32)
        xp[PAD_ROWS:, :] = x_ref[0]
        bias = b_ref[...]

        def blk(i, carry):
            r0 = pl.multiple_of(i * rb, rb)
            y_ref[0, pl.ds(r0, rb), :] = _silu(_conv_taps(xp, w_ref, r0, K, rb) + bias)
            return carry

        lax.fori_loop(0, nrb, blk, 0)

    return pl.pallas_call(
        body, name="mconv_fwd", grid=(B, width // ct),
        in_specs=[pl.BlockSpec((1, L, ct), lambda b, j: (b, 0, off + j)), pl.BlockSpec((K, ct), lambda b, j: (0, j)),
                  pl.BlockSpec((1, ct), lambda b, j: (0, j))],
        out_specs=pl.BlockSpec((1, L, ct), lambda b, j: (b, 0, j)),
        out_shape=jax.ShapeDtypeStruct((B, L, width), F32),
        scratch_shapes=[pltpu.VMEM((L + PAD_ROWS, ct), F32)],
        compiler_params=_cparams("parallel", "parallel"))(zx, cw, cb.reshape(1, width))


def _mconv_bwd(zx, dya, cw, cb, col0, wcol0, name):
    B, L, _ = zx.shape
    K = cw.shape[0]
    npart, _, _, wq = dya.shape
    width = npart * wq
    ct = _pick_tile(wq, 256)
    rb = min(CONV_ROWS, L)
    nrb = L // rb
    off = (col0 + wcol0) // ct
    woff = wcol0 // ct
    pq = wq // ct

    def body(x_ref, dy_ref, w_ref, b_ref, dx_ref, dw_ref, db_ref, xp, dcp):
        xp[0:PAD_ROWS, :] = jnp.zeros((PAD_ROWS, ct), F32)
        xp[PAD_ROWS:, :] = x_ref[0]
        dcp[L:, :] = jnp.zeros((PAD_ROWS, ct), F32)
        bias = b_ref[...]

        def blk1(i, carry):
            dw, db = carry
            r0 = pl.multiple_of(i * rb, rb)
            cpre, shifts = _conv_taps(xp, w_ref, r0, K, rb, keep=True)
            dc = dy_ref[0, 0, pl.ds(r0, rb), :] * _dsilu(cpre + bias)
            dcp[pl.ds(r0, rb), :] = dc
            return dw + _conv_dw(shifts, dc, K), db + jnp.sum(dc, axis=0, keepdims=True)

        dw, db = lax.fori_loop(0, nrb, blk1, (jnp.zeros((8, ct), F32), jnp.zeros((1, ct), F32)))
        dw_ref[0] = dw
        db_ref[0] = db

        def blk2(i, carry):
            r0 = pl.multiple_of(i * rb, rb)
            dx_ref[0, pl.ds(r0, rb), :] = _conv_taps(dcp, w_ref, r0, K, rb, forward=False).astype(BF16)
            return carry

        lax.fori_loop(0, nrb, blk2, 0)

    dx, dw, db = pl.pallas_call(
        body, name=name, grid=(B, width // ct),
        in_specs=[pl.BlockSpec((1, L, ct), lambda b, j: (b, 0, off + j)),
                  pl.BlockSpec((1, 1, L, ct), lambda b, j: (j // pq, b, 0, j % pq)),
                  pl.BlockSpec((K, ct), lambda b, j: (0, woff + j)), pl.BlockSpec((1, ct), lambda b, j: (0, woff + j))],
        out_specs=[pl.BlockSpec((1, L, ct), lambda b, j: (b, 0, j)), pl.BlockSpec((1, 8, ct), lambda b, j: (b, 0, j)),
                   pl.BlockSpec((1, 1, ct), lambda b, j: (b, 0, j))],
        out_shape=[jax.ShapeDtypeStruct((B, L, width), BF16), jax.ShapeDtypeStruct((B, 8, width), F32),
                   jax.ShapeDtypeStruct((B, 1, width), F32)],
        scratch_shapes=[pltpu.VMEM((L + PAD_ROWS, ct), F32), pltpu.VMEM((L + PAD_ROWS, ct), F32)],
        compiler_params=_cparams("parallel", "parallel"))(zx, dya, cw, cb.reshape(1, -1))
    return dx, dw[:, :K, :], db


def _ffn_mid_fwd(up, cw, cb, dff):
    B, L, _ = up.shape
    K = cw.shape[0]
    ct = _pick_tile(dff, 256)
    rb = min(CONV_ROWS, L)
    nrb = L // rb
    half = dff // ct

    def body(g_ref, u_ref, wg_ref, wu_ref, bg_ref, bu_ref, a_ref, gp, upad):
        gp[0:PAD_ROWS, :] = jnp.zeros((PAD_ROWS, ct), F32)
        upad[0:PAD_ROWS, :] = jnp.zeros((PAD_ROWS, ct), F32)
        gp[PAD_ROWS:, :] = g_ref[0]
        upad[PAD_ROWS:, :] = u_ref[0]
        bg, bu = bg_ref[...], bu_ref[...]

        def blk(i, carry):
            r0 = pl.multiple_of(i * rb, rb)
            cg = _conv_taps(gp, wg_ref, r0, K, rb) + bg
            cu = _conv_taps(upad, wu_ref, r0, K, rb) + bu
            a_ref[0, pl.ds(r0, rb), :] = (_silu(cg) * cu).astype(BF16)
            return carry

        lax.fori_loop(0, nrb, blk, 0)

    xg = pl.BlockSpec((1, L, ct), lambda b, j: (b, 0, j))
    xu = pl.BlockSpec((1, L, ct), lambda b, j: (b, 0, half + j))
    wgs = pl.BlockSpec((K, ct), lambda b, j: (0, j))
    wus = pl.BlockSpec((K, ct), lambda b, j: (0, half + j))
    bgs = pl.BlockSpec((1, ct), lambda b, j: (0, j))
    bus = pl.BlockSpec((1, ct), lambda b, j: (0, half + j))
    cb2 = cb.reshape(1, 2 * dff)
    return pl.pallas_call(
        body, name="ffn_mid_fwd", grid=(B, half), in_specs=[xg, xu, wgs, wus, bgs, bus], out_specs=xg,
        out_shape=jax.ShapeDtypeStruct((B, L, dff), BF16),
        scratch_shapes=[pltpu.VMEM((L + PAD_ROWS, ct), F32), pltpu.VMEM((L + PAD_ROWS, ct), F32)],
        compiler_params=_cparams("parallel", "parallel"))(up, up, cw, cw, cb2, cb2)


def _ffn_mid_bwd(up, dact, cw, cb, dff):
    B, L, _ = up.shape
    K = cw.shape[0]
    ct = _pick_tile(dff, 256)
    rb = min(CONV_ROWS, L)
    nrb = L // rb
    half = dff // ct

    def body(g_ref, u_ref, da_ref, wg_ref, wu_ref, bg_ref, bu_ref, dx_ref, dwg_ref, dwu_ref, dbg_ref, dbu_ref,
             gp, upad, dgp, dup):
        gp[0:PAD_ROWS, :] = jnp.zeros((PAD_ROWS, ct), F32)
        upad[0:PAD_ROWS, :] = jnp.zeros((PAD_ROWS, ct), F32)
        gp[PAD_ROWS:, :] = g_ref[0]
        upad[PAD_ROWS:, :] = u_ref[0]
        dgp[L:, :] = jnp.zeros((PAD_ROWS, ct), F32)
        dup[L:, :] = jnp.zeros((PAD_ROWS, ct), F32)
        bg, bu = bg_ref[...], bu_ref[...]

        def blk1(i, carry):
            dwg, dwu, dbg, dbu = carry
            r0 = pl.multiple_of(i * rb, rb)
            cg, sg_ = _conv_taps(gp, wg_ref, r0, K, rb, keep=True)
            cu, su_ = _conv_taps(upad, wu_ref, r0, K, rb, keep=True)
            cg = cg + bg
            cu = cu + bu
            da = da_ref[0, pl.ds(r0, rb), :]
            sig = jax.nn.sigmoid(cg)
            dcg = da * cu * (sig * (1.0 + cg * (1.0 - sig)))
            dcu = da * (cg * sig)
            dgp[pl.ds(r0, rb), :] = dcg
            dup[pl.ds(r0, rb), :] = dcu
            return (dwg + _conv_dw(sg_, dcg, K), dwu + _conv_dw(su_, dcu, K), dbg + jnp.sum(dcg, axis=0, keepdims=True),
                    dbu + jnp.sum(dcu, axis=0, keepdims=True))

        z8 = jnp.zeros((8, ct), F32)
        z1 = jnp.zeros((1, ct), F32)
        dwg, dwu, dbg, dbu = lax.fori_loop(0, nrb, blk1, (z8, z8, z1, z1))
        dwg_ref[0] = dwg
        dwu_ref[0] = dwu
        dbg_ref[0] = dbg
        dbu_ref[0] = dbu

        def blk2(i, carry):
            r0 = pl.multiple_of(i * rb, rb)
            dx_ref[0, 0, pl.ds(r0, rb), :] = _conv_taps(dgp, wg_ref, r0, K, rb, forward=False).astype(BF16)
            dx_ref[1, 0, pl.ds(r0, rb), :] = _conv_taps(dup, wu_ref, r0, K, rb, forward=False).astype(BF16)
            return carry

        lax.fori_loop(0, nrb, blk2, 0)

    xg = pl.BlockSpec((1, L, ct), lambda b, j: (b, 0, j))
    xu = pl.BlockSpec((1, L, ct), lambda b, j: (b, 0, half + j))
    wgs = pl.BlockSpec((K, ct), lambda b, j: (0, j))
    wus = pl.BlockSpec((K, ct), lambda b, j: (0, half + j))
    bgs = pl.BlockSpec((1, ct), lambda b, j: (0, j))
    bus = pl.BlockSpec((1, ct), lambda b, j: (0, half + j))
    w8 = pl.BlockSpec((1, 8, ct), lambda b, j: (b, 0, j))
    b1 = pl.BlockSpec((1, 1, ct), lambda b, j: (b, 0, j))
    cb2 = cb.reshape(1, 2 * dff)
    pad = pltpu.VMEM((L + PAD_ROWS, ct), F32)
    dx2, dwg, dwu, dbg, dbu = pl.pallas_call(
        body, name="ffn_mid_bwd", grid=(B, half), in_specs=[xg, xu, xg, wgs, wus, bgs, bus],
        out_specs=[pl.BlockSpec((2, 1, L, ct), lambda b, j: (0, b, 0, j)), w8, w8, b1, b1],
        out_shape=[jax.ShapeDtypeStruct((2, B, L, dff), BF16)] + [jax.ShapeDtypeStruct((B, 8, dff), F32)] * 2
        + [jax.ShapeDtypeStruct((B, 1, dff), F32)] * 2,
        scratch_shapes=[pad, pad, pad, pad],
        compiler_params=_cparams("parallel", "parallel"))(up, up, dact, cw, cw, cb2, cb2)
    dw = jnp.concatenate([dwg[:, :K], dwu[:, :K]], axis=-1)
    db = jnp.concatenate([dbg, dbu], axis=-1)
    return dx2, dw, db


def _ssd_consts(hpg, W):
    P = M_HEADDIM
    E = (_iota((LANES, W), 0) == _iota((LANES, W), 1) // P).astype(BF16)
    Ebig = (_iota((LANES, hpg * LANES), 0) == _iota((LANES, hpg * LANES), 1) // LANES).astype(BF16)
    causal = _iota((M_CHUNK, M_CHUNK), 0) >= _iota((M_CHUNK, M_CHUNK), 1)
    head_of_lane = _iota((1, W), 1) // P
    return E, Ebig, causal, head_of_lane


def _ssd_chunk_fwd(xs, Bm, Cm, dtr, bias, Aneg, E, Ebig, causal, head_of_lane, hpg, st, ar_sc, ae_sc):
    pre = dtr + bias
    dt = jnp.maximum(pre, 0.0) + jnp.log(1.0 + jnp.exp(-jnp.abs(pre)))
    Ad = dt * Aneg
    a_c = _cumsum_rows(Ad)
    ar_sc[...] = a_c.T
    aexp = _dot_exact(a_c, E)
    ae_sc[...] = aexp
    alast = ae_sc[M_CHUNK - 1:M_CHUNK, :]
    dtexp = _dot_exact(dt, E)
    X = xs * dtexp
    AC = _dot_exact(a_c, Ebig)
    CB = _dot(Cm, Bm, NT)
    Xb = X.astype(BF16)
    Ls = [jnp.where(causal, jnp.exp(jnp.minimum(AC[:, j * LANES:(j + 1) * LANES] - ar_sc[j:j + 1, :], 0.0)), 0.0)
          for j in range(hpg)]
    first = _iota((1, LANES), 1) < M_HEADDIM
    pairs = []
    for p in range(hpg // 2):
        Xp = Xb[:, p * LANES:(p + 1) * LANES]
        pairs.append(jnp.where(first, _dot(CB * Ls[2 * p], Xp), _dot(CB * Ls[2 * p + 1], Xp)))
    ydiag = pairs[0] if len(pairs) == 1 else jnp.concatenate(pairs, axis=1)
    ea = jnp.exp(aexp)
    yoff = ea * _dot(Cm, st)
    dec = jnp.exp(alast - aexp)
    return dict(dt=dt, a_c=a_c, aexp=aexp, alast=alast, dtexp=dtexp, X=X, Xb=Xb, CB=CB, Ls=Ls, ydiag=ydiag, ea=ea,
                yoff=yoff, dec=dec)


def _ssd_fwd(xbca, zx, dtc, bias, Aneg, Dexp, nw, hpg):
    B, L, _ = xbca.shape
    G, N, C = M_GROUPS, M_D_STATE, M_CHUNK
    W = hpg * M_HEADDIM
    DI = G * W
    NC = L // C
    LB = min(L, 4 * C)
    ncb = LB // C

    def body(xs_ref, b_ref, c_ref, z_ref, dt_ref, bias_ref, a_ref, d_ref, nw_ref, y_ref, yn_ref, st_ref, ST, ar_sc, ae_sc):
        @pl.when(pl.program_id(2) == 0)
        def _():
            ST[...] = jnp.zeros_like(ST)

        E, Ebig, causal, head_of_lane = _ssd_consts(hpg, W)
        bias_ = bias_ref[0]
        Aneg_ = a_ref[0]
        Dv = d_ref[...]
        nwv = nw_ref[...]

        def chunk(ci, carry):
            r0 = pl.multiple_of(ci * C, C)
            rows = pl.ds(r0, C)
            xs = xs_ref[0, rows, :]
            Bm = b_ref[0, rows, :]
            Cm = c_ref[0, rows, :]
            st = ST[...]
            st_ref[0, 0, ci] = st
            f = _ssd_chunk_fwd(xs, Bm, Cm, dt_ref[0, 0, ci], bias_, Aneg_, E, Ebig, causal, head_of_lane, hpg, st, ar_sc, ae_sc)
            y = f["ydiag"] + f["yoff"] + xs * Dv
            ST[...] = st * jnp.exp(f["alast"]) + _dot(Bm, f["X"] * f["dec"], TN)
            yg = y * _silu(z_ref[0, rows, :])
            rstd = lax.rsqrt(jnp.mean(yg * yg, axis=-1, keepdims=True) + NORM_EPS)
            y_ref[0, rows, :] = y
            yn_ref[0, rows, :] = (yg * rstd * nwv).astype(BF16)
            return carry

        lax.fori_loop(0, ncb, chunk, 0)

    xw = pl.BlockSpec((1, LB, W), lambda b, g, s: (b, s, g))
    bsp = pl.BlockSpec((1, LB, N), lambda b, g, s: (b, s, DI // N + g))
    csp = pl.BlockSpec((1, LB, N), lambda b, g, s: (b, s, DI // N + G + g))
    dts = pl.BlockSpec((1, 1, ncb, C, LANES), lambda b, g, s: (b, g, s, 0, 0))
    hv = pl.BlockSpec((1, 1, LANES), lambda b, g, s: (g, 0, 0))
    wv = pl.BlockSpec((1, W), lambda b, g, s: (0, g))
    sts = pl.BlockSpec((1, 1, ncb, N, W), lambda b, g, s: (b, g, s, 0, 0))
    return pl.pallas_call(
        body, name="ssd_fwd", grid=(B, G, L // LB), in_specs=[xw, bsp, csp, xw, dts, hv, hv, wv, wv],
        out_specs=[xw, xw, sts],
        out_shape=[jax.ShapeDtypeStruct((B, L, DI), F32), jax.ShapeDtypeStruct((B, L, DI), BF16),
                   jax.ShapeDtypeStruct((B, G, NC, N, W), F32)],
        scratch_shapes=[pltpu.VMEM((N, W), F32), pltpu.VMEM((LANES, C), F32), pltpu.VMEM((C, W), F32)],
        compiler_params=_cparams("parallel", "parallel", "arbitrary"))(xbca, xbca, xbca, zx, dtc, bias, Aneg, Dexp, nw)


def _ssd_bwd(xbca, zx, dtc, ypre, dyn, st, bias, Aneg, Dexp, nw, hpg):
    B, L, _ = xbca.shape
    G, N, C = M_GROUPS, M_D_STATE, M_CHUNK
    W = hpg * M_HEADDIM
    DI = G * W
    NC = L // C
    LB = min(L, 4 * C)
    ncb = LB // C
    nsb = L // LB

    def body(xs_ref, b_ref, c_ref, z_ref, dt_ref, y_ref, dyn_ref, st_ref, bias_ref, a_ref, d_ref, nw_ref,
             dxs_ref, dbc_ref, dz_ref, ddt_ref, dnw_ref, dd_ref, da_ref, dbias_ref, DST, ar_sc, ae_sc):
        @pl.when(pl.program_id(2) == 0)
        def _():
            DST[...] = jnp.zeros_like(DST)
            dnw_ref[...] = jnp.zeros_like(dnw_ref)
            dd_ref[...] = jnp.zeros_like(dd_ref)
            da_ref[...] = jnp.zeros_like(da_ref)
            dbias_ref[...] = jnp.zeros_like(dbias_ref)

        E, Ebig, causal, head_of_lane = _ssd_consts(hpg, W)
        bias_ = bias_ref[0]
        Aneg_ = a_ref[0]
        Dv = d_ref[...]
        nwv = nw_ref[...]
        lane = _iota((1, LANES), 1)
        subl = _iota((LANES, 1), 0)
        lastrow = _iota((C, W), 0) == C - 1

        def chunk(i, carry):
            ci = ncb - 1 - i
            r0 = pl.multiple_of(ci * C, C)
            rows = pl.ds(r0, C)
            xs = xs_ref[0, rows, :]
            Bm = b_ref[0, rows, :]
            Cm = c_ref[0, rows, :]
            zr = z_ref[0, rows, :]
            dtr = dt_ref[0, 0, ci]
            st_in = st_ref[0, 0, ci]
            dst = DST[...]
            f = _ssd_chunk_fwd(xs, Bm, Cm, dtr, bias_, Aneg_, E, Ebig, causal, head_of_lane, hpg, st_in, ar_sc, ae_sc)
            X, Xb, dec, ea, CB = f["X"], f["Xb"], f["dec"], f["ea"], f["CB"]
            y = y_ref[0, rows, :]
            sz = _silu(zr)
            yg = y * sz
            rstd = lax.rsqrt(jnp.mean(yg * yg, axis=-1, keepdims=True) + NORM_EPS)
            yh = yg * rstd
            dyn_ = dyn_ref[0, rows, :]
            dnw_ref[0, 0] += jnp.sum(dyn_ * yh, axis=0, keepdims=True)
            dyh = dyn_ * nwv
            dyg = rstd * (dyh - yh * jnp.mean(dyh * yh, axis=-1, keepdims=True))
            dz_ref[0, rows, :] = (dyg * y * _dsilu(zr)).astype(BF16)
            dy = dyg * sz
            dd_ref[0, 0] += jnp.sum(dy * xs, axis=0, keepdims=True)
            dxs = dy * Dv
            dYo = dy * ea
            daexp = dy * f["yoff"]
            dCm = _dot(dYo, st_in, NT)
            dst_in = _dot(Cm, dYo, TN)
            dyb = dy.astype(BF16)
            dCB = jnp.zeros((C, C), F32)
            da_col = jnp.zeros((C, LANES), F32)
            da_row = jnp.zeros((LANES, C), F32)
            first = lane < M_HEADDIM
            dXs = []
            for p in range(hpg // 2):
                Xp = Xb[:, p * LANES:(p + 1) * LANES]
                dYp = dyb[:, p * LANES:(p + 1) * LANES]
                dXp = None
                for j in (2 * p, 2 * p + 1):
                    Lj = f["Ls"][j]
                    Gj = CB * Lj
                    dYj = jnp.where(first if j % 2 == 0 else jnp.logical_not(first), dYp, jnp.zeros_like(dYp))
                    t = _dot(Gj, dYj, TN)
                    dXp = t if dXp is None else dXp + t
                    dGj = _dot(dYj, Xp, NT)
                    dCB = dCB + dGj * Lj
                    Wj = dGj * Gj
                    da_col = da_col + jnp.sum(Wj, axis=1, keepdims=True) * (lane == j).astype(F32)
                    da_row = da_row + (subl == j).astype(F32) * jnp.sum(Wj, axis=0, keepdims=True)
                dXs.append(dXp)
            dX = dXs[0] if len(dXs) == 1 else jnp.concatenate(dXs, axis=1)
            dCm = dCm + _dot(dCB, Bm)
            dBm = _dot(dCB, Cm, TN)
            ela = jnp.exp(f["alast"])
            dalast = jnp.sum(dst * st_in, axis=0, keepdims=True) * ela
            DST[...] = dst * ela + dst_in
            dXd = _dot(Bm, dst)
            dBm = dBm + _dot(X * dec, dst, NT)
            dX = dX + dXd * dec
            ddec = dXd * X * dec
            dalast = dalast + jnp.sum(ddec, axis=0, keepdims=True)
            daexp = daexp - ddec + jnp.where(lastrow, dalast, 0.0)
            dxs = dxs + dX * f["dtexp"]
            ddtexp = dX * xs
            ddt = _dot_exact(ddtexp, E, NT, passes=2)
            da_c = _dot_exact(daexp, E, NT, passes=2) + da_col - da_row.T
            dAd = _cumsum_rows(da_c, reverse=True)
            ddt = ddt + dAd * Aneg_
            da_ref[0, 0] += jnp.sum(dAd * f["dt"], axis=0, keepdims=True) * Aneg_
            ddtr = ddt * jax.nn.sigmoid(dtr + bias_)
            dbias_ref[0, 0] += jnp.sum(ddtr, axis=0, keepdims=True)
            ddt_ref[0, 0, ci] = ddtr
            dxs_ref[0, rows, :] = dxs
            dbc_ref[0, 0, rows, :] = dBm
            dbc_ref[1, 0, rows, :] = dCm
            return carry

        lax.fori_loop(0, ncb, chunk, 0)

    def rev(s):
        return nsb - 1 - s

    xw = pl.BlockSpec((1, LB, W), lambda b, g, s: (b, rev(s), g))
    bsp = pl.BlockSpec((1, LB, N), lambda b, g, s: (b, rev(s), DI // N + g))
    csp = pl.BlockSpec((1, LB, N), lambda b, g, s: (b, rev(s), DI // N + G + g))
    gsp = pl.BlockSpec((1, LB, N), lambda b, g, s: (b, rev(s), g))
    dts = pl.BlockSpec((1, 1, ncb, C, LANES), lambda b, g, s: (b, g, rev(s), 0, 0))
    hv = pl.BlockSpec((1, 1, LANES), lambda b, g, s: (g, 0, 0))
    wv = pl.BlockSpec((1, W), lambda b, g, s: (0, g))
    sts = pl.BlockSpec((1, 1, ncb, N, W), lambda b, g, s: (b, g, rev(s), 0, 0))
    accw = pl.BlockSpec((1, 1, 1, W), lambda b, g, s: (b, g, 0, 0))
    acch = pl.BlockSpec((1, 1, 1, LANES), lambda b, g, s: (b, g, 0, 0))
    return pl.pallas_call(
        body, name="ssd_bwd", grid=(B, G, nsb), in_specs=[xw, bsp, csp, xw, dts, xw, xw, sts, hv, hv, wv, wv],
        out_specs=[xw, pl.BlockSpec((2, 1, LB, N), lambda b, g, s: (0, b, rev(s), g)), xw, dts, accw, accw, acch, acch],
        out_shape=[jax.ShapeDtypeStruct((B, L, DI), F32), jax.ShapeDtypeStruct((2, B, L, G * N), F32),
                   jax.ShapeDtypeStruct((B, L, DI), BF16),
                   jax.ShapeDtypeStruct((B, G, NC, C, LANES), F32), jax.ShapeDtypeStruct((B, G, 1, W), F32),
                   jax.ShapeDtypeStruct((B, G, 1, W), F32), jax.ShapeDtypeStruct((B, G, 1, LANES), F32),
                   jax.ShapeDtypeStruct((B, G, 1, LANES), F32)],
        scratch_shapes=[pltpu.VMEM((N, W), F32), pltpu.VMEM((LANES, C), F32), pltpu.VMEM((C, W), F32)],
        compiler_params=_cparams("parallel", "parallel", "arbitrary"))(
            xbca, xbca, xbca, zx, dtc, ypre, dyn, st, bias, Aneg, Dexp, nw)


def _adamw(w, g, m, v, name, echo=False):
    shape = w.shape
    n = w.size
    cols = shape[-1]
    rows = n // cols
    tr = rows
    for cand in (512, 256, 128, 64, 32, 16, 8):
        if rows % cand == 0 and cand * cols * 4 <= 1024 * 1024:
            tr = cand
            break
    c1 = 1.0 / (1.0 - ADAM_B1 ** ADAM_STEP)
    c2 = 1.0 / (1.0 - ADAM_B2 ** ADAM_STEP)

    def body(w_ref, g_ref, m_ref, v_ref, d_ref, mo_ref, vo_ref, *go_ref):
        g_ = g_ref[...]
        mn = ADAM_B1 * m_ref[...] + (1.0 - ADAM_B1) * g_
        vn = ADAM_B2 * v_ref[...] + (1.0 - ADAM_B2) * (g_ * g_)
        d_ref[...] = -ADAM_LR * ((mn * c1) / (jnp.sqrt(vn * c2) + ADAM_EPS) + ADAM_WD * w_ref[...])
        mo_ref[...] = mn
        vo_ref[...] = vn
        if echo:
            go_ref[0][...] = g_

    spec = pl.BlockSpec((tr, cols), lambda i: (i, 0))
    r2 = lambda a: a.reshape(rows, cols)
    nout = 4 if echo else 3
    outs = pl.pallas_call(
        body, name=name, grid=(rows // tr,), in_specs=[spec] * 4, out_specs=[spec] * nout,
        out_shape=[jax.ShapeDtypeStruct((rows, cols), F32)] * nout,
        compiler_params=_cparams("parallel"))(r2(w), r2(g), r2(m), r2(v))
    return tuple(o.reshape(shape) for o in outs)


def _lower_bounds(lb_logits):
    p = jax.nn.softmax(lb_logits.astype(F32), axis=0)
    return jnp.cumsum(p, axis=0) - p[0]


def _pad_cols(a, n):
    return a if a.shape[-1] == n else jnp.pad(a, [(0, 0)] * (a.ndim - 1) + [(0, n - a.shape[-1])])


def _heads_to_lanes(a, G, hpg):
    return _pad_cols(a.reshape(G, 1, hpg), LANES)


def _local_step(x, target, P, fetch, emit):
    B, L, D = x.shape
    T = B * L
    depth = P["mix_norm"].shape[0]
    H = D // HGRN_DK
    F_ = H * HGRN_DK
    DI = P["m_norm"].shape[1]
    G, N = M_GROUPS, M_D_STATE
    MH = DI // M_HEADDIM
    hpg = MH // G
    assert hpg <= 8
    W = hpg * M_HEADDIM
    CD = DI + 2 * G * N
    MIN = DI + CD + MH
    MPAD = -(-MIN // LANES) * LANES
    dff = P["f_conv_b"].shape[1] // 2
    NC = L // M_CHUNK
    lbs = _lower_bounds(P["hgrn_lb_logits"])

    h = x.reshape(T, D)
    saved = []
    for i in range(depth):
        j = i // 2
        Wl = dict(fetch(i, ("mix_in",), h))
        s = {"h_in": h, "W": Wl}
        u = _rmsnorm_fwd(h, P["mix_norm"][i], "mix_norm_fwd")
        s["u"] = u
        if i % 2 == 0:
            proj = _matmul(u, Wl["mix_in"], name="hgrn_in_fwd").reshape(B, L, 4 * F_)
            o, on, st = _hgrn_fwd(proj, lbs[j].reshape(1, F_), P["hgrn_gnorm"][j].reshape(1, HGRN_DK), H)
            Wl.update(fetch(i, ("mix_out",), on))
            h = _matmul(on.reshape(T, F_), Wl["mix_out"], res=h, name="hgrn_out_fwd")
            s.update(proj=proj, o=o, on=on, st=st)
        else:
            zx = _matmul(u, Wl["mix_in"], tb=True, tn=1152, name="m_in_fwd").reshape(B, L, MPAD)
            xbca = _mconv_fwd(zx, P["m_conv_w"][j], P["m_conv_b"][j], DI, CD)
            dtr = zx[:, :, DI + CD:DI + CD + MH].reshape(B, NC, M_CHUNK, G, hpg).transpose(0, 3, 1, 2, 4)
            dtc = _pad_cols(dtr, LANES)
            bias = _heads_to_lanes(P["m_dt_bias"][j], G, hpg)
            Aneg = _heads_to_lanes(-jnp.exp(P["m_A_log"][j]), G, hpg)
            Dexp = jnp.repeat(P["m_D"][j], M_HEADDIM).reshape(1, DI)
            nw = P["m_norm"][j].reshape(1, DI)
            ypre, yn, st = _ssd_fwd(xbca, zx, dtc, bias, Aneg, Dexp, nw, hpg)
            Wl.update(fetch(i, ("mix_out",), yn))
            h = _matmul(yn.reshape(T, DI), Wl["mix_out"], res=h, name="m_out_fwd")
            s.update(zx=zx, xbca=xbca, dtc=dtc, bias=bias, Aneg=Aneg, Dexp=Dexp, nw=nw, ypre=ypre, yn=yn, st=st)
        s["h_mid"] = h
        u2 = _rmsnorm_fwd(h, P["ffn_norm"][i], "ffn_norm_fwd")
        Wl.update(fetch(i, ("f_w_up", "f_w_down"), h))
        up = _matmul(u2, Wl["f_w_up"], name="ffn_up_fwd").reshape(B, L, 2 * dff)
        act = _ffn_mid_fwd(up, P["f_conv_w"][i], P["f_conv_b"][i], dff)
        h = _matmul(act.reshape(T, dff), Wl["f_w_down"], res=h, name="ffn_down_fwd")
        s.update(u2=u2, up=up, act=act)
        saved.append(s)

    loss, dh, dhb, d_final = _loss_head(h, P["final_norm"], target.reshape(T, D))

    g = {k: [None] * P[k].shape[0] for k in ("mix_norm", "ffn_norm", "hgrn_gnorm", "m_conv_w", "m_conv_b", "m_dt_bias",
                                              "m_A_log", "m_D", "m_norm", "f_conv_w", "f_conv_b")}
    dlbs = [None] * lbs.shape[0]
    for i in reversed(range(depth)):
        j = i // 2
        s = saved[i]
        Wl = s["W"]
        gm = {}

        def dw(key, a, b, name, **kw):
            gm[key] = _matmul(a, b, ta=True, out_dtype=BF16, tk=T, name=name, **kw)

        dact = _matmul(dhb, Wl["f_w_down"], tb=True, name="ffn_down_dx").reshape(B, L, dff)
        dw("f_w_down", s["act"].reshape(T, dff), dhb, "ffn_down_dw")
        dup, dcw, dcb = _ffn_mid_bwd(s["up"], dact, P["f_conv_w"][i], P["f_conv_b"][i], dff)
        g["f_conv_w"][i] = jnp.sum(dcw, axis=0)
        g["f_conv_b"][i] = jnp.sum(dcb, axis=(0, 1))
        dup = dup.reshape(2, T, dff)
        dw("f_w_up", s["u2"], dup, "ffn_up_dw", b_parts=True)
        tok, finish = emit(i, {key: gm[key] for key in ("f_w_up", "f_w_down")})
        du2 = _matmul(dup, Wl["f_w_up"], a_parts=True, tb=True, name="ffn_up_dx", dep=tok)
        dh, dhb, g["ffn_norm"][i] = _rmsnorm_bwd(s["h_mid"], P["ffn_norm"][i], du2, dh, "ffn_norm_bwd", dep=finish(du2))
        if i % 2 == 0:
            don = _matmul(dhb, Wl["mix_out"], tb=True, name="hgrn_out_dx").reshape(B, L, F_)
            dw("mix_out", s["on"].reshape(T, F_), dhb, "hgrn_out_dw")
            dproj, dlb, dgn = _hgrn_bwd(s["proj"], s["o"], don, s["st"], lbs[j].reshape(1, F_),
                                        P["hgrn_gnorm"][j].reshape(1, HGRN_DK), H)
            dlbs[j] = jnp.sum(dlb, axis=(0, 1))
            g["hgrn_gnorm"][j] = jnp.sum(dgn, axis=(0, 1, 2))
            dproj = dproj.reshape(4, T, F_)
            dw("mix_in", s["u"], dproj, "hgrn_in_dw", b_parts=True)
            tok, finish = emit(i, {key: gm[key] for key in ("mix_in", "mix_out")})
            du = _matmul(dproj, Wl["mix_in"], a_parts=True, tb=True, name="hgrn_in_dx", dep=tok)
        else:
            dyn = _matmul(dhb, Wl["mix_out"], tb=True, name="m_out_dx").reshape(B, L, DI)
            dw("mix_out", s["yn"].reshape(T, DI), dhb, "m_out_dw")
            dxs, dbc, dz, ddt, dnw, dD, dA, dbias = _ssd_bwd(s["xbca"], s["zx"], s["dtc"], s["ypre"], dyn, s["st"],
                                                             s["bias"], s["Aneg"], s["Dexp"], s["nw"], hpg)
            g["m_norm"][j] = jnp.sum(dnw, axis=(0, 2)).reshape(DI)
            g["m_D"][j] = jnp.sum(dD, axis=(0, 2)).reshape(MH, M_HEADDIM).sum(axis=-1)
            g["m_A_log"][j] = jnp.sum(dA, axis=(0, 2))[:, :hpg].reshape(MH)
            g["m_dt_bias"][j] = jnp.sum(dbias, axis=(0, 2))[:, :hpg].reshape(MH)
            cw, cb = P["m_conv_w"][j], P["m_conv_b"][j]
            dxx, dcw_x, dcb_x = _mconv_bwd(s["zx"], dxs[None], cw, cb, DI, 0, "mconv_bwd_x")
            dxb, dcw_b, dcb_b = _mconv_bwd(s["zx"], dbc, cw, cb, DI, DI, "mconv_bwd_bc")
            g["m_conv_w"][j] = jnp.concatenate([jnp.sum(dcw_x, axis=0), jnp.sum(dcw_b, axis=0)], axis=-1)
            g["m_conv_b"][j] = jnp.concatenate([jnp.sum(dcb_x, axis=(0, 1)), jnp.sum(dcb_b, axis=(0, 1))], axis=-1)
            ddt_t = _pad_cols(ddt[..., :hpg].transpose(0, 2, 3, 1, 4).reshape(T, MH), MPAD - DI - CD).astype(BF16)
            pieces = [(dz.reshape(T, DI), 0), (dxx.reshape(T, DI), DI), (dxb.reshape(T, 2 * G * N), 2 * DI), (ddt_t, DI + CD)]
            gm["mix_in"] = lax.empty((MPAD, D), BF16)
            for n_, (piece, off) in enumerate(pieces):
                gm["mix_in"] = _matmul(piece, s["u"], ta=True, out_dtype=BF16, tk=T, out=gm["mix_in"], out_off=off,
                                       name="m_in_dw%d" % n_)
            tok, finish = emit(i, {key: gm[key] for key in ("mix_in", "mix_out")})
            du = None
            for n_, (piece, off) in enumerate(pieces):
                du = _matmul(piece, Wl["mix_in"], b_off=off, res=du, name="m_in_dx%d" % n_, dep=tok if n_ == 0 else None)
        dh, dhb, g["mix_norm"][i] = _rmsnorm_bwd(s["h_in"], P["mix_norm"][i], du, dh, "mix_norm_bwd", dep=finish(du))

    grads = {k: jnp.stack(vs) for k, vs in g.items()}
    grads["final_norm"] = d_final
    _, lb_vjp = jax.vjp(_lower_bounds, P["hgrn_lb_logits"])
    grads["hgrn_lb_logits"] = lb_vjp(jnp.stack(dlbs))[0]
    return loss, dh.reshape(B, L, D), grads


ANY = pl.BlockSpec(memory_space=pl.ANY)
N_CHIPS = 4
N_DEV = 8


def _place():
    x, y, c = lax.axis_index("x"), lax.axis_index("y"), lax.axis_index("c")
    sibling = (x, y, 1 - c)
    chips = [(1 - x, y), (x, 1 - y), (1 - x, 1 - y)]
    return x, y, c, sibling, chips


def _remote(src, dst, send_sem, recv_sem, to):
    return pltpu.make_async_remote_copy(src_ref=src, dst_ref=dst, send_sem=send_sem, recv_sem=recv_sem, device_id=to,
                                        device_id_type=MESH)


KIND_AXIS = {"hgrn_w_in": "col", "f_w_up": "col", "m_w_in_t": "row", "hgrn_w_out": "row", "m_w_out": "row", "f_w_down": "row"}
KINDS = tuple(KIND_AXIS)
PEER_MASKS = (2, 1, 3)
ALL = slice(None)


def _chip_win(axis, cw, s):
    return (ALL, slice(s * cw, (s + 1) * cw)) if axis == "col" else (slice(s * cw, (s + 1) * cw), ALL)


def _half_win(axis, rows, cols, h):
    return (slice(h * rows // 2, (h + 1) * rows // 2), ALL) if axis == "col" else (ALL, slice(h * cols // 2, (h + 1) * cols // 2))


def _per_place(fn):
    x, y, c, sibling, chips = _place()
    chip = 2 * x + y
    for s in range(N_CHIPS):
        for cc in range(2):
            @pl.when(jnp.logical_and(chip == s, c == cc))
            def _():
                fn(s, cc, c, sibling, chips)


HBM = pl.BlockSpec(memory_space=pltpu.HBM)
SEM = pl.BlockSpec(memory_space=pltpu.SEMAPHORE)
EFFECT = pltpu.SideEffectType.DATAFLOW_SIDE_EFFECTING


def _cell(axis, rows, cols, cw, s, h):
    if axis == "col":
        return (slice(h * rows // 2, (h + 1) * rows // 2), slice(s * cw, (s + 1) * cw))
    return (slice(s * cw, (s + 1) * cw), slice(h * cols // 2, (h + 1) * cols // 2))


def _in_hbm(a):
    return pltpu.with_memory_space_constraint(a, pltpu.HBM)


def _stage_shard(kind, shard, layer, chip, pad_rows=0, dep=None):
    _, R, C = shard.shape
    axis = KIND_AXIS[kind]
    tr, tc = _row_tile(R), _pick_tile(C, 2048)
    nr, nc = R // tr, C // tc
    full = (R, N_CHIPS * C) if axis == "col" else (N_CHIPS * R + pad_rows, C)

    def body(s_ref, x_ref, *rest):
        o_ref = rest[-1]
        o_ref[...] = x_ref[...].astype(BF16)

    if axis == "col":
        dst = pl.BlockSpec((tr, tc), lambda i, j, s_ref: (i, s_ref[0] * nc + j))
    else:
        dst = pl.BlockSpec((tr, tc), lambda i, j, s_ref: (s_ref[0] * nr + i, j))
    extra_specs, extra = ([], ()) if dep is None else ([ANY], (dep,))
    grid_spec = pltpu.PrefetchScalarGridSpec(
        num_scalar_prefetch=1, grid=(nr, nc),
        in_specs=[pl.BlockSpec((None, tr, tc), lambda i, j, s_ref: (layer, i, j))] + extra_specs, out_specs=dst)
    out = pl.pallas_call(
        body, name="stage_" + kind, grid_spec=grid_spec, out_shape=jax.ShapeDtypeStruct(full, BF16),
        compiler_params=_cparams("parallel", "parallel"))(chip.reshape(1).astype(jnp.int32), shard, *extra)
    if pad_rows:
        rows0 = N_CHIPS * R
        pr = math.gcd(rows0, pad_rows)

        def zero_body(x_ref, o_ref):
            o_ref[...] = jnp.zeros_like(o_ref)

        out = pl.pallas_call(
            zero_body, name="zero_pad_" + kind, grid=(pad_rows // pr,), in_specs=[ANY],
            out_specs=pl.BlockSpec((pr, C), lambda i: (rows0 // pr + i, 0)), out_shape=jax.ShapeDtypeStruct(full, BF16),
            input_output_aliases={0: 0}, compiler_params=_cparams("parallel"))(out)
    return out


def _gather_start(items, mats, cws, after, name):
    n = len(items)

    def body(*refs):
        send_sems, recv_sems, token = refs[n + 1], refs[n + 2], refs[-1]
        m = refs[n + 3:2 * n + 3]

        def run(s, cc, c, sibling, chips):
            for q, (k, _) in enumerate(items):
                r, c_ = m[q].shape
                mine = m[q].at[_cell(KIND_AXIS[k], r, c_, cws[k], s, cc)]
                for j, (px, py) in enumerate(chips):
                    _remote(mine, mine, send_sems.at[3 * q + j], recv_sems.at[3 * q + j], (px, py, c)).start()

        _per_place(run)
        token[...] = jnp.zeros_like(token)

    outs = pl.pallas_call(
        body, name=name, in_specs=[HBM] * n + [ANY],
        out_specs=[SEM, SEM] + [HBM] * n + [pl.BlockSpec(memory_space=pltpu.VMEM)],
        out_shape=[pltpu.SemaphoreType.DMA((3 * n,)), pltpu.SemaphoreType.DMA((3 * n,))]
        + [pltpu.HBM(a.shape, a.dtype) for a in mats] + [jax.ShapeDtypeStruct((8, LANES), F32)],
        input_output_aliases={q: 2 + q for q in range(n)},
        compiler_params=pltpu.CompilerParams(has_side_effects=EFFECT),
    )(*[_in_hbm(a) for a in mats], after)
    return outs[0], outs[1], list(outs[2:2 + n]), outs[-1]


def _gather_wait(items, idx, mats, send_sems, recv_sems, cws, after, name):
    n = len(idx)

    def body(*refs):
        m = refs[:n]
        s_sems, r_sems = refs[n], refs[n + 1]

        def run(s, cc, c, sibling, chips):
            for a, q in enumerate(idx):
                k = items[q][0]
                r, c_ = m[a].shape
                mine = m[a].at[_cell(KIND_AXIS[k], r, c_, cws[k], s, cc)]
                for j, (px, py) in enumerate(chips):
                    theirs = m[a].at[_cell(KIND_AXIS[k], r, c_, cws[k], s ^ PEER_MASKS[j], cc)]
                    cp = _remote(mine, theirs, s_sems.at[3 * q + j], r_sems.at[3 * q + j], (px, py, c))
                    cp.wait_send()
                    cp.wait_recv()

        _per_place(run)

    outs = pl.pallas_call(
        body, name=name, in_specs=[HBM] * n + [SEM, SEM, ANY], out_specs=[HBM] * n,
        out_shape=[pltpu.HBM(a.shape, a.dtype) for a in mats], input_output_aliases={a: a for a in range(n)},
        compiler_params=pltpu.CompilerParams(has_side_effects=EFFECT),
    )(*mats, send_sems, recv_sems, after)
    return list(outs)


def _forward_halves(kinds, mats, cws, name):
    n = len(mats)

    def body(*refs):
        m = refs[n:2 * n]
        send_sems, recv_sems = refs[2 * n:]

        def run(s, cc, c, sibling, chips):
            cps = []
            for a, k in enumerate(kinds):
                r, c_ = m[a].shape
                for j in range(3):
                    have = m[a].at[_cell(KIND_AXIS[k], r, c_, cws[k], s ^ PEER_MASKS[j], cc)]
                    cps.append(_remote(have, have, send_sems.at[3 * a + j], recv_sems.at[3 * a + j], sibling))
            for cp in cps:
                cp.start()
            for cp in cps:
                cp.wait()

        _per_place(run)

    outs = pl.pallas_call(
        body, name=name, in_specs=[ANY] * n, out_specs=[ANY] * n,
        out_shape=[jax.ShapeDtypeStruct(a.shape, a.dtype) for a in mats], input_output_aliases={a: a for a in range(n)},
        scratch_shapes=[pltpu.SemaphoreType.DMA((3 * n,)), pltpu.SemaphoreType.DMA((3 * n,))],
    )(*mats)
    return list(outs)


def _swap_start(kinds, gms, name):
    n = len(gms)
    lands = [lax.empty((g.shape[0] // 2, g.shape[1]) if KIND_AXIS[k] == "col" else (g.shape[0], g.shape[1] // 2), BF16)
             for k, g in zip(kinds, gms)]

    def body(*refs):
        send_sems, recv_sems, token = refs[2 * n], refs[2 * n + 1], refs[-1]
        g, ra = refs[2 * n + 2:3 * n + 2], refs[3 * n + 2:4 * n + 2]

        def run(s, cc, c, sibling, chips):
            for a, k in enumerate(kinds):
                r, c_ = g[a].shape
                _remote(g[a].at[_half_win(KIND_AXIS[k], r, c_, 1 - cc)], ra[a], send_sems.at[a], recv_sems.at[a],
                        sibling).start()

        _per_place(run)
        token[...] = jnp.zeros_like(token)

    outs = pl.pallas_call(
        body, name=name, in_specs=[HBM] * (2 * n),
        out_specs=[SEM, SEM] + [HBM] * (2 * n) + [pl.BlockSpec(memory_space=pltpu.VMEM)],
        out_shape=[pltpu.SemaphoreType.DMA((n,)), pltpu.SemaphoreType.DMA((n,))]
        + [pltpu.HBM(a.shape, a.dtype) for a in gms + lands] + [jax.ShapeDtypeStruct((8, LANES), F32)],
        input_output_aliases={q: 2 + q for q in range(2 * n)},
        compiler_params=pltpu.CompilerParams(has_side_effects=EFFECT),
    )(*[_in_hbm(a) for a in gms + lands])
    return outs[0], outs[1], list(outs[2:2 + n]), list(outs[2 + n:2 + 2 * n]), outs[-1]


def _swap_wait(kinds, gms, lands, send_sems, recv_sems, after, name):
    n = len(gms)

    def body(*refs):
        g, ra = refs[:n], refs[n:2 * n]
        s_sems, r_sems = refs[2 * n], refs[2 * n + 1]

        def run(s, cc, c, sibling, chips):
            for a, k in enumerate(kinds):
                r, c_ = g[a].shape
                cp = _remote(g[a].at[_half_win(KIND_AXIS[k], r, c_, 1 - cc)], ra[a], s_sems.at[a], r_sems.at[a], sibling)
                cp.wait_send()
                cp.wait_recv()

        _per_place(run)

    outs = pl.pallas_call(
        body, name=name, in_specs=[HBM] * (2 * n) + [SEM, SEM, ANY], out_specs=[HBM] * (2 * n),
        out_shape=[pltpu.HBM(a.shape, a.dtype) for a in gms + lands], input_output_aliases={a: a for a in range(2 * n)},
        compiler_params=pltpu.CompilerParams(has_side_effects=EFFECT),
    )(*gms, *lands, send_sems, recv_sems, after)
    return list(outs[:n]), list(outs[n:])


def _win_shape(kind, pa, cw):
    return (pa.shape[0], cw) if KIND_AXIS[kind] == "col" else (cw, pa.shape[1])


def _scatter_start(kinds, pas, cws, name):
    n = len(pas)
    lands = [lax.empty((3,) + _win_shape(k, p, cws[k]), BF16) for k, p in zip(kinds, pas)]

    def body(*refs):
        send_sems, recv_sems, token = refs[2 * n], refs[2 * n + 1], refs[-1]
        p, rb = refs[2 * n + 2:3 * n + 2], refs[3 * n + 2:4 * n + 2]

        def run(s, cc, c, sibling, chips):
            for a, k in enumerate(kinds):
                for j, (px, py) in enumerate(chips):
                    src = p[a].at[_chip_win(KIND_AXIS[k], cws[k], s ^ PEER_MASKS[j])]
                    _remote(src, rb[a].at[j], send_sems.at[3 * a + j], recv_sems.at[3 * a + j], (px, py, c)).start()

        _per_place(run)
        token[...] = jnp.zeros_like(token)

    outs = pl.pallas_call(
        body, name=name, in_specs=[HBM] * (2 * n),
        out_specs=[SEM, SEM] + [HBM] * (2 * n) + [pl.BlockSpec(memory_space=pltpu.VMEM)],
        out_shape=[pltpu.SemaphoreType.DMA((3 * n,)), pltpu.SemaphoreType.DMA((3 * n,))]
        + [pltpu.HBM(a.shape, a.dtype) for a in pas + lands] + [jax.ShapeDtypeStruct((8, LANES), F32)],
        input_output_aliases={q: 2 + q for q in range(2 * n)},
        compiler_params=pltpu.CompilerParams(has_side_effects=EFFECT),
    )(*[_in_hbm(a) for a in pas + lands])
    return outs[0], outs[1], list(outs[2:2 + n]), list(outs[2 + n:2 + 2 * n]), outs[-1]


def _scatter_wait(kinds, pas, lands, send_sems, recv_sems, cws, after, name):
    n = len(pas)

    def body(*refs):
        p, rb = refs[:n], refs[n:2 * n]
        s_sems, r_sems = refs[2 * n], refs[2 * n + 1]

        def run(s, cc, c, sibling, chips):
            for a, k in enumerate(kinds):
                for j, (px, py) in enumerate(chips):
                    src = p[a].at[_chip_win(KIND_AXIS[k], cws[k], s ^ PEER_MASKS[j])]
                    cp = _remote(src, rb[a].at[j], s_sems.at[3 * a + j], r_sems.at[3 * a + j], (px, py, c))
                    cp.wait_send()
                    cp.wait_recv()

        _per_place(run)

    outs = pl.pallas_call(
        body, name=name, in_specs=[HBM] * (2 * n) + [SEM, SEM, ANY], out_specs=[HBM] * (2 * n),
        out_shape=[pltpu.HBM(a.shape, a.dtype) for a in pas + lands], input_output_aliases={a: a for a in range(2 * n)},
        compiler_params=pltpu.CompilerParams(has_side_effects=EFFECT),
    )(*pas, *lands, send_sems, recv_sems, after)
    return list(outs[:n]), list(outs[n:])


def _share_halves(g):
    nq = len(KINDS)

    def body(*refs):
        out = dict(zip(KINDS, refs[nq:2 * nq]))
        send_sems, recv_sems = refs[2 * nq:]

        def run(s, cc, c, sibling, chips):
            cps = []
            for q, k in enumerate(KINDS):
                _, r, c_ = out[k].shape
                mine = out[k].at[(ALL,) + _half_win(KIND_AXIS[k], r, c_, cc)]
                cps.append(_remote(mine, mine, send_sems.at[q], recv_sems.at[q], sibling))
            for cp in cps:
                cp.start()
            for cp in cps:
                cp.wait()

        _per_place(run)

    outs = pl.pallas_call(
        body, name="share_halves", in_specs=[ANY] * nq, out_specs=[ANY] * nq,
        out_shape=[jax.ShapeDtypeStruct(g[k].shape, F32) for k in KINDS],
        input_output_aliases={q: q for q in range(nq)},
        scratch_shapes=[pltpu.SemaphoreType.DMA((nq,)), pltpu.SemaphoreType.DMA((nq,))],
    )(*[g[k] for k in KINDS])
    return dict(zip(KINDS, outs))


def _all_gather_small(xs, name):
    m_per, n = xs.shape

    def body(x_ref, out_ref, send_sems, recv_sems, local_sem):
        x, y, c, sibling, chips = _place()
        me = (x, y, c)

        def rows(px, py, pc):
            return out_ref.at[pl.ds((4 * px + 2 * py + pc) * m_per, m_per), :]

        def copy(k, block, to, src=None):
            return _remote(rows(*block) if src is None else src, rows(*block), send_sems.at[k], recv_sems.at[k], to)

        mine = pltpu.make_async_copy(x_ref, rows(*me), local_sem)
        mine.start()
        first = [copy(0, me, sibling, src=x_ref)]
        first += [copy(1 + j, me, (*chip, c), src=x_ref) for j, chip in enumerate(chips)]
        for cp in first:
            cp.start()
        passed = [copy(4 + j, (*chip, c), sibling) for j, chip in enumerate(chips)]
        for j, chip in enumerate(chips):
            copy(1 + j, (*chip, c), me).wait_recv()
            passed[j].start()
        copy(0, sibling, me).wait_recv()
        for j, chip in enumerate(chips):
            copy(4 + j, (*chip, 1 - c), me).wait_recv()
        for cp in first + passed:
            cp.wait_send()
        mine.wait()

    vm = pl.BlockSpec(memory_space=pltpu.VMEM)
    return pl.pallas_call(
        body, name=name, in_specs=[vm], out_specs=vm, out_shape=jax.ShapeDtypeStruct((N_DEV * m_per, n), xs.dtype),
        scratch_shapes=[pltpu.SemaphoreType.DMA((7,)), pltpu.SemaphoreType.DMA((7,)), pltpu.SemaphoreType.DMA],
        compiler_params=pltpu.CompilerParams(vmem_limit_bytes=VMEM_LIMIT_BYTES),
    )(xs)


def _row_tile(rows, cap=512):
    for mult in (16, 8):
        best = None
        t = mult
        while t <= min(rows, cap):
            if rows % t == 0:
                best = t
            t += mult
        if best is not None:
            return best
    raise ValueError(rows)


def _add_sibling(kind, g, ra, core):
    R, C = ra.shape
    axis = KIND_AXIS[kind]
    tr, tc = _row_tile(R), _pick_tile(C, 2048)
    nr, nc = R // tr, C // tc

    def body(c_ref, a_ref, b_ref, o_ref):
        o_ref[...] = (a_ref[...].astype(F32) + b_ref[...].astype(F32)).astype(o_ref.dtype)

    if axis == "col":
        own = pl.BlockSpec((tr, tc), lambda i, j, c_ref: (c_ref[0] * nr + i, j))
    else:
        own = pl.BlockSpec((tr, tc), lambda i, j, c_ref: (i, c_ref[0] * nc + j))
    same = pl.BlockSpec((tr, tc), lambda i, j, c_ref: (i, j))
    grid_spec = pltpu.PrefetchScalarGridSpec(num_scalar_prefetch=1, grid=(nr, nc), in_specs=[own, same], out_specs=same)
    return pl.pallas_call(
        body, name="add_sibling_" + kind, grid_spec=grid_spec, out_shape=jax.ShapeDtypeStruct(ra.shape, BF16),
        compiler_params=_cparams("parallel", "parallel"))(core.reshape(1).astype(jnp.int32), g, ra)


def _sum_chips(kind, pa, rb, chip, core, out, layer):
    _, R, C = rb.shape
    axis = KIND_AXIS[kind]
    tr, tc = _row_tile(R), _pick_tile(C, 2048)
    nr, nc = R // tr, C // tc

    def body(s_ref, c_ref, a_ref, b0_ref, b1_ref, b2_ref, old_ref, o_ref):
        o_ref[...] = ((a_ref[...].astype(F32) + b0_ref[...].astype(F32)) + b1_ref[...].astype(F32)) + b2_ref[...].astype(F32)

    def rb_spec(n):
        return pl.BlockSpec((None, tr, tc), lambda i, j, s_ref, c_ref: (n, i, j))

    if axis == "col":
        own = pl.BlockSpec((tr, tc), lambda i, j, s_ref, c_ref: (i, s_ref[0] * nc + j))
        dst = pl.BlockSpec((None, tr, tc), lambda i, j, s_ref, c_ref: (layer, c_ref[0] * nr + i, j))
        assert out.shape[1:] == (2 * R, C)
    else:
        own = pl.BlockSpec((tr, tc), lambda i, j, s_ref, c_ref: (s_ref[0] * nr + i, j))
        dst = pl.BlockSpec((None, tr, tc), lambda i, j, s_ref, c_ref: (layer, i, c_ref[0] * nc + j))
        assert out.shape[1:] == (R, 2 * C)
    grid_spec = pltpu.PrefetchScalarGridSpec(
        num_scalar_prefetch=2, grid=(nr, nc), in_specs=[own, rb_spec(0), rb_spec(1), rb_spec(2), ANY], out_specs=dst)
    return pl.pallas_call(
        body, name="sum_chips_" + kind, grid_spec=grid_spec, out_shape=jax.ShapeDtypeStruct(out.shape, F32),
        input_output_aliases={6: 0}, compiler_params=_cparams("parallel", "parallel"))(
            chip.reshape(1).astype(jnp.int32), core.reshape(1).astype(jnp.int32), pa, rb, rb, rb, out)


def _sum_devices(gathered):
    M = gathered.shape[0] // N_DEV
    C = gathered.shape[1]

    def body(g_ref, o_ref):
        acc = g_ref[0:M, :]
        for d in range(1, N_DEV):
            acc = acc + g_ref[d * M:(d + 1) * M, :]
        o_ref[...] = acc

    vm = pl.BlockSpec(memory_space=pltpu.VMEM)
    return pl.pallas_call(body, name="sum_devices", in_specs=[vm], out_specs=vm, out_shape=jax.ShapeDtypeStruct((M, C), F32),
                          compiler_params=pltpu.CompilerParams(vmem_limit_bytes=VMEM_LIMIT_BYTES))(gathered)


WEIGHTS = ["mix_norm", "ffn_norm", "final_norm", "hgrn_w_in", "hgrn_lb_logits", "hgrn_gnorm", "hgrn_w_out", "m_w_in",
           "m_conv_w", "m_conv_b", "m_dt_bias", "m_A_log", "m_D", "m_norm", "m_w_out", "f_w_up", "f_conv_w", "f_conv_b",
           "f_w_down"]
BIG_COLS = ("hgrn_w_in", "m_w_in", "f_w_up")
BIG_ROWS = ("hgrn_w_out", "m_w_out", "f_w_down")
BIG = BIG_COLS + BIG_ROWS
SMALL_SHARDED = ("m_conv_w", "m_conv_b", "m_norm", "f_conv_w")
SMALL_REPLICATED = ("mix_norm", "ffn_norm", "final_norm", "hgrn_lb_logits", "hgrn_gnorm", "m_dt_bias", "m_A_log", "m_D",
                    "f_conv_b")
SMALL = SMALL_REPLICATED + SMALL_SHARDED


def _pack_rows(arrs, row_mult=8):
    flat = jnp.concatenate([a.reshape(-1).astype(F32) for a in arrs])
    unit = FLAT_COLS * row_mult
    n = -(-flat.size // unit) * unit
    return jnp.pad(flat, (0, n - flat.size)).reshape(-1, FLAT_COLS)


def _unpack_rows(buf, shapes):
    flat = buf.reshape(-1)
    out, off = [], 0
    for shp in shapes:
        n = math.prod(shp)
        out.append(flat[off:off + n].reshape(shp))
        off += n
    return out


def kernel(x, mix_norm, ffn_norm, final_norm, hgrn_w_in, hgrn_lb_logits, hgrn_gnorm, hgrn_w_out, m_w_in, m_conv_w, m_conv_b, m_dt_bias, m_A_log, m_D, m_norm, m_w_out, f_w_up, f_conv_w, f_conv_b, f_w_down, loss_target, m_mix_norm, m_ffn_norm, m_final_norm, m_hgrn_w_in, m_hgrn_lb_logits, m_hgrn_gnorm, m_hgrn_w_out, m_m_w_in, m_m_conv_w, m_m_conv_b, m_m_dt_bias, m_m_A_log, m_m_D, m_m_norm, m_m_w_out, m_f_w_up, m_f_conv_w, m_f_conv_b, m_f_w_down, v_mix_norm, v_ffn_norm, v_final_norm, v_hgrn_w_in, v_hgrn_lb_logits, v_hgrn_gnorm, v_hgrn_w_out, v_m_w_in, v_m_conv_w, v_m_conv_b, v_m_dt_bias, v_m_A_log, v_m_D, v_m_norm, v_m_w_out, v_f_w_up, v_f_conv_w, v_f_conv_b, v_f_w_down):
    given = dict(locals())
    w = {n: given[n] for n in WEIGHTS}
    mom1 = {n: given["m_" + n] for n in WEIGHTS}
    mom2 = {n: given["v_" + n] for n in WEIGHTS}
    chip = 2 * lax.axis_index("x") + lax.axis_index("y")
    core = lax.axis_index("c")

    shards = {k: w[k] for k in KINDS if k != "m_w_in_t"}
    shards["m_w_in_t"] = w["m_w_in"].transpose(0, 2, 1).astype(BF16)
    m_in = N_CHIPS * w["m_w_in"].shape[2]
    pad_rows = {"m_w_in_t": -(-m_in // LANES) * LANES - m_in}
    cws = {k: shards[k].shape[2] if KIND_AXIS[k] == "col" else shards[k].shape[1] for k in KINDS}
    depth = w["mix_norm"].shape[0]

    def layer_kinds(i):
        mixer = {"mix_in": ("hgrn_w_in", i // 2), "mix_out": ("hgrn_w_out", i // 2)} if i % 2 == 0 else \
                {"mix_in": ("m_w_in_t", i // 2), "mix_out": ("m_w_out", i // 2)}
        return {**mixer, "f_w_up": ("f_w_up", i), "f_w_down": ("f_w_down", i)}

    own = _pack_rows([w[n] for n in SMALL_SHARDED])
    all_small = _all_gather_small(own, "gather_small_params")
    groups, started = [list(layer_kinds(0).values()), [it for i in range(1, depth) for it in layer_kinds(i).values()]], []
    after = all_small
    for n_, items in enumerate(groups):
        staged = [_stage_shard(k, shards[k], l, chip, pad_rows.get(k, 0), dep=None if n_ == 0 else after) for k, l in items]
        send_sems, recv_sems, mats, after = _gather_start(items, staged, cws, after, "gather_start_%d" % n_)
        started.append((items, send_sems, recv_sems, mats))
    all_small = all_small.reshape(N_CHIPS, 2, -1)[:, 0]
    per_chip = [_unpack_rows(all_small[s], [w[n].shape for n in SMALL_SHARDED]) for s in range(N_CHIPS)]
    P = {}
    for i, n in enumerate(SMALL_SHARDED):
        P[n] = jnp.concatenate([per_chip[s][i] for s in range(N_CHIPS)], axis=-1)
    for n in SMALL_REPLICATED:
        P[n] = w[n]

    def fetch(i, keys, h):
        lk = {key: layer_kinds(i)[key] for key in keys}
        items, send_sems, recv_sems, mats = started[0 if i == 0 else 1]
        idx = [items.index(it) for it in lk.values()]
        tag = "%d_%s" % (i, keys[0])
        if i == 0 and keys[0] == "mix_in":
            h = after
        got = _gather_wait(items, idx, [mats[q] for q in idx], send_sems, recv_sems, cws, h, "gather_wait_" + tag)
        got = _forward_halves([k for k, _ in lk.values()], got, cws, "forward_halves_" + tag)
        return dict(zip(lk.keys(), got))

    pending = []

    def emit(i, gm):
        lk = {key: layer_kinds(i)[key] for key in gm}
        kinds = [k for k, _ in lk.values()]
        gms = list(gm.values())
        tag = "%d_%s" % (i, next(iter(gm)))
        s1, r1, gms, half_lands, tok = _swap_start(kinds, gms, "swap_start_" + tag)

        def finish(after):
            gms2, ra = _swap_wait(kinds, gms, half_lands, s1, r1, after, "swap_wait_" + tag)
            pas = [_add_sibling(k, g_, r_, core) for k, g_, r_ in zip(kinds, gms2, ra)]
            s_sems, r_sems, pas, lands, tok2 = _scatter_start(kinds, pas, cws, "scatter_start_" + tag)
            pending.append((tag, list(lk.values()), pas, lands, s_sems, r_sems))
            return tok2

        return tok, finish

    loss_part, grad_x, g_full = _local_step(x, loss_target, P, fetch, emit)

    g_sh = {k: lax.empty(shards[k].shape, F32) for k in KINDS}
    for tag, its, pas, lands, s_sems, r_sems in pending:
        kinds = [k for k, _ in its]
        pas, lands = _scatter_wait(kinds, pas, lands, s_sems, r_sems, cws, grad_x, "scatter_wait_" + tag)
        for (k, l), p_, rb_ in zip(its, pas, lands):
            g_sh[k] = _sum_chips(k, p_, rb_, chip, core, g_sh[k], l)
    g_sh = _share_halves(g_sh)
    grads = {k: g_sh[k] for k in KINDS if k != "m_w_in_t"}
    grads["m_w_in"] = g_sh["m_w_in_t"].transpose(0, 2, 1)

    small_shapes = [g_full[n].shape for n in SMALL] + [(1,)]
    packed = _pack_rows([g_full[n] for n in SMALL] + [loss_part[0, 0:1]])
    summed = _sum_devices(_all_gather_small(packed, "gather_small_grads"))
    small = _unpack_rows(summed, small_shapes)
    loss = small[-1][0]
    for n, gs in zip(SMALL, small[:-1]):
        if n in SMALL_SHARDED:
            width = w[n].shape[-1]
            gs = lax.dynamic_slice_in_dim(gs, chip * width, width, axis=gs.ndim - 1)
        grads[n] = gs

    delta, new_m, new_v = {}, {}, {}
    for n in BIG:
        if n == "m_w_in":
            delta[n], new_m[n], new_v[n] = _adamw(w[n], grads[n], mom1[n], mom2[n], "adamw_" + n)
        else:
            delta[n], new_m[n], new_v[n], grads[n] = _adamw(w[n], grads[n], mom1[n], mom2[n], "adamw_" + n, echo=True)
    shapes = [w[n].shape for n in SMALL]
    ds, ms, vs = _adamw(_pack_rows([w[n] for n in SMALL]), _pack_rows([grads[n] for n in SMALL]),
                        _pack_rows([mom1[n] for n in SMALL]), _pack_rows([mom2[n] for n in SMALL]), "adamw_small")
    for n, d_, m_, v_ in zip(SMALL, _unpack_rows(ds, shapes), _unpack_rows(ms, shapes), _unpack_rows(vs, shapes)):
        delta[n], new_m[n], new_v[n] = d_, m_, v_

    return (loss, grad_x, *[grads[n] for n in WEIGHTS], *[delta[n] for n in WEIGHTS], *[new_m[n] for n in WEIGHTS],
            *[new_v[n] for n in WEIGHTS])
```

```python
import functools
import math

import jax
import jax.numpy as jnp
from jax import lax
from jax.experimental import pallas as pl
from jax.experimental.pallas import tpu as pltpu

F32 = jnp.float32
BF16 = jnp.bfloat16
NORM_EPS = 1e-5
HGRN_DK = 128
HGRN_CHUNK = 64
HGRN_HEADS_PER_STEP = 16
HGRN_SEQ_BLOCK = 128
M_HEADDIM = 64
M_GROUPS = 8
M_D_STATE = 128
M_CONV = 4
M_CHUNK = 128
FFN_CONV = 3
EXP_CLIP = 80.0
LANES = 128
VMEM_LIMIT_BYTES = 56 * 1024 * 1024
FLAT_COLS = 1024
ADAM_LR, ADAM_B1, ADAM_B2, ADAM_EPS, ADAM_WD, ADAM_STEP = 0.001, 0.9, 0.999, 1e-08, 0.01, 10
MESH = pl.DeviceIdType.MESH

NN = (((1,), (0,)), ((), ()))
NT = (((1,), (1,)), ((), ()))
TN = (((0,), (0,)), ((), ()))


def _cparams(*sems):
    return pltpu.CompilerParams(dimension_semantics=sems, vmem_limit_bytes=VMEM_LIMIT_BYTES)


def _dot(a, b, dn=NN):
    return lax.dot_general(a.astype(BF16), b.astype(BF16), dn, preferred_element_type=F32)


def _dot_exact(x, m, dn=NN, passes=3, x_first=True):
    acc = None
    r = x
    for _ in range(passes):
        p = r.astype(BF16)
        r = r - p.astype(F32)
        t = lax.dot_general(p, m, dn, preferred_element_type=F32) if x_first else lax.dot_general(m, p, dn, preferred_element_type=F32)
        acc = t if acc is None else acc + t
    return acc


def _iota(shape, dim):
    return lax.broadcasted_iota(jnp.int32, shape, dim)


def _cumsum_rows(x, reverse=False):
    n = x.shape[0]
    row = _iota(x.shape, 0)
    s = 1
    while s < n:
        if reverse:
            x = x + jnp.where(row < n - s, pltpu.roll(x, n - s, 0), 0.0)
        else:
            x = x + jnp.where(row >= s, pltpu.roll(x, s, 0), 0.0)
        s *= 2
    return x


def _silu(x):
    return x * jax.nn.sigmoid(x)


def _dsilu(x):
    s = jax.nn.sigmoid(x)
    return s * (1.0 + x * (1.0 - s))


def _pick_tile(dim, pref):
    if dim <= pref:
        return dim
    best = None
    t = LANES
    while t <= pref:
        if dim % t == 0:
            best = t
        t += LANES
    assert best is not None, (dim, pref)
    return best


def _matmul(a, b, *, ta=False, tb=False, res=None, out_dtype=F32, tm=1024, tn=1024, tk=4096, name,
            a_parts=False, b_parts=False, b_layer=None, b_off=0, out=None, out_layer=None, out_off=0, dep=None):
    a = a.astype(BF16)
    b = b.astype(BF16)
    if a_parts:
        assert not ta
        pa, M, kp = a.shape
        K = pa * kp
    else:
        M, K = (a.shape[1], a.shape[0]) if ta else a.shape
    bsh = b.shape[1:] if b_layer is not None else b.shape
    if b_parts:
        assert not tb
        pb, _, np_ = bsh
        N = pb * np_
    else:
        N = bsh[0] if tb else bsh[1]
    tm, tn, tk = _pick_tile(M, tm), _pick_tile(np_ if b_parts else N, tn), _pick_tile(kp if a_parts else K, tk)
    nk = K // tk
    dn = (((0 if ta else 1,), (1 if tb else 0,)), ((), ()))
    assert b_off % tk == 0 and out_off % tm == 0

    def body(*refs):
        refs = list(refs)
        acc = refs.pop() if nk > 1 else None
        o_ref = refs.pop()
        if dep is not None:
            refs.pop()
        if out is not None:
            refs.pop()
        a_ref, b_ref = refs[0], refs[1]
        r_ref = refs[2] if res is not None else None
        k = pl.program_id(2)

        def prod():
            return lax.dot_general(a_ref[...], b_ref[...], dn, preferred_element_type=F32)

        def finish(r):
            if res is not None:
                r = r + r_ref[...]
            o_ref[...] = r.astype(out_dtype)

        if nk == 1:
            finish(prod())
            return

        @pl.when(k == 0)
        def _():
            acc[...] = prod()

        @pl.when(jnp.logical_and(k > 0, k < nk - 1))
        def _():
            acc[...] += prod()

        @pl.when(k == nk - 1)
        def _():
            finish(acc[...] + prod())

    if a_parts:
        kpb = kp // tk
        a_spec = pl.BlockSpec((None, tm, tk), lambda i, j, k: (k // kpb, i, k % kpb))
    elif ta:
        a_spec = pl.BlockSpec((tk, tm), lambda i, j, k: (k, i))
    else:
        a_spec = pl.BlockSpec((tm, tk), lambda i, j, k: (i, k))
    lead = () if b_layer is None else (b_layer,)
    lead_blk = () if b_layer is None else (None,)
    kb0 = b_off // tk
    if b_parts:
        npb = np_ // tn
        b_spec = pl.BlockSpec(lead_blk + (None, tk, tn), lambda i, j, k: lead + (j // npb, k, j % npb))
    elif tb:
        b_spec = pl.BlockSpec(lead_blk + (tn, tk), lambda i, j, k: lead + (j, k))
    else:
        b_spec = pl.BlockSpec(lead_blk + (tk, tn), lambda i, j, k: lead + (kb0 + k, j))
    r_spec = pl.BlockSpec((tm, tn), lambda i, j, k: (i, j))
    in_specs = [a_spec, b_spec] + ([r_spec] if res is not None else [])
    args = (a, b) + ((res,) if res is not None else ())
    if out is None:
        o_spec, out_shape, aliases = r_spec, jax.ShapeDtypeStruct((M, N), out_dtype), {}
    else:
        assert out.dtype == out_dtype and out.shape[-1] == N
        olead = () if out_layer is None else (out_layer,)
        olead_blk = () if out_layer is None else (None,)
        ob0 = out_off // tm
        o_spec = pl.BlockSpec(olead_blk + (tm, tn), lambda i, j, k: olead + (ob0 + i, j))
        out_shape = jax.ShapeDtypeStruct(out.shape, out.dtype)
        aliases = {len(args): 0}
        in_specs = in_specs + [pl.BlockSpec(memory_space=pl.ANY)]
        args = args + (out,)
    if dep is not None:
        in_specs = in_specs + [pl.BlockSpec(memory_space=pl.ANY)]
        args = args + (dep,)
    return pl.pallas_call(
        body, name=name, grid=(M // tm, N // tn, nk), in_specs=in_specs, out_specs=o_spec, out_shape=out_shape,
        scratch_shapes=[pltpu.VMEM((tm, tn), F32)] if nk > 1 else [], input_output_aliases=aliases,
        compiler_params=_cparams("parallel", "parallel", "arbitrary"))(*args)


def _rmsnorm_fwd(h, w, name):
    T, D = h.shape
    tm = _pick_tile(T, 256)

    def body(h_ref, w_ref, u_ref):
        x = h_ref[...]
        r = lax.rsqrt(jnp.mean(x * x, axis=-1, keepdims=True) + NORM_EPS)
        u_ref[...] = (x * r * w_ref[...]).astype(BF16)

    return pl.pallas_call(
        body, name=name, grid=(T // tm,),
        in_specs=[pl.BlockSpec((tm, D), lambda i: (i, 0)), pl.BlockSpec((1, D), lambda i: (0, 0))],
        out_specs=pl.BlockSpec((tm, D), lambda i: (i, 0)), out_shape=jax.ShapeDtypeStruct((T, D), BF16),
        compiler_params=_cparams("parallel"))(h, w.reshape(1, D))


def _rmsnorm_bwd(h, w, du, dres, name, dep=None):
    T, D = h.shape
    tm = _pick_tile(T, 256)

    def body(h_ref, w_ref, du_ref, dr_ref, *rest):
        dh_ref, dhb_ref, dw_ref = rest[-3:]
        x = h_ref[...]
        r = lax.rsqrt(jnp.mean(x * x, axis=-1, keepdims=True) + NORM_EPS)
        xh = x * r
        du_ = du_ref[...]
        dy = du_ * w_ref[...]
        dh = dr_ref[...] + r * (dy - xh * jnp.mean(dy * xh, axis=-1, keepdims=True))
        dh_ref[...] = dh
        dhb_ref[...] = dh.astype(BF16)
        part = jnp.sum(du_ * xh, axis=0, keepdims=True)

        @pl.when(pl.program_id(0) == 0)
        def _():
            dw_ref[...] = part

        @pl.when(pl.program_id(0) > 0)
        def _():
            dw_ref[...] += part

    row = pl.BlockSpec((tm, D), lambda i: (i, 0))
    vec = pl.BlockSpec((1, D), lambda i: (0, 0))
    extra_specs, extra = ([], ()) if dep is None else ([pl.BlockSpec(memory_space=pl.ANY)], (dep,))
    dh, dhb, dw = pl.pallas_call(
        body, name=name, grid=(T // tm,), in_specs=[row, vec, row, row] + extra_specs, out_specs=[row, row, vec],
        out_shape=[jax.ShapeDtypeStruct((T, D), F32), jax.ShapeDtypeStruct((T, D), BF16), jax.ShapeDtypeStruct((1, D), F32)],
        compiler_params=_cparams("arbitrary"))(h, w.reshape(1, D), du, dres, *extra)
    return dh, dhb, dw.reshape(D)


def _loss_head(h, w, target):
    T, D = h.shape
    tm = _pick_tile(T, 256)

    def body(h_ref, w_ref, t_ref, loss_ref, dh_ref, dhb_ref, dw_ref):
        x = h_ref[...]
        wv = w_ref[...]
        r = lax.rsqrt(jnp.mean(x * x, axis=-1, keepdims=True) + NORM_EPS)
        xh = x * r
        e = xh * wv - t_ref[...]
        lpart = jnp.zeros((1, LANES), F32) + 0.5 * jnp.sum(jnp.mean(e * e, axis=-1, keepdims=True))
        dyo = e * (1.0 / D)
        dy = dyo * wv
        dh = r * (dy - xh * jnp.mean(dy * xh, axis=-1, keepdims=True))
        dh_ref[...] = dh
        dhb_ref[...] = dh.astype(BF16)
        part = jnp.sum(dyo * xh, axis=0, keepdims=True)

        @pl.when(pl.program_id(0) == 0)
        def _():
            dw_ref[...] = part
            loss_ref[...] = lpart

        @pl.when(pl.program_id(0) > 0)
        def _():
            dw_ref[...] += part
            loss_ref[...] += lpart

    row = pl.BlockSpec((tm, D), lambda i: (i, 0))
    vec = pl.BlockSpec((1, D), lambda i: (0, 0))
    lvec = pl.BlockSpec((1, LANES), lambda i: (0, 0))
    loss, dh, dhb, dw = pl.pallas_call(
        body, name="loss_head", grid=(T // tm,), in_specs=[row, vec, row], out_specs=[lvec, row, row, vec],
        out_shape=[jax.ShapeDtypeStruct((1, LANES), F32), jax.ShapeDtypeStruct((T, D), F32),
                   jax.ShapeDtypeStruct((T, D), BF16), jax.ShapeDtypeStruct((1, D), F32)],
        compiler_params=_cparams("arbitrary"))(h, w.reshape(1, D), target)
    return loss, dh, dhb, dw.reshape(D)


def _hgrn_gates(qr, fr, lb):
    sig = jax.nn.sigmoid(fr)
    nsig = jax.nn.sigmoid(-fr)
    fg = lb + (1.0 - lb) * sig
    logf = jnp.log(fg)
    k = (1.0 - lb) * nsig
    q = _silu(qr)
    return q, k, logf, sig, nsig, fg


def _hgrn_scaled(q, k, b, bmid):
    eq = jnp.exp(jnp.clip(b - bmid, -EXP_CLIP, EXP_CLIP))
    ek = jnp.exp(jnp.clip(bmid - b, -EXP_CLIP, EXP_CLIP))
    return q * eq, k * ek, eq, ek


def _hgrn_fwd(proj, lb, gnw, H):
    B, L, _ = proj.shape
    C, DK = HGRN_CHUNK, HGRN_DK
    F_ = H * DK
    NC = L // C

    nh = HGRN_HEADS_PER_STEP if H % HGRN_HEADS_PER_STEP == 0 else 1
    LB = min(L, HGRN_SEQ_BLOCK)
    ncb, nsb, WD = LB // C, L // LB, nh * DK

    def body(q_ref, f_ref, v_ref, g_ref, lb_ref, gn_ref, o_ref, on_ref, st_ref, ST, bsc):
        @pl.when(pl.program_id(2) == 0)
        def _():
            ST[...] = jnp.zeros_like(ST)

        gn = gn_ref[...]
        causal = _iota((C, C), 0) >= _iota((C, C), 1)

        def chunk(c, carry):
            r0 = pl.multiple_of(c * C, C)
            rows = pl.ds(r0, C)
            for hh in range(nh):
                ln = slice(hh * DK, (hh + 1) * DK)
                q, k, logf, _, _, _ = _hgrn_gates(q_ref[0, rows, ln], f_ref[0, rows, ln], lb_ref[:, ln])
                v = v_ref[0, rows, ln]
                b = _cumsum_rows(logf)
                bsc[hh] = b
                bmid = bsc[hh, C // 2 - 1:C // 2, :]
                blast = bsc[hh, C - 1:C, :]
                qs, ks, _, _ = _hgrn_scaled(q, k, b, bmid)
                A = jnp.where(causal, _dot(qs, ks, NT), 0.0)
                st = ST[hh]
                st_ref[0, hh, c] = st
                o = _dot(A, v) + _dot(q * jnp.exp(b), st, NT)
                kb = k * jnp.exp(blast - b)
                ST[hh] = st * jnp.exp(blast) + _dot(v, kb, TN)
                rms = lax.rsqrt(jnp.mean(o * o, axis=-1, keepdims=True) + NORM_EPS)
                o_ref[0, rows, ln] = o
                on_ref[0, rows, ln] = (o * rms * gn * _silu(g_ref[0, rows, ln])).astype(BF16)
            return carry

        lax.fori_loop(0, ncb, chunk, 0)

    def col(off):
        return pl.BlockSpec((1, LB, WD), lambda b, hp, s: (b, s, off // nh + hp))

    return pl.pallas_call(
        body, name="hgrn_fwd", grid=(B, H // nh, nsb),
        in_specs=[col(0), col(H), col(2 * H), col(3 * H), pl.BlockSpec((1, WD), lambda b, hp, s: (0, hp)),
                  pl.BlockSpec((1, DK), lambda b, hp, s: (0, 0))],
        out_specs=[col(0), col(0), pl.BlockSpec((1, nh, ncb, DK, DK), lambda b, hp, s: (b, hp, s, 0, 0))],
        out_shape=[jax.ShapeDtypeStruct((B, L, F_), F32), jax.ShapeDtypeStruct((B, L, F_), BF16),
                   jax.ShapeDtypeStruct((B, H, NC, DK, DK), F32)],
        scratch_shapes=[pltpu.VMEM((nh, DK, DK), F32), pltpu.VMEM((nh, C, DK), F32)],
        compiler_params=_cparams("parallel", "parallel", "arbitrary"))(proj, proj, proj, proj, lb, gnw)


def _hgrn_bwd(proj, o, don, st, lb, gnw, H):
    B, L, _ = proj.shape
    C, DK = HGRN_CHUNK, HGRN_DK
    F_ = H * DK
    NC = L // C

    nh = HGRN_HEADS_PER_STEP if H % HGRN_HEADS_PER_STEP == 0 else 1
    LB = min(L, HGRN_SEQ_BLOCK)
    ncb, nsb, WD = LB // C, L // LB, nh * DK

    def body(q_ref, f_ref, v_ref, g_ref, o_ref, do_ref, st_ref, lb_ref, gn_ref,
             dp_ref, dlb_ref, dgn_ref, DST, bsc):
        @pl.when(pl.program_id(2) == 0)
        def _():
            DST[...] = jnp.zeros_like(DST)
            dlb_ref[...] = jnp.zeros_like(dlb_ref)
            dgn_ref[...] = jnp.zeros_like(dgn_ref)

        gn = gn_ref[...]
        causal = _iota((C, C), 0) >= _iota((C, C), 1)
        lastrow = _iota((C, DK), 0) == C - 1

        def chunk(i, carry):
            c = ncb - 1 - i
            r0 = pl.multiple_of(c * C, C)
            rows = pl.ds(r0, C)
            for hh in range(nh):
                ln = slice(hh * DK, (hh + 1) * DK)
                lbv = lb_ref[:, ln]
                qr = q_ref[0, rows, ln]
                fr = f_ref[0, rows, ln]
                q, k, logf, sig, nsig, fg = _hgrn_gates(qr, fr, lbv)
                v = v_ref[0, rows, ln]
                b = _cumsum_rows(logf)
                bsc[hh] = b
                bmid = bsc[hh, C // 2 - 1:C // 2, :]
                blast = bsc[hh, C - 1:C, :]
                qs, ks, eq, ek = _hgrn_scaled(q, k, b, bmid)
                A = jnp.where(causal, _dot(qs, ks, NT), 0.0)
                st_in = st_ref[0, hh, c]
                dst = DST[hh]
                eb = jnp.exp(b)
                ebl = jnp.exp(blast)
                ekb = jnp.exp(blast - b)
                qb = q * eb
                kb = k * ekb
                ov = o_ref[0, rows, ln]
                gr = g_ref[0, rows, ln]
                rms = lax.rsqrt(jnp.mean(ov * ov, axis=-1, keepdims=True) + NORM_EPS)
                oh = ov * rms
                sg = _silu(gr)
                don_ = do_ref[0, rows, ln]
                dgn_ref[0, hh] += jnp.sum(don_ * oh * sg, axis=0, keepdims=True)
                dp_ref[3, 0, rows, ln] = (don_ * oh * gn * _dsilu(gr)).astype(BF16)
                doh = don_ * gn * sg
                do_ = rms * (doh - oh * jnp.mean(doh * oh, axis=-1, keepdims=True))
                dA = jnp.where(causal, _dot(do_, v, NT), 0.0)
                dp_ref[2, 0, rows, ln] = (_dot(A, do_, TN) + _dot(kb, dst, NT)).astype(BF16)
                dqb = _dot(do_, st_in)
                dkb = _dot(v, dst)
                dq = _dot(dA, ks) * eq + dqb * eb
                dk_inter = dkb * ekb
                dk = _dot(dA, qs, TN) * ek + dk_inter
                db = q * dq - k * dk
                extra = jnp.sum(k * dk_inter, axis=0, keepdims=True) + ebl * jnp.sum(st_in * dst, axis=0, keepdims=True)
                db = db + jnp.where(lastrow, extra, 0.0)
                dlogf = _cumsum_rows(db, reverse=True)
                DST[hh] = dst * ebl + _dot(do_, qb, TN)
                dp_ref[0, 0, rows, ln] = (dq * _dsilu(qr)).astype(BF16)
                ss = sig * nsig
                dp_ref[1, 0, rows, ln] = ((1.0 - lbv) * ss * (dlogf / fg - dk)).astype(BF16)
                dlb_ref[0, :, ln] += jnp.sum(dlogf * nsig / fg - dk * nsig, axis=0, keepdims=True)
            return carry

        lax.fori_loop(0, ncb, chunk, 0)

    def col(off):
        return pl.BlockSpec((1, LB, WD), lambda b, hp, s: (b, nsb - 1 - s, off // nh + hp))

    outs = pl.pallas_call(
        body, name="hgrn_bwd", grid=(B, H // nh, nsb),
        in_specs=[col(0), col(H), col(2 * H), col(3 * H), col(0), col(0),
                  pl.BlockSpec((1, nh, ncb, DK, DK), lambda b, hp, s: (b, hp, nsb - 1 - s, 0, 0)),
                  pl.BlockSpec((1, WD), lambda b, hp, s: (0, hp)), pl.BlockSpec((1, DK), lambda b, hp, s: (0, 0))],
        out_specs=[pl.BlockSpec((4, 1, LB, WD), lambda b, hp, s: (0, b, nsb - 1 - s, hp)),
                   pl.BlockSpec((1, 1, WD), lambda b, hp, s: (b, 0, hp)),
                   pl.BlockSpec((1, nh, 1, DK), lambda b, hp, s: (b, hp, 0, 0))],
        out_shape=[jax.ShapeDtypeStruct((4, B, L, F_), BF16), jax.ShapeDtypeStruct((B, 1, F_), F32),
                   jax.ShapeDtypeStruct((B, H, 1, DK), F32)],
        scratch_shapes=[pltpu.VMEM((nh, DK, DK), F32), pltpu.VMEM((nh, C, DK), F32)],
        compiler_params=_cparams("parallel", "parallel", "arbitrary"))(proj, proj, proj, proj, o, don, st, lb, gnw)
    return outs


CONV_ROWS = 256
PAD_ROWS = 8


def _conv_taps(pad_ref, w_ref, r0, K, rb, forward=True, keep=False):
    ext = pad_ref[pl.ds(r0, rb + PAD_ROWS), :]
    n = rb + PAD_ROWS
    acc = None
    shifts = []
    for s in range(K):
        if forward:
            sh = ext if s == 0 else pltpu.roll(ext, s, 0)
            term = sh[PAD_ROWS:, :]
        else:
            sh = ext if s == 0 else pltpu.roll(ext, n - s, 0)
            term = sh[:rb, :]
        shifts.append(term)
        term = term * w_ref[K - 1 - s:K - s, :]
        acc = term if acc is None else acc + term
    return (acc, shifts) if keep else acc


def _conv_dw(shifts, dc, K):
    row = _iota((8, dc.shape[1]), 0)
    out = jnp.zeros((8, dc.shape[1]), F32)
    for kk in range(K):
        out = out + jnp.where(row == kk, jnp.sum(dc * shifts[K - 1 - kk], axis=0, keepdims=True), 0.0)
    return out


def _mconv_fwd(zx, cw, cb, col0, width):
    B, L, _ = zx.shape
    K = cw.shape[0]
    ct = _pick_tile(width, 256)
    rb = min(CONV_ROWS, L)
    nrb = L // rb
    off = col0 // ct

    def body(x_ref, w_ref, b_ref, y_ref, xp):
        xp[0:PAD_ROWS, :] = jnp.zeros((PAD_ROWS, ct), F32)
        xp[PAD_ROWS:, :] = x_ref[0]
        bias = b_ref[...]

        def blk(i, carry):
            r0 = pl.multiple_of(i * rb, rb)
            y_ref[0, pl.ds(r0, rb), :] = _silu(_conv_taps(xp, w_ref, r0, K, rb) + bias)
            return carry

        lax.fori_loop(0, nrb, blk, 0)

    return pl.pallas_call(
        body, name="mconv_fwd", grid=(B, width // ct),
        in_specs=[pl.BlockSpec((1, L, ct), lambda b, j: (b, 0, off + j)), pl.BlockSpec((K, ct), lambda b, j: (0, j)),
                  pl.BlockSpec((1, ct), lambda b, j: (0, j))],
        out_specs=pl.BlockSpec((1, L, ct), lambda b, j: (b, 0, j)),
        out_shape=jax.ShapeDtypeStruct((B, L, width), F32),
        scratch_shapes=[pltpu.VMEM((L + PAD_ROWS, ct), F32)],
        compiler_params=_cparams("parallel", "parallel"))(zx, cw, cb.reshape(1, width))


def _mconv_bwd(zx, dya, cw, cb, col0, wcol0, name):
    B, L, _ = zx.shape
    K = cw.shape[0]
    npart, _, _, wq = dya.shape
    width = npart * wq
    ct = _pick_tile(wq, 256)
    rb = min(CONV_ROWS, L)
    nrb = L // rb
    off = (col0 + wcol0) // ct
    woff = wcol0 // ct
    pq = wq // ct

    def body(x_ref, dy_ref, w_ref, b_ref, dx_ref, dw_ref, db_ref, xp, dcp):
        xp[0:PAD_ROWS, :] = jnp.zeros((PAD_ROWS, ct), F32)
        xp[PAD_ROWS:, :] = x_ref[0]
        dcp[L:, :] = jnp.zeros((PAD_ROWS, ct), F32)
        bias = b_ref[...]

        def blk1(i, carry):
            dw, db = carry
            r0 = pl.multiple_of(i * rb, rb)
            cpre, shifts = _conv_taps(xp, w_ref, r0, K, rb, keep=True)
            dc = dy_ref[0, 0, pl.ds(r0, rb), :] * _dsilu(cpre + bias)
            dcp[pl.ds(r0, rb), :] = dc
            return dw + _conv_dw(shifts, dc, K), db + jnp.sum(dc, axis=0, keepdims=True)

        dw, db = lax.fori_loop(0, nrb, blk1, (jnp.zeros((8, ct), F32), jnp.zeros((1, ct), F32)))
        dw_ref[0] = dw
        db_ref[0] = db

        def blk2(i, carry):
            r0 = pl.multiple_of(i * rb, rb)
            dx_ref[0, pl.ds(r0, rb), :] = _conv_taps(dcp, w_ref, r0, K, rb, forward=False).astype(BF16)
            return carry

        lax.fori_loop(0, nrb, blk2, 0)

    dx, dw, db = pl.pallas_call(
        body, name=name, grid=(B, width // ct),
        in_specs=[pl.BlockSpec((1, L, ct), lambda b, j: (b, 0, off + j)),
                  pl.BlockSpec((1, 1, L, ct), lambda b, j: (j // pq, b, 0, j % pq)),
                  pl.BlockSpec((K, ct), lambda b, j: (0, woff + j)), pl.BlockSpec((1, ct), lambda b, j: (0, woff + j))],
        out_specs=[pl.BlockSpec((1, L, ct), lambda b, j: (b, 0, j)), pl.BlockSpec((1, 8, ct), lambda b, j: (b, 0, j)),
                   pl.BlockSpec((1, 1, ct), lambda b, j: (b, 0, j))],
        out_shape=[jax.ShapeDtypeStruct((B, L, width), BF16), jax.ShapeDtypeStruct((B, 8, width), F32),
                   jax.ShapeDtypeStruct((B, 1, width), F32)],
        scratch_shapes=[pltpu.VMEM((L + PAD_ROWS, ct), F32), pltpu.VMEM((L + PAD_ROWS, ct), F32)],
        compiler_params=_cparams("parallel", "parallel"))(zx, dya, cw, cb.reshape(1, -1))
    return dx, dw[:, :K, :], db


def _ffn_mid_fwd(up, cw, cb, dff):
    B, L, _ = up.shape
    K = cw.shape[0]
    ct = _pick_tile(dff, 256)
    rb = min(CONV_ROWS, L)
    nrb = L // rb
    half = dff // ct

    def body(g_ref, u_ref, wg_ref, wu_ref, bg_ref, bu_ref, a_ref, gp, upad):
        gp[0:PAD_ROWS, :] = jnp.zeros((PAD_ROWS, ct), F32)
        upad[0:PAD_ROWS, :] = jnp.zeros((PAD_ROWS, ct), F32)
        gp[PAD_ROWS:, :] = g_ref[0]
        upad[PAD_ROWS:, :] = u_ref[0]
        bg, bu = bg_ref[...], bu_ref[...]

        def blk(i, carry):
            r0 = pl.multiple_of(i * rb, rb)
            cg = _conv_taps(gp, wg_ref, r0, K, rb) + bg
            cu = _conv_taps(upad, wu_ref, r0, K, rb) + bu
            a_ref[0, pl.ds(r0, rb), :] = (_silu(cg) * cu).astype(BF16)
            return carry

        lax.fori_loop(0, nrb, blk, 0)

    xg = pl.BlockSpec((1, L, ct), lambda b, j: (b, 0, j))
    xu = pl.BlockSpec((1, L, ct), lambda b, j: (b, 0, half + j))
    wgs = pl.BlockSpec((K, ct), lambda b, j: (0, j))
    wus = pl.BlockSpec((K, ct), lambda b, j: (0, half + j))
    bgs = pl.BlockSpec((1, ct), lambda b, j: (0, j))
    bus = pl.BlockSpec((1, ct), lambda b, j: (0, half + j))
    cb2 = cb.reshape(1, 2 * dff)
    return pl.pallas_call(
        body, name="ffn_mid_fwd", grid=(B, half), in_specs=[xg, xu, wgs, wus, bgs, bus], out_specs=xg,
        out_shape=jax.ShapeDtypeStruct((B, L, dff), BF16),
        scratch_shapes=[pltpu.VMEM((L + PAD_ROWS, ct), F32), pltpu.VMEM((L + PAD_ROWS, ct), F32)],
        compiler_params=_cparams("parallel", "parallel"))(up, up, cw, cw, cb2, cb2)


def _ffn_mid_bwd(up, dact, cw, cb, dff):
    B, L, _ = up.shape
    K = cw.shape[0]
    ct = _pick_tile(dff, 256)
    rb = min(CONV_ROWS, L)
    nrb = L // rb
    half = dff // ct

    def body(g_ref, u_ref, da_ref, wg_ref, wu_ref, bg_ref, bu_ref, dx_ref, dwg_ref, dwu_ref, dbg_ref, dbu_ref,
             gp, upad, dgp, dup):
        gp[0:PAD_ROWS, :] = jnp.zeros((PAD_ROWS, ct), F32)
        upad[0:PAD_ROWS, :] = jnp.zeros((PAD_ROWS, ct), F32)
        gp[PAD_ROWS:, :] = g_ref[0]
        upad[PAD_ROWS:, :] = u_ref[0]
        dgp[L:, :] = jnp.zeros((PAD_ROWS, ct), F32)
        dup[L:, :] = jnp.zeros((PAD_ROWS, ct), F32)
        bg, bu = bg_ref[...], bu_ref[...]

        def blk1(i, carry):
            dwg, dwu, dbg, dbu = carry
            r0 = pl.multiple_of(i * rb, rb)
            cg, sg_ = _conv_taps(gp, wg_ref, r0, K, rb, keep=True)
            cu, su_ = _conv_taps(upad, wu_ref, r0, K, rb, keep=True)
            cg = cg + bg
            cu = cu + bu
            da = da_ref[0, pl.ds(r0, rb), :]
            sig = jax.nn.sigmoid(cg)
            dcg = da * cu * (sig * (1.0 + cg * (1.0 - sig)))
            dcu = da * (cg * sig)
            dgp[pl.ds(r0, rb), :] = dcg
            dup[pl.ds(r0, rb), :] = dcu
            return (dwg + _conv_dw(sg_, dcg, K), dwu + _conv_dw(su_, dcu, K), dbg + jnp.sum(dcg, axis=0, keepdims=True),
                    dbu + jnp.sum(dcu, axis=0, keepdims=True))

        z8 = jnp.zeros((8, ct), F32)
        z1 = jnp.zeros((1, ct), F32)
        dwg, dwu, dbg, dbu = lax.fori_loop(0, nrb, blk1, (z8, z8, z1, z1))
        dwg_ref[0] = dwg
        dwu_ref[0] = dwu
        dbg_ref[0] = dbg
        dbu_ref[0] = dbu

        def blk2(i, carry):
            r0 = pl.multiple_of(i * rb, rb)
            dx_ref[0, 0, pl.ds(r0, rb), :] = _conv_taps(dgp, wg_ref, r0, K, rb, forward=False).astype(BF16)
            dx_ref[1, 0, pl.ds(r0, rb), :] = _conv_taps(dup, wu_ref, r0, K, rb, forward=False).astype(BF16)
            return carry

        lax.fori_loop(0, nrb, blk2, 0)

    xg = pl.BlockSpec((1, L, ct), lambda b, j: (b, 0, j))
    xu = pl.BlockSpec((1, L, ct), lambda b, j: (b, 0, half + j))
    wgs = pl.BlockSpec((K, ct), lambda b, j: (0, j))
    wus = pl.BlockSpec((K, ct), lambda b, j: (0, half + j))
    bgs = pl.BlockSpec((1, ct), lambda b, j: (0, j))
    bus = pl.BlockSpec((1, ct), lambda b, j: (0, half + j))
    w8 = pl.BlockSpec((1, 8, ct), lambda b, j: (b, 0, j))
    b1 = pl.BlockSpec((1, 1, ct), lambda b, j: (b, 0, j))
    cb2 = cb.reshape(1, 2 * dff)
    pad = pltpu.VMEM((L + PAD_ROWS, ct), F32)
    dx2, dwg, dwu, dbg, dbu = pl.pallas_call(
        body, name="ffn_mid_bwd", grid=(B, half), in_specs=[xg, xu, xg, wgs, wus, bgs, bus],
        out_specs=[pl.BlockSpec((2, 1, L, ct), lambda b, j: (0, b, 0, j)), w8, w8, b1, b1],
        out_shape=[jax.ShapeDtypeStruct((2, B, L, dff), BF16)] + [jax.ShapeDtypeStruct((B, 8, dff), F32)] * 2
        + [jax.ShapeDtypeStruct((B, 1, dff), F32)] * 2,
        scratch_shapes=[pad, pad, pad, pad],
        compiler_params=_cparams("parallel", "parallel"))(up, up, dact, cw, cw, cb2, cb2)
    dw = jnp.concatenate([dwg[:, :K], dwu[:, :K]], axis=-1)
    db = jnp.concatenate([dbg, dbu], axis=-1)
    return dx2, dw, db


def _ssd_consts(hpg, W):
    P = M_HEADDIM
    E = (_iota((LANES, W), 0) == _iota((LANES, W), 1) // P).astype(BF16)
    Ebig = (_iota((LANES, hpg * LANES), 0) == _iota((LANES, hpg * LANES), 1) // LANES).astype(BF16)
    causal = _iota((M_CHUNK, M_CHUNK), 0) >= _iota((M_CHUNK, M_CHUNK), 1)
    head_of_lane = _iota((1, W), 1) // P
    return E, Ebig, causal, head_of_lane


def _ssd_chunk_fwd(xs, Bm, Cm, dtr, bias, Aneg, E, Ebig, causal, head_of_lane, hpg, st, ar_sc, ae_sc):
    pre = dtr + bias
    dt = jnp.maximum(pre, 0.0) + jnp.log(1.0 + jnp.exp(-jnp.abs(pre)))
    Ad = dt * Aneg
    a_c = _cumsum_rows(Ad)
    ar_sc[...] = a_c.T
    aexp = _dot_exact(a_c, E)
    ae_sc[...] = aexp
    alast = ae_sc[M_CHUNK - 1:M_CHUNK, :]
    dtexp = _dot_exact(dt, E)
    X = xs * dtexp
    AC = _dot_exact(a_c, Ebig)
    CB = _dot(Cm, Bm, NT)
    Xb = X.astype(BF16)
    Ls = [jnp.where(causal, jnp.exp(jnp.minimum(AC[:, j * LANES:(j + 1) * LANES] - ar_sc[j:j + 1, :], 0.0)), 0.0)
          for j in range(hpg)]
    first = _iota((1, LANES), 1) < M_HEADDIM
    pairs = []
    for p in range(hpg // 2):
        Xp = Xb[:, p * LANES:(p + 1) * LANES]
        pairs.append(jnp.where(first, _dot(CB * Ls[2 * p], Xp), _dot(CB * Ls[2 * p + 1], Xp)))
    ydiag = pairs[0] if len(pairs) == 1 else jnp.concatenate(pairs, axis=1)
    ea = jnp.exp(aexp)
    yoff = ea * _dot(Cm, st)
    dec = jnp.exp(alast - aexp)
    return dict(dt=dt, a_c=a_c, aexp=aexp, alast=alast, dtexp=dtexp, X=X, Xb=Xb, CB=CB, Ls=Ls, ydiag=ydiag, ea=ea,
                yoff=yoff, dec=dec)


def _ssd_fwd(xbca, zx, dtc, bias, Aneg, Dexp, nw, hpg):
    B, L, _ = xbca.shape
    G, N, C = M_GROUPS, M_D_STATE, M_CHUNK
    W = hpg * M_HEADDIM
    DI = G * W
    NC = L // C
    LB = min(L, 4 * C)
    ncb = LB // C

    def body(xs_ref, b_ref, c_ref, z_ref, dt_ref, bias_ref, a_ref, d_ref, nw_ref, y_ref, yn_ref, st_ref, ST, ar_sc, ae_sc):
        @pl.when(pl.program_id(2) == 0)
        def _():
            ST[...] = jnp.zeros_like(ST)

        E, Ebig, causal, head_of_lane = _ssd_consts(hpg, W)
        bias_ = bias_ref[0]
        Aneg_ = a_ref[0]
        Dv = d_ref[...]
        nwv = nw_ref[...]

        def chunk(ci, carry):
            r0 = pl.multiple_of(ci * C, C)
            rows = pl.ds(r0, C)
            xs = xs_ref[0, rows, :]
            Bm = b_ref[0, rows, :]
            Cm = c_ref[0, rows, :]
            st = ST[...]
            st_ref[0, 0, ci] = st
            f = _ssd_chunk_fwd(xs, Bm, Cm, dt_ref[0, 0, ci], bias_, Aneg_, E, Ebig, causal, head_of_lane, hpg, st, ar_sc, ae_sc)
            y = f["ydiag"] + f["yoff"] + xs * Dv
            ST[...] = st * jnp.exp(f["alast"]) + _dot(Bm, f["X"] * f["dec"], TN)
            yg = y * _silu(z_ref[0, rows, :])
            rstd = lax.rsqrt(jnp.mean(yg * yg, axis=-1, keepdims=True) + NORM_EPS)
            y_ref[0, rows, :] = y
            yn_ref[0, rows, :] = (yg * rstd * nwv).astype(BF16)
            return carry

        lax.fori_loop(0, ncb, chunk, 0)

    xw = pl.BlockSpec((1, LB, W), lambda b, g, s: (b, s, g))
    bsp = pl.BlockSpec((1, LB, N), lambda b, g, s: (b, s, DI // N + g))
    csp = pl.BlockSpec((1, LB, N), lambda b, g, s: (b, s, DI // N + G + g))
    dts = pl.BlockSpec((1, 1, ncb, C, LANES), lambda b, g, s: (b, g, s, 0, 0))
    hv = pl.BlockSpec((1, 1, LANES), lambda b, g, s: (g, 0, 0))
    wv = pl.BlockSpec((1, W), lambda b, g, s: (0, g))
    sts = pl.BlockSpec((1, 1, ncb, N, W), lambda b, g, s: (b, g, s, 0, 0))
    return pl.pallas_call(
        body, name="ssd_fwd", grid=(B, G, L // LB), in_specs=[xw, bsp, csp, xw, dts, hv, hv, wv, wv],
        out_specs=[xw, xw, sts],
        out_shape=[jax.ShapeDtypeStruct((B, L, DI), F32), jax.ShapeDtypeStruct((B, L, DI), BF16),
                   jax.ShapeDtypeStruct((B, G, NC, N, W), F32)],
        scratch_shapes=[pltpu.VMEM((N, W), F32), pltpu.VMEM((LANES, C), F32), pltpu.VMEM((C, W), F32)],
        compiler_params=_cparams("parallel", "parallel", "arbitrary"))(xbca, xbca, xbca, zx, dtc, bias, Aneg, Dexp, nw)


def _ssd_bwd(xbca, zx, dtc, ypre, dyn, st, bias, Aneg, Dexp, nw, hpg):
    B, L, _ = xbca.shape
    G, N, C = M_GROUPS, M_D_STATE, M_CHUNK
    W = hpg * M_HEADDIM
    DI = G * W
    NC = L // C
    LB = min(L, 4 * C)
    ncb = LB // C
    nsb = L // LB

    def body(xs_ref, b_ref, c_ref, z_ref, dt_ref, y_ref, dyn_ref, st_ref, bias_ref, a_ref, d_ref, nw_ref,
             dxs_ref, dbc_ref, dz_ref, ddt_ref, dnw_ref, dd_ref, da_ref, dbias_ref, DST, ar_sc, ae_sc):
        @pl.when(pl.program_id(2) == 0)
        def _():
            DST[...] = jnp.zeros_like(DST)
            dnw_ref[...] = jnp.zeros_like(dnw_ref)
            dd_ref[...] = jnp.zeros_like(dd_ref)
            da_ref[...] = jnp.zeros_like(da_ref)
            dbias_ref[...] = jnp.zeros_like(dbias_ref)

        E, Ebig, causal, head_of_lane = _ssd_consts(hpg, W)
        bias_ = bias_ref[0]
        Aneg_ = a_ref[0]
        Dv = d_ref[...]
        nwv = nw_ref[...]
        lane = _iota((1, LANES), 1)
        subl = _iota((LANES, 1), 0)
        lastrow = _iota((C, W), 0) == C - 1

        def chunk(i, carry):
            ci = ncb - 1 - i
            r0 = pl.multiple_of(ci * C, C)
            rows = pl.ds(r0, C)
            xs = xs_ref[0, rows, :]
            Bm = b_ref[0, rows, :]
            Cm = c_ref[0, rows, :]
            zr = z_ref[0, rows, :]
            dtr = dt_ref[0, 0, ci]
            st_in = st_ref[0, 0, ci]
            dst = DST[...]
            f = _ssd_chunk_fwd(xs, Bm, Cm, dtr, bias_, Aneg_, E, Ebig, causal, head_of_lane, hpg, st_in, ar_sc, ae_sc)
            X, Xb, dec, ea, CB = f["X"], f["Xb"], f["dec"], f["ea"], f["CB"]
            y = y_ref[0, rows, :]
            sz = _silu(zr)
            yg = y * sz
            rstd = lax.rsqrt(jnp.mean(yg * yg, axis=-1, keepdims=True) + NORM_EPS)
            yh = yg * rstd
            dyn_ = dyn_ref[0, rows, :]
            dnw_ref[0, 0] += jnp.sum(dyn_ * yh, axis=0, keepdims=True)
            dyh = dyn_ * nwv
            dyg = rstd * (dyh - yh * jnp.mean(dyh * yh, axis=-1, keepdims=True))
            dz_ref[0, rows, :] = (dyg * y * _dsilu(zr)).astype(BF16)
            dy = dyg * sz
            dd_ref[0, 0] += jnp.sum(dy * xs, axis=0, keepdims=True)
            dxs = dy * Dv
            dYo = dy * ea
            daexp = dy * f["yoff"]
            dCm = _dot(dYo, st_in, NT)
            dst_in = _dot(Cm, dYo, TN)
            dyb = dy.astype(BF16)
            dCB = jnp.zeros((C, C), F32)
            da_col = jnp.zeros((C, LANES), F32)
            da_row = jnp.zeros((LANES, C), F32)
            first = lane < M_HEADDIM
            dXs = []
            for p in range(hpg // 2):
                Xp = Xb[:, p * LANES:(p + 1) * LANES]
                dYp = dyb[:, p * LANES:(p + 1) * LANES]
                dXp = None
                for j in (2 * p, 2 * p + 1):
                    Lj = f["Ls"][j]
                    Gj = CB * Lj
                    dYj = jnp.where(first if j % 2 == 0 else jnp.logical_not(first), dYp, jnp.zeros_like(dYp))
                    t = _dot(Gj, dYj, TN)
                    dXp = t if dXp is None else dXp + t
                    dGj = _dot(dYj, Xp, NT)
                    dCB = dCB + dGj * Lj
                    Wj = dGj * Gj
                    da_col = da_col + jnp.sum(Wj, axis=1, keepdims=True) * (lane == j).astype(F32)
                    da_row = da_row + (subl == j).astype(F32) * jnp.sum(Wj, axis=0, keepdims=True)
                dXs.append(dXp)
            dX = dXs[0] if len(dXs) == 1 else jnp.concatenate(dXs, axis=1)
            dCm = dCm + _dot(dCB, Bm)
            dBm = _dot(dCB, Cm, TN)
            ela = jnp.exp(f["alast"])
            dalast = jnp.sum(dst * st_in, axis=0, keepdims=True) * ela
            DST[...] = dst * ela + dst_in
            dXd = _dot(Bm, dst)
            dBm = dBm + _dot(X * dec, dst, NT)
            dX = dX + dXd * dec
            ddec = dXd * X * dec
            dalast = dalast + jnp.sum(ddec, axis=0, keepdims=True)
            daexp = daexp - ddec + jnp.where(lastrow, dalast, 0.0)
            dxs = dxs + dX * f["dtexp"]
            ddtexp = dX * xs
            ddt = _dot_exact(ddtexp, E, NT, passes=2)
            da_c = _dot_exact(daexp, E, NT, passes=2) + da_col - da_row.T
            dAd = _cumsum_rows(da_c, reverse=True)
            ddt = ddt + dAd * Aneg_
            da_ref[0, 0] += jnp.sum(dAd * f["dt"], axis=0, keepdims=True) * Aneg_
            ddtr = ddt * jax.nn.sigmoid(dtr + bias_)
            dbias_ref[0, 0] += jnp.sum(ddtr, axis=0, keepdims=True)
            ddt_ref[0, 0, ci] = ddtr
            dxs_ref[0, rows, :] = dxs
            dbc_ref[0, 0, rows, :] = dBm
            dbc_ref[1, 0, rows, :] = dCm
            return carry

        lax.fori_loop(0, ncb, chunk, 0)

    def rev(s):
        return nsb - 1 - s

    xw = pl.BlockSpec((1, LB, W), lambda b, g, s: (b, rev(s), g))
    bsp = pl.BlockSpec((1, LB, N), lambda b, g, s: (b, rev(s), DI // N + g))
    csp = pl.BlockSpec((1, LB, N), lambda b, g, s: (b, rev(s), DI // N + G + g))
    gsp = pl.BlockSpec((1, LB, N), lambda b, g, s: (b, rev(s), g))
    dts = pl.BlockSpec((1, 1, ncb, C, LANES), lambda b, g, s: (b, g, rev(s), 0, 0))
    hv = pl.BlockSpec((1, 1, LANES), lambda b, g, s: (g, 0, 0))
    wv = pl.BlockSpec((1, W), lambda b, g, s: (0, g))
    sts = pl.BlockSpec((1, 1, ncb, N, W), lambda b, g, s: (b, g, rev(s), 0, 0))
    accw = pl.BlockSpec((1, 1, 1, W), lambda b, g, s: (b, g, 0, 0))
    acch = pl.BlockSpec((1, 1, 1, LANES), lambda b, g, s: (b, g, 0, 0))
    return pl.pallas_call(
        body, name="ssd_bwd", grid=(B, G, nsb), in_specs=[xw, bsp, csp, xw, dts, xw, xw, sts, hv, hv, wv, wv],
        out_specs=[xw, pl.BlockSpec((2, 1, LB, N), lambda b, g, s: (0, b, rev(s), g)), xw, dts, accw, accw, acch, acch],
        out_shape=[jax.ShapeDtypeStruct((B, L, DI), F32), jax.ShapeDtypeStruct((2, B, L, G * N), F32),
                   jax.ShapeDtypeStruct((B, L, DI), BF16),
                   jax.ShapeDtypeStruct((B, G, NC, C, LANES), F32), jax.ShapeDtypeStruct((B, G, 1, W), F32),
                   jax.ShapeDtypeStruct((B, G, 1, W), F32), jax.ShapeDtypeStruct((B, G, 1, LANES), F32),
                   jax.ShapeDtypeStruct((B, G, 1, LANES), F32)],
        scratch_shapes=[pltpu.VMEM((N, W), F32), pltpu.VMEM((LANES, C), F32), pltpu.VMEM((C, W), F32)],
        compiler_params=_cparams("parallel", "parallel", "arbitrary"))(
            xbca, xbca, xbca, zx, dtc, ypre, dyn, st, bias, Aneg, Dexp, nw)


def _adamw(w, g, m, v, name, echo=False):
    shape = w.shape
    n = w.size
    cols = shape[-1]
    rows = n // cols
    tr = rows
    for cand in (512, 256, 128, 64, 32, 16, 8):
        if rows % cand == 0 and cand * cols * 4 <= 1024 * 1024:
            tr = cand
            break
    c1 = 1.0 / (1.0 - ADAM_B1 ** ADAM_STEP)
    c2 = 1.0 / (1.0 - ADAM_B2 ** ADAM_STEP)

    def body(w_ref, g_ref, m_ref, v_ref, d_ref, mo_ref, vo_ref, *go_ref):
        g_ = g_ref[...]
        mn = ADAM_B1 * m_ref[...] + (1.0 - ADAM_B1) * g_
        vn = ADAM_B2 * v_ref[...] + (1.0 - ADAM_B2) * (g_ * g_)
        d_ref[...] = -ADAM_LR * ((mn * c1) / (jnp.sqrt(vn * c2) + ADAM_EPS) + ADAM_WD * w_ref[...])
        mo_ref[...] = mn
        vo_ref[...] = vn
        if echo:
            go_ref[0][...] = g_

    spec = pl.BlockSpec((tr, cols), lambda i: (i, 0))
    r2 = lambda a: a.reshape(rows, cols)
    nout = 4 if echo else 3
    outs = pl.pallas_call(
        body, name=name, grid=(rows // tr,), in_specs=[spec] * 4, out_specs=[spec] * nout,
        out_shape=[jax.ShapeDtypeStruct((rows, cols), F32)] * nout,
        compiler_params=_cparams("parallel"))(r2(w), r2(g), r2(m), r2(v))
    return tuple(o.reshape(shape) for o in outs)


def _lower_bounds(lb_logits):
    p = jax.nn.softmax(lb_logits.astype(F32), axis=0)
    return jnp.cumsum(p, axis=0) - p[0]


def _pad_cols(a, n):
    return a if a.shape[-1] == n else jnp.pad(a, [(0, 0)] * (a.ndim - 1) + [(0, n - a.shape[-1])])


def _heads_to_lanes(a, G, hpg):
    return _pad_cols(a.reshape(G, 1, hpg), LANES)


def _local_step(x, target, P, fetch, emit):
    B, L, D = x.shape
    T = B * L
    depth = P["mix_norm"].shape[0]
    H = D // HGRN_DK
    F_ = H * HGRN_DK
    DI = P["m_norm"].shape[1]
    G, N = M_GROUPS, M_D_STATE
    MH = DI // M_HEADDIM
    hpg = MH // G
    assert hpg <= 8
    W = hpg * M_HEADDIM
    CD = DI + 2 * G * N
    MIN = DI + CD + MH
    MPAD = -(-MIN // LANES) * LANES
    dff = P["f_conv_b"].shape[1] // 2
    NC = L // M_CHUNK
    lbs = _lower_bounds(P["hgrn_lb_logits"])

    h = x.reshape(T, D)
    saved = []
    for i in range(depth):
        j = i // 2
        Wl = dict(fetch(i, ("mix_in",), h))
        s = {"h_in": h, "W": Wl}
        u = _rmsnorm_fwd(h, P["mix_norm"][i], "mix_norm_fwd")
        s["u"] = u
        if i % 2 == 0:
            proj = _matmul(u, Wl["mix_in"], name="hgrn_in_fwd").reshape(B, L, 4 * F_)
            o, on, st = _hgrn_fwd(proj, lbs[j].reshape(1, F_), P["hgrn_gnorm"][j].reshape(1, HGRN_DK), H)
            Wl.update(fetch(i, ("mix_out",), on))
            h = _matmul(on.reshape(T, F_), Wl["mix_out"], res=h, name="hgrn_out_fwd")
            s.update(proj=proj, o=o, on=on, st=st)
        else:
            zx = _matmul(u, Wl["mix_in"], tb=True, tn=1152, name="m_in_fwd").reshape(B, L, MPAD)
            xbca = _mconv_fwd(zx, P["m_conv_w"][j], P["m_conv_b"][j], DI, CD)
            dtr = zx[:, :, DI + CD:DI + CD + MH].reshape(B, NC, M_CHUNK, G, hpg).transpose(0, 3, 1, 2, 4)
            dtc = _pad_cols(dtr, LANES)
            bias = _heads_to_lanes(P["m_dt_bias"][j], G, hpg)
            Aneg = _heads_to_lanes(-jnp.exp(P["m_A_log"][j]), G, hpg)
            Dexp = jnp.repeat(P["m_D"][j], M_HEADDIM).reshape(1, DI)
            nw = P["m_norm"][j].reshape(1, DI)
            ypre, yn, st = _ssd_fwd(xbca, zx, dtc, bias, Aneg, Dexp, nw, hpg)
            Wl.update(fetch(i, ("mix_out",), yn))
            h = _matmul(yn.reshape(T, DI), Wl["mix_out"], res=h, name="m_out_fwd")
            s.update(zx=zx, xbca=xbca, dtc=dtc, bias=bias, Aneg=Aneg, Dexp=Dexp, nw=nw, ypre=ypre, yn=yn, st=st)
        s["h_mid"] = h
        u2 = _rmsnorm_fwd(h, P["ffn_norm"][i], "ffn_norm_fwd")
        Wl.update(fetch(i, ("f_w_up", "f_w_down"), h))
        up = _matmul(u2, Wl["f_w_up"], name="ffn_up_fwd").reshape(B, L, 2 * dff)
        act = _ffn_mid_fwd(up, P["f_conv_w"][i], P["f_conv_b"][i], dff)
        h = _matmul(act.reshape(T, dff), Wl["f_w_down"], res=h, name="ffn_down_fwd")
        s.update(u2=u2, up=up, act=act)
        saved.append(s)

    loss, dh, dhb, d_final = _loss_head(h, P["final_norm"], target.reshape(T, D))

    g = {k: [None] * P[k].shape[0] for k in ("mix_norm", "ffn_norm", "hgrn_gnorm", "m_conv_w", "m_conv_b", "m_dt_bias",
                                              "m_A_log", "m_D", "m_norm", "f_conv_w", "f_conv_b")}
    dlbs = [None] * lbs.shape[0]
    for i in reversed(range(depth)):
        j = i // 2
        s = saved[i]
        Wl = s["W"]
        gm = {}

        def dw(key, a, b, name, **kw):
            gm[key] = _matmul(a, b, ta=True, out_dtype=BF16, tk=T, name=name, **kw)

        dact = _matmul(dhb, Wl["f_w_down"], tb=True, name="ffn_down_dx").reshape(B, L, dff)
        dw("f_w_down", s["act"].reshape(T, dff), dhb, "ffn_down_dw")
        dup, dcw, dcb = _ffn_mid_bwd(s["up"], dact, P["f_conv_w"][i], P["f_conv_b"][i], dff)
        g["f_conv_w"][i] = jnp.sum(dcw, axis=0)
        g["f_conv_b"][i] = jnp.sum(dcb, axis=(0, 1))
        dup = dup.reshape(2, T, dff)
        dw("f_w_up", s["u2"], dup, "ffn_up_dw", b_parts=True)
        tok, finish = emit(i, {key: gm[key] for key in ("f_w_up", "f_w_down")})
        du2 = _matmul(dup, Wl["f_w_up"], a_parts=True, tb=True, name="ffn_up_dx", dep=tok)
        dh, dhb, g["ffn_norm"][i] = _rmsnorm_bwd(s["h_mid"], P["ffn_norm"][i], du2, dh, "ffn_norm_bwd", dep=finish(du2))
        if i % 2 == 0:
            don = _matmul(dhb, Wl["mix_out"], tb=True, name="hgrn_out_dx").reshape(B, L, F_)
            dw("mix_out", s["on"].reshape(T, F_), dhb, "hgrn_out_dw")
            dproj, dlb, dgn = _hgrn_bwd(s["proj"], s["o"], don, s["st"], lbs[j].reshape(1, F_),
                                        P["hgrn_gnorm"][j].reshape(1, HGRN_DK), H)
            dlbs[j] = jnp.sum(dlb, axis=(0, 1))
            g["hgrn_gnorm"][j] = jnp.sum(dgn, axis=(0, 1, 2))
            dproj = dproj.reshape(4, T, F_)
            dw("mix_in", s["u"], dproj, "hgrn_in_dw", b_parts=True)
            tok, finish = emit(i, {key: gm[key] for key in ("mix_in", "mix_out")})
            du = _matmul(dproj, Wl["mix_in"], a_parts=True, tb=True, name="hgrn_in_dx", dep=tok)
        else:
            dyn = _matmul(dhb, Wl["mix_out"], tb=True, name="m_out_dx").reshape(B, L, DI)
            dw("mix_out", s["yn"].reshape(T, DI), dhb, "m_out_dw")
            dxs, dbc, dz, ddt, dnw, dD, dA, dbias = _ssd_bwd(s["xbca"], s["zx"], s["dtc"], s["ypre"], dyn, s["st"],
                                                             s["bias"], s["Aneg"], s["Dexp"], s["nw"], hpg)
            g["m_norm"][j] = jnp.sum(dnw, axis=(0, 2)).reshape(DI)
            g["m_D"][j] = jnp.sum(dD, axis=(0, 2)).reshape(MH, M_HEADDIM).sum(axis=-1)
            g["m_A_log"][j] = jnp.sum(dA, axis=(0, 2))[:, :hpg].reshape(MH)
            g["m_dt_bias"][j] = jnp.sum(dbias, axis=(0, 2))[:, :hpg].reshape(MH)
            cw, cb = P["m_conv_w"][j], P["m_conv_b"][j]
            dxx, dcw_x, dcb_x = _mconv_bwd(s["zx"], dxs[None], cw, cb, DI, 0, "mconv_bwd_x")
            dxb, dcw_b, dcb_b = _mconv_bwd(s["zx"], dbc, cw, cb, DI, DI, "mconv_bwd_bc")
            g["m_conv_w"][j] = jnp.concatenate([jnp.sum(dcw_x, axis=0), jnp.sum(dcw_b, axis=0)], axis=-1)
            g["m_conv_b"][j] = jnp.concatenate([jnp.sum(dcb_x, axis=(0, 1)), jnp.sum(dcb_b, axis=(0, 1))], axis=-1)
            ddt_t = _pad_cols(ddt[..., :hpg].transpose(0, 2, 3, 1, 4).reshape(T, MH), MPAD - DI - CD).astype(BF16)
            pieces = [(dz.reshape(T, DI), 0), (dxx.reshape(T, DI), DI), (dxb.reshape(T, 2 * G * N), 2 * DI), (ddt_t, DI + CD)]
            gm["mix_in"] = lax.empty((MPAD, D), BF16)
            for n_, (piece, off) in enumerate(pieces):
                gm["mix_in"] = _matmul(piece, s["u"], ta=True, out_dtype=BF16, tk=T, out=gm["mix_in"], out_off=off,
                                       name="m_in_dw%d" % n_)
            tok, finish = emit(i, {key: gm[key] for key in ("mix_in", "mix_out")})
            du = None
            for n_, (piece, off) in enumerate(pieces):
                du = _matmul(piece, Wl["mix_in"], b_off=off, res=du, name="m_in_dx%d" % n_, dep=tok if n_ == 0 else None)
        dh, dhb, g["mix_norm"][i] = _rmsnorm_bwd(s["h_in"], P["mix_norm"][i], du, dh, "mix_norm_bwd", dep=finish(du))

    grads = {k: jnp.stack(vs) for k, vs in g.items()}
    grads["final_norm"] = d_final
    _, lb_vjp = jax.vjp(_lower_bounds, P["hgrn_lb_logits"])
    grads["hgrn_lb_logits"] = lb_vjp(jnp.stack(dlbs))[0]
    return loss, dh.reshape(B, L, D), grads


ANY = pl.BlockSpec(memory_space=pl.ANY)
N_CHIPS = 4
N_DEV = 8


def _place():
    x, y, c = lax.axis_index("x"), lax.axis_index("y"), lax.axis_index("c")
    sibling = (x, y, 1 - c)
    chips = [(1 - x, y), (x, 1 - y), (1 - x, 1 - y)]
    return x, y, c, sibling, chips


def _remote(src, dst, send_sem, recv_sem, to):
    return pltpu.make_async_remote_copy(src_ref=src, dst_ref=dst, send_sem=send_sem, recv_sem=recv_sem, device_id=to,
                                        device_id_type=MESH)


KIND_AXIS = {"hgrn_w_in": "col", "f_w_up": "col", "m_w_in_t": "row", "hgrn_w_out": "row", "m_w_out": "row", "f_w_down": "row"}
KINDS = tuple(KIND_AXIS)
PEER_MASKS = (2, 1, 3)
ALL = slice(None)


def _chip_win(axis, cw, s):
    return (ALL, slice(s * cw, (s + 1) * cw)) if axis == "col" else (slice(s * cw, (s + 1) * cw), ALL)


def _half_win(axis, rows, cols, h):
    return (slice(h * rows // 2, (h + 1) * rows // 2), ALL) if axis == "col" else (ALL, slice(h * cols // 2, (h + 1) * cols // 2))


def _per_place(fn):
    x, y, c, sibling, chips = _place()
    chip = 2 * x + y
    for s in range(N_CHIPS):
        for cc in range(2):
            @pl.when(jnp.logical_and(chip == s, c == cc))
            def _():
                fn(s, cc, c, sibling, chips)


HBM = pl.BlockSpec(memory_space=pltpu.HBM)
SEM = pl.BlockSpec(memory_space=pltpu.SEMAPHORE)
EFFECT = pltpu.SideEffectType.DATAFLOW_SIDE_EFFECTING


def _cell(axis, rows, cols, cw, s, h):
    if axis == "col":
        return (slice(h * rows // 2, (h + 1) * rows // 2), slice(s * cw, (s + 1) * cw))
    return (slice(s * cw, (s + 1) * cw), slice(h * cols // 2, (h + 1) * cols // 2))


def _in_hbm(a):
    return pltpu.with_memory_space_constraint(a, pltpu.HBM)


def _stage_shard(kind, shard, layer, chip, pad_rows=0, dep=None):
    _, R, C = shard.shape
    axis = KIND_AXIS[kind]
    tr, tc = _row_tile(R), _pick_tile(C, 2048)
    nr, nc = R // tr, C // tc
    full = (R, N_CHIPS * C) if axis == "col" else (N_CHIPS * R + pad_rows, C)

    def body(s_ref, x_ref, *rest):
        o_ref = rest[-1]
        o_ref[...] = x_ref[...].astype(BF16)

    if axis == "col":
        dst = pl.BlockSpec((tr, tc), lambda i, j, s_ref: (i, s_ref[0] * nc + j))
    else:
        dst = pl.BlockSpec((tr, tc), lambda i, j, s_ref: (s_ref[0] * nr + i, j))
    extra_specs, extra = ([], ()) if dep is None else ([ANY], (dep,))
    grid_spec = pltpu.PrefetchScalarGridSpec(
        num_scalar_prefetch=1, grid=(nr, nc),
        in_specs=[pl.BlockSpec((None, tr, tc), lambda i, j, s_ref: (layer, i, j))] + extra_specs, out_specs=dst)
    out = pl.pallas_call(
        body, name="stage_" + kind, grid_spec=grid_spec, out_shape=jax.ShapeDtypeStruct(full, BF16),
        compiler_params=_cparams("parallel", "parallel"))(chip.reshape(1).astype(jnp.int32), shard, *extra)
    if pad_rows:
        rows0 = N_CHIPS * R
        pr = math.gcd(rows0, pad_rows)

        def zero_body(x_ref, o_ref):
            o_ref[...] = jnp.zeros_like(o_ref)

        out = pl.pallas_call(
            zero_body, name="zero_pad_" + kind, grid=(pad_rows // pr,), in_specs=[ANY],
            out_specs=pl.BlockSpec((pr, C), lambda i: (rows0 // pr + i, 0)), out_shape=jax.ShapeDtypeStruct(full, BF16),
            input_output_aliases={0: 0}, compiler_params=_cparams("parallel"))(out)
    return out


def _gather_start(items, mats, cws, after, name):
    n = len(items)

    def body(*refs):
        send_sems, recv_sems, token = refs[n + 1], refs[n + 2], refs[-1]
        m = refs[n + 3:2 * n + 3]

        def run(s, cc, c, sibling, chips):
            for q, (k, _) in enumerate(items):
                r, c_ = m[q].shape
                mine = m[q].at[_cell(KIND_AXIS[k], r, c_, cws[k], s, cc)]
                for j, (px, py) in enumerate(chips):
                    _remote(mine, mine, send_sems.at[3 * q + j], recv_sems.at[3 * q + j], (px, py, c)).start()

        _per_place(run)
        token[...] = jnp.zeros_like(token)

    outs = pl.pallas_call(
        body, name=name, in_specs=[HBM] * n + [ANY],
        out_specs=[SEM, SEM] + [HBM] * n + [pl.BlockSpec(memory_space=pltpu.VMEM)],
        out_shape=[pltpu.SemaphoreType.DMA((3 * n,)), pltpu.SemaphoreType.DMA((3 * n,))]
        + [pltpu.HBM(a.shape, a.dtype) for a in mats] + [jax.ShapeDtypeStruct((8, LANES), F32)],
        input_output_aliases={q: 2 + q for q in range(n)},
        compiler_params=pltpu.CompilerParams(has_side_effects=EFFECT),
    )(*[_in_hbm(a) for a in mats], after)
    return outs[0], outs[1], list(outs[2:2 + n]), outs[-1]


def _gather_wait(items, idx, mats, send_sems, recv_sems, cws, after, name):
    n = len(idx)

    def body(*refs):
        m = refs[:n]
        s_sems, r_sems = refs[n], refs[n + 1]

        def run(s, cc, c, sibling, chips):
            for a, q in enumerate(idx):
                k = items[q][0]
                r, c_ = m[a].shape
                mine = m[a].at[_cell(KIND_AXIS[k], r, c_, cws[k], s, cc)]
                for j, (px, py) in enumerate(chips):
                    theirs = m[a].at[_cell(KIND_AXIS[k], r, c_, cws[k], s ^ PEER_MASKS[j], cc)]
                    cp = _remote(mine, theirs, s_sems.at[3 * q + j], r_sems.at[3 * q + j], (px, py, c))
                    cp.wait_send()
                    cp.wait_recv()

        _per_place(run)

    outs = pl.pallas_call(
        body, name=name, in_specs=[HBM] * n + [SEM, SEM, ANY], out_specs=[HBM] * n,
        out_shape=[pltpu.HBM(a.shape, a.dtype) for a in mats], input_output_aliases={a: a for a in range(n)},
        compiler_params=pltpu.CompilerParams(has_side_effects=EFFECT),
    )(*mats, send_sems, recv_sems, after)
    return list(outs)


def _forward_halves(kinds, mats, cws, name):
    n = len(mats)

    def body(*refs):
        m = refs[n:2 * n]
        send_sems, recv_sems = refs[2 * n:]

        def run(s, cc, c, sibling, chips):
            cps = []
            for a, k in enumerate(kinds):
                r, c_ = m[a].shape
                for j in range(3):
                    have = m[a].at[_cell(KIND_AXIS[k], r, c_, cws[k], s ^ PEER_MASKS[j], cc)]
                    cps.append(_remote(have, have, send_sems.at[3 * a + j], recv_sems.at[3 * a + j], sibling))
            for cp in cps:
                cp.start()
            for cp in cps:
                cp.wait()

        _per_place(run)

    outs = pl.pallas_call(
        body, name=name, in_specs=[ANY] * n, out_specs=[ANY] * n,
        out_shape=[jax.ShapeDtypeStruct(a.shape, a.dtype) for a in mats], input_output_aliases={a: a for a in range(n)},
        scratch_shapes=[pltpu.SemaphoreType.DMA((3 * n,)), pltpu.SemaphoreType.DMA((3 * n,))],
    )(*mats)
    return list(outs)


def _swap_start(kinds, gms, name):
    n = len(gms)
    lands = [lax.empty((g.shape[0] // 2, g.shape[1]) if KIND_AXIS[k] == "col" else (g.shape[0], g.shape[1] // 2), BF16)
             for k, g in zip(kinds, gms)]

    def body(*refs):
        send_sems, recv_sems, token = refs[2 * n], refs[2 * n + 1], refs[-1]
        g, ra = refs[2 * n + 2:3 * n + 2], refs[3 * n + 2:4 * n + 2]

        def run(s, cc, c, sibling, chips):
            for a, k in enumerate(kinds):
                r, c_ = g[a].shape
                _remote(g[a].at[_half_win(KIND_AXIS[k], r, c_, 1 - cc)], ra[a], send_sems.at[a], recv_sems.at[a],
                        sibling).start()

        _per_place(run)
        token[...] = jnp.zeros_like(token)

    outs = pl.pallas_call(
        body, name=name, in_specs=[HBM] * (2 * n),
        out_specs=[SEM, SEM] + [HBM] * (2 * n) + [pl.BlockSpec(memory_space=pltpu.VMEM)],
        out_shape=[pltpu.SemaphoreType.DMA((n,)), pltpu.SemaphoreType.DMA((n,))]
        + [pltpu.HBM(a.shape, a.dtype) for a in gms + lands] + [jax.ShapeDtypeStruct((8, LANES), F32)],
        input_output_aliases={q: 2 + q for q in range(2 * n)},
        compiler_params=pltpu.CompilerParams(has_side_effects=EFFECT),
    )(*[_in_hbm(a) for a in gms + lands])
    return outs[0], outs[1], list(outs[2:2 + n]), list(outs[2 + n:2 + 2 * n]), outs[-1]


def _swap_wait(kinds, gms, lands, send_sems, recv_sems, after, name):
    n = len(gms)

    def body(*refs):
        g, ra = refs[:n], refs[n:2 * n]
        s_sems, r_sems = refs[2 * n], refs[2 * n + 1]

        def run(s, cc, c, sibling, chips):
            for a, k in enumerate(kinds):
                r, c_ = g[a].shape
                cp = _remote(g[a].at[_half_win(KIND_AXIS[k], r, c_, 1 - cc)], ra[a], s_sems.at[a], r_sems.at[a], sibling)
                cp.wait_send()
                cp.wait_recv()

        _per_place(run)

    outs = pl.pallas_call(
        body, name=name, in_specs=[HBM] * (2 * n) + [SEM, SEM, ANY], out_specs=[HBM] * (2 * n),
        out_shape=[pltpu.HBM(a.shape, a.dtype) for a in gms + lands], input_output_aliases={a: a for a in range(2 * n)},
        compiler_params=pltpu.CompilerParams(has_side_effects=EFFECT),
    )(*gms, *lands, send_sems, recv_sems, after)
    return list(outs[:n]), list(outs[n:])


def _win_shape(kind, pa, cw):
    return (pa.shape[0], cw) if KIND_AXIS[kind] == "col" else (cw, pa.shape[1])


def _scatter_start(kinds, pas, cws, name):
    n = len(pas)
    lands = [lax.empty((3,) + _win_shape(k, p, cws[k]), BF16) for k, p in zip(kinds, pas)]

    def body(*refs):
        send_sems, recv_sems, token = refs[2 * n], refs[2 * n + 1], refs[-1]
        p, rb = refs[2 * n + 2:3 * n + 2], refs[3 * n + 2:4 * n + 2]

        def run(s, cc, c, sibling, chips):
            for a, k in enumerate(kinds):
                for j, (px, py) in enumerate(chips):
                    src = p[a].at[_chip_win(KIND_AXIS[k], cws[k], s ^ PEER_MASKS[j])]
                    _remote(src, rb[a].at[j], send_sems.at[3 * a + j], recv_sems.at[3 * a + j], (px, py, c)).start()

        _per_place(run)
        token[...] = jnp.zeros_like(token)

    outs = pl.pallas_call(
        body, name=name, in_specs=[HBM] * (2 * n),
        out_specs=[SEM, SEM] + [HBM] * (2 * n) + [pl.BlockSpec(memory_space=pltpu.VMEM)],
        out_shape=[pltpu.SemaphoreType.DMA((3 * n,)), pltpu.SemaphoreType.DMA((3 * n,))]
        + [pltpu.HBM(a.shape, a.dtype) for a in pas + lands] + [jax.ShapeDtypeStruct((8, LANES), F32)],
        input_output_aliases={q: 2 + q for q in range(2 * n)},
        compiler_params=pltpu.CompilerParams(has_side_effects=EFFECT),
    )(*[_in_hbm(a) for a in pas + lands])
    return outs[0], outs[1], list(outs[2:2 + n]), list(outs[2 + n:2 + 2 * n]), outs[-1]


def _scatter_wait(kinds, pas, lands, send_sems, recv_sems, cws, after, name):
    n = len(pas)

    def body(*refs):
        p, rb = refs[:n], refs[n:2 * n]
        s_sems, r_sems = refs[2 * n], refs[2 * n + 1]

        def run(s, cc, c, sibling, chips):
            for a, k in enumerate(kinds):
                for j, (px, py) in enumerate(chips):
                    src = p[a].at[_chip_win(KIND_AXIS[k], cws[k], s ^ PEER_MASKS[j])]
                    cp = _remote(src, rb[a].at[j], s_sems.at[3 * a + j], r_sems.at[3 * a + j], (px, py, c))
                    cp.wait_send()
                    cp.wait_recv()

        _per_place(run)

    outs = pl.pallas_call(
        body, name=name, in_specs=[HBM] * (2 * n) + [SEM, SEM, ANY], out_specs=[HBM] * (2 * n),
        out_shape=[pltpu.HBM(a.shape, a.dtype) for a in pas + lands], input_output_aliases={a: a for a in range(2 * n)},
        compiler_params=pltpu.CompilerParams(has_side_effects=EFFECT),
    )(*pas, *lands, send_sems, recv_sems, after)
    return list(outs[:n]), list(outs[n:])


def _share_halves(g):
    nq = len(KINDS)

    def body(*refs):
        out = dict(zip(KINDS, refs[nq:2 * nq]))
        send_sems, recv_sems = refs[2 * nq:]

        def run(s, cc, c, sibling, chips):
            cps = []
            for q, k in enumerate(KINDS):
                _, r, c_ = out[k].shape
                mine = out[k].at[(ALL,) + _half_win(KIND_AXIS[k], r, c_, cc)]
                cps.append(_remote(mine, mine, send_sems.at[q], recv_sems.at[q], sibling))
            for cp in cps:
                cp.start()
            for cp in cps:
                cp.wait()

        _per_place(run)

    outs = pl.pallas_call(
        body, name="share_halves", in_specs=[ANY] * nq, out_specs=[ANY] * nq,
        out_shape=[jax.ShapeDtypeStruct(g[k].shape, F32) for k in KINDS],
        input_output_aliases={q: q for q in range(nq)},
        scratch_shapes=[pltpu.SemaphoreType.DMA((nq,)), pltpu.SemaphoreType.DMA((nq,))],
    )(*[g[k] for k in KINDS])
    return dict(zip(KINDS, outs))


def _all_gather_small(xs, name):
    m_per, n = xs.shape

    def body(x_ref, out_ref, send_sems, recv_sems, local_sem):
        x, y, c, sibling, chips = _place()
        me = (x, y, c)

        def rows(px, py, pc):
            return out_ref.at[pl.ds((4 * px + 2 * py + pc) * m_per, m_per), :]

        def copy(k, block, to, src=None):
            return _remote(rows(*block) if src is None else src, rows(*block), send_sems.at[k], recv_sems.at[k], to)

        mine = pltpu.make_async_copy(x_ref, rows(*me), local_sem)
        mine.start()
        first = [copy(0, me, sibling, src=x_ref)]
        first += [copy(1 + j, me, (*chip, c), src=x_ref) for j, chip in enumerate(chips)]
        for cp in first:
            cp.start()
        passed = [copy(4 + j, (*chip, c), sibling) for j, chip in enumerate(chips)]
        for j, chip in enumerate(chips):
            copy(1 + j, (*chip, c), me).wait_recv()
            passed[j].start()
        copy(0, sibling, me).wait_recv()
        for j, chip in enumerate(chips):
            copy(4 + j, (*chip, 1 - c), me).wait_recv()
        for cp in first + passed:
            cp.wait_send()
        mine.wait()

    vm = pl.BlockSpec(memory_space=pltpu.VMEM)
    return pl.pallas_call(
        body, name=name, in_specs=[vm], out_specs=vm, out_shape=jax.ShapeDtypeStruct((N_DEV * m_per, n), xs.dtype),
        scratch_shapes=[pltpu.SemaphoreType.DMA((7,)), pltpu.SemaphoreType.DMA((7,)), pltpu.SemaphoreType.DMA],
        compiler_params=pltpu.CompilerParams(vmem_limit_bytes=VMEM_LIMIT_BYTES),
    )(xs)


def _row_tile(rows, cap=512):
    for mult in (16, 8):
        best = None
        t = mult
        while t <= min(rows, cap):
            if rows % t == 0:
                best = t
            t += mult
        if best is not None:
            return best
    raise ValueError(rows)


def _add_sibling(kind, g, ra, core):
    R, C = ra.shape
    axis = KIND_AXIS[kind]
    tr, tc = _row_tile(R), _pick_tile(C, 2048)
    nr, nc = R // tr, C // tc

    def body(c_ref, a_ref, b_ref, o_ref):
        o_ref[...] = (a_ref[...].astype(F32) + b_ref[...].astype(F32)).astype(o_ref.dtype)

    if axis == "col":
        own = pl.BlockSpec((tr, tc), lambda i, j, c_ref: (c_ref[0] * nr + i, j))
    else:
        own = pl.BlockSpec((tr, tc), lambda i, j, c_ref: (i, c_ref[0] * nc + j))
    same = pl.BlockSpec((tr, tc), lambda i, j, c_ref: (i, j))
    grid_spec = pltpu.PrefetchScalarGridSpec(num_scalar_prefetch=1, grid=(nr, nc), in_specs=[own, same], out_specs=same)
    return pl.pallas_call(
        body, name="add_sibling_" + kind, grid_spec=grid_spec, out_shape=jax.ShapeDtypeStruct(ra.shape, BF16),
        compiler_params=_cparams("parallel", "parallel"))(core.reshape(1).astype(jnp.int32), g, ra)


def _sum_chips(kind, pa, rb, chip, core, out, layer):
    _, R, C = rb.shape
    axis = KIND_AXIS[kind]
    tr, tc = _row_tile(R), _pick_tile(C, 2048)
    nr, nc = R // tr, C // tc

    def body(s_ref, c_ref, a_ref, b0_ref, b1_ref, b2_ref, old_ref, o_ref):
        o_ref[...] = ((a_ref[...].astype(F32) + b0_ref[...].astype(F32)) + b1_ref[...].astype(F32)) + b2_ref[...].astype(F32)

    def rb_spec(n):
        return pl.BlockSpec((None, tr, tc), lambda i, j, s_ref, c_ref: (n, i, j))

    if axis == "col":
        own = pl.BlockSpec((tr, tc), lambda i, j, s_ref, c_ref: (i, s_ref[0] * nc + j))
        dst = pl.BlockSpec((None, tr, tc), lambda i, j, s_ref, c_ref: (layer, c_ref[0] * nr + i, j))
        assert out.shape[1:] == (2 * R, C)
    else:
        own = pl.BlockSpec((tr, tc), lambda i, j, s_ref, c_ref: (s_ref[0] * nr + i, j))
        dst = pl.BlockSpec((None, tr, tc), lambda i, j, s_ref, c_ref: (layer, i, c_ref[0] * nc + j))
        assert out.shape[1:] == (R, 2 * C)
    grid_spec = pltpu.PrefetchScalarGridSpec(
        num_scalar_prefetch=2, grid=(nr, nc), in_specs=[own, rb_spec(0), rb_spec(1), rb_spec(2), ANY], out_specs=dst)
    return pl.pallas_call(
        body, name="sum_chips_" + kind, grid_spec=grid_spec, out_shape=jax.ShapeDtypeStruct(out.shape, F32),
        input_output_aliases={6: 0}, compiler_params=_cparams("parallel", "parallel"))(
            chip.reshape(1).astype(jnp.int32), core.reshape(1).astype(jnp.int32), pa, rb, rb, rb, out)


def _sum_devices(gathered):
    M = gathered.shape[0] // N_DEV
    C = gathered.shape[1]

    def body(g_ref, o_ref):
        acc = g_ref[0:M, :]
        for d in range(1, N_DEV):
            acc = acc + g_ref[d * M:(d + 1) * M, :]
        o_ref[...] = acc

    vm = pl.BlockSpec(memory_space=pltpu.VMEM)
    return pl.pallas_call(body, name="sum_devices", in_specs=[vm], out_specs=vm, out_shape=jax.ShapeDtypeStruct((M, C), F32),
                          compiler_params=pltpu.CompilerParams(vmem_limit_bytes=VMEM_LIMIT_BYTES))(gathered)


WEIGHTS = ["mix_norm", "ffn_norm", "final_norm", "hgrn_w_in", "hgrn_lb_logits", "hgrn_gnorm", "hgrn_w_out", "m_w_in",
           "m_conv_w", "m_conv_b", "m_dt_bias", "m_A_log", "m_D", "m_norm", "m_w_out", "f_w_up", "f_conv_w", "f_conv_b",
           "f_w_down"]
BIG_COLS = ("hgrn_w_in", "m_w_in", "f_w_up")
BIG_ROWS = ("hgrn_w_out", "m_w_out", "f_w_down")
BIG = BIG_COLS + BIG_ROWS
SMALL_SHARDED = ("m_conv_w", "m_conv_b", "m_norm", "f_conv_w")
SMALL_REPLICATED = ("mix_norm", "ffn_norm", "final_norm", "hgrn_lb_logits", "hgrn_gnorm", "m_dt_bias", "m_A_log", "m_D",
                    "f_conv_b")
SMALL = SMALL_REPLICATED + SMALL_SHARDED


def _pack_rows(arrs, row_mult=8):
    flat = jnp.concatenate([a.reshape(-1).astype(F32) for a in arrs])
    unit = FLAT_COLS * row_mult
    n = -(-flat.size // unit) * unit
    return jnp.pad(flat, (0, n - flat.size)).reshape(-1, FLAT_COLS)


def _unpack_rows(buf, shapes):
    flat = buf.reshape(-1)
    out, off = [], 0
    for shp in shapes:
        n = math.prod(shp)
        out.append(flat[off:off + n].reshape(shp))
        off += n
    return out


def kernel(x, mix_norm, ffn_norm, final_norm, hgrn_w_in, hgrn_lb_logits, hgrn_gnorm, hgrn_w_out, m_w_in, m_conv_w, m_conv_b, m_dt_bias, m_A_log, m_D, m_norm, m_w_out, f_w_up, f_conv_w, f_conv_b, f_w_down, loss_target, m_mix_norm, m_ffn_norm, m_final_norm, m_hgrn_w_in, m_hgrn_lb_logits, m_hgrn_gnorm, m_hgrn_w_out, m_m_w_in, m_m_conv_w, m_m_conv_b, m_m_dt_bias, m_m_A_log, m_m_D, m_m_norm, m_m_w_out, m_f_w_up, m_f_conv_w, m_f_conv_b, m_f_w_down, v_mix_norm, v_ffn_norm, v_final_norm, v_hgrn_w_in, v_hgrn_lb_logits, v_hgrn_gnorm, v_hgrn_w_out, v_m_w_in, v_m_conv_w, v_m_conv_b, v_m_dt_bias, v_m_A_log, v_m_D, v_m_norm, v_m_w_out, v_f_w_up, v_f_conv_w, v_f_conv_b, v_f_w_down):
    given = dict(locals())
    w = {n: given[n] for n in WEIGHTS}
    mom1 = {n: given["m_" + n] for n in WEIGHTS}
    mom2 = {n: given["v_" + n] for n in WEIGHTS}
    chip = 2 * lax.axis_index("x") + lax.axis_index("y")
    core = lax.axis_index("c")

    shards = {k: w[k] for k in KINDS if k != "m_w_in_t"}
    shards["m_w_in_t"] = w["m_w_in"].transpose(0, 2, 1).astype(BF16)
    m_in = N_CHIPS * w["m_w_in"].shape[2]
    pad_rows = {"m_w_in_t": -(-m_in // LANES) * LANES - m_in}
    cws = {k: shards[k].shape[2] if KIND_AXIS[k] == "col" else shards[k].shape[1] for k in KINDS}
    depth = w["mix_norm"].shape[0]

    def layer_kinds(i):
        mixer = {"mix_in": ("hgrn_w_in", i // 2), "mix_out": ("hgrn_w_out", i // 2)} if i % 2 == 0 else \
                {"mix_in": ("m_w_in_t", i // 2), "mix_out": ("m_w_out", i // 2)}
        return {**mixer, "f_w_up": ("f_w_up", i), "f_w_down": ("f_w_down", i)}

    own = _pack_rows([w[n] for n in SMALL_SHARDED])
    all_small = _all_gather_small(own, "gather_small_params")
    groups, started = [list(layer_kinds(0).values()), [it for i in range(1, depth) for it in layer_kinds(i).values()]], []
    after = all_small
    for n_, items in enumerate(groups):
        staged = [_stage_shard(k, shards[k], l, chip, pad_rows.get(k, 0), dep=None if n_ == 0 else after) for k, l in items]
        send_sems, recv_sems, mats, after = _gather_start(items, staged, cws, after, "gather_start_%d" % n_)
        started.append((items, send_sems, recv_sems, mats))
    all_small = all_small.reshape(N_CHIPS, 2, -1)[:, 0]
    per_chip = [_unpack_rows(all_small[s], [w[n].shape for n in SMALL_SHARDED]) for s in range(N_CHIPS)]
    P = {}
    for i, n in enumerate(SMALL_SHARDED):
        P[n] = jnp.concatenate([per_chip[s][i] for s in range(N_CHIPS)], axis=-1)
    for n in SMALL_REPLICATED:
        P[n] = w[n]

    def fetch(i, keys, h):
        lk = {key: layer_kinds(i)[key] for key in keys}
        items, send_sems, recv_sems, mats = started[0 if i == 0 else 1]
        idx = [items.index(it) for it in lk.values()]
        tag = "%d_%s" % (i, keys[0])
        if i == 0 and keys[0] == "mix_in":
            h = after
        got = _gather_wait(items, idx, [mats[q] for q in idx], send_sems, recv_sems, cws, h, "gather_wait_" + tag)
        got = _forward_halves([k for k, _ in lk.values()], got, cws, "forward_halves_" + tag)
        return dict(zip(lk.keys(), got))

    pending = []

    def emit(i, gm):
        lk = {key: layer_kinds(i)[key] for key in gm}
        kinds = [k for k, _ in lk.values()]
        gms = list(gm.values())
        tag = "%d_%s" % (i, next(iter(gm)))
        s1, r1, gms, half_lands, tok = _swap_start(kinds, gms, "swap_start_" + tag)

        def finish(after):
            gms2, ra = _swap_wait(kinds, gms, half_lands, s1, r1, after, "swap_wait_" + tag)
            pas = [_add_sibling(k, g_, r_, core) for k, g_, r_ in zip(kinds, gms2, ra)]
            s_sems, r_sems, pas, lands, tok2 = _scatter_start(kinds, pas, cws, "scatter_start_" + tag)
            pending.append((tag, list(lk.values()), pas, lands, s_sems, r_sems))
            return tok2

        return tok, finish

    loss_part, grad_x, g_full = _local_step(x, loss_target, P, fetch, emit)

    g_sh = {k: lax.empty(shards[k].shape, F32) for k in KINDS}
    for tag, its, pas, lands, s_sems, r_sems in pending:
        kinds = [k for k, _ in its]
        pas, lands = _scatter_wait(kinds, pas, lands, s_sems, r_sems, cws, grad_x, "scatter_wait_" + tag)
        for (k, l), p_, rb_ in zip(its, pas, lands):
            g_sh[k] = _sum_chips(k, p_, rb_, chip, core, g_sh[k], l)
    g_sh = _share_halves(g_sh)
    grads = {k: g_sh[k] for k in KINDS if k != "m_w_in_t"}
    grads["m_w_in"] = g_sh["m_w_in_t"].transpose(0, 2, 1)

    small_shapes = [g_full[n].shape for n in SMALL] + [(1,)]
    packed = _pack_rows([g_full[n] for n in SMALL] + [loss_part[0, 0:1]])
    summed = _sum_devices(_all_gather_small(packed, "gather_small_grads"))
    small = _unpack_rows(summed, small_shapes)
    loss = small[-1][0]
    for n, gs in zip(SMALL, small[:-1]):
        if n in SMALL_SHARDED:
            width = w[n].shape[-1]
            gs = lax.dynamic_slice_in_dim(gs, chip * width, width, axis=gs.ndim - 1)
        grads[n] = gs

    delta, new_m, new_v = {}, {}, {}
    for n in BIG:
        if n == "m_w_in":
            delta[n], new_m[n], new_v[n] = _adamw(w[n], grads[n], mom1[n], mom2[n], "adamw_" + n)
        else:
            delta[n], new_m[n], new_v[n], grads[n] = _adamw(w[n], grads[n], mom1[n], mom2[n], "adamw_" + n, echo=True)
    shapes = [w[n].shape for n in SMALL]
    ds, ms, vs = _adamw(_pack_rows([w[n] for n in SMALL]), _pack_rows([grads[n] for n in SMALL]),
                        _pack_rows([mom1[n] for n in SMALL]), _pack_rows([mom2[n] for n in SMALL]), "adamw_small")
    for n, d_, m_, v_ in zip(SMALL, _unpack_rows(ds, shapes), _unpack_rows(ms, shapes), _unpack_rows(vs, shapes)):
        delta[n], new_m[n], new_v[n] = d_, m_, v_

    return (loss, grad_x, *[grads[n] for n in WEIGHTS], *[delta[n] for n in WEIGHTS], *[new_m[n] for n in WEIGHTS],
            *[new_v[n] for n in WEIGHTS])
```
